```python
import jax, jax.numpy as jnp
from jax import lax
import numpy as np

D_MODEL = 1024
BATCH = 8
SEQ = 4096
DEPTH = 1

N_Q_HEADS = 8
N_KV_HEADS = 2
HEAD_DIM = 64
WINDOW = 128
GQA_GROUP = N_Q_HEADS // N_KV_HEADS
M_HEADS = 4
M_QK_DIM = 64
M_V_DIM = 128
CONV_WIDTH = 4
CHUNK = 64
N_GROUPS = 4
EXPERTS_PER_GROUP = 8
N_EXPERTS = N_GROUPS * EXPERTS_PER_GROUP
TOP_K = 2
D_EXPERT = 512
EXPERT_BLOCK = 256
EPS = 1e-6

ATT_Q_WIDTH = N_Q_HEADS * HEAD_DIM
ATT_KV_WIDTH = N_KV_HEADS * HEAD_DIM
M_QK_WIDTH = M_HEADS * M_QK_DIM
M_V_WIDTH = M_HEADS * M_V_DIM
IN_SIZES = (ATT_Q_WIDTH, ATT_KV_WIDTH, ATT_KV_WIDTH, M_QK_WIDTH, M_QK_WIDTH, M_V_WIDTH, M_V_WIDTH, M_HEADS, M_HEADS, D_MODEL, D_MODEL)
IN_WIDTH = sum(IN_SIZES)

kernel_name = 'hybrid_swa_mlstm_hmoe_block'


def rms_norm(x, w):
    xf = x.astype(jnp.float32)
    y = xf * lax.rsqrt(jnp.mean(xf * xf, axis=-1, keepdims=True) + EPS)
    return (y * w.astype(jnp.float32)).astype(x.dtype)


def causal_conv(u, w, b):
    c = u.shape[-1]
    y = lax.conv_general_dilated(u, w[:, None, :].astype(u.dtype), window_strides=(1,),
                                 padding=[(CONV_WIDTH - 1, 0)],
                                 dimension_numbers=('NWC', 'WIO', 'NWC'),
                                 feature_group_count=c)
    return y + b.astype(u.dtype)


def sliding_window_attention(q, k, v, sinks):
    B, S, _ = q.shape
    nb = S // WINDOW
    f32 = jnp.float32
    qb = q.astype(f32).reshape(B, nb, WINDOW, N_KV_HEADS, GQA_GROUP, HEAD_DIM)

    def band_keys(t):
        t = t.astype(f32).reshape(B, S, N_KV_HEADS, HEAD_DIM)
        t = jnp.concatenate([jnp.zeros((B, WINDOW, N_KV_HEADS, HEAD_DIM), f32), t], axis=1)
        t = t.reshape(B, nb + 1, WINDOW, N_KV_HEADS, HEAD_DIM)
        return jnp.concatenate([t[:, :-1], t[:, 1:]], axis=2)

    kw, vw = band_keys(k), band_keys(v)
    s = jnp.einsum('bnqkgd,bnskd->bnkgqs', qb, kw) * (HEAD_DIM ** -0.5)
    qi = jnp.arange(WINDOW)[:, None]
    kj = jnp.arange(2 * WINDOW)[None, :]
    band = (kj > qi) & (kj <= qi + WINDOW)
    valid = band[None] & ((jnp.arange(nb)[:, None, None] > 0) | (kj[None] >= WINDOW))
    s = jnp.where(valid[None, :, None, None], s, -jnp.inf)
    sink = jnp.broadcast_to(sinks.astype(f32).reshape(1, 1, N_KV_HEADS, GQA_GROUP, 1, 1), s.shape[:-1] + (1,))
    p = jax.nn.softmax(jnp.concatenate([s, sink], axis=-1), axis=-1)[..., :-1]
    o = jnp.einsum('bnkgqs,bnskd->bnqkgd', p, vw)
    return o.reshape(B, S, ATT_Q_WIDTH).astype(q.dtype)


def mlstm_chunkwise(q, k, v, ig, lf):
    B, S, H, DK = q.shape
    DV = v.shape[-1]
    nc = S // CHUNK

    def chunks(t):
        return t.reshape(B, nc, CHUNK, H, -1).transpose(1, 0, 3, 2, 4)

    def gate_chunks(t):
        return t.reshape(B, nc, CHUNK, H).transpose(1, 0, 3, 2)

    causal = jnp.tril(jnp.ones((CHUNK, CHUNK), dtype=bool))

    def step(carry, inp):
        C, n, m = carry
        qb, kb, vb, ib, fb = inp
        b = jnp.cumsum(fb, axis=-1)
        a = b + m[..., None]
        dmat = jnp.where(causal, b[..., :, None] - b[..., None, :] + ib[..., None, :], -jnp.inf)
        mt = jnp.maximum(a, jnp.max(dmat, axis=-1))
        wq = jnp.exp(dmat - mt[..., None])
        wa = jnp.exp(a - mt)
        sw = jnp.einsum('bhtd,bhsd->bhts', qb, kb) * wq
        num = wa[..., None] * jnp.einsum('bhtd,bhdv->bhtv', qb, C) + jnp.einsum('bhts,bhsv->bhtv', sw, vb)
        den = wa * jnp.einsum('bhtd,bhd->bht', qb, n) + jnp.sum(sw, axis=-1)
        hb = num / jnp.maximum(jnp.abs(den), jnp.exp(-mt))[..., None]
        m_new = mt[..., -1]
        wc = jnp.exp(b[..., -1] + m - m_new)
        ws = jnp.exp(b[..., -1:] - b + ib - m_new[..., None])
        C_new = wc[..., None, None] * C + jnp.einsum('bhs,bhsd,bhsv->bhdv', ws, kb, vb)
        n_new = wc[..., None] * n + jnp.einsum('bhs,bhsd->bhd', ws, kb)
        return (C_new, n_new, m_new), hb

    f32 = jnp.float32
    init = (jnp.zeros((B, H, DK, DV), f32), jnp.zeros((B, H, DK), f32), jnp.zeros((B, H), f32))
    _, hs = lax.scan(step, init, (chunks(q), chunks(k), chunks(v), gate_chunks(ig), gate_chunks(lf)))
    return hs.transpose(1, 0, 3, 2, 4).reshape(B, S, H, DV)


def hierarchical_moe(h, w_group, b_group, w_router, b_router, w_gate, w_up, w_down):
    B, S, D = h.shape
    T = B * S
    f32 = jnp.float32
    xt = h.reshape(T, D)
    gprob = jax.nn.softmax((xt @ w_group).astype(f32) + b_group.astype(f32), axis=-1)
    gp, gi = lax.top_k(gprob, 1)
    elog = ((xt @ w_router).astype(f32) + b_router.astype(f32)).reshape(T, N_GROUPS, EXPERTS_PER_GROUP)
    elog_g = jnp.take_along_axis(elog, gi[:, :, None], axis=1)[:, 0]
    ev, ej = lax.top_k(elog_g, TOP_K)
    ew = jax.nn.softmax(ev, axis=-1) * gp
    eid = gi * EXPERTS_PER_GROUP + ej

    A = T * TOP_K
    flat_e = eid.reshape(A)
    flat_tok = jnp.repeat(jnp.arange(T, dtype=jnp.int32), TOP_K)
    flat_w = ew.reshape(A)
    order = jnp.argsort(flat_e, stable=True)
    se, stok, sw = flat_e[order], flat_tok[order], flat_w[order]
    counts = jnp.bincount(flat_e, length=N_EXPERTS)
    starts = jnp.cumsum(counts) - counts
    pcounts = ((counts + EXPERT_BLOCK - 1) // EXPERT_BLOCK) * EXPERT_BLOCK
    pends = jnp.cumsum(pcounts)
    pstarts = pends - pcounts
    dest = pstarts[se] + jnp.arange(A, dtype=jnp.int32) - starts[se]
    nblk = (A + EXPERT_BLOCK - 1) // EXPERT_BLOCK + N_EXPERTS
    P = nblk * EXPERT_BLOCK
    slot_tok = jnp.full((P,), T, dtype=jnp.int32).at[dest].set(stok)
    slot_w = jnp.zeros((P,), f32).at[dest].set(sw)
    blk_e = jnp.minimum(jnp.searchsorted(pends, jnp.arange(nblk, dtype=jnp.int32) * EXPERT_BLOCK, side='right'), N_EXPERTS - 1)
    xpad = jnp.concatenate([xt, jnp.zeros((1, D), xt.dtype)], axis=0)
    xs = xpad[slot_tok].reshape(nblk, EXPERT_BLOCK, D)

    def expert_block(args):
        xb, e = args
        return (jax.nn.silu(xb @ w_gate[e]) * (xb @ w_up[e])) @ w_down[e]

    ys = lax.map(expert_block, (xs, blk_e)).reshape(P, D)
    y = jnp.zeros((T + 1, D), ys.dtype).at[slot_tok].add(ys * slot_w[:, None].astype(ys.dtype))[:T]
    return y.reshape(B, S, D)


def setup_inputs(seed: int = 0) -> dict:
    key = jax.random.key(seed)
    ks = jax.random.split(key, 24)
    f32 = jnp.float32

    def nrm(k, shape, scale):
        return jax.random.normal(k, shape, f32) * scale

    L = DEPTH
    return {
        'x': nrm(ks[0], (BATCH, SEQ, D_MODEL), 1.0),
        'norm_mix_w': 1.0 + nrm(ks[1], (L, D_MODEL), 0.02),
        'w_in': nrm(ks[2], (L, D_MODEL, IN_WIDTH), D_MODEL ** -0.5),
        'conv_w': nrm(ks[3], (L, CONV_WIDTH, 2 * M_QK_WIDTH), CONV_WIDTH ** -0.5),
        'conv_b': nrm(ks[4], (L, 2 * M_QK_WIDTH), 0.02),
        'b_igate': nrm(ks[5], (L, M_HEADS), 0.1),
        'b_fgate': jnp.linspace(3.0, 6.0, M_HEADS, dtype=f32)[None] + nrm(ks[6], (L, M_HEADS), 0.1),
        'attn_sinks': nrm(ks[7], (L, N_Q_HEADS), 0.5),
        'mlstm_norm_w': 1.0 + nrm(ks[8], (L, M_V_WIDTH), 0.02),
        'w_attn_o': nrm(ks[9], (L, ATT_Q_WIDTH, D_MODEL), ATT_Q_WIDTH ** -0.5),
        'w_mlstm_o': nrm(ks[10], (L, M_V_WIDTH, D_MODEL), M_V_WIDTH ** -0.5),
        'w_out': nrm(ks[11], (L, D_MODEL, D_MODEL), D_MODEL ** -0.5),
        'norm_ffn_w': 1.0 + nrm(ks[12], (L, D_MODEL), 0.02),
        'w_group': nrm(ks[13], (L, D_MODEL, N_GROUPS), D_MODEL ** -0.5),
        'b_group': nrm(ks[14], (L, N_GROUPS), 0.01),
        'w_router': nrm(ks[15], (L, D_MODEL, N_EXPERTS), D_MODEL ** -0.5),
        'b_router': nrm(ks[16], (L, N_EXPERTS), 0.01),
        'w_gate': nrm(ks[17], (L, N_EXPERTS, D_MODEL, D_EXPERT), D_MODEL ** -0.5),
        'w_up': nrm(ks[18], (L, N_EXPERTS, D_MODEL, D_EXPERT), D_MODEL ** -0.5),
        'w_down': nrm(ks[19], (L, N_EXPERTS, D_EXPERT, D_MODEL), D_EXPERT ** -0.5),
        'norm_final_w': 1.0 + nrm(ks[20], (D_MODEL,), 0.02),
    }


def reference(x, norm_mix_w, w_in, conv_w, conv_b, b_igate, b_fgate, attn_sinks, mlstm_norm_w,
              w_attn_o, w_mlstm_o, w_out, norm_ffn_w, w_group, b_group, w_router, b_router,
              w_gate, w_up, w_down, norm_final_w):
    B, S, _ = x.shape
    f32 = jnp.float32
    splits = np.cumsum(np.array(IN_SIZES))[:-1].tolist()
    for l in range(DEPTH):
        h = rms_norm(x, norm_mix_w[l])
        proj = h @ w_in[l]
        aq, ak, av, mq, mk, mv, mo, mi, mf, ga, gb = jnp.split(proj, splits, axis=-1)

        ya = sliding_window_attention(aq, ak, av, attn_sinks[l])

        qk = jax.nn.silu(causal_conv(jnp.concatenate([mq, mk], axis=-1), conv_w[l], conv_b[l]))
        mq, mk = jnp.split(qk, 2, axis=-1)
        q_m = mq.astype(f32).reshape(B, S, M_HEADS, M_QK_DIM)
        k_m = mk.astype(f32).reshape(B, S, M_HEADS, M_QK_DIM) * (M_QK_DIM ** -0.5)
        v_m = mv.astype(f32).reshape(B, S, M_HEADS, M_V_DIM)
        ig = mi.astype(f32) + b_igate[l].astype(f32)
        lf = jax.nn.log_sigmoid(mf.astype(f32) + b_fgate[l].astype(f32))
        hm = mlstm_chunkwise(q_m, k_m, v_m, ig, lf)
        hm = hm * lax.rsqrt(jnp.mean(hm * hm, axis=-1, keepdims=True) + EPS)
        hm = hm.reshape(B, S, M_V_WIDTH) * mlstm_norm_w[l].astype(f32)
        ym = (jax.nn.sigmoid(mo.astype(f32)) * hm).astype(x.dtype)

        mix = jax.nn.sigmoid(ga) * (ya @ w_attn_o[l]) + jax.nn.sigmoid(gb) * (ym @ w_mlstm_o[l])
        x = x + mix @ w_out[l]

        h = rms_norm(x, norm_ffn_w[l])
        x = x + hierarchical_moe(h, w_group[l], b_group[l], w_router[l], b_router[l],
                                 w_gate[l], w_up[l], w_down[l])
    return rms_norm(x, norm_final_w)
```

```python
import functools

import jax
import jax.numpy as jnp
from jax import lax
from jax.experimental import pallas as pl
from jax.experimental.pallas import tpu as pltpu

F32 = jnp.float32
BF16 = jnp.bfloat16

D_MODEL = 1024
N_Q_HEADS = 8
N_KV_HEADS = 2
HEAD_DIM = 64
WINDOW = 128
GQA_GROUP = N_Q_HEADS // N_KV_HEADS
M_HEADS = 4
M_QK_DIM = 64
M_V_DIM = 128
CONV_WIDTH = 4
N_GROUPS = 4
EXPERTS_PER_GROUP = 8
N_EXPERTS = N_GROUPS * EXPERTS_PER_GROUP
D_EXPERT = 512
EPS = 1e-6

ATT_Q_WIDTH = N_Q_HEADS * HEAD_DIM
ATT_KV_WIDTH = N_KV_HEADS * HEAD_DIM
M_QK_WIDTH = M_HEADS * M_QK_DIM
M_V_WIDTH = M_HEADS * M_V_DIM

LANES = 128
NEG = -1e30
VMEM_LIMIT = 56 * 1024 * 1024

TOK_TILE = 512
ATT_TILE = 256
M_CHUNK = 128
SLOT_BLOCK = 256
COMB_TILE = 256

_O_AQ = 0
_O_AK = _O_AQ + ATT_Q_WIDTH
_O_AV = _O_AK + ATT_KV_WIDTH
_O_MQ = _O_AV + ATT_KV_WIDTH
_O_MK = _O_MQ + M_QK_WIDTH
_O_MV = _O_MK + M_QK_WIDTH
_O_MO = _O_MV + M_V_WIDTH
_O_MI = _O_MO + M_V_WIDTH
_O_MF = _O_MI + M_HEADS
_O_GA = _O_MF + M_HEADS
_O_GB = _O_GA + D_MODEL
_O_END = _O_GB + D_MODEL


def _rms(x, w):
    return x * lax.rsqrt(jnp.mean(x * x, axis=-1, keepdims=True) + EPS) * w


def _sigmoid(x):
    return 1.0 / (1.0 + jnp.exp(-x))


def _log_sigmoid(x):
    return jnp.minimum(x, 0.0) - jnp.log1p(jnp.exp(-jnp.abs(x)))


def _inproj_kernel(x_ref, nw_ref, w_ref, wg_ref,
                   q_ref, kv_ref, mqk_ref, mv_ref, mo_ref, gcol_ref, grow_ref, gab_ref):
    h = _rms(x_ref[...], nw_ref[...]).astype(BF16)

    def proj(lo, hi):
        return jnp.dot(h, w_ref[:, lo:hi], preferred_element_type=F32)

    q_ref[...] = proj(_O_AQ, _O_AK).astype(BF16)
    kv_ref[...] = proj(_O_AK, _O_MQ).astype(BF16)
    mqk_ref[...] = proj(_O_MQ, _O_MV).astype(BF16)
    mv_ref[...] = proj(_O_MV, _O_MO).astype(BF16)
    mo_ref[...] = proj(_O_MO, _O_MI).astype(BF16)
    gab_ref[...] = proj(_O_MI, _O_MI + 2 * D_MODEL).astype(BF16)
    g = jnp.dot(h, wg_ref[...], preferred_element_type=F32)
    gcol_ref[...] = g
    grow_ref[...] = g.T[0:8, :]


def _inproj(x2, norm_w, w_main, w_gates):
    T = x2.shape[0]
    tm = TOK_TILE
    nmain = w_main.shape[1]
    row = lambda i: (i, 0)
    const = lambda i: (0, 0)
    return pl.pallas_call(
        _inproj_kernel,
        grid=(T // tm,),
        in_specs=[pl.BlockSpec((tm, D_MODEL), row),
                  pl.BlockSpec((1, D_MODEL), const),
                  pl.BlockSpec((D_MODEL, nmain), const),
                  pl.BlockSpec((D_MODEL, LANES), const)],
        out_specs=[pl.BlockSpec((tm, ATT_Q_WIDTH), row),
                   pl.BlockSpec((tm, 2 * ATT_KV_WIDTH), row),
                   pl.BlockSpec((tm, 2 * M_QK_WIDTH), row),
                   pl.BlockSpec((tm, M_V_WIDTH), row),
                   pl.BlockSpec((tm, M_V_WIDTH), row),
                   pl.BlockSpec((tm, LANES), row),
                   pl.BlockSpec((8, tm), lambda i: (0, i)),
                   pl.BlockSpec((tm, 2 * D_MODEL), row)],
        out_shape=[jax.ShapeDtypeStruct((T, ATT_Q_WIDTH), BF16),
                   jax.ShapeDtypeStruct((T, 2 * ATT_KV_WIDTH), BF16),
                   jax.ShapeDtypeStruct((T, 2 * M_QK_WIDTH), BF16),
                   jax.ShapeDtypeStruct((T, M_V_WIDTH), BF16),
                   jax.ShapeDtypeStruct((T, M_V_WIDTH), BF16),
                   jax.ShapeDtypeStruct((T, LANES), F32),
                   jax.ShapeDtypeStruct((8, T), F32),
                   jax.ShapeDtypeStruct((T, 2 * D_MODEL), BF16)],
        compiler_params=pltpu.CompilerParams(dimension_semantics=("arbitrary",),
                                             vmem_limit_bytes=VMEM_LIMIT),
        name="inproj",
    )(x2, norm_w, w_main, w_gates)


def _attn_kernel(sink_ref, q_ref, kv_ref, kvp_ref, o_ref):
    i = pl.program_id(1)
    W = WINDOW
    r = lax.broadcasted_iota(jnp.int32, (W, 2 * W), 0)
    c = lax.broadcasted_iota(jnp.int32, (W, 2 * W), 1)
    band = (c > r) & (c <= r + W)
    for j in range(ATT_TILE // W):
        qj = q_ref[0, j * W:(j + 1) * W, :]
        if j == 0:
            kprev = kvp_ref[0]
            valid = band & ((i > 0) | (c >= W))
        else:
            kprev = kv_ref[0, (j - 1) * W:j * W, :]
            valid = band
        kwin = jnp.concatenate([kprev, kv_ref[0, j * W:(j + 1) * W, :]], axis=0)
        outs = []
        for h in range(N_Q_HEADS):
            g = h // GQA_GROUP
            qh = qj[:, h * HEAD_DIM:(h + 1) * HEAD_DIM]
            kg = kwin[:, g * HEAD_DIM:(g + 1) * HEAD_DIM]
            vg = kwin[:, ATT_KV_WIDTH + g * HEAD_DIM:ATT_KV_WIDTH + (g + 1) * HEAD_DIM]
            s = lax.dot_general(qh, kg, (((1,), (1,)), ((), ())),
                                preferred_element_type=F32) * (HEAD_DIM ** -0.5)
            s = jnp.where(valid, s, NEG)
            sink = sink_ref[h]
            m = jnp.maximum(jnp.max(s, axis=-1, keepdims=True), sink)
            p = jnp.exp(s - m)
            l = jnp.sum(p, axis=-1, keepdims=True) + jnp.exp(sink - m)
            o = jnp.dot(p.astype(BF16), vg, preferred_element_type=F32)
            outs.append(o / l)
        o_ref[0, j * W:(j + 1) * W, :] = jnp.concatenate(outs, axis=-1).astype(BF16)


def _attn(sinks, q3, kv3):
    B, S, _ = q3.shape
    tq = ATT_TILE
    per = tq // WINDOW
    return pl.pallas_call(
        _attn_kernel,
        grid=(B, S // tq),
        in_specs=[pl.BlockSpec(memory_space=pltpu.SMEM),
                  pl.BlockSpec((1, tq, ATT_Q_WIDTH), lambda b, i: (b, i, 0)),
                  pl.BlockSpec((1, tq, 2 * ATT_KV_WIDTH), lambda b, i: (b, i, 0)),
                  pl.BlockSpec((1, WINDOW, 2 * ATT_KV_WIDTH),
                               lambda b, i: (b, jnp.maximum(i * per - 1, 0), 0))],
        out_specs=pl.BlockSpec((1, tq, ATT_Q_WIDTH), lambda b, i: (b, i, 0)),
        out_shape=jax.ShapeDtypeStruct((B, S, ATT_Q_WIDTH), BF16),
        compiler_params=pltpu.CompilerParams(dimension_semantics=("arbitrary", "arbitrary"),
                                             vmem_limit_bytes=VMEM_LIMIT),
        name="attn",
    )(sinks, q3, kv3, kv3)


def _mlstm_kernel(mqk_ref, mv_ref, mo_ref, gcol_ref, grow_ref, cw_ref, cb_ref, bcol_ref, brow_ref,
                  nw_ref, y_ref, cext_ref, m_ref, ubuf_ref):
    L = M_CHUNK
    cidx = pl.program_id(1)

    @pl.when(cidx == 0)
    def _():
        cext_ref[...] = jnp.zeros_like(cext_ref)
        m_ref[...] = jnp.zeros_like(m_ref)
        ubuf_ref[0:8, :] = jnp.zeros((8, 2 * M_QK_WIDTH), F32)

    ubuf_ref[8:L + 8, :] = mqk_ref[0].astype(F32)
    acc = cb_ref[...] + cw_ref[CONV_WIDTH - 1:CONV_WIDTH, :] * ubuf_ref[8:L + 8, :]
    for j in range(CONV_WIDTH - 1):
        off = 8 - (CONV_WIDTH - 1) + j
        acc = acc + cw_ref[j:j + 1, :] * ubuf_ref[off:off + L, :]
    ubuf_ref[0:8, :] = ubuf_ref[L:L + 8, :]
    qk = acc * _sigmoid(acc)
    q = qk[:, :M_QK_WIDTH].astype(BF16)
    k = qk[:, M_QK_WIDTH:] * (M_QK_DIM ** -0.5)

    gc = gcol_ref[0] + bcol_ref[...]
    gr = grow_ref[...] + brow_ref[...]
    ti = lax.broadcasted_iota(jnp.int32, (L, L), 0)
    si = lax.broadcasted_iota(jnp.int32, (L, L), 1)
    causal = si <= ti
    tril = jnp.where(causal, 1.0, 0.0).astype(F32)
    triu = jnp.where(ti <= si, 1.0, 0.0).astype(F32)
    b_c = jnp.dot(tril, _log_sigmoid(gc), preferred_element_type=F32,
                  precision=lax.Precision.HIGHEST)
    b_r = jnp.dot(_log_sigmoid(gr), triu, preferred_element_type=F32,
                  precision=lax.Precision.HIGHEST)

    ext = jnp.where(lax.broadcasted_iota(jnp.int32, (L, LANES), 1) == 0, 1.0, 0.0).astype(BF16)
    outs = []
    for h in range(M_HEADS):
        bc = b_c[:, M_HEADS + h:M_HEADS + h + 1]
        br = b_r[M_HEADS + h:M_HEADS + h + 1, :]
        ir = gr[h:h + 1, :]
        ic = gc[:, h:h + 1]
        m_prev = m_ref[h:h + 1, 0:1]
        dmat = jnp.where(causal, bc - br + ir, NEG)
        a = bc + m_prev
        mt = jnp.maximum(a, jnp.max(dmat, axis=-1, keepdims=True))
        wq = jnp.exp(dmat - mt)
        wa = jnp.exp(a - mt)
        qh = q[:, h * M_QK_DIM:(h + 1) * M_QK_DIM]
        kh = k[:, h * M_QK_DIM:(h + 1) * M_QK_DIM]
        sw = lax.dot_general(qh, kh.astype(BF16), (((1,), (1,)), ((), ())),
                             preferred_element_type=F32) * wq
        vext = jnp.concatenate([mv_ref[0, :, h * M_V_DIM:(h + 1) * M_V_DIM], ext], axis=-1)
        cprev = cext_ref[h]
        res = wa * jnp.dot(qh, cprev.astype(BF16), preferred_element_type=F32) \
            + jnp.dot(sw.astype(BF16), vext, preferred_element_type=F32)
        num = res[:, :M_V_DIM]
        den = res[:, M_V_DIM:M_V_DIM + 1]
        hb = num / jnp.maximum(jnp.abs(den), jnp.exp(-mt))
        m_new = mt[L - 1:L, :]
        bl = bc[L - 1:L, :]
        wc = jnp.exp(bl + m_prev - m_new)
        ws = jnp.exp(bl - bc + ic - m_new)
        kw = (kh * ws).astype(BF16)
        cext_ref[h] = wc * cprev + lax.dot_general(kw, vext, (((0,), (0,)), ((), ())),
                                                   preferred_element_type=F32)
        m_ref[h:h + 1, :] = jnp.broadcast_to(m_new, (1, LANES))
        hn = hb * lax.rsqrt(jnp.mean(hb * hb, axis=-1, keepdims=True) + EPS)
        hn = hn * nw_ref[:, h * M_V_DIM:(h + 1) * M_V_DIM]
        og = _sigmoid(mo_ref[0, :, h * M_V_DIM:(h + 1) * M_V_DIM].astype(F32))
        outs.append(og * hn)
    y_ref[0] = jnp.concatenate(outs, axis=-1).astype(BF16)


def _mlstm(mqk3, mv3, mo3, gcol3, grow, conv_w, conv_b, bias_col, bias_row, norm_w):
    B, S, _ = mqk3.shape
    L = M_CHUNK
    nc = S // L
    tok = lambda b, c: (b, c, 0)
    const = lambda b, c: (0, 0)
    return pl.pallas_call(
        _mlstm_kernel,
        grid=(B, nc),
        in_specs=[pl.BlockSpec((1, L, 2 * M_QK_WIDTH), tok),
                  pl.BlockSpec((1, L, M_V_WIDTH), tok),
                  pl.BlockSpec((1, L, M_V_WIDTH), tok),
                  pl.BlockSpec((1, L, LANES), tok),
                  pl.BlockSpec((8, L), lambda b, c: (0, b * nc + c)),
                  pl.BlockSpec((CONV_WIDTH, 2 * M_QK_WIDTH), const),
                  pl.BlockSpec((1, 2 * M_QK_WIDTH), const),
                  pl.BlockSpec((1, LANES), const),
                  pl.BlockSpec((8, 1), const),
                  pl.BlockSpec((1, M_V_WIDTH), const)],
        out_specs=pl.BlockSpec((1, L, M_V_WIDTH), tok),
        out_shape=jax.ShapeDtypeStruct((B, S, M_V_WIDTH), BF16),
        scratch_shapes=[pltpu.VMEM((M_HEADS, M_QK_DIM, 2 * M_V_DIM), F32),
                        pltpu.VMEM((8, LANES), F32),
                        pltpu.VMEM((L + 8, 2 * M_QK_WIDTH), F32)],
        compiler_params=pltpu.CompilerParams(dimension_semantics=("arbitrary", "arbitrary"),
                                             vmem_limit_bytes=VMEM_LIMIT),
        name="mlstm",
    )(mqk3, mv3, mo3, gcol3, grow, conv_w, conv_b, bias_col, bias_row, norm_w)


def _outproj_kernel(x_ref, ya_ref, ym_ref, gab_ref, wa_ref, wm_ref, wo_ref, nw_ref, wr_ref, br_ref,
                    x1_ref, h2_ref, meta_ref, cnt_ref, cnt_scr):
    tm = TOK_TILE
    i = pl.program_id(0)

    @pl.when(i == 0)
    def _():
        cnt_scr[...] = jnp.zeros_like(cnt_scr)

    pa = jnp.dot(ya_ref[...], wa_ref[...], preferred_element_type=F32)
    pm = jnp.dot(ym_ref[...], wm_ref[...], preferred_element_type=F32)
    ga = gab_ref[:, :D_MODEL].astype(F32)
    gb = gab_ref[:, D_MODEL:].astype(F32)
    mix = _sigmoid(ga) * pa + _sigmoid(gb) * pm
    x1 = x_ref[...] + jnp.dot(mix.astype(BF16), wo_ref[...], preferred_element_type=F32)
    x1_ref[...] = x1
    h2 = _rms(x1, nw_ref[...])
    h2_ref[...] = h2

    logits = jnp.dot(h2, wr_ref[...], preferred_element_type=F32,
                     precision=lax.Precision.HIGHEST) + br_ref[...]
    lane = lax.broadcasted_iota(jnp.int32, (tm, LANES), 1)
    big = jnp.int32(LANES)

    def first_argmax(v):
        mx = jnp.max(v, axis=-1, keepdims=True)
        idx = jnp.min(jnp.where(v == mx, lane, big), axis=-1, keepdims=True)
        return mx, idx

    gl = jnp.where(lane < N_GROUPS, logits, NEG)
    gmax, gi = first_argmax(gl)
    gp = 1.0 / jnp.sum(jnp.exp(gl - gmax), axis=-1, keepdims=True)
    lo = N_GROUPS + gi * EXPERTS_PER_GROUP
    el = jnp.where((lane >= lo) & (lane < lo + EXPERTS_PER_GROUP), logits, NEG)
    v1, j1 = first_argmax(el)
    v2, j2 = first_argmax(jnp.where(lane == j1, NEG, el))
    t = jnp.exp(v2 - v1)
    w1 = gp / (1.0 + t)
    w2 = gp * t / (1.0 + t)

    hit1 = lane == j1
    hit2 = lane == j2
    onehot = jnp.where(hit1 | hit2, 1.0, 0.0)
    ri = lax.broadcasted_iota(jnp.int32, (tm, tm), 0)
    ci = lax.broadcasted_iota(jnp.int32, (tm, tm), 1)
    before = jnp.where(ci < ri, 1.0, 0.0).astype(BF16)
    rank = jnp.dot(before, onehot.astype(BF16), preferred_element_type=F32) + cnt_scr[0:1, :]
    r1 = jnp.sum(jnp.where(hit1, rank, 0.0), axis=-1, keepdims=True)
    r2 = jnp.sum(jnp.where(hit2, rank, 0.0), axis=-1, keepdims=True)
    cnt = cnt_scr[0:1, :] + jnp.sum(onehot, axis=0, keepdims=True)
    cnt_scr[...] = jnp.broadcast_to(cnt, cnt_scr.shape)
    cnt_ref[...] = jnp.broadcast_to(cnt, cnt_ref.shape)

    e1 = (j1 - N_GROUPS).astype(F32)
    e2 = (j2 - N_GROUPS).astype(F32)
    meta = jnp.where(lane == 0, e1, 0.0)
    meta = jnp.where(lane == 1, e2, meta)
    meta = jnp.where(lane == 2, r1, meta)
    meta = jnp.where(lane == 3, r2, meta)
    meta = jnp.where(lane == 4, w1, meta)
    meta = jnp.where(lane == 5, w2, meta)
    meta_ref[...] = meta


def _outproj(x2, ya, ym, gab, wa, wm, wo, norm_w, w_rt, b_rt):
    T = x2.shape[0]
    tm = TOK_TILE
    row = lambda i: (i, 0)
    const = lambda i: (0, 0)
    return pl.pallas_call(
        _outproj_kernel,
        grid=(T // tm,),
        in_specs=[pl.BlockSpec((tm, D_MODEL), row),
                  pl.BlockSpec((tm, ATT_Q_WIDTH), row),
                  pl.BlockSpec((tm, M_V_WIDTH), row),
                  pl.BlockSpec((tm, 2 * D_MODEL), row),
                  pl.BlockSpec((ATT_Q_WIDTH, D_MODEL), const),
                  pl.BlockSpec((M_V_WIDTH, D_MODEL), const),
                  pl.BlockSpec((D_MODEL, D_MODEL), const),
                  pl.BlockSpec((1, D_MODEL), const),
                  pl.BlockSpec((D_MODEL, LANES), const),
                  pl.BlockSpec((1, LANES), const)],
        out_specs=[pl.BlockSpec((tm, D_MODEL), row),
                   pl.BlockSpec((tm, D_MODEL), row),
                   pl.BlockSpec((tm, LANES), row),
                   pl.BlockSpec((8, LANES), const)],
        out_shape=[jax.ShapeDtypeStruct((T, D_MODEL), F32),
                   jax.ShapeDtypeStruct((T, D_MODEL), F32),
                   jax.ShapeDtypeStruct((T, LANES), F32),
                   jax.ShapeDtypeStruct((8, LANES), F32)],
        scratch_shapes=[pltpu.VMEM((8, LANES), F32)],
        compiler_params=pltpu.CompilerParams(dimension_semantics=("arbitrary",),
                                             vmem_limit_bytes=VMEM_LIMIT),
        name="outproj",
    )(x2, ya, ym, gab, wa, wm, wo, norm_w, w_rt, b_rt)


def _experts_kernel(blk_e_ref, nused_ref, tok_ref, h2_hbm, wg_ref, wu_ref, wd_ref, ys_ref,
                    xbuf, wg_bf, wu_bf, wd_bf, sem):
    j = pl.program_id(0)
    R = SLOT_BLOCK

    @pl.when(j < nused_ref[0])
    def _():
        def issue(r, carry):
            tok = tok_ref[j * R + r]
            pltpu.make_async_copy(h2_hbm.at[pl.ds(tok, 1)], xbuf.at[pl.ds(r, 1)], sem).start()
            return carry
        lax.fori_loop(0, R, issue, 0, unroll=8)

        changed = jnp.logical_or(j == 0, blk_e_ref[j] != blk_e_ref[jnp.maximum(j - 1, 0)])

        @pl.when(changed)
        def _():
            wg_bf[...] = wg_ref[0].astype(BF16)
            wu_bf[...] = wu_ref[0].astype(BF16)
            wd_bf[...] = wd_ref[0].astype(BF16)

        pltpu.make_async_copy(h2_hbm.at[pl.ds(0, R)], xbuf, sem).wait()
        xb = xbuf[...].astype(BF16)
        g = jnp.dot(xb, wg_bf[...], preferred_element_type=F32)
        u = jnp.dot(xb, wu_bf[...], preferred_element_type=F32)
        act = (g * _sigmoid(g) * u).astype(BF16)
        ys_ref[...] = jnp.dot(act, wd_bf[...], preferred_element_type=F32)

    @pl.when(j >= nused_ref[0])
    def _():
        ys_ref[...] = jnp.zeros_like(ys_ref)


def _experts(blk_e, nused, slot_tok, h2, w_gate, w_up, w_down):
    nblk = blk_e.shape[0]
    R = SLOT_BLOCK
    wspec = lambda shape: pl.BlockSpec((1,) + shape, lambda j, be, nu, st: (be[j], 0, 0))
    grid_spec = pltpu.PrefetchScalarGridSpec(
        num_scalar_prefetch=3,
        grid=(nblk,),
        in_specs=[pl.BlockSpec(memory_space=pl.ANY),
                  wspec((D_MODEL, D_EXPERT)),
                  wspec((D_MODEL, D_EXPERT)),
                  wspec((D_EXPERT, D_MODEL))],
        out_specs=pl.BlockSpec((R, D_MODEL), lambda j, be, nu, st: (j, 0)),
        scratch_shapes=[pltpu.VMEM((R, D_MODEL), F32),
                        pltpu.VMEM((D_MODEL, D_EXPERT), BF16),
                        pltpu.VMEM((D_MODEL, D_EXPERT), BF16),
                        pltpu.VMEM((D_EXPERT, D_MODEL), BF16),
                        pltpu.SemaphoreType.DMA],
    )
    return pl.pallas_call(
        _experts_kernel,
        grid_spec=grid_spec,
        out_shape=jax.ShapeDtypeStruct((nblk * R, D_MODEL), F32),
        compiler_params=pltpu.CompilerParams(dimension_semantics=("arbitrary",),
                                             vmem_limit_bytes=VMEM_LIMIT),
        name="experts",
    )(blk_e, nused, slot_tok, h2, w_gate, w_up, w_down)


def _combine_kernel(d1_ref, d2_ref, ys_hbm, x1_ref, meta_ref, nw_ref, o_ref, ybuf, sem):
    i = pl.program_id(0)
    tm = COMB_TILE

    def issue(r, carry):
        t = i * tm + r
        pltpu.make_async_copy(ys_hbm.at[pl.ds(d1_ref[t], 1)], ybuf.at[0, pl.ds(r, 1)], sem).start()
        pltpu.make_async_copy(ys_hbm.at[pl.ds(d2_ref[t], 1)], ybuf.at[1, pl.ds(r, 1)], sem).start()
        return carry
    lax.fori_loop(0, tm, issue, 0, unroll=8)
    pltpu.make_async_copy(ys_hbm.at[pl.ds(0, tm)], ybuf.at[0], sem).wait()
    pltpu.make_async_copy(ys_hbm.at[pl.ds(0, tm)], ybuf.at[1], sem).wait()
    w1 = meta_ref[:, 4:5]
    w2 = meta_ref[:, 5:6]
    x2 = x1_ref[...] + (w1 * ybuf[0] + w2 * ybuf[1])
    o_ref[...] = _rms(x2, nw_ref[...])


def _combine(d1, d2, ys, x1, meta, norm_w):
    T = x1.shape[0]
    tm = COMB_TILE
    row = lambda i, a, b: (i, 0)
    grid_spec = pltpu.PrefetchScalarGridSpec(
        num_scalar_prefetch=2,
        grid=(T // tm,),
        in_specs=[pl.BlockSpec(memory_space=pl.ANY),
                  pl.BlockSpec((tm, D_MODEL), row),
                  pl.BlockSpec((tm, LANES), row),
                  pl.BlockSpec((1, D_MODEL), lambda i, a, b: (0, 0))],
        out_specs=pl.BlockSpec((tm, D_MODEL), row),
        scratch_shapes=[pltpu.VMEM((2, tm, D_MODEL), F32),
                        pltpu.SemaphoreType.DMA],
    )
    return pl.pallas_call(
        _combine_kernel,
        grid_spec=grid_spec,
        out_shape=jax.ShapeDtypeStruct((T, D_MODEL), F32),
        compiler_params=pltpu.CompilerParams(dimension_semantics=("arbitrary",),
                                             vmem_limit_bytes=VMEM_LIMIT),
        name="combine",
    )(d1, d2, ys, x1, meta, norm_w)


def _layer(x, norm_mix_w, w_in, conv_w, conv_b, b_igate, b_fgate, attn_sinks, mlstm_norm_w,
           w_attn_o, w_mlstm_o, w_out, norm_ffn_w, w_group, b_group, w_router, b_router,
           w_gate, w_up, w_down, out_norm_w):
    B, S, D = x.shape
    T = B * S
    x2 = x.reshape(T, D)

    w_main = jnp.concatenate([w_in[:, :_O_MI], w_in[:, _O_GA:]], axis=1).astype(BF16)
    w_gates = jnp.pad(w_in[:, _O_MI:_O_GA], ((0, 0), (0, LANES - 2 * M_HEADS))).astype(BF16)
    q, kv, mqk, mv, mo, gcol, grow, gab = _inproj(x2, norm_mix_w.reshape(1, D), w_main, w_gates)

    ya = _attn(attn_sinks.astype(F32), q.reshape(B, S, -1), kv.reshape(B, S, -1))

    bias = jnp.concatenate([b_igate, b_fgate]).astype(F32)
    bias_col = jnp.pad(bias, (0, LANES - 2 * M_HEADS)).reshape(1, LANES)
    ym = _mlstm(mqk.reshape(B, S, -1), mv.reshape(B, S, -1), mo.reshape(B, S, -1),
                gcol.reshape(B, S, LANES), grow, conv_w.astype(F32), conv_b.reshape(1, -1).astype(F32),
                bias_col, bias.reshape(2 * M_HEADS, 1), mlstm_norm_w.reshape(1, -1).astype(F32))

    w_rt = jnp.pad(jnp.concatenate([w_group, w_router], axis=1),
                   ((0, 0), (0, LANES - N_GROUPS - N_EXPERTS))).astype(F32)
    b_rt = jnp.pad(jnp.concatenate([b_group, b_router]), (0, LANES - N_GROUPS - N_EXPERTS)).reshape(1, LANES)
    x1, h2, meta, cnt = _outproj(x2, ya.reshape(T, -1), ym.reshape(T, -1), gab,
                                 w_attn_o.astype(BF16), w_mlstm_o.astype(BF16), w_out.astype(BF16),
                                 norm_ffn_w.reshape(1, D), w_rt, b_rt.astype(F32))

    R = SLOT_BLOCK
    counts = cnt[0, N_GROUPS:N_GROUPS + N_EXPERTS].astype(jnp.int32)
    pcounts = ((counts + R - 1) // R) * R
    pends = jnp.cumsum(pcounts)
    pstarts = pends - pcounts
    nblk = (2 * T) // R + N_EXPERTS
    e1 = meta[:, 0].astype(jnp.int32)
    e2 = meta[:, 1].astype(jnp.int32)
    d1 = pstarts[e1] + meta[:, 2].astype(jnp.int32)
    d2 = pstarts[e2] + meta[:, 3].astype(jnp.int32)
    tok = jnp.arange(T, dtype=jnp.int32)
    slot_tok = jnp.zeros((nblk * R,), jnp.int32).at[jnp.concatenate([d1, d2])].set(
        jnp.concatenate([tok, tok]))
    blk_e = jnp.minimum(jnp.searchsorted(pends, jnp.arange(nblk, dtype=jnp.int32) * R, side='right'),
                        N_EXPERTS - 1).astype(jnp.int32)
    nused = (pends[-1] // R).astype(jnp.int32).reshape(1)

    ys = _experts(blk_e, nused, slot_tok, h2, w_gate, w_up, w_down)
    out = _combine(d1, d2, ys, x1, meta, out_norm_w.reshape(1, D))
    return out.reshape(B, S, D)


def kernel(x, norm_mix_w, w_in, conv_w, conv_b, b_igate, b_fgate, attn_sinks, mlstm_norm_w, w_attn_o,
           w_mlstm_o, w_out, norm_ffn_w, w_group, b_group, w_router, b_router, w_gate, w_up, w_down,
           norm_final_w):
    depth = w_in.shape[0]
    assert depth == 1, "final RMSNorm is fused into the last layer's combine kernel"
    return _layer(x, norm_mix_w[0], w_in[0], conv_w[0], conv_b[0], b_igate[0], b_fgate[0],
                  attn_sinks[0], mlstm_norm_w[0], w_attn_o[0], w_mlstm_o[0], w_out[0], norm_ffn_w[0],
                  w_group[0], b_group[0], w_router[0], b_router[0], w_gate[0], w_up[0], w_down[0],
                  norm_final_w)
```

```python
import functools

import jax
import jax.numpy as jnp
from jax import lax
from jax.experimental import pallas as pl
from jax.experimental.pallas import tpu as pltpu

F32 = jnp.float32
BF16 = jnp.bfloat16

D_MODEL = 1024
N_Q_HEADS = 8
N_KV_HEADS = 2
HEAD_DIM = 64
WINDOW = 128
GQA_GROUP = N_Q_HEADS // N_KV_HEADS
M_HEADS = 4
M_QK_DIM = 64
M_V_DIM = 128
CONV_WIDTH = 4
N_GROUPS = 4
EXPERTS_PER_GROUP = 8
N_EXPERTS = N_GROUPS * EXPERTS_PER_GROUP
D_EXPERT = 512
EPS = 1e-6

ATT_Q_WIDTH = N_Q_HEADS * HEAD_DIM
ATT_KV_WIDTH = N_KV_HEADS * HEAD_DIM
M_QK_WIDTH = M_HEADS * M_QK_DIM
M_V_WIDTH = M_HEADS * M_V_DIM

LANES = 128
NEG = -1e30
VMEM_LIMIT = 56 * 1024 * 1024

TOK_TILE = 512
ATT_TILE = 256
M_CHUNK = 128
SLOT_BLOCK = 256
COMB_TILE = 256
RANK_RADIX = 65536
RANK_BITS = 16

_O_AQ = 0
_O_AK = _O_AQ + ATT_Q_WIDTH
_O_AV = _O_AK + ATT_KV_WIDTH
_O_MQ = _O_AV + ATT_KV_WIDTH
_O_MK = _O_MQ + M_QK_WIDTH
_O_MV = _O_MK + M_QK_WIDTH
_O_MO = _O_MV + M_V_WIDTH
_O_MI = _O_MO + M_V_WIDTH
_O_MF = _O_MI + M_HEADS
_O_GA = _O_MF + M_HEADS
_O_GB = _O_GA + D_MODEL
_O_END = _O_GB + D_MODEL


def _rms(x, w):
    return x * lax.rsqrt(jnp.mean(x * x, axis=-1, keepdims=True) + EPS) * w


def _sigmoid(x):
    return 1.0 / (1.0 + jnp.exp(-x))


def _log_sigmoid(x):
    return jnp.minimum(x, 0.0) - jnp.log1p(jnp.exp(-jnp.abs(x)))


def _inproj_kernel(x_ref, nw_ref, w_ref, wg_ref,
                   q_ref, kv_ref, mqk_ref, mv_ref, mo_ref, gcol_ref, grow_ref, gab_ref):
    h = _rms(x_ref[...], nw_ref[...]).astype(BF16)

    def proj(lo, hi):
        return jnp.dot(h, w_ref[:, lo:hi], preferred_element_type=F32)

    q_ref[...] = proj(_O_AQ, _O_AK).astype(BF16)
    kv_ref[...] = proj(_O_AK, _O_MQ).astype(BF16)
    mqk_ref[...] = proj(_O_MQ, _O_MV).astype(BF16)
    mv_ref[...] = proj(_O_MV, _O_MO).astype(BF16)
    mo_ref[...] = proj(_O_MO, _O_MI).astype(BF16)
    gab_ref[...] = proj(_O_MI, _O_MI + 2 * D_MODEL).astype(BF16)
    g = jnp.dot(h, wg_ref[...], preferred_element_type=F32)
    gcol_ref[...] = g
    grow_ref[...] = g.T[0:8, :]


def _inproj(x2, norm_w, w_main, w_gates):
    T = x2.shape[0]
    tm = TOK_TILE
    nmain = w_main.shape[1]
    row = lambda i: (i, 0)
    const = lambda i: (0, 0)
    return pl.pallas_call(
        _inproj_kernel,
        grid=(T // tm,),
        in_specs=[pl.BlockSpec((tm, D_MODEL), row),
                  pl.BlockSpec((1, D_MODEL), const),
                  pl.BlockSpec((D_MODEL, nmain), const),
                  pl.BlockSpec((D_MODEL, LANES), const)],
        out_specs=[pl.BlockSpec((tm, ATT_Q_WIDTH), row),
                   pl.BlockSpec((tm, 2 * ATT_KV_WIDTH), row),
                   pl.BlockSpec((tm, 2 * M_QK_WIDTH), row),
                   pl.BlockSpec((tm, M_V_WIDTH), row),
                   pl.BlockSpec((tm, M_V_WIDTH), row),
                   pl.BlockSpec((tm, LANES), row),
                   pl.BlockSpec((8, tm), lambda i: (0, i)),
                   pl.BlockSpec((tm, 2 * D_MODEL), row)],
        out_shape=[jax.ShapeDtypeStruct((T, ATT_Q_WIDTH), BF16),
                   jax.ShapeDtypeStruct((T, 2 * ATT_KV_WIDTH), BF16),
                   jax.ShapeDtypeStruct((T, 2 * M_QK_WIDTH), BF16),
                   jax.ShapeDtypeStruct((T, M_V_WIDTH), BF16),
                   jax.ShapeDtypeStruct((T, M_V_WIDTH), BF16),
                   jax.ShapeDtypeStruct((T, LANES), F32),
                   jax.ShapeDtypeStruct((8, T), F32),
                   jax.ShapeDtypeStruct((T, 2 * D_MODEL), BF16)],
        compiler_params=pltpu.CompilerParams(dimension_semantics=("arbitrary",),
                                             vmem_limit_bytes=VMEM_LIMIT),
        name="inproj",
    )(x2, norm_w, w_main, w_gates)


def _attn_kernel(sink_ref, q_ref, kv_ref, kvp_ref, o_ref):
    i = pl.program_id(1)
    W = WINDOW
    r = lax.broadcasted_iota(jnp.int32, (W, 2 * W), 0)
    c = lax.broadcasted_iota(jnp.int32, (W, 2 * W), 1)
    band = (c > r) & (c <= r + W)
    for j in range(ATT_TILE // W):
        qj = q_ref[0, j * W:(j + 1) * W, :]
        if j == 0:
            kprev = kvp_ref[0]
            valid = band & ((i > 0) | (c >= W))
        else:
            kprev = kv_ref[0, (j - 1) * W:j * W, :]
            valid = band
        kwin = jnp.concatenate([kprev, kv_ref[0, j * W:(j + 1) * W, :]], axis=0)
        outs = []
        for h in range(N_Q_HEADS):
            g = h // GQA_GROUP
            qh = qj[:, h * HEAD_DIM:(h + 1) * HEAD_DIM]
            kg = kwin[:, g * HEAD_DIM:(g + 1) * HEAD_DIM]
            vg = kwin[:, ATT_KV_WIDTH + g * HEAD_DIM:ATT_KV_WIDTH + (g + 1) * HEAD_DIM]
            s = lax.dot_general(qh, kg, (((1,), (1,)), ((), ())),
                                preferred_element_type=F32) * (HEAD_DIM ** -0.5)
            s = jnp.where(valid, s, NEG)
            sink = sink_ref[h]
            m = jnp.maximum(jnp.max(s, axis=-1, keepdims=True), sink)
            p = jnp.exp(s - m)
            l = jnp.sum(p, axis=-1, keepdims=True) + jnp.exp(sink - m)
            o = jnp.dot(p.astype(BF16), vg, preferred_element_type=F32)
            outs.append(o / l)
        o_ref[0, j * W:(j + 1) * W, :] = jnp.concatenate(outs, axis=-1).astype(BF16)


def _attn(sinks, q3, kv3):
    B, S, _ = q3.shape
    tq = ATT_TILE
    per = tq // WINDOW
    return pl.pallas_call(
        _attn_kernel,
        grid=(B, S // tq),
        in_specs=[pl.BlockSpec(memory_space=pltpu.SMEM),
                  pl.BlockSpec((1, tq, ATT_Q_WIDTH), lambda b, i: (b, i, 0)),
                  pl.BlockSpec((1, tq, 2 * ATT_KV_WIDTH), lambda b, i: (b, i, 0)),
                  pl.BlockSpec((1, WINDOW, 2 * ATT_KV_WIDTH),
                               lambda b, i: (b, jnp.maximum(i * per - 1, 0), 0))],
        out_specs=pl.BlockSpec((1, tq, ATT_Q_WIDTH), lambda b, i: (b, i, 0)),
        out_shape=jax.ShapeDtypeStruct((B, S, ATT_Q_WIDTH), BF16),
        compiler_params=pltpu.CompilerParams(dimension_semantics=("arbitrary", "arbitrary"),
                                             vmem_limit_bytes=VMEM_LIMIT),
        name="attn",
    )(sinks, q3, kv3, kv3)


def _mlstm_kernel(mqk_ref, mv_ref, mo_ref, gcol_ref, grow_ref, cw_ref, cb_ref, bcol_ref, brow_ref,
                  nw_ref, y_ref, cext_ref, m_ref, ubuf_ref):
    L = M_CHUNK
    cidx = pl.program_id(1)

    @pl.when(cidx == 0)
    def _():
        cext_ref[...] = jnp.zeros_like(cext_ref)
        m_ref[...] = jnp.zeros_like(m_ref)
        ubuf_ref[0:8, :] = jnp.zeros((8, 2 * M_QK_WIDTH), F32)

    ubuf_ref[8:L + 8, :] = mqk_ref[0].astype(F32)
    acc = cb_ref[...] + cw_ref[CONV_WIDTH - 1:CONV_WIDTH, :] * ubuf_ref[8:L + 8, :]
    for j in range(CONV_WIDTH - 1):
        off = 8 - (CONV_WIDTH - 1) + j
        acc = acc + cw_ref[j:j + 1, :] * ubuf_ref[off:off + L, :]
    ubuf_ref[0:8, :] = ubuf_ref[L:L + 8, :]
    qk = acc * _sigmoid(acc)
    q = qk[:, :M_QK_WIDTH].astype(BF16)
    k = qk[:, M_QK_WIDTH:] * (M_QK_DIM ** -0.5)

    gc = gcol_ref[0] + bcol_ref[...]
    gr = grow_ref[...] + brow_ref[...]
    ti = lax.broadcasted_iota(jnp.int32, (L, L), 0)
    si = lax.broadcasted_iota(jnp.int32, (L, L), 1)
    causal = si <= ti
    tril = jnp.where(causal, 1.0, 0.0).astype(F32)
    triu = jnp.where(ti <= si, 1.0, 0.0).astype(F32)
    b_c = jnp.dot(tril, _log_sigmoid(gc), preferred_element_type=F32,
                  precision=lax.Precision.HIGHEST)
    b_r = jnp.dot(_log_sigmoid(gr), triu, preferred_element_type=F32,
                  precision=lax.Precision.HIGHEST)

    ext = jnp.where(lax.broadcasted_iota(jnp.int32, (L, LANES), 1) == 0, 1.0, 0.0).astype(BF16)
    outs = []
    for h in range(M_HEADS):
        bc = b_c[:, M_HEADS + h:M_HEADS + h + 1]
        br = b_r[M_HEADS + h:M_HEADS + h + 1, :]
        ir = gr[h:h + 1, :]
        ic = gc[:, h:h + 1]
        m_prev = m_ref[h:h + 1, 0:1]
        dmat = jnp.where(causal, bc - br + ir, NEG)
        a = bc + m_prev
        mt = jnp.maximum(a, jnp.max(dmat, axis=-1, keepdims=True))
        wq = jnp.exp(dmat - mt)
        wa = jnp.exp(a - mt)
        qh = q[:, h * M_QK_DIM:(h + 1) * M_QK_DIM]
        kh = k[:, h * M_QK_DIM:(h + 1) * M_QK_DIM]
        sw = lax.dot_general(qh, kh.astype(BF16), (((1,), (1,)), ((), ())),
                             preferred_element_type=F32) * wq
        vext = jnp.concatenate([mv_ref[0, :, h * M_V_DIM:(h + 1) * M_V_DIM], ext], axis=-1)
        cprev = cext_ref[h]
        res = wa * jnp.dot(qh, cprev.astype(BF16), preferred_element_type=F32) \
            + jnp.dot(sw.astype(BF16), vext, preferred_element_type=F32)
        num = res[:, :M_V_DIM]
        den = res[:, M_V_DIM:M_V_DIM + 1]
        hb = num / jnp.maximum(jnp.abs(den), jnp.exp(-mt))
        m_new = mt[L - 1:L, :]
        bl = bc[L - 1:L, :]
        wc = jnp.exp(bl + m_prev - m_new)
        ws = jnp.exp(bl - bc + ic - m_new)
        kw = (kh * ws).astype(BF16)
        cext_ref[h] = wc * cprev + lax.dot_general(kw, vext, (((0,), (0,)), ((), ())),
                                                   preferred_element_type=F32)
        m_ref[h:h + 1, :] = jnp.broadcast_to(m_new, (1, LANES))
        hn = hb * lax.rsqrt(jnp.mean(hb * hb, axis=-1, keepdims=True) + EPS)
        hn = hn * nw_ref[:, h * M_V_DIM:(h + 1) * M_V_DIM]
        og = _sigmoid(mo_ref[0, :, h * M_V_DIM:(h + 1) * M_V_DIM].astype(F32))
        outs.append(og * hn)
    y_ref[0] = jnp.concatenate(outs, axis=-1).astype(BF16)


def _mlstm(mqk3, mv3, mo3, gcol3, grow, conv_w, conv_b, bias_col, bias_row, norm_w):
    B, S, _ = mqk3.shape
    L = M_CHUNK
    nc = S // L
    tok = lambda b, c: (b, c, 0)
    const = lambda b, c: (0, 0)
    return pl.pallas_call(
        _mlstm_kernel,
        grid=(B, nc),
        in_specs=[pl.BlockSpec((1, L, 2 * M_QK_WIDTH), tok),
                  pl.BlockSpec((1, L, M_V_WIDTH), tok),
                  pl.BlockSpec((1, L, M_V_WIDTH), tok),
                  pl.BlockSpec((1, L, LANES), tok),
                  pl.BlockSpec((8, L), lambda b, c: (0, b * nc + c)),
                  pl.BlockSpec((CONV_WIDTH, 2 * M_QK_WIDTH), const),
                  pl.BlockSpec((1, 2 * M_QK_WIDTH), const),
                  pl.BlockSpec((1, LANES), const),
                  pl.BlockSpec((8, 1), const),
                  pl.BlockSpec((1, M_V_WIDTH), const)],
        out_specs=pl.BlockSpec((1, L, M_V_WIDTH), tok),
        out_shape=jax.ShapeDtypeStruct((B, S, M_V_WIDTH), BF16),
        scratch_shapes=[pltpu.VMEM((M_HEADS, M_QK_DIM, 2 * M_V_DIM), F32),
                        pltpu.VMEM((8, LANES), F32),
                        pltpu.VMEM((L + 8, 2 * M_QK_WIDTH), F32)],
        compiler_params=pltpu.CompilerParams(dimension_semantics=("arbitrary", "arbitrary"),
                                             vmem_limit_bytes=VMEM_LIMIT),
        name="mlstm",
    )(mqk3, mv3, mo3, gcol3, grow, conv_w, conv_b, bias_col, bias_row, norm_w)


def _outproj_kernel(x_ref, ya_ref, ym_ref, gab_ref, wa_ref, wm_ref, wo_ref, nw_ref, wr_ref, br_ref,
                    x1_ref, h2_ref, meta_ref, code_ref, cnt_ref, cnt_scr):
    tm = TOK_TILE
    i = pl.program_id(0)

    @pl.when(i == 0)
    def _():
        cnt_scr[...] = jnp.zeros_like(cnt_scr)

    pa = jnp.dot(ya_ref[...], wa_ref[...], preferred_element_type=F32)
    pm = jnp.dot(ym_ref[...], wm_ref[...], preferred_element_type=F32)
    ga = gab_ref[:, :D_MODEL].astype(F32)
    gb = gab_ref[:, D_MODEL:].astype(F32)
    mix = _sigmoid(ga) * pa + _sigmoid(gb) * pm
    x1 = x_ref[...] + jnp.dot(mix.astype(BF16), wo_ref[...], preferred_element_type=F32)
    x1_ref[...] = x1
    h2 = _rms(x1, nw_ref[...])
    h2_ref[...] = h2

    logits = jnp.dot(h2, wr_ref[...], preferred_element_type=F32,
                     precision=lax.Precision.HIGHEST) + br_ref[...]
    lane = lax.broadcasted_iota(jnp.int32, (tm, LANES), 1)
    big = jnp.int32(LANES)

    def first_argmax(v):
        mx = jnp.max(v, axis=-1, keepdims=True)
        idx = jnp.min(jnp.where(v == mx, lane, big), axis=-1, keepdims=True)
        return mx, idx

    gl = jnp.where(lane < N_GROUPS, logits, NEG)
    gmax, gi = first_argmax(gl)
    gp = 1.0 / jnp.sum(jnp.exp(gl - gmax), axis=-1, keepdims=True)
    lo = N_GROUPS + gi * EXPERTS_PER_GROUP
    el = jnp.where((lane >= lo) & (lane < lo + EXPERTS_PER_GROUP), logits, NEG)
    v1, j1 = first_argmax(el)
    v2, j2 = first_argmax(jnp.where(lane == j1, NEG, el))
    t = jnp.exp(v2 - v1)
    w1 = gp / (1.0 + t)
    w2 = gp * t / (1.0 + t)

    hit1 = lane == j1
    hit2 = lane == j2
    onehot = jnp.where(hit1 | hit2, 1.0, 0.0)
    ri = lax.broadcasted_iota(jnp.int32, (tm, tm), 0)
    ci = lax.broadcasted_iota(jnp.int32, (tm, tm), 1)
    before = jnp.where(ci < ri, 1.0, 0.0).astype(BF16)
    rank = jnp.dot(before, onehot.astype(BF16), preferred_element_type=F32) + cnt_scr[0:1, :]
    r1 = jnp.sum(jnp.where(hit1, rank, 0.0), axis=-1, keepdims=True)
    r2 = jnp.sum(jnp.where(hit2, rank, 0.0), axis=-1, keepdims=True)
    cnt = cnt_scr[0:1, :] + jnp.sum(onehot, axis=0, keepdims=True)
    cnt_scr[...] = jnp.broadcast_to(cnt, cnt_scr.shape)
    cnt_ref[...] = jnp.broadcast_to(cnt, cnt_ref.shape)

    c1 = (j1 - N_GROUPS).astype(F32) * RANK_RADIX + r1
    c2 = (j2 - N_GROUPS).astype(F32) * RANK_RADIX + r2
    meta = jnp.where(lane == 0, c1, 0.0)
    meta = jnp.where(lane == 1, c2, meta)
    meta = jnp.where(lane == 4, w1, meta)
    meta = jnp.where(lane == 5, w2, meta)
    meta_ref[...] = meta
    code_ref[...] = meta.T[0:8, :].astype(jnp.int32)


def _outproj(x2, ya, ym, gab, wa, wm, wo, norm_w, w_rt, b_rt):
    T = x2.shape[0]
    tm = TOK_TILE
    row = lambda i: (i, 0)
    const = lambda i: (0, 0)
    return pl.pallas_call(
        _outproj_kernel,
        grid=(T // tm,),
        in_specs=[pl.BlockSpec((tm, D_MODEL), row),
                  pl.BlockSpec((tm, ATT_Q_WIDTH), row),
                  pl.BlockSpec((tm, M_V_WIDTH), row),
                  pl.BlockSpec((tm, 2 * D_MODEL), row),
                  pl.BlockSpec((ATT_Q_WIDTH, D_MODEL), const),
                  pl.BlockSpec((M_V_WIDTH, D_MODEL), const),
                  pl.BlockSpec((D_MODEL, D_MODEL), const),
                  pl.BlockSpec((1, D_MODEL), const),
                  pl.BlockSpec((D_MODEL, LANES), const),
                  pl.BlockSpec((1, LANES), const)],
        out_specs=[pl.BlockSpec((tm, D_MODEL), row),
                   pl.BlockSpec((tm, D_MODEL), row),
                   pl.BlockSpec((tm, LANES), row),
                   pl.BlockSpec((8, tm), lambda i: (0, i)),
                   pl.BlockSpec((8, LANES), const)],
        out_shape=[jax.ShapeDtypeStruct((T, D_MODEL), F32),
                   jax.ShapeDtypeStruct((T, D_MODEL), F32),
                   jax.ShapeDtypeStruct((T, LANES), F32),
                   jax.ShapeDtypeStruct((8, T), jnp.int32),
                   jax.ShapeDtypeStruct((8, LANES), F32)],
        scratch_shapes=[pltpu.VMEM((8, LANES), F32)],
        compiler_params=pltpu.CompilerParams(dimension_semantics=("arbitrary",),
                                             vmem_limit_bytes=VMEM_LIMIT),
        name="outproj",
    )(x2, ya, ym, gab, wa, wm, wo, norm_w, w_rt, b_rt)


def _slots_kernel(codes_ref, cnt_ref, tok_ref, pstart_ref, blk_e_ref, nused_ref):
    R = SLOT_BLOCK
    T = codes_ref.shape[0] // 2
    nblk = blk_e_ref.shape[0]

    def per_expert(e, carry):
        run, nassign = carry
        c = cnt_ref[e]
        end = run + ((c + R - 1) // R) * R
        pstart_ref[e] = run

        def set_blk(b, carry):
            blk_e_ref[b] = e
            return carry
        lax.fori_loop(run // R, end // R, set_blk, 0)

        def set_pad(s, carry):
            tok_ref[s] = 0
            return carry
        lax.fori_loop(run + c, end, set_pad, 0)
        return end, nassign + c

    total, nassign = lax.fori_loop(0, N_EXPERTS, per_expert, (0, 0))
    nused_ref[0] = total // R

    def tail_blk(b, carry):
        blk_e_ref[b] = N_EXPERTS - 1
        return carry
    lax.fori_loop(total // R, nblk, tail_blk, 0)

    def tail_slot(s, carry):
        tok_ref[s] = 0
        return carry
    lax.fori_loop(total, nblk * R, tail_slot, 0)

    def per_token(t, carry):
        c1 = codes_ref[t]
        c2 = codes_ref[T + t]
        tok_ref[pstart_ref[c1 >> RANK_BITS] + (c1 & (RANK_RADIX - 1))] = t
        tok_ref[pstart_ref[c2 >> RANK_BITS] + (c2 & (RANK_RADIX - 1))] = t
        return carry
    lax.fori_loop(0, nassign // 2, per_token, 0)


def _slots(codes, counts):
    T = codes.shape[0] // 2
    nblk = (2 * T) // SLOT_BLOCK + N_EXPERTS
    smem = pl.BlockSpec(memory_space=pltpu.SMEM)
    return pl.pallas_call(
        _slots_kernel,
        in_specs=[smem, smem],
        out_specs=[smem, smem, smem, smem],
        out_shape=[jax.ShapeDtypeStruct((nblk * SLOT_BLOCK,), jnp.int32),
                   jax.ShapeDtypeStruct((N_EXPERTS,), jnp.int32),
                   jax.ShapeDtypeStruct((nblk,), jnp.int32),
                   jax.ShapeDtypeStruct((1,), jnp.int32)],
        name="slots",
    )(codes, counts)


def _experts_kernel(blk_e_ref, nused_ref, tok_ref, h2_hbm, wg_ref, wu_ref, wd_ref, ys_ref,
                    xbuf0, xbuf1, wg_bf, wu_bf, wd_bf, sem0, sem1):
    j = pl.program_id(0)
    R = SLOT_BLOCK
    nused = nused_ref[0]
    even = j % 2 == 0
    valid = j < nused

    def row_copy(blk, r, buf, sem):
        tok = tok_ref[blk * R + r]
        return pltpu.make_async_copy(h2_hbm.at[pl.ds(tok, 1)], buf.at[pl.ds(r, 1)], sem)

    def wait_rows(buf, sem):
        pltpu.make_async_copy(h2_hbm.at[pl.ds(0, R)], buf, sem).wait()

    @pl.when(j == 0)
    def _():
        def issue(r, carry):
            row_copy(0, r, xbuf0, sem0).start()
            return carry
        lax.fori_loop(0, R, issue, 0, unroll=8)

    changed = jnp.logical_or(j == 0, blk_e_ref[j] != blk_e_ref[jnp.maximum(j - 1, 0)])

    @pl.when(jnp.logical_and(valid, changed))
    def _():
        wg_bf[...] = wg_ref[0].astype(BF16)
        wu_bf[...] = wu_ref[0].astype(BF16)
        wd_bf[...] = wd_ref[0].astype(BF16)

    def step(cur_buf, cur_sem, nxt_buf, nxt_sem):
        wait_rows(cur_buf, cur_sem)
        nxt = jnp.where(j + 1 < nused, j + 1, 0)
        for r in range(R):
            row_copy(nxt, r, nxt_buf, nxt_sem).start()
        xb = cur_buf[...].astype(BF16)
        g = jnp.dot(xb, wg_bf[...], preferred_element_type=F32)
        u = jnp.dot(xb, wu_bf[...], preferred_element_type=F32)
        act = (g * _sigmoid(g) * u).astype(BF16)
        ys_ref[...] = jnp.dot(act, wd_bf[...], preferred_element_type=F32)

    @pl.when(jnp.logical_and(valid, even))
    def _():
        step(xbuf0, sem0, xbuf1, sem1)

    @pl.when(jnp.logical_and(valid, jnp.logical_not(even)))
    def _():
        step(xbuf1, sem1, xbuf0, sem0)

    @pl.when(jnp.logical_and(j == nused, even))
    def _():
        wait_rows(xbuf0, sem0)

    @pl.when(jnp.logical_and(j == nused, jnp.logical_not(even)))
    def _():
        wait_rows(xbuf1, sem1)

    @pl.when(j >= nused)
    def _():
        ys_ref[...] = jnp.zeros_like(ys_ref)


def _experts(blk_e, nused, slot_tok, h2, w_gate, w_up, w_down):
    nblk = blk_e.shape[0]
    R = SLOT_BLOCK
    wspec = lambda shape: pl.BlockSpec((1,) + shape, lambda j, be, nu, st: (be[j], 0, 0))
    grid_spec = pltpu.PrefetchScalarGridSpec(
        num_scalar_prefetch=3,
        grid=(nblk,),
        in_specs=[pl.BlockSpec(memory_space=pl.ANY),
                  wspec((D_MODEL, D_EXPERT)),
                  wspec((D_MODEL, D_EXPERT)),
                  wspec((D_EXPERT, D_MODEL))],
        out_specs=pl.BlockSpec((R, D_MODEL), lambda j, be, nu, st: (j, 0)),
        scratch_shapes=[pltpu.VMEM((R, D_MODEL), F32),
                        pltpu.VMEM((R, D_MODEL), F32),
                        pltpu.VMEM((D_MODEL, D_EXPERT), BF16),
                        pltpu.VMEM((D_MODEL, D_EXPERT), BF16),
                        pltpu.VMEM((D_EXPERT, D_MODEL), BF16),
                        pltpu.SemaphoreType.DMA,
                        pltpu.SemaphoreType.DMA],
    )
    return pl.pallas_call(
        _experts_kernel,
        grid_spec=grid_spec,
        out_shape=jax.ShapeDtypeStruct((nblk * R, D_MODEL), F32),
        compiler_params=pltpu.CompilerParams(dimension_semantics=("arbitrary",),
                                             vmem_limit_bytes=VMEM_LIMIT),
        name="experts",
    )(blk_e, nused, slot_tok, h2, w_gate, w_up, w_down)


def _combine_kernel(codes_ref, pstart_ref, ys_hbm, x1_ref, meta_ref, nw_ref, o_ref, ybuf, sem):
    i = pl.program_id(0)
    tm = COMB_TILE
    T = codes_ref.shape[0] // 2

    def issue(r, carry):
        t = i * tm + r
        for k in range(2):
            c = codes_ref[k * T + t]
            slot = pstart_ref[c >> RANK_BITS] + (c & (RANK_RADIX - 1))
            pltpu.make_async_copy(ys_hbm.at[pl.ds(slot, 1)], ybuf.at[k, pl.ds(r, 1)], sem).start()
        return carry
    lax.fori_loop(0, tm, issue, 0, unroll=8)
    pltpu.make_async_copy(ys_hbm.at[pl.ds(0, tm)], ybuf.at[0], sem).wait()
    pltpu.make_async_copy(ys_hbm.at[pl.ds(0, tm)], ybuf.at[1], sem).wait()
    w1 = meta_ref[:, 4:5]
    w2 = meta_ref[:, 5:6]
    x2 = x1_ref[...] + (w1 * ybuf[0] + w2 * ybuf[1])
    o_ref[...] = _rms(x2, nw_ref[...])


def _combine(codes, pstart, ys, x1, meta, norm_w):
    T = x1.shape[0]
    tm = COMB_TILE
    row = lambda i, a, b: (i, 0)
    grid_spec = pltpu.PrefetchScalarGridSpec(
        num_scalar_prefetch=2,
        grid=(T // tm,),
        in_specs=[pl.BlockSpec(memory_space=pl.ANY),
                  pl.BlockSpec((tm, D_MODEL), row),
                  pl.BlockSpec((tm, LANES), row),
                  pl.BlockSpec((1, D_MODEL), lambda i, a, b: (0, 0))],
        out_specs=pl.BlockSpec((tm, D_MODEL), row),
        scratch_shapes=[pltpu.VMEM((2, tm, D_MODEL), F32),
                        pltpu.SemaphoreType.DMA],
    )
    return pl.pallas_call(
        _combine_kernel,
        grid_spec=grid_spec,
        out_shape=jax.ShapeDtypeStruct((T, D_MODEL), F32),
        compiler_params=pltpu.CompilerParams(dimension_semantics=("arbitrary",),
                                             vmem_limit_bytes=VMEM_LIMIT),
        name="combine",
    )(codes, pstart, ys, x1, meta, norm_w)


def _layer(x, norm_mix_w, w_in, conv_w, conv_b, b_igate, b_fgate, attn_sinks, mlstm_norm_w,
           w_attn_o, w_mlstm_o, w_out, norm_ffn_w, w_group, b_group, w_router, b_router,
           w_gate, w_up, w_down, out_norm_w):
    B, S, D = x.shape
    T = B * S
    x2 = x.reshape(T, D)

    w_main = jnp.concatenate([w_in[:, :_O_MI], w_in[:, _O_GA:]], axis=1).astype(BF16)
    w_gates = jnp.pad(w_in[:, _O_MI:_O_GA], ((0, 0), (0, LANES - 2 * M_HEADS))).astype(BF16)
    q, kv, mqk, mv, mo, gcol, grow, gab = _inproj(x2, norm_mix_w.reshape(1, D), w_main, w_gates)

    ya = _attn(attn_sinks.astype(F32), q.reshape(B, S, -1), kv.reshape(B, S, -1))

    bias = jnp.concatenate([b_igate, b_fgate]).astype(F32)
    bias_col = jnp.pad(bias, (0, LANES - 2 * M_HEADS)).reshape(1, LANES)
    ym = _mlstm(mqk.reshape(B, S, -1), mv.reshape(B, S, -1), mo.reshape(B, S, -1),
                gcol.reshape(B, S, LANES), grow, conv_w.astype(F32), conv_b.reshape(1, -1).astype(F32),
                bias_col, bias.reshape(2 * M_HEADS, 1), mlstm_norm_w.reshape(1, -1).astype(F32))

    w_rt = jnp.pad(jnp.concatenate([w_group, w_router], axis=1),
                   ((0, 0), (0, LANES - N_GROUPS - N_EXPERTS))).astype(F32)
    b_rt = jnp.pad(jnp.concatenate([b_group, b_router]), (0, LANES - N_GROUPS - N_EXPERTS)).reshape(1, LANES)
    x1, h2, meta, code_rows, cnt = _outproj(x2, ya.reshape(T, -1), ym.reshape(T, -1), gab,
                                            w_attn_o.astype(BF16), w_mlstm_o.astype(BF16),
                                            w_out.astype(BF16), norm_ffn_w.reshape(1, D), w_rt,
                                            b_rt.astype(F32))

    codes = code_rows[0:2].reshape(2 * T)
    counts = cnt[0, N_GROUPS:N_GROUPS + N_EXPERTS].astype(jnp.int32)
    slot_tok, pstart, blk_e, nused = _slots(codes, counts)

    ys = _experts(blk_e, nused, slot_tok, h2, w_gate, w_up, w_down)
    out = _combine(codes, pstart, ys, x1, meta, out_norm_w.reshape(1, D))
    return out.reshape(B, S, D)


def kernel(x, norm_mix_w, w_in, conv_w, conv_b, b_igate, b_fgate, attn_sinks, mlstm_norm_w, w_attn_o,
           w_mlstm_o, w_out, norm_ffn_w, w_group, b_group, w_router, b_router, w_gate, w_up, w_down,
           norm_final_w):
    depth = w_in.shape[0]
    assert depth == 1, "final RMSNorm is fused into the last layer's combine kernel"
    return _layer(x, norm_mix_w[0], w_in[0], conv_w[0], conv_b[0], b_igate[0], b_fgate[0],
                  attn_sinks[0], mlstm_norm_w[0], w_attn_o[0], w_mlstm_o[0], w_out[0], norm_ffn_w[0],
                  w_group[0], b_group[0], w_router[0], b_router[0], w_gate[0], w_up[0], w_down[0],
                  norm_final_w)
```

```python
import functools

import jax
import jax.numpy as jnp
from jax import lax
from jax.experimental import pallas as pl
from jax.experimental.pallas import tpu as pltpu

F32 = jnp.float32
BF16 = jnp.bfloat16

D_MODEL = 1024
N_Q_HEADS = 8
N_KV_HEADS = 2
HEAD_DIM = 64
WINDOW = 128
GQA_GROUP = N_Q_HEADS // N_KV_HEADS
M_HEADS = 4
M_QK_DIM = 64
M_V_DIM = 128
CONV_WIDTH = 4
N_GROUPS = 4
EXPERTS_PER_GROUP = 8
N_EXPERTS = N_GROUPS * EXPERTS_PER_GROUP
D_EXPERT = 512
EPS = 1e-6

ATT_Q_WIDTH = N_Q_HEADS * HEAD_DIM
ATT_KV_WIDTH = N_KV_HEADS * HEAD_DIM
M_QK_WIDTH = M_HEADS * M_QK_DIM
M_V_WIDTH = M_HEADS * M_V_DIM

LANES = 128
NEG = -1e30
VMEM_LIMIT = 56 * 1024 * 1024

TOK_TILE = 512
ATT_TILE = 256
M_CHUNK = 128
SLOT_BLOCK = 256
COMB_TILE = 256
RANK_RADIX = 65536
RANK_BITS = 16
DMA_THREADS = 2

_O_AQ = 0
_O_AK = _O_AQ + ATT_Q_WIDTH
_O_AV = _O_AK + ATT_KV_WIDTH
_O_MQ = _O_AV + ATT_KV_WIDTH
_O_MK = _O_MQ + M_QK_WIDTH
_O_MV = _O_MK + M_QK_WIDTH
_O_MO = _O_MV + M_V_WIDTH
_O_MI = _O_MO + M_V_WIDTH
_O_MF = _O_MI + M_HEADS
_O_GA = _O_MF + M_HEADS
_O_GB = _O_GA + D_MODEL
_O_END = _O_GB + D_MODEL


def _rms(x, w):
    return x * lax.rsqrt(jnp.mean(x * x, axis=-1, keepdims=True) + EPS) * w


def _sigmoid(x):
    return 1.0 / (1.0 + jnp.exp(-x))


def _log_sigmoid(x):
    return jnp.minimum(x, 0.0) - jnp.log1p(jnp.exp(-jnp.abs(x)))


def _inproj_kernel(x_ref, nw_ref, w_ref, wg_ref,
                   q_ref, kv_ref, mqk_ref, mv_ref, mo_ref, gcol_ref, grow_ref, gab_ref):
    h = _rms(x_ref[...], nw_ref[...]).astype(BF16)

    def proj(lo, hi):
        return jnp.dot(h, w_ref[:, lo:hi], preferred_element_type=F32)

    q_ref[...] = proj(_O_AQ, _O_AK).astype(BF16)
    kv_ref[...] = proj(_O_AK, _O_MQ).astype(BF16)
    mqk_ref[...] = proj(_O_MQ, _O_MV).astype(BF16)
    mv_ref[...] = proj(_O_MV, _O_MO).astype(BF16)
    mo_ref[...] = proj(_O_MO, _O_MI).astype(BF16)
    gab_ref[...] = proj(_O_MI, _O_MI + 2 * D_MODEL).astype(BF16)
    g = jnp.dot(h, wg_ref[...], preferred_element_type=F32)
    gcol_ref[...] = g
    grow_ref[...] = g.T[0:8, :]


def _inproj(x2, norm_w, w_main, w_gates):
    T = x2.shape[0]
    tm = TOK_TILE
    nmain = w_main.shape[1]
    row = lambda i: (i, 0)
    const = lambda i: (0, 0)
    return pl.pallas_call(
        _inproj_kernel,
        grid=(T // tm,),
        in_specs=[pl.BlockSpec((tm, D_MODEL), row),
                  pl.BlockSpec((1, D_MODEL), const),
                  pl.BlockSpec((D_MODEL, nmain), const),
                  pl.BlockSpec((D_MODEL, LANES), const)],
        out_specs=[pl.BlockSpec((tm, ATT_Q_WIDTH), row),
                   pl.BlockSpec((tm, 2 * ATT_KV_WIDTH), row),
                   pl.BlockSpec((tm, 2 * M_QK_WIDTH), row),
                   pl.BlockSpec((tm, M_V_WIDTH), row),
                   pl.BlockSpec((tm, M_V_WIDTH), row),
                   pl.BlockSpec((tm, LANES), row),
                   pl.BlockSpec((8, tm), lambda i: (0, i)),
                   pl.BlockSpec((tm, 2 * D_MODEL), row)],
        out_shape=[jax.ShapeDtypeStruct((T, ATT_Q_WIDTH), BF16),
                   jax.ShapeDtypeStruct((T, 2 * ATT_KV_WIDTH), BF16),
                   jax.ShapeDtypeStruct((T, 2 * M_QK_WIDTH), BF16),
                   jax.ShapeDtypeStruct((T, M_V_WIDTH), BF16),
                   jax.ShapeDtypeStruct((T, M_V_WIDTH), BF16),
                   jax.ShapeDtypeStruct((T, LANES), F32),
                   jax.ShapeDtypeStruct((8, T), F32),
                   jax.ShapeDtypeStruct((T, 2 * D_MODEL), BF16)],
        compiler_params=pltpu.CompilerParams(dimension_semantics=("arbitrary",),
                                             vmem_limit_bytes=VMEM_LIMIT),
        name="inproj",
    )(x2, norm_w, w_main, w_gates)


def _attn_kernel(sink_ref, q_ref, kv_ref, kvp_ref, o_ref):
    i = pl.program_id(1)
    W = WINDOW
    r = lax.broadcasted_iota(jnp.int32, (W, 2 * W), 0)
    c = lax.broadcasted_iota(jnp.int32, (W, 2 * W), 1)
    band = (c > r) & (c <= r + W)
    for j in range(ATT_TILE // W):
        qj = q_ref[0, j * W:(j + 1) * W, :]
        if j == 0:
            kprev = kvp_ref[0]
            valid = band & ((i > 0) | (c >= W))
        else:
            kprev = kv_ref[0, (j - 1) * W:j * W, :]
            valid = band
        kwin = jnp.concatenate([kprev, kv_ref[0, j * W:(j + 1) * W, :]], axis=0)
        outs = []
        for h in range(N_Q_HEADS):
            g = h // GQA_GROUP
            qh = qj[:, h * HEAD_DIM:(h + 1) * HEAD_DIM]
            kg = kwin[:, g * HEAD_DIM:(g + 1) * HEAD_DIM]
            vg = kwin[:, ATT_KV_WIDTH + g * HEAD_DIM:ATT_KV_WIDTH + (g + 1) * HEAD_DIM]
            s = lax.dot_general(qh, kg, (((1,), (1,)), ((), ())),
                                preferred_element_type=F32) * (HEAD_DIM ** -0.5)
            s = jnp.where(valid, s, NEG)
            sink = sink_ref[h]
            m = jnp.maximum(jnp.max(s, axis=-1, keepdims=True), sink)
            p = jnp.exp(s - m)
            l = jnp.sum(p, axis=-1, keepdims=True) + jnp.exp(sink - m)
            o = jnp.dot(p.astype(BF16), vg, preferred_element_type=F32)
            outs.append(o / l)
        o_ref[0, j * W:(j + 1) * W, :] = jnp.concatenate(outs, axis=-1).astype(BF16)


def _attn(sinks, q3, kv3):
    B, S, _ = q3.shape
    tq = ATT_TILE
    per = tq // WINDOW
    return pl.pallas_call(
        _attn_kernel,
        grid=(B, S // tq),
        in_specs=[pl.BlockSpec(memory_space=pltpu.SMEM),
                  pl.BlockSpec((1, tq, ATT_Q_WIDTH), lambda b, i: (b, i, 0)),
                  pl.BlockSpec((1, tq, 2 * ATT_KV_WIDTH), lambda b, i: (b, i, 0)),
                  pl.BlockSpec((1, WINDOW, 2 * ATT_KV_WIDTH),
                               lambda b, i: (b, jnp.maximum(i * per - 1, 0), 0))],
        out_specs=pl.BlockSpec((1, tq, ATT_Q_WIDTH), lambda b, i: (b, i, 0)),
        out_shape=jax.ShapeDtypeStruct((B, S, ATT_Q_WIDTH), BF16),
        compiler_params=pltpu.CompilerParams(dimension_semantics=("arbitrary", "arbitrary"),
                                             vmem_limit_bytes=VMEM_LIMIT),
        name="attn",
    )(sinks, q3, kv3, kv3)


def _mlstm_kernel(mqk_ref, mv_ref, mo_ref, gcol_ref, grow_ref, cw_ref, cb_ref, bcol_ref, brow_ref,
                  nw_ref, y_ref, cext_ref, m_ref, ubuf_ref):
    L = M_CHUNK
    cidx = pl.program_id(1)

    @pl.when(cidx == 0)
    def _():
        cext_ref[...] = jnp.zeros_like(cext_ref)
        m_ref[...] = jnp.zeros_like(m_ref)
        ubuf_ref[0:8, :] = jnp.zeros((8, 2 * M_QK_WIDTH), F32)

    ubuf_ref[8:L + 8, :] = mqk_ref[0].astype(F32)
    acc = cb_ref[...] + cw_ref[CONV_WIDTH - 1:CONV_WIDTH, :] * ubuf_ref[8:L + 8, :]
    for j in range(CONV_WIDTH - 1):
        off = 8 - (CONV_WIDTH - 1) + j
        acc = acc + cw_ref[j:j + 1, :] * ubuf_ref[off:off + L, :]
    ubuf_ref[0:8, :] = ubuf_ref[L:L + 8, :]
    qk = acc * _sigmoid(acc)
    q = qk[:, :M_QK_WIDTH].astype(BF16)
    k = qk[:, M_QK_WIDTH:] * (M_QK_DIM ** -0.5)

    gc = gcol_ref[0] + bcol_ref[...]
    gr = grow_ref[...] + brow_ref[...]
    ti = lax.broadcasted_iota(jnp.int32, (L, L), 0)
    si = lax.broadcasted_iota(jnp.int32, (L, L), 1)
    causal = si <= ti
    tril = jnp.where(causal, 1.0, 0.0).astype(F32)
    triu = jnp.where(ti <= si, 1.0, 0.0).astype(F32)
    b_c = jnp.dot(tril, _log_sigmoid(gc), preferred_element_type=F32,
                  precision=lax.Precision.HIGHEST)
    b_r = jnp.dot(_log_sigmoid(gr), triu, preferred_element_type=F32,
                  precision=lax.Precision.HIGHEST)

    ext = jnp.where(lax.broadcasted_iota(jnp.int32, (L, LANES), 1) == 0, 1.0, 0.0).astype(BF16)
    outs = []
    for h in range(M_HEADS):
        bc = b_c[:, M_HEADS + h:M_HEADS + h + 1]
        br = b_r[M_HEADS + h:M_HEADS + h + 1, :]
        ir = gr[h:h + 1, :]
        ic = gc[:, h:h + 1]
        m_prev = m_ref[h:h + 1, 0:1]
        dmat = jnp.where(causal, bc - br + ir, NEG)
        a = bc + m_prev
        mt = jnp.maximum(a, jnp.max(dmat, axis=-1, keepdims=True))
        wq = jnp.exp(dmat - mt)
        wa = jnp.exp(a - mt)
        qh = q[:, h * M_QK_DIM:(h + 1) * M_QK_DIM]
        kh = k[:, h * M_QK_DIM:(h + 1) * M_QK_DIM]
        sw = lax.dot_general(qh, kh.astype(BF16), (((1,), (1,)), ((), ())),
                             preferred_element_type=F32) * wq
        vext = jnp.concatenate([mv_ref[0, :, h * M_V_DIM:(h + 1) * M_V_DIM], ext], axis=-1)
        cprev = cext_ref[h]
        res = wa * jnp.dot(qh, cprev.astype(BF16), preferred_element_type=F32) \
            + jnp.dot(sw.astype(BF16), vext, preferred_element_type=F32)
        num = res[:, :M_V_DIM]
        den = res[:, M_V_DIM:M_V_DIM + 1]
        hb = num / jnp.maximum(jnp.abs(den), jnp.exp(-mt))
        m_new = mt[L - 1:L, :]
        bl = bc[L - 1:L, :]
        wc = jnp.exp(bl + m_prev - m_new)
        ws = jnp.exp(bl - bc + ic - m_new)
        kw = (kh * ws).astype(BF16)
        cext_ref[h] = wc * cprev + lax.dot_general(kw, vext, (((0,), (0,)), ((), ())),
                                                   preferred_element_type=F32)
        m_ref[h:h + 1, :] = jnp.broadcast_to(m_new, (1, LANES))
        hn = hb * lax.rsqrt(jnp.mean(hb * hb, axis=-1, keepdims=True) + EPS)
        hn = hn * nw_ref[:, h * M_V_DIM:(h + 1) * M_V_DIM]
        og = _sigmoid(mo_ref[0, :, h * M_V_DIM:(h + 1) * M_V_DIM].astype(F32))
        outs.append(og * hn)
    y_ref[0] = jnp.concatenate(outs, axis=-1).astype(BF16)


def _mlstm(mqk3, mv3, mo3, gcol3, grow, conv_w, conv_b, bias_col, bias_row, norm_w):
    B, S, _ = mqk3.shape
    L = M_CHUNK
    nc = S // L
    tok = lambda b, c: (b, c, 0)
    const = lambda b, c: (0, 0)
    return pl.pallas_call(
        _mlstm_kernel,
        grid=(B, nc),
        in_specs=[pl.BlockSpec((1, L, 2 * M_QK_WIDTH), tok),
                  pl.BlockSpec((1, L, M_V_WIDTH), tok),
                  pl.BlockSpec((1, L, M_V_WIDTH), tok),
                  pl.BlockSpec((1, L, LANES), tok),
                  pl.BlockSpec((8, L), lambda b, c: (0, b * nc + c)),
                  pl.BlockSpec((CONV_WIDTH, 2 * M_QK_WIDTH), const),
                  pl.BlockSpec((1, 2 * M_QK_WIDTH), const),
                  pl.BlockSpec((1, LANES), const),
                  pl.BlockSpec((8, 1), const),
                  pl.BlockSpec((1, M_V_WIDTH), const)],
        out_specs=pl.BlockSpec((1, L, M_V_WIDTH), tok),
        out_shape=jax.ShapeDtypeStruct((B, S, M_V_WIDTH), BF16),
        scratch_shapes=[pltpu.VMEM((M_HEADS, M_QK_DIM, 2 * M_V_DIM), F32),
                        pltpu.VMEM((8, LANES), F32),
                        pltpu.VMEM((L + 8, 2 * M_QK_WIDTH), F32)],
        compiler_params=pltpu.CompilerParams(dimension_semantics=("arbitrary", "arbitrary"),
                                             vmem_limit_bytes=VMEM_LIMIT),
        name="mlstm",
    )(mqk3, mv3, mo3, gcol3, grow, conv_w, conv_b, bias_col, bias_row, norm_w)


def _outproj_kernel(x_ref, ya_ref, ym_ref, gab_ref, wa_ref, wm_ref, wo_ref, nw_ref, wr_ref, br_ref,
                    x1_ref, h2_ref, meta_ref, code_ref, cnt_ref, cnt_scr):
    tm = TOK_TILE
    i = pl.program_id(0)

    @pl.when(i == 0)
    def _():
        cnt_scr[...] = jnp.zeros_like(cnt_scr)

    pa = jnp.dot(ya_ref[...], wa_ref[...], preferred_element_type=F32)
    pm = jnp.dot(ym_ref[...], wm_ref[...], preferred_element_type=F32)
    ga = gab_ref[:, :D_MODEL].astype(F32)
    gb = gab_ref[:, D_MODEL:].astype(F32)
    mix = _sigmoid(ga) * pa + _sigmoid(gb) * pm
    x1 = x_ref[...] + jnp.dot(mix.astype(BF16), wo_ref[...], preferred_element_type=F32)
    x1_ref[...] = x1
    h2 = _rms(x1, nw_ref[...])
    h2_ref[...] = h2

    logits = jnp.dot(h2, wr_ref[...], preferred_element_type=F32,
                     precision=lax.Precision.HIGHEST) + br_ref[...]
    lane = lax.broadcasted_iota(jnp.int32, (tm, LANES), 1)
    big = jnp.int32(LANES)

    def first_argmax(v):
        mx = jnp.max(v, axis=-1, keepdims=True)
        idx = jnp.min(jnp.where(v == mx, lane, big), axis=-1, keepdims=True)
        return mx, idx

    gl = jnp.where(lane < N_GROUPS, logits, NEG)
    gmax, gi = first_argmax(gl)
    gp = 1.0 / jnp.sum(jnp.exp(gl - gmax), axis=-1, keepdims=True)
    lo = N_GROUPS + gi * EXPERTS_PER_GROUP
    el = jnp.where((lane >= lo) & (lane < lo + EXPERTS_PER_GROUP), logits, NEG)
    v1, j1 = first_argmax(el)
    v2, j2 = first_argmax(jnp.where(lane == j1, NEG, el))
    t = jnp.exp(v2 - v1)
    w1 = gp / (1.0 + t)
    w2 = gp * t / (1.0 + t)

    hit1 = lane == j1
    hit2 = lane == j2
    onehot = jnp.where(hit1 | hit2, 1.0, 0.0)
    ri = lax.broadcasted_iota(jnp.int32, (tm, tm), 0)
    ci = lax.broadcasted_iota(jnp.int32, (tm, tm), 1)
    before = jnp.where(ci < ri, 1.0, 0.0).astype(BF16)
    rank = jnp.dot(before, onehot.astype(BF16), preferred_element_type=F32) + cnt_scr[0:1, :]
    r1 = jnp.sum(jnp.where(hit1, rank, 0.0), axis=-1, keepdims=True)
    r2 = jnp.sum(jnp.where(hit2, rank, 0.0), axis=-1, keepdims=True)
    cnt = cnt_scr[0:1, :] + jnp.sum(onehot, axis=0, keepdims=True)
    cnt_scr[...] = jnp.broadcast_to(cnt, cnt_scr.shape)
    cnt_ref[...] = jnp.broadcast_to(cnt, cnt_ref.shape)

    c1 = (j1 - N_GROUPS).astype(F32) * RANK_RADIX + r1
    c2 = (j2 - N_GROUPS).astype(F32) * RANK_RADIX + r2
    meta = jnp.where(lane == 0, c1, 0.0)
    meta = jnp.where(lane == 1, c2, meta)
    meta = jnp.where(lane == 4, w1, meta)
    meta = jnp.where(lane == 5, w2, meta)
    meta_ref[...] = meta
    code_ref[...] = meta.T[0:8, :].astype(jnp.int32)


def _outproj(x2, ya, ym, gab, wa, wm, wo, norm_w, w_rt, b_rt):
    T = x2.shape[0]
    tm = TOK_TILE
    row = lambda i: (i, 0)
    const = lambda i: (0, 0)
    return pl.pallas_call(
        _outproj_kernel,
        grid=(T // tm,),
        in_specs=[pl.BlockSpec((tm, D_MODEL), row),
                  pl.BlockSpec((tm, ATT_Q_WIDTH), row),
                  pl.BlockSpec((tm, M_V_WIDTH), row),
                  pl.BlockSpec((tm, 2 * D_MODEL), row),
                  pl.BlockSpec((ATT_Q_WIDTH, D_MODEL), const),
                  pl.BlockSpec((M_V_WIDTH, D_MODEL), const),
                  pl.BlockSpec((D_MODEL, D_MODEL), const),
                  pl.BlockSpec((1, D_MODEL), const),
                  pl.BlockSpec((D_MODEL, LANES), const),
                  pl.BlockSpec((1, LANES), const)],
        out_specs=[pl.BlockSpec((tm, D_MODEL), row),
                   pl.BlockSpec((tm, D_MODEL), row),
                   pl.BlockSpec((tm, LANES), row),
                   pl.BlockSpec((8, tm), lambda i: (0, i)),
                   pl.BlockSpec((8, LANES), const)],
        out_shape=[jax.ShapeDtypeStruct((T, D_MODEL), F32),
                   jax.ShapeDtypeStruct((T, D_MODEL), F32),
                   jax.ShapeDtypeStruct((T, LANES), F32),
                   jax.ShapeDtypeStruct((8, T), jnp.int32),
                   jax.ShapeDtypeStruct((8, LANES), F32)],
        scratch_shapes=[pltpu.VMEM((8, LANES), F32)],
        compiler_params=pltpu.CompilerParams(dimension_semantics=("arbitrary",),
                                             vmem_limit_bytes=VMEM_LIMIT),
        name="outproj",
    )(x2, ya, ym, gab, wa, wm, wo, norm_w, w_rt, b_rt)


def _slots_kernel(codes_ref, cnt_ref, tok_ref, pstart_ref, blk_e_ref, nused_ref):
    R = SLOT_BLOCK
    T = codes_ref.shape[0] // 2
    nblk = blk_e_ref.shape[0]

    def per_expert(e, carry):
        run, nassign = carry
        c = cnt_ref[e]
        end = run + ((c + R - 1) // R) * R
        pstart_ref[e] = run

        def set_blk(b, carry):
            blk_e_ref[b] = e
            return carry
        lax.fori_loop(run // R, end // R, set_blk, 0)

        def set_pad(s, carry):
            tok_ref[s] = 0
            return carry
        lax.fori_loop(run + c, end, set_pad, 0)
        return end, nassign + c

    total, nassign = lax.fori_loop(0, N_EXPERTS, per_expert, (0, 0))
    nused_ref[0] = total // R

    def tail_blk(b, carry):
        blk_e_ref[b] = N_EXPERTS - 1
        return carry
    lax.fori_loop(total // R, nblk, tail_blk, 0)

    def tail_slot(s, carry):
        tok_ref[s] = 0
        return carry
    lax.fori_loop(total, nblk * R, tail_slot, 0)

    def per_token_group(i, carry):
        for u in range(8):
            t = i * 8 + u
            c1 = codes_ref[t]
            c2 = codes_ref[T + t]
            tok_ref[pstart_ref[c1 >> RANK_BITS] + (c1 & (RANK_RADIX - 1))] = t
            tok_ref[pstart_ref[c2 >> RANK_BITS] + (c2 & (RANK_RADIX - 1))] = t
        return carry
    lax.fori_loop(0, nassign // 16, per_token_group, 0)


def _slots(codes, counts):
    T = codes.shape[0] // 2
    nblk = (2 * T) // SLOT_BLOCK + N_EXPERTS
    smem = pl.BlockSpec(memory_space=pltpu.SMEM)
    return pl.pallas_call(
        _slots_kernel,
        in_specs=[smem, smem],
        out_specs=[smem, smem, smem, smem],
        out_shape=[jax.ShapeDtypeStruct((nblk * SLOT_BLOCK,), jnp.int32),
                   jax.ShapeDtypeStruct((N_EXPERTS,), jnp.int32),
                   jax.ShapeDtypeStruct((nblk,), jnp.int32),
                   jax.ShapeDtypeStruct((1,), jnp.int32)],
        name="slots",
    )(codes, counts)


def _experts_kernel(blk_e_ref, nused_ref, tok_ref, h2_hbm, wg_ref, wu_ref, wd_ref, ys_ref,
                    xbuf0, xbuf1, wg_bf, wu_bf, wd_bf, sem0, sem1):
    j = pl.program_id(0)
    R = SLOT_BLOCK
    nused = nused_ref[0]
    even = j % 2 == 0
    valid = j < nused

    def row_copy(blk, r, buf, sem):
        tok = tok_ref[blk * R + r]
        return pltpu.make_async_copy(h2_hbm.at[pl.ds(tok, 1)], buf.at[pl.ds(r, 1)], sem)

    def wait_rows(buf, sem):
        pltpu.make_async_copy(h2_hbm.at[pl.ds(0, R)], buf, sem).wait()

    @pl.when(j == 0)
    def _():
        def issue(r, carry):
            row_copy(0, r, xbuf0, sem0).start()
            return carry
        lax.fori_loop(0, R, issue, 0, unroll=8)

    changed = jnp.logical_or(j == 0, blk_e_ref[j] != blk_e_ref[jnp.maximum(j - 1, 0)])

    @pl.when(jnp.logical_and(valid, changed))
    def _():
        wg_bf[...] = wg_ref[0].astype(BF16)
        wu_bf[...] = wu_ref[0].astype(BF16)
        wd_bf[...] = wd_ref[0].astype(BF16)

    def step(cur_buf, cur_sem, nxt_buf, nxt_sem):
        wait_rows(cur_buf, cur_sem)
        nxt = jnp.where(j + 1 < nused, j + 1, 0)
        for r in range(R):
            row_copy(nxt, r, nxt_buf, nxt_sem).start(priority=r % DMA_THREADS)
        xb = cur_buf[...].astype(BF16)
        g = jnp.dot(xb, wg_bf[...], preferred_element_type=F32)
        u = jnp.dot(xb, wu_bf[...], preferred_element_type=F32)
        act = (g * _sigmoid(g) * u).astype(BF16)
        ys_ref[...] = jnp.dot(act, wd_bf[...], preferred_element_type=F32)

    @pl.when(jnp.logical_and(valid, even))
    def _():
        step(xbuf0, sem0, xbuf1, sem1)

    @pl.when(jnp.logical_and(valid, jnp.logical_not(even)))
    def _():
        step(xbuf1, sem1, xbuf0, sem0)

    @pl.when(jnp.logical_and(j == nused, even))
    def _():
        wait_rows(xbuf0, sem0)

    @pl.when(jnp.logical_and(j == nused, jnp.logical_not(even)))
    def _():
        wait_rows(xbuf1, sem1)

    @pl.when(j >= nused)
    def _():
        ys_ref[...] = jnp.zeros_like(ys_ref)


def _experts(blk_e, nused, slot_tok, h2, w_gate, w_up, w_down):
    nblk = blk_e.shape[0]
    R = SLOT_BLOCK
    wspec = lambda shape: pl.BlockSpec((1,) + shape, lambda j, be, nu, st: (be[j], 0, 0))
    grid_spec = pltpu.PrefetchScalarGridSpec(
        num_scalar_prefetch=3,
        grid=(nblk,),
        in_specs=[pl.BlockSpec(memory_space=pl.ANY),
                  wspec((D_MODEL, D_EXPERT)),
                  wspec((D_MODEL, D_EXPERT)),
                  wspec((D_EXPERT, D_MODEL))],
        out_specs=pl.BlockSpec((R, D_MODEL), lambda j, be, nu, st: (j, 0)),
        scratch_shapes=[pltpu.VMEM((R, D_MODEL), F32),
                        pltpu.VMEM((R, D_MODEL), F32),
                        pltpu.VMEM((D_MODEL, D_EXPERT), BF16),
                        pltpu.VMEM((D_MODEL, D_EXPERT), BF16),
                        pltpu.VMEM((D_EXPERT, D_MODEL), BF16),
                        pltpu.SemaphoreType.DMA,
                        pltpu.SemaphoreType.DMA],
    )
    return pl.pallas_call(
        _experts_kernel,
        grid_spec=grid_spec,
        out_shape=jax.ShapeDtypeStruct((nblk * R, D_MODEL), F32),
        compiler_params=pltpu.CompilerParams(dimension_semantics=("arbitrary",),
                                             vmem_limit_bytes=VMEM_LIMIT),
        name="experts",
    )(blk_e, nused, slot_tok, h2, w_gate, w_up, w_down)


def _combine_kernel(codes_ref, pstart_ref, ys_hbm, x1_ref, meta_ref, nw_ref, o_ref, ybuf, sem):
    i = pl.program_id(0)
    tm = COMB_TILE
    T = codes_ref.shape[0] // 2

    def issue(r, carry):
        t = i * tm + r
        for k in range(2):
            c = codes_ref[k * T + t]
            slot = pstart_ref[c >> RANK_BITS] + (c & (RANK_RADIX - 1))
            pltpu.make_async_copy(ys_hbm.at[pl.ds(slot, 1)], ybuf.at[k, pl.ds(r, 1)], sem).start(
                priority=k % DMA_THREADS)
        return carry
    lax.fori_loop(0, tm, issue, 0, unroll=8)
    pltpu.make_async_copy(ys_hbm.at[pl.ds(0, tm)], ybuf.at[0], sem).wait()
    pltpu.make_async_copy(ys_hbm.at[pl.ds(0, tm)], ybuf.at[1], sem).wait()
    w1 = meta_ref[:, 4:5]
    w2 = meta_ref[:, 5:6]
    x2 = x1_ref[...] + (w1 * ybuf[0] + w2 * ybuf[1])
    o_ref[...] = _rms(x2, nw_ref[...])


def _combine(codes, pstart, ys, x1, meta, norm_w):
    T = x1.shape[0]
    tm = COMB_TILE
    row = lambda i, a, b: (i, 0)
    grid_spec = pltpu.PrefetchScalarGridSpec(
        num_scalar_prefetch=2,
        grid=(T // tm,),
        in_specs=[pl.BlockSpec(memory_space=pl.ANY),
                  pl.BlockSpec((tm, D_MODEL), row),
                  pl.BlockSpec((tm, LANES), row),
                  pl.BlockSpec((1, D_MODEL), lambda i, a, b: (0, 0))],
        out_specs=pl.BlockSpec((tm, D_MODEL), row),
        scratch_shapes=[pltpu.VMEM((2, tm, D_MODEL), F32),
                        pltpu.SemaphoreType.DMA],
    )
    return pl.pallas_call(
        _combine_kernel,
        grid_spec=grid_spec,
        out_shape=jax.ShapeDtypeStruct((T, D_MODEL), F32),
        compiler_params=pltpu.CompilerParams(dimension_semantics=("arbitrary",),
                                             vmem_limit_bytes=VMEM_LIMIT),
        name="combine",
    )(codes, pstart, ys, x1, meta, norm_w)


def _layer(x, norm_mix_w, w_in, conv_w, conv_b, b_igate, b_fgate, attn_sinks, mlstm_norm_w,
           w_attn_o, w_mlstm_o, w_out, norm_ffn_w, w_group, b_group, w_router, b_router,
           w_gate, w_up, w_down, out_norm_w):
    B, S, D = x.shape
    T = B * S
    x2 = x.reshape(T, D)

    w_main = jnp.concatenate([w_in[:, :_O_MI], w_in[:, _O_GA:]], axis=1).astype(BF16)
    w_gates = jnp.pad(w_in[:, _O_MI:_O_GA], ((0, 0), (0, LANES - 2 * M_HEADS))).astype(BF16)
    q, kv, mqk, mv, mo, gcol, grow, gab = _inproj(x2, norm_mix_w.reshape(1, D), w_main, w_gates)

    ya = _attn(attn_sinks.astype(F32), q.reshape(B, S, -1), kv.reshape(B, S, -1))

    bias = jnp.concatenate([b_igate, b_fgate]).astype(F32)
    bias_col = jnp.pad(bias, (0, LANES - 2 * M_HEADS)).reshape(1, LANES)
    ym = _mlstm(mqk.reshape(B, S, -1), mv.reshape(B, S, -1), mo.reshape(B, S, -1),
                gcol.reshape(B, S, LANES), grow, conv_w.astype(F32), conv_b.reshape(1, -1).astype(F32),
                bias_col, bias.reshape(2 * M_HEADS, 1), mlstm_norm_w.reshape(1, -1).astype(F32))

    w_rt = jnp.pad(jnp.concatenate([w_group, w_router], axis=1),
                   ((0, 0), (0, LANES - N_GROUPS - N_EXPERTS))).astype(F32)
    b_rt = jnp.pad(jnp.concatenate([b_group, b_router]), (0, LANES - N_GROUPS - N_EXPERTS)).reshape(1, LANES)
    x1, h2, meta, code_rows, cnt = _outproj(x2, ya.reshape(T, -1), ym.reshape(T, -1), gab,
                                            w_attn_o.astype(BF16), w_mlstm_o.astype(BF16),
                                            w_out.astype(BF16), norm_ffn_w.reshape(1, D), w_rt,
                                            b_rt.astype(F32))

    codes = code_rows[0:2].reshape(2 * T)
    counts = cnt[0, N_GROUPS:N_GROUPS + N_EXPERTS].astype(jnp.int32)
    slot_tok, pstart, blk_e, nused = _slots(codes, counts)

    ys = _experts(blk_e, nused, slot_tok, h2, w_gate, w_up, w_down)
    out = _combine(codes, pstart, ys, x1, meta, out_norm_w.reshape(1, D))
    return out.reshape(B, S, D)


def kernel(x, norm_mix_w, w_in, conv_w, conv_b, b_igate, b_fgate, attn_sinks, mlstm_norm_w, w_attn_o,
           w_mlstm_o, w_out, norm_ffn_w, w_group, b_group, w_router, b_router, w_gate, w_up, w_down,
           norm_final_w):
    depth = w_in.shape[0]
    assert depth == 1, "final RMSNorm is fused into the last layer's combine kernel"
    return _layer(x, norm_mix_w[0], w_in[0], conv_w[0], conv_b[0], b_igate[0], b_fgate[0],
                  attn_sinks[0], mlstm_norm_w[0], w_attn_o[0], w_mlstm_o[0], w_out[0], norm_ffn_w[0],
                  w_group[0], b_group[0], w_router[0], b_router[0], w_gate[0], w_up[0], w_down[0],
                  norm_final_w)
```

```python
import functools

import jax
import jax.numpy as jnp
from jax import lax
from jax.experimental import pallas as pl
from jax.experimental.pallas import tpu as pltpu
from jax.experimental.pallas import tpu_sc as plsc

F32 = jnp.float32
BF16 = jnp.bfloat16

D_MODEL = 1024
N_Q_HEADS = 8
N_KV_HEADS = 2
HEAD_DIM = 64
WINDOW = 128
GQA_GROUP = N_Q_HEADS // N_KV_HEADS
M_HEADS = 4
M_QK_DIM = 64
M_V_DIM = 128
CONV_WIDTH = 4
N_GROUPS = 4
EXPERTS_PER_GROUP = 8
N_EXPERTS = N_GROUPS * EXPERTS_PER_GROUP
D_EXPERT = 512
EPS = 1e-6

ATT_Q_WIDTH = N_Q_HEADS * HEAD_DIM
ATT_KV_WIDTH = N_KV_HEADS * HEAD_DIM
M_QK_WIDTH = M_HEADS * M_QK_DIM
M_V_WIDTH = M_HEADS * M_V_DIM

LANES = 128
NEG = -1e30
VMEM_LIMIT = 56 * 1024 * 1024

TOK_TILE = 512
ATT_TILE = 256
M_CHUNK = 128
SLOT_BLOCK = 256
COMB_TILE = 256
RANK_RADIX = 65536
RANK_BITS = 16
SC_WINDOW = 128
SC_SPLIT = 4
DMA_THREADS = 2

_O_AQ = 0
_O_AK = _O_AQ + ATT_Q_WIDTH
_O_AV = _O_AK + ATT_KV_WIDTH
_O_MQ = _O_AV + ATT_KV_WIDTH
_O_MK = _O_MQ + M_QK_WIDTH
_O_MV = _O_MK + M_QK_WIDTH
_O_MO = _O_MV + M_V_WIDTH
_O_MI = _O_MO + M_V_WIDTH
_O_MF = _O_MI + M_HEADS
_O_GA = _O_MF + M_HEADS
_O_GB = _O_GA + D_MODEL
_O_END = _O_GB + D_MODEL


def _rms(x, w):
    return x * lax.rsqrt(jnp.mean(x * x, axis=-1, keepdims=True) + EPS) * w


def _sigmoid(x):
    return 1.0 / (1.0 + jnp.exp(-x))


def _log_sigmoid(x):
    return jnp.minimum(x, 0.0) - jnp.log1p(jnp.exp(-jnp.abs(x)))


def _inproj_kernel(x_ref, nw_ref, w_ref, wg_ref,
                   q_ref, kv_ref, mqk_ref, mv_ref, mo_ref, gcol_ref, grow_ref, gab_ref):
    h = _rms(x_ref[...], nw_ref[...]).astype(BF16)

    def proj(lo, hi):
        return jnp.dot(h, w_ref[:, lo:hi], preferred_element_type=F32)

    q_ref[...] = proj(_O_AQ, _O_AK).astype(BF16)
    kv_ref[...] = proj(_O_AK, _O_MQ).astype(BF16)
    mqk_ref[...] = proj(_O_MQ, _O_MV).astype(BF16)
    mv_ref[...] = proj(_O_MV, _O_MO).astype(BF16)
    mo_ref[...] = proj(_O_MO, _O_MI).astype(BF16)
    gab_ref[...] = proj(_O_MI, _O_MI + 2 * D_MODEL).astype(BF16)
    g = jnp.dot(h, wg_ref[...], preferred_element_type=F32)
    gcol_ref[...] = g
    grow_ref[...] = g.T[0:8, :]


def _inproj(x2, norm_w, w_main, w_gates):
    T = x2.shape[0]
    tm = TOK_TILE
    nmain = w_main.shape[1]
    row = lambda i: (i, 0)
    const = lambda i: (0, 0)
    return pl.pallas_call(
        _inproj_kernel,
        grid=(T // tm,),
        in_specs=[pl.BlockSpec((tm, D_MODEL), row),
                  pl.BlockSpec((1, D_MODEL), const),
                  pl.BlockSpec((D_MODEL, nmain), const),
                  pl.BlockSpec((D_MODEL, LANES), const)],
        out_specs=[pl.BlockSpec((tm, ATT_Q_WIDTH), row),
                   pl.BlockSpec((tm, 2 * ATT_KV_WIDTH), row),
                   pl.BlockSpec((tm, 2 * M_QK_WIDTH), row),
                   pl.BlockSpec((tm, M_V_WIDTH), row),
                   pl.BlockSpec((tm, M_V_WIDTH), row),
                   pl.BlockSpec((tm, LANES), row),
                   pl.BlockSpec((8, tm), lambda i: (0, i)),
                   pl.BlockSpec((tm, 2 * D_MODEL), row)],
        out_shape=[jax.ShapeDtypeStruct((T, ATT_Q_WIDTH), BF16),
                   jax.ShapeDtypeStruct((T, 2 * ATT_KV_WIDTH), BF16),
                   jax.ShapeDtypeStruct((T, 2 * M_QK_WIDTH), BF16),
                   jax.ShapeDtypeStruct((T, M_V_WIDTH), BF16),
                   jax.ShapeDtypeStruct((T, M_V_WIDTH), BF16),
                   jax.ShapeDtypeStruct((T, LANES), F32),
                   jax.ShapeDtypeStruct((8, T), F32),
                   jax.ShapeDtypeStruct((T, 2 * D_MODEL), BF16)],
        compiler_params=pltpu.CompilerParams(dimension_semantics=("arbitrary",),
                                             vmem_limit_bytes=VMEM_LIMIT),
        name="inproj",
    )(x2, norm_w, w_main, w_gates)


def _attn_kernel(sink_ref, q_ref, kv_ref, kvp_ref, o_ref):
    i = pl.program_id(1)
    W = WINDOW
    r = lax.broadcasted_iota(jnp.int32, (W, 2 * W), 0)
    c = lax.broadcasted_iota(jnp.int32, (W, 2 * W), 1)
    band = (c > r) & (c <= r + W)
    for j in range(ATT_TILE // W):
        qj = q_ref[0, j * W:(j + 1) * W, :]
        if j == 0:
            kprev = kvp_ref[0]
            valid = band & ((i > 0) | (c >= W))
        else:
            kprev = kv_ref[0, (j - 1) * W:j * W, :]
            valid = band
        kwin = jnp.concatenate([kprev, kv_ref[0, j * W:(j + 1) * W, :]], axis=0)
        outs = []
        for h in range(N_Q_HEADS):
            g = h // GQA_GROUP
            qh = qj[:, h * HEAD_DIM:(h + 1) * HEAD_DIM]
            kg = kwin[:, g * HEAD_DIM:(g + 1) * HEAD_DIM]
            vg = kwin[:, ATT_KV_WIDTH + g * HEAD_DIM:ATT_KV_WIDTH + (g + 1) * HEAD_DIM]
            s = lax.dot_general(qh, kg, (((1,), (1,)), ((), ())),
                                preferred_element_type=F32) * (HEAD_DIM ** -0.5)
            s = jnp.where(valid, s, NEG)
            sink = sink_ref[h]
            m = jnp.maximum(jnp.max(s, axis=-1, keepdims=True), sink)
            p = jnp.exp(s - m)
            l = jnp.sum(p, axis=-1, keepdims=True) + jnp.exp(sink - m)
            o = jnp.dot(p.astype(BF16), vg, preferred_element_type=F32)
            outs.append(o / l)
        o_ref[0, j * W:(j + 1) * W, :] = jnp.concatenate(outs, axis=-1).astype(BF16)


def _attn(sinks, q3, kv3):
    B, S, _ = q3.shape
    tq = ATT_TILE
    per = tq // WINDOW
    return pl.pallas_call(
        _attn_kernel,
        grid=(B, S // tq),
        in_specs=[pl.BlockSpec(memory_space=pltpu.SMEM),
                  pl.BlockSpec((1, tq, ATT_Q_WIDTH), lambda b, i: (b, i, 0)),
                  pl.BlockSpec((1, tq, 2 * ATT_KV_WIDTH), lambda b, i: (b, i, 0)),
                  pl.BlockSpec((1, WINDOW, 2 * ATT_KV_WIDTH),
                               lambda b, i: (b, jnp.maximum(i * per - 1, 0), 0))],
        out_specs=pl.BlockSpec((1, tq, ATT_Q_WIDTH), lambda b, i: (b, i, 0)),
        out_shape=jax.ShapeDtypeStruct((B, S, ATT_Q_WIDTH), BF16),
        compiler_params=pltpu.CompilerParams(dimension_semantics=("arbitrary", "arbitrary"),
                                             vmem_limit_bytes=VMEM_LIMIT),
        name="attn",
    )(sinks, q3, kv3, kv3)


def _mlstm_kernel(mqk_ref, mv_ref, mo_ref, gcol_ref, grow_ref, cw_ref, cb_ref, bcol_ref, brow_ref,
                  nw_ref, y_ref, cext_ref, m_ref, ubuf_ref):
    L = M_CHUNK
    cidx = pl.program_id(1)

    @pl.when(cidx == 0)
    def _():
        cext_ref[...] = jnp.zeros_like(cext_ref)
        m_ref[...] = jnp.zeros_like(m_ref)
        ubuf_ref[0:8, :] = jnp.zeros((8, 2 * M_QK_WIDTH), F32)

    ubuf_ref[8:L + 8, :] = mqk_ref[0].astype(F32)
    acc = cb_ref[...] + cw_ref[CONV_WIDTH - 1:CONV_WIDTH, :] * ubuf_ref[8:L + 8, :]
    for j in range(CONV_WIDTH - 1):
        off = 8 - (CONV_WIDTH - 1) + j
        acc = acc + cw_ref[j:j + 1, :] * ubuf_ref[off:off + L, :]
    ubuf_ref[0:8, :] = ubuf_ref[L:L + 8, :]
    qk = acc * _sigmoid(acc)
    q = qk[:, :M_QK_WIDTH].astype(BF16)
    k = qk[:, M_QK_WIDTH:] * (M_QK_DIM ** -0.5)

    gc = gcol_ref[0] + bcol_ref[...]
    gr = grow_ref[...] + brow_ref[...]
    ti = lax.broadcasted_iota(jnp.int32, (L, L), 0)
    si = lax.broadcasted_iota(jnp.int32, (L, L), 1)
    causal = si <= ti
    tril = jnp.where(causal, 1.0, 0.0).astype(F32)
    triu = jnp.where(ti <= si, 1.0, 0.0).astype(F32)
    b_c = jnp.dot(tril, _log_sigmoid(gc), preferred_element_type=F32,
                  precision=lax.Precision.HIGHEST)
    b_r = jnp.dot(_log_sigmoid(gr), triu, preferred_element_type=F32,
                  precision=lax.Precision.HIGHEST)

    ext = jnp.where(lax.broadcasted_iota(jnp.int32, (L, LANES), 1) == 0, 1.0, 0.0).astype(BF16)
    outs = []
    for h in range(M_HEADS):
        bc = b_c[:, M_HEADS + h:M_HEADS + h + 1]
        br = b_r[M_HEADS + h:M_HEADS + h + 1, :]
        ir = gr[h:h + 1, :]
        ic = gc[:, h:h + 1]
        m_prev = m_ref[h:h + 1, 0:1]
        dmat = jnp.where(causal, bc - br + ir, NEG)
        a = bc + m_prev
        mt = jnp.maximum(a, jnp.max(dmat, axis=-1, keepdims=True))
        wq = jnp.exp(dmat - mt)
        wa = jnp.exp(a - mt)
        qh = q[:, h * M_QK_DIM:(h + 1) * M_QK_DIM]
        kh = k[:, h * M_QK_DIM:(h + 1) * M_QK_DIM]
        sw = lax.dot_general(qh, kh.astype(BF16), (((1,), (1,)), ((), ())),
                             preferred_element_type=F32) * wq
        vext = jnp.concatenate([mv_ref[0, :, h * M_V_DIM:(h + 1) * M_V_DIM], ext], axis=-1)
        cprev = cext_ref[h]
        res = wa * jnp.dot(qh, cprev.astype(BF16), preferred_element_type=F32) \
            + jnp.dot(sw.astype(BF16), vext, preferred_element_type=F32)
        num = res[:, :M_V_DIM]
        den = res[:, M_V_DIM:M_V_DIM + 1]
        hb = num / jnp.maximum(jnp.abs(den), jnp.exp(-mt))
        m_new = mt[L - 1:L, :]
        bl = bc[L - 1:L, :]
        wc = jnp.exp(bl + m_prev - m_new)
        ws = jnp.exp(bl - bc + ic - m_new)
        kw = (kh * ws).astype(BF16)
        cext_ref[h] = wc * cprev + lax.dot_general(kw, vext, (((0,), (0,)), ((), ())),
                                                   preferred_element_type=F32)
        m_ref[h:h + 1, :] = jnp.broadcast_to(m_new, (1, LANES))
        hn = hb * lax.rsqrt(jnp.mean(hb * hb, axis=-1, keepdims=True) + EPS)
        hn = hn * nw_ref[:, h * M_V_DIM:(h + 1) * M_V_DIM]
        og = _sigmoid(mo_ref[0, :, h * M_V_DIM:(h + 1) * M_V_DIM].astype(F32))
        outs.append(og * hn)
    y_ref[0] = jnp.concatenate(outs, axis=-1).astype(BF16)


def _mlstm(mqk3, mv3, mo3, gcol3, grow, conv_w, conv_b, bias_col, bias_row, norm_w):
    B, S, _ = mqk3.shape
    L = M_CHUNK
    nc = S // L
    tok = lambda b, c: (b, c, 0)
    const = lambda b, c: (0, 0)
    return pl.pallas_call(
        _mlstm_kernel,
        grid=(B, nc),
        in_specs=[pl.BlockSpec((1, L, 2 * M_QK_WIDTH), tok),
                  pl.BlockSpec((1, L, M_V_WIDTH), tok),
                  pl.BlockSpec((1, L, M_V_WIDTH), tok),
                  pl.BlockSpec((1, L, LANES), tok),
                  pl.BlockSpec((8, L), lambda b, c: (0, b * nc + c)),
                  pl.BlockSpec((CONV_WIDTH, 2 * M_QK_WIDTH), const),
                  pl.BlockSpec((1, 2 * M_QK_WIDTH), const),
                  pl.BlockSpec((1, LANES), const),
                  pl.BlockSpec((8, 1), const),
                  pl.BlockSpec((1, M_V_WIDTH), const)],
        out_specs=pl.BlockSpec((1, L, M_V_WIDTH), tok),
        out_shape=jax.ShapeDtypeStruct((B, S, M_V_WIDTH), BF16),
        scratch_shapes=[pltpu.VMEM((M_HEADS, M_QK_DIM, 2 * M_V_DIM), F32),
                        pltpu.VMEM((8, LANES), F32),
                        pltpu.VMEM((L + 8, 2 * M_QK_WIDTH), F32)],
        compiler_params=pltpu.CompilerParams(dimension_semantics=("arbitrary", "arbitrary"),
                                             vmem_limit_bytes=VMEM_LIMIT),
        name="mlstm",
    )(mqk3, mv3, mo3, gcol3, grow, conv_w, conv_b, bias_col, bias_row, norm_w)


def _outproj_kernel(x_ref, ya_ref, ym_ref, gab_ref, wa_ref, wm_ref, wo_ref, nw_ref, wr_ref, br_ref,
                    x1_ref, h2_ref, meta_ref, code_ref, cnt_ref, cnt_scr):
    tm = TOK_TILE
    i = pl.program_id(0)

    @pl.when(i == 0)
    def _():
        cnt_scr[...] = jnp.zeros_like(cnt_scr)

    pa = jnp.dot(ya_ref[...], wa_ref[...], preferred_element_type=F32)
    pm = jnp.dot(ym_ref[...], wm_ref[...], preferred_element_type=F32)
    ga = gab_ref[:, :D_MODEL].astype(F32)
    gb = gab_ref[:, D_MODEL:].astype(F32)
    mix = _sigmoid(ga) * pa + _sigmoid(gb) * pm
    x1 = x_ref[...] + jnp.dot(mix.astype(BF16), wo_ref[...], preferred_element_type=F32)
    x1_ref[...] = x1
    h2 = _rms(x1, nw_ref[...])
    h2_ref[...] = h2

    logits = jnp.dot(h2, wr_ref[...], preferred_element_type=F32,
                     precision=lax.Precision.HIGHEST) + br_ref[...]
    lane = lax.broadcasted_iota(jnp.int32, (tm, LANES), 1)
    big = jnp.int32(LANES)

    def first_argmax(v):
        mx = jnp.max(v, axis=-1, keepdims=True)
        idx = jnp.min(jnp.where(v == mx, lane, big), axis=-1, keepdims=True)
        return mx, idx

    gl = jnp.where(lane < N_GROUPS, logits, NEG)
    gmax, gi = first_argmax(gl)
    gp = 1.0 / jnp.sum(jnp.exp(gl - gmax), axis=-1, keepdims=True)
    lo = N_GROUPS + gi * EXPERTS_PER_GROUP
    el = jnp.where((lane >= lo) & (lane < lo + EXPERTS_PER_GROUP), logits, NEG)
    v1, j1 = first_argmax(el)
    v2, j2 = first_argmax(jnp.where(lane == j1, NEG, el))
    t = jnp.exp(v2 - v1)
    w1 = gp / (1.0 + t)
    w2 = gp * t / (1.0 + t)

    hit1 = lane == j1
    hit2 = lane == j2
    onehot = jnp.where(hit1 | hit2, 1.0, 0.0)
    ri = lax.broadcasted_iota(jnp.int32, (tm, tm), 0)
    ci = lax.broadcasted_iota(jnp.int32, (tm, tm), 1)
    before = jnp.where(ci < ri, 1.0, 0.0).astype(BF16)
    rank = jnp.dot(before, onehot.astype(BF16), preferred_element_type=F32) + cnt_scr[0:1, :]
    r1 = jnp.sum(jnp.where(hit1, rank, 0.0), axis=-1, keepdims=True)
    r2 = jnp.sum(jnp.where(hit2, rank, 0.0), axis=-1, keepdims=True)
    cnt = cnt_scr[0:1, :] + jnp.sum(onehot, axis=0, keepdims=True)
    cnt_scr[...] = jnp.broadcast_to(cnt, cnt_scr.shape)
    cnt_ref[...] = jnp.broadcast_to(cnt, cnt_ref.shape)

    c1 = (j1 - N_GROUPS).astype(F32) * RANK_RADIX + r1
    c2 = (j2 - N_GROUPS).astype(F32) * RANK_RADIX + r2
    meta = jnp.where(lane == 0, c1, 0.0)
    meta = jnp.where(lane == 1, c2, meta)
    meta = jnp.where(lane == 4, w1, meta)
    meta = jnp.where(lane == 5, w2, meta)
    meta_ref[...] = meta
    code_ref[...] = meta.T[0:8, :].astype(jnp.int32)


def _outproj(x2, ya, ym, gab, wa, wm, wo, norm_w, w_rt, b_rt):
    T = x2.shape[0]
    tm = TOK_TILE
    row = lambda i: (i, 0)
    const = lambda i: (0, 0)
    return pl.pallas_call(
        _outproj_kernel,
        grid=(T // tm,),
        in_specs=[pl.BlockSpec((tm, D_MODEL), row),
                  pl.BlockSpec((tm, ATT_Q_WIDTH), row),
                  pl.BlockSpec((tm, M_V_WIDTH), row),
                  pl.BlockSpec((tm, 2 * D_MODEL), row),
                  pl.BlockSpec((ATT_Q_WIDTH, D_MODEL), const),
                  pl.BlockSpec((M_V_WIDTH, D_MODEL), const),
                  pl.BlockSpec((D_MODEL, D_MODEL), const),
                  pl.BlockSpec((1, D_MODEL), const),
                  pl.BlockSpec((D_MODEL, LANES), const),
                  pl.BlockSpec((1, LANES), const)],
        out_specs=[pl.BlockSpec((tm, D_MODEL), row),
                   pl.BlockSpec((tm, D_MODEL), row),
                   pl.BlockSpec((tm, LANES), row),
                   pl.BlockSpec((8, tm), lambda i: (0, i)),
                   pl.BlockSpec((8, LANES), const)],
        out_shape=[jax.ShapeDtypeStruct((T, D_MODEL), F32),
                   jax.ShapeDtypeStruct((T, D_MODEL), F32),
                   jax.ShapeDtypeStruct((T, LANES), F32),
                   jax.ShapeDtypeStruct((8, T), jnp.int32),
                   jax.ShapeDtypeStruct((8, LANES), F32)],
        scratch_shapes=[pltpu.VMEM((8, LANES), F32)],
        compiler_params=pltpu.CompilerParams(dimension_semantics=("arbitrary",),
                                             vmem_limit_bytes=VMEM_LIMIT),
        name="outproj",
    )(x2, ya, ym, gab, wa, wm, wo, norm_w, w_rt, b_rt)


def _slots_kernel(codes_ref, cnt_ref, tok_ref, pstart_ref, blk_e_ref, nused_ref):
    R = SLOT_BLOCK
    T = codes_ref.shape[0] // 2
    nblk = blk_e_ref.shape[0]

    def per_expert(e, carry):
        run, nassign = carry
        c = cnt_ref[e]
        end = run + ((c + R - 1) // R) * R
        pstart_ref[e] = run

        def set_blk(b, carry):
            blk_e_ref[b] = e
            return carry
        lax.fori_loop(run // R, end // R, set_blk, 0)

        def set_pad(s, carry):
            tok_ref[s] = 0
            return carry
        lax.fori_loop(run + c, end, set_pad, 0)
        return end, nassign + c

    total, nassign = lax.fori_loop(0, N_EXPERTS, per_expert, (0, 0))
    nused_ref[0] = total // R

    def tail_blk(b, carry):
        blk_e_ref[b] = N_EXPERTS - 1
        return carry
    lax.fori_loop(total // R, nblk, tail_blk, 0)

    def tail_slot(s, carry):
        tok_ref[s] = 0
        return carry
    lax.fori_loop(total, nblk * R, tail_slot, 0)

    def per_token_group(i, carry):
        for u in range(8):
            t = i * 8 + u
            c1 = codes_ref[t]
            c2 = codes_ref[T + t]
            tok_ref[pstart_ref[c1 >> RANK_BITS] + (c1 & (RANK_RADIX - 1))] = t
            tok_ref[pstart_ref[c2 >> RANK_BITS] + (c2 & (RANK_RADIX - 1))] = t
        return carry
    lax.fori_loop(0, nassign // 16, per_token_group, 0)


def _slots(codes, counts):
    T = codes.shape[0] // 2
    nblk = (2 * T) // SLOT_BLOCK + N_EXPERTS
    smem = pl.BlockSpec(memory_space=pltpu.SMEM)
    return pl.pallas_call(
        _slots_kernel,
        in_specs=[smem, smem],
        out_specs=[smem, smem, smem, smem],
        out_shape=[jax.ShapeDtypeStruct((nblk * SLOT_BLOCK,), jnp.int32),
                   jax.ShapeDtypeStruct((N_EXPERTS,), jnp.int32),
                   jax.ShapeDtypeStruct((nblk,), jnp.int32),
                   jax.ShapeDtypeStruct((1,), jnp.int32)],
        name="slots",
    )(codes, counts)


def _experts_kernel(blk_e_ref, nused_ref, tok_ref, h2_hbm, wg_ref, wu_ref, wd_ref, ys_ref,
                    xbuf0, xbuf1, wg_bf, wu_bf, wd_bf, sem0, sem1):
    j = pl.program_id(0)
    R = SLOT_BLOCK
    nused = nused_ref[0]
    even = j % 2 == 0
    valid = j < nused

    def row_copy(blk, r, buf, sem):
        tok = tok_ref[blk * R + r]
        return pltpu.make_async_copy(h2_hbm.at[pl.ds(tok, 1)], buf.at[pl.ds(r, 1)], sem)

    def wait_rows(buf, sem):
        pltpu.make_async_copy(h2_hbm.at[pl.ds(0, R)], buf, sem).wait()

    @pl.when(j == 0)
    def _():
        def issue(r, carry):
            row_copy(0, r, xbuf0, sem0).start()
            return carry
        lax.fori_loop(0, R, issue, 0, unroll=8)

    changed = jnp.logical_or(j == 0, blk_e_ref[j] != blk_e_ref[jnp.maximum(j - 1, 0)])

    @pl.when(jnp.logical_and(valid, changed))
    def _():
        wg_bf[...] = wg_ref[0].astype(BF16)
        wu_bf[...] = wu_ref[0].astype(BF16)
        wd_bf[...] = wd_ref[0].astype(BF16)

    def step(cur_buf, cur_sem, nxt_buf, nxt_sem):
        wait_rows(cur_buf, cur_sem)
        nxt = jnp.where(j + 1 < nused, j + 1, 0)
        for r in range(R):
            row_copy(nxt, r, nxt_buf, nxt_sem).start(priority=r % DMA_THREADS)
        xb = cur_buf[...].astype(BF16)
        g = jnp.dot(xb, wg_bf[...], preferred_element_type=F32)
        u = jnp.dot(xb, wu_bf[...], preferred_element_type=F32)
        act = (g * _sigmoid(g) * u).astype(BF16)
        ys_ref[...] = jnp.dot(act, wd_bf[...], preferred_element_type=F32)

    @pl.when(jnp.logical_and(valid, even))
    def _():
        step(xbuf0, sem0, xbuf1, sem1)

    @pl.when(jnp.logical_and(valid, jnp.logical_not(even)))
    def _():
        step(xbuf1, sem1, xbuf0, sem0)

    @pl.when(jnp.logical_and(j == nused, even))
    def _():
        wait_rows(xbuf0, sem0)

    @pl.when(jnp.logical_and(j == nused, jnp.logical_not(even)))
    def _():
        wait_rows(xbuf1, sem1)

    @pl.when(j >= nused)
    def _():
        ys_ref[...] = jnp.zeros_like(ys_ref)


def _experts(blk_e, nused, slot_tok, h2, w_gate, w_up, w_down):
    nblk = blk_e.shape[0]
    R = SLOT_BLOCK
    wspec = lambda shape: pl.BlockSpec((1,) + shape, lambda j, be, nu, st: (be[j], 0, 0))
    grid_spec = pltpu.PrefetchScalarGridSpec(
        num_scalar_prefetch=3,
        grid=(nblk,),
        in_specs=[pl.BlockSpec(memory_space=pl.ANY),
                  wspec((D_MODEL, D_EXPERT)),
                  wspec((D_MODEL, D_EXPERT)),
                  wspec((D_EXPERT, D_MODEL))],
        out_specs=pl.BlockSpec((R, D_MODEL), lambda j, be, nu, st: (j, 0)),
        scratch_shapes=[pltpu.VMEM((R, D_MODEL), F32),
                        pltpu.VMEM((R, D_MODEL), F32),
                        pltpu.VMEM((D_MODEL, D_EXPERT), BF16),
                        pltpu.VMEM((D_MODEL, D_EXPERT), BF16),
                        pltpu.VMEM((D_EXPERT, D_MODEL), BF16),
                        pltpu.SemaphoreType.DMA,
                        pltpu.SemaphoreType.DMA],
    )
    return pl.pallas_call(
        _experts_kernel,
        grid_spec=grid_spec,
        out_shape=jax.ShapeDtypeStruct((nblk * R, D_MODEL), F32),
        compiler_params=pltpu.CompilerParams(dimension_semantics=("arbitrary",),
                                             vmem_limit_bytes=VMEM_LIMIT),
        name="experts",
    )(blk_e, nused, slot_tok, h2, w_gate, w_up, w_down)


def _dest_kernel(pstart_ref, code_ref, dest_ref):
    c = code_ref[...]
    e = c >> RANK_BITS
    base = jnp.zeros_like(c)
    for k in range(N_EXPERTS):
        base = jnp.where(e == k, pstart_ref[k], base)
    dest_ref[...] = base + (c & (RANK_RADIX - 1))


def _dests(pstart, code_rows):
    T = code_rows.shape[1]
    tc = 4096
    return pl.pallas_call(
        _dest_kernel,
        grid=(T // tc,),
        in_specs=[pl.BlockSpec(memory_space=pltpu.SMEM),
                  pl.BlockSpec((8, tc), lambda i: (0, i))],
        out_specs=pl.BlockSpec((8, tc), lambda i: (0, i)),
        out_shape=jax.ShapeDtypeStruct((8, T), jnp.int32),
        name="dests",
    )(pstart, code_rows)


def _sc_gather_rows(table, idx):
    n_rows, full_width = table.shape
    width = full_width // SC_SPLIT
    table = table.reshape(n_rows * SC_SPLIT, width)
    idx = (idx[:, None] * SC_SPLIT + jnp.arange(SC_SPLIT, dtype=jnp.int32)[None, :]).reshape(-1)
    M = idx.shape[0]
    mesh = plsc.VectorSubcoreMesh(core_axis_name="c", subcore_axis_name="s")

    @functools.partial(pl.kernel, mesh=mesh,
                       out_type=jax.ShapeDtypeStruct((M, width), table.dtype), name="sc_gather")
    def gather(table_hbm, idx_hbm, out_hbm):
        def body(idx_vmem, out_vmem):
            pltpu.sync_copy(table_hbm.at[idx_vmem.at[0]], out_vmem)

        pltpu.emit_pipeline(
            body,
            grid=(M // SC_WINDOW,),
            in_specs=[pl.BlockSpec((1, SC_WINDOW), lambda i: (0, i))],
            out_specs=[pl.BlockSpec((SC_WINDOW, width), lambda i: (i, 0))],
            core_axis_name=("c", "s"),
            dimension_semantics=(pltpu.PARALLEL,),
        )(idx_hbm, out_hbm)

    return gather(table, idx.reshape(1, M)).reshape(M // SC_SPLIT, full_width)


def _final_kernel(x1_ref, y1_ref, y2_ref, meta_ref, nw_ref, o_ref):
    w1 = meta_ref[:, 4:5]
    w2 = meta_ref[:, 5:6]
    x2 = x1_ref[...] + (w1 * y1_ref[...] + w2 * y2_ref[...])
    o_ref[...] = _rms(x2, nw_ref[...])


def _final(x1, yg, meta, norm_w):
    T = x1.shape[0]
    tm = COMB_TILE
    nt = T // tm
    row = lambda i: (i, 0)
    return pl.pallas_call(
        _final_kernel,
        grid=(nt,),
        in_specs=[pl.BlockSpec((tm, D_MODEL), row),
                  pl.BlockSpec((tm, D_MODEL), row),
                  pl.BlockSpec((tm, D_MODEL), lambda i: (i + nt, 0)),
                  pl.BlockSpec((tm, LANES), row),
                  pl.BlockSpec((1, D_MODEL), lambda i: (0, 0))],
        out_specs=pl.BlockSpec((tm, D_MODEL), row),
        out_shape=jax.ShapeDtypeStruct((T, D_MODEL), F32),
        compiler_params=pltpu.CompilerParams(dimension_semantics=("arbitrary",),
                                             vmem_limit_bytes=VMEM_LIMIT),
        name="final",
    )(x1, yg, yg, meta, norm_w)


def _layer(x, norm_mix_w, w_in, conv_w, conv_b, b_igate, b_fgate, attn_sinks, mlstm_norm_w,
           w_attn_o, w_mlstm_o, w_out, norm_ffn_w, w_group, b_group, w_router, b_router,
           w_gate, w_up, w_down, out_norm_w):
    B, S, D = x.shape
    T = B * S
    x2 = x.reshape(T, D)

    w_main = jnp.concatenate([w_in[:, :_O_MI], w_in[:, _O_GA:]], axis=1).astype(BF16)
    w_gates = jnp.pad(w_in[:, _O_MI:_O_GA], ((0, 0), (0, LANES - 2 * M_HEADS))).astype(BF16)
    q, kv, mqk, mv, mo, gcol, grow, gab = _inproj(x2, norm_mix_w.reshape(1, D), w_main, w_gates)

    ya = _attn(attn_sinks.astype(F32), q.reshape(B, S, -1), kv.reshape(B, S, -1))

    bias = jnp.concatenate([b_igate, b_fgate]).astype(F32)
    bias_col = jnp.pad(bias, (0, LANES - 2 * M_HEADS)).reshape(1, LANES)
    ym = _mlstm(mqk.reshape(B, S, -1), mv.reshape(B, S, -1), mo.reshape(B, S, -1),
                gcol.reshape(B, S, LANES), grow, conv_w.astype(F32), conv_b.reshape(1, -1).astype(F32),
                bias_col, bias.reshape(2 * M_HEADS, 1), mlstm_norm_w.reshape(1, -1).astype(F32))

    w_rt = jnp.pad(jnp.concatenate([w_group, w_router], axis=1),
                   ((0, 0), (0, LANES - N_GROUPS - N_EXPERTS))).astype(F32)
    b_rt = jnp.pad(jnp.concatenate([b_group, b_router]), (0, LANES - N_GROUPS - N_EXPERTS)).reshape(1, LANES)
    x1, h2, meta, code_rows, cnt = _outproj(x2, ya.reshape(T, -1), ym.reshape(T, -1), gab,
                                            w_attn_o.astype(BF16), w_mlstm_o.astype(BF16),
                                            w_out.astype(BF16), norm_ffn_w.reshape(1, D), w_rt,
                                            b_rt.astype(F32))

    codes = code_rows[0:2].reshape(2 * T)
    counts = cnt[0, N_GROUPS:N_GROUPS + N_EXPERTS].astype(jnp.int32)
    slot_tok, pstart, blk_e, nused = _slots(codes, counts)

    ys = _experts(blk_e, nused, slot_tok, h2, w_gate, w_up, w_down)
    dest = _dests(pstart, code_rows)[0:2].reshape(2 * T)
    yg = _sc_gather_rows(ys, dest)
    out = _final(x1, yg, meta, out_norm_w.reshape(1, D))
    return out.reshape(B, S, D)


def kernel(x, norm_mix_w, w_in, conv_w, conv_b, b_igate, b_fgate, attn_sinks, mlstm_norm_w, w_attn_o,
           w_mlstm_o, w_out, norm_ffn_w, w_group, b_group, w_router, b_router, w_gate, w_up, w_down,
           norm_final_w):
    depth = w_in.shape[0]
    assert depth == 1, "final RMSNorm is fused into the last layer's combine kernel"
    return _layer(x, norm_mix_w[0], w_in[0], conv_w[0], conv_b[0], b_igate[0], b_fgate[0],
                  attn_sinks[0], mlstm_norm_w[0], w_attn_o[0], w_mlstm_o[0], w_out[0], norm_ffn_w[0],
                  w_group[0], b_group[0], w_router[0], b_router[0], w_gate[0], w_up[0], w_down[0],
                  norm_final_w)
```

```python
import functools

import jax
import jax.numpy as jnp
from jax import lax
from jax.experimental import pallas as pl
from jax.experimental.pallas import tpu as pltpu
from jax.experimental.pallas import tpu_sc as plsc

F32 = jnp.float32
BF16 = jnp.bfloat16

D_MODEL = 1024
N_Q_HEADS = 8
N_KV_HEADS = 2
HEAD_DIM = 64
WINDOW = 128
GQA_GROUP = N_Q_HEADS // N_KV_HEADS
M_HEADS = 4
M_QK_DIM = 64
M_V_DIM = 128
CONV_WIDTH = 4
N_GROUPS = 4
EXPERTS_PER_GROUP = 8
N_EXPERTS = N_GROUPS * EXPERTS_PER_GROUP
D_EXPERT = 512
EPS = 1e-6

ATT_Q_WIDTH = N_Q_HEADS * HEAD_DIM
ATT_KV_WIDTH = N_KV_HEADS * HEAD_DIM
M_QK_WIDTH = M_HEADS * M_QK_DIM
M_V_WIDTH = M_HEADS * M_V_DIM

LANES = 128
NEG = -1e30
VMEM_LIMIT = 56 * 1024 * 1024

TOK_TILE = 512
ATT_TILE = 256
M_CHUNK = 128
SLOT_BLOCK = 256
COMB_TILE = 256
RANK_RADIX = 65536
RANK_BITS = 16
PACKED = D_MODEL // 2
SC_CHUNK = 64
DMA_THREADS = 2

_O_AQ = 0
_O_AK = _O_AQ + ATT_Q_WIDTH
_O_AV = _O_AK + ATT_KV_WIDTH
_O_MQ = _O_AV + ATT_KV_WIDTH
_O_MK = _O_MQ + M_QK_WIDTH
_O_MV = _O_MK + M_QK_WIDTH
_O_MO = _O_MV + M_V_WIDTH
_O_MI = _O_MO + M_V_WIDTH
_O_MF = _O_MI + M_HEADS
_O_GA = _O_MF + M_HEADS
_O_GB = _O_GA + D_MODEL
_O_END = _O_GB + D_MODEL


def _rms(x, w):
    return x * lax.rsqrt(jnp.mean(x * x, axis=-1, keepdims=True) + EPS) * w


def _sigmoid(x):
    return 1.0 / (1.0 + jnp.exp(-x))


def _pack_rows(x):
    half = x.shape[1] // 2
    bits = lax.bitcast_convert_type(x.astype(BF16).astype(F32), jnp.uint32)
    packed = (bits[:, :half] >> 16) | (bits[:, half:] & jnp.uint32(0xFFFF0000))
    return lax.bitcast_convert_type(packed, jnp.int32)


def _unpack_rows(w):
    u = lax.bitcast_convert_type(w, jnp.uint32)
    lo = lax.bitcast_convert_type(u << 16, F32)
    hi = lax.bitcast_convert_type(u & jnp.uint32(0xFFFF0000), F32)
    return jnp.concatenate([lo, hi], axis=-1)


def _log_sigmoid(x):
    return jnp.minimum(x, 0.0) - jnp.log1p(jnp.exp(-jnp.abs(x)))


def _inproj_kernel(x_ref, nw_ref, w_ref, wg_ref,
                   q_ref, kv_ref, mqk_ref, mv_ref, mo_ref, gcol_ref, grow_ref, gab_ref):
    h = _rms(x_ref[...], nw_ref[...]).astype(BF16)

    def proj(lo, hi):
        return jnp.dot(h, w_ref[:, lo:hi], preferred_element_type=F32)

    q_ref[...] = proj(_O_AQ, _O_AK).astype(BF16)
    kv_ref[...] = proj(_O_AK, _O_MQ).astype(BF16)
    mqk_ref[...] = proj(_O_MQ, _O_MV).astype(BF16)
    mv_ref[...] = proj(_O_MV, _O_MO).astype(BF16)
    mo_ref[...] = proj(_O_MO, _O_MI).astype(BF16)
    gab_ref[...] = proj(_O_MI, _O_MI + 2 * D_MODEL).astype(BF16)
    g = jnp.dot(h, wg_ref[...], preferred_element_type=F32)
    gcol_ref[...] = g
    grow_ref[...] = g.T[0:8, :]


def _inproj(x2, norm_w, w_main, w_gates):
    T = x2.shape[0]
    tm = TOK_TILE
    nmain = w_main.shape[1]
    row = lambda i: (i, 0)
    const = lambda i: (0, 0)
    return pl.pallas_call(
        _inproj_kernel,
        grid=(T // tm,),
        in_specs=[pl.BlockSpec((tm, D_MODEL), row),
                  pl.BlockSpec((1, D_MODEL), const),
                  pl.BlockSpec((D_MODEL, nmain), const),
                  pl.BlockSpec((D_MODEL, LANES), const)],
        out_specs=[pl.BlockSpec((tm, ATT_Q_WIDTH), row),
                   pl.BlockSpec((tm, 2 * ATT_KV_WIDTH), row),
                   pl.BlockSpec((tm, 2 * M_QK_WIDTH), row),
                   pl.BlockSpec((tm, M_V_WIDTH), row),
                   pl.BlockSpec((tm, M_V_WIDTH), row),
                   pl.BlockSpec((tm, LANES), row),
                   pl.BlockSpec((8, tm), lambda i: (0, i)),
                   pl.BlockSpec((tm, 2 * D_MODEL), row)],
        out_shape=[jax.ShapeDtypeStruct((T, ATT_Q_WIDTH), BF16),
                   jax.ShapeDtypeStruct((T, 2 * ATT_KV_WIDTH), BF16),
                   jax.ShapeDtypeStruct((T, 2 * M_QK_WIDTH), BF16),
                   jax.ShapeDtypeStruct((T, M_V_WIDTH), BF16),
                   jax.ShapeDtypeStruct((T, M_V_WIDTH), BF16),
                   jax.ShapeDtypeStruct((T, LANES), F32),
                   jax.ShapeDtypeStruct((8, T), F32),
                   jax.ShapeDtypeStruct((T, 2 * D_MODEL), BF16)],
        compiler_params=pltpu.CompilerParams(dimension_semantics=("arbitrary",),
                                             vmem_limit_bytes=VMEM_LIMIT),
        name="inproj",
    )(x2, norm_w, w_main, w_gates)


def _attn_kernel(sink_ref, q_ref, kv_ref, kvp_ref, o_ref):
    i = pl.program_id(1)
    W = WINDOW
    r = lax.broadcasted_iota(jnp.int32, (W, 2 * W), 0)
    c = lax.broadcasted_iota(jnp.int32, (W, 2 * W), 1)
    band = (c > r) & (c <= r + W)
    for j in range(ATT_TILE // W):
        qj = q_ref[0, j * W:(j + 1) * W, :]
        if j == 0:
            kprev = kvp_ref[0]
            valid = band & ((i > 0) | (c >= W))
        else:
            kprev = kv_ref[0, (j - 1) * W:j * W, :]
            valid = band
        kwin = jnp.concatenate([kprev, kv_ref[0, j * W:(j + 1) * W, :]], axis=0)
        outs = []
        for h in range(N_Q_HEADS):
            g = h // GQA_GROUP
            qh = qj[:, h * HEAD_DIM:(h + 1) * HEAD_DIM]
            kg = kwin[:, g * HEAD_DIM:(g + 1) * HEAD_DIM]
            vg = kwin[:, ATT_KV_WIDTH + g * HEAD_DIM:ATT_KV_WIDTH + (g + 1) * HEAD_DIM]
            s = lax.dot_general(qh, kg, (((1,), (1,)), ((), ())),
                                preferred_element_type=F32) * (HEAD_DIM ** -0.5)
            s = jnp.where(valid, s, NEG)
            sink = sink_ref[h]
            m = jnp.maximum(jnp.max(s, axis=-1, keepdims=True), sink)
            p = jnp.exp(s - m)
            l = jnp.sum(p, axis=-1, keepdims=True) + jnp.exp(sink - m)
            o = jnp.dot(p.astype(BF16), vg, preferred_element_type=F32)
            outs.append(o / l)
        o_ref[0, j * W:(j + 1) * W, :] = jnp.concatenate(outs, axis=-1).astype(BF16)


def _attn(sinks, q3, kv3):
    B, S, _ = q3.shape
    tq = ATT_TILE
    per = tq // WINDOW
    return pl.pallas_call(
        _attn_kernel,
        grid=(B, S // tq),
        in_specs=[pl.BlockSpec(memory_space=pltpu.SMEM),
                  pl.BlockSpec((1, tq, ATT_Q_WIDTH), lambda b, i: (b, i, 0)),
                  pl.BlockSpec((1, tq, 2 * ATT_KV_WIDTH), lambda b, i: (b, i, 0)),
                  pl.BlockSpec((1, WINDOW, 2 * ATT_KV_WIDTH),
                               lambda b, i: (b, jnp.maximum(i * per - 1, 0), 0))],
        out_specs=pl.BlockSpec((1, tq, ATT_Q_WIDTH), lambda b, i: (b, i, 0)),
        out_shape=jax.ShapeDtypeStruct((B, S, ATT_Q_WIDTH), BF16),
        compiler_params=pltpu.CompilerParams(dimension_semantics=("arbitrary", "arbitrary"),
                                             vmem_limit_bytes=VMEM_LIMIT),
        name="attn",
    )(sinks, q3, kv3, kv3)


def _mlstm_kernel(mqk_ref, mv_ref, mo_ref, gcol_ref, grow_ref, cw_ref, cb_ref, bcol_ref, brow_ref,
                  nw_ref, y_ref, cext_ref, m_ref, ubuf_ref):
    L = M_CHUNK
    cidx = pl.program_id(1)

    @pl.when(cidx == 0)
    def _():
        cext_ref[...] = jnp.zeros_like(cext_ref)
        m_ref[...] = jnp.zeros_like(m_ref)
        ubuf_ref[0:8, :] = jnp.zeros((8, 2 * M_QK_WIDTH), F32)

    ubuf_ref[8:L + 8, :] = mqk_ref[0].astype(F32)
    acc = cb_ref[...] + cw_ref[CONV_WIDTH - 1:CONV_WIDTH, :] * ubuf_ref[8:L + 8, :]
    for j in range(CONV_WIDTH - 1):
        off = 8 - (CONV_WIDTH - 1) + j
        acc = acc + cw_ref[j:j + 1, :] * ubuf_ref[off:off + L, :]
    ubuf_ref[0:8, :] = ubuf_ref[L:L + 8, :]
    qk = acc * _sigmoid(acc)
    q = qk[:, :M_QK_WIDTH].astype(BF16)
    k = qk[:, M_QK_WIDTH:] * (M_QK_DIM ** -0.5)

    gc = gcol_ref[0] + bcol_ref[...]
    gr = grow_ref[...] + brow_ref[...]
    ti = lax.broadcasted_iota(jnp.int32, (L, L), 0)
    si = lax.broadcasted_iota(jnp.int32, (L, L), 1)
    causal = si <= ti
    tril = jnp.where(causal, 1.0, 0.0).astype(F32)
    triu = jnp.where(ti <= si, 1.0, 0.0).astype(F32)
    b_c = jnp.dot(tril, _log_sigmoid(gc), preferred_element_type=F32,
                  precision=lax.Precision.HIGHEST)
    b_r = jnp.dot(_log_sigmoid(gr), triu, preferred_element_type=F32,
                  precision=lax.Precision.HIGHEST)

    ext = jnp.where(lax.broadcasted_iota(jnp.int32, (L, LANES), 1) == 0, 1.0, 0.0).astype(BF16)
    outs = []
    for h in range(M_HEADS):
        bc = b_c[:, M_HEADS + h:M_HEADS + h + 1]
        br = b_r[M_HEADS + h:M_HEADS + h + 1, :]
        ir = gr[h:h + 1, :]
        ic = gc[:, h:h + 1]
        m_prev = m_ref[h:h + 1, 0:1]
        dmat = jnp.where(causal, bc - br + ir, NEG)
        a = bc + m_prev
        mt = jnp.maximum(a, jnp.max(dmat, axis=-1, keepdims=True))
        wq = jnp.exp(dmat - mt)
        wa = jnp.exp(a - mt)
        qh = q[:, h * M_QK_DIM:(h + 1) * M_QK_DIM]
        kh = k[:, h * M_QK_DIM:(h + 1) * M_QK_DIM]
        sw = lax.dot_general(qh, kh.astype(BF16), (((1,), (1,)), ((), ())),
                             preferred_element_type=F32) * wq
        vext = jnp.concatenate([mv_ref[0, :, h * M_V_DIM:(h + 1) * M_V_DIM], ext], axis=-1)
        cprev = cext_ref[h]
        res = wa * jnp.dot(qh, cprev.astype(BF16), preferred_element_type=F32) \
            + jnp.dot(sw.astype(BF16), vext, preferred_element_type=F32)
        num = res[:, :M_V_DIM]
        den = res[:, M_V_DIM:M_V_DIM + 1]
        hb = num / jnp.maximum(jnp.abs(den), jnp.exp(-mt))
        m_new = mt[L - 1:L, :]
        bl = bc[L - 1:L, :]
        wc = jnp.exp(bl + m_prev - m_new)
        ws = jnp.exp(bl - bc + ic - m_new)
        kw = (kh * ws).astype(BF16)
        cext_ref[h] = wc * cprev + lax.dot_general(kw, vext, (((0,), (0,)), ((), ())),
                                                   preferred_element_type=F32)
        m_ref[h:h + 1, :] = jnp.broadcast_to(m_new, (1, LANES))
        hn = hb * lax.rsqrt(jnp.mean(hb * hb, axis=-1, keepdims=True) + EPS)
        hn = hn * nw_ref[:, h * M_V_DIM:(h + 1) * M_V_DIM]
        og = _sigmoid(mo_ref[0, :, h * M_V_DIM:(h + 1) * M_V_DIM].astype(F32))
        outs.append(og * hn)
    y_ref[0] = jnp.concatenate(outs, axis=-1).astype(BF16)


def _mlstm(mqk3, mv3, mo3, gcol3, grow, conv_w, conv_b, bias_col, bias_row, norm_w):
    B, S, _ = mqk3.shape
    L = M_CHUNK
    nc = S // L
    tok = lambda b, c: (b, c, 0)
    const = lambda b, c: (0, 0)
    return pl.pallas_call(
        _mlstm_kernel,
        grid=(B, nc),
        in_specs=[pl.BlockSpec((1, L, 2 * M_QK_WIDTH), tok),
                  pl.BlockSpec((1, L, M_V_WIDTH), tok),
                  pl.BlockSpec((1, L, M_V_WIDTH), tok),
                  pl.BlockSpec((1, L, LANES), tok),
                  pl.BlockSpec((8, L), lambda b, c: (0, b * nc + c)),
                  pl.BlockSpec((CONV_WIDTH, 2 * M_QK_WIDTH), const),
                  pl.BlockSpec((1, 2 * M_QK_WIDTH), const),
                  pl.BlockSpec((1, LANES), const),
                  pl.BlockSpec((8, 1), const),
                  pl.BlockSpec((1, M_V_WIDTH), const)],
        out_specs=pl.BlockSpec((1, L, M_V_WIDTH), tok),
        out_shape=jax.ShapeDtypeStruct((B, S, M_V_WIDTH), BF16),
        scratch_shapes=[pltpu.VMEM((M_HEADS, M_QK_DIM, 2 * M_V_DIM), F32),
                        pltpu.VMEM((8, LANES), F32),
                        pltpu.VMEM((L + 8, 2 * M_QK_WIDTH), F32)],
        compiler_params=pltpu.CompilerParams(dimension_semantics=("arbitrary", "arbitrary"),
                                             vmem_limit_bytes=VMEM_LIMIT),
        name="mlstm",
    )(mqk3, mv3, mo3, gcol3, grow, conv_w, conv_b, bias_col, bias_row, norm_w)


def _outproj_kernel(x_ref, ya_ref, ym_ref, gab_ref, wa_ref, wm_ref, wo_ref, nw_ref, wr_ref, br_ref,
                    x1_ref, h2_ref, meta_ref, code_ref, cnt_ref, cnt_scr):
    tm = TOK_TILE
    i = pl.program_id(0)

    @pl.when(i == 0)
    def _():
        cnt_scr[...] = jnp.zeros_like(cnt_scr)

    pa = jnp.dot(ya_ref[...], wa_ref[...], preferred_element_type=F32)
    pm = jnp.dot(ym_ref[...], wm_ref[...], preferred_element_type=F32)
    ga = gab_ref[:, :D_MODEL].astype(F32)
    gb = gab_ref[:, D_MODEL:].astype(F32)
    mix = _sigmoid(ga) * pa + _sigmoid(gb) * pm
    x1 = x_ref[...] + jnp.dot(mix.astype(BF16), wo_ref[...], preferred_element_type=F32)
    x1_ref[...] = x1
    h2 = _rms(x1, nw_ref[...])
    h2_ref[...] = _pack_rows(h2)

    logits = jnp.dot(h2, wr_ref[...], preferred_element_type=F32,
                     precision=lax.Precision.HIGHEST) + br_ref[...]
    lane = lax.broadcasted_iota(jnp.int32, (tm, LANES), 1)
    big = jnp.int32(LANES)

    def first_argmax(v):
        mx = jnp.max(v, axis=-1, keepdims=True)
        idx = jnp.min(jnp.where(v == mx, lane, big), axis=-1, keepdims=True)
        return mx, idx

    gl = jnp.where(lane < N_GROUPS, logits, NEG)
    gmax, gi = first_argmax(gl)
    gp = 1.0 / jnp.sum(jnp.exp(gl - gmax), axis=-1, keepdims=True)
    lo = N_GROUPS + gi * EXPERTS_PER_GROUP
    el = jnp.where((lane >= lo) & (lane < lo + EXPERTS_PER_GROUP), logits, NEG)
    v1, j1 = first_argmax(el)
    v2, j2 = first_argmax(jnp.where(lane == j1, NEG, el))
    t = jnp.exp(v2 - v1)
    w1 = gp / (1.0 + t)
    w2 = gp * t / (1.0 + t)

    hit1 = lane == j1
    hit2 = lane == j2
    onehot = jnp.where(hit1 | hit2, 1.0, 0.0)
    ri = lax.broadcasted_iota(jnp.int32, (tm, tm), 0)
    ci = lax.broadcasted_iota(jnp.int32, (tm, tm), 1)
    before = jnp.where(ci < ri, 1.0, 0.0).astype(BF16)
    rank = jnp.dot(before, onehot.astype(BF16), preferred_element_type=F32) + cnt_scr[0:1, :]
    r1 = jnp.sum(jnp.where(hit1, rank, 0.0), axis=-1, keepdims=True)
    r2 = jnp.sum(jnp.where(hit2, rank, 0.0), axis=-1, keepdims=True)
    cnt = cnt_scr[0:1, :] + jnp.sum(onehot, axis=0, keepdims=True)
    cnt_scr[...] = jnp.broadcast_to(cnt, cnt_scr.shape)
    cnt_ref[...] = jnp.broadcast_to(cnt, cnt_ref.shape)

    c1 = (j1 - N_GROUPS).astype(F32) * RANK_RADIX + r1
    c2 = (j2 - N_GROUPS).astype(F32) * RANK_RADIX + r2
    meta = jnp.where(lane == 0, c1, 0.0)
    meta = jnp.where(lane == 1, c2, meta)
    meta = jnp.where(lane == 4, w1, meta)
    meta = jnp.where(lane == 5, w2, meta)
    meta_ref[...] = meta
    code_ref[...] = meta.T[0:8, :].astype(jnp.int32)


def _outproj(x2, ya, ym, gab, wa, wm, wo, norm_w, w_rt, b_rt):
    T = x2.shape[0]
    tm = TOK_TILE
    row = lambda i: (i, 0)
    const = lambda i: (0, 0)
    return pl.pallas_call(
        _outproj_kernel,
        grid=(T // tm,),
        in_specs=[pl.BlockSpec((tm, D_MODEL), row),
                  pl.BlockSpec((tm, ATT_Q_WIDTH), row),
                  pl.BlockSpec((tm, M_V_WIDTH), row),
                  pl.BlockSpec((tm, 2 * D_MODEL), row),
                  pl.BlockSpec((ATT_Q_WIDTH, D_MODEL), const),
                  pl.BlockSpec((M_V_WIDTH, D_MODEL), const),
                  pl.BlockSpec((D_MODEL, D_MODEL), const),
                  pl.BlockSpec((1, D_MODEL), const),
                  pl.BlockSpec((D_MODEL, LANES), const),
                  pl.BlockSpec((1, LANES), const)],
        out_specs=[pl.BlockSpec((tm, D_MODEL), row),
                   pl.BlockSpec((tm, PACKED), row),
                   pl.BlockSpec((tm, LANES), row),
                   pl.BlockSpec((8, tm), lambda i: (0, i)),
                   pl.BlockSpec((8, LANES), const)],
        out_shape=[jax.ShapeDtypeStruct((T, D_MODEL), F32),
                   jax.ShapeDtypeStruct((T, PACKED), jnp.int32),
                   jax.ShapeDtypeStruct((T, LANES), F32),
                   jax.ShapeDtypeStruct((8, T), jnp.int32),
                   jax.ShapeDtypeStruct((8, LANES), F32)],
        scratch_shapes=[pltpu.VMEM((8, LANES), F32)],
        compiler_params=pltpu.CompilerParams(dimension_semantics=("arbitrary",),
                                             vmem_limit_bytes=VMEM_LIMIT),
        name="outproj",
    )(x2, ya, ym, gab, wa, wm, wo, norm_w, w_rt, b_rt)


def _slots_kernel(cnt_ref, pstart_ref, blk_e_ref, nvalid_ref, nused_ref):
    R = SLOT_BLOCK
    nblk = blk_e_ref.shape[0]

    def per_expert(e, run):
        c = cnt_ref[e]
        end = run + ((c + R - 1) // R) * R
        pstart_ref[e] = run

        def set_blk(b, carry):
            blk_e_ref[b] = e
            nvalid_ref[b] = jnp.minimum(run + c - b * R, R)
            return carry
        lax.fori_loop(run // R, end // R, set_blk, 0)
        return end

    total = lax.fori_loop(0, N_EXPERTS, per_expert, 0)
    nused_ref[0] = total // R

    def tail_blk(b, carry):
        blk_e_ref[b] = N_EXPERTS - 1
        nvalid_ref[b] = 0
        return carry
    lax.fori_loop(total // R, nblk, tail_blk, 0)


def _slots(counts, nblk):
    smem = pl.BlockSpec(memory_space=pltpu.SMEM)
    return pl.pallas_call(
        _slots_kernel,
        in_specs=[smem],
        out_specs=[smem, smem, smem, smem],
        out_shape=[jax.ShapeDtypeStruct((N_EXPERTS,), jnp.int32),
                   jax.ShapeDtypeStruct((nblk,), jnp.int32),
                   jax.ShapeDtypeStruct((nblk,), jnp.int32),
                   jax.ShapeDtypeStruct((1,), jnp.int32)],
        name="slots",
    )(counts)


def _experts_kernel(blk_e_ref, nvalid_ref, nused_ref, xs_ref, wg_ref, wu_ref, wd_ref, ys_ref,
                    wg_bf, wu_bf, wd_bf):
    j = pl.program_id(0)
    R = SLOT_BLOCK
    valid = j < nused_ref[0]
    changed = jnp.logical_or(j == 0, blk_e_ref[j] != blk_e_ref[jnp.maximum(j - 1, 0)])

    @pl.when(jnp.logical_and(valid, changed))
    def _():
        wg_bf[...] = wg_ref[0].astype(BF16)
        wu_bf[...] = wu_ref[0].astype(BF16)
        wd_bf[...] = wd_ref[0].astype(BF16)

    @pl.when(valid)
    def _():
        rows = lax.broadcasted_iota(jnp.int32, (R, PACKED), 0)
        xw = jnp.where(rows < nvalid_ref[j], xs_ref[...], 0)
        xb = _unpack_rows(xw).astype(BF16)
        g = jnp.dot(xb, wg_bf[...], preferred_element_type=F32)
        u = jnp.dot(xb, wu_bf[...], preferred_element_type=F32)
        act = (g * _sigmoid(g) * u).astype(BF16)
        ys_ref[...] = _pack_rows(jnp.dot(act, wd_bf[...], preferred_element_type=F32))

    @pl.when(jnp.logical_not(valid))
    def _():
        ys_ref[...] = jnp.zeros_like(ys_ref)


def _experts(blk_e, nvalid, nused, xs, w_gate, w_up, w_down):
    nblk = blk_e.shape[0]
    R = SLOT_BLOCK
    wspec = lambda shape: pl.BlockSpec((1,) + shape, lambda j, be, nv, nu: (be[j], 0, 0))
    grid_spec = pltpu.PrefetchScalarGridSpec(
        num_scalar_prefetch=3,
        grid=(nblk,),
        in_specs=[pl.BlockSpec((R, PACKED), lambda j, be, nv, nu: (jnp.minimum(j, nu[0] - 1), 0)),
                  wspec((D_MODEL, D_EXPERT)),
                  wspec((D_MODEL, D_EXPERT)),
                  wspec((D_EXPERT, D_MODEL))],
        out_specs=pl.BlockSpec((R, PACKED), lambda j, be, nv, nu: (j, 0)),
        scratch_shapes=[pltpu.VMEM((D_MODEL, D_EXPERT), BF16),
                        pltpu.VMEM((D_MODEL, D_EXPERT), BF16),
                        pltpu.VMEM((D_EXPERT, D_MODEL), BF16)],
    )
    return pl.pallas_call(
        _experts_kernel,
        grid_spec=grid_spec,
        out_shape=jax.ShapeDtypeStruct((nblk * R, PACKED), jnp.int32),
        compiler_params=pltpu.CompilerParams(dimension_semantics=("arbitrary",),
                                             vmem_limit_bytes=VMEM_LIMIT),
        name="experts",
    )(blk_e, nvalid, nused, xs, w_gate, w_up, w_down)


def _dest_kernel(pstart_ref, code_ref, dest_ref):
    c = code_ref[...]
    e = c >> RANK_BITS
    base = jnp.zeros_like(c)
    for k in range(N_EXPERTS):
        base = jnp.where(e == k, pstart_ref[k], base)
    dest_ref[...] = base + (c & (RANK_RADIX - 1))


def _dests(pstart, code_rows):
    T = code_rows.shape[1]
    tc = min(4096, T)
    return pl.pallas_call(
        _dest_kernel,
        grid=(T // tc,),
        in_specs=[pl.BlockSpec(memory_space=pltpu.SMEM),
                  pl.BlockSpec((8, tc), lambda i: (0, i))],
        out_specs=pl.BlockSpec((8, tc), lambda i: (0, i)),
        out_shape=jax.ShapeDtypeStruct((8, T), jnp.int32),
        name="dests",
    )(pstart, code_rows)


def _sc_move_rows(src, idx, n_out, scatter):
    M = idx.shape[0]
    n_src, width = src.shape
    info = plsc.get_sparse_core_info()
    nc, nw = info.num_cores, info.num_cores * info.num_subcores
    per_w = M // nw
    ch = SC_CHUNK
    nch = per_w // ch
    assert per_w * nw == M and nch * ch == per_w and nch % 2 == 0 and n_src % per_w == 0
    mesh = plsc.VectorSubcoreMesh(core_axis_name="c", subcore_axis_name="s")

    @functools.partial(
        pl.kernel, mesh=mesh, out_type=jax.ShapeDtypeStruct((n_out, width), src.dtype),
        scratch_types=[pltpu.VMEM((per_w,), jnp.int32),
                       pltpu.VMEM((ch, width), src.dtype), pltpu.VMEM((ch, width), src.dtype),
                       pltpu.SemaphoreType.DMA, pltpu.SemaphoreType.DMA,
                       pltpu.SemaphoreType.DMA, pltpu.SemaphoreType.DMA],
        name="sc_scatter" if scatter else "sc_gather")
    def move(src_hbm, idx_hbm, out_hbm, idx_v, buf0, buf1, in0, in1, out0, out1):
        wid = lax.axis_index("s") * nc + lax.axis_index("c")
        base = wid * per_w
        pltpu.sync_copy(idx_hbm.at[pl.ds(base, per_w)], idx_v)
        lin_base = lax.rem(base, n_src) if scatter else base

        def read(c, buf, sem):
            if scatter:
                return pltpu.make_async_copy(src_hbm.at[pl.ds(lin_base + c * ch, ch)], buf, sem)
            return pltpu.make_async_copy(src_hbm.at[idx_v.at[pl.ds(c * ch, ch)]], buf, sem)

        def write(c, buf, sem):
            if scatter:
                return pltpu.make_async_copy(buf, out_hbm.at[idx_v.at[pl.ds(c * ch, ch)]], sem)
            return pltpu.make_async_copy(buf, out_hbm.at[pl.ds(lin_base + c * ch, ch)], sem)

        read(0, buf0, in0).start()

        @pl.loop(0, nch, step=2)
        def _(c):
            @pl.when(c > 0)
            def _():
                write(c - 1, buf1, out1).wait()
            read(c + 1, buf1, in1).start()
            read(c, buf0, in0).wait()
            write(c, buf0, out0).start()
            read(c + 1, buf1, in1).wait()
            write(c + 1, buf1, out1).start()
            write(c, buf0, out0).wait()

            @pl.when(c + 2 < nch)
            def _():
                read(c + 2, buf0, in0).start()

        write(nch - 1, buf1, out1).wait()

    return move(src, idx)


def _final_kernel(x1_ref, y1_ref, y2_ref, meta_ref, nw_ref, o_ref):
    w1 = meta_ref[:, 4:5]
    w2 = meta_ref[:, 5:6]
    x2 = x1_ref[...] + (w1 * _unpack_rows(y1_ref[...]) + w2 * _unpack_rows(y2_ref[...]))
    o_ref[...] = _rms(x2, nw_ref[...])


def _final(x1, yg, meta, norm_w):
    T = x1.shape[0]
    tm = COMB_TILE
    nt = T // tm
    row = lambda i: (i, 0)
    return pl.pallas_call(
        _final_kernel,
        grid=(nt,),
        in_specs=[pl.BlockSpec((tm, D_MODEL), row),
                  pl.BlockSpec((tm, PACKED), row),
                  pl.BlockSpec((tm, PACKED), lambda i: (i + nt, 0)),
                  pl.BlockSpec((tm, LANES), row),
                  pl.BlockSpec((1, D_MODEL), lambda i: (0, 0))],
        out_specs=pl.BlockSpec((tm, D_MODEL), row),
        out_shape=jax.ShapeDtypeStruct((T, D_MODEL), F32),
        compiler_params=pltpu.CompilerParams(dimension_semantics=("arbitrary",),
                                             vmem_limit_bytes=VMEM_LIMIT),
        name="final",
    )(x1, yg, yg, meta, norm_w)


def _layer(x, norm_mix_w, w_in, conv_w, conv_b, b_igate, b_fgate, attn_sinks, mlstm_norm_w,
           w_attn_o, w_mlstm_o, w_out, norm_ffn_w, w_group, b_group, w_router, b_router,
           w_gate, w_up, w_down, out_norm_w):
    B, S, D = x.shape
    T = B * S
    x2 = x.reshape(T, D)

    w_main = jnp.concatenate([w_in[:, :_O_MI], w_in[:, _O_GA:]], axis=1).astype(BF16)
    w_gates = jnp.pad(w_in[:, _O_MI:_O_GA], ((0, 0), (0, LANES - 2 * M_HEADS))).astype(BF16)
    q, kv, mqk, mv, mo, gcol, grow, gab = _inproj(x2, norm_mix_w.reshape(1, D), w_main, w_gates)

    ya = _attn(attn_sinks.astype(F32), q.reshape(B, S, -1), kv.reshape(B, S, -1))

    bias = jnp.concatenate([b_igate, b_fgate]).astype(F32)
    bias_col = jnp.pad(bias, (0, LANES - 2 * M_HEADS)).reshape(1, LANES)
    ym = _mlstm(mqk.reshape(B, S, -1), mv.reshape(B, S, -1), mo.reshape(B, S, -1),
                gcol.reshape(B, S, LANES), grow, conv_w.astype(F32), conv_b.reshape(1, -1).astype(F32),
                bias_col, bias.reshape(2 * M_HEADS, 1), mlstm_norm_w.reshape(1, -1).astype(F32))

    w_rt = jnp.pad(jnp.concatenate([w_group, w_router], axis=1),
                   ((0, 0), (0, LANES - N_GROUPS - N_EXPERTS))).astype(F32)
    b_rt = jnp.pad(jnp.concatenate([b_group, b_router]), (0, LANES - N_GROUPS - N_EXPERTS)).reshape(1, LANES)
    x1, h2p, meta, code_rows, cnt = _outproj(x2, ya.reshape(T, -1), ym.reshape(T, -1), gab,
                                            w_attn_o.astype(BF16), w_mlstm_o.astype(BF16),
                                            w_out.astype(BF16), norm_ffn_w.reshape(1, D), w_rt,
                                            b_rt.astype(F32))

    counts = cnt[0, N_GROUPS:N_GROUPS + N_EXPERTS].astype(jnp.int32)
    nblk = (2 * T) // SLOT_BLOCK + N_EXPERTS
    pstart, blk_e, nvalid, nused = _slots(counts, nblk)
    dest = _dests(pstart, code_rows)[0:2].reshape(2 * T)

    xs = _sc_move_rows(h2p, dest, nblk * SLOT_BLOCK, scatter=True)
    ys = _experts(blk_e, nvalid, nused, xs, w_gate, w_up, w_down)
    yg = _sc_move_rows(ys, dest, 2 * T, scatter=False)
    out = _final(x1, yg, meta, out_norm_w.reshape(1, D))
    return out.reshape(B, S, D)


def kernel(x, norm_mix_w, w_in, conv_w, conv_b, b_igate, b_fgate, attn_sinks, mlstm_norm_w, w_attn_o,
           w_mlstm_o, w_out, norm_ffn_w, w_group, b_group, w_router, b_router, w_gate, w_up, w_down,
           norm_final_w):
    depth = w_in.shape[0]
    assert depth == 1, "final RMSNorm is fused into the last layer's combine kernel"
    return _layer(x, norm_mix_w[0], w_in[0], conv_w[0], conv_b[0], b_igate[0], b_fgate[0],
                  attn_sinks[0], mlstm_norm_w[0], w_attn_o[0], w_mlstm_o[0], w_out[0], norm_ffn_w[0],
                  w_group[0], b_group[0], w_router[0], b_router[0], w_gate[0], w_up[0], w_down[0],
                  norm_final_w)
```

```python
import functools

import jax
import jax.numpy as jnp
from jax import lax
from jax.experimental import pallas as pl
from jax.experimental.pallas import tpu as pltpu
from jax.experimental.pallas import tpu_sc as plsc

F32 = jnp.float32
BF16 = jnp.bfloat16

D_MODEL = 1024
N_Q_HEADS = 8
N_KV_HEADS = 2
HEAD_DIM = 64
WINDOW = 128
GQA_GROUP = N_Q_HEADS // N_KV_HEADS
M_HEADS = 4
M_QK_DIM = 64
M_V_DIM = 128
CONV_WIDTH = 4
N_GROUPS = 4
EXPERTS_PER_GROUP = 8
N_EXPERTS = N_GROUPS * EXPERTS_PER_GROUP
D_EXPERT = 512
EPS = 1e-6

ATT_Q_WIDTH = N_Q_HEADS * HEAD_DIM
ATT_KV_WIDTH = N_KV_HEADS * HEAD_DIM
M_QK_WIDTH = M_HEADS * M_QK_DIM
M_V_WIDTH = M_HEADS * M_V_DIM

LANES = 128
NEG = -1e30
VMEM_LIMIT = 56 * 1024 * 1024

TOK_TILE = 512
ATT_TILE = 256
M_CHUNK = 128
M_BATCH = 2
SLOT_BLOCK = 512
COMB_TILE = 512
RANK_RADIX = 65536
RANK_BITS = 16
PACKED = D_MODEL // 2
SC_CHUNK = 64
DMA_THREADS = 2

_O_AQ = 0
_O_AK = _O_AQ + ATT_Q_WIDTH
_O_AV = _O_AK + ATT_KV_WIDTH
_O_MQ = _O_AV + ATT_KV_WIDTH
_O_MK = _O_MQ + M_QK_WIDTH
_O_MV = _O_MK + M_QK_WIDTH
_O_MO = _O_MV + M_V_WIDTH
_O_MI = _O_MO + M_V_WIDTH
_O_MF = _O_MI + M_HEADS
_O_GA = _O_MF + M_HEADS
_O_GB = _O_GA + D_MODEL
_O_END = _O_GB + D_MODEL


def _rms(x, w):
    return x * lax.rsqrt(jnp.mean(x * x, axis=-1, keepdims=True) + EPS) * w


def _sigmoid(x):
    return 0.5 * jnp.tanh(0.5 * x) + 0.5


def _pack_rows(x):
    half = x.shape[1] // 2
    bits = lax.bitcast_convert_type(x.astype(BF16).astype(F32), jnp.uint32)
    packed = (bits[:, :half] >> 16) | (bits[:, half:] & jnp.uint32(0xFFFF0000))
    return lax.bitcast_convert_type(packed, jnp.int32)


def _unpack_rows(w):
    u = lax.bitcast_convert_type(w, jnp.uint32)
    lo = lax.bitcast_convert_type(u << 16, F32)
    hi = lax.bitcast_convert_type(u & jnp.uint32(0xFFFF0000), F32)
    return jnp.concatenate([lo, hi], axis=-1)


def _log_sigmoid(x):
    return jnp.minimum(x, 0.0) - jnp.log1p(jnp.exp(-jnp.abs(x)))


def _inproj_kernel(x_ref, nw_ref, w_ref, wg_ref,
                   q_ref, kv_ref, mqk_ref, mv_ref, mo_ref, gcol_ref, grow_ref, gab_ref):
    h = _rms(x_ref[...], nw_ref[...]).astype(BF16)

    def proj(lo, hi):
        return jnp.dot(h, w_ref[:, lo:hi], preferred_element_type=F32)

    q_ref[...] = proj(_O_AQ, _O_AK).astype(BF16)
    kv_ref[...] = proj(_O_AK, _O_MQ).astype(BF16)
    mqk_ref[...] = proj(_O_MQ, _O_MV).astype(BF16)
    mv_ref[...] = proj(_O_MV, _O_MO).astype(BF16)
    mo_ref[...] = proj(_O_MO, _O_MI).astype(BF16)
    gab_ref[...] = proj(_O_MI, _O_MI + 2 * D_MODEL).astype(BF16)
    g = jnp.dot(h, wg_ref[...], preferred_element_type=F32)
    gcol_ref[...] = g
    grow_ref[...] = g.T[0:8, :]


def _inproj(x2, norm_w, w_main, w_gates):
    T = x2.shape[0]
    tm = TOK_TILE
    nmain = w_main.shape[1]
    row = lambda i: (i, 0)
    const = lambda i: (0, 0)
    return pl.pallas_call(
        _inproj_kernel,
        grid=(T // tm,),
        in_specs=[pl.BlockSpec((tm, D_MODEL), row),
                  pl.BlockSpec((1, D_MODEL), const),
                  pl.BlockSpec((D_MODEL, nmain), const),
                  pl.BlockSpec((D_MODEL, LANES), const)],
        out_specs=[pl.BlockSpec((tm, ATT_Q_WIDTH), row),
                   pl.BlockSpec((tm, 2 * ATT_KV_WIDTH), row),
                   pl.BlockSpec((tm, 2 * M_QK_WIDTH), row),
                   pl.BlockSpec((tm, M_V_WIDTH), row),
                   pl.BlockSpec((tm, M_V_WIDTH), row),
                   pl.BlockSpec((tm, LANES), row),
                   pl.BlockSpec((8, tm), lambda i: (0, i)),
                   pl.BlockSpec((tm, 2 * D_MODEL), row)],
        out_shape=[jax.ShapeDtypeStruct((T, ATT_Q_WIDTH), BF16),
                   jax.ShapeDtypeStruct((T, 2 * ATT_KV_WIDTH), BF16),
                   jax.ShapeDtypeStruct((T, 2 * M_QK_WIDTH), BF16),
                   jax.ShapeDtypeStruct((T, M_V_WIDTH), BF16),
                   jax.ShapeDtypeStruct((T, M_V_WIDTH), BF16),
                   jax.ShapeDtypeStruct((T, LANES), F32),
                   jax.ShapeDtypeStruct((8, T), F32),
                   jax.ShapeDtypeStruct((T, 2 * D_MODEL), BF16)],
        compiler_params=pltpu.CompilerParams(dimension_semantics=("arbitrary",),
                                             vmem_limit_bytes=VMEM_LIMIT),
        name="inproj",
    )(x2, norm_w, w_main, w_gates)


def _attn_kernel(sink_ref, q_ref, kv_ref, kvp_ref, o_ref):
    i = pl.program_id(1)
    W = WINDOW
    r = lax.broadcasted_iota(jnp.int32, (W, 2 * W), 0)
    c = lax.broadcasted_iota(jnp.int32, (W, 2 * W), 1)
    band = (c > r) & (c <= r + W)
    for j in range(ATT_TILE // W):
        qj = q_ref[0, j * W:(j + 1) * W, :]
        if j == 0:
            kprev = kvp_ref[0]
            valid = band & ((i > 0) | (c >= W))
        else:
            kprev = kv_ref[0, (j - 1) * W:j * W, :]
            valid = band
        kwin = jnp.concatenate([kprev, kv_ref[0, j * W:(j + 1) * W, :]], axis=0)
        outs = []
        for h in range(N_Q_HEADS):
            g = h // GQA_GROUP
            qh = qj[:, h * HEAD_DIM:(h + 1) * HEAD_DIM]
            kg = kwin[:, g * HEAD_DIM:(g + 1) * HEAD_DIM]
            vg = kwin[:, ATT_KV_WIDTH + g * HEAD_DIM:ATT_KV_WIDTH + (g + 1) * HEAD_DIM]
            s = lax.dot_general(qh, kg, (((1,), (1,)), ((), ())),
                                preferred_element_type=F32) * (HEAD_DIM ** -0.5)
            s = jnp.where(valid, s, NEG)
            sink = sink_ref[h]
            m = jnp.maximum(jnp.max(s, axis=-1, keepdims=True), sink)
            p = jnp.exp(s - m)
            l = jnp.sum(p, axis=-1, keepdims=True) + jnp.exp(sink - m)
            o = jnp.dot(p.astype(BF16), vg, preferred_element_type=F32)
            outs.append(o / l)
        o_ref[0, j * W:(j + 1) * W, :] = jnp.concatenate(outs, axis=-1).astype(BF16)


def _attn(sinks, q3, kv3):
    B, S, _ = q3.shape
    tq = ATT_TILE
    per = tq // WINDOW
    return pl.pallas_call(
        _attn_kernel,
        grid=(B, S // tq),
        in_specs=[pl.BlockSpec(memory_space=pltpu.SMEM),
                  pl.BlockSpec((1, tq, ATT_Q_WIDTH), lambda b, i: (b, i, 0)),
                  pl.BlockSpec((1, tq, 2 * ATT_KV_WIDTH), lambda b, i: (b, i, 0)),
                  pl.BlockSpec((1, WINDOW, 2 * ATT_KV_WIDTH),
                               lambda b, i: (b, jnp.maximum(i * per - 1, 0), 0))],
        out_specs=pl.BlockSpec((1, tq, ATT_Q_WIDTH), lambda b, i: (b, i, 0)),
        out_shape=jax.ShapeDtypeStruct((B, S, ATT_Q_WIDTH), BF16),
        compiler_params=pltpu.CompilerParams(dimension_semantics=("arbitrary", "arbitrary"),
                                             vmem_limit_bytes=VMEM_LIMIT),
        name="attn",
    )(sinks, q3, kv3, kv3)


def _mlstm_kernel(*refs):
    nb = M_BATCH
    mqk_ref, mv_ref, mo_ref, gcol_ref = refs[:4]
    grow_refs = refs[4:4 + nb]
    cw_ref, cb_ref, bcol_ref, brow_ref, nw_ref, y_ref, cext_ref, m_ref, ubuf_ref = refs[4 + nb:]
    cidx = pl.program_id(1)

    @pl.when(cidx == 0)
    def _():
        cext_ref[...] = jnp.zeros_like(cext_ref)
        m_ref[...] = jnp.zeros_like(m_ref)
        ubuf_ref[:, 0:8, :] = jnp.zeros((nb, 8, 2 * M_QK_WIDTH), F32)

    for bb in range(nb):
        _mlstm_chunk(bb, mqk_ref, mv_ref, mo_ref, gcol_ref, grow_refs[bb], cw_ref, cb_ref, bcol_ref,
                     brow_ref, nw_ref, y_ref, cext_ref, m_ref, ubuf_ref)


def _mlstm_chunk(bb, mqk_ref, mv_ref, mo_ref, gcol_ref, grow_ref, cw_ref, cb_ref, bcol_ref, brow_ref,
                 nw_ref, y_ref, cext_ref, m_ref, ubuf_ref):
    L = M_CHUNK
    ubuf_ref[bb, 8:L + 8, :] = mqk_ref[bb].astype(F32)
    acc = cb_ref[...] + cw_ref[CONV_WIDTH - 1:CONV_WIDTH, :] * ubuf_ref[bb, 8:L + 8, :]
    for j in range(CONV_WIDTH - 1):
        off = 8 - (CONV_WIDTH - 1) + j
        acc = acc + cw_ref[j:j + 1, :] * ubuf_ref[bb, off:off + L, :]
    ubuf_ref[bb, 0:8, :] = ubuf_ref[bb, L:L + 8, :]
    qk = acc * _sigmoid(acc)
    q = qk[:, :M_QK_WIDTH].astype(BF16)
    k = qk[:, M_QK_WIDTH:] * (M_QK_DIM ** -0.5)

    gc = gcol_ref[bb] + bcol_ref[...]
    gr = grow_ref[...] + brow_ref[...]
    ti = lax.broadcasted_iota(jnp.int32, (L, L), 0)
    si = lax.broadcasted_iota(jnp.int32, (L, L), 1)
    causal = si <= ti
    tril = jnp.where(causal, 1.0, 0.0).astype(F32)
    triu = jnp.where(ti <= si, 1.0, 0.0).astype(F32)
    b_c = jnp.dot(tril, _log_sigmoid(gc), preferred_element_type=F32,
                  precision=lax.Precision.HIGHEST)
    b_r = jnp.dot(_log_sigmoid(gr), triu, preferred_element_type=F32,
                  precision=lax.Precision.HIGHEST)

    ext = jnp.where(lax.broadcasted_iota(jnp.int32, (L, LANES), 1) == 0, 1.0, 0.0).astype(BF16)
    outs = []
    for h in range(M_HEADS):
        bc = b_c[:, M_HEADS + h:M_HEADS + h + 1]
        br = b_r[M_HEADS + h:M_HEADS + h + 1, :]
        ir = gr[h:h + 1, :]
        ic = gc[:, h:h + 1]
        m_prev = m_ref[bb * 8 + h:bb * 8 + h + 1, 0:1]
        dmat = jnp.where(causal, bc - br + ir, NEG)
        a = bc + m_prev
        mt = jnp.maximum(a, jnp.max(dmat, axis=-1, keepdims=True))
        wq = jnp.exp(dmat - mt)
        wa = jnp.exp(a - mt)
        qh = q[:, h * M_QK_DIM:(h + 1) * M_QK_DIM]
        kh = k[:, h * M_QK_DIM:(h + 1) * M_QK_DIM]
        sw = lax.dot_general(qh, kh.astype(BF16), (((1,), (1,)), ((), ())),
                             preferred_element_type=F32) * wq
        vext = jnp.concatenate([mv_ref[bb, :, h * M_V_DIM:(h + 1) * M_V_DIM], ext], axis=-1)
        cprev = cext_ref[bb * M_HEADS + h]
        res = wa * jnp.dot(qh, cprev.astype(BF16), preferred_element_type=F32) \
            + jnp.dot(sw.astype(BF16), vext, preferred_element_type=F32)
        num = res[:, :M_V_DIM]
        den = res[:, M_V_DIM:M_V_DIM + 1]
        hb = num / jnp.maximum(jnp.abs(den), jnp.exp(-mt))
        m_new = mt[L - 1:L, :]
        bl = bc[L - 1:L, :]
        wc = jnp.exp(bl + m_prev - m_new)
        ws = jnp.exp(bl - bc + ic - m_new)
        kw = (kh * ws).astype(BF16)
        cext_ref[bb * M_HEADS + h] = wc * cprev + lax.dot_general(kw, vext, (((0,), (0,)), ((), ())),
                                                                  preferred_element_type=F32)
        m_ref[bb * 8 + h:bb * 8 + h + 1, :] = jnp.broadcast_to(m_new, (1, LANES))
        hn = hb * lax.rsqrt(jnp.mean(hb * hb, axis=-1, keepdims=True) + EPS)
        hn = hn * nw_ref[:, h * M_V_DIM:(h + 1) * M_V_DIM]
        og = _sigmoid(mo_ref[bb, :, h * M_V_DIM:(h + 1) * M_V_DIM].astype(F32))
        outs.append(og * hn)
    y_ref[bb] = jnp.concatenate(outs, axis=-1).astype(BF16)


def _mlstm(mqk3, mv3, mo3, gcol3, grow, conv_w, conv_b, bias_col, bias_row, norm_w):
    B, S, _ = mqk3.shape
    L = M_CHUNK
    nb = M_BATCH
    nc = S // L
    tok = lambda b, c: (b, c, 0)
    const = lambda b, c: (0, 0)
    grow_specs = [pl.BlockSpec((8, L), functools.partial(lambda b, c, n: (0, (b * nb + n) * nc + c), n=n))
                  for n in range(nb)]
    return pl.pallas_call(
        _mlstm_kernel,
        grid=(B // nb, nc),
        in_specs=[pl.BlockSpec((nb, L, 2 * M_QK_WIDTH), tok),
                  pl.BlockSpec((nb, L, M_V_WIDTH), tok),
                  pl.BlockSpec((nb, L, M_V_WIDTH), tok),
                  pl.BlockSpec((nb, L, LANES), tok),
                  *grow_specs,
                  pl.BlockSpec((CONV_WIDTH, 2 * M_QK_WIDTH), const),
                  pl.BlockSpec((1, 2 * M_QK_WIDTH), const),
                  pl.BlockSpec((1, LANES), const),
                  pl.BlockSpec((8, 1), const),
                  pl.BlockSpec((1, M_V_WIDTH), const)],
        out_specs=pl.BlockSpec((nb, L, M_V_WIDTH), tok),
        out_shape=jax.ShapeDtypeStruct((B, S, M_V_WIDTH), BF16),
        scratch_shapes=[pltpu.VMEM((nb * M_HEADS, M_QK_DIM, 2 * M_V_DIM), F32),
                        pltpu.VMEM((nb * 8, LANES), F32),
                        pltpu.VMEM((nb, L + 8, 2 * M_QK_WIDTH), F32)],
        compiler_params=pltpu.CompilerParams(dimension_semantics=("arbitrary", "arbitrary"),
                                             vmem_limit_bytes=VMEM_LIMIT),
        name="mlstm",
    )(mqk3, mv3, mo3, gcol3, *([grow] * nb), conv_w, conv_b, bias_col, bias_row, norm_w)


def _outproj_kernel(x_ref, ya_ref, ym_ref, gab_ref, wa_ref, wm_ref, wo_ref, nw_ref, wr_ref, br_ref,
                    x1_ref, h2_ref, meta_ref, code_ref, cnt_ref, cnt_scr):
    tm = TOK_TILE
    i = pl.program_id(0)

    @pl.when(i == 0)
    def _():
        cnt_scr[...] = jnp.zeros_like(cnt_scr)

    pa = jnp.dot(ya_ref[...], wa_ref[...], preferred_element_type=F32)
    pm = jnp.dot(ym_ref[...], wm_ref[...], preferred_element_type=F32)
    ga = gab_ref[:, :D_MODEL].astype(F32)
    gb = gab_ref[:, D_MODEL:].astype(F32)
    mix = _sigmoid(ga) * pa + _sigmoid(gb) * pm
    x1 = x_ref[...] + jnp.dot(mix.astype(BF16), wo_ref[...], preferred_element_type=F32)
    x1_ref[...] = x1
    h2 = _rms(x1, nw_ref[...])
    h2_ref[...] = _pack_rows(h2)

    h_hi = h2.astype(BF16)
    h_lo = (h2 - h_hi.astype(F32)).astype(BF16)
    hw = jnp.dot(h_hi, wr_ref[...], preferred_element_type=F32)
    logits = (hw[:, :LANES] + hw[:, LANES:]
              + jnp.dot(h_lo, wr_ref[:, :LANES], preferred_element_type=F32)) + br_ref[...]
    lane = lax.broadcasted_iota(jnp.int32, (tm, LANES), 1)
    big = jnp.int32(LANES)

    def first_argmax(v):
        mx = jnp.max(v, axis=-1, keepdims=True)
        idx = jnp.min(jnp.where(v == mx, lane, big), axis=-1, keepdims=True)
        return mx, idx

    gl = jnp.where(lane < N_GROUPS, logits, NEG)
    gmax, gi = first_argmax(gl)
    gp = 1.0 / jnp.sum(jnp.exp(gl - gmax), axis=-1, keepdims=True)
    lo = N_GROUPS + gi * EXPERTS_PER_GROUP
    el = jnp.where((lane >= lo) & (lane < lo + EXPERTS_PER_GROUP), logits, NEG)
    v1, j1 = first_argmax(el)
    v2, j2 = first_argmax(jnp.where(lane == j1, NEG, el))
    t = jnp.exp(v2 - v1)
    w1 = gp / (1.0 + t)
    w2 = gp * t / (1.0 + t)

    hit1 = lane == j1
    hit2 = lane == j2
    onehot = jnp.where(hit1 | hit2, 1.0, 0.0)
    ri = lax.broadcasted_iota(jnp.int32, (tm, tm), 0)
    ci = lax.broadcasted_iota(jnp.int32, (tm, tm), 1)
    before = jnp.where(ci < ri, 1.0, 0.0).astype(BF16)
    rank = jnp.dot(before, onehot.astype(BF16), preferred_element_type=F32) + cnt_scr[0:1, :]
    r1 = jnp.sum(jnp.where(hit1, rank, 0.0), axis=-1, keepdims=True)
    r2 = jnp.sum(jnp.where(hit2, rank, 0.0), axis=-1, keepdims=True)
    cnt = cnt_scr[0:1, :] + jnp.sum(onehot, axis=0, keepdims=True)
    cnt_scr[...] = jnp.broadcast_to(cnt, cnt_scr.shape)
    cnt_ref[...] = jnp.broadcast_to(cnt, cnt_ref.shape)

    c1 = (j1 - N_GROUPS).astype(F32) * RANK_RADIX + r1
    c2 = (j2 - N_GROUPS).astype(F32) * RANK_RADIX + r2
    meta = jnp.where(lane == 0, c1, 0.0)
    meta = jnp.where(lane == 1, c2, meta)
    meta = jnp.where(lane == 4, w1, meta)
    meta = jnp.where(lane == 5, w2, meta)
    meta_ref[...] = meta
    code_ref[...] = meta.T[0:8, :].astype(jnp.int32)


def _outproj(x2, ya, ym, gab, wa, wm, wo, norm_w, w_rt, b_rt):
    T = x2.shape[0]
    tm = TOK_TILE
    row = lambda i: (i, 0)
    const = lambda i: (0, 0)
    return pl.pallas_call(
        _outproj_kernel,
        grid=(T // tm,),
        in_specs=[pl.BlockSpec((tm, D_MODEL), row),
                  pl.BlockSpec((tm, ATT_Q_WIDTH), row),
                  pl.BlockSpec((tm, M_V_WIDTH), row),
                  pl.BlockSpec((tm, 2 * D_MODEL), row),
                  pl.BlockSpec((ATT_Q_WIDTH, D_MODEL), const),
                  pl.BlockSpec((M_V_WIDTH, D_MODEL), const),
                  pl.BlockSpec((D_MODEL, D_MODEL), const),
                  pl.BlockSpec((1, D_MODEL), const),
                  pl.BlockSpec((D_MODEL, 2 * LANES), const),
                  pl.BlockSpec((1, LANES), const)],
        out_specs=[pl.BlockSpec((tm, D_MODEL), row),
                   pl.BlockSpec((tm, PACKED), row),
                   pl.BlockSpec((tm, LANES), row),
                   pl.BlockSpec((8, tm), lambda i: (0, i)),
                   pl.BlockSpec((8, LANES), const)],
        out_shape=[jax.ShapeDtypeStruct((T, D_MODEL), F32),
                   jax.ShapeDtypeStruct((T, PACKED), jnp.int32),
                   jax.ShapeDtypeStruct((T, LANES), F32),
                   jax.ShapeDtypeStruct((8, T), jnp.int32),
                   jax.ShapeDtypeStruct((8, LANES), F32)],
        scratch_shapes=[pltpu.VMEM((8, LANES), F32)],
        compiler_params=pltpu.CompilerParams(dimension_semantics=("arbitrary",),
                                             vmem_limit_bytes=VMEM_LIMIT),
        name="outproj",
    )(x2, ya, ym, gab, wa, wm, wo, norm_w, w_rt, b_rt)


def _slots_kernel(cnt_ref, pstart_ref, blk_e_ref, nvalid_ref, nused_ref):
    R = SLOT_BLOCK
    nblk = blk_e_ref.shape[0]

    def per_expert(e, run):
        c = cnt_ref[e]
        end = run + ((c + R - 1) // R) * R
        pstart_ref[e] = run

        def set_blk(b, carry):
            blk_e_ref[b] = e
            nvalid_ref[b] = jnp.minimum(run + c - b * R, R)
            return carry
        lax.fori_loop(run // R, end // R, set_blk, 0)
        return end

    total = lax.fori_loop(0, N_EXPERTS, per_expert, 0)
    nused_ref[0] = total // R

    def tail_blk(b, carry):
        blk_e_ref[b] = N_EXPERTS - 1
        nvalid_ref[b] = 0
        return carry
    lax.fori_loop(total // R, nblk, tail_blk, 0)


def _slots(counts, nblk):
    smem = pl.BlockSpec(memory_space=pltpu.SMEM)
    return pl.pallas_call(
        _slots_kernel,
        in_specs=[smem],
        out_specs=[smem, smem, smem, smem],
        out_shape=[jax.ShapeDtypeStruct((N_EXPERTS,), jnp.int32),
                   jax.ShapeDtypeStruct((nblk,), jnp.int32),
                   jax.ShapeDtypeStruct((nblk,), jnp.int32),
                   jax.ShapeDtypeStruct((1,), jnp.int32)],
        name="slots",
    )(counts)


def _experts_kernel(blk_e_ref, nvalid_ref, nused_ref, xs_ref, wg_ref, wu_ref, wd_ref, ys_ref,
                    wg_bf, wu_bf, wd_bf):
    j = pl.program_id(0)
    R = SLOT_BLOCK
    valid = j < nused_ref[0]
    changed = jnp.logical_or(j == 0, blk_e_ref[j] != blk_e_ref[jnp.maximum(j - 1, 0)])

    @pl.when(jnp.logical_and(valid, changed))
    def _():
        wg_bf[...] = wg_ref[0].astype(BF16)
        wu_bf[...] = wu_ref[0].astype(BF16)
        wd_bf[...] = wd_ref[0].astype(BF16)

    @pl.when(valid)
    def _():
        rows = lax.broadcasted_iota(jnp.int32, (R, PACKED), 0)
        xw = jnp.where(rows < nvalid_ref[j], xs_ref[...], 0)
        xb = _unpack_rows(xw).astype(BF16)
        g = jnp.dot(xb, wg_bf[...], preferred_element_type=F32)
        u = jnp.dot(xb, wu_bf[...], preferred_element_type=F32)
        act = (g * _sigmoid(g) * u).astype(BF16)
        ys_ref[...] = _pack_rows(jnp.dot(act, wd_bf[...], preferred_element_type=F32))

    @pl.when(jnp.logical_not(valid))
    def _():
        ys_ref[...] = jnp.zeros_like(ys_ref)


def _experts(blk_e, nvalid, nused, xs, w_gate, w_up, w_down):
    nblk = blk_e.shape[0]
    R = SLOT_BLOCK
    wspec = lambda shape: pl.BlockSpec((1,) + shape, lambda j, be, nv, nu: (be[j], 0, 0))
    grid_spec = pltpu.PrefetchScalarGridSpec(
        num_scalar_prefetch=3,
        grid=(nblk,),
        in_specs=[pl.BlockSpec((R, PACKED), lambda j, be, nv, nu: (jnp.minimum(j, nu[0] - 1), 0)),
                  wspec((D_MODEL, D_EXPERT)),
                  wspec((D_MODEL, D_EXPERT)),
                  wspec((D_EXPERT, D_MODEL))],
        out_specs=pl.BlockSpec((R, PACKED), lambda j, be, nv, nu: (j, 0)),
        scratch_shapes=[pltpu.VMEM((D_MODEL, D_EXPERT), BF16),
                        pltpu.VMEM((D_MODEL, D_EXPERT), BF16),
                        pltpu.VMEM((D_EXPERT, D_MODEL), BF16)],
    )
    return pl.pallas_call(
        _experts_kernel,
        grid_spec=grid_spec,
        out_shape=jax.ShapeDtypeStruct((nblk * R, PACKED), jnp.int32),
        compiler_params=pltpu.CompilerParams(dimension_semantics=("arbitrary",),
                                             vmem_limit_bytes=VMEM_LIMIT),
        name="experts",
    )(blk_e, nvalid, nused, xs, w_gate, w_up, w_down)


def _dest_kernel(pstart_ref, code_ref, dest_ref):
    c = code_ref[...]
    e = c >> RANK_BITS
    base = jnp.zeros_like(c)
    for k in range(N_EXPERTS):
        base = jnp.where(e == k, pstart_ref[k], base)
    dest_ref[...] = base + (c & (RANK_RADIX - 1))


def _dests(pstart, code_rows):
    T = code_rows.shape[1]
    tc = min(4096, T)
    return pl.pallas_call(
        _dest_kernel,
        grid=(T // tc,),
        in_specs=[pl.BlockSpec(memory_space=pltpu.SMEM),
                  pl.BlockSpec((8, tc), lambda i: (0, i))],
        out_specs=pl.BlockSpec((8, tc), lambda i: (0, i)),
        out_shape=jax.ShapeDtypeStruct((8, T), jnp.int32),
        name="dests",
    )(pstart, code_rows)


def _sc_move_rows(src, idx, n_out, scatter):
    M = idx.shape[0]
    n_src, width = src.shape
    info = plsc.get_sparse_core_info()
    nc, nw = info.num_cores, info.num_cores * info.num_subcores
    per_w = M // nw
    ch = SC_CHUNK
    nch = per_w // ch
    assert per_w * nw == M and nch * ch == per_w and nch % 2 == 0 and n_src % per_w == 0
    mesh = plsc.VectorSubcoreMesh(core_axis_name="c", subcore_axis_name="s")

    @functools.partial(
        pl.kernel, mesh=mesh, out_type=jax.ShapeDtypeStruct((n_out, width), src.dtype),
        scratch_types=[pltpu.VMEM((per_w,), jnp.int32),
                       pltpu.VMEM((ch, width), src.dtype), pltpu.VMEM((ch, width), src.dtype),
                       pltpu.SemaphoreType.DMA, pltpu.SemaphoreType.DMA,
                       pltpu.SemaphoreType.DMA, pltpu.SemaphoreType.DMA],
        name="sc_scatter" if scatter else "sc_gather")
    def move(src_hbm, idx_hbm, out_hbm, idx_v, buf0, buf1, in0, in1, out0, out1):
        wid = lax.axis_index("s") * nc + lax.axis_index("c")
        base = wid * per_w
        pltpu.sync_copy(idx_hbm.at[pl.ds(base, per_w)], idx_v)
        lin_base = lax.rem(base, n_src) if scatter else base

        def read(c, buf, sem):
            if scatter:
                return pltpu.make_async_copy(src_hbm.at[pl.ds(lin_base + c * ch, ch)], buf, sem)
            return pltpu.make_async_copy(src_hbm.at[idx_v.at[pl.ds(c * ch, ch)]], buf, sem)

        def write(c, buf, sem):
            if scatter:
                return pltpu.make_async_copy(buf, out_hbm.at[idx_v.at[pl.ds(c * ch, ch)]], sem)
            return pltpu.make_async_copy(buf, out_hbm.at[pl.ds(lin_base + c * ch, ch)], sem)

        read(0, buf0, in0).start()

        @pl.loop(0, nch, step=2)
        def _(c):
            @pl.when(c > 0)
            def _():
                write(c - 1, buf1, out1).wait()
            read(c + 1, buf1, in1).start()
            read(c, buf0, in0).wait()
            write(c, buf0, out0).start()
            read(c + 1, buf1, in1).wait()
            write(c + 1, buf1, out1).start()
            write(c, buf0, out0).wait()

            @pl.when(c + 2 < nch)
            def _():
                read(c + 2, buf0, in0).start()

        write(nch - 1, buf1, out1).wait()

    return move(src, idx)


def _final_kernel(x1_ref, y1_ref, y2_ref, meta_ref, nw_ref, o_ref):
    w1 = meta_ref[:, 4:5]
    w2 = meta_ref[:, 5:6]
    x2 = x1_ref[...] + (w1 * _unpack_rows(y1_ref[...]) + w2 * _unpack_rows(y2_ref[...]))
    o_ref[...] = _rms(x2, nw_ref[...])


def _final(x1, yg, meta, norm_w):
    T = x1.shape[0]
    tm = COMB_TILE
    nt = T // tm
    row = lambda i: (i, 0)
    return pl.pallas_call(
        _final_kernel,
        grid=(nt,),
        in_specs=[pl.BlockSpec((tm, D_MODEL), row),
                  pl.BlockSpec((tm, PACKED), row),
                  pl.BlockSpec((tm, PACKED), lambda i: (i + nt, 0)),
                  pl.BlockSpec((tm, LANES), row),
                  pl.BlockSpec((1, D_MODEL), lambda i: (0, 0))],
        out_specs=pl.BlockSpec((tm, D_MODEL), row),
        out_shape=jax.ShapeDtypeStruct((T, D_MODEL), F32),
        compiler_params=pltpu.CompilerParams(dimension_semantics=("arbitrary",),
                                             vmem_limit_bytes=VMEM_LIMIT),
        name="final",
    )(x1, yg, yg, meta, norm_w)


def _layer(x, norm_mix_w, w_in, conv_w, conv_b, b_igate, b_fgate, attn_sinks, mlstm_norm_w,
           w_attn_o, w_mlstm_o, w_out, norm_ffn_w, w_group, b_group, w_router, b_router,
           w_gate, w_up, w_down, out_norm_w):
    B, S, D = x.shape
    T = B * S
    x2 = x.reshape(T, D)

    w_main = jnp.concatenate([w_in[:, :_O_MI], w_in[:, _O_GA:]], axis=1).astype(BF16)
    w_gates = jnp.pad(w_in[:, _O_MI:_O_GA], ((0, 0), (0, LANES - 2 * M_HEADS))).astype(BF16)
    q, kv, mqk, mv, mo, gcol, grow, gab = _inproj(x2, norm_mix_w.reshape(1, D), w_main, w_gates)

    ya = _attn(attn_sinks.astype(F32), q.reshape(B, S, -1), kv.reshape(B, S, -1))

    bias = jnp.concatenate([b_igate, b_fgate]).astype(F32)
    bias_col = jnp.pad(bias, (0, LANES - 2 * M_HEADS)).reshape(1, LANES)
    ym = _mlstm(mqk.reshape(B, S, -1), mv.reshape(B, S, -1), mo.reshape(B, S, -1),
                gcol.reshape(B, S, LANES), grow, conv_w.astype(F32), conv_b.reshape(1, -1).astype(F32),
                bias_col, bias.reshape(2 * M_HEADS, 1), mlstm_norm_w.reshape(1, -1).astype(F32))

    w_rt = jnp.pad(jnp.concatenate([w_group, w_router], axis=1),
                   ((0, 0), (0, LANES - N_GROUPS - N_EXPERTS))).astype(F32)
    w_rt_hi = w_rt.astype(BF16)
    w_rt = jnp.concatenate([w_rt_hi, (w_rt - w_rt_hi.astype(F32)).astype(BF16)], axis=1)
    b_rt =jnp.pad(jnp.concatenate([b_group, b_router]), (0, LANES - N_GROUPS - N_EXPERTS)).reshape(1, LANES)
    x1, h2p, meta, code_rows, cnt = _outproj(x2, ya.reshape(T, -1), ym.reshape(T, -1), gab,
                                            w_attn_o.astype(BF16), w_mlstm_o.astype(BF16),
                                            w_out.astype(BF16), norm_ffn_w.reshape(1, D), w_rt,
                                            b_rt.astype(F32))

    counts = cnt[0, N_GROUPS:N_GROUPS + N_EXPERTS].astype(jnp.int32)
    nblk = (2 * T) // SLOT_BLOCK + N_EXPERTS
    pstart, blk_e, nvalid, nused = _slots(counts, nblk)
    dest = _dests(pstart, code_rows)[0:2].reshape(2 * T)

    xs = _sc_move_rows(h2p, dest, nblk * SLOT_BLOCK, scatter=True)
    ys = _experts(blk_e, nvalid, nused, xs, w_gate, w_up, w_down)
    yg = _sc_move_rows(ys, dest, 2 * T, scatter=False)
    out = _final(x1, yg, meta, out_norm_w.reshape(1, D))
    return out.reshape(B, S, D)


def kernel(x, norm_mix_w, w_in, conv_w, conv_b, b_igate, b_fgate, attn_sinks, mlstm_norm_w, w_attn_o,
           w_mlstm_o, w_out, norm_ffn_w, w_group, b_group, w_router, b_router, w_gate, w_up, w_down,
           norm_final_w):
    depth = w_in.shape[0]
    assert depth == 1, "final RMSNorm is fused into the last layer's combine kernel"
    return _layer(x, norm_mix_w[0], w_in[0], conv_w[0], conv_b[0], b_igate[0], b_fgate[0],
                  attn_sinks[0], mlstm_norm_w[0], w_attn_o[0], w_mlstm_o[0], w_out[0], norm_ffn_w[0],
                  w_group[0], b_group[0], w_router[0], b_router[0], w_gate[0], w_up[0], w_down[0],
                  norm_final_w)
```

```python
import functools

import jax
import jax.numpy as jnp
from jax import lax
from jax.experimental import pallas as pl
from jax.experimental.pallas import tpu as pltpu
from jax.experimental.pallas import tpu_sc as plsc

F32 = jnp.float32
BF16 = jnp.bfloat16

D_MODEL = 1024
N_Q_HEADS = 8
N_KV_HEADS = 2
HEAD_DIM = 64
WINDOW = 128
GQA_GROUP = N_Q_HEADS // N_KV_HEADS
M_HEADS = 4
M_QK_DIM = 64
M_V_DIM = 128
CONV_WIDTH = 4
N_GROUPS = 4
EXPERTS_PER_GROUP = 8
N_EXPERTS = N_GROUPS * EXPERTS_PER_GROUP
D_EXPERT = 512
EPS = 1e-6

ATT_Q_WIDTH = N_Q_HEADS * HEAD_DIM
ATT_KV_WIDTH = N_KV_HEADS * HEAD_DIM
M_QK_WIDTH = M_HEADS * M_QK_DIM
M_V_WIDTH = M_HEADS * M_V_DIM

LANES = 128
NEG = -1e30
VMEM_LIMIT = 56 * 1024 * 1024

TOK_TILE = 512
ATT_TILE = 256
M_CHUNK = 128
M_BATCH = 8
M_STATE_ROWS = M_V_DIM + 16
SLOT_BLOCK = 512
COMB_TILE = 512
RANK_RADIX = 65536
RANK_BITS = 16
PACKED = D_MODEL // 2
SC_CHUNK = 64
DMA_THREADS = 2

_O_AQ = 0
_O_AK = _O_AQ + ATT_Q_WIDTH
_O_AV = _O_AK + ATT_KV_WIDTH
_O_MQ = _O_AV + ATT_KV_WIDTH
_O_MK = _O_MQ + M_QK_WIDTH
_O_MV = _O_MK + M_QK_WIDTH
_O_MO = _O_MV + M_V_WIDTH
_O_MI = _O_MO + M_V_WIDTH
_O_MF = _O_MI + M_HEADS
_O_GA = _O_MF + M_HEADS
_O_GB = _O_GA + D_MODEL
_O_END = _O_GB + D_MODEL


def _rms(x, w):
    return x * lax.rsqrt(jnp.mean(x * x, axis=-1, keepdims=True) + EPS) * w


def _sigmoid(x):
    return 0.5 * jnp.tanh(0.5 * x) + 0.5


def _pack_rows(x):
    half = x.shape[1] // 2
    bits = lax.bitcast_convert_type(x.astype(BF16).astype(F32), jnp.uint32)
    packed = (bits[:, :half] >> 16) | (bits[:, half:] & jnp.uint32(0xFFFF0000))
    return lax.bitcast_convert_type(packed, jnp.int32)


def _unpack_rows(w):
    u = lax.bitcast_convert_type(w, jnp.uint32)
    lo = lax.bitcast_convert_type(u << 16, F32)
    hi = lax.bitcast_convert_type(u & jnp.uint32(0xFFFF0000), F32)
    return jnp.concatenate([lo, hi], axis=-1)


def _log_sigmoid(x):
    return jnp.minimum(x, 0.0) - jnp.log1p(jnp.exp(-jnp.abs(x)))


def _inproj_kernel(x_ref, nw_ref, w_ref, wt_ref, wg_ref,
                   q_ref, kv_ref, mk_ref, gab_ref, mqt_ref, mvt_ref, mot_ref, grow_ref):
    h = _rms(x_ref[...], nw_ref[...]).astype(BF16)

    def proj(lo, hi):
        return jnp.dot(h, w_ref[:, lo:hi], preferred_element_type=F32).astype(BF16)

    def proj_t(lo, hi):
        return lax.dot_general(wt_ref[lo:hi, :], h, (((1,), (1,)), ((), ())),
                               preferred_element_type=F32).astype(BF16)

    o_kv = ATT_Q_WIDTH
    o_mk = o_kv + 2 * ATT_KV_WIDTH
    o_g = o_mk + M_QK_WIDTH
    q_ref[...] = proj(0, o_kv)
    kv_ref[...] = proj(o_kv, o_mk)
    mk_ref[...] = proj(o_mk, o_g)
    gab_ref[...] = proj(o_g, o_g + 2 * D_MODEL)
    mqt_ref[0] = proj_t(0, M_QK_WIDTH)
    mvt_ref[0] = proj_t(M_QK_WIDTH, M_QK_WIDTH + M_V_WIDTH)
    mot_ref[0] = proj_t(M_QK_WIDTH + M_V_WIDTH, M_QK_WIDTH + 2 * M_V_WIDTH)
    g = jnp.dot(h, wg_ref[...], preferred_element_type=F32)
    grow_ref[...] = g.T[0:8, :]


def _inproj(x2, norm_w, w_tok, w_chan, w_gates, B, S):
    T = x2.shape[0]
    tm = TOK_TILE
    tpb = S // tm
    row = lambda i: (i, 0)
    const = lambda i: (0, 0)
    chan = lambda i: (i // tpb, 0, i % tpb)
    return pl.pallas_call(
        _inproj_kernel,
        grid=(T // tm,),
        in_specs=[pl.BlockSpec((tm, D_MODEL), row),
                  pl.BlockSpec((1, D_MODEL), const),
                  pl.BlockSpec(w_tok.shape, const),
                  pl.BlockSpec(w_chan.shape, const),
                  pl.BlockSpec((D_MODEL, LANES), const)],
        out_specs=[pl.BlockSpec((tm, ATT_Q_WIDTH), row),
                   pl.BlockSpec((tm, 2 * ATT_KV_WIDTH), row),
                   pl.BlockSpec((tm, M_QK_WIDTH), row),
                   pl.BlockSpec((tm, 2 * D_MODEL), row),
                   pl.BlockSpec((1, M_QK_WIDTH, tm), chan),
                   pl.BlockSpec((1, M_V_WIDTH, tm), chan),
                   pl.BlockSpec((1, M_V_WIDTH, tm), chan),
                   pl.BlockSpec((8, tm), lambda i: (0, i))],
        out_shape=[jax.ShapeDtypeStruct((T, ATT_Q_WIDTH), BF16),
                   jax.ShapeDtypeStruct((T, 2 * ATT_KV_WIDTH), BF16),
                   jax.ShapeDtypeStruct((T, M_QK_WIDTH), BF16),
                   jax.ShapeDtypeStruct((T, 2 * D_MODEL), BF16),
                   jax.ShapeDtypeStruct((B, M_QK_WIDTH, S), BF16),
                   jax.ShapeDtypeStruct((B, M_V_WIDTH, S), BF16),
                   jax.ShapeDtypeStruct((B, M_V_WIDTH, S), BF16),
                   jax.ShapeDtypeStruct((8, T), F32)],
        compiler_params=pltpu.CompilerParams(dimension_semantics=("arbitrary",),
                                             vmem_limit_bytes=VMEM_LIMIT),
        name="inproj",
    )(x2, norm_w, w_tok, w_chan, w_gates)


def _attn_kernel(sink_ref, q_ref, kv_ref, kvp_ref, o_ref):
    i = pl.program_id(1)
    W = WINDOW
    r = lax.broadcasted_iota(jnp.int32, (W, 2 * W), 0)
    c = lax.broadcasted_iota(jnp.int32, (W, 2 * W), 1)
    band = (c > r) & (c <= r + W)
    for j in range(ATT_TILE // W):
        qj = q_ref[0, j * W:(j + 1) * W, :]
        if j == 0:
            kprev = kvp_ref[0]
            valid = band & ((i > 0) | (c >= W))
        else:
            kprev = kv_ref[0, (j - 1) * W:j * W, :]
            valid = band
        kwin = jnp.concatenate([kprev, kv_ref[0, j * W:(j + 1) * W, :]], axis=0)
        outs = []
        for h in range(N_Q_HEADS):
            g = h // GQA_GROUP
            qh = qj[:, h * HEAD_DIM:(h + 1) * HEAD_DIM]
            kg = kwin[:, g * HEAD_DIM:(g + 1) * HEAD_DIM]
            vg = kwin[:, ATT_KV_WIDTH + g * HEAD_DIM:ATT_KV_WIDTH + (g + 1) * HEAD_DIM]
            s = lax.dot_general(qh, kg, (((1,), (1,)), ((), ())),
                                preferred_element_type=F32) * (HEAD_DIM ** -0.5)
            s = jnp.where(valid, s, NEG)
            sink = sink_ref[h]
            m = jnp.maximum(jnp.max(s, axis=-1, keepdims=True), sink)
            p = jnp.exp(s - m)
            l = jnp.sum(p, axis=-1, keepdims=True) + jnp.exp(sink - m)
            o = jnp.dot(p.astype(BF16), vg, preferred_element_type=F32)
            outs.append(o / l)
        o_ref[0, j * W:(j + 1) * W, :] = jnp.concatenate(outs, axis=-1).astype(BF16)


def _attn(sinks, q3, kv3):
    B, S, _ = q3.shape
    tq = ATT_TILE
    per = tq // WINDOW
    return pl.pallas_call(
        _attn_kernel,
        grid=(B, S // tq),
        in_specs=[pl.BlockSpec(memory_space=pltpu.SMEM),
                  pl.BlockSpec((1, tq, ATT_Q_WIDTH), lambda b, i: (b, i, 0)),
                  pl.BlockSpec((1, tq, 2 * ATT_KV_WIDTH), lambda b, i: (b, i, 0)),
                  pl.BlockSpec((1, WINDOW, 2 * ATT_KV_WIDTH),
                               lambda b, i: (b, jnp.maximum(i * per - 1, 0), 0))],
        out_specs=pl.BlockSpec((1, tq, ATT_Q_WIDTH), lambda b, i: (b, i, 0)),
        out_shape=jax.ShapeDtypeStruct((B, S, ATT_Q_WIDTH), BF16),
        compiler_params=pltpu.CompilerParams(dimension_semantics=("arbitrary", "arbitrary"),
                                             vmem_limit_bytes=VMEM_LIMIT),
        name="attn",
    )(sinks, q3, kv3, kv3)


def _mlstm_kernel(nb, *refs):
    L = M_CHUNK
    H = M_HEADS
    mqt_ref, mk_ref, mvt_ref, mot_ref = refs[:4]
    grow_refs = refs[4:4 + nb]
    (cwq_ref, cbq_ref, cwk_ref, cbk_ref, brow_ref, nw_ref, y_ref,
     state_ref, m_ref, prevq_ref, ubuf_ref) = refs[4 + nb:]
    cidx = pl.program_id(1)

    @pl.when(cidx == 0)
    def _():
        state_ref[...] = jnp.zeros_like(state_ref)
        m_ref[...] = jnp.zeros_like(m_ref)
        prevq_ref[...] = jnp.zeros_like(prevq_ref)
        ubuf_ref[:, 0:8, :] = jnp.zeros((nb, 8, M_QK_WIDTH), F32)

    ri = lax.broadcasted_iota(jnp.int32, (L, L), 0)
    ci = lax.broadcasted_iota(jnp.int32, (L, L), 1)
    causal_t = ri <= ci
    triu = jnp.where(causal_t, 1.0, 0.0).astype(F32)
    lane = lax.broadcasted_iota(jnp.int32, (8, L), 1)
    r2 = lax.broadcasted_iota(jnp.int32, (2 * L, L), 0)
    c2 = lax.broadcasted_iota(jnp.int32, (2 * L, L), 1)
    shifts = [jnp.where(r2 - c2 == L - k, 1.0, 0.0).astype(BF16) for k in range(1, CONV_WIDTH)]
    ones_rows = jnp.where(lax.broadcasted_iota(jnp.int32, (M_STATE_ROWS - M_V_DIM, L), 0) == 0,
                          1.0, 0.0).astype(BF16)

    pairs = [(bb, h) for bb in range(nb) for h in range(H)]
    states = [state_ref[bb * H + h] for bb, h in pairs]
    m_prevs = [m_ref[bb] for bb in range(nb)]

    seqs = []
    for bb in range(nb):
        cur = mqt_ref[bb]
        both = jnp.concatenate([prevq_ref[bb], cur], axis=1)
        acc = cbq_ref[...] + cwq_ref[CONV_WIDTH - 1] * cur.astype(F32)
        for k in range(1, CONV_WIDTH):
            acc = acc + cwq_ref[CONV_WIDTH - 1 - k] * jnp.dot(both, shifts[k - 1],
                                                              preferred_element_type=F32)
        qt = (acc * _sigmoid(acc)).astype(BF16)
        ubuf_ref[bb, 8:L + 8, :] = mk_ref[bb].astype(F32)
        acc = cbk_ref[...] + cwk_ref[CONV_WIDTH - 1:CONV_WIDTH, :] * ubuf_ref[bb, 8:L + 8, :]
        for j in range(CONV_WIDTH - 1):
            off = 8 - (CONV_WIDTH - 1) + j
            acc = acc + cwk_ref[j:j + 1, :] * ubuf_ref[bb, off:off + L, :]
        ubuf_ref[bb, 0:8, :] = ubuf_ref[bb, L:L + 8, :]
        kk =(acc * _sigmoid(acc) * (M_QK_DIM ** -0.5)).astype(BF16)

        gr = grow_refs[bb][...] + brow_ref[...]
        b = jnp.dot(_log_sigmoid(gr), triu, preferred_element_type=F32,
                    precision=lax.Precision.HIGHEST)
        gi = pltpu.roll(gr, 4, axis=0)
        u = gi - b
        cm = u
        for sh in (1, 2, 4, 8, 16, 32, 64):
            if sh < L:
                cm = jnp.maximum(cm, jnp.where(lane >= sh, pltpu.roll(cm, sh, axis=1), NEG))
        m_prev = m_prevs[bb]
        mt = b + jnp.maximum(m_prev, cm)
        wa = jnp.exp(b + m_prev - mt)
        emt = jnp.exp(-mt)
        m_new = jnp.broadcast_to(mt[:, L - 1:L], (8, L))
        b_last = jnp.broadcast_to(b[:, L - 1:L], (8, L))
        wc = jnp.exp(b_last + m_prev - m_new)
        ws = jnp.exp(b_last + u - m_new)
        seqs.append((qt, kk, b - mt, u.T, wa, emt, wc, ws, m_new))

    new_states, outs = [], []
    for (bb, h), state in zip(pairs, states):
        qt, kk, bmt, ucol, wa, emt, wc, ws, _ = seqs[bb]
        row = slice(H + h, H + h + 1)
        qh = qt[h * M_QK_DIM:(h + 1) * M_QK_DIM, :]
        kh = kk[:, h * M_QK_DIM:(h + 1) * M_QK_DIM]
        wq = jnp.exp(jnp.where(causal_t, bmt[row, :] + ucol[:, H + h:H + h + 1], NEG))
        sw = (jnp.dot(kh, qh, preferred_element_type=F32) * wq).astype(BF16)
        vext = jnp.concatenate([mvt_ref[bb, h * M_V_DIM:(h + 1) * M_V_DIM, :], ones_rows], axis=0)
        res = wa[row, :] * jnp.dot(state.astype(BF16), qh, preferred_element_type=F32) \
            + jnp.dot(vext, sw, preferred_element_type=F32)
        num = res[:M_V_DIM, :]
        den = res[M_V_DIM:M_V_DIM + 1, :]
        hb = num / jnp.maximum(jnp.abs(den), emt[row, :])
        vw = (vext.astype(F32) * ws[row, :]).astype(BF16)
        new_states.append(wc[row, 0:1] * state + jnp.dot(vw, kh, preferred_element_type=F32))
        hn = hb * lax.rsqrt(jnp.mean(hb * hb, axis=0, keepdims=True) + EPS)
        hn = hn * nw_ref[h * M_V_DIM:(h + 1) * M_V_DIM, :]
        og = _sigmoid(mot_ref[bb, h * M_V_DIM:(h + 1) * M_V_DIM, :].astype(F32))
        outs.append((og * hn).astype(BF16))

    for i, (bb, h) in enumerate(pairs):
        state_ref[bb * H + h] = new_states[i]
    for bb in range(nb):
        m_ref[bb] = seqs[bb][8]
        prevq_ref[bb] = mqt_ref[bb]
        y_ref[bb] = jnp.concatenate(outs[bb * H:(bb + 1) * H], axis=0)


def _mlstm(mqt, mk3, mvt, mot, grow, conv_w, conv_b, bias_row, norm_w):
    B, S, _ = mk3.shape
    L = M_CHUNK
    nb = max(d for d in range(1, M_BATCH + 1) if B % d == 0)
    nc = S // L
    assert L == LANES, "per-head scalars are kept lane-replicated next to (8, L) gate rows"
    tok = lambda b, c: (b, c, 0)
    chan = lambda b, c: (b, 0, c)
    const2 = lambda b, c: (0, 0)
    const3 = lambda b, c: (0, 0, 0)
    grow_specs = [pl.BlockSpec((8, L), functools.partial(lambda b, c, n: (0, (b * nb + n) * nc + c), n=n))
                  for n in range(nb)]
    rep = lambda v: jnp.broadcast_to(v.astype(F32)[..., None], v.shape + (L,))
    cwq, cbq = rep(conv_w[:, :M_QK_WIDTH]), rep(conv_b[:M_QK_WIDTH])
    cwk, cbk = conv_w[:, M_QK_WIDTH:].astype(F32), conv_b[M_QK_WIDTH:].reshape(1, -1).astype(F32)
    return pl.pallas_call(
        functools.partial(_mlstm_kernel, nb),
        grid=(B // nb, nc),
        in_specs=[pl.BlockSpec((nb, M_QK_WIDTH, L), chan),
                  pl.BlockSpec((nb, L, M_QK_WIDTH), tok),
                  pl.BlockSpec((nb, M_V_WIDTH, L), chan),
                  pl.BlockSpec((nb, M_V_WIDTH, L), chan),
                  *grow_specs,
                  pl.BlockSpec((CONV_WIDTH, M_QK_WIDTH, L), const3),
                  pl.BlockSpec((M_QK_WIDTH, L), const2),
                  pl.BlockSpec((CONV_WIDTH, M_QK_WIDTH), const2),
                  pl.BlockSpec((1, M_QK_WIDTH), const2),
                  pl.BlockSpec((8, 1), const2),
                  pl.BlockSpec((M_V_WIDTH, L), const2)],
        out_specs=pl.BlockSpec((nb, M_V_WIDTH, L), chan),
        out_shape=jax.ShapeDtypeStruct((B, M_V_WIDTH, S), BF16),
        scratch_shapes=[pltpu.VMEM((nb * M_HEADS, M_STATE_ROWS, M_QK_DIM), F32),
                        pltpu.VMEM((nb, 8, LANES), F32),
                        pltpu.VMEM((nb, M_QK_WIDTH, L), BF16),
                        pltpu.VMEM((nb, L + 8, M_QK_WIDTH), F32)],
        compiler_params=pltpu.CompilerParams(dimension_semantics=("arbitrary", "arbitrary"),
                                             vmem_limit_bytes=VMEM_LIMIT),
        name="mlstm",
    )(mqt, mk3, mvt, mot, *([grow] * nb), cwq, cbq, cwk, cbk, bias_row, rep(norm_w))


def _outproj_kernel(x_ref, ya_ref, ym_ref, gab_ref, wa_ref, wm_ref, wo_ref, nw_ref, wr_ref, br_ref,
                    x1_ref, h2_ref, meta_ref, code_ref, cnt_ref, cnt_scr):
    tm = TOK_TILE
    i = pl.program_id(0)

    @pl.when(i == 0)
    def _():
        cnt_scr[...] = jnp.zeros_like(cnt_scr)

    pa = jnp.dot(ya_ref[...], wa_ref[...], preferred_element_type=F32)
    pm = lax.dot_general(ym_ref[0], wm_ref[...], (((0,), (0,)), ((), ())), preferred_element_type=F32)
    ga = gab_ref[:, :D_MODEL].astype(F32)
    gb = gab_ref[:, D_MODEL:].astype(F32)
    mix = _sigmoid(ga) * pa + _sigmoid(gb) * pm
    x1 = x_ref[...] + jnp.dot(mix.astype(BF16), wo_ref[...], preferred_element_type=F32)
    x1_ref[...] = x1
    h2 = _rms(x1, nw_ref[...])
    h2_ref[...] = _pack_rows(h2)

    h_hi = h2.astype(BF16)
    h_lo = (h2 - h_hi.astype(F32)).astype(BF16)
    hw = jnp.dot(h_hi, wr_ref[...], preferred_element_type=F32)
    logits = (hw[:, :LANES] + hw[:, LANES:]
              + jnp.dot(h_lo, wr_ref[:, :LANES], preferred_element_type=F32)) + br_ref[...]
    lane = lax.broadcasted_iota(jnp.int32, (tm, LANES), 1)
    big = jnp.int32(LANES)

    def first_argmax(v):
        mx = jnp.max(v, axis=-1, keepdims=True)
        idx = jnp.min(jnp.where(v == mx, lane, big), axis=-1, keepdims=True)
        return mx, idx

    gl = jnp.where(lane < N_GROUPS, logits, NEG)
    gmax, gi = first_argmax(gl)
    gp = 1.0 / jnp.sum(jnp.exp(gl - gmax), axis=-1, keepdims=True)
    lo = N_GROUPS + gi * EXPERTS_PER_GROUP
    el = jnp.where((lane >= lo) & (lane < lo + EXPERTS_PER_GROUP), logits, NEG)
    v1, j1 = first_argmax(el)
    v2, j2 = first_argmax(jnp.where(lane == j1, NEG, el))
    t = jnp.exp(v2 - v1)
    w1 = gp / (1.0 + t)
    w2 = gp * t / (1.0 + t)

    hit1 = lane == j1
    hit2 = lane == j2
    onehot = jnp.where(hit1 | hit2, 1.0, 0.0)
    ri = lax.broadcasted_iota(jnp.int32, (tm, tm), 0)
    ci = lax.broadcasted_iota(jnp.int32, (tm, tm), 1)
    before = jnp.where(ci < ri, 1.0, 0.0).astype(BF16)
    rank = jnp.dot(before, onehot.astype(BF16), preferred_element_type=F32) + cnt_scr[0:1, :]
    r1 = jnp.sum(jnp.where(hit1, rank, 0.0), axis=-1, keepdims=True)
    r2 = jnp.sum(jnp.where(hit2, rank, 0.0), axis=-1, keepdims=True)
    cnt = cnt_scr[0:1, :] + jnp.sum(onehot, axis=0, keepdims=True)
    cnt_scr[...] = jnp.broadcast_to(cnt, cnt_scr.shape)
    cnt_ref[...] = jnp.broadcast_to(cnt, cnt_ref.shape)

    c1 = (j1 - N_GROUPS).astype(F32) * RANK_RADIX + r1
    c2 = (j2 - N_GROUPS).astype(F32) * RANK_RADIX + r2
    meta = jnp.where(lane == 0, c1, 0.0)
    meta = jnp.where(lane == 1, c2, meta)
    meta = jnp.where(lane == 4, w1, meta)
    meta = jnp.where(lane == 5, w2, meta)
    meta_ref[...] = meta
    code_ref[...] = meta.T[0:8, :].astype(jnp.int32)


def _outproj(x2, ya, ym, gab, wa, wm, wo, norm_w, w_rt, b_rt):
    T = x2.shape[0]
    tm = TOK_TILE
    tpb = ym.shape[2] // tm
    row = lambda i: (i, 0)
    const = lambda i: (0, 0)
    return pl.pallas_call(
        _outproj_kernel,
        grid=(T // tm,),
        in_specs=[pl.BlockSpec((tm, D_MODEL), row),
                  pl.BlockSpec((tm, ATT_Q_WIDTH), row),
                  pl.BlockSpec((1, M_V_WIDTH, tm), lambda i: (i // tpb, 0, i % tpb)),
                  pl.BlockSpec((tm, 2 * D_MODEL), row),
                  pl.BlockSpec((ATT_Q_WIDTH, D_MODEL), const),
                  pl.BlockSpec((M_V_WIDTH, D_MODEL), const),
                  pl.BlockSpec((D_MODEL, D_MODEL), const),
                  pl.BlockSpec((1, D_MODEL), const),
                  pl.BlockSpec((D_MODEL, 2 * LANES), const),
                  pl.BlockSpec((1, LANES), const)],
        out_specs=[pl.BlockSpec((tm, D_MODEL), row),
                   pl.BlockSpec((tm, PACKED), row),
                   pl.BlockSpec((tm, LANES), row),
                   pl.BlockSpec((8, tm), lambda i: (0, i)),
                   pl.BlockSpec((8, LANES), const)],
        out_shape=[jax.ShapeDtypeStruct((T, D_MODEL), F32),
                   jax.ShapeDtypeStruct((T, PACKED), jnp.int32),
                   jax.ShapeDtypeStruct((T, LANES), F32),
                   jax.ShapeDtypeStruct((8, T), jnp.int32),
                   jax.ShapeDtypeStruct((8, LANES), F32)],
        scratch_shapes=[pltpu.VMEM((8, LANES), F32)],
        compiler_params=pltpu.CompilerParams(dimension_semantics=("arbitrary",),
                                             vmem_limit_bytes=VMEM_LIMIT),
        name="outproj",
    )(x2, ya, ym, gab, wa, wm, wo, norm_w, w_rt, b_rt)


def _slots_kernel(cnt_ref, pstart_ref, blk_e_ref, nvalid_ref, nused_ref):
    R = SLOT_BLOCK
    nblk = blk_e_ref.shape[0]

    def per_expert(e, run):
        c = cnt_ref[e]
        end = run + ((c + R - 1) // R) * R
        pstart_ref[e] = run

        def set_blk(b, carry):
            blk_e_ref[b] = e
            nvalid_ref[b] = jnp.minimum(run + c - b * R, R)
            return carry
        lax.fori_loop(run // R, end // R, set_blk, 0)
        return end

    total = lax.fori_loop(0, N_EXPERTS, per_expert, 0)
    nused_ref[0] = total // R

    def tail_blk(b, carry):
        blk_e_ref[b] = N_EXPERTS - 1
        nvalid_ref[b] = 0
        return carry
    lax.fori_loop(total // R, nblk, tail_blk, 0)


def _slots(counts, nblk):
    smem = pl.BlockSpec(memory_space=pltpu.SMEM)
    return pl.pallas_call(
        _slots_kernel,
        in_specs=[smem],
        out_specs=[smem, smem, smem, smem],
        out_shape=[jax.ShapeDtypeStruct((N_EXPERTS,), jnp.int32),
                   jax.ShapeDtypeStruct((nblk,), jnp.int32),
                   jax.ShapeDtypeStruct((nblk,), jnp.int32),
                   jax.ShapeDtypeStruct((1,), jnp.int32)],
        name="slots",
    )(counts)


def _experts_kernel(blk_e_ref, nvalid_ref, nused_ref, xs_ref, wg_ref, wu_ref, wd_ref, ys_ref,
                    wg_bf, wu_bf, wd_bf):
    j = pl.program_id(0)
    R = SLOT_BLOCK
    valid = j < nused_ref[0]
    changed = jnp.logical_or(j == 0, blk_e_ref[j] != blk_e_ref[jnp.maximum(j - 1, 0)])

    @pl.when(jnp.logical_and(valid, changed))
    def _():
        wg_bf[...] = wg_ref[0].astype(BF16)
        wu_bf[...] = wu_ref[0].astype(BF16)
        wd_bf[...] = wd_ref[0].astype(BF16)

    @pl.when(valid)
    def _():
        rows = lax.broadcasted_iota(jnp.int32, (R, PACKED), 0)
        xw = jnp.where(rows < nvalid_ref[j], xs_ref[...], 0)
        xb = _unpack_rows(xw).astype(BF16)
        g = jnp.dot(xb, wg_bf[...], preferred_element_type=F32)
        u = jnp.dot(xb, wu_bf[...], preferred_element_type=F32)
        act = (g * _sigmoid(g) * u).astype(BF16)
        ys_ref[...] = _pack_rows(jnp.dot(act, wd_bf[...], preferred_element_type=F32))

    @pl.when(jnp.logical_not(valid))
    def _():
        ys_ref[...] = jnp.zeros_like(ys_ref)


def _experts(blk_e, nvalid, nused, xs, w_gate, w_up, w_down):
    nblk = blk_e.shape[0]
    R = SLOT_BLOCK
    wspec = lambda shape: pl.BlockSpec((1,) + shape, lambda j, be, nv, nu: (be[j], 0, 0))
    grid_spec = pltpu.PrefetchScalarGridSpec(
        num_scalar_prefetch=3,
        grid=(nblk,),
        in_specs=[pl.BlockSpec((R, PACKED), lambda j, be, nv, nu: (jnp.minimum(j, nu[0] - 1), 0)),
                  wspec((D_MODEL, D_EXPERT)),
                  wspec((D_MODEL, D_EXPERT)),
                  wspec((D_EXPERT, D_MODEL))],
        out_specs=pl.BlockSpec((R, PACKED), lambda j, be, nv, nu: (j, 0)),
        scratch_shapes=[pltpu.VMEM((D_MODEL, D_EXPERT), BF16),
                        pltpu.VMEM((D_MODEL, D_EXPERT), BF16),
                        pltpu.VMEM((D_EXPERT, D_MODEL), BF16)],
    )
    return pl.pallas_call(
        _experts_kernel,
        grid_spec=grid_spec,
        out_shape=jax.ShapeDtypeStruct((nblk * R, PACKED), jnp.int32),
        compiler_params=pltpu.CompilerParams(dimension_semantics=("arbitrary",),
                                             vmem_limit_bytes=VMEM_LIMIT),
        name="experts",
    )(blk_e, nvalid, nused, xs, w_gate, w_up, w_down)


def _dest_kernel(pstart_ref, code_ref, dest_ref):
    c = code_ref[...]
    e = c >> RANK_BITS
    base = jnp.zeros_like(c)
    for k in range(N_EXPERTS):
        base = jnp.where(e == k, pstart_ref[k], base)
    dest_ref[...] = base + (c & (RANK_RADIX - 1))


def _dests(pstart, code_rows):
    T = code_rows.shape[1]
    tc = min(4096, T)
    return pl.pallas_call(
        _dest_kernel,
        grid=(T // tc,),
        in_specs=[pl.BlockSpec(memory_space=pltpu.SMEM),
                  pl.BlockSpec((8, tc), lambda i: (0, i))],
        out_specs=pl.BlockSpec((8, tc), lambda i: (0, i)),
        out_shape=jax.ShapeDtypeStruct((8, T), jnp.int32),
        name="dests",
    )(pstart, code_rows)


def _sc_move_rows(src, idx, n_out, scatter):
    M = idx.shape[0]
    n_src, width = src.shape
    info = plsc.get_sparse_core_info()
    nc, nw = info.num_cores, info.num_cores * info.num_subcores
    per_w = M // nw
    ch = SC_CHUNK
    nch = per_w // ch
    assert per_w * nw == M and nch * ch == per_w and nch % 2 == 0 and n_src % per_w == 0
    mesh = plsc.VectorSubcoreMesh(core_axis_name="c", subcore_axis_name="s")

    @functools.partial(
        pl.kernel, mesh=mesh, out_type=jax.ShapeDtypeStruct((n_out, width), src.dtype),
        scratch_types=[pltpu.VMEM((per_w,), jnp.int32),
                       pltpu.VMEM((ch, width), src.dtype), pltpu.VMEM((ch, width), src.dtype),
                       pltpu.SemaphoreType.DMA, pltpu.SemaphoreType.DMA,
                       pltpu.SemaphoreType.DMA, pltpu.SemaphoreType.DMA],
        name="sc_scatter" if scatter else "sc_gather")
    def move(src_hbm, idx_hbm, out_hbm, idx_v, buf0, buf1, in0, in1, out0, out1):
        wid = lax.axis_index("s") * nc + lax.axis_index("c")
        base = wid * per_w
        pltpu.sync_copy(idx_hbm.at[pl.ds(base, per_w)], idx_v)
        lin_base = lax.rem(base, n_src) if scatter else base

        def read(c, buf, sem):
            if scatter:
                return pltpu.make_async_copy(src_hbm.at[pl.ds(lin_base + c * ch, ch)], buf, sem)
            return pltpu.make_async_copy(src_hbm.at[idx_v.at[pl.ds(c * ch, ch)]], buf, sem)

        def write(c, buf, sem):
            if scatter:
                return pltpu.make_async_copy(buf, out_hbm.at[idx_v.at[pl.ds(c * ch, ch)]], sem)
            return pltpu.make_async_copy(buf, out_hbm.at[pl.ds(lin_base + c * ch, ch)], sem)

        read(0, buf0, in0).start()

        @pl.loop(0, nch, step=2)
        def _(c):
            @pl.when(c > 0)
            def _():
                write(c - 1, buf1, out1).wait()
            read(c + 1, buf1, in1).start()
            read(c, buf0, in0).wait()
            write(c, buf0, out0).start()
            read(c + 1, buf1, in1).wait()
            write(c + 1, buf1, out1).start()
            write(c, buf0, out0).wait()

            @pl.when(c + 2 < nch)
            def _():
                read(c + 2, buf0, in0).start()

        write(nch - 1, buf1, out1).wait()

    return move(src, idx)


def _final_kernel(x1_ref, y1_ref, y2_ref, meta_ref, nw_ref, o_ref):
    w1 = meta_ref[:, 4:5]
    w2 = meta_ref[:, 5:6]
    x2 = x1_ref[...] + (w1 * _unpack_rows(y1_ref[...]) + w2 * _unpack_rows(y2_ref[...]))
    o_ref[...] = _rms(x2, nw_ref[...])


def _final(x1, yg, meta, norm_w):
    T = x1.shape[0]
    tm = COMB_TILE
    nt = T // tm
    row = lambda i: (i, 0)
    return pl.pallas_call(
        _final_kernel,
        grid=(nt,),
        in_specs=[pl.BlockSpec((tm, D_MODEL), row),
                  pl.BlockSpec((tm, PACKED), row),
                  pl.BlockSpec((tm, PACKED), lambda i: (i + nt, 0)),
                  pl.BlockSpec((tm, LANES), row),
                  pl.BlockSpec((1, D_MODEL), lambda i: (0, 0))],
        out_specs=pl.BlockSpec((tm, D_MODEL), row),
        out_shape=jax.ShapeDtypeStruct((T, D_MODEL), F32),
        compiler_params=pltpu.CompilerParams(dimension_semantics=("arbitrary",),
                                             vmem_limit_bytes=VMEM_LIMIT),
        name="final",
    )(x1, yg, yg, meta, norm_w)


def _layer(x, norm_mix_w, w_in, conv_w, conv_b, b_igate, b_fgate, attn_sinks, mlstm_norm_w,
           w_attn_o, w_mlstm_o, w_out, norm_ffn_w, w_group, b_group, w_router, b_router,
           w_gate, w_up, w_down, out_norm_w):
    B, S, D = x.shape
    T = B * S
    x2 = x.reshape(T, D)

    w_tok = jnp.concatenate([w_in[:, _O_AQ:_O_MQ], w_in[:, _O_MK:_O_MV], w_in[:, _O_GA:]], axis=1).astype(BF16)
    w_chan = jnp.concatenate([w_in[:, _O_MQ:_O_MK], w_in[:, _O_MV:_O_MI]], axis=1).T.astype(BF16)
    w_gates = jnp.pad(w_in[:, _O_MI:_O_GA], ((0, 0), (0, LANES - 2 * M_HEADS))).astype(BF16)
    q, kv, mk, gab, mqt, mvt, mot, grow = _inproj(x2, norm_mix_w.reshape(1, D), w_tok, w_chan, w_gates, B, S)

    ya = _attn(attn_sinks.astype(F32), q.reshape(B, S, -1), kv.reshape(B, S, -1))

    bias = jnp.concatenate([b_igate, b_fgate]).astype(F32)
    ymt = _mlstm(mqt, mk.reshape(B, S, -1), mvt, mot, grow, conv_w, conv_b,
                 bias.reshape(2 * M_HEADS, 1), mlstm_norm_w)

    w_rt = jnp.pad(jnp.concatenate([w_group, w_router], axis=1),
                   ((0, 0), (0, LANES - N_GROUPS - N_EXPERTS))).astype(F32)
    w_rt_hi = w_rt.astype(BF16)
    w_rt = jnp.concatenate([w_rt_hi, (w_rt - w_rt_hi.astype(F32)).astype(BF16)], axis=1)
    b_rt = jnp.pad(jnp.concatenate([b_group, b_router]), (0, LANES - N_GROUPS - N_EXPERTS)).reshape(1, LANES)
    x1, h2p, meta, code_rows, cnt = _outproj(x2, ya.reshape(T, -1), ymt, gab,
                                            w_attn_o.astype(BF16), w_mlstm_o.astype(BF16),
                                            w_out.astype(BF16), norm_ffn_w.reshape(1, D), w_rt,
                                            b_rt.astype(F32))

    counts = cnt[0, N_GROUPS:N_GROUPS + N_EXPERTS].astype(jnp.int32)
    nblk = (2 * T) // SLOT_BLOCK + N_EXPERTS
    pstart, blk_e, nvalid, nused = _slots(counts, nblk)
    dest = _dests(pstart, code_rows)[0:2].reshape(2 * T)

    xs = _sc_move_rows(h2p, dest, nblk * SLOT_BLOCK, scatter=True)
    ys = _experts(blk_e, nvalid, nused, xs, w_gate, w_up, w_down)
    yg = _sc_move_rows(ys, dest, 2 * T, scatter=False)
    out = _final(x1, yg, meta, out_norm_w.reshape(1, D))
    return out.reshape(B, S, D)


def kernel(x, norm_mix_w, w_in, conv_w, conv_b, b_igate, b_fgate, attn_sinks, mlstm_norm_w, w_attn_o,
           w_mlstm_o, w_out, norm_ffn_w, w_group, b_group, w_router, b_router, w_gate, w_up, w_down,
           norm_final_w):
    depth = w_in.shape[0]
    assert depth == 1, "final RMSNorm is fused into the last layer's combine kernel"
    return _layer(x, norm_mix_w[0], w_in[0], conv_w[0], conv_b[0], b_igate[0], b_fgate[0],
                  attn_sinks[0], mlstm_norm_w[0], w_attn_o[0], w_mlstm_o[0], w_out[0], norm_ffn_w[0],
                  w_group[0], b_group[0], w_router[0], b_router[0], w_gate[0], w_up[0], w_down[0],
                  norm_final_w)
```

```python
import functools

import jax
import jax.numpy as jnp
from jax import lax
from jax.experimental import pallas as pl
from jax.experimental.pallas import tpu as pltpu
from jax.experimental.pallas import tpu_sc as plsc

F32 = jnp.float32
BF16 = jnp.bfloat16

D_MODEL = 1024
N_Q_HEADS = 8
N_KV_HEADS = 2
HEAD_DIM = 64
WINDOW = 128
GQA_GROUP = N_Q_HEADS // N_KV_HEADS
M_HEADS = 4
M_QK_DIM = 64
M_V_DIM = 128
CONV_WIDTH = 4
N_GROUPS = 4
EXPERTS_PER_GROUP = 8
N_EXPERTS = N_GROUPS * EXPERTS_PER_GROUP
D_EXPERT = 512
EPS = 1e-6

ATT_Q_WIDTH = N_Q_HEADS * HEAD_DIM
ATT_KV_WIDTH = N_KV_HEADS * HEAD_DIM
M_QK_WIDTH = M_HEADS * M_QK_DIM
M_V_WIDTH = M_HEADS * M_V_DIM

LANES = 128
NEG = -1e30
VMEM_LIMIT = 56 * 1024 * 1024

TOK_TILE = 512
ATT_TILE = 256
M_CHUNK = 128
M_BATCH = 8
M_STATE_ROWS = M_V_DIM + 16
SLOT_BLOCK = 512
COMB_TILE = 512
RANK_RADIX = 65536
RANK_BITS = 16
ROUTER_ROWS = 8 + N_EXPERTS
PACKED = D_MODEL // 2
SC_CHUNK = 64
DMA_THREADS = 2

_O_AQ = 0
_O_AK = _O_AQ + ATT_Q_WIDTH
_O_AV = _O_AK + ATT_KV_WIDTH
_O_MQ = _O_AV + ATT_KV_WIDTH
_O_MK = _O_MQ + M_QK_WIDTH
_O_MV = _O_MK + M_QK_WIDTH
_O_MO = _O_MV + M_V_WIDTH
_O_MI = _O_MO + M_V_WIDTH
_O_MF = _O_MI + M_HEADS
_O_GA = _O_MF + M_HEADS
_O_GB = _O_GA + D_MODEL
_O_END = _O_GB + D_MODEL


def _rms(x, w):
    return x * lax.rsqrt(jnp.mean(x * x, axis=-1, keepdims=True) + EPS) * w


def _sigmoid(x):
    return 0.5 * jnp.tanh(0.5 * x) + 0.5


def _pack_rows(x):
    half = x.shape[1] // 2
    bits = lax.bitcast_convert_type(x.astype(BF16).astype(F32), jnp.uint32)
    packed = (bits[:, :half] >> 16) | (bits[:, half:] & jnp.uint32(0xFFFF0000))
    return lax.bitcast_convert_type(packed, jnp.int32)


def _unpack_rows(w):
    u = lax.bitcast_convert_type(w, jnp.uint32)
    lo = lax.bitcast_convert_type(u << 16, F32)
    hi = lax.bitcast_convert_type(u & jnp.uint32(0xFFFF0000), F32)
    return jnp.concatenate([lo, hi], axis=-1)


def _log_sigmoid(x):
    return jnp.minimum(x, 0.0) - jnp.log1p(jnp.exp(-jnp.abs(x)))


def _inproj_kernel(x_ref, nw_ref, w_ref, wt_ref, wg_ref,
                   q_ref, kv_ref, mk_ref, gab_ref, mqt_ref, mvt_ref, mot_ref, grow_ref):
    h = _rms(x_ref[...], nw_ref[...]).astype(BF16)

    def proj(lo, hi):
        return jnp.dot(h, w_ref[:, lo:hi], preferred_element_type=F32).astype(BF16)

    def proj_t(lo, hi):
        return lax.dot_general(wt_ref[lo:hi, :], h, (((1,), (1,)), ((), ())),
                               preferred_element_type=F32).astype(BF16)

    o_kv = ATT_Q_WIDTH
    o_mk = o_kv + 2 * ATT_KV_WIDTH
    o_g = o_mk + M_QK_WIDTH
    q_ref[...] = proj(0, o_kv)
    kv_ref[...] = proj(o_kv, o_mk)
    mk_ref[...] = proj(o_mk, o_g)
    gab_ref[...] = proj(o_g, o_g + 2 * D_MODEL)
    mqt_ref[0] = proj_t(0, M_QK_WIDTH)
    mvt_ref[0] = proj_t(M_QK_WIDTH, M_QK_WIDTH + M_V_WIDTH)
    mot_ref[0] = proj_t(M_QK_WIDTH + M_V_WIDTH, M_QK_WIDTH + 2 * M_V_WIDTH)
    g = jnp.dot(h, wg_ref[...], preferred_element_type=F32)
    grow_ref[...] = g.T[0:8, :]


def _inproj(x2, norm_w, w_tok, w_chan, w_gates, B, S):
    T = x2.shape[0]
    tm = TOK_TILE
    tpb = S // tm
    row = lambda i: (i, 0)
    const = lambda i: (0, 0)
    chan = lambda i: (i // tpb, 0, i % tpb)
    return pl.pallas_call(
        _inproj_kernel,
        grid=(T // tm,),
        in_specs=[pl.BlockSpec((tm, D_MODEL), row),
                  pl.BlockSpec((1, D_MODEL), const),
                  pl.BlockSpec(w_tok.shape, const),
                  pl.BlockSpec(w_chan.shape, const),
                  pl.BlockSpec((D_MODEL, LANES), const)],
        out_specs=[pl.BlockSpec((tm, ATT_Q_WIDTH), row),
                   pl.BlockSpec((tm, 2 * ATT_KV_WIDTH), row),
                   pl.BlockSpec((tm, M_QK_WIDTH), row),
                   pl.BlockSpec((tm, 2 * D_MODEL), row),
                   pl.BlockSpec((1, M_QK_WIDTH, tm), chan),
                   pl.BlockSpec((1, M_V_WIDTH, tm), chan),
                   pl.BlockSpec((1, M_V_WIDTH, tm), chan),
                   pl.BlockSpec((8, tm), lambda i: (0, i))],
        out_shape=[jax.ShapeDtypeStruct((T, ATT_Q_WIDTH), BF16),
                   jax.ShapeDtypeStruct((T, 2 * ATT_KV_WIDTH), BF16),
                   jax.ShapeDtypeStruct((T, M_QK_WIDTH), BF16),
                   jax.ShapeDtypeStruct((T, 2 * D_MODEL), BF16),
                   jax.ShapeDtypeStruct((B, M_QK_WIDTH, S), BF16),
                   jax.ShapeDtypeStruct((B, M_V_WIDTH, S), BF16),
                   jax.ShapeDtypeStruct((B, M_V_WIDTH, S), BF16),
                   jax.ShapeDtypeStruct((8, T), F32)],
        compiler_params=pltpu.CompilerParams(dimension_semantics=("arbitrary",),
                                             vmem_limit_bytes=VMEM_LIMIT),
        name="inproj",
    )(x2, norm_w, w_tok, w_chan, w_gates)


def _attn_kernel(sink_ref, q_ref, kv_ref, kvp_ref, o_ref):
    i = pl.program_id(1)
    W = WINDOW
    r = lax.broadcasted_iota(jnp.int32, (W, 2 * W), 0)
    c = lax.broadcasted_iota(jnp.int32, (W, 2 * W), 1)
    band = (c > r) & (c <= r + W)
    for j in range(ATT_TILE // W):
        qj = q_ref[0, j * W:(j + 1) * W, :]
        if j == 0:
            kprev = kvp_ref[0]
            valid = band & ((i > 0) | (c >= W))
        else:
            kprev = kv_ref[0, (j - 1) * W:j * W, :]
            valid = band
        kwin = jnp.concatenate([kprev, kv_ref[0, j * W:(j + 1) * W, :]], axis=0)
        outs = []
        for h in range(N_Q_HEADS):
            g = h // GQA_GROUP
            qh = qj[:, h * HEAD_DIM:(h + 1) * HEAD_DIM]
            kg = kwin[:, g * HEAD_DIM:(g + 1) * HEAD_DIM]
            vg = kwin[:, ATT_KV_WIDTH + g * HEAD_DIM:ATT_KV_WIDTH + (g + 1) * HEAD_DIM]
            s = lax.dot_general(qh, kg, (((1,), (1,)), ((), ())),
                                preferred_element_type=F32) * (HEAD_DIM ** -0.5)
            s = jnp.where(valid, s, NEG)
            sink = sink_ref[h]
            m = jnp.maximum(jnp.max(s, axis=-1, keepdims=True), sink)
            p = jnp.exp(s - m)
            l = jnp.sum(p, axis=-1, keepdims=True) + jnp.exp(sink - m)
            o = jnp.dot(p.astype(BF16), vg, preferred_element_type=F32)
            outs.append(o / l)
        o_ref[0, j * W:(j + 1) * W, :] = jnp.concatenate(outs, axis=-1).astype(BF16)


def _attn(sinks, q3, kv3):
    B, S, _ = q3.shape
    tq = ATT_TILE
    per = tq // WINDOW
    return pl.pallas_call(
        _attn_kernel,
        grid=(B, S // tq),
        in_specs=[pl.BlockSpec(memory_space=pltpu.SMEM),
                  pl.BlockSpec((1, tq, ATT_Q_WIDTH), lambda b, i: (b, i, 0)),
                  pl.BlockSpec((1, tq, 2 * ATT_KV_WIDTH), lambda b, i: (b, i, 0)),
                  pl.BlockSpec((1, WINDOW, 2 * ATT_KV_WIDTH),
                               lambda b, i: (b, jnp.maximum(i * per - 1, 0), 0))],
        out_specs=pl.BlockSpec((1, tq, ATT_Q_WIDTH), lambda b, i: (b, i, 0)),
        out_shape=jax.ShapeDtypeStruct((B, S, ATT_Q_WIDTH), BF16),
        compiler_params=pltpu.CompilerParams(dimension_semantics=("arbitrary", "arbitrary"),
                                             vmem_limit_bytes=VMEM_LIMIT),
        name="attn",
    )(sinks, q3, kv3, kv3)


def _mlstm_kernel(nb, *refs):
    L = M_CHUNK
    H = M_HEADS
    mqt_ref, mk_ref, mvt_ref, mot_ref = refs[:4]
    grow_refs = refs[4:4 + nb]
    (cwq_ref, cbq_ref, cwk_ref, cbk_ref, brow_ref, nw_ref, y_ref,
     state_ref, m_ref, prevq_ref, ubuf_ref) = refs[4 + nb:]
    cidx = pl.program_id(1)

    @pl.when(cidx == 0)
    def _():
        state_ref[...] = jnp.zeros_like(state_ref)
        m_ref[...] = jnp.zeros_like(m_ref)
        prevq_ref[...] = jnp.zeros_like(prevq_ref)
        ubuf_ref[:, 0:8, :] = jnp.zeros((nb, 8, M_QK_WIDTH), F32)

    ri = lax.broadcasted_iota(jnp.int32, (L, L), 0)
    ci = lax.broadcasted_iota(jnp.int32, (L, L), 1)
    causal_t = ri <= ci
    triu = jnp.where(causal_t, 1.0, 0.0).astype(F32)
    lane = lax.broadcasted_iota(jnp.int32, (8, L), 1)
    r2 = lax.broadcasted_iota(jnp.int32, (2 * L, L), 0)
    c2 = lax.broadcasted_iota(jnp.int32, (2 * L, L), 1)
    shifts = [jnp.where(r2 - c2 == L - k, 1.0, 0.0).astype(BF16) for k in range(1, CONV_WIDTH)]
    ones_rows = jnp.where(lax.broadcasted_iota(jnp.int32, (M_STATE_ROWS - M_V_DIM, L), 0) == 0,
                          1.0, 0.0).astype(BF16)

    pairs = [(bb, h) for bb in range(nb) for h in range(H)]
    states = [state_ref[bb * H + h] for bb, h in pairs]
    m_prevs = [m_ref[bb] for bb in range(nb)]

    seqs = []
    for bb in range(nb):
        cur = mqt_ref[bb]
        both = jnp.concatenate([prevq_ref[bb], cur], axis=1)
        acc = cbq_ref[...] + cwq_ref[CONV_WIDTH - 1] * cur.astype(F32)
        for k in range(1, CONV_WIDTH):
            acc = acc + cwq_ref[CONV_WIDTH - 1 - k] * jnp.dot(both, shifts[k - 1],
                                                              preferred_element_type=F32)
        qt = (acc * _sigmoid(acc)).astype(BF16)
        ubuf_ref[bb, 8:L + 8, :] = mk_ref[bb].astype(F32)
        acc = cbk_ref[...] + cwk_ref[CONV_WIDTH - 1:CONV_WIDTH, :] * ubuf_ref[bb, 8:L + 8, :]
        for j in range(CONV_WIDTH - 1):
            off = 8 - (CONV_WIDTH - 1) + j
            acc = acc + cwk_ref[j:j + 1, :] * ubuf_ref[bb, off:off + L, :]
        ubuf_ref[bb, 0:8, :] = ubuf_ref[bb, L:L + 8, :]
        kk =(acc * _sigmoid(acc) * (M_QK_DIM ** -0.5)).astype(BF16)

        gr = grow_refs[bb][...] + brow_ref[...]
        b = jnp.dot(_log_sigmoid(gr), triu, preferred_element_type=F32,
                    precision=lax.Precision.HIGHEST)
        gi = pltpu.roll(gr, 4, axis=0)
        u = gi - b
        cm = u
        for sh in (1, 2, 4, 8, 16, 32, 64):
            if sh < L:
                cm = jnp.maximum(cm, jnp.where(lane >= sh, pltpu.roll(cm, sh, axis=1), NEG))
        m_prev = m_prevs[bb]
        mt = b + jnp.maximum(m_prev, cm)
        wa = jnp.exp(b + m_prev - mt)
        emt = jnp.exp(-mt)
        m_new = jnp.broadcast_to(mt[:, L - 1:L], (8, L))
        b_last = jnp.broadcast_to(b[:, L - 1:L], (8, L))
        wc = jnp.exp(b_last + m_prev - m_new)
        ws = jnp.exp(b_last + u - m_new)
        seqs.append((qt, kk, b - mt, u.T, wa, emt, wc, ws, m_new))

    new_states, outs = [], []
    for (bb, h), state in zip(pairs, states):
        qt, kk, bmt, ucol, wa, emt, wc, ws, _ = seqs[bb]
        row = slice(H + h, H + h + 1)
        qh = qt[h * M_QK_DIM:(h + 1) * M_QK_DIM, :]
        kh = kk[:, h * M_QK_DIM:(h + 1) * M_QK_DIM]
        wq = jnp.exp(jnp.where(causal_t, bmt[row, :] + ucol[:, H + h:H + h + 1], NEG))
        sw = (jnp.dot(kh, qh, preferred_element_type=F32) * wq).astype(BF16)
        vext = jnp.concatenate([mvt_ref[bb, h * M_V_DIM:(h + 1) * M_V_DIM, :], ones_rows], axis=0)
        res = wa[row, :] * jnp.dot(state.astype(BF16), qh, preferred_element_type=F32) \
            + jnp.dot(vext, sw, preferred_element_type=F32)
        num = res[:M_V_DIM, :]
        den = res[M_V_DIM:M_V_DIM + 1, :]
        hb = num / jnp.maximum(jnp.abs(den), emt[row, :])
        vw = (vext.astype(F32) * ws[row, :]).astype(BF16)
        new_states.append(wc[row, 0:1] * state + jnp.dot(vw, kh, preferred_element_type=F32))
        hn = hb * lax.rsqrt(jnp.mean(hb * hb, axis=0, keepdims=True) + EPS)
        hn = hn * nw_ref[h * M_V_DIM:(h + 1) * M_V_DIM, :]
        og = _sigmoid(mot_ref[bb, h * M_V_DIM:(h + 1) * M_V_DIM, :].astype(F32))
        outs.append((og * hn).T.astype(BF16))

    for i, (bb, h) in enumerate(pairs):
        state_ref[bb * H + h] = new_states[i]
    for bb in range(nb):
        m_ref[bb] = seqs[bb][8]
        prevq_ref[bb] = mqt_ref[bb]
        y_ref[bb] = jnp.concatenate(outs[bb * H:(bb + 1) * H], axis=1)


def _mlstm(mqt, mk3, mvt, mot, grow, conv_w, conv_b, bias_row, norm_w):
    B, S, _ = mk3.shape
    L = M_CHUNK
    nb = max(d for d in range(1, M_BATCH + 1) if B % d == 0)
    nc = S // L
    assert L == LANES, "per-head scalars are kept lane-replicated next to (8, L) gate rows"
    tok = lambda b, c: (b, c, 0)
    chan = lambda b, c: (b, 0, c)
    const2 = lambda b, c: (0, 0)
    const3 = lambda b, c: (0, 0, 0)
    grow_specs = [pl.BlockSpec((8, L), functools.partial(lambda b, c, n: (0, (b * nb + n) * nc + c), n=n))
                  for n in range(nb)]
    rep = lambda v: jnp.broadcast_to(v.astype(F32)[..., None], v.shape + (L,))
    cwq, cbq = rep(conv_w[:, :M_QK_WIDTH]), rep(conv_b[:M_QK_WIDTH])
    cwk, cbk = conv_w[:, M_QK_WIDTH:].astype(F32), conv_b[M_QK_WIDTH:].reshape(1, -1).astype(F32)
    return pl.pallas_call(
        functools.partial(_mlstm_kernel, nb),
        grid=(B // nb, nc),
        in_specs=[pl.BlockSpec((nb, M_QK_WIDTH, L), chan),
                  pl.BlockSpec((nb, L, M_QK_WIDTH), tok),
                  pl.BlockSpec((nb, M_V_WIDTH, L), chan),
                  pl.BlockSpec((nb, M_V_WIDTH, L), chan),
                  *grow_specs,
                  pl.BlockSpec((CONV_WIDTH, M_QK_WIDTH, L), const3),
                  pl.BlockSpec((M_QK_WIDTH, L), const2),
                  pl.BlockSpec((CONV_WIDTH, M_QK_WIDTH), const2),
                  pl.BlockSpec((1, M_QK_WIDTH), const2),
                  pl.BlockSpec((8, 1), const2),
                  pl.BlockSpec((M_V_WIDTH, L), const2)],
        out_specs=pl.BlockSpec((nb, L, M_V_WIDTH), tok),
        out_shape=jax.ShapeDtypeStruct((B, S, M_V_WIDTH), BF16),
        scratch_shapes=[pltpu.VMEM((nb * M_HEADS, M_STATE_ROWS, M_QK_DIM), F32),
                        pltpu.VMEM((nb, 8, LANES), F32),
                        pltpu.VMEM((nb, M_QK_WIDTH, L), BF16),
                        pltpu.VMEM((nb, L + 8, M_QK_WIDTH), F32)],
        compiler_params=pltpu.CompilerParams(dimension_semantics=("arbitrary", "arbitrary"),
                                             vmem_limit_bytes=VMEM_LIMIT),
        name="mlstm",
    )(mqt, mk3, mvt, mot, *([grow] * nb), cwq, cbq, cwk, cbk, bias_row, rep(norm_w))


def _outproj_kernel(x_ref, ya_ref, ym_ref, gab_ref, wa_ref, wm_ref, wo_ref, nw_ref, wr_ref, br_ref,
                    x1_ref, h2_ref, meta_ref, code_ref, cnt_ref, cnt_scr):
    tm = TOK_TILE
    i = pl.program_id(0)

    @pl.when(i == 0)
    def _():
        cnt_scr[...] = jnp.zeros_like(cnt_scr)

    pa = jnp.dot(ya_ref[...], wa_ref[...], preferred_element_type=F32)
    pm = jnp.dot(ym_ref[...], wm_ref[...], preferred_element_type=F32)
    ga = gab_ref[:, :D_MODEL].astype(F32)
    gb = gab_ref[:, D_MODEL:].astype(F32)
    mix = _sigmoid(ga) * pa + _sigmoid(gb) * pm
    x1 = x_ref[...] + jnp.dot(mix.astype(BF16), wo_ref[...], preferred_element_type=F32)
    x1_ref[...] = x1
    h2 = _rms(x1, nw_ref[...])
    h2_ref[...] = _pack_rows(h2)

    nt = (((1,), (1,)), ((), ()))
    h_hi = h2.astype(BF16)
    h_lo = (h2 - h_hi.astype(F32)).astype(BF16)
    hw = lax.dot_general(wr_ref[...], h_hi, nt, preferred_element_type=F32)
    logits = (hw[:LANES] + hw[LANES:]
              + lax.dot_general(wr_ref[:LANES, :], h_lo, nt, preferred_element_type=F32))[:ROUTER_ROWS]
    logits = logits + br_ref[...]
    row8 = lax.broadcasted_iota(jnp.int32, (8, tm), 0)

    def first_argmax(v):
        mx = jnp.max(v, axis=0, keepdims=True)
        idx = jnp.min(jnp.where(v == mx, row8, 8), axis=0, keepdims=True)
        return mx, idx

    gl = jnp.where(row8 < N_GROUPS, logits[0:8], NEG)
    gmax, gi = first_argmax(gl)
    gp = 1.0 / jnp.sum(jnp.exp(gl - gmax), axis=0, keepdims=True)
    el = jnp.zeros((8, tm), F32)
    for g in range(N_GROUPS):
        el = jnp.where(gi == g, logits[8 + 8 * g:16 + 8 * g], el)
    v1, j1 = first_argmax(el)
    v2, j2 = first_argmax(jnp.where(row8 == j1, NEG, el))
    t = jnp.exp(v2 - v1)
    w1 = gp / (1.0 + t)
    w2 = gp * t / (1.0 + t)
    e1 = gi * EXPERTS_PER_GROUP + j1
    e2 = gi * EXPERTS_PER_GROUP + j2

    erow = lax.broadcasted_iota(jnp.int32, (N_EXPERTS, tm), 0)
    hit1 = erow == e1
    hit2 = erow == e2
    onehot = jnp.where(hit1 | hit2, 1.0, 0.0)
    ri = lax.broadcasted_iota(jnp.int32, (tm, tm), 0)
    ci = lax.broadcasted_iota(jnp.int32, (tm, tm), 1)
    before = jnp.where(ri < ci, 1.0, 0.0).astype(BF16)
    rank = jnp.dot(onehot.astype(BF16), before, preferred_element_type=F32) + cnt_scr[...]
    r1 = jnp.sum(jnp.where(hit1, rank, 0.0), axis=0, keepdims=True)
    r2 = jnp.sum(jnp.where(hit2, rank, 0.0), axis=0, keepdims=True)
    cnt = jnp.broadcast_to((rank + onehot)[:, tm - 1:tm], (N_EXPERTS, tm))
    cnt_scr[...] = cnt
    cnt_ref[...] = cnt[:, :LANES]

    c1 = e1 * RANK_RADIX + r1.astype(jnp.int32)
    c2 = e2 * RANK_RADIX + r2.astype(jnp.int32)
    code_ref[...] = jnp.where(row8 == 0, c1, jnp.where(row8 == 1, c2, 0))
    wrow = lax.broadcasted_iota(jnp.int32, (LANES, tm), 0)
    meta_ref[...] = jnp.where(wrow == 4, w1, jnp.where(wrow == 5, w2, 0.0)).T


def _outproj(x2, ya, ym, gab, wa, wm, wo, norm_w, w_rt, b_rt):
    T = x2.shape[0]
    tm = TOK_TILE
    row = lambda i: (i, 0)
    const = lambda i: (0, 0)
    return pl.pallas_call(
        _outproj_kernel,
        grid=(T // tm,),
        in_specs=[pl.BlockSpec((tm, D_MODEL), row),
                  pl.BlockSpec((tm, ATT_Q_WIDTH), row),
                  pl.BlockSpec((tm, M_V_WIDTH), row),
                  pl.BlockSpec((tm, 2 * D_MODEL), row),
                  pl.BlockSpec((ATT_Q_WIDTH, D_MODEL), const),
                  pl.BlockSpec((M_V_WIDTH, D_MODEL), const),
                  pl.BlockSpec((D_MODEL, D_MODEL), const),
                  pl.BlockSpec((1, D_MODEL), const),
                  pl.BlockSpec((2 * LANES, D_MODEL), const),
                  pl.BlockSpec((ROUTER_ROWS, tm), const)],
        out_specs=[pl.BlockSpec((tm, D_MODEL), row),
                   pl.BlockSpec((tm, PACKED), row),
                   pl.BlockSpec((tm, LANES), row),
                   pl.BlockSpec((8, tm), lambda i: (0, i)),
                   pl.BlockSpec((N_EXPERTS, LANES), const)],
        out_shape=[jax.ShapeDtypeStruct((T, D_MODEL), F32),
                   jax.ShapeDtypeStruct((T, PACKED), jnp.int32),
                   jax.ShapeDtypeStruct((T, LANES), F32),
                   jax.ShapeDtypeStruct((8, T), jnp.int32),
                   jax.ShapeDtypeStruct((N_EXPERTS, LANES), F32)],
        scratch_shapes=[pltpu.VMEM((N_EXPERTS, tm), F32)],
        compiler_params=pltpu.CompilerParams(dimension_semantics=("arbitrary",),
                                             vmem_limit_bytes=VMEM_LIMIT),
        name="outproj",
    )(x2, ya, ym, gab, wa, wm, wo, norm_w, w_rt, b_rt)


def _slots_kernel(cnt_ref, pstart_ref, blk_e_ref, nvalid_ref, nused_ref):
    R = SLOT_BLOCK
    nblk = blk_e_ref.shape[0]

    def per_expert(e, run):
        c = cnt_ref[e]
        end = run + ((c + R - 1) // R) * R
        pstart_ref[e] = run

        def set_blk(b, carry):
            blk_e_ref[b] = e
            nvalid_ref[b] = jnp.minimum(run + c - b * R, R)
            return carry
        lax.fori_loop(run // R, end // R, set_blk, 0)
        return end

    total = lax.fori_loop(0, N_EXPERTS, per_expert, 0)
    nused_ref[0] = total // R

    def tail_blk(b, carry):
        blk_e_ref[b] = N_EXPERTS - 1
        nvalid_ref[b] = 0
        return carry
    lax.fori_loop(total // R, nblk, tail_blk, 0)


def _slots(counts, nblk):
    smem = pl.BlockSpec(memory_space=pltpu.SMEM)
    return pl.pallas_call(
        _slots_kernel,
        in_specs=[smem],
        out_specs=[smem, smem, smem, smem],
        out_shape=[jax.ShapeDtypeStruct((N_EXPERTS,), jnp.int32),
                   jax.ShapeDtypeStruct((nblk,), jnp.int32),
                   jax.ShapeDtypeStruct((nblk,), jnp.int32),
                   jax.ShapeDtypeStruct((1,), jnp.int32)],
        name="slots",
    )(counts)


def _experts_kernel(blk_e_ref, nvalid_ref, nused_ref, xs_ref, wg_ref, wu_ref, wd_ref, ys_ref,
                    wg_bf, wu_bf, wd_bf):
    j = pl.program_id(0)
    R = SLOT_BLOCK
    valid = j < nused_ref[0]
    changed = jnp.logical_or(j == 0, blk_e_ref[j] != blk_e_ref[jnp.maximum(j - 1, 0)])

    @pl.when(jnp.logical_and(valid, changed))
    def _():
        wg_bf[...] = wg_ref[0].astype(BF16)
        wu_bf[...] = wu_ref[0].astype(BF16)
        wd_bf[...] = wd_ref[0].astype(BF16)

    @pl.when(valid)
    def _():
        rows = lax.broadcasted_iota(jnp.int32, (R, PACKED), 0)
        xw = jnp.where(rows < nvalid_ref[j], xs_ref[...], 0)
        xb = _unpack_rows(xw).astype(BF16)
        g = jnp.dot(xb, wg_bf[...], preferred_element_type=F32)
        u = jnp.dot(xb, wu_bf[...], preferred_element_type=F32)
        act = (g * _sigmoid(g) * u).astype(BF16)
        ys_ref[...] = _pack_rows(jnp.dot(act, wd_bf[...], preferred_element_type=F32))

    @pl.when(jnp.logical_not(valid))
    def _():
        ys_ref[...] = jnp.zeros_like(ys_ref)


def _experts(blk_e, nvalid, nused, xs, w_gate, w_up, w_down):
    nblk = blk_e.shape[0]
    R = SLOT_BLOCK
    wspec = lambda shape: pl.BlockSpec((1,) + shape, lambda j, be, nv, nu: (be[j], 0, 0))
    grid_spec = pltpu.PrefetchScalarGridSpec(
        num_scalar_prefetch=3,
        grid=(nblk,),
        in_specs=[pl.BlockSpec((R, PACKED), lambda j, be, nv, nu: (jnp.minimum(j, nu[0] - 1), 0)),
                  wspec((D_MODEL, D_EXPERT)),
                  wspec((D_MODEL, D_EXPERT)),
                  wspec((D_EXPERT, D_MODEL))],
        out_specs=pl.BlockSpec((R, PACKED), lambda j, be, nv, nu: (j, 0)),
        scratch_shapes=[pltpu.VMEM((D_MODEL, D_EXPERT), BF16),
                        pltpu.VMEM((D_MODEL, D_EXPERT), BF16),
                        pltpu.VMEM((D_EXPERT, D_MODEL), BF16)],
    )
    return pl.pallas_call(
        _experts_kernel,
        grid_spec=grid_spec,
        out_shape=jax.ShapeDtypeStruct((nblk * R, PACKED), jnp.int32),
        compiler_params=pltpu.CompilerParams(dimension_semantics=("arbitrary",),
                                             vmem_limit_bytes=VMEM_LIMIT),
        name="experts",
    )(blk_e, nvalid, nused, xs, w_gate, w_up, w_down)


def _dest_kernel(pstart_ref, code_ref, dest_ref):
    c = code_ref[...]
    e = c >> RANK_BITS
    base = jnp.zeros_like(c)
    for k in range(N_EXPERTS):
        base = jnp.where(e == k, pstart_ref[k], base)
    dest_ref[...] = base + (c & (RANK_RADIX - 1))


def _dests(pstart, code_rows):
    T = code_rows.shape[1]
    tc = min(4096, T)
    return pl.pallas_call(
        _dest_kernel,
        grid=(T // tc,),
        in_specs=[pl.BlockSpec(memory_space=pltpu.SMEM),
                  pl.BlockSpec((8, tc), lambda i: (0, i))],
        out_specs=pl.BlockSpec((8, tc), lambda i: (0, i)),
        out_shape=jax.ShapeDtypeStruct((8, T), jnp.int32),
        name="dests",
    )(pstart, code_rows)


def _sc_move_rows(src, idx, n_out, scatter):
    M = idx.shape[0]
    n_src, width = src.shape
    info = plsc.get_sparse_core_info()
    nc, nw = info.num_cores, info.num_cores * info.num_subcores
    per_w = M // nw
    ch = SC_CHUNK
    nch = per_w // ch
    assert per_w * nw == M and nch * ch == per_w and nch % 2 == 0 and n_src % per_w == 0
    mesh = plsc.VectorSubcoreMesh(core_axis_name="c", subcore_axis_name="s")

    @functools.partial(
        pl.kernel, mesh=mesh, out_type=jax.ShapeDtypeStruct((n_out, width), src.dtype),
        scratch_types=[pltpu.VMEM((per_w,), jnp.int32),
                       pltpu.VMEM((ch, width), src.dtype), pltpu.VMEM((ch, width), src.dtype),
                       pltpu.SemaphoreType.DMA, pltpu.SemaphoreType.DMA,
                       pltpu.SemaphoreType.DMA, pltpu.SemaphoreType.DMA],
        name="sc_scatter" if scatter else "sc_gather")
    def move(src_hbm, idx_hbm, out_hbm, idx_v, buf0, buf1, in0, in1, out0, out1):
        wid = lax.axis_index("s") * nc + lax.axis_index("c")
        base = wid * per_w
        pltpu.sync_copy(idx_hbm.at[pl.ds(base, per_w)], idx_v)
        lin_base = lax.rem(base, n_src) if scatter else base

        def read(c, buf, sem):
            if scatter:
                return pltpu.make_async_copy(src_hbm.at[pl.ds(lin_base + c * ch, ch)], buf, sem)
            return pltpu.make_async_copy(src_hbm.at[idx_v.at[pl.ds(c * ch, ch)]], buf, sem)

        def write(c, buf, sem):
            if scatter:
                return pltpu.make_async_copy(buf, out_hbm.at[idx_v.at[pl.ds(c * ch, ch)]], sem)
            return pltpu.make_async_copy(buf, out_hbm.at[pl.ds(lin_base + c * ch, ch)], sem)

        read(0, buf0, in0).start()

        @pl.loop(0, nch, step=2)
        def _(c):
            @pl.when(c > 0)
            def _():
                write(c - 1, buf1, out1).wait()
            read(c + 1, buf1, in1).start()
            read(c, buf0, in0).wait()
            write(c, buf0, out0).start()
            read(c + 1, buf1, in1).wait()
            write(c + 1, buf1, out1).start()
            write(c, buf0, out0).wait()

            @pl.when(c + 2 < nch)
            def _():
                read(c + 2, buf0, in0).start()

        write(nch - 1, buf1, out1).wait()

    return move(src, idx)


def _final_kernel(x1_ref, y1_ref, y2_ref, meta_ref, nw_ref, o_ref):
    w1 = meta_ref[:, 4:5]
    w2 = meta_ref[:, 5:6]
    x2 = x1_ref[...] + (w1 * _unpack_rows(y1_ref[...]) + w2 * _unpack_rows(y2_ref[...]))
    o_ref[...] = _rms(x2, nw_ref[...])


def _final(x1, yg, meta, norm_w):
    T = x1.shape[0]
    tm = COMB_TILE
    nt = T // tm
    row = lambda i: (i, 0)
    return pl.pallas_call(
        _final_kernel,
        grid=(nt,),
        in_specs=[pl.BlockSpec((tm, D_MODEL), row),
                  pl.BlockSpec((tm, PACKED), row),
                  pl.BlockSpec((tm, PACKED), lambda i: (i + nt, 0)),
                  pl.BlockSpec((tm, LANES), row),
                  pl.BlockSpec((1, D_MODEL), lambda i: (0, 0))],
        out_specs=pl.BlockSpec((tm, D_MODEL), row),
        out_shape=jax.ShapeDtypeStruct((T, D_MODEL), F32),
        compiler_params=pltpu.CompilerParams(dimension_semantics=("arbitrary",),
                                             vmem_limit_bytes=VMEM_LIMIT),
        name="final",
    )(x1, yg, yg, meta, norm_w)


def _layer(x, norm_mix_w, w_in, conv_w, conv_b, b_igate, b_fgate, attn_sinks, mlstm_norm_w,
           w_attn_o, w_mlstm_o, w_out, norm_ffn_w, w_group, b_group, w_router, b_router,
           w_gate, w_up, w_down, out_norm_w):
    B, S, D = x.shape
    T = B * S
    x2 = x.reshape(T, D)

    w_tok = jnp.concatenate([w_in[:, _O_AQ:_O_MQ], w_in[:, _O_MK:_O_MV], w_in[:, _O_GA:]], axis=1).astype(BF16)
    w_chan = jnp.concatenate([w_in[:, _O_MQ:_O_MK], w_in[:, _O_MV:_O_MI]], axis=1).T.astype(BF16)
    w_gates = jnp.pad(w_in[:, _O_MI:_O_GA], ((0, 0), (0, LANES - 2 * M_HEADS))).astype(BF16)
    q, kv, mk, gab, mqt, mvt, mot, grow = _inproj(x2, norm_mix_w.reshape(1, D), w_tok, w_chan, w_gates, B, S)

    ya = _attn(attn_sinks.astype(F32), q.reshape(B, S, -1), kv.reshape(B, S, -1))

    bias = jnp.concatenate([b_igate, b_fgate]).astype(F32)
    ym = _mlstm(mqt, mk.reshape(B, S, -1), mvt, mot, grow, conv_w, conv_b,
                bias.reshape(2 * M_HEADS, 1), mlstm_norm_w)

    gpad = jnp.zeros((8 - N_GROUPS, D), F32)
    w_rt = jnp.concatenate([w_group.T, gpad, w_router.T, jnp.zeros((LANES - ROUTER_ROWS, D), F32)], axis=0)
    w_rt_hi = w_rt.astype(BF16)
    w_rt = jnp.concatenate([w_rt_hi, (w_rt - w_rt_hi.astype(F32)).astype(BF16)], axis=0)
    b_rt = jnp.concatenate([b_group, jnp.zeros((8 - N_GROUPS,), F32), b_router]).astype(F32)
    b_rt = jnp.broadcast_to(b_rt[:, None], (ROUTER_ROWS, TOK_TILE))
    x1, h2p, meta, code_rows, cnt = _outproj(x2, ya.reshape(T, -1), ym.reshape(T, -1), gab,
                                            w_attn_o.astype(BF16), w_mlstm_o.astype(BF16),
                                            w_out.astype(BF16), norm_ffn_w.reshape(1, D), w_rt, b_rt)

    counts = cnt[:, 0].astype(jnp.int32)
    nblk = (2 * T) // SLOT_BLOCK + N_EXPERTS
    pstart, blk_e, nvalid, nused = _slots(counts, nblk)
    dest = _dests(pstart, code_rows)[0:2].reshape(2 * T)

    xs = _sc_move_rows(h2p, dest, nblk * SLOT_BLOCK, scatter=True)
    ys = _experts(blk_e, nvalid, nused, xs, w_gate, w_up, w_down)
    yg = _sc_move_rows(ys, dest, 2 * T, scatter=False)
    out = _final(x1, yg, meta, out_norm_w.reshape(1, D))
    return out.reshape(B, S, D)


def kernel(x, norm_mix_w, w_in, conv_w, conv_b, b_igate, b_fgate, attn_sinks, mlstm_norm_w, w_attn_o,
           w_mlstm_o, w_out, norm_ffn_w, w_group, b_group, w_router, b_router, w_gate, w_up, w_down,
           norm_final_w):
    depth = w_in.shape[0]
    assert depth == 1, "final RMSNorm is fused into the last layer's combine kernel"
    return _layer(x, norm_mix_w[0], w_in[0], conv_w[0], conv_b[0], b_igate[0], b_fgate[0],
                  attn_sinks[0], mlstm_norm_w[0], w_attn_o[0], w_mlstm_o[0], w_out[0], norm_ffn_w[0],
                  w_group[0], b_group[0], w_router[0], b_router[0], w_gate[0], w_up[0], w_down[0],
                  norm_final_w)
```

```python
import functools

import jax
import jax.numpy as jnp
from jax import lax
from jax.experimental import pallas as pl
from jax.experimental.pallas import tpu as pltpu
from jax.experimental.pallas import tpu_sc as plsc

F32 = jnp.float32
BF16 = jnp.bfloat16

D_MODEL = 1024
N_Q_HEADS = 8
N_KV_HEADS = 2
HEAD_DIM = 64
WINDOW = 128
GQA_GROUP = N_Q_HEADS // N_KV_HEADS
M_HEADS = 4
M_QK_DIM = 64
M_V_DIM = 128
CONV_WIDTH = 4
N_GROUPS = 4
EXPERTS_PER_GROUP = 8
N_EXPERTS = N_GROUPS * EXPERTS_PER_GROUP
D_EXPERT = 512
EPS = 1e-6

ATT_Q_WIDTH = N_Q_HEADS * HEAD_DIM
ATT_KV_WIDTH = N_KV_HEADS * HEAD_DIM
M_QK_WIDTH = M_HEADS * M_QK_DIM
M_V_WIDTH = M_HEADS * M_V_DIM

LANES = 128
NEG = -1e30
VMEM_LIMIT = 56 * 1024 * 1024

TOK_TILE = 512
ATT_TILE = 512
M_CHUNK = 128
M_BATCH = 8
M_STATE_ROWS = M_V_DIM + 16
SLOT_BLOCK = 512
COMB_TILE = 512
RANK_RADIX = 65536
RANK_BITS = 16
ROUTER_ROWS = 8 + N_EXPERTS
PACKED = D_MODEL // 2
SC_CHUNK = 64
DMA_THREADS = 2

_O_AQ = 0
_O_AK = _O_AQ + ATT_Q_WIDTH
_O_AV = _O_AK + ATT_KV_WIDTH
_O_MQ = _O_AV + ATT_KV_WIDTH
_O_MK = _O_MQ + M_QK_WIDTH
_O_MV = _O_MK + M_QK_WIDTH
_O_MO = _O_MV + M_V_WIDTH
_O_MI = _O_MO + M_V_WIDTH
_O_MF = _O_MI + M_HEADS
_O_GA = _O_MF + M_HEADS
_O_GB = _O_GA + D_MODEL
_O_END = _O_GB + D_MODEL


def _rms(x, w):
    return x * lax.rsqrt(jnp.mean(x * x, axis=-1, keepdims=True) + EPS) * w


def _sigmoid(x):
    return 0.5 * jnp.tanh(0.5 * x) + 0.5


def _pack_rows(x):
    half = x.shape[1] // 2
    bits = lax.bitcast_convert_type(x.astype(BF16).astype(F32), jnp.uint32)
    packed = (bits[:, :half] >> 16) | (bits[:, half:] & jnp.uint32(0xFFFF0000))
    return lax.bitcast_convert_type(packed, jnp.int32)


def _unpack_rows(w):
    u = lax.bitcast_convert_type(w, jnp.uint32)
    lo = lax.bitcast_convert_type(u << 16, F32)
    hi = lax.bitcast_convert_type(u & jnp.uint32(0xFFFF0000), F32)
    return jnp.concatenate([lo, hi], axis=-1)


def _log_sigmoid(x):
    return jnp.minimum(x, 0.0) - jnp.log1p(jnp.exp(-jnp.abs(x)))


_TOK_SPLITS = (ATT_KV_WIDTH, M_QK_WIDTH, 2 * D_MODEL)
_CHAN_SPLITS = (ATT_Q_WIDTH, ATT_KV_WIDTH, M_QK_WIDTH, M_V_WIDTH, M_V_WIDTH)


def _inproj_kernel(x_ref, nw_ref, w_ref, wt_ref, wg_ref, *out_refs):
    tok_refs = out_refs[:len(_TOK_SPLITS)]
    chan_refs = out_refs[len(_TOK_SPLITS):len(_TOK_SPLITS) + len(_CHAN_SPLITS)]
    grow_ref = out_refs[-1]
    h = _rms(x_ref[...], nw_ref[...]).astype(BF16)
    lo = 0
    for ref, width in zip(tok_refs, _TOK_SPLITS):
        ref[...] = jnp.dot(h, w_ref[:, lo:lo + width], preferred_element_type=F32).astype(BF16)
        lo += width
    lo = 0
    for ref, width in zip(chan_refs, _CHAN_SPLITS):
        ref[0] = lax.dot_general(wt_ref[lo:lo + width, :], h, (((1,), (1,)), ((), ())),
                                 preferred_element_type=F32).astype(BF16)
        lo += width
    g = jnp.dot(h, wg_ref[...], preferred_element_type=F32)
    grow_ref[...] = g.T[0:8, :]


def _inproj(x2, norm_w, w_tok, w_chan, w_gates, B, S):
    T = x2.shape[0]
    tm = TOK_TILE
    tpb = S // tm
    row = lambda i: (i, 0)
    const = lambda i: (0, 0)
    chan = lambda i: (i // tpb, 0, i % tpb)
    return pl.pallas_call(
        _inproj_kernel,
        grid=(T // tm,),
        in_specs=[pl.BlockSpec((tm, D_MODEL), row),
                  pl.BlockSpec((1, D_MODEL), const),
                  pl.BlockSpec(w_tok.shape, const),
                  pl.BlockSpec(w_chan.shape, const),
                  pl.BlockSpec((D_MODEL, LANES), const)],
        out_specs=[*[pl.BlockSpec((tm, w), row) for w in _TOK_SPLITS],
                   *[pl.BlockSpec((1, w, tm), chan) for w in _CHAN_SPLITS],
                   pl.BlockSpec((8, tm), lambda i: (0, i))],
        out_shape=[*[jax.ShapeDtypeStruct((T, w), BF16) for w in _TOK_SPLITS],
                   *[jax.ShapeDtypeStruct((B, w, S), BF16) for w in _CHAN_SPLITS],
                   jax.ShapeDtypeStruct((8, T), F32)],
        compiler_params=pltpu.CompilerParams(dimension_semantics=("arbitrary",),
                                             vmem_limit_bytes=VMEM_LIMIT),
        name="inproj",
    )(x2, norm_w, w_tok, w_chan, w_gates)


def _attn_kernel(sink_ref, qt_ref, k_ref, kp_ref, vt_ref, vtp_ref, o_ref):
    i = pl.program_id(1)
    W = WINDOW
    G = GQA_GROUP
    u = lax.broadcasted_iota(jnp.int32, (W, W), 0)
    t = lax.broadcasted_iota(jnp.int32, (W, W), 1)
    from_prev = u > t

    def keys_values(j):
        if j == 0:
            return kp_ref[0], vtp_ref[0], k_ref[0, 0:W, :], vt_ref[0, :, 0:W]
        return (k_ref[0, (j - 1) * W:j * W, :], vt_ref[0, :, (j - 1) * W:j * W],
                k_ref[0, j * W:(j + 1) * W, :], vt_ref[0, :, j * W:(j + 1) * W])

    def scores(j, g):
        k_prev, _, k_cur, _ = keys_values(j)
        dims = slice(g * HEAD_DIM, (g + 1) * HEAD_DIM)
        qg = jnp.concatenate([qt_ref[0, h * HEAD_DIM:(h + 1) * HEAD_DIM, j * W:(j + 1) * W]
                              for h in range(g * G, (g + 1) * G)], axis=1)
        return (jnp.dot(k_prev[:, dims], qg, preferred_element_type=F32),
                jnp.dot(k_cur[:, dims], qg, preferred_element_type=F32))

    tasks = [(j, g) for j in range(ATT_TILE // W) for g in range(N_KV_HEADS)]
    ahead = scores(*tasks[0])
    pair_rows = []
    for n_task, (j, g) in enumerate(tasks):
        s_prev, s_cur = ahead
        if n_task + 1 < len(tasks):
            ahead = scores(*tasks[n_task + 1])
        cols = slice(j * W, (j + 1) * W)
        _, vt_prev, _, vt_cur = keys_values(j)
        dims = slice(g * HEAD_DIM, (g + 1) * HEAD_DIM)
        for n in range(G):
            hc = slice(n * W, (n + 1) * W)
            sp = s_prev[:, hc]
            if j == 0:
                sp = jnp.where(i > 0, sp, NEG)
            s = jnp.where(from_prev, sp, s_cur[:, hc])
            sink = sink_ref[g * G + n]
            m = jnp.maximum(jnp.max(s, axis=0, keepdims=True), sink)
            p = jnp.exp(s - m)
            inv = 1.0 / (jnp.sum(p, axis=0, keepdims=True) + jnp.exp(sink - m))
            p_prev = jnp.where(from_prev, p, 0.0).astype(BF16)
            p_cur = jnp.where(from_prev, 0.0, p).astype(BF16)
            o = jnp.dot(vt_prev[dims, :], p_prev, preferred_element_type=F32) \
                + jnp.dot(vt_cur[dims, :], p_cur, preferred_element_type=F32)
            pair_rows.append(o * inv)
        if g == N_KV_HEADS - 1:
            out = [jnp.concatenate(pair_rows[n:n + 2], axis=0).T for n in range(0, N_Q_HEADS, 2)]
            o_ref[0, cols, :] = jnp.concatenate(out, axis=1).astype(BF16)
            pair_rows = []


def _attn(sinks, qt, k3, vt):
    B, S, _ = k3.shape
    tq = ATT_TILE
    per = tq // WINDOW
    chan = lambda b, i: (b, 0, i)
    tok = lambda b, i: (b, i, 0)
    return pl.pallas_call(
        _attn_kernel,
        grid=(B, S // tq),
        in_specs=[pl.BlockSpec(memory_space=pltpu.SMEM),
                  pl.BlockSpec((1, ATT_Q_WIDTH, tq), chan),
                  pl.BlockSpec((1, tq, ATT_KV_WIDTH), tok),
                  pl.BlockSpec((1, WINDOW, ATT_KV_WIDTH), lambda b, i: (b, jnp.maximum(i * per - 1, 0), 0)),
                  pl.BlockSpec((1, ATT_KV_WIDTH, tq), chan),
                  pl.BlockSpec((1, ATT_KV_WIDTH, WINDOW), lambda b, i: (b, 0, jnp.maximum(i * per - 1, 0)))],
        out_specs=pl.BlockSpec((1, tq, ATT_Q_WIDTH), tok),
        out_shape=jax.ShapeDtypeStruct((B, S, ATT_Q_WIDTH), BF16),
        compiler_params=pltpu.CompilerParams(dimension_semantics=("arbitrary", "arbitrary"),
                                             vmem_limit_bytes=VMEM_LIMIT),
        name="attn",
    )(sinks, qt, k3, k3, vt, vt)


def _mlstm_kernel(nb, *refs):
    L = M_CHUNK
    H = M_HEADS
    mqt_ref, mk_ref, mvt_ref, mot_ref = refs[:4]
    grow_refs = refs[4:4 + nb]
    (cwq_ref, cbq_ref, cwk_ref, cbk_ref, brow_ref, nw_ref, y_ref,
     state_ref, m_ref, prevq_ref, ubuf_ref) = refs[4 + nb:]
    cidx = pl.program_id(1)

    @pl.when(cidx == 0)
    def _():
        state_ref[...] = jnp.zeros_like(state_ref)
        m_ref[...] = jnp.zeros_like(m_ref)
        prevq_ref[...] = jnp.zeros_like(prevq_ref)
        ubuf_ref[:, 0:8, :] = jnp.zeros((nb, 8, M_QK_WIDTH), F32)

    ri = lax.broadcasted_iota(jnp.int32, (L, L), 0)
    ci = lax.broadcasted_iota(jnp.int32, (L, L), 1)
    causal_t = ri <= ci
    triu = jnp.where(causal_t, 1.0, 0.0).astype(F32)
    lane = lax.broadcasted_iota(jnp.int32, (8, L), 1)
    r2 = lax.broadcasted_iota(jnp.int32, (2 * L, L), 0)
    c2 = lax.broadcasted_iota(jnp.int32, (2 * L, L), 1)
    shifts = [jnp.where(r2 - c2 == L - k, 1.0, 0.0).astype(BF16) for k in range(1, CONV_WIDTH)]
    ones_rows = jnp.where(lax.broadcasted_iota(jnp.int32, (M_STATE_ROWS - M_V_DIM, L), 0) == 0,
                          1.0, 0.0).astype(BF16)

    pairs = [(bb, h) for bb in range(nb) for h in range(H)]
    states = [state_ref[bb * H + h] for bb, h in pairs]
    m_prevs = [m_ref[bb] for bb in range(nb)]

    seqs = []
    for bb in range(nb):
        cur = mqt_ref[bb]
        both = jnp.concatenate([prevq_ref[bb], cur], axis=1)
        acc = cbq_ref[...] + cwq_ref[CONV_WIDTH - 1] * cur.astype(F32)
        for k in range(1, CONV_WIDTH):
            acc = acc + cwq_ref[CONV_WIDTH - 1 - k] * jnp.dot(both, shifts[k - 1],
                                                              preferred_element_type=F32)
        qt = (acc * _sigmoid(acc)).astype(BF16)
        ubuf_ref[bb, 8:L + 8, :] = mk_ref[bb].astype(F32)
        acc = cbk_ref[...] + cwk_ref[CONV_WIDTH - 1:CONV_WIDTH, :] * ubuf_ref[bb, 8:L + 8, :]
        for j in range(CONV_WIDTH - 1):
            off = 8 - (CONV_WIDTH - 1) + j
            acc = acc + cwk_ref[j:j + 1, :] * ubuf_ref[bb, off:off + L, :]
        ubuf_ref[bb, 0:8, :] = ubuf_ref[bb, L:L + 8, :]
        kk =(acc * _sigmoid(acc) * (M_QK_DIM ** -0.5)).astype(BF16)

        gr = grow_refs[bb][...] + brow_ref[...]
        b = jnp.dot(_log_sigmoid(gr), triu, preferred_element_type=F32,
                    precision=lax.Precision.HIGHEST)
        gi = pltpu.roll(gr, 4, axis=0)
        u = gi - b
        cm = u
        for sh in (1, 2, 4, 8, 16, 32, 64):
            if sh < L:
                cm = jnp.maximum(cm, jnp.where(lane >= sh, pltpu.roll(cm, sh, axis=1), NEG))
        m_prev = m_prevs[bb]
        mt = b + jnp.maximum(m_prev, cm)
        wa = jnp.exp(b + m_prev - mt)
        emt = jnp.exp(-mt)
        m_new = jnp.broadcast_to(mt[:, L - 1:L], (8, L))
        b_last = jnp.broadcast_to(b[:, L - 1:L], (8, L))
        wc = jnp.exp(b_last + m_prev - m_new)
        ws = jnp.exp(b_last + u - m_new)
        seqs.append((qt, kk, b - mt, u.T, wa, emt, wc, ws, m_new))

    new_states, outs = [], []
    for (bb, h), state in zip(pairs, states):
        qt, kk, bmt, ucol, wa, emt, wc, ws, _ = seqs[bb]
        row = slice(H + h, H + h + 1)
        qh = qt[h * M_QK_DIM:(h + 1) * M_QK_DIM, :]
        kh = kk[:, h * M_QK_DIM:(h + 1) * M_QK_DIM]
        wq = jnp.exp(jnp.where(causal_t, bmt[row, :] + ucol[:, H + h:H + h + 1], NEG))
        sw = (jnp.dot(kh, qh, preferred_element_type=F32) * wq).astype(BF16)
        vext = jnp.concatenate([mvt_ref[bb, h * M_V_DIM:(h + 1) * M_V_DIM, :], ones_rows], axis=0)
        res = wa[row, :] * jnp.dot(state.astype(BF16), qh, preferred_element_type=F32) \
            + jnp.dot(vext, sw, preferred_element_type=F32)
        num = res[:M_V_DIM, :]
        den = res[M_V_DIM:M_V_DIM + 1, :]
        hb = num / jnp.maximum(jnp.abs(den), emt[row, :])
        vw = (vext.astype(F32) * ws[row, :]).astype(BF16)
        new_states.append(wc[row, 0:1] * state + jnp.dot(vw, kh, preferred_element_type=F32))
        hn = hb * lax.rsqrt(jnp.mean(hb * hb, axis=0, keepdims=True) + EPS)
        hn = hn * nw_ref[h * M_V_DIM:(h + 1) * M_V_DIM, :]
        og = _sigmoid(mot_ref[bb, h * M_V_DIM:(h + 1) * M_V_DIM, :].astype(F32))
        outs.append((og * hn).T.astype(BF16))

    for i, (bb, h) in enumerate(pairs):
        state_ref[bb * H + h] = new_states[i]
    for bb in range(nb):
        m_ref[bb] = seqs[bb][8]
        prevq_ref[bb] = mqt_ref[bb]
        y_ref[bb] = jnp.concatenate(outs[bb * H:(bb + 1) * H], axis=1)


def _mlstm(mqt, mk3, mvt, mot, grow, conv_w, conv_b, bias_row, norm_w):
    B, S, _ = mk3.shape
    L = M_CHUNK
    nb = max(d for d in range(1, M_BATCH + 1) if B % d == 0)
    nc = S // L
    assert L == LANES, "per-head scalars are kept lane-replicated next to (8, L) gate rows"
    tok = lambda b, c: (b, c, 0)
    chan = lambda b, c: (b, 0, c)
    const2 = lambda b, c: (0, 0)
    const3 = lambda b, c: (0, 0, 0)
    grow_specs = [pl.BlockSpec((8, L), functools.partial(lambda b, c, n: (0, (b * nb + n) * nc + c), n=n))
                  for n in range(nb)]
    rep = lambda v: jnp.broadcast_to(v.astype(F32)[..., None], v.shape + (L,))
    cwq, cbq = rep(conv_w[:, :M_QK_WIDTH]), rep(conv_b[:M_QK_WIDTH])
    cwk, cbk = conv_w[:, M_QK_WIDTH:].astype(F32), conv_b[M_QK_WIDTH:].reshape(1, -1).astype(F32)
    return pl.pallas_call(
        functools.partial(_mlstm_kernel, nb),
        grid=(B // nb, nc),
        in_specs=[pl.BlockSpec((nb, M_QK_WIDTH, L), chan),
                  pl.BlockSpec((nb, L, M_QK_WIDTH), tok),
                  pl.BlockSpec((nb, M_V_WIDTH, L), chan),
                  pl.BlockSpec((nb, M_V_WIDTH, L), chan),
                  *grow_specs,
                  pl.BlockSpec((CONV_WIDTH, M_QK_WIDTH, L), const3),
                  pl.BlockSpec((M_QK_WIDTH, L), const2),
                  pl.BlockSpec((CONV_WIDTH, M_QK_WIDTH), const2),
                  pl.BlockSpec((1, M_QK_WIDTH), const2),
                  pl.BlockSpec((8, 1), const2),
                  pl.BlockSpec((M_V_WIDTH, L), const2)],
        out_specs=pl.BlockSpec((nb, L, M_V_WIDTH), tok),
        out_shape=jax.ShapeDtypeStruct((B, S, M_V_WIDTH), BF16),
        scratch_shapes=[pltpu.VMEM((nb * M_HEADS, M_STATE_ROWS, M_QK_DIM), F32),
                        pltpu.VMEM((nb, 8, LANES), F32),
                        pltpu.VMEM((nb, M_QK_WIDTH, L), BF16),
                        pltpu.VMEM((nb, L + 8, M_QK_WIDTH), F32)],
        compiler_params=pltpu.CompilerParams(dimension_semantics=("arbitrary", "arbitrary"),
                                             vmem_limit_bytes=VMEM_LIMIT),
        name="mlstm",
    )(mqt, mk3, mvt, mot, *([grow] * nb), cwq, cbq, cwk, cbk, bias_row, rep(norm_w))


def _outproj_kernel(x_ref, ya_ref, ym_ref, gab_ref, wa_ref, wm_ref, wo_ref, nw_ref, wr_ref, br_ref,
                    x1_ref, h2_ref, meta_ref, code_ref, cnt_ref, cnt_scr):
    tm = TOK_TILE
    i = pl.program_id(0)

    @pl.when(i == 0)
    def _():
        cnt_scr[...] = jnp.zeros_like(cnt_scr)

    pa = jnp.dot(ya_ref[...], wa_ref[...], preferred_element_type=F32)
    pm = jnp.dot(ym_ref[...], wm_ref[...], preferred_element_type=F32)
    ga = gab_ref[:, :D_MODEL].astype(F32)
    gb = gab_ref[:, D_MODEL:].astype(F32)
    mix = _sigmoid(ga) * pa + _sigmoid(gb) * pm
    x1 = x_ref[...] + jnp.dot(mix.astype(BF16), wo_ref[...], preferred_element_type=F32)
    x1_ref[...] = x1
    h2 = _rms(x1, nw_ref[...])
    h2_ref[...] = _pack_rows(h2)

    nt = (((1,), (1,)), ((), ()))
    h_hi = h2.astype(BF16)
    h_lo = (h2 - h_hi.astype(F32)).astype(BF16)
    hw = lax.dot_general(wr_ref[...], h_hi, nt, preferred_element_type=F32)
    logits = (hw[:LANES] + hw[LANES:]
              + lax.dot_general(wr_ref[:LANES, :], h_lo, nt, preferred_element_type=F32))[:ROUTER_ROWS]
    logits = logits + br_ref[...]
    row8 = lax.broadcasted_iota(jnp.int32, (8, tm), 0)

    def first_argmax(v):
        mx = jnp.max(v, axis=0, keepdims=True)
        idx = jnp.min(jnp.where(v == mx, row8, 8), axis=0, keepdims=True)
        return mx, idx

    gl = jnp.where(row8 < N_GROUPS, logits[0:8], NEG)
    gmax, gi = first_argmax(gl)
    gp = 1.0 / jnp.sum(jnp.exp(gl - gmax), axis=0, keepdims=True)
    el = jnp.zeros((8, tm), F32)
    for g in range(N_GROUPS):
        el = jnp.where(gi == g, logits[8 + 8 * g:16 + 8 * g], el)
    v1, j1 = first_argmax(el)
    v2, j2 = first_argmax(jnp.where(row8 == j1, NEG, el))
    t = jnp.exp(v2 - v1)
    w1 = gp / (1.0 + t)
    w2 = gp * t / (1.0 + t)
    e1 = gi * EXPERTS_PER_GROUP + j1
    e2 = gi * EXPERTS_PER_GROUP + j2

    erow = lax.broadcasted_iota(jnp.int32, (N_EXPERTS, tm), 0)
    hit1 = erow == e1
    hit2 = erow == e2
    onehot = jnp.where(hit1 | hit2, 1.0, 0.0)
    ri = lax.broadcasted_iota(jnp.int32, (tm, tm), 0)
    ci = lax.broadcasted_iota(jnp.int32, (tm, tm), 1)
    before = jnp.where(ri < ci, 1.0, 0.0).astype(BF16)
    rank = jnp.dot(onehot.astype(BF16), before, preferred_element_type=F32) + cnt_scr[...]
    r1 = jnp.sum(jnp.where(hit1, rank, 0.0), axis=0, keepdims=True)
    r2 = jnp.sum(jnp.where(hit2, rank, 0.0), axis=0, keepdims=True)
    cnt = jnp.broadcast_to((rank + onehot)[:, tm - 1:tm], (N_EXPERTS, tm))
    cnt_scr[...] = cnt
    cnt_ref[...] = cnt[:, :LANES]

    c1 = e1 * RANK_RADIX + r1.astype(jnp.int32)
    c2 = e2 * RANK_RADIX + r2.astype(jnp.int32)
    code_ref[...] = jnp.where(row8 == 0, c1, jnp.where(row8 == 1, c2, 0))
    wrow = lax.broadcasted_iota(jnp.int32, (LANES, tm), 0)
    meta_ref[...] = jnp.where(wrow == 4, w1, jnp.where(wrow == 5, w2, 0.0)).T


def _outproj(x2, ya, ym, gab, wa, wm, wo, norm_w, w_rt, b_rt):
    T = x2.shape[0]
    tm = TOK_TILE
    row = lambda i: (i, 0)
    const = lambda i: (0, 0)
    return pl.pallas_call(
        _outproj_kernel,
        grid=(T // tm,),
        in_specs=[pl.BlockSpec((tm, D_MODEL), row),
                  pl.BlockSpec((tm, ATT_Q_WIDTH), row),
                  pl.BlockSpec((tm, M_V_WIDTH), row),
                  pl.BlockSpec((tm, 2 * D_MODEL), row),
                  pl.BlockSpec((ATT_Q_WIDTH, D_MODEL), const),
                  pl.BlockSpec((M_V_WIDTH, D_MODEL), const),
                  pl.BlockSpec((D_MODEL, D_MODEL), const),
                  pl.BlockSpec((1, D_MODEL), const),
                  pl.BlockSpec((2 * LANES, D_MODEL), const),
                  pl.BlockSpec((ROUTER_ROWS, tm), const)],
        out_specs=[pl.BlockSpec((tm, D_MODEL), row),
                   pl.BlockSpec((tm, PACKED), row),
                   pl.BlockSpec((tm, LANES), row),
                   pl.BlockSpec((8, tm), lambda i: (0, i)),
                   pl.BlockSpec((N_EXPERTS, LANES), const)],
        out_shape=[jax.ShapeDtypeStruct((T, D_MODEL), F32),
                   jax.ShapeDtypeStruct((T, PACKED), jnp.int32),
                   jax.ShapeDtypeStruct((T, LANES), F32),
                   jax.ShapeDtypeStruct((8, T), jnp.int32),
                   jax.ShapeDtypeStruct((N_EXPERTS, LANES), F32)],
        scratch_shapes=[pltpu.VMEM((N_EXPERTS, tm), F32)],
        compiler_params=pltpu.CompilerParams(dimension_semantics=("arbitrary",),
                                             vmem_limit_bytes=VMEM_LIMIT),
        name="outproj",
    )(x2, ya, ym, gab, wa, wm, wo, norm_w, w_rt, b_rt)


def _slots_kernel(cnt_ref, pstart_ref, blk_e_ref, nvalid_ref, nused_ref):
    R = SLOT_BLOCK
    nblk = blk_e_ref.shape[0]

    def per_expert(e, run):
        c = cnt_ref[e]
        end = run + ((c + R - 1) // R) * R
        pstart_ref[e] = run

        def set_blk(b, carry):
            blk_e_ref[b] = e
            nvalid_ref[b] = jnp.minimum(run + c - b * R, R)
            return carry
        lax.fori_loop(run // R, end // R, set_blk, 0)
        return end

    total = lax.fori_loop(0, N_EXPERTS, per_expert, 0)
    nused_ref[0] = total // R

    def tail_blk(b, carry):
        blk_e_ref[b] = N_EXPERTS - 1
        nvalid_ref[b] = 0
        return carry
    lax.fori_loop(total // R, nblk, tail_blk, 0)


def _slots(counts, nblk):
    smem = pl.BlockSpec(memory_space=pltpu.SMEM)
    return pl.pallas_call(
        _slots_kernel,
        in_specs=[smem],
        out_specs=[smem, smem, smem, smem],
        out_shape=[jax.ShapeDtypeStruct((N_EXPERTS,), jnp.int32),
                   jax.ShapeDtypeStruct((nblk,), jnp.int32),
                   jax.ShapeDtypeStruct((nblk,), jnp.int32),
                   jax.ShapeDtypeStruct((1,), jnp.int32)],
        name="slots",
    )(counts)


def _experts_kernel(blk_e_ref, nvalid_ref, nused_ref, xs_ref, wg_ref, wu_ref, wd_ref, ys_ref,
                    wg_bf, wu_bf, wd_bf):
    j = pl.program_id(0)
    R = SLOT_BLOCK
    valid = j < nused_ref[0]
    changed = jnp.logical_or(j == 0, blk_e_ref[j] != blk_e_ref[jnp.maximum(j - 1, 0)])

    @pl.when(jnp.logical_and(valid, changed))
    def _():
        wg_bf[...] = wg_ref[0].astype(BF16)
        wu_bf[...] = wu_ref[0].astype(BF16)
        wd_bf[...] = wd_ref[0].astype(BF16)

    @pl.when(valid)
    def _():
        rows = lax.broadcasted_iota(jnp.int32, (R, PACKED), 0)
        xw = jnp.where(rows < nvalid_ref[j], xs_ref[...], 0)
        xb = _unpack_rows(xw).astype(BF16)
        g = jnp.dot(xb, wg_bf[...], preferred_element_type=F32)
        u = jnp.dot(xb, wu_bf[...], preferred_element_type=F32)
        act = (g * _sigmoid(g) * u).astype(BF16)
        ys_ref[...] = _pack_rows(jnp.dot(act, wd_bf[...], preferred_element_type=F32))

    @pl.when(jnp.logical_not(valid))
    def _():
        ys_ref[...] = jnp.zeros_like(ys_ref)


def _experts(blk_e, nvalid, nused, xs, w_gate, w_up, w_down):
    nblk = blk_e.shape[0]
    R = SLOT_BLOCK
    wspec = lambda shape: pl.BlockSpec((1,) + shape, lambda j, be, nv, nu: (be[j], 0, 0))
    grid_spec = pltpu.PrefetchScalarGridSpec(
        num_scalar_prefetch=3,
        grid=(nblk,),
        in_specs=[pl.BlockSpec((R, PACKED), lambda j, be, nv, nu: (jnp.minimum(j, nu[0] - 1), 0)),
                  wspec((D_MODEL, D_EXPERT)),
                  wspec((D_MODEL, D_EXPERT)),
                  wspec((D_EXPERT, D_MODEL))],
        out_specs=pl.BlockSpec((R, PACKED), lambda j, be, nv, nu: (j, 0)),
        scratch_shapes=[pltpu.VMEM((D_MODEL, D_EXPERT), BF16),
                        pltpu.VMEM((D_MODEL, D_EXPERT), BF16),
                        pltpu.VMEM((D_EXPERT, D_MODEL), BF16)],
    )
    return pl.pallas_call(
        _experts_kernel,
        grid_spec=grid_spec,
        out_shape=jax.ShapeDtypeStruct((nblk * R, PACKED), jnp.int32),
        compiler_params=pltpu.CompilerParams(dimension_semantics=("arbitrary",),
                                             vmem_limit_bytes=VMEM_LIMIT),
        name="experts",
    )(blk_e, nvalid, nused, xs, w_gate, w_up, w_down)


def _dest_kernel(pstart_ref, code_ref, dest_ref):
    c = code_ref[...]
    e = c >> RANK_BITS
    base = jnp.zeros_like(c)
    for k in range(N_EXPERTS):
        base = jnp.where(e == k, pstart_ref[k], base)
    dest_ref[...] = base + (c & (RANK_RADIX - 1))


def _dests(pstart, code_rows):
    T = code_rows.shape[1]
    tc = min(4096, T)
    return pl.pallas_call(
        _dest_kernel,
        grid=(T // tc,),
        in_specs=[pl.BlockSpec(memory_space=pltpu.SMEM),
                  pl.BlockSpec((8, tc), lambda i: (0, i))],
        out_specs=pl.BlockSpec((8, tc), lambda i: (0, i)),
        out_shape=jax.ShapeDtypeStruct((8, T), jnp.int32),
        name="dests",
    )(pstart, code_rows)


def _sc_move_rows(src, idx, n_out, scatter):
    M = idx.shape[0]
    n_src, width = src.shape
    info = plsc.get_sparse_core_info()
    nc, nw = info.num_cores, info.num_cores * info.num_subcores
    per_w = M // nw
    ch = SC_CHUNK
    nch = per_w // ch
    assert per_w * nw == M and nch * ch == per_w and nch % 2 == 0 and n_src % per_w == 0
    mesh = plsc.VectorSubcoreMesh(core_axis_name="c", subcore_axis_name="s")

    @functools.partial(
        pl.kernel, mesh=mesh, out_type=jax.ShapeDtypeStruct((n_out, width), src.dtype),
        scratch_types=[pltpu.VMEM((per_w,), jnp.int32),
                       pltpu.VMEM((ch, width), src.dtype), pltpu.VMEM((ch, width), src.dtype),
                       pltpu.SemaphoreType.DMA, pltpu.SemaphoreType.DMA,
                       pltpu.SemaphoreType.DMA, pltpu.SemaphoreType.DMA],
        name="sc_scatter" if scatter else "sc_gather")
    def move(src_hbm, idx_hbm, out_hbm, idx_v, buf0, buf1, in0, in1, out0, out1):
        wid = lax.axis_index("s") * nc + lax.axis_index("c")
        base = wid * per_w
        pltpu.sync_copy(idx_hbm.at[pl.ds(base, per_w)], idx_v)
        lin_base = lax.rem(base, n_src) if scatter else base

        def read(c, buf, sem):
            if scatter:
                return pltpu.make_async_copy(src_hbm.at[pl.ds(lin_base + c * ch, ch)], buf, sem)
            return pltpu.make_async_copy(src_hbm.at[idx_v.at[pl.ds(c * ch, ch)]], buf, sem)

        def write(c, buf, sem):
            if scatter:
                return pltpu.make_async_copy(buf, out_hbm.at[idx_v.at[pl.ds(c * ch, ch)]], sem)
            return pltpu.make_async_copy(buf, out_hbm.at[pl.ds(lin_base + c * ch, ch)], sem)

        read(0, buf0, in0).start()

        @pl.loop(0, nch, step=2)
        def _(c):
            @pl.when(c > 0)
            def _():
                write(c - 1, buf1, out1).wait()
            read(c + 1, buf1, in1).start()
            read(c, buf0, in0).wait()
            write(c, buf0, out0).start()
            read(c + 1, buf1, in1).wait()
            write(c + 1, buf1, out1).start()
            write(c, buf0, out0).wait()

            @pl.when(c + 2 < nch)
            def _():
                read(c + 2, buf0, in0).start()

        write(nch - 1, buf1, out1).wait()

    return move(src, idx)


def _final_kernel(x1_ref, y1_ref, y2_ref, meta_ref, nw_ref, o_ref):
    w1 = meta_ref[:, 4:5]
    w2 = meta_ref[:, 5:6]
    x2 = x1_ref[...] + (w1 * _unpack_rows(y1_ref[...]) + w2 * _unpack_rows(y2_ref[...]))
    o_ref[...] = _rms(x2, nw_ref[...])


def _final(x1, yg, meta, norm_w):
    T = x1.shape[0]
    tm = COMB_TILE
    nt = T // tm
    row = lambda i: (i, 0)
    return pl.pallas_call(
        _final_kernel,
        grid=(nt,),
        in_specs=[pl.BlockSpec((tm, D_MODEL), row),
                  pl.BlockSpec((tm, PACKED), row),
                  pl.BlockSpec((tm, PACKED), lambda i: (i + nt, 0)),
                  pl.BlockSpec((tm, LANES), row),
                  pl.BlockSpec((1, D_MODEL), lambda i: (0, 0))],
        out_specs=pl.BlockSpec((tm, D_MODEL), row),
        out_shape=jax.ShapeDtypeStruct((T, D_MODEL), F32),
        compiler_params=pltpu.CompilerParams(dimension_semantics=("arbitrary",),
                                             vmem_limit_bytes=VMEM_LIMIT),
        name="final",
    )(x1, yg, yg, meta, norm_w)


def _layer(x, norm_mix_w, w_in, conv_w, conv_b, b_igate, b_fgate, attn_sinks, mlstm_norm_w,
           w_attn_o, w_mlstm_o, w_out, norm_ffn_w, w_group, b_group, w_router, b_router,
           w_gate, w_up, w_down, out_norm_w):
    B, S, D = x.shape
    T = B * S
    x2 = x.reshape(T, D)

    w_tok = jnp.concatenate([w_in[:, _O_AK:_O_AV], w_in[:, _O_MK:_O_MV], w_in[:, _O_GA:]], axis=1).astype(BF16)
    w_chan = jnp.concatenate([w_in[:, _O_AQ:_O_AK] * (HEAD_DIM ** -0.5), w_in[:, _O_AV:_O_MQ],
                              w_in[:, _O_MQ:_O_MK], w_in[:, _O_MV:_O_MI]], axis=1).T.astype(BF16)
    w_gates = jnp.pad(w_in[:, _O_MI:_O_GA], ((0, 0), (0, LANES - 2 * M_HEADS))).astype(BF16)
    ak, mk, gab, aqt, avt, mqt, mvt, mot, grow = _inproj(x2, norm_mix_w.reshape(1, D), w_tok, w_chan,
                                                         w_gates, B, S)

    ya = _attn(attn_sinks.astype(F32), aqt, ak.reshape(B, S, -1), avt)

    bias = jnp.concatenate([b_igate, b_fgate]).astype(F32)
    ym = _mlstm(mqt, mk.reshape(B, S, -1), mvt, mot, grow, conv_w, conv_b,
                bias.reshape(2 * M_HEADS, 1), mlstm_norm_w)

    gpad = jnp.zeros((8 - N_GROUPS, D), F32)
    w_rt = jnp.concatenate([w_group.T, gpad, w_router.T, jnp.zeros((LANES - ROUTER_ROWS, D), F32)], axis=0)
    w_rt_hi = w_rt.astype(BF16)
    w_rt = jnp.concatenate([w_rt_hi, (w_rt - w_rt_hi.astype(F32)).astype(BF16)], axis=0)
    b_rt = jnp.concatenate([b_group, jnp.zeros((8 - N_GROUPS,), F32), b_router]).astype(F32)
    b_rt = jnp.broadcast_to(b_rt[:, None], (ROUTER_ROWS, TOK_TILE))
    x1, h2p, meta, code_rows, cnt = _outproj(x2, ya.reshape(T, -1), ym.reshape(T, -1), gab,
                                            w_attn_o.astype(BF16), w_mlstm_o.astype(BF16),
                                            w_out.astype(BF16), norm_ffn_w.reshape(1, D), w_rt, b_rt)

    counts = cnt[:, 0].astype(jnp.int32)
    nblk = (2 * T) // SLOT_BLOCK + N_EXPERTS
    pstart, blk_e, nvalid, nused = _slots(counts, nblk)
    dest = _dests(pstart, code_rows)[0:2].reshape(2 * T)

    xs = _sc_move_rows(h2p, dest, nblk * SLOT_BLOCK, scatter=True)
    ys = _experts(blk_e, nvalid, nused, xs, w_gate, w_up, w_down)
    yg = _sc_move_rows(ys, dest, 2 * T, scatter=False)
    out = _final(x1, yg, meta, out_norm_w.reshape(1, D))
    return out.reshape(B, S, D)


def kernel(x, norm_mix_w, w_in, conv_w, conv_b, b_igate, b_fgate, attn_sinks, mlstm_norm_w, w_attn_o,
           w_mlstm_o, w_out, norm_ffn_w, w_group, b_group, w_router, b_router, w_gate, w_up, w_down,
           norm_final_w):
    depth = w_in.shape[0]
    assert depth == 1, "final RMSNorm is fused into the last layer's combine kernel"
    return _layer(x, norm_mix_w[0], w_in[0], conv_w[0], conv_b[0], b_igate[0], b_fgate[0],
                  attn_sinks[0], mlstm_norm_w[0], w_attn_o[0], w_mlstm_o[0], w_out[0], norm_ffn_w[0],
                  w_group[0], b_group[0], w_router[0], b_router[0], w_gate[0], w_up[0], w_down[0],
                  norm_final_w)
```

```python
import functools

import jax
import jax.numpy as jnp
from jax import lax
from jax.experimental import pallas as pl
from jax.experimental.pallas import tpu as pltpu
from jax.experimental.pallas import tpu_sc as plsc

F32 = jnp.float32
BF16 = jnp.bfloat16

D_MODEL = 1024
N_Q_HEADS = 8
N_KV_HEADS = 2
HEAD_DIM = 64
WINDOW = 128
GQA_GROUP = N_Q_HEADS // N_KV_HEADS
M_HEADS = 4
M_QK_DIM = 64
M_V_DIM = 128
CONV_WIDTH = 4
N_GROUPS = 4
EXPERTS_PER_GROUP = 8
N_EXPERTS = N_GROUPS * EXPERTS_PER_GROUP
D_EXPERT = 512
EPS = 1e-6

ATT_Q_WIDTH = N_Q_HEADS * HEAD_DIM
ATT_KV_WIDTH = N_KV_HEADS * HEAD_DIM
M_QK_WIDTH = M_HEADS * M_QK_DIM
M_V_WIDTH = M_HEADS * M_V_DIM

LANES = 128
NEG = -1e30
VMEM_LIMIT = 56 * 1024 * 1024

TOK_TILE = 512
ATT_TILE = 512
M_CHUNK = 128
M_BATCH = 8
M_STATE_ROWS = M_V_DIM + 16
SLOT_BLOCK = 512
COMB_TILE = 512
RANK_RADIX = 65536
RANK_BITS = 16
ROUTER_ROWS = 8 + N_EXPERTS
PACKED = D_MODEL // 2
SC_CHUNK = 64
DMA_THREADS = 2

_O_AQ = 0
_O_AK = _O_AQ + ATT_Q_WIDTH
_O_AV = _O_AK + ATT_KV_WIDTH
_O_MQ = _O_AV + ATT_KV_WIDTH
_O_MK = _O_MQ + M_QK_WIDTH
_O_MV = _O_MK + M_QK_WIDTH
_O_MO = _O_MV + M_V_WIDTH
_O_MI = _O_MO + M_V_WIDTH
_O_MF = _O_MI + M_HEADS
_O_GA = _O_MF + M_HEADS
_O_GB = _O_GA + D_MODEL
_O_END = _O_GB + D_MODEL


def _rms(x, w):
    return x * lax.rsqrt(jnp.mean(x * x, axis=-1, keepdims=True) + EPS) * w


def _sigmoid(x):
    return 0.5 * jnp.tanh(0.5 * x) + 0.5


def _pack_rows(x):
    half = x.shape[1] // 2
    bits = lax.bitcast_convert_type(x.astype(BF16).astype(F32), jnp.uint32)
    packed = (bits[:, :half] >> 16) | (bits[:, half:] & jnp.uint32(0xFFFF0000))
    return lax.bitcast_convert_type(packed, jnp.int32)


def _unpack_rows(w):
    u = lax.bitcast_convert_type(w, jnp.uint32)
    lo = lax.bitcast_convert_type(u << 16, F32)
    hi = lax.bitcast_convert_type(u & jnp.uint32(0xFFFF0000), F32)
    return jnp.concatenate([lo, hi], axis=-1)


def _log_sigmoid(x):
    return jnp.minimum(x, 0.0) - jnp.log1p(jnp.exp(-jnp.abs(x)))


_TOK_SPLITS = (ATT_KV_WIDTH, M_QK_WIDTH, 2 * D_MODEL)
_CHAN_SPLITS = (ATT_Q_WIDTH, ATT_KV_WIDTH, M_QK_WIDTH, M_V_WIDTH, M_V_WIDTH)


def _inproj_kernel(x_ref, nw_ref, w_ref, wt_ref, wg_ref, eg_ref, eu_ref, ed_ref, *out_refs):
    tok_refs = out_refs[:len(_TOK_SPLITS)]
    chan_refs = out_refs[len(_TOK_SPLITS):len(_TOK_SPLITS) + len(_CHAN_SPLITS)]
    grow_ref = out_refs[len(_TOK_SPLITS) + len(_CHAN_SPLITS)]
    for src, dst in zip((eg_ref, eu_ref, ed_ref), out_refs[-3:]):
        dst[...] = src[...].astype(BF16)
    h = _rms(x_ref[...], nw_ref[...]).astype(BF16)
    lo = 0
    for ref, width in zip(tok_refs, _TOK_SPLITS):
        ref[...] = jnp.dot(h, w_ref[:, lo:lo + width], preferred_element_type=F32).astype(BF16)
        lo += width
    lo = 0
    for ref, width in zip(chan_refs, _CHAN_SPLITS):
        ref[0] = lax.dot_general(wt_ref[lo:lo + width, :], h, (((1,), (1,)), ((), ())),
                                 preferred_element_type=F32).astype(BF16)
        lo += width
    g = jnp.dot(h, wg_ref[...], preferred_element_type=F32)
    grow_ref[...] = g.T[0:8, :]


def _inproj(x2, norm_w, w_tok, w_chan, w_gates, expert_ws, B, S):
    T = x2.shape[0]
    tm = TOK_TILE
    tpb = S // tm
    nsteps = T // tm
    row = lambda i: (i, 0)
    const = lambda i: (0, 0)
    chan = lambda i: (i // tpb, 0, i % tpb)
    cast_specs = [pl.BlockSpec((w.shape[0] // nsteps, w.shape[1]), row) for w in expert_ws]
    assert all(w.shape[0] % (8 * nsteps) == 0 for w in expert_ws)
    return pl.pallas_call(
        _inproj_kernel,
        grid=(nsteps,),
        in_specs=[pl.BlockSpec((tm, D_MODEL), row),
                  pl.BlockSpec((1, D_MODEL), const),
                  pl.BlockSpec(w_tok.shape, const),
                  pl.BlockSpec(w_chan.shape, const),
                  pl.BlockSpec((D_MODEL, LANES), const),
                  *cast_specs],
        out_specs=[*[pl.BlockSpec((tm, w), row) for w in _TOK_SPLITS],
                   *[pl.BlockSpec((1, w, tm), chan) for w in _CHAN_SPLITS],
                   pl.BlockSpec((8, tm), lambda i: (0, i)),
                   *cast_specs],
        out_shape=[*[jax.ShapeDtypeStruct((T, w), BF16) for w in _TOK_SPLITS],
                   *[jax.ShapeDtypeStruct((B, w, S), BF16) for w in _CHAN_SPLITS],
                   jax.ShapeDtypeStruct((8, T), F32),
                   *[jax.ShapeDtypeStruct(w.shape, BF16) for w in expert_ws]],
        compiler_params=pltpu.CompilerParams(dimension_semantics=("arbitrary",),
                                             vmem_limit_bytes=VMEM_LIMIT),
        name="inproj",
    )(x2, norm_w, w_tok, w_chan, w_gates, *expert_ws)


def _attn_kernel(sink_ref, qt_ref, k_ref, kp_ref, vt_ref, vtp_ref, o_ref):
    i = pl.program_id(1)
    W = WINDOW
    G = GQA_GROUP
    u = lax.broadcasted_iota(jnp.int32, (W, W), 0)
    t = lax.broadcasted_iota(jnp.int32, (W, W), 1)
    from_prev = u > t

    def keys_values(j):
        if j == 0:
            return kp_ref[0], vtp_ref[0], k_ref[0, 0:W, :], vt_ref[0, :, 0:W]
        return (k_ref[0, (j - 1) * W:j * W, :], vt_ref[0, :, (j - 1) * W:j * W],
                k_ref[0, j * W:(j + 1) * W, :], vt_ref[0, :, j * W:(j + 1) * W])

    def scores(j, g):
        k_prev, _, k_cur, _ = keys_values(j)
        dims = slice(g * HEAD_DIM, (g + 1) * HEAD_DIM)
        qg = jnp.concatenate([qt_ref[0, h * HEAD_DIM:(h + 1) * HEAD_DIM, j * W:(j + 1) * W]
                              for h in range(g * G, (g + 1) * G)], axis=1)
        return (jnp.dot(k_prev[:, dims], qg, preferred_element_type=F32),
                jnp.dot(k_cur[:, dims], qg, preferred_element_type=F32))

    tasks = [(j, g) for j in range(ATT_TILE // W) for g in range(N_KV_HEADS)]
    ahead = scores(*tasks[0])
    pair_rows = []
    for n_task, (j, g) in enumerate(tasks):
        s_prev, s_cur = ahead
        if n_task + 1 < len(tasks):
            ahead = scores(*tasks[n_task + 1])
        cols = slice(j * W, (j + 1) * W)
        _, vt_prev, _, vt_cur = keys_values(j)
        dims = slice(g * HEAD_DIM, (g + 1) * HEAD_DIM)
        for n in range(G):
            hc = slice(n * W, (n + 1) * W)
            sp = s_prev[:, hc]
            if j == 0:
                sp = jnp.where(i > 0, sp, NEG)
            s = jnp.where(from_prev, sp, s_cur[:, hc])
            sink = sink_ref[g * G + n]
            m = jnp.maximum(jnp.max(s, axis=0, keepdims=True), sink)
            p = jnp.exp(s - m)
            inv = 1.0 / (jnp.sum(p, axis=0, keepdims=True) + jnp.exp(sink - m))
            p_prev = jnp.where(from_prev, p, 0.0).astype(BF16)
            p_cur = jnp.where(from_prev, 0.0, p).astype(BF16)
            o = jnp.dot(vt_prev[dims, :], p_prev, preferred_element_type=F32) \
                + jnp.dot(vt_cur[dims, :], p_cur, preferred_element_type=F32)
            pair_rows.append(o * inv)
        if g == N_KV_HEADS - 1:
            out = [jnp.concatenate(pair_rows[n:n + 2], axis=0).T for n in range(0, N_Q_HEADS, 2)]
            o_ref[0, cols, :] = jnp.concatenate(out, axis=1).astype(BF16)
            pair_rows = []


def _attn(sinks, qt, k3, vt):
    B, S, _ = k3.shape
    tq = ATT_TILE
    per = tq // WINDOW
    chan = lambda b, i: (b, 0, i)
    tok = lambda b, i: (b, i, 0)
    return pl.pallas_call(
        _attn_kernel,
        grid=(B, S // tq),
        in_specs=[pl.BlockSpec(memory_space=pltpu.SMEM),
                  pl.BlockSpec((1, ATT_Q_WIDTH, tq), chan),
                  pl.BlockSpec((1, tq, ATT_KV_WIDTH), tok),
                  pl.BlockSpec((1, WINDOW, ATT_KV_WIDTH), lambda b, i: (b, jnp.maximum(i * per - 1, 0), 0)),
                  pl.BlockSpec((1, ATT_KV_WIDTH, tq), chan),
                  pl.BlockSpec((1, ATT_KV_WIDTH, WINDOW), lambda b, i: (b, 0, jnp.maximum(i * per - 1, 0)))],
        out_specs=pl.BlockSpec((1, tq, ATT_Q_WIDTH), tok),
        out_shape=jax.ShapeDtypeStruct((B, S, ATT_Q_WIDTH), BF16),
        compiler_params=pltpu.CompilerParams(dimension_semantics=("arbitrary", "arbitrary"),
                                             vmem_limit_bytes=VMEM_LIMIT),
        name="attn",
    )(sinks, qt, k3, k3, vt, vt)


def _mlstm_kernel(nb, *refs):
    L = M_CHUNK
    H = M_HEADS
    mqt_ref, mk_ref, mvt_ref, mot_ref = refs[:4]
    grow_refs = refs[4:4 + nb]
    (cwq_ref, cbq_ref, cwk_ref, cbk_ref, brow_ref, nw_ref, y_ref,
     state_ref, m_ref, prevq_ref, ubuf_ref) = refs[4 + nb:]
    cidx = pl.program_id(1)

    @pl.when(cidx == 0)
    def _():
        state_ref[...] = jnp.zeros_like(state_ref)
        m_ref[...] = jnp.zeros_like(m_ref)
        prevq_ref[...] = jnp.zeros_like(prevq_ref)
        ubuf_ref[:, 0:8, :] = jnp.zeros((nb, 8, M_QK_WIDTH), F32)

    ri = lax.broadcasted_iota(jnp.int32, (L, L), 0)
    ci = lax.broadcasted_iota(jnp.int32, (L, L), 1)
    causal_t = ri <= ci
    triu = jnp.where(causal_t, 1.0, 0.0).astype(F32)
    lane = lax.broadcasted_iota(jnp.int32, (8, L), 1)
    r2 = lax.broadcasted_iota(jnp.int32, (2 * L, L), 0)
    c2 = lax.broadcasted_iota(jnp.int32, (2 * L, L), 1)
    shifts = [jnp.where(r2 - c2 == L - k, 1.0, 0.0).astype(BF16) for k in range(1, CONV_WIDTH)]
    ones_rows = jnp.where(lax.broadcasted_iota(jnp.int32, (M_STATE_ROWS - M_V_DIM, L), 0) == 0,
                          1.0, 0.0).astype(BF16)

    pairs = [(bb, h) for bb in range(nb) for h in range(H)]
    states = [state_ref[bb * H + h] for bb, h in pairs]
    m_prevs = [m_ref[bb] for bb in range(nb)]

    seqs = []
    for bb in range(nb):
        cur = mqt_ref[bb]
        both = jnp.concatenate([prevq_ref[bb], cur], axis=1)
        acc = cbq_ref[...] + cwq_ref[CONV_WIDTH - 1] * cur.astype(F32)
        for k in range(1, CONV_WIDTH):
            acc = acc + cwq_ref[CONV_WIDTH - 1 - k] * jnp.dot(both, shifts[k - 1],
                                                              preferred_element_type=F32)
        qt = (acc * _sigmoid(acc)).astype(BF16)
        ubuf_ref[bb, 8:L + 8, :] = mk_ref[bb].astype(F32)
        acc = cbk_ref[...] + cwk_ref[CONV_WIDTH - 1:CONV_WIDTH, :] * ubuf_ref[bb, 8:L + 8, :]
        for j in range(CONV_WIDTH - 1):
            off = 8 - (CONV_WIDTH - 1) + j
            acc = acc + cwk_ref[j:j + 1, :] * ubuf_ref[bb, off:off + L, :]
        ubuf_ref[bb, 0:8, :] = ubuf_ref[bb, L:L + 8, :]
        kk =(acc * _sigmoid(acc) * (M_QK_DIM ** -0.5)).astype(BF16)

        gr = grow_refs[bb][...] + brow_ref[...]
        b = jnp.dot(_log_sigmoid(gr), triu, preferred_element_type=F32,
                    precision=lax.Precision.HIGHEST)
        gi = pltpu.roll(gr, 4, axis=0)
        u = gi - b
        cm = u
        for sh in (1, 2, 4, 8, 16, 32, 64):
            if sh < L:
                cm = jnp.maximum(cm, jnp.where(lane >= sh, pltpu.roll(cm, sh, axis=1), NEG))
        m_prev = m_prevs[bb]
        mt = b + jnp.maximum(m_prev, cm)
        wa = jnp.exp(b + m_prev - mt)
        emt = jnp.exp(-mt)
        m_new = jnp.broadcast_to(mt[:, L - 1:L], (8, L))
        b_last = jnp.broadcast_to(b[:, L - 1:L], (8, L))
        wc = jnp.exp(b_last + m_prev - m_new)
        ws = jnp.exp(b_last + u - m_new)
        seqs.append((qt, kk, b - mt, u.T, wa, emt, wc, ws, m_new))

    new_states, outs = [], []
    for (bb, h), state in zip(pairs, states):
        qt, kk, bmt, ucol, wa, emt, wc, ws, _ = seqs[bb]
        row = slice(H + h, H + h + 1)
        qh = qt[h * M_QK_DIM:(h + 1) * M_QK_DIM, :]
        kh = kk[:, h * M_QK_DIM:(h + 1) * M_QK_DIM]
        wq = jnp.exp(jnp.where(causal_t, bmt[row, :] + ucol[:, H + h:H + h + 1], NEG))
        sw = (jnp.dot(kh, qh, preferred_element_type=F32) * wq).astype(BF16)
        vext = jnp.concatenate([mvt_ref[bb, h * M_V_DIM:(h + 1) * M_V_DIM, :], ones_rows], axis=0)
        res = wa[row, :] * jnp.dot(state.astype(BF16), qh, preferred_element_type=F32) \
            + jnp.dot(vext, sw, preferred_element_type=F32)
        num = res[:M_V_DIM, :]
        den = res[M_V_DIM:M_V_DIM + 1, :]
        hb = num / jnp.maximum(jnp.abs(den), emt[row, :])
        vw = (vext.astype(F32) * ws[row, :]).astype(BF16)
        new_states.append(wc[row, 0:1] * state + jnp.dot(vw, kh, preferred_element_type=F32))
        hn = hb * lax.rsqrt(jnp.mean(hb * hb, axis=0, keepdims=True) + EPS)
        hn = hn * nw_ref[h * M_V_DIM:(h + 1) * M_V_DIM, :]
        og = _sigmoid(mot_ref[bb, h * M_V_DIM:(h + 1) * M_V_DIM, :].astype(F32))
        outs.append((og * hn).T.astype(BF16))

    for i, (bb, h) in enumerate(pairs):
        state_ref[bb * H + h] = new_states[i]
    for bb in range(nb):
        m_ref[bb] = seqs[bb][8]
        prevq_ref[bb] = mqt_ref[bb]
        y_ref[bb] = jnp.concatenate(outs[bb * H:(bb + 1) * H], axis=1)


def _mlstm(mqt, mk3, mvt, mot, grow, conv_w, conv_b, bias_row, norm_w):
    B, S, _ = mk3.shape
    L = M_CHUNK
    nb = max(d for d in range(1, M_BATCH + 1) if B % d == 0)
    nc = S // L
    assert L == LANES, "per-head scalars are kept lane-replicated next to (8, L) gate rows"
    tok = lambda b, c: (b, c, 0)
    chan = lambda b, c: (b, 0, c)
    const2 = lambda b, c: (0, 0)
    const3 = lambda b, c: (0, 0, 0)
    grow_specs = [pl.BlockSpec((8, L), functools.partial(lambda b, c, n: (0, (b * nb + n) * nc + c), n=n))
                  for n in range(nb)]
    rep = lambda v: jnp.broadcast_to(v.astype(F32)[..., None], v.shape + (L,))
    cwq, cbq = rep(conv_w[:, :M_QK_WIDTH]), rep(conv_b[:M_QK_WIDTH])
    cwk, cbk = conv_w[:, M_QK_WIDTH:].astype(F32), conv_b[M_QK_WIDTH:].reshape(1, -1).astype(F32)
    return pl.pallas_call(
        functools.partial(_mlstm_kernel, nb),
        grid=(B // nb, nc),
        in_specs=[pl.BlockSpec((nb, M_QK_WIDTH, L), chan),
                  pl.BlockSpec((nb, L, M_QK_WIDTH), tok),
                  pl.BlockSpec((nb, M_V_WIDTH, L), chan),
                  pl.BlockSpec((nb, M_V_WIDTH, L), chan),
                  *grow_specs,
                  pl.BlockSpec((CONV_WIDTH, M_QK_WIDTH, L), const3),
                  pl.BlockSpec((M_QK_WIDTH, L), const2),
                  pl.BlockSpec((CONV_WIDTH, M_QK_WIDTH), const2),
                  pl.BlockSpec((1, M_QK_WIDTH), const2),
                  pl.BlockSpec((8, 1), const2),
                  pl.BlockSpec((M_V_WIDTH, L), const2)],
        out_specs=pl.BlockSpec((nb, L, M_V_WIDTH), tok),
        out_shape=jax.ShapeDtypeStruct((B, S, M_V_WIDTH), BF16),
        scratch_shapes=[pltpu.VMEM((nb * M_HEADS, M_STATE_ROWS, M_QK_DIM), F32),
                        pltpu.VMEM((nb, 8, LANES), F32),
                        pltpu.VMEM((nb, M_QK_WIDTH, L), BF16),
                        pltpu.VMEM((nb, L + 8, M_QK_WIDTH), F32)],
        compiler_params=pltpu.CompilerParams(dimension_semantics=("arbitrary", "arbitrary"),
                                             vmem_limit_bytes=VMEM_LIMIT),
        name="mlstm",
    )(mqt, mk3, mvt, mot, *([grow] * nb), cwq, cbq, cwk, cbk, bias_row, rep(norm_w))


def _outproj_kernel(x_ref, ya_ref, ym_ref, gab_ref, wa_ref, wm_ref, wo_ref, nw_ref, wr_ref, br_ref,
                    x1_ref, h2_ref, meta_ref, code_ref, cnt_ref, cnt_scr):
    tm = TOK_TILE
    i = pl.program_id(0)

    @pl.when(i == 0)
    def _():
        cnt_scr[...] = jnp.zeros_like(cnt_scr)

    pa = jnp.dot(ya_ref[...], wa_ref[...], preferred_element_type=F32)
    pm = jnp.dot(ym_ref[...], wm_ref[...], preferred_element_type=F32)
    ga = gab_ref[:, :D_MODEL].astype(F32)
    gb = gab_ref[:, D_MODEL:].astype(F32)
    mix = _sigmoid(ga) * pa + _sigmoid(gb) * pm
    x1 = x_ref[...] + jnp.dot(mix.astype(BF16), wo_ref[...], preferred_element_type=F32)
    x1_ref[...] = x1
    h2 = _rms(x1, nw_ref[...])
    h2_ref[...] = _pack_rows(h2)

    nt = (((1,), (1,)), ((), ()))
    h_hi = h2.astype(BF16)
    h_lo = (h2 - h_hi.astype(F32)).astype(BF16)
    hw = lax.dot_general(wr_ref[...], h_hi, nt, preferred_element_type=F32)
    logits = (hw[:LANES] + hw[LANES:]
              + lax.dot_general(wr_ref[:LANES, :], h_lo, nt, preferred_element_type=F32))[:ROUTER_ROWS]
    logits = logits + br_ref[...]
    row8 = lax.broadcasted_iota(jnp.int32, (8, tm), 0)

    def first_argmax(v):
        mx = jnp.max(v, axis=0, keepdims=True)
        idx = jnp.min(jnp.where(v == mx, row8, 8), axis=0, keepdims=True)
        return mx, idx

    gl = jnp.where(row8 < N_GROUPS, logits[0:8], NEG)
    gmax, gi = first_argmax(gl)
    gp = 1.0 / jnp.sum(jnp.exp(gl - gmax), axis=0, keepdims=True)
    el = jnp.zeros((8, tm), F32)
    for g in range(N_GROUPS):
        el = jnp.where(gi == g, logits[8 + 8 * g:16 + 8 * g], el)
    v1, j1 = first_argmax(el)
    v2, j2 = first_argmax(jnp.where(row8 == j1, NEG, el))
    t = jnp.exp(v2 - v1)
    w1 = gp / (1.0 + t)
    w2 = gp * t / (1.0 + t)
    e1 = gi * EXPERTS_PER_GROUP + j1
    e2 = gi * EXPERTS_PER_GROUP + j2

    erow = lax.broadcasted_iota(jnp.int32, (N_EXPERTS, tm), 0)
    hit1 = erow == e1
    hit2 = erow == e2
    onehot = jnp.where(hit1 | hit2, 1.0, 0.0)
    ri = lax.broadcasted_iota(jnp.int32, (tm, tm), 0)
    ci = lax.broadcasted_iota(jnp.int32, (tm, tm), 1)
    before = jnp.where(ri < ci, 1.0, 0.0).astype(BF16)
    rank = jnp.dot(onehot.astype(BF16), before, preferred_element_type=F32) + cnt_scr[...]
    r1 = jnp.sum(jnp.where(hit1, rank, 0.0), axis=0, keepdims=True)
    r2 = jnp.sum(jnp.where(hit2, rank, 0.0), axis=0, keepdims=True)
    cnt = jnp.broadcast_to((rank + onehot)[:, tm - 1:tm], (N_EXPERTS, tm))
    cnt_scr[...] = cnt
    cnt_ref[...] = cnt[:, :LANES]

    c1 = e1 * RANK_RADIX + r1.astype(jnp.int32)
    c2 = e2 * RANK_RADIX + r2.astype(jnp.int32)
    code_ref[...] = jnp.where(row8 == 0, c1, jnp.where(row8 == 1, c2, 0))
    wrow = lax.broadcasted_iota(jnp.int32, (LANES, tm), 0)
    meta_ref[...] = jnp.where(wrow == 4, w1, jnp.where(wrow == 5, w2, 0.0)).T


def _outproj(x2, ya, ym, gab, wa, wm, wo, norm_w, w_rt, b_rt):
    T = x2.shape[0]
    tm = TOK_TILE
    row = lambda i: (i, 0)
    const = lambda i: (0, 0)
    return pl.pallas_call(
        _outproj_kernel,
        grid=(T // tm,),
        in_specs=[pl.BlockSpec((tm, D_MODEL), row),
                  pl.BlockSpec((tm, ATT_Q_WIDTH), row),
                  pl.BlockSpec((tm, M_V_WIDTH), row),
                  pl.BlockSpec((tm, 2 * D_MODEL), row),
                  pl.BlockSpec((ATT_Q_WIDTH, D_MODEL), const),
                  pl.BlockSpec((M_V_WIDTH, D_MODEL), const),
                  pl.BlockSpec((D_MODEL, D_MODEL), const),
                  pl.BlockSpec((1, D_MODEL), const),
                  pl.BlockSpec((2 * LANES, D_MODEL), const),
                  pl.BlockSpec((ROUTER_ROWS, tm), const)],
        out_specs=[pl.BlockSpec((tm, D_MODEL), row),
                   pl.BlockSpec((tm, PACKED), row),
                   pl.BlockSpec((tm, LANES), row),
                   pl.BlockSpec((8, tm), lambda i: (0, i)),
                   pl.BlockSpec((N_EXPERTS, LANES), const)],
        out_shape=[jax.ShapeDtypeStruct((T, D_MODEL), F32),
                   jax.ShapeDtypeStruct((T, PACKED), jnp.int32),
                   jax.ShapeDtypeStruct((T, LANES), F32),
                   jax.ShapeDtypeStruct((8, T), jnp.int32),
                   jax.ShapeDtypeStruct((N_EXPERTS, LANES), F32)],
        scratch_shapes=[pltpu.VMEM((N_EXPERTS, tm), F32)],
        compiler_params=pltpu.CompilerParams(dimension_semantics=("arbitrary",),
                                             vmem_limit_bytes=VMEM_LIMIT),
        name="outproj",
    )(x2, ya, ym, gab, wa, wm, wo, norm_w, w_rt, b_rt)


def _slots_kernel(cnt_ref, pstart_ref, blk_e_ref, nvalid_ref, nused_ref):
    R = SLOT_BLOCK
    nblk = blk_e_ref.shape[0]

    def per_expert(e, run):
        c = cnt_ref[e]
        end = run + ((c + R - 1) // R) * R
        pstart_ref[e] = run

        def set_blk(b, carry):
            blk_e_ref[b] = e
            nvalid_ref[b] = jnp.minimum(run + c - b * R, R)
            return carry
        lax.fori_loop(run // R, end // R, set_blk, 0)
        return end

    total = lax.fori_loop(0, N_EXPERTS, per_expert, 0)
    nused_ref[0] = total // R

    def tail_blk(b, carry):
        blk_e_ref[b] = N_EXPERTS - 1
        nvalid_ref[b] = 0
        return carry
    lax.fori_loop(total // R, nblk, tail_blk, 0)


def _slots(counts, nblk):
    smem = pl.BlockSpec(memory_space=pltpu.SMEM)
    return pl.pallas_call(
        _slots_kernel,
        in_specs=[smem],
        out_specs=[smem, smem, smem, smem],
        out_shape=[jax.ShapeDtypeStruct((N_EXPERTS,), jnp.int32),
                   jax.ShapeDtypeStruct((nblk,), jnp.int32),
                   jax.ShapeDtypeStruct((nblk,), jnp.int32),
                   jax.ShapeDtypeStruct((1,), jnp.int32)],
        name="slots",
    )(counts)


def _experts_kernel(blk_e_ref, nvalid_ref, nused_ref, xs_ref, wg_ref, wu_ref, wd_ref, ys_ref):
    j = pl.program_id(0)
    R = SLOT_BLOCK
    valid = j < nused_ref[0]

    @pl.when(valid)
    def _():
        rows = lax.broadcasted_iota(jnp.int32, (R, PACKED), 0)
        xw = jnp.where(rows < nvalid_ref[j], xs_ref[...], 0)
        xb = _unpack_rows(xw).astype(BF16)
        g = jnp.dot(xb, wg_ref[0], preferred_element_type=F32)
        u = jnp.dot(xb, wu_ref[0], preferred_element_type=F32)
        act = (g * _sigmoid(g) * u).astype(BF16)
        ys_ref[...] = _pack_rows(jnp.dot(act, wd_ref[0], preferred_element_type=F32))

    @pl.when(jnp.logical_not(valid))
    def _():
        ys_ref[...] = jnp.zeros_like(ys_ref)


def _experts(blk_e, nvalid, nused, xs, w_gate, w_up, w_down):
    nblk = blk_e.shape[0]
    R = SLOT_BLOCK
    wspec = lambda shape: pl.BlockSpec((1,) + shape, lambda j, be, nv, nu: (be[j], 0, 0))
    grid_spec = pltpu.PrefetchScalarGridSpec(
        num_scalar_prefetch=3,
        grid=(nblk,),
        in_specs=[pl.BlockSpec((R, PACKED), lambda j, be, nv, nu: (jnp.minimum(j, nu[0] - 1), 0)),
                  wspec((D_MODEL, D_EXPERT)),
                  wspec((D_MODEL, D_EXPERT)),
                  wspec((D_EXPERT, D_MODEL))],
        out_specs=pl.BlockSpec((R, PACKED), lambda j, be, nv, nu: (j, 0)),
    )
    return pl.pallas_call(
        _experts_kernel,
        grid_spec=grid_spec,
        out_shape=jax.ShapeDtypeStruct((nblk * R, PACKED), jnp.int32),
        compiler_params=pltpu.CompilerParams(dimension_semantics=("arbitrary",),
                                             vmem_limit_bytes=VMEM_LIMIT),
        name="experts",
    )(blk_e, nvalid, nused, xs, w_gate, w_up, w_down)


def _dest_kernel(pstart_ref, code_ref, dest_ref):
    c = code_ref[...]
    e = c >> RANK_BITS
    base = jnp.zeros_like(c)
    for k in range(N_EXPERTS):
        base = jnp.where(e == k, pstart_ref[k], base)
    dest_ref[...] = base + (c & (RANK_RADIX - 1))


def _dests(pstart, code_rows):
    T = code_rows.shape[1]
    tc = min(4096, T)
    return pl.pallas_call(
        _dest_kernel,
        grid=(T // tc,),
        in_specs=[pl.BlockSpec(memory_space=pltpu.SMEM),
                  pl.BlockSpec((8, tc), lambda i: (0, i))],
        out_specs=pl.BlockSpec((8, tc), lambda i: (0, i)),
        out_shape=jax.ShapeDtypeStruct((8, T), jnp.int32),
        name="dests",
    )(pstart, code_rows)


def _sc_move_rows(src, idx, n_out, scatter):
    M = idx.shape[0]
    n_src, width = src.shape
    info = plsc.get_sparse_core_info()
    nc, nw = info.num_cores, info.num_cores * info.num_subcores
    per_w = M // nw
    ch = SC_CHUNK
    nch = per_w // ch
    assert per_w * nw == M and nch * ch == per_w and nch % 2 == 0 and n_src % per_w == 0
    mesh = plsc.VectorSubcoreMesh(core_axis_name="c", subcore_axis_name="s")

    @functools.partial(
        pl.kernel, mesh=mesh, out_type=jax.ShapeDtypeStruct((n_out, width), src.dtype),
        scratch_types=[pltpu.VMEM((per_w,), jnp.int32),
                       pltpu.VMEM((ch, width), src.dtype), pltpu.VMEM((ch, width), src.dtype),
                       pltpu.SemaphoreType.DMA, pltpu.SemaphoreType.DMA,
                       pltpu.SemaphoreType.DMA, pltpu.SemaphoreType.DMA],
        name="sc_scatter" if scatter else "sc_gather")
    def move(src_hbm, idx_hbm, out_hbm, idx_v, buf0, buf1, in0, in1, out0, out1):
        wid = lax.axis_index("s") * nc + lax.axis_index("c")
        base = wid * per_w
        pltpu.sync_copy(idx_hbm.at[pl.ds(base, per_w)], idx_v)
        lin_base = lax.rem(base, n_src) if scatter else base

        def read(c, buf, sem):
            if scatter:
                return pltpu.make_async_copy(src_hbm.at[pl.ds(lin_base + c * ch, ch)], buf, sem)
            return pltpu.make_async_copy(src_hbm.at[idx_v.at[pl.ds(c * ch, ch)]], buf, sem)

        def write(c, buf, sem):
            if scatter:
                return pltpu.make_async_copy(buf, out_hbm.at[idx_v.at[pl.ds(c * ch, ch)]], sem)
            return pltpu.make_async_copy(buf, out_hbm.at[pl.ds(lin_base + c * ch, ch)], sem)

        read(0, buf0, in0).start()

        @pl.loop(0, nch, step=2)
        def _(c):
            @pl.when(c > 0)
            def _():
                write(c - 1, buf1, out1).wait()
            read(c + 1, buf1, in1).start()
            read(c, buf0, in0).wait()
            write(c, buf0, out0).start()
            read(c + 1, buf1, in1).wait()
            write(c + 1, buf1, out1).start()
            write(c, buf0, out0).wait()

            @pl.when(c + 2 < nch)
            def _():
                read(c + 2, buf0, in0).start()

        write(nch - 1, buf1, out1).wait()

    return move(src, idx)


def _final_kernel(x1_ref, y1_ref, y2_ref, meta_ref, nw_ref, o_ref):
    w1 = meta_ref[:, 4:5]
    w2 = meta_ref[:, 5:6]
    x2 = x1_ref[...] + (w1 * _unpack_rows(y1_ref[...]) + w2 * _unpack_rows(y2_ref[...]))
    o_ref[...] = _rms(x2, nw_ref[...])


def _final(x1, yg, meta, norm_w):
    T = x1.shape[0]
    tm = COMB_TILE
    nt = T // tm
    row = lambda i: (i, 0)
    return pl.pallas_call(
        _final_kernel,
        grid=(nt,),
        in_specs=[pl.BlockSpec((tm, D_MODEL), row),
                  pl.BlockSpec((tm, PACKED), row),
                  pl.BlockSpec((tm, PACKED), lambda i: (i + nt, 0)),
                  pl.BlockSpec((tm, LANES), row),
                  pl.BlockSpec((1, D_MODEL), lambda i: (0, 0))],
        out_specs=pl.BlockSpec((tm, D_MODEL), row),
        out_shape=jax.ShapeDtypeStruct((T, D_MODEL), F32),
        compiler_params=pltpu.CompilerParams(dimension_semantics=("arbitrary",),
                                             vmem_limit_bytes=VMEM_LIMIT),
        name="final",
    )(x1, yg, yg, meta, norm_w)


def _layer(x, norm_mix_w, w_in, conv_w, conv_b, b_igate, b_fgate, attn_sinks, mlstm_norm_w,
           w_attn_o, w_mlstm_o, w_out, norm_ffn_w, w_group, b_group, w_router, b_router,
           w_gate, w_up, w_down, out_norm_w):
    B, S, D = x.shape
    T = B * S
    x2 = x.reshape(T, D)

    w_tok = jnp.concatenate([w_in[:, _O_AK:_O_AV], w_in[:, _O_MK:_O_MV], w_in[:, _O_GA:]], axis=1).astype(BF16)
    w_chan = jnp.concatenate([w_in[:, _O_AQ:_O_AK] * (HEAD_DIM ** -0.5), w_in[:, _O_AV:_O_MQ],
                              w_in[:, _O_MQ:_O_MK], w_in[:, _O_MV:_O_MI]], axis=1).T.astype(BF16)
    w_gates = jnp.pad(w_in[:, _O_MI:_O_GA], ((0, 0), (0, LANES - 2 * M_HEADS))).astype(BF16)
    expert_ws = [w.reshape(-1, w.shape[-1]) for w in (w_gate, w_up, w_down)]
    ak, mk, gab, aqt, avt, mqt, mvt, mot, grow, wg16, wu16, wd16 = _inproj(
        x2, norm_mix_w.reshape(1, D), w_tok, w_chan, w_gates, expert_ws, B, S)

    ya = _attn(attn_sinks.astype(F32), aqt, ak.reshape(B, S, -1), avt)

    bias = jnp.concatenate([b_igate, b_fgate]).astype(F32)
    ym = _mlstm(mqt, mk.reshape(B, S, -1), mvt, mot, grow, conv_w, conv_b,
                bias.reshape(2 * M_HEADS, 1), mlstm_norm_w)

    gpad = jnp.zeros((8 - N_GROUPS, D), F32)
    w_rt = jnp.concatenate([w_group.T, gpad, w_router.T, jnp.zeros((LANES - ROUTER_ROWS, D), F32)], axis=0)
    w_rt_hi = w_rt.astype(BF16)
    w_rt = jnp.concatenate([w_rt_hi, (w_rt - w_rt_hi.astype(F32)).astype(BF16)], axis=0)
    b_rt = jnp.concatenate([b_group, jnp.zeros((8 - N_GROUPS,), F32), b_router]).astype(F32)
    b_rt = jnp.broadcast_to(b_rt[:, None], (ROUTER_ROWS, TOK_TILE))
    x1, h2p, meta, code_rows, cnt = _outproj(x2, ya.reshape(T, -1), ym.reshape(T, -1), gab,
                                            w_attn_o.astype(BF16), w_mlstm_o.astype(BF16),
                                            w_out.astype(BF16), norm_ffn_w.reshape(1, D), w_rt, b_rt)

    counts = cnt[:, 0].astype(jnp.int32)
    nblk = (2 * T) // SLOT_BLOCK + N_EXPERTS
    pstart, blk_e, nvalid, nused = _slots(counts, nblk)
    dest = _dests(pstart, code_rows)[0:2].reshape(2 * T)

    xs = _sc_move_rows(h2p, dest, nblk * SLOT_BLOCK, scatter=True)
    ys = _experts(blk_e, nvalid, nused, xs, wg16.reshape(w_gate.shape), wu16.reshape(w_up.shape),
                  wd16.reshape(w_down.shape))
    yg = _sc_move_rows(ys, dest, 2 * T, scatter=False)
    out = _final(x1, yg, meta, out_norm_w.reshape(1, D))
    return out.reshape(B, S, D)


def kernel(x, norm_mix_w, w_in, conv_w, conv_b, b_igate, b_fgate, attn_sinks, mlstm_norm_w, w_attn_o,
           w_mlstm_o, w_out, norm_ffn_w, w_group, b_group, w_router, b_router, w_gate, w_up, w_down,
           norm_final_w):
    depth = w_in.shape[0]
    assert depth == 1, "final RMSNorm is fused into the last layer's combine kernel"
    return _layer(x, norm_mix_w[0], w_in[0], conv_w[0], conv_b[0], b_igate[0], b_fgate[0],
                  attn_sinks[0], mlstm_norm_w[0], w_attn_o[0], w_mlstm_o[0], w_out[0], norm_ffn_w[0],
                  w_group[0], b_group[0], w_router[0], b_router[0], w_gate[0], w_up[0], w_down[0],
                  norm_final_w)
```

```python
import functools

import jax
import jax.numpy as jnp
from jax import lax
from jax.experimental import pallas as pl
from jax.experimental.pallas import tpu as pltpu
from jax.experimental.pallas import tpu_sc as plsc

F32 = jnp.float32
BF16 = jnp.bfloat16

D_MODEL = 1024
N_Q_HEADS = 8
N_KV_HEADS = 2
HEAD_DIM = 64
WINDOW = 128
GQA_GROUP = N_Q_HEADS // N_KV_HEADS
M_HEADS = 4
M_QK_DIM = 64
M_V_DIM = 128
CONV_WIDTH = 4
N_GROUPS = 4
EXPERTS_PER_GROUP = 8
N_EXPERTS = N_GROUPS * EXPERTS_PER_GROUP
D_EXPERT = 512
EPS = 1e-6

ATT_Q_WIDTH = N_Q_HEADS * HEAD_DIM
ATT_KV_WIDTH = N_KV_HEADS * HEAD_DIM
M_QK_WIDTH = M_HEADS * M_QK_DIM
M_V_WIDTH = M_HEADS * M_V_DIM

LANES = 128
NEG = -1e30
VMEM_LIMIT = 56 * 1024 * 1024

TOK_TILE = 512
ATT_TILE = 512
M_CHUNK = 128
M_BATCH = 8
M_STATE_ROWS = M_V_DIM + 16
SLOT_BLOCK = 512
COMB_TILE = 512
RANK_RADIX = 65536
RANK_BITS = 16
ROUTER_ROWS = 8 + N_EXPERTS
PACKED = D_MODEL // 2
MOE_PARTS = 2
SC_CHUNK = 64
DMA_THREADS = 2

_O_AQ = 0
_O_AK = _O_AQ + ATT_Q_WIDTH
_O_AV = _O_AK + ATT_KV_WIDTH
_O_MQ = _O_AV + ATT_KV_WIDTH
_O_MK = _O_MQ + M_QK_WIDTH
_O_MV = _O_MK + M_QK_WIDTH
_O_MO = _O_MV + M_V_WIDTH
_O_MI = _O_MO + M_V_WIDTH
_O_MF = _O_MI + M_HEADS
_O_GA = _O_MF + M_HEADS
_O_GB = _O_GA + D_MODEL
_O_END = _O_GB + D_MODEL


def _rms(x, w):
    return x * lax.rsqrt(jnp.mean(x * x, axis=-1, keepdims=True) + EPS) * w


def _sigmoid(x):
    return 0.5 * jnp.tanh(0.5 * x) + 0.5


def _pack_rows(x):
    half = x.shape[1] // 2
    bits = lax.bitcast_convert_type(x.astype(BF16).astype(F32), jnp.uint32)
    packed = (bits[:, :half] >> 16) | (bits[:, half:] & jnp.uint32(0xFFFF0000))
    return lax.bitcast_convert_type(packed, jnp.int32)


def _unpack_rows(w):
    u = lax.bitcast_convert_type(w, jnp.uint32)
    lo = lax.bitcast_convert_type(u << 16, F32)
    hi = lax.bitcast_convert_type(u & jnp.uint32(0xFFFF0000), F32)
    return jnp.concatenate([lo, hi], axis=-1)


def _log_sigmoid(x):
    return jnp.minimum(x, 0.0) - jnp.log1p(jnp.exp(-jnp.abs(x)))


_TOK_SPLITS = (ATT_KV_WIDTH, M_QK_WIDTH, 2 * D_MODEL)
_CHAN_SPLITS = (ATT_Q_WIDTH, ATT_KV_WIDTH, M_QK_WIDTH, M_V_WIDTH, M_V_WIDTH)


def _inproj_kernel(x_ref, nw_ref, w_ref, wt_ref, wg_ref, eg_ref, eu_ref, ed_ref, *out_refs):
    tok_refs = out_refs[:len(_TOK_SPLITS)]
    chan_refs = out_refs[len(_TOK_SPLITS):len(_TOK_SPLITS) + len(_CHAN_SPLITS)]
    grow_ref = out_refs[len(_TOK_SPLITS) + len(_CHAN_SPLITS)]
    for src, dst in zip((eg_ref, eu_ref, ed_ref), out_refs[-3:]):
        dst[...] = src[...].astype(BF16)
    h = _rms(x_ref[...], nw_ref[...]).astype(BF16)
    lo = 0
    for ref, width in zip(tok_refs, _TOK_SPLITS):
        ref[...] = jnp.dot(h, w_ref[:, lo:lo + width], preferred_element_type=F32).astype(BF16)
        lo += width
    lo = 0
    for ref, width in zip(chan_refs, _CHAN_SPLITS):
        ref[0] = lax.dot_general(wt_ref[lo:lo + width, :], h, (((1,), (1,)), ((), ())),
                                 preferred_element_type=F32).astype(BF16)
        lo += width
    g = jnp.dot(h, wg_ref[...], preferred_element_type=F32)
    grow_ref[...] = g.T[0:8, :]


def _inproj(x2, norm_w, w_tok, w_chan, w_gates, expert_ws, B, S):
    T = x2.shape[0]
    tm = TOK_TILE
    tpb = S // tm
    nsteps = T // tm
    row = lambda i: (i, 0)
    const = lambda i: (0, 0)
    chan = lambda i: (i // tpb, 0, i % tpb)
    cast_specs = [pl.BlockSpec((w.shape[0] // nsteps, w.shape[1]), row) for w in expert_ws]
    assert all(w.shape[0] % (8 * nsteps) == 0 for w in expert_ws)
    return pl.pallas_call(
        _inproj_kernel,
        grid=(nsteps,),
        in_specs=[pl.BlockSpec((tm, D_MODEL), row),
                  pl.BlockSpec((1, D_MODEL), const),
                  pl.BlockSpec(w_tok.shape, const),
                  pl.BlockSpec(w_chan.shape, const),
                  pl.BlockSpec((D_MODEL, LANES), const),
                  *cast_specs],
        out_specs=[*[pl.BlockSpec((tm, w), row) for w in _TOK_SPLITS],
                   *[pl.BlockSpec((1, w, tm), chan) for w in _CHAN_SPLITS],
                   pl.BlockSpec((8, tm), lambda i: (0, i)),
                   *cast_specs],
        out_shape=[*[jax.ShapeDtypeStruct((T, w), BF16) for w in _TOK_SPLITS],
                   *[jax.ShapeDtypeStruct((B, w, S), BF16) for w in _CHAN_SPLITS],
                   jax.ShapeDtypeStruct((8, T), F32),
                   *[jax.ShapeDtypeStruct(w.shape, BF16) for w in expert_ws]],
        compiler_params=pltpu.CompilerParams(dimension_semantics=("arbitrary",),
                                             vmem_limit_bytes=VMEM_LIMIT),
        name="inproj",
    )(x2, norm_w, w_tok, w_chan, w_gates, *expert_ws)


def _attn_kernel(sink_ref, qt_ref, k_ref, kp_ref, vt_ref, vtp_ref, o_ref):
    i = pl.program_id(1)
    W = WINDOW
    G = GQA_GROUP
    u = lax.broadcasted_iota(jnp.int32, (W, W), 0)
    t = lax.broadcasted_iota(jnp.int32, (W, W), 1)
    from_prev = u > t

    def keys_values(j):
        if j == 0:
            return kp_ref[0], vtp_ref[0], k_ref[0, 0:W, :], vt_ref[0, :, 0:W]
        return (k_ref[0, (j - 1) * W:j * W, :], vt_ref[0, :, (j - 1) * W:j * W],
                k_ref[0, j * W:(j + 1) * W, :], vt_ref[0, :, j * W:(j + 1) * W])

    def scores(j, g):
        k_prev, _, k_cur, _ = keys_values(j)
        dims = slice(g * HEAD_DIM, (g + 1) * HEAD_DIM)
        qg = jnp.concatenate([qt_ref[0, h * HEAD_DIM:(h + 1) * HEAD_DIM, j * W:(j + 1) * W]
                              for h in range(g * G, (g + 1) * G)], axis=1)
        return (jnp.dot(k_prev[:, dims], qg, preferred_element_type=F32),
                jnp.dot(k_cur[:, dims], qg, preferred_element_type=F32))

    tasks = [(j, g) for j in range(ATT_TILE // W) for g in range(N_KV_HEADS)]
    ahead = scores(*tasks[0])
    pair_rows = []
    for n_task, (j, g) in enumerate(tasks):
        s_prev, s_cur = ahead
        if n_task + 1 < len(tasks):
            ahead = scores(*tasks[n_task + 1])
        cols = slice(j * W, (j + 1) * W)
        _, vt_prev, _, vt_cur = keys_values(j)
        dims = slice(g * HEAD_DIM, (g + 1) * HEAD_DIM)
        for n in range(G):
            hc = slice(n * W, (n + 1) * W)
            sp = s_prev[:, hc]
            if j == 0:
                sp = jnp.where(i > 0, sp, NEG)
            s = jnp.where(from_prev, sp, s_cur[:, hc])
            sink = sink_ref[g * G + n]
            m = jnp.maximum(jnp.max(s, axis=0, keepdims=True), sink)
            p = jnp.exp(s - m)
            inv = 1.0 / (jnp.sum(p, axis=0, keepdims=True) + jnp.exp(sink - m))
            p_prev = jnp.where(from_prev, p, 0.0).astype(BF16)
            p_cur = jnp.where(from_prev, 0.0, p).astype(BF16)
            o = jnp.dot(vt_prev[dims, :], p_prev, preferred_element_type=F32) \
                + jnp.dot(vt_cur[dims, :], p_cur, preferred_element_type=F32)
            pair_rows.append(o * inv)
        if g == N_KV_HEADS - 1:
            out = [jnp.concatenate(pair_rows[n:n + 2], axis=0).T for n in range(0, N_Q_HEADS, 2)]
            o_ref[0, cols, :] = jnp.concatenate(out, axis=1).astype(BF16)
            pair_rows = []


def _attn(sinks, qt, k3, vt):
    B, S, _ = k3.shape
    tq = ATT_TILE
    per = tq // WINDOW
    chan = lambda b, i: (b, 0, i)
    tok = lambda b, i: (b, i, 0)
    return pl.pallas_call(
        _attn_kernel,
        grid=(B, S // tq),
        in_specs=[pl.BlockSpec(memory_space=pltpu.SMEM),
                  pl.BlockSpec((1, ATT_Q_WIDTH, tq), chan),
                  pl.BlockSpec((1, tq, ATT_KV_WIDTH), tok),
                  pl.BlockSpec((1, WINDOW, ATT_KV_WIDTH), lambda b, i: (b, jnp.maximum(i * per - 1, 0), 0)),
                  pl.BlockSpec((1, ATT_KV_WIDTH, tq), chan),
                  pl.BlockSpec((1, ATT_KV_WIDTH, WINDOW), lambda b, i: (b, 0, jnp.maximum(i * per - 1, 0)))],
        out_specs=pl.BlockSpec((1, tq, ATT_Q_WIDTH), tok),
        out_shape=jax.ShapeDtypeStruct((B, S, ATT_Q_WIDTH), BF16),
        compiler_params=pltpu.CompilerParams(dimension_semantics=("arbitrary", "arbitrary"),
                                             vmem_limit_bytes=VMEM_LIMIT),
        name="attn",
    )(sinks, qt, k3, k3, vt, vt)


def _mlstm_kernel(nb, *refs):
    L = M_CHUNK
    H = M_HEADS
    mqt_ref, mk_ref, mvt_ref, mot_ref = refs[:4]
    grow_refs = refs[4:4 + nb]
    (cwq_ref, cbq_ref, cwk_ref, cbk_ref, brow_ref, nw_ref, y_ref,
     state_ref, m_ref, prevq_ref, ubuf_ref) = refs[4 + nb:]
    cidx = pl.program_id(1)

    @pl.when(cidx == 0)
    def _():
        state_ref[...] = jnp.zeros_like(state_ref)
        m_ref[...] = jnp.zeros_like(m_ref)
        prevq_ref[...] = jnp.zeros_like(prevq_ref)
        ubuf_ref[:, 0:8, :] = jnp.zeros((nb, 8, M_QK_WIDTH), F32)

    ri = lax.broadcasted_iota(jnp.int32, (L, L), 0)
    ci = lax.broadcasted_iota(jnp.int32, (L, L), 1)
    causal_t = ri <= ci
    triu = jnp.where(causal_t, 1.0, 0.0).astype(F32)
    lane = lax.broadcasted_iota(jnp.int32, (8, L), 1)
    r2 = lax.broadcasted_iota(jnp.int32, (2 * L, L), 0)
    c2 = lax.broadcasted_iota(jnp.int32, (2 * L, L), 1)
    shifts = [jnp.where(r2 - c2 == L - k, 1.0, 0.0).astype(BF16) for k in range(1, CONV_WIDTH)]
    ones_rows = jnp.where(lax.broadcasted_iota(jnp.int32, (M_STATE_ROWS - M_V_DIM, L), 0) == 0,
                          1.0, 0.0).astype(BF16)

    pairs = [(bb, h) for bb in range(nb) for h in range(H)]
    states = [state_ref[bb * H + h] for bb, h in pairs]
    m_prevs = [m_ref[bb] for bb in range(nb)]

    seqs = []
    for bb in range(nb):
        cur = mqt_ref[bb]
        both = jnp.concatenate([prevq_ref[bb], cur], axis=1)
        acc = cbq_ref[...] + cwq_ref[CONV_WIDTH - 1] * cur.astype(F32)
        for k in range(1, CONV_WIDTH):
            acc = acc + cwq_ref[CONV_WIDTH - 1 - k] * jnp.dot(both, shifts[k - 1],
                                                              preferred_element_type=F32)
        qt = (acc * _sigmoid(acc)).astype(BF16)
        ubuf_ref[bb, 8:L + 8, :] = mk_ref[bb].astype(F32)
        acc = cbk_ref[...] + cwk_ref[CONV_WIDTH - 1:CONV_WIDTH, :] * ubuf_ref[bb, 8:L + 8, :]
        for j in range(CONV_WIDTH - 1):
            off = 8 - (CONV_WIDTH - 1) + j
            acc = acc + cwk_ref[j:j + 1, :] * ubuf_ref[bb, off:off + L, :]
        ubuf_ref[bb, 0:8, :] = ubuf_ref[bb, L:L + 8, :]
        kk =(acc * _sigmoid(acc) * (M_QK_DIM ** -0.5)).astype(BF16)

        gr = grow_refs[bb][...] + brow_ref[...]
        b = jnp.dot(_log_sigmoid(gr), triu, preferred_element_type=F32,
                    precision=lax.Precision.HIGHEST)
        gi = pltpu.roll(gr, 4, axis=0)
        u = gi - b
        cm = u
        for sh in (1, 2, 4, 8, 16, 32, 64):
            if sh < L:
                cm = jnp.maximum(cm, jnp.where(lane >= sh, pltpu.roll(cm, sh, axis=1), NEG))
        m_prev = m_prevs[bb]
        mt = b + jnp.maximum(m_prev, cm)
        wa = jnp.exp(b + m_prev - mt)
        emt = jnp.exp(-mt)
        m_new = jnp.broadcast_to(mt[:, L - 1:L], (8, L))
        b_last = jnp.broadcast_to(b[:, L - 1:L], (8, L))
        wc = jnp.exp(b_last + m_prev - m_new)
        ws = jnp.exp(b_last + u - m_new)
        seqs.append((qt, kk, b - mt, u.T, wa, emt, wc, ws, m_new))

    new_states, outs = [], []
    for (bb, h), state in zip(pairs, states):
        qt, kk, bmt, ucol, wa, emt, wc, ws, _ = seqs[bb]
        row = slice(H + h, H + h + 1)
        qh = qt[h * M_QK_DIM:(h + 1) * M_QK_DIM, :]
        kh = kk[:, h * M_QK_DIM:(h + 1) * M_QK_DIM]
        wq = jnp.exp(jnp.where(causal_t, bmt[row, :] + ucol[:, H + h:H + h + 1], NEG))
        sw = (jnp.dot(kh, qh, preferred_element_type=F32) * wq).astype(BF16)
        vext = jnp.concatenate([mvt_ref[bb, h * M_V_DIM:(h + 1) * M_V_DIM, :], ones_rows], axis=0)
        res = wa[row, :] * jnp.dot(state.astype(BF16), qh, preferred_element_type=F32) \
            + jnp.dot(vext, sw, preferred_element_type=F32)
        num = res[:M_V_DIM, :]
        den = res[M_V_DIM:M_V_DIM + 1, :]
        hb = num / jnp.maximum(jnp.abs(den), emt[row, :])
        vw = (vext.astype(F32) * ws[row, :]).astype(BF16)
        new_states.append(wc[row, 0:1] * state + jnp.dot(vw, kh, preferred_element_type=F32))
        hn = hb * lax.rsqrt(jnp.mean(hb * hb, axis=0, keepdims=True) + EPS)
        hn = hn * nw_ref[h * M_V_DIM:(h + 1) * M_V_DIM, :]
        og = _sigmoid(mot_ref[bb, h * M_V_DIM:(h + 1) * M_V_DIM, :].astype(F32))
        outs.append((og * hn).T.astype(BF16))

    for i, (bb, h) in enumerate(pairs):
        state_ref[bb * H + h] = new_states[i]
    for bb in range(nb):
        m_ref[bb] = seqs[bb][8]
        prevq_ref[bb] = mqt_ref[bb]
        y_ref[bb] = jnp.concatenate(outs[bb * H:(bb + 1) * H], axis=1)


def _mlstm(mqt, mk3, mvt, mot, grow, conv_w, conv_b, bias_row, norm_w):
    B, S, _ = mk3.shape
    L = M_CHUNK
    nb = max(d for d in range(1, M_BATCH + 1) if B % d == 0)
    nc = S // L
    assert L == LANES, "per-head scalars are kept lane-replicated next to (8, L) gate rows"
    tok = lambda b, c: (b, c, 0)
    chan = lambda b, c: (b, 0, c)
    const2 = lambda b, c: (0, 0)
    const3 = lambda b, c: (0, 0, 0)
    grow_specs = [pl.BlockSpec((8, L), functools.partial(lambda b, c, n: (0, (b * nb + n) * nc + c), n=n))
                  for n in range(nb)]
    rep = lambda v: jnp.broadcast_to(v.astype(F32)[..., None], v.shape + (L,))
    cwq, cbq = rep(conv_w[:, :M_QK_WIDTH]), rep(conv_b[:M_QK_WIDTH])
    cwk, cbk = conv_w[:, M_QK_WIDTH:].astype(F32), conv_b[M_QK_WIDTH:].reshape(1, -1).astype(F32)
    return pl.pallas_call(
        functools.partial(_mlstm_kernel, nb),
        grid=(B // nb, nc),
        in_specs=[pl.BlockSpec((nb, M_QK_WIDTH, L), chan),
                  pl.BlockSpec((nb, L, M_QK_WIDTH), tok),
                  pl.BlockSpec((nb, M_V_WIDTH, L), chan),
                  pl.BlockSpec((nb, M_V_WIDTH, L), chan),
                  *grow_specs,
                  pl.BlockSpec((CONV_WIDTH, M_QK_WIDTH, L), const3),
                  pl.BlockSpec((M_QK_WIDTH, L), const2),
                  pl.BlockSpec((CONV_WIDTH, M_QK_WIDTH), const2),
                  pl.BlockSpec((1, M_QK_WIDTH), const2),
                  pl.BlockSpec((8, 1), const2),
                  pl.BlockSpec((M_V_WIDTH, L), const2)],
        out_specs=pl.BlockSpec((nb, L, M_V_WIDTH), tok),
        out_shape=jax.ShapeDtypeStruct((B, S, M_V_WIDTH), BF16),
        scratch_shapes=[pltpu.VMEM((nb * M_HEADS, M_STATE_ROWS, M_QK_DIM), F32),
                        pltpu.VMEM((nb, 8, LANES), F32),
                        pltpu.VMEM((nb, M_QK_WIDTH, L), BF16),
                        pltpu.VMEM((nb, L + 8, M_QK_WIDTH), F32)],
        compiler_params=pltpu.CompilerParams(dimension_semantics=("arbitrary", "arbitrary"),
                                             vmem_limit_bytes=VMEM_LIMIT),
        name="mlstm",
    )(mqt, mk3, mvt, mot, *([grow] * nb), cwq, cbq, cwk, cbk, bias_row, rep(norm_w))


def _outproj_kernel(x_ref, ya_ref, ym_ref, gab_ref, wa_ref, wm_ref, wo_ref, nw_ref, wr_ref, br_ref,
                    x1_ref, h2_ref, meta_ref, code_ref, cnt_ref, cnt_scr):
    tm = TOK_TILE
    i = pl.program_id(0)

    @pl.when(i == 0)
    def _():
        cnt_scr[...] = jnp.zeros_like(cnt_scr)

    pa = jnp.dot(ya_ref[...], wa_ref[...], preferred_element_type=F32)
    pm = jnp.dot(ym_ref[...], wm_ref[...], preferred_element_type=F32)
    ga = gab_ref[:, :D_MODEL].astype(F32)
    gb = gab_ref[:, D_MODEL:].astype(F32)
    mix = _sigmoid(ga) * pa + _sigmoid(gb) * pm
    x1 = x_ref[...] + jnp.dot(mix.astype(BF16), wo_ref[...], preferred_element_type=F32)
    x1_ref[...] = x1
    h2 = _rms(x1, nw_ref[...])
    h2_ref[...] = _pack_rows(h2)

    nt = (((1,), (1,)), ((), ()))
    h_hi = h2.astype(BF16)
    h_lo = (h2 - h_hi.astype(F32)).astype(BF16)
    hw = lax.dot_general(wr_ref[...], h_hi, nt, preferred_element_type=F32)
    logits = (hw[:LANES] + hw[LANES:]
              + lax.dot_general(wr_ref[:LANES, :], h_lo, nt, preferred_element_type=F32))[:ROUTER_ROWS]
    logits = logits + br_ref[...]
    row8 = lax.broadcasted_iota(jnp.int32, (8, tm), 0)

    def first_argmax(v):
        mx = jnp.max(v, axis=0, keepdims=True)
        idx = jnp.min(jnp.where(v == mx, row8, 8), axis=0, keepdims=True)
        return mx, idx

    gl = jnp.where(row8 < N_GROUPS, logits[0:8], NEG)
    gmax, gi = first_argmax(gl)
    gp = 1.0 / jnp.sum(jnp.exp(gl - gmax), axis=0, keepdims=True)
    el = jnp.zeros((8, tm), F32)
    for g in range(N_GROUPS):
        el = jnp.where(gi == g, logits[8 + 8 * g:16 + 8 * g], el)
    v1, j1 = first_argmax(el)
    v2, j2 = first_argmax(jnp.where(row8 == j1, NEG, el))
    t = jnp.exp(v2 - v1)
    w1 = gp / (1.0 + t)
    w2 = gp * t / (1.0 + t)
    e1 = gi * EXPERTS_PER_GROUP + j1
    e2 = gi * EXPERTS_PER_GROUP + j2

    erow = lax.broadcasted_iota(jnp.int32, (N_EXPERTS, tm), 0)
    hit1 = erow == e1
    hit2 = erow == e2
    onehot = jnp.where(hit1 | hit2, 1.0, 0.0)
    ri = lax.broadcasted_iota(jnp.int32, (tm, tm), 0)
    ci = lax.broadcasted_iota(jnp.int32, (tm, tm), 1)
    before = jnp.where(ri < ci, 1.0, 0.0).astype(BF16)
    rank = jnp.dot(onehot.astype(BF16), before, preferred_element_type=F32) + cnt_scr[...]
    r1 = jnp.sum(jnp.where(hit1, rank, 0.0), axis=0, keepdims=True)
    r2 = jnp.sum(jnp.where(hit2, rank, 0.0), axis=0, keepdims=True)
    cnt = jnp.broadcast_to((rank + onehot)[:, tm - 1:tm], (N_EXPERTS, tm))
    cnt_scr[...] = cnt
    cnt_ref[...] = cnt[:, :LANES]

    c1 = e1 * RANK_RADIX + r1.astype(jnp.int32)
    c2 = e2 * RANK_RADIX + r2.astype(jnp.int32)
    code_ref[...] = jnp.where(row8 == 0, c1, jnp.where(row8 == 1, c2, 0))
    wrow = lax.broadcasted_iota(jnp.int32, (LANES, tm), 0)
    meta_ref[...] = jnp.where(wrow == 4, w1, jnp.where(wrow == 5, w2, 0.0)).T


def _outproj(x2, ya, ym, gab, wa, wm, wo, norm_w, w_rt, b_rt, part):
    T = x2.shape[0] // MOE_PARTS
    tm = TOK_TILE
    first = part * (T // tm)
    row = lambda i: (i, 0)
    row_in = lambda i: (i + first, 0)
    const = lambda i: (0, 0)
    return pl.pallas_call(
        _outproj_kernel,
        grid=(T // tm,),
        in_specs=[pl.BlockSpec((tm, D_MODEL), row_in),
                  pl.BlockSpec((tm, ATT_Q_WIDTH), row_in),
                  pl.BlockSpec((tm, M_V_WIDTH), row_in),
                  pl.BlockSpec((tm, 2 * D_MODEL), row_in),
                  pl.BlockSpec((ATT_Q_WIDTH, D_MODEL), const),
                  pl.BlockSpec((M_V_WIDTH, D_MODEL), const),
                  pl.BlockSpec((D_MODEL, D_MODEL), const),
                  pl.BlockSpec((1, D_MODEL), const),
                  pl.BlockSpec((2 * LANES, D_MODEL), const),
                  pl.BlockSpec((ROUTER_ROWS, tm), const)],
        out_specs=[pl.BlockSpec((tm, D_MODEL), row),
                   pl.BlockSpec((tm, PACKED), row),
                   pl.BlockSpec((tm, LANES), row),
                   pl.BlockSpec((8, tm), lambda i: (0, i)),
                   pl.BlockSpec((N_EXPERTS, LANES), const)],
        out_shape=[jax.ShapeDtypeStruct((T, D_MODEL), F32),
                   jax.ShapeDtypeStruct((T, PACKED), jnp.int32),
                   jax.ShapeDtypeStruct((T, LANES), F32),
                   jax.ShapeDtypeStruct((8, T), jnp.int32),
                   jax.ShapeDtypeStruct((N_EXPERTS, LANES), F32)],
        scratch_shapes=[pltpu.VMEM((N_EXPERTS, tm), F32)],
        compiler_params=pltpu.CompilerParams(dimension_semantics=("arbitrary",),
                                             vmem_limit_bytes=VMEM_LIMIT),
        name="outproj",
    )(x2, ya, ym, gab, wa, wm, wo, norm_w, w_rt, b_rt)


def _slots_kernel(cnt_ref, pstart_ref, blk_e_ref, nvalid_ref, nused_ref):
    R = SLOT_BLOCK
    nblk = blk_e_ref.shape[0]

    def per_expert(e, run):
        c = cnt_ref[e]
        end = run + ((c + R - 1) // R) * R
        pstart_ref[e] = run

        def set_blk(b, carry):
            blk_e_ref[b] = e
            nvalid_ref[b] = jnp.minimum(run + c - b * R, R)
            return carry
        lax.fori_loop(run // R, end // R, set_blk, 0)
        return end

    total = lax.fori_loop(0, N_EXPERTS, per_expert, 0)
    nused_ref[0] = total // R

    def tail_blk(b, carry):
        blk_e_ref[b] = N_EXPERTS - 1
        nvalid_ref[b] = 0
        return carry
    lax.fori_loop(total // R, nblk, tail_blk, 0)


def _slots(counts, nblk):
    smem = pl.BlockSpec(memory_space=pltpu.SMEM)
    return pl.pallas_call(
        _slots_kernel,
        in_specs=[smem],
        out_specs=[smem, smem, smem, smem],
        out_shape=[jax.ShapeDtypeStruct((N_EXPERTS,), jnp.int32),
                   jax.ShapeDtypeStruct((nblk,), jnp.int32),
                   jax.ShapeDtypeStruct((nblk,), jnp.int32),
                   jax.ShapeDtypeStruct((1,), jnp.int32)],
        name="slots",
    )(counts)


def _experts_kernel(blk_e_ref, nvalid_ref, nused_ref, xs_ref, wg_ref, wu_ref, wd_ref, ys_ref):
    j = pl.program_id(0)
    R = SLOT_BLOCK
    valid = j < nused_ref[0]

    @pl.when(valid)
    def _():
        rows = lax.broadcasted_iota(jnp.int32, (R, PACKED), 0)
        xw = jnp.where(rows < nvalid_ref[j], xs_ref[...], 0)
        xb = _unpack_rows(xw).astype(BF16)
        g = jnp.dot(xb, wg_ref[0], preferred_element_type=F32)
        u = jnp.dot(xb, wu_ref[0], preferred_element_type=F32)
        act = (g * _sigmoid(g) * u).astype(BF16)
        ys_ref[...] = _pack_rows(jnp.dot(act, wd_ref[0], preferred_element_type=F32))

    @pl.when(jnp.logical_not(valid))
    def _():
        ys_ref[...] = jnp.zeros_like(ys_ref)


def _experts(blk_e, nvalid, nused, xs, w_gate, w_up, w_down):
    nblk = blk_e.shape[0]
    R = SLOT_BLOCK
    wspec = lambda shape: pl.BlockSpec((1,) + shape, lambda j, be, nv, nu: (be[j], 0, 0))
    grid_spec = pltpu.PrefetchScalarGridSpec(
        num_scalar_prefetch=3,
        grid=(nblk,),
        in_specs=[pl.BlockSpec((R, PACKED), lambda j, be, nv, nu: (jnp.minimum(j, nu[0] - 1), 0)),
                  wspec((D_MODEL, D_EXPERT)),
                  wspec((D_MODEL, D_EXPERT)),
                  wspec((D_EXPERT, D_MODEL))],
        out_specs=pl.BlockSpec((R, PACKED), lambda j, be, nv, nu: (j, 0)),
    )
    return pl.pallas_call(
        _experts_kernel,
        grid_spec=grid_spec,
        out_shape=jax.ShapeDtypeStruct((nblk * R, PACKED), jnp.int32),
        compiler_params=pltpu.CompilerParams(dimension_semantics=("arbitrary",),
                                             vmem_limit_bytes=VMEM_LIMIT),
        name="experts",
    )(blk_e, nvalid, nused, xs, w_gate, w_up, w_down)


def _dest_kernel(pstart_ref, code_ref, dest_ref):
    c = code_ref[...]
    e = c >> RANK_BITS
    base = jnp.zeros_like(c)
    for k in range(N_EXPERTS):
        base = jnp.where(e == k, pstart_ref[k], base)
    dest_ref[...] = base + (c & (RANK_RADIX - 1))


def _dests(pstart, code_rows):
    T = code_rows.shape[1]
    tc = min(4096, T)
    return pl.pallas_call(
        _dest_kernel,
        grid=(T // tc,),
        in_specs=[pl.BlockSpec(memory_space=pltpu.SMEM),
                  pl.BlockSpec((8, tc), lambda i: (0, i))],
        out_specs=pl.BlockSpec((8, tc), lambda i: (0, i)),
        out_shape=jax.ShapeDtypeStruct((8, T), jnp.int32),
        name="dests",
    )(pstart, code_rows)


def _sc_move_rows(src, idx, n_out, scatter):
    M = idx.shape[0]
    n_src, width = src.shape
    info = plsc.get_sparse_core_info()
    nc, nw = info.num_cores, info.num_cores * info.num_subcores
    per_w = M // nw
    ch = SC_CHUNK
    nch = per_w // ch
    assert per_w * nw == M and nch * ch == per_w and nch % 2 == 0 and n_src % per_w == 0
    mesh = plsc.VectorSubcoreMesh(core_axis_name="c", subcore_axis_name="s")

    @functools.partial(
        pl.kernel, mesh=mesh, out_type=jax.ShapeDtypeStruct((n_out, width), src.dtype),
        scratch_types=[pltpu.VMEM((per_w,), jnp.int32),
                       pltpu.VMEM((ch, width), src.dtype), pltpu.VMEM((ch, width), src.dtype),
                       pltpu.SemaphoreType.DMA, pltpu.SemaphoreType.DMA,
                       pltpu.SemaphoreType.DMA, pltpu.SemaphoreType.DMA],
        name="sc_scatter" if scatter else "sc_gather")
    def move(src_hbm, idx_hbm, out_hbm, idx_v, buf0, buf1, in0, in1, out0, out1):
        wid = lax.axis_index("s") * nc + lax.axis_index("c")
        base = wid * per_w
        pltpu.sync_copy(idx_hbm.at[pl.ds(base, per_w)], idx_v)
        lin_base = lax.rem(base, n_src) if scatter else base

        def read(c, buf, sem):
            if scatter:
                return pltpu.make_async_copy(src_hbm.at[pl.ds(lin_base + c * ch, ch)], buf, sem)
            return pltpu.make_async_copy(src_hbm.at[idx_v.at[pl.ds(c * ch, ch)]], buf, sem)

        def write(c, buf, sem):
            if scatter:
                return pltpu.make_async_copy(buf, out_hbm.at[idx_v.at[pl.ds(c * ch, ch)]], sem)
            return pltpu.make_async_copy(buf, out_hbm.at[pl.ds(lin_base + c * ch, ch)], sem)

        read(0, buf0, in0).start()

        @pl.loop(0, nch, step=2)
        def _(c):
            @pl.when(c > 0)
            def _():
                write(c - 1, buf1, out1).wait()
            read(c + 1, buf1, in1).start()
            read(c, buf0, in0).wait()
            write(c, buf0, out0).start()
            read(c + 1, buf1, in1).wait()
            write(c + 1, buf1, out1).start()
            write(c, buf0, out0).wait()

            @pl.when(c + 2 < nch)
            def _():
                read(c + 2, buf0, in0).start()

        write(nch - 1, buf1, out1).wait()

    return move(src, idx)


def _final_kernel(x1_ref, y1_ref, y2_ref, meta_ref, nw_ref, o_ref):
    w1 = meta_ref[:, 4:5]
    w2 = meta_ref[:, 5:6]
    x2 = x1_ref[...] + (w1 * _unpack_rows(y1_ref[...]) + w2 * _unpack_rows(y2_ref[...]))
    o_ref[...] = _rms(x2, nw_ref[...])


def _final(x1, yg, meta, norm_w, part, out_prev):
    T = x1.shape[0]
    tm = COMB_TILE
    nt = T // tm
    row = lambda i: (i, 0)
    in_specs = [pl.BlockSpec((tm, D_MODEL), row),
                pl.BlockSpec((tm, PACKED), row),
                pl.BlockSpec((tm, PACKED), lambda i: (i + nt, 0)),
                pl.BlockSpec((tm, LANES), row),
                pl.BlockSpec((1, D_MODEL), lambda i: (0, 0))]
    args = [x1, yg, yg, meta, norm_w]
    if out_prev is None:
        kern, aliases = _final_kernel, {}
    else:
        kern = lambda *refs: _final_kernel(*refs[:5], refs[6])
        in_specs.append(pl.BlockSpec(memory_space=pl.ANY))
        args.append(out_prev)
        aliases = {5: 0}
    return pl.pallas_call(
        kern,
        grid=(nt,),
        in_specs=in_specs,
        out_specs=pl.BlockSpec((tm, D_MODEL), lambda i: (i + part * nt, 0)),
        out_shape=jax.ShapeDtypeStruct((T * MOE_PARTS, D_MODEL), F32),
        input_output_aliases=aliases,
        compiler_params=pltpu.CompilerParams(dimension_semantics=("arbitrary",),
                                             vmem_limit_bytes=VMEM_LIMIT),
        name="final",
    )(*args)


def _layer(x, norm_mix_w, w_in, conv_w, conv_b, b_igate, b_fgate, attn_sinks, mlstm_norm_w,
           w_attn_o, w_mlstm_o, w_out, norm_ffn_w, w_group, b_group, w_router, b_router,
           w_gate, w_up, w_down, out_norm_w):
    B, S, D = x.shape
    T = B * S
    x2 = x.reshape(T, D)

    w_tok = jnp.concatenate([w_in[:, _O_AK:_O_AV], w_in[:, _O_MK:_O_MV], w_in[:, _O_GA:]], axis=1).astype(BF16)
    w_chan = jnp.concatenate([w_in[:, _O_AQ:_O_AK] * (HEAD_DIM ** -0.5), w_in[:, _O_AV:_O_MQ],
                              w_in[:, _O_MQ:_O_MK], w_in[:, _O_MV:_O_MI]], axis=1).T.astype(BF16)
    w_gates = jnp.pad(w_in[:, _O_MI:_O_GA], ((0, 0), (0, LANES - 2 * M_HEADS))).astype(BF16)
    expert_ws = [w.reshape(-1, w.shape[-1]) for w in (w_gate, w_up, w_down)]
    ak, mk, gab, aqt, avt, mqt, mvt, mot, grow, wg16, wu16, wd16 = _inproj(
        x2, norm_mix_w.reshape(1, D), w_tok, w_chan, w_gates, expert_ws, B, S)

    ya = _attn(attn_sinks.astype(F32), aqt, ak.reshape(B, S, -1), avt)

    bias = jnp.concatenate([b_igate, b_fgate]).astype(F32)
    ym = _mlstm(mqt, mk.reshape(B, S, -1), mvt, mot, grow, conv_w, conv_b,
                bias.reshape(2 * M_HEADS, 1), mlstm_norm_w)

    gpad = jnp.zeros((8 - N_GROUPS, D), F32)
    w_rt = jnp.concatenate([w_group.T, gpad, w_router.T, jnp.zeros((LANES - ROUTER_ROWS, D), F32)], axis=0)
    w_rt_hi = w_rt.astype(BF16)
    w_rt = jnp.concatenate([w_rt_hi, (w_rt - w_rt_hi.astype(F32)).astype(BF16)], axis=0)
    b_rt = jnp.concatenate([b_group, jnp.zeros((8 - N_GROUPS,), F32), b_router]).astype(F32)
    b_rt = jnp.broadcast_to(b_rt[:, None], (ROUTER_ROWS, TOK_TILE))
    Tp = T // MOE_PARTS
    nblk = (2 * Tp) // SLOT_BLOCK + N_EXPERTS
    ya2, ym2 = ya.reshape(T, -1), ym.reshape(T, -1)
    wa16, wm16, wo16 = w_attn_o.astype(BF16), w_mlstm_o.astype(BF16), w_out.astype(BF16)
    ew = (wg16.reshape(w_gate.shape), wu16.reshape(w_up.shape), wd16.reshape(w_down.shape))
    out = None
    for part in range(MOE_PARTS):
        x1, h2p, meta, code_rows, cnt = _outproj(x2, ya2, ym2, gab, wa16, wm16, wo16,
                                                norm_ffn_w.reshape(1, D), w_rt, b_rt, part)
        pstart, blk_e, nvalid, nused = _slots(cnt[:, 0].astype(jnp.int32), nblk)
        dest = _dests(pstart, code_rows)[0:2].reshape(2 * Tp)
        xs = _sc_move_rows(h2p, dest, nblk * SLOT_BLOCK, scatter=True)
        ys = _experts(blk_e, nvalid, nused, xs, *ew)
        yg = _sc_move_rows(ys, dest, 2 * Tp, scatter=False)
        out = _final(x1, yg, meta, out_norm_w.reshape(1, D), part, out)
    return out.reshape(B, S, D)


def kernel(x, norm_mix_w, w_in, conv_w, conv_b, b_igate, b_fgate, attn_sinks, mlstm_norm_w, w_attn_o,
           w_mlstm_o, w_out, norm_ffn_w, w_group, b_group, w_router, b_router, w_gate, w_up, w_down,
           norm_final_w):
    depth = w_in.shape[0]
    assert depth == 1, "final RMSNorm is fused into the last layer's combine kernel"
    return _layer(x, norm_mix_w[0], w_in[0], conv_w[0], conv_b[0], b_igate[0], b_fgate[0],
                  attn_sinks[0], mlstm_norm_w[0], w_attn_o[0], w_mlstm_o[0], w_out[0], norm_ffn_w[0],
                  w_group[0], b_group[0], w_router[0], b_router[0], w_gate[0], w_up[0], w_down[0],
                  norm_final_w)
```

```python
import functools

import jax
import jax.numpy as jnp
from jax import lax
from jax.experimental import pallas as pl
from jax.experimental.pallas import tpu as pltpu
from jax.experimental.pallas import tpu_sc as plsc

F32 = jnp.float32
BF16 = jnp.bfloat16

D_MODEL = 1024
N_Q_HEADS = 8
N_KV_HEADS = 2
HEAD_DIM = 64
WINDOW = 128
GQA_GROUP = N_Q_HEADS // N_KV_HEADS
M_HEADS = 4
M_QK_DIM = 64
M_V_DIM = 128
CONV_WIDTH = 4
N_GROUPS = 4
EXPERTS_PER_GROUP = 8
N_EXPERTS = N_GROUPS * EXPERTS_PER_GROUP
D_EXPERT = 512
EPS = 1e-6

ATT_Q_WIDTH = N_Q_HEADS * HEAD_DIM
ATT_KV_WIDTH = N_KV_HEADS * HEAD_DIM
M_QK_WIDTH = M_HEADS * M_QK_DIM
M_V_WIDTH = M_HEADS * M_V_DIM

LANES = 128
NEG = -1e30
VMEM_LIMIT = 56 * 1024 * 1024

TOK_TILE = 512
ATT_TILE = 512
M_CHUNK = 128
M_BATCH = 8
M_STATE_ROWS = M_V_DIM + 16
SLOT_BLOCK = 512
COMB_TILE = 512
RANK_RADIX = 65536
RANK_BITS = 16
ROUTER_ROWS = 8 + N_EXPERTS
PACKED = D_MODEL // 2
SC_CHUNK = 64

_O_AQ = 0
_O_AK = _O_AQ + ATT_Q_WIDTH
_O_AV = _O_AK + ATT_KV_WIDTH
_O_MQ = _O_AV + ATT_KV_WIDTH
_O_MK = _O_MQ + M_QK_WIDTH
_O_MV = _O_MK + M_QK_WIDTH
_O_MO = _O_MV + M_V_WIDTH
_O_MI = _O_MO + M_V_WIDTH
_O_MF = _O_MI + M_HEADS
_O_GA = _O_MF + M_HEADS
_O_GB = _O_GA + D_MODEL
_O_END = _O_GB + D_MODEL


def _rms(x, w):
    return x * lax.rsqrt(jnp.mean(x * x, axis=-1, keepdims=True) + EPS) * w


def _sigmoid(x):
    return 0.5 * jnp.tanh(0.5 * x) + 0.5


def _pack_rows(x):
    half = x.shape[1] // 2
    bits = lax.bitcast_convert_type(x.astype(BF16).astype(F32), jnp.uint32)
    packed = (bits[:, :half] >> 16) | (bits[:, half:] & jnp.uint32(0xFFFF0000))
    return lax.bitcast_convert_type(packed, jnp.int32)


def _unpack_rows(w):
    u = lax.bitcast_convert_type(w, jnp.uint32)
    lo = lax.bitcast_convert_type(u << 16, F32)
    hi = lax.bitcast_convert_type(u & jnp.uint32(0xFFFF0000), F32)
    return jnp.concatenate([lo, hi], axis=-1)


def _log_sigmoid(x):
    return jnp.minimum(x, 0.0) - jnp.log1p(jnp.exp(-jnp.abs(x)))


_TOK_SPLITS = (ATT_KV_WIDTH, M_QK_WIDTH)
_CHAN_SPLITS = (ATT_Q_WIDTH, ATT_KV_WIDTH, M_QK_WIDTH, M_V_WIDTH, M_V_WIDTH)


def _inproj_kernel(x_ref, nw_ref, w_ref, wt_ref, wg_ref, eg_ref, eu_ref, ed_ref, *out_refs):
    tok_refs = out_refs[:len(_TOK_SPLITS)]
    chan_refs = out_refs[len(_TOK_SPLITS):len(_TOK_SPLITS) + len(_CHAN_SPLITS)]
    grow_ref = out_refs[len(_TOK_SPLITS) + len(_CHAN_SPLITS)]
    for src, dst in zip((eg_ref, eu_ref, ed_ref), out_refs[-3:]):
        dst[...] = src[...].astype(BF16)
    h = _rms(x_ref[...], nw_ref[...]).astype(BF16)
    lo = 0
    for ref, width in zip(tok_refs, _TOK_SPLITS):
        ref[...] = jnp.dot(h, w_ref[:, lo:lo + width], preferred_element_type=F32).astype(BF16)
        lo += width
    lo = 0
    for ref, width in zip(chan_refs, _CHAN_SPLITS):
        ref[0] = lax.dot_general(wt_ref[lo:lo + width, :], h, (((1,), (1,)), ((), ())),
                                 preferred_element_type=F32).astype(BF16)
        lo += width
    g = jnp.dot(h, wg_ref[...], preferred_element_type=F32)
    grow_ref[...] = g.T[0:8, :]


def _inproj(x2, norm_w, w_tok, w_chan, w_gates, expert_ws, B, S):
    T = x2.shape[0]
    tm = TOK_TILE
    tpb = S // tm
    nsteps = T // tm
    row = lambda i: (i, 0)
    const = lambda i: (0, 0)
    chan = lambda i: (i // tpb, 0, i % tpb)
    cast_specs = [pl.BlockSpec((w.shape[0] // nsteps, w.shape[1]), row) for w in expert_ws]
    assert all(w.shape[0] % (8 * nsteps) == 0 for w in expert_ws)
    return pl.pallas_call(
        _inproj_kernel,
        grid=(nsteps,),
        in_specs=[pl.BlockSpec((tm, D_MODEL), row),
                  pl.BlockSpec((1, D_MODEL), const),
                  pl.BlockSpec(w_tok.shape, const),
                  pl.BlockSpec(w_chan.shape, const),
                  pl.BlockSpec((D_MODEL, LANES), const),
                  *cast_specs],
        out_specs=[*[pl.BlockSpec((tm, w), row) for w in _TOK_SPLITS],
                   *[pl.BlockSpec((1, w, tm), chan) for w in _CHAN_SPLITS],
                   pl.BlockSpec((8, tm), lambda i: (0, i)),
                   *cast_specs],
        out_shape=[*[jax.ShapeDtypeStruct((T, w), BF16) for w in _TOK_SPLITS],
                   *[jax.ShapeDtypeStruct((B, w, S), BF16) for w in _CHAN_SPLITS],
                   jax.ShapeDtypeStruct((8, T), F32),
                   *[jax.ShapeDtypeStruct(w.shape, BF16) for w in expert_ws]],
        compiler_params=pltpu.CompilerParams(dimension_semantics=("arbitrary",),
                                             vmem_limit_bytes=VMEM_LIMIT),
        name="inproj",
    )(x2, norm_w, w_tok, w_chan, w_gates, *expert_ws)


def _attn_kernel(sink_ref, qt_ref, k_ref, kp_ref, vt_ref, vtp_ref, o_ref):
    i = pl.program_id(1)
    W = WINDOW
    G = GQA_GROUP
    u = lax.broadcasted_iota(jnp.int32, (W, W), 0)
    t = lax.broadcasted_iota(jnp.int32, (W, W), 1)
    from_prev = u > t

    def keys_values(j):
        if j == 0:
            return kp_ref[0], vtp_ref[0], k_ref[0, 0:W, :], vt_ref[0, :, 0:W]
        return (k_ref[0, (j - 1) * W:j * W, :], vt_ref[0, :, (j - 1) * W:j * W],
                k_ref[0, j * W:(j + 1) * W, :], vt_ref[0, :, j * W:(j + 1) * W])

    def scores(j, g):
        k_prev, _, k_cur, _ = keys_values(j)
        dims = slice(g * HEAD_DIM, (g + 1) * HEAD_DIM)
        qg = jnp.concatenate([qt_ref[0, h * HEAD_DIM:(h + 1) * HEAD_DIM, j * W:(j + 1) * W]
                              for h in range(g * G, (g + 1) * G)], axis=1)
        return (jnp.dot(k_prev[:, dims], qg, preferred_element_type=F32),
                jnp.dot(k_cur[:, dims], qg, preferred_element_type=F32))

    tasks = [(j, g) for j in range(ATT_TILE // W) for g in range(N_KV_HEADS)]
    ahead = scores(*tasks[0])
    pair_rows = []
    for n_task, (j, g) in enumerate(tasks):
        s_prev, s_cur = ahead
        if n_task + 1 < len(tasks):
            ahead = scores(*tasks[n_task + 1])
        cols = slice(j * W, (j + 1) * W)
        _, vt_prev, _, vt_cur = keys_values(j)
        dims = slice(g * HEAD_DIM, (g + 1) * HEAD_DIM)
        for n in range(G):
            hc = slice(n * W, (n + 1) * W)
            sp = s_prev[:, hc]
            if j == 0:
                sp = jnp.where(i > 0, sp, NEG)
            s = jnp.where(from_prev, sp, s_cur[:, hc])
            sink = sink_ref[g * G + n]
            m = jnp.maximum(jnp.max(s, axis=0, keepdims=True), sink)
            p = jnp.exp(s - m)
            inv = 1.0 / (jnp.sum(p, axis=0, keepdims=True) + jnp.exp(sink - m))
            p_prev = jnp.where(from_prev, p, 0.0).astype(BF16)
            p_cur = jnp.where(from_prev, 0.0, p).astype(BF16)
            o = jnp.dot(vt_prev[dims, :], p_prev, preferred_element_type=F32) \
                + jnp.dot(vt_cur[dims, :], p_cur, preferred_element_type=F32)
            pair_rows.append(o * inv)
        if g == N_KV_HEADS - 1:
            out = [jnp.concatenate(pair_rows[n:n + 2], axis=0).T for n in range(0, N_Q_HEADS, 2)]
            o_ref[0, cols, :] = jnp.concatenate(out, axis=1).astype(BF16)
            pair_rows = []


def _attn(sinks, qt, k3, vt):
    B, S, _ = k3.shape
    tq = ATT_TILE
    per = tq // WINDOW
    chan = lambda b, i: (b, 0, i)
    tok = lambda b, i: (b, i, 0)
    return pl.pallas_call(
        _attn_kernel,
        grid=(B, S // tq),
        in_specs=[pl.BlockSpec(memory_space=pltpu.SMEM),
                  pl.BlockSpec((1, ATT_Q_WIDTH, tq), chan),
                  pl.BlockSpec((1, tq, ATT_KV_WIDTH), tok),
                  pl.BlockSpec((1, WINDOW, ATT_KV_WIDTH), lambda b, i: (b, jnp.maximum(i * per - 1, 0), 0)),
                  pl.BlockSpec((1, ATT_KV_WIDTH, tq), chan),
                  pl.BlockSpec((1, ATT_KV_WIDTH, WINDOW), lambda b, i: (b, 0, jnp.maximum(i * per - 1, 0)))],
        out_specs=pl.BlockSpec((1, tq, ATT_Q_WIDTH), tok),
        out_shape=jax.ShapeDtypeStruct((B, S, ATT_Q_WIDTH), BF16),
        compiler_params=pltpu.CompilerParams(dimension_semantics=("arbitrary", "arbitrary"),
                                             vmem_limit_bytes=VMEM_LIMIT),
        name="attn",
    )(sinks, qt, k3, k3, vt, vt)


def _mlstm_kernel(nb, *refs):
    L = M_CHUNK
    H = M_HEADS
    mqt_ref, mk_ref, mvt_ref, mot_ref = refs[:4]
    grow_refs = refs[4:4 + nb]
    (cwq_ref, cbq_ref, cwk_ref, cbk_ref, brow_ref, nw_ref, y_ref,
     state_ref, m_ref, prevq_ref, ubuf_ref) = refs[4 + nb:]
    cidx = pl.program_id(1)

    @pl.when(cidx == 0)
    def _():
        state_ref[...] = jnp.zeros_like(state_ref)
        m_ref[...] = jnp.zeros_like(m_ref)
        prevq_ref[...] = jnp.zeros_like(prevq_ref)
        ubuf_ref[:, 0:8, :] = jnp.zeros((nb, 8, M_QK_WIDTH), F32)

    ri = lax.broadcasted_iota(jnp.int32, (L, L), 0)
    ci = lax.broadcasted_iota(jnp.int32, (L, L), 1)
    causal_t = ri <= ci
    triu = jnp.where(causal_t, 1.0, 0.0).astype(F32)
    lane = lax.broadcasted_iota(jnp.int32, (8, L), 1)
    r2 = lax.broadcasted_iota(jnp.int32, (2 * L, L), 0)
    c2 = lax.broadcasted_iota(jnp.int32, (2 * L, L), 1)
    shifts = [jnp.where(r2 - c2 == L - k, 1.0, 0.0).astype(BF16) for k in range(1, CONV_WIDTH)]
    ones_rows = jnp.where(lax.broadcasted_iota(jnp.int32, (M_STATE_ROWS - M_V_DIM, L), 0) == 0,
                          1.0, 0.0).astype(BF16)

    pairs = [(bb, h) for bb in range(nb) for h in range(H)]
    states = [state_ref[bb * H + h] for bb, h in pairs]
    m_prevs = [m_ref[bb] for bb in range(nb)]

    seqs = []
    for bb in range(nb):
        cur = mqt_ref[bb]
        both = jnp.concatenate([prevq_ref[bb], cur], axis=1)
        acc = cbq_ref[...] + cwq_ref[CONV_WIDTH - 1] * cur.astype(F32)
        for k in range(1, CONV_WIDTH):
            acc = acc + cwq_ref[CONV_WIDTH - 1 - k] * jnp.dot(both, shifts[k - 1],
                                                              preferred_element_type=F32)
        qt = (acc * _sigmoid(acc)).astype(BF16)
        ubuf_ref[bb, 8:L + 8, :] = mk_ref[bb].astype(F32)
        acc = cbk_ref[...] + cwk_ref[CONV_WIDTH - 1:CONV_WIDTH, :] * ubuf_ref[bb, 8:L + 8, :]
        for j in range(CONV_WIDTH - 1):
            off = 8 - (CONV_WIDTH - 1) + j
            acc = acc + cwk_ref[j:j + 1, :] * ubuf_ref[bb, off:off + L, :]
        ubuf_ref[bb, 0:8, :] = ubuf_ref[bb, L:L + 8, :]
        kk = (acc * _sigmoid(acc) * (M_QK_DIM ** -0.5)).astype(BF16)

        gr = grow_refs[bb][...] + brow_ref[...]
        b = jnp.dot(_log_sigmoid(gr), triu, preferred_element_type=F32,
                    precision=lax.Precision.HIGHEST)
        gi = pltpu.roll(gr, 4, axis=0)
        u = gi - b
        cm = u
        for sh in (1, 2, 4, 8, 16, 32, 64):
            if sh < L:
                cm = jnp.maximum(cm, jnp.where(lane >= sh, pltpu.roll(cm, sh, axis=1), NEG))
        m_prev = m_prevs[bb]
        mt = b + jnp.maximum(m_prev, cm)
        wa = jnp.exp(b + m_prev - mt)
        emt = jnp.exp(-mt)
        m_new = jnp.broadcast_to(mt[:, L - 1:L], (8, L))
        b_last = jnp.broadcast_to(b[:, L - 1:L], (8, L))
        wc = jnp.exp(b_last + m_prev - m_new)
        ws = jnp.exp(b_last + u - m_new)
        seqs.append((qt, kk, b - mt, u.T, wa, emt, wc, ws, m_new))

    new_states, outs = [], []
    for (bb, h), state in zip(pairs, states):
        qt, kk, bmt, ucol, wa, emt, wc, ws, _ = seqs[bb]
        row = slice(H + h, H + h + 1)
        qh = qt[h * M_QK_DIM:(h + 1) * M_QK_DIM, :]
        kh = kk[:, h * M_QK_DIM:(h + 1) * M_QK_DIM]
        wq = jnp.exp(jnp.where(causal_t, bmt[row, :] + ucol[:, H + h:H + h + 1], NEG))
        sw = (jnp.dot(kh, qh, preferred_element_type=F32) * wq).astype(BF16)
        vext = jnp.concatenate([mvt_ref[bb, h * M_V_DIM:(h + 1) * M_V_DIM, :], ones_rows], axis=0)
        res = wa[row, :] * jnp.dot(state.astype(BF16), qh, preferred_element_type=F32) \
            + jnp.dot(vext, sw, preferred_element_type=F32)
        num = res[:M_V_DIM, :]
        den = res[M_V_DIM:M_V_DIM + 1, :]
        hb = num / jnp.maximum(jnp.abs(den), emt[row, :])
        vw = (vext.astype(F32) * ws[row, :]).astype(BF16)
        new_states.append(wc[row, 0:1] * state + jnp.dot(vw, kh, preferred_element_type=F32))
        hn = hb * lax.rsqrt(jnp.mean(hb * hb, axis=0, keepdims=True) + EPS)
        hn = hn * nw_ref[h * M_V_DIM:(h + 1) * M_V_DIM, :]
        og = _sigmoid(mot_ref[bb, h * M_V_DIM:(h + 1) * M_V_DIM, :].astype(F32))
        outs.append((og * hn).T.astype(BF16))

    for i, (bb, h) in enumerate(pairs):
        state_ref[bb * H + h] = new_states[i]
    for bb in range(nb):
        m_ref[bb] = seqs[bb][8]
        prevq_ref[bb] = mqt_ref[bb]
        y_ref[bb] = jnp.concatenate(outs[bb * H:(bb + 1) * H], axis=1)


def _mlstm(mqt, mk3, mvt, mot, grow, conv_w, conv_b, bias_row, norm_w):
    B, S, _ = mk3.shape
    L = M_CHUNK
    nb = max(d for d in range(1, M_BATCH + 1) if B % d == 0)
    nc = S // L
    assert L == LANES, "per-head scalars are kept lane-replicated next to (8, L) gate rows"
    tok = lambda b, c: (b, c, 0)
    chan = lambda b, c: (b, 0, c)
    const2 = lambda b, c: (0, 0)
    const3 = lambda b, c: (0, 0, 0)
    grow_specs = [pl.BlockSpec((8, L), functools.partial(lambda b, c, n: (0, (b * nb + n) * nc + c), n=n))
                  for n in range(nb)]
    rep = lambda v: jnp.broadcast_to(v.astype(F32)[..., None], v.shape + (L,))
    cwq, cbq = rep(conv_w[:, :M_QK_WIDTH]), rep(conv_b[:M_QK_WIDTH])
    cwk, cbk = conv_w[:, M_QK_WIDTH:].astype(F32), conv_b[M_QK_WIDTH:].reshape(1, -1).astype(F32)
    return pl.pallas_call(
        functools.partial(_mlstm_kernel, nb),
        grid=(B // nb, nc),
        in_specs=[pl.BlockSpec((nb, M_QK_WIDTH, L), chan),
                  pl.BlockSpec((nb, L, M_QK_WIDTH), tok),
                  pl.BlockSpec((nb, M_V_WIDTH, L), chan),
                  pl.BlockSpec((nb, M_V_WIDTH, L), chan),
                  *grow_specs,
                  pl.BlockSpec((CONV_WIDTH, M_QK_WIDTH, L), const3),
                  pl.BlockSpec((M_QK_WIDTH, L), const2),
                  pl.BlockSpec((CONV_WIDTH, M_QK_WIDTH), const2),
                  pl.BlockSpec((1, M_QK_WIDTH), const2),
                  pl.BlockSpec((8, 1), const2),
                  pl.BlockSpec((M_V_WIDTH, L), const2)],
        out_specs=pl.BlockSpec((nb, L, M_V_WIDTH), tok),
        out_shape=jax.ShapeDtypeStruct((B, S, M_V_WIDTH), BF16),
        scratch_shapes=[pltpu.VMEM((nb * M_HEADS, M_STATE_ROWS, M_QK_DIM), F32),
                        pltpu.VMEM((nb, 8, LANES), F32),
                        pltpu.VMEM((nb, M_QK_WIDTH, L), BF16),
                        pltpu.VMEM((nb, L + 8, M_QK_WIDTH), F32)],
        compiler_params=pltpu.CompilerParams(dimension_semantics=("arbitrary", "arbitrary"),
                                             vmem_limit_bytes=VMEM_LIMIT),
        name="mlstm",
    )(mqt, mk3, mvt, mot, *([grow] * nb), cwq, cbq, cwk, cbk, bias_row, rep(norm_w))


def _outproj_kernel(x_ref, ya_ref, ym_ref, nmix_ref, wgab_ref, wa_ref, wm_ref, wo_ref, nw_ref, wr_ref, br_ref,
                    x1_ref, h2_ref, meta_ref, code_ref, cnt_ref, cnt_scr):
    tm = TOK_TILE
    i = pl.program_id(0)

    @pl.when(i == 0)
    def _():
        cnt_scr[...] = jnp.zeros_like(cnt_scr)

    pa = jnp.dot(ya_ref[...], wa_ref[...], preferred_element_type=F32)
    pm = jnp.dot(ym_ref[...], wm_ref[...], preferred_element_type=F32)
    x = x_ref[...]
    h1 = _rms(x, nmix_ref[...]).astype(BF16)
    ga = jnp.dot(h1, wgab_ref[:, :D_MODEL], preferred_element_type=F32)
    gb = jnp.dot(h1, wgab_ref[:, D_MODEL:], preferred_element_type=F32)
    mix = _sigmoid(ga) * pa + _sigmoid(gb) * pm
    x1 = x + jnp.dot(mix.astype(BF16), wo_ref[...], preferred_element_type=F32)
    x1_ref[...] = x1
    h2 = _rms(x1, nw_ref[...])
    h2_ref[...] = _pack_rows(h2)

    nt = (((1,), (1,)), ((), ()))
    h_hi = h2.astype(BF16)
    h_lo = (h2 - h_hi.astype(F32)).astype(BF16)
    hw = lax.dot_general(wr_ref[...], h_hi, nt, preferred_element_type=F32)
    logits = (hw[:LANES] + hw[LANES:]
              + lax.dot_general(wr_ref[:LANES, :], h_lo, nt, preferred_element_type=F32))[:ROUTER_ROWS]
    logits = logits + br_ref[...]
    row8 = lax.broadcasted_iota(jnp.int32, (8, tm), 0)

    def first_argmax(v):
        mx = jnp.max(v, axis=0, keepdims=True)
        idx = jnp.min(jnp.where(v == mx, row8, 8), axis=0, keepdims=True)
        return mx, idx

    gl = jnp.where(row8 < N_GROUPS, logits[0:8], NEG)
    gmax, gi = first_argmax(gl)
    gp = 1.0 / jnp.sum(jnp.exp(gl - gmax), axis=0, keepdims=True)
    el = jnp.zeros((8, tm), F32)
    for g in range(N_GROUPS):
        el = jnp.where(gi == g, logits[8 + 8 * g:16 + 8 * g], el)
    v1, j1 = first_argmax(el)
    v2, j2 = first_argmax(jnp.where(row8 == j1, NEG, el))
    t = jnp.exp(v2 - v1)
    w1 = gp / (1.0 + t)
    w2 = gp * t / (1.0 + t)
    e1 = gi * EXPERTS_PER_GROUP + j1
    e2 = gi * EXPERTS_PER_GROUP + j2

    erow = lax.broadcasted_iota(jnp.int32, (N_EXPERTS, tm), 0)
    hit1 = erow == e1
    hit2 = erow == e2
    onehot = jnp.where(hit1 | hit2, 1.0, 0.0)
    ri = lax.broadcasted_iota(jnp.int32, (tm, tm), 0)
    ci = lax.broadcasted_iota(jnp.int32, (tm, tm), 1)
    before = jnp.where(ri < ci, 1.0, 0.0).astype(BF16)
    rank = jnp.dot(onehot.astype(BF16), before, preferred_element_type=F32) + cnt_scr[...]
    r1 = jnp.sum(jnp.where(hit1, rank, 0.0), axis=0, keepdims=True)
    r2 = jnp.sum(jnp.where(hit2, rank, 0.0), axis=0, keepdims=True)
    cnt = jnp.broadcast_to((rank + onehot)[:, tm - 1:tm], (N_EXPERTS, tm))
    cnt_scr[...] = cnt
    cnt_ref[...] = cnt[:, :LANES]

    c1 = e1 * RANK_RADIX + r1.astype(jnp.int32)
    c2 = e2 * RANK_RADIX + r2.astype(jnp.int32)
    code_ref[...] = jnp.where(row8 == 0, c1, jnp.where(row8 == 1, c2, 0))
    wrow = lax.broadcasted_iota(jnp.int32, (LANES, tm), 0)
    meta_ref[...] = jnp.where(wrow == 4, w1, jnp.where(wrow == 5, w2, 0.0)).T


def _outproj(x2, ya, ym, norm_mix, w_gab, wa, wm, wo, norm_w, w_rt, b_rt):
    T = x2.shape[0]
    tm = TOK_TILE
    row = lambda i: (i, 0)
    const = lambda i: (0, 0)
    return pl.pallas_call(
        _outproj_kernel,
        grid=(T // tm,),
        in_specs=[pl.BlockSpec((tm, D_MODEL), row),
                  pl.BlockSpec((tm, ATT_Q_WIDTH), row),
                  pl.BlockSpec((tm, M_V_WIDTH), row),
                  pl.BlockSpec((1, D_MODEL), const),
                  pl.BlockSpec((D_MODEL, 2 * D_MODEL), const),
                  pl.BlockSpec((ATT_Q_WIDTH, D_MODEL), const),
                  pl.BlockSpec((M_V_WIDTH, D_MODEL), const),
                  pl.BlockSpec((D_MODEL, D_MODEL), const),
                  pl.BlockSpec((1, D_MODEL), const),
                  pl.BlockSpec((2 * LANES, D_MODEL), const),
                  pl.BlockSpec((ROUTER_ROWS, tm), const)],
        out_specs=[pl.BlockSpec((tm, D_MODEL), row),
                   pl.BlockSpec((tm, PACKED), row),
                   pl.BlockSpec((tm, LANES), row),
                   pl.BlockSpec((8, tm), lambda i: (0, i)),
                   pl.BlockSpec((N_EXPERTS, LANES), const)],
        out_shape=[jax.ShapeDtypeStruct((T, D_MODEL), F32),
                   jax.ShapeDtypeStruct((T, PACKED), jnp.int32),
                   jax.ShapeDtypeStruct((T, LANES), F32),
                   jax.ShapeDtypeStruct((8, T), jnp.int32),
                   jax.ShapeDtypeStruct((N_EXPERTS, LANES), F32)],
        scratch_shapes=[pltpu.VMEM((N_EXPERTS, tm), F32)],
        compiler_params=pltpu.CompilerParams(dimension_semantics=("arbitrary",),
                                             vmem_limit_bytes=VMEM_LIMIT),
        name="outproj",
    )(x2, ya, ym, norm_mix, w_gab, wa, wm, wo, norm_w, w_rt, b_rt)


def _slots_kernel(cnt_ref, pstart_ref, blk_e_ref, nvalid_ref, nused_ref):
    R = SLOT_BLOCK
    nblk = blk_e_ref.shape[0]

    def per_expert(e, run):
        c = cnt_ref[e]
        end = run + ((c + R - 1) // R) * R
        pstart_ref[e] = run

        def set_blk(b, carry):
            blk_e_ref[b] = e
            nvalid_ref[b] = jnp.minimum(run + c - b * R, R)
            return carry
        lax.fori_loop(run // R, end // R, set_blk, 0)
        return end

    total = lax.fori_loop(0, N_EXPERTS, per_expert, 0)
    nused_ref[0] = total // R

    def tail_blk(b, carry):
        blk_e_ref[b] = N_EXPERTS - 1
        nvalid_ref[b] = 0
        return carry
    lax.fori_loop(total // R, nblk, tail_blk, 0)


def _slots(counts, nblk):
    smem = pl.BlockSpec(memory_space=pltpu.SMEM)
    return pl.pallas_call(
        _slots_kernel,
        in_specs=[smem],
        out_specs=[smem, smem, smem, smem],
        out_shape=[jax.ShapeDtypeStruct((N_EXPERTS,), jnp.int32),
                   jax.ShapeDtypeStruct((nblk,), jnp.int32),
                   jax.ShapeDtypeStruct((nblk,), jnp.int32),
                   jax.ShapeDtypeStruct((1,), jnp.int32)],
        name="slots",
    )(counts)


def _experts_kernel(blk_e_ref, nvalid_ref, nused_ref, xs_ref, wg_ref, wu_ref, wd_ref, ys_ref):
    j = pl.program_id(0)
    R = SLOT_BLOCK
    valid = j < nused_ref[0]

    @pl.when(valid)
    def _():
        rows = lax.broadcasted_iota(jnp.int32, (R, PACKED), 0)
        xw = jnp.where(rows < nvalid_ref[j], xs_ref[...], 0)
        xb = _unpack_rows(xw).astype(BF16)
        g = jnp.dot(xb, wg_ref[0], preferred_element_type=F32)
        u = jnp.dot(xb, wu_ref[0], preferred_element_type=F32)
        act = (g * _sigmoid(g) * u).astype(BF16)
        ys_ref[...] = _pack_rows(jnp.dot(act, wd_ref[0], preferred_element_type=F32))

    @pl.when(jnp.logical_not(valid))
    def _():
        ys_ref[...] = jnp.zeros_like(ys_ref)


def _experts(blk_e, nvalid, nused, xs, w_gate, w_up, w_down):
    nblk = blk_e.shape[0]
    R = SLOT_BLOCK
    wspec = lambda shape: pl.BlockSpec((1,) + shape, lambda j, be, nv, nu: (be[j], 0, 0))
    grid_spec = pltpu.PrefetchScalarGridSpec(
        num_scalar_prefetch=3,
        grid=(nblk,),
        in_specs=[pl.BlockSpec((R, PACKED), lambda j, be, nv, nu: (jnp.minimum(j, nu[0] - 1), 0)),
                  wspec((D_MODEL, D_EXPERT)),
                  wspec((D_MODEL, D_EXPERT)),
                  wspec((D_EXPERT, D_MODEL))],
        out_specs=pl.BlockSpec((R, PACKED), lambda j, be, nv, nu: (j, 0)),
    )
    return pl.pallas_call(
        _experts_kernel,
        grid_spec=grid_spec,
        out_shape=jax.ShapeDtypeStruct((nblk * R, PACKED), jnp.int32),
        compiler_params=pltpu.CompilerParams(dimension_semantics=("arbitrary",),
                                             vmem_limit_bytes=VMEM_LIMIT),
        name="experts",
    )(blk_e, nvalid, nused, xs, w_gate, w_up, w_down)


def _dest_kernel(pstart_ref, code_ref, dest_ref):
    c = code_ref[...]
    e = c >> RANK_BITS
    base = jnp.zeros_like(c)
    for k in range(N_EXPERTS):
        base = jnp.where(e == k, pstart_ref[k], base)
    dest_ref[...] = base + (c & (RANK_RADIX - 1))


def _dests(pstart, code_rows):
    T = code_rows.shape[1]
    tc = min(4096, T)
    return pl.pallas_call(
        _dest_kernel,
        grid=(T // tc,),
        in_specs=[pl.BlockSpec(memory_space=pltpu.SMEM),
                  pl.BlockSpec((8, tc), lambda i: (0, i))],
        out_specs=pl.BlockSpec((8, tc), lambda i: (0, i)),
        out_shape=jax.ShapeDtypeStruct((8, T), jnp.int32),
        name="dests",
    )(pstart, code_rows)


def _sc_move_rows(src, idx, n_out, scatter):
    M = idx.shape[0]
    n_src, width = src.shape
    info = plsc.get_sparse_core_info()
    nc, nw = info.num_cores, info.num_cores * info.num_subcores
    per_w = M // nw
    ch = SC_CHUNK
    nch = per_w // ch
    assert per_w * nw == M and nch * ch == per_w and nch % 2 == 0 and n_src % per_w == 0
    mesh = plsc.VectorSubcoreMesh(core_axis_name="c", subcore_axis_name="s")

    @functools.partial(
        pl.kernel, mesh=mesh, out_type=jax.ShapeDtypeStruct((n_out, width), src.dtype),
        scratch_types=[pltpu.VMEM((per_w,), jnp.int32),
                       pltpu.VMEM((ch, width), src.dtype), pltpu.VMEM((ch, width), src.dtype),
                       pltpu.SemaphoreType.DMA, pltpu.SemaphoreType.DMA,
                       pltpu.SemaphoreType.DMA, pltpu.SemaphoreType.DMA],
        name="sc_scatter" if scatter else "sc_gather")
    def move(src_hbm, idx_hbm, out_hbm, idx_v, buf0, buf1, in0, in1, out0, out1):
        wid = lax.axis_index("s") * nc + lax.axis_index("c")
        base = wid * per_w
        pltpu.sync_copy(idx_hbm.at[pl.ds(base, per_w)], idx_v)
        lin_base = lax.rem(base, n_src) if scatter else base

        def read(c, buf, sem):
            if scatter:
                return pltpu.make_async_copy(src_hbm.at[pl.ds(lin_base + c * ch, ch)], buf, sem)
            return pltpu.make_async_copy(src_hbm.at[idx_v.at[pl.ds(c * ch, ch)]], buf, sem)

        def write(c, buf, sem):
            if scatter:
                return pltpu.make_async_copy(buf, out_hbm.at[idx_v.at[pl.ds(c * ch, ch)]], sem)
            return pltpu.make_async_copy(buf, out_hbm.at[pl.ds(lin_base + c * ch, ch)], sem)

        read(0, buf0, in0).start()

        @pl.loop(0, nch, step=2)
        def _(c):
            @pl.when(c > 0)
            def _():
                write(c - 1, buf1, out1).wait()
            read(c + 1, buf1, in1).start()
            read(c, buf0, in0).wait()
            write(c, buf0, out0).start()
            read(c + 1, buf1, in1).wait()
            write(c + 1, buf1, out1).start()
            write(c, buf0, out0).wait()

            @pl.when(c + 2 < nch)
            def _():
                read(c + 2, buf0, in0).start()

        write(nch - 1, buf1, out1).wait()

    return move(src, idx)


def _final_kernel(x1_ref, y1_ref, y2_ref, meta_ref, nw_ref, o_ref):
    w1 = meta_ref[:, 4:5]
    w2 = meta_ref[:, 5:6]
    x2 = x1_ref[...] + (w1 * _unpack_rows(y1_ref[...]) + w2 * _unpack_rows(y2_ref[...]))
    o_ref[...] = _rms(x2, nw_ref[...])


def _final(x1, yg, meta, norm_w):
    T = x1.shape[0]
    tm = COMB_TILE
    nt = T // tm
    row = lambda i: (i, 0)
    return pl.pallas_call(
        _final_kernel,
        grid=(nt,),
        in_specs=[pl.BlockSpec((tm, D_MODEL), row),
                  pl.BlockSpec((tm, PACKED), row),
                  pl.BlockSpec((tm, PACKED), lambda i: (i + nt, 0)),
                  pl.BlockSpec((tm, LANES), row),
                  pl.BlockSpec((1, D_MODEL), lambda i: (0, 0))],
        out_specs=pl.BlockSpec((tm, D_MODEL), row),
        out_shape=jax.ShapeDtypeStruct((T, D_MODEL), F32),
        compiler_params=pltpu.CompilerParams(dimension_semantics=("arbitrary",),
                                             vmem_limit_bytes=VMEM_LIMIT),
        name="final",
    )(x1, yg, yg, meta, norm_w)


def _layer(x, norm_mix_w, w_in, conv_w, conv_b, b_igate, b_fgate, attn_sinks, mlstm_norm_w,
           w_attn_o, w_mlstm_o, w_out, norm_ffn_w, w_group, b_group, w_router, b_router,
           w_gate, w_up, w_down, out_norm_w):
    B, S, D = x.shape
    T = B * S
    x2 = x.reshape(T, D)

    w_tok = jnp.concatenate([w_in[:, _O_AK:_O_AV], w_in[:, _O_MK:_O_MV]], axis=1).astype(BF16)
    w_chan = jnp.concatenate([w_in[:, _O_AQ:_O_AK] * (HEAD_DIM ** -0.5), w_in[:, _O_AV:_O_MQ],
                              w_in[:, _O_MQ:_O_MK], w_in[:, _O_MV:_O_MI]], axis=1).T.astype(BF16)
    w_gates = jnp.pad(w_in[:, _O_MI:_O_GA], ((0, 0), (0, LANES - 2 * M_HEADS))).astype(BF16)
    expert_ws = [w.reshape(-1, w.shape[-1]) for w in (w_gate, w_up, w_down)]
    ak, mk, aqt, avt, mqt, mvt, mot, grow, wg16, wu16, wd16 = _inproj(
        x2, norm_mix_w.reshape(1, D), w_tok, w_chan, w_gates, expert_ws, B, S)

    ya = _attn(attn_sinks.astype(F32), aqt, ak.reshape(B, S, -1), avt)

    bias = jnp.concatenate([b_igate, b_fgate]).astype(F32)
    ym = _mlstm(mqt, mk.reshape(B, S, -1), mvt, mot, grow, conv_w, conv_b,
                bias.reshape(2 * M_HEADS, 1), mlstm_norm_w)

    gpad = jnp.zeros((8 - N_GROUPS, D), F32)
    w_rt = jnp.concatenate([w_group.T, gpad, w_router.T, jnp.zeros((LANES - ROUTER_ROWS, D), F32)], axis=0)
    w_rt_hi = w_rt.astype(BF16)
    w_rt = jnp.concatenate([w_rt_hi, (w_rt - w_rt_hi.astype(F32)).astype(BF16)], axis=0)
    b_rt = jnp.concatenate([b_group, jnp.zeros((8 - N_GROUPS,), F32), b_router]).astype(F32)
    b_rt = jnp.broadcast_to(b_rt[:, None], (ROUTER_ROWS, TOK_TILE))
    x1, h2p, meta, code_rows, cnt = _outproj(x2, ya.reshape(T, -1), ym.reshape(T, -1),
                                            norm_mix_w.reshape(1, D), w_in[:, _O_GA:].astype(BF16),
                                            w_attn_o.astype(BF16), w_mlstm_o.astype(BF16),
                                            w_out.astype(BF16), norm_ffn_w.reshape(1, D), w_rt, b_rt)

    counts = cnt[:, 0].astype(jnp.int32)
    nblk = (2 * T) // SLOT_BLOCK + N_EXPERTS
    pstart, blk_e, nvalid, nused = _slots(counts, nblk)
    dest = _dests(pstart, code_rows)[0:2].reshape(2 * T)

    xs = _sc_move_rows(h2p, dest, nblk * SLOT_BLOCK, scatter=True)
    ys = _experts(blk_e, nvalid, nused, xs, wg16.reshape(w_gate.shape), wu16.reshape(w_up.shape),
                  wd16.reshape(w_down.shape))
    yg = _sc_move_rows(ys, dest, 2 * T, scatter=False)
    out = _final(x1, yg, meta, out_norm_w.reshape(1, D))
    return out.reshape(B, S, D)


def kernel(x, norm_mix_w, w_in, conv_w, conv_b, b_igate, b_fgate, attn_sinks, mlstm_norm_w, w_attn_o,
           w_mlstm_o, w_out, norm_ffn_w, w_group, b_group, w_router, b_router, w_gate, w_up, w_down,
           norm_final_w):
    depth = w_in.shape[0]
    assert depth == 1, "final RMSNorm is fused into the last layer's combine kernel"
    return _layer(x, norm_mix_w[0], w_in[0], conv_w[0], conv_b[0], b_igate[0], b_fgate[0],
                  attn_sinks[0], mlstm_norm_w[0], w_attn_o[0], w_mlstm_o[0], w_out[0], norm_ffn_w[0],
                  w_group[0], b_group[0], w_router[0], b_router[0], w_gate[0], w_up[0], w_down[0],
                  norm_final_w)
```

```python
import functools

import jax
import jax.numpy as jnp
from jax import lax
from jax.experimental import pallas as pl
from jax.experimental.pallas import tpu as pltpu
from jax.experimental.pallas import tpu_sc as plsc

F32 = jnp.float32
BF16 = jnp.bfloat16

D_MODEL = 1024
N_Q_HEADS = 8
N_KV_HEADS = 2
HEAD_DIM = 64
WINDOW = 128
GQA_GROUP = N_Q_HEADS // N_KV_HEADS
M_HEADS = 4
M_QK_DIM = 64
M_V_DIM = 128
CONV_WIDTH = 4
N_GROUPS = 4
EXPERTS_PER_GROUP = 8
N_EXPERTS = N_GROUPS * EXPERTS_PER_GROUP
D_EXPERT = 512
EPS = 1e-6

ATT_Q_WIDTH = N_Q_HEADS * HEAD_DIM
ATT_KV_WIDTH = N_KV_HEADS * HEAD_DIM
M_QK_WIDTH = M_HEADS * M_QK_DIM
M_V_WIDTH = M_HEADS * M_V_DIM

LANES = 128
NEG = -1e30
VMEM_LIMIT = 56 * 1024 * 1024

TOK_TILE = 512
ATT_TILE = 512
M_CHUNK = 128
M_BATCH = 8
M_STATE_ROWS = M_V_DIM + 16
SLOT_BLOCK = 512
COMB_TILE = 512
RANK_RADIX = 65536
RANK_BITS = 16
ROUTER_ROWS = 8 + N_EXPERTS
PACKED = D_MODEL // 2
SC_CHUNK = 64

_O_AQ = 0
_O_AK = _O_AQ + ATT_Q_WIDTH
_O_AV = _O_AK + ATT_KV_WIDTH
_O_MQ = _O_AV + ATT_KV_WIDTH
_O_MK = _O_MQ + M_QK_WIDTH
_O_MV = _O_MK + M_QK_WIDTH
_O_MO = _O_MV + M_V_WIDTH
_O_MI = _O_MO + M_V_WIDTH
_O_MF = _O_MI + M_HEADS
_O_GA = _O_MF + M_HEADS
_O_GB = _O_GA + D_MODEL
_O_END = _O_GB + D_MODEL


def _rms(x, w):
    return x * lax.rsqrt(jnp.mean(x * x, axis=-1, keepdims=True) + EPS) * w


def _sigmoid(x):
    return 0.5 * jnp.tanh(0.5 * x) + 0.5


def _pack_rows(x):
    half = x.shape[1] // 2
    bits = lax.bitcast_convert_type(x.astype(BF16).astype(F32), jnp.uint32)
    packed = (bits[:, :half] >> 16) | (bits[:, half:] & jnp.uint32(0xFFFF0000))
    return lax.bitcast_convert_type(packed, jnp.int32)


def _unpack_rows(w):
    u = lax.bitcast_convert_type(w, jnp.uint32)
    lo = lax.bitcast_convert_type(u << 16, F32)
    hi = lax.bitcast_convert_type(u & jnp.uint32(0xFFFF0000), F32)
    return jnp.concatenate([lo, hi], axis=-1)


def _log_sigmoid(x):
    return jnp.minimum(x, 0.0) - jnp.log1p(jnp.exp(-jnp.abs(x)))


_TOK_SPLITS = (ATT_KV_WIDTH, M_QK_WIDTH)
_CHAN_SPLITS = (ATT_Q_WIDTH, ATT_KV_WIDTH, M_QK_WIDTH, M_V_WIDTH, M_V_WIDTH)


def _inproj_kernel(x_ref, nw_ref, w_ref, wt_ref, wg_ref, eg_ref, eu_ref, ed_ref, *out_refs):
    tok_refs = out_refs[:len(_TOK_SPLITS)]
    chan_refs = out_refs[len(_TOK_SPLITS):len(_TOK_SPLITS) + len(_CHAN_SPLITS)]
    grow_ref = out_refs[len(_TOK_SPLITS) + len(_CHAN_SPLITS)]
    for src, dst in zip((eg_ref, eu_ref, ed_ref), out_refs[-3:]):
        dst[...] = src[...].astype(BF16)
    h = _rms(x_ref[...], nw_ref[...]).astype(BF16)
    lo = 0
    for ref, width in zip(tok_refs, _TOK_SPLITS):
        ref[...] = jnp.dot(h, w_ref[:, lo:lo + width], preferred_element_type=F32).astype(BF16)
        lo += width
    lo = 0
    for ref, width in zip(chan_refs, _CHAN_SPLITS):
        ref[0] = lax.dot_general(wt_ref[lo:lo + width, :], h, (((1,), (1,)), ((), ())),
                                 preferred_element_type=F32).astype(BF16)
        lo += width
    g = jnp.dot(h, wg_ref[...], preferred_element_type=F32)
    grow_ref[...] = g.T[0:8, :]


def _inproj(x2, norm_w, w_tok, w_chan, w_gates, expert_ws, B, S):
    T = x2.shape[0]
    tm = TOK_TILE
    tpb = S // tm
    nsteps = T // tm
    row = lambda i: (i, 0)
    const = lambda i: (0, 0)
    chan = lambda i: (i // tpb, 0, i % tpb)
    cast_specs = [pl.BlockSpec((w.shape[0] // nsteps, w.shape[1]), row) for w in expert_ws]
    assert all(w.shape[0] % (8 * nsteps) == 0 for w in expert_ws)
    return pl.pallas_call(
        _inproj_kernel,
        grid=(nsteps,),
        in_specs=[pl.BlockSpec((tm, D_MODEL), row),
                  pl.BlockSpec((1, D_MODEL), const),
                  pl.BlockSpec(w_tok.shape, const),
                  pl.BlockSpec(w_chan.shape, const),
                  pl.BlockSpec((D_MODEL, LANES), const),
                  *cast_specs],
        out_specs=[*[pl.BlockSpec((tm, w), row) for w in _TOK_SPLITS],
                   *[pl.BlockSpec((1, w, tm), chan) for w in _CHAN_SPLITS],
                   pl.BlockSpec((8, tm), lambda i: (0, i)),
                   *cast_specs],
        out_shape=[*[jax.ShapeDtypeStruct((T, w), BF16) for w in _TOK_SPLITS],
                   *[jax.ShapeDtypeStruct((B, w, S), BF16) for w in _CHAN_SPLITS],
                   jax.ShapeDtypeStruct((8, T), F32),
                   *[jax.ShapeDtypeStruct(w.shape, BF16) for w in expert_ws]],
        compiler_params=pltpu.CompilerParams(dimension_semantics=("arbitrary",),
                                             vmem_limit_bytes=VMEM_LIMIT),
        name="inproj",
    )(x2, norm_w, w_tok, w_chan, w_gates, *expert_ws)


def _attn_kernel(sink_ref, qt_ref, k_ref, kp_ref, vt_ref, vtp_ref, o_ref):
    i = pl.program_id(1)
    W = WINDOW
    G = GQA_GROUP
    u = lax.broadcasted_iota(jnp.int32, (W, W), 0)
    t = lax.broadcasted_iota(jnp.int32, (W, W), 1)
    from_prev = u > t

    def keys_values(j):
        if j == 0:
            return kp_ref[0], vtp_ref[0], k_ref[0, 0:W, :], vt_ref[0, :, 0:W]
        return (k_ref[0, (j - 1) * W:j * W, :], vt_ref[0, :, (j - 1) * W:j * W],
                k_ref[0, j * W:(j + 1) * W, :], vt_ref[0, :, j * W:(j + 1) * W])

    def scores(j, g):
        k_prev, _, k_cur, _ = keys_values(j)
        dims = slice(g * HEAD_DIM, (g + 1) * HEAD_DIM)
        qg = jnp.concatenate([qt_ref[0, h * HEAD_DIM:(h + 1) * HEAD_DIM, j * W:(j + 1) * W]
                              for h in range(g * G, (g + 1) * G)], axis=1)
        return (jnp.dot(k_prev[:, dims], qg, preferred_element_type=F32),
                jnp.dot(k_cur[:, dims], qg, preferred_element_type=F32))

    tasks = [(j, g) for j in range(ATT_TILE // W) for g in range(N_KV_HEADS)]
    ahead = scores(*tasks[0])
    pair_rows = []
    for n_task, (j, g) in enumerate(tasks):
        s_prev, s_cur = ahead
        if n_task + 1 < len(tasks):
            ahead = scores(*tasks[n_task + 1])
        cols = slice(j * W, (j + 1) * W)
        _, vt_prev, _, vt_cur = keys_values(j)
        dims = slice(g * HEAD_DIM, (g + 1) * HEAD_DIM)
        for n in range(G):
            hc = slice(n * W, (n + 1) * W)
            sp = s_prev[:, hc]
            if j == 0:
                sp = jnp.where(i > 0, sp, NEG)
            s = jnp.where(from_prev, sp, s_cur[:, hc])
            sink = sink_ref[g * G + n]
            m = jnp.maximum(jnp.max(s, axis=0, keepdims=True), sink)
            p = jnp.exp(s - m)
            inv = 1.0 / (jnp.sum(p, axis=0, keepdims=True) + jnp.exp(sink - m))
            p_prev = jnp.where(from_prev, p, 0.0).astype(BF16)
            p_cur = jnp.where(from_prev, 0.0, p).astype(BF16)
            o = jnp.dot(vt_prev[dims, :], p_prev, preferred_element_type=F32) \
                + jnp.dot(vt_cur[dims, :], p_cur, preferred_element_type=F32)
            pair_rows.append(o * inv)
        if g == N_KV_HEADS - 1:
            out = [jnp.concatenate(pair_rows[n:n + 2], axis=0).T for n in range(0, N_Q_HEADS, 2)]
            o_ref[0, cols, :] = jnp.concatenate(out, axis=1).astype(BF16)
            pair_rows = []


def _attn(sinks, qt, k3, vt):
    B, S, _ = k3.shape
    tq = ATT_TILE
    per = tq // WINDOW
    chan = lambda b, i: (b, 0, i)
    tok = lambda b, i: (b, i, 0)
    return pl.pallas_call(
        _attn_kernel,
        grid=(B, S // tq),
        in_specs=[pl.BlockSpec(memory_space=pltpu.SMEM),
                  pl.BlockSpec((1, ATT_Q_WIDTH, tq), chan),
                  pl.BlockSpec((1, tq, ATT_KV_WIDTH), tok),
                  pl.BlockSpec((1, WINDOW, ATT_KV_WIDTH), lambda b, i: (b, jnp.maximum(i * per - 1, 0), 0)),
                  pl.BlockSpec((1, ATT_KV_WIDTH, tq), chan),
                  pl.BlockSpec((1, ATT_KV_WIDTH, WINDOW), lambda b, i: (b, 0, jnp.maximum(i * per - 1, 0)))],
        out_specs=pl.BlockSpec((1, tq, ATT_Q_WIDTH), tok),
        out_shape=jax.ShapeDtypeStruct((B, S, ATT_Q_WIDTH), BF16),
        compiler_params=pltpu.CompilerParams(dimension_semantics=("arbitrary", "arbitrary"),
                                             vmem_limit_bytes=VMEM_LIMIT),
        name="attn",
    )(sinks, qt, k3, k3, vt, vt)


def _mlstm_kernel(nb, *refs):
    L = M_CHUNK
    H = M_HEADS
    mqt_ref, mk_ref, mvt_ref, mot_ref = refs[:4]
    grow_refs = refs[4:4 + nb]
    (cwq_ref, cbq_ref, cwk_ref, cbk_ref, brow_ref, nw_ref, y_ref,
     state_ref, m_ref, prevq_ref, ubuf_ref) = refs[4 + nb:]
    cidx = pl.program_id(1)

    @pl.when(cidx == 0)
    def _():
        state_ref[...] = jnp.zeros_like(state_ref)
        m_ref[...] = jnp.zeros_like(m_ref)
        prevq_ref[...] = jnp.zeros_like(prevq_ref)
        ubuf_ref[:, 0:8, :] = jnp.zeros((nb, 8, M_QK_WIDTH), F32)

    ri = lax.broadcasted_iota(jnp.int32, (L, L), 0)
    ci = lax.broadcasted_iota(jnp.int32, (L, L), 1)
    causal_t = ri <= ci
    triu = jnp.where(causal_t, 1.0, 0.0).astype(F32)
    lane = lax.broadcasted_iota(jnp.int32, (8, L), 1)
    r2 = lax.broadcasted_iota(jnp.int32, (2 * L, L), 0)
    c2 = lax.broadcasted_iota(jnp.int32, (2 * L, L), 1)
    shifts = [jnp.where(r2 - c2 == L - k, 1.0, 0.0).astype(BF16) for k in range(1, CONV_WIDTH)]
    ones_rows = jnp.where(lax.broadcasted_iota(jnp.int32, (M_STATE_ROWS - M_V_DIM, L), 0) == 0,
                          1.0, 0.0).astype(BF16)

    pairs = [(bb, h) for bb in range(nb) for h in range(H)]
    states = [state_ref[bb * H + h] for bb, h in pairs]
    m_prevs = [m_ref[bb] for bb in range(nb)]

    def prepare(bb):
        cur = mqt_ref[bb]
        both = jnp.concatenate([prevq_ref[bb], cur], axis=1)
        acc = cbq_ref[...] + cwq_ref[CONV_WIDTH - 1] * cur.astype(F32)
        for k in range(1, CONV_WIDTH):
            acc = acc + cwq_ref[CONV_WIDTH - 1 - k] * jnp.dot(both, shifts[k - 1],
                                                              preferred_element_type=F32)
        qt = (acc * _sigmoid(acc)).astype(BF16)
        ubuf_ref[bb, 8:L + 8, :] = mk_ref[bb].astype(F32)
        acc = cbk_ref[...] + cwk_ref[CONV_WIDTH - 1:CONV_WIDTH, :] * ubuf_ref[bb, 8:L + 8, :]
        for j in range(CONV_WIDTH - 1):
            off = 8 - (CONV_WIDTH - 1) + j
            acc = acc + cwk_ref[j:j + 1, :] * ubuf_ref[bb, off:off + L, :]
        ubuf_ref[bb, 0:8, :] = ubuf_ref[bb, L:L + 8, :]
        kk = (acc * _sigmoid(acc) * (M_QK_DIM ** -0.5)).astype(BF16)

        gr = grow_refs[bb][...] + brow_ref[...]
        b = jnp.dot(_log_sigmoid(gr), triu, preferred_element_type=F32,
                    precision=lax.Precision.HIGHEST)
        gi = pltpu.roll(gr, 4, axis=0)
        u = gi - b
        cm = u
        for sh in (1, 2, 4, 8, 16, 32, 64):
            if sh < L:
                cm = jnp.maximum(cm, jnp.where(lane >= sh, pltpu.roll(cm, sh, axis=1), NEG))
        m_prev = m_prevs[bb]
        mt = b + jnp.maximum(m_prev, cm)
        wa = jnp.exp(b + m_prev - mt)
        emt = jnp.exp(-mt)
        m_new = jnp.broadcast_to(mt[:, L - 1:L], (8, L))
        b_last = jnp.broadcast_to(b[:, L - 1:L], (8, L))
        wc = jnp.exp(b_last + m_prev - m_new)
        ws = jnp.exp(b_last + u - m_new)
        return qt, kk, b - mt, u.T, wa, emt, wc, ws, m_new

    seqs = [prepare(bb) for bb in range(nb)]

    def products(bb, h, state):
        qt, kk = seqs[bb][0], seqs[bb][1]
        qh = qt[h * M_QK_DIM:(h + 1) * M_QK_DIM, :]
        kh = kk[:, h * M_QK_DIM:(h + 1) * M_QK_DIM]
        return (kh, jnp.dot(kh, qh, preferred_element_type=F32),
                jnp.dot(state.astype(BF16), qh, preferred_element_type=F32))

    new_states, outs = [], []
    ahead = products(*pairs[0], states[0])
    for n, ((bb, h), state) in enumerate(zip(pairs, states)):
        kh, st, inter = ahead
        if n + 1 < len(pairs):
            ahead = products(*pairs[n + 1], states[n + 1])
        _, _, bmt, ucol, wa, emt, wc, ws, _ = seqs[bb]
        row = slice(H + h, H + h + 1)
        wq = jnp.exp(jnp.where(causal_t, bmt[row, :] + ucol[:, H + h:H + h + 1], NEG))
        sw = (st * wq).astype(BF16)
        vext = jnp.concatenate([mvt_ref[bb, h * M_V_DIM:(h + 1) * M_V_DIM, :], ones_rows], axis=0)
        res = wa[row, :] * inter + jnp.dot(vext, sw, preferred_element_type=F32)
        num = res[:M_V_DIM, :]
        den = res[M_V_DIM:M_V_DIM + 1, :]
        hb = num / jnp.maximum(jnp.abs(den), emt[row, :])
        vw = (vext.astype(F32) * ws[row, :]).astype(BF16)
        new_states.append(wc[row, 0:1] * state + jnp.dot(vw, kh, preferred_element_type=F32))
        hn = hb * lax.rsqrt(jnp.mean(hb * hb, axis=0, keepdims=True) + EPS)
        hn = hn * nw_ref[h * M_V_DIM:(h + 1) * M_V_DIM, :]
        og = _sigmoid(mot_ref[bb, h * M_V_DIM:(h + 1) * M_V_DIM, :].astype(F32))
        outs.append((og * hn).T.astype(BF16))

    for i, (bb, h) in enumerate(pairs):
        state_ref[bb * H + h] = new_states[i]
    for bb in range(nb):
        m_ref[bb] = seqs[bb][8]
        prevq_ref[bb] = mqt_ref[bb]
        y_ref[bb] = jnp.concatenate(outs[bb * H:(bb + 1) * H], axis=1)


def _mlstm(mqt, mk3, mvt, mot, grow, conv_w, conv_b, bias_row, norm_w):
    B, S, _ = mk3.shape
    L = M_CHUNK
    nb = max(d for d in range(1, M_BATCH + 1) if B % d == 0)
    nc = S // L
    assert L == LANES, "per-head scalars are kept lane-replicated next to (8, L) gate rows"
    tok = lambda b, c: (b, c, 0)
    chan = lambda b, c: (b, 0, c)
    const2 = lambda b, c: (0, 0)
    const3 = lambda b, c: (0, 0, 0)
    grow_specs = [pl.BlockSpec((8, L), functools.partial(lambda b, c, n: (0, (b * nb + n) * nc + c), n=n))
                  for n in range(nb)]
    rep = lambda v: jnp.broadcast_to(v.astype(F32)[..., None], v.shape + (L,))
    cwq, cbq = rep(conv_w[:, :M_QK_WIDTH]), rep(conv_b[:M_QK_WIDTH])
    cwk, cbk = conv_w[:, M_QK_WIDTH:].astype(F32), conv_b[M_QK_WIDTH:].reshape(1, -1).astype(F32)
    return pl.pallas_call(
        functools.partial(_mlstm_kernel, nb),
        grid=(B // nb, nc),
        in_specs=[pl.BlockSpec((nb, M_QK_WIDTH, L), chan),
                  pl.BlockSpec((nb, L, M_QK_WIDTH), tok),
                  pl.BlockSpec((nb, M_V_WIDTH, L), chan),
                  pl.BlockSpec((nb, M_V_WIDTH, L), chan),
                  *grow_specs,
                  pl.BlockSpec((CONV_WIDTH, M_QK_WIDTH, L), const3),
                  pl.BlockSpec((M_QK_WIDTH, L), const2),
                  pl.BlockSpec((CONV_WIDTH, M_QK_WIDTH), const2),
                  pl.BlockSpec((1, M_QK_WIDTH), const2),
                  pl.BlockSpec((8, 1), const2),
                  pl.BlockSpec((M_V_WIDTH, L), const2)],
        out_specs=pl.BlockSpec((nb, L, M_V_WIDTH), tok),
        out_shape=jax.ShapeDtypeStruct((B, S, M_V_WIDTH), BF16),
        scratch_shapes=[pltpu.VMEM((nb * M_HEADS, M_STATE_ROWS, M_QK_DIM), F32),
                        pltpu.VMEM((nb, 8, LANES), F32),
                        pltpu.VMEM((nb, M_QK_WIDTH, L), BF16),
                        pltpu.VMEM((nb, L + 8, M_QK_WIDTH), F32)],
        compiler_params=pltpu.CompilerParams(dimension_semantics=("arbitrary", "arbitrary"),
                                             vmem_limit_bytes=VMEM_LIMIT),
        name="mlstm",
    )(mqt, mk3, mvt, mot, *([grow] * nb), cwq, cbq, cwk, cbk, bias_row, rep(norm_w))


def _outproj_kernel(x_ref, ya_ref, ym_ref, nmix_ref, wgab_ref, wa_ref, wm_ref, wo_ref, nw_ref, wr_ref, br_ref,
                    x1_ref, h2_ref, meta_ref, code_ref, cnt_ref, cnt_scr):
    tm = TOK_TILE
    i = pl.program_id(0)

    @pl.when(i == 0)
    def _():
        cnt_scr[...] = jnp.zeros_like(cnt_scr)

    pa = jnp.dot(ya_ref[...], wa_ref[...], preferred_element_type=F32)
    pm = jnp.dot(ym_ref[...], wm_ref[...], preferred_element_type=F32)
    x = x_ref[...]
    h1 = _rms(x, nmix_ref[...]).astype(BF16)
    ga = jnp.dot(h1, wgab_ref[:, :D_MODEL], preferred_element_type=F32)
    gb = jnp.dot(h1, wgab_ref[:, D_MODEL:], preferred_element_type=F32)
    mix = _sigmoid(ga) * pa + _sigmoid(gb) * pm
    x1 = x + jnp.dot(mix.astype(BF16), wo_ref[...], preferred_element_type=F32)
    x1_ref[...] = x1
    h2 = _rms(x1, nw_ref[...])
    h2_ref[...] = _pack_rows(h2)

    nt = (((1,), (1,)), ((), ()))
    h_hi = h2.astype(BF16)
    h_lo = (h2 - h_hi.astype(F32)).astype(BF16)
    hw = lax.dot_general(wr_ref[...], h_hi, nt, preferred_element_type=F32)
    logits = (hw[:LANES] + hw[LANES:]
              + lax.dot_general(wr_ref[:LANES, :], h_lo, nt, preferred_element_type=F32))[:ROUTER_ROWS]
    logits = logits + br_ref[...]
    row8 = lax.broadcasted_iota(jnp.int32, (8, tm), 0)

    def first_argmax(v):
        mx = jnp.max(v, axis=0, keepdims=True)
        idx = jnp.min(jnp.where(v == mx, row8, 8), axis=0, keepdims=True)
        return mx, idx

    gl = jnp.where(row8 < N_GROUPS, logits[0:8], NEG)
    gmax, gi = first_argmax(gl)
    gp = 1.0 / jnp.sum(jnp.exp(gl - gmax), axis=0, keepdims=True)
    el = jnp.zeros((8, tm), F32)
    for g in range(N_GROUPS):
        el = jnp.where(gi == g, logits[8 + 8 * g:16 + 8 * g], el)
    v1, j1 = first_argmax(el)
    v2, j2 = first_argmax(jnp.where(row8 == j1, NEG, el))
    t = jnp.exp(v2 - v1)
    w1 = gp / (1.0 + t)
    w2 = gp * t / (1.0 + t)
    e1 = gi * EXPERTS_PER_GROUP + j1
    e2 = gi * EXPERTS_PER_GROUP + j2

    erow = lax.broadcasted_iota(jnp.int32, (N_EXPERTS, tm), 0)
    hit1 = erow == e1
    hit2 = erow == e2
    onehot = jnp.where(hit1 | hit2, 1.0, 0.0)
    ri = lax.broadcasted_iota(jnp.int32, (tm, tm), 0)
    ci = lax.broadcasted_iota(jnp.int32, (tm, tm), 1)
    before = jnp.where(ri < ci, 1.0, 0.0).astype(BF16)
    rank = jnp.dot(onehot.astype(BF16), before, preferred_element_type=F32) + cnt_scr[...]
    r1 = jnp.sum(jnp.where(hit1, rank, 0.0), axis=0, keepdims=True)
    r2 = jnp.sum(jnp.where(hit2, rank, 0.0), axis=0, keepdims=True)
    cnt = jnp.broadcast_to((rank + onehot)[:, tm - 1:tm], (N_EXPERTS, tm))
    cnt_scr[...] = cnt
    cnt_ref[...] = cnt[:, :LANES]

    c1 = e1 * RANK_RADIX + r1.astype(jnp.int32)
    c2 = e2 * RANK_RADIX + r2.astype(jnp.int32)
    code_ref[...] = jnp.where(row8 == 0, c1, jnp.where(row8 == 1, c2, 0))
    wrow = lax.broadcasted_iota(jnp.int32, (LANES, tm), 0)
    meta_ref[...] = jnp.where(wrow == 4, w1, jnp.where(wrow == 5, w2, 0.0)).T


def _outproj(x2, ya, ym, norm_mix, w_gab, wa, wm, wo, norm_w, w_rt, b_rt):
    T = x2.shape[0]
    tm = TOK_TILE
    row = lambda i: (i, 0)
    const = lambda i: (0, 0)
    return pl.pallas_call(
        _outproj_kernel,
        grid=(T // tm,),
        in_specs=[pl.BlockSpec((tm, D_MODEL), row),
                  pl.BlockSpec((tm, ATT_Q_WIDTH), row),
                  pl.BlockSpec((tm, M_V_WIDTH), row),
                  pl.BlockSpec((1, D_MODEL), const),
                  pl.BlockSpec((D_MODEL, 2 * D_MODEL), const),
                  pl.BlockSpec((ATT_Q_WIDTH, D_MODEL), const),
                  pl.BlockSpec((M_V_WIDTH, D_MODEL), const),
                  pl.BlockSpec((D_MODEL, D_MODEL), const),
                  pl.BlockSpec((1, D_MODEL), const),
                  pl.BlockSpec((2 * LANES, D_MODEL), const),
                  pl.BlockSpec((ROUTER_ROWS, tm), const)],
        out_specs=[pl.BlockSpec((tm, D_MODEL), row),
                   pl.BlockSpec((tm, PACKED), row),
                   pl.BlockSpec((tm, LANES), row),
                   pl.BlockSpec((8, tm), lambda i: (0, i)),
                   pl.BlockSpec((N_EXPERTS, LANES), const)],
        out_shape=[jax.ShapeDtypeStruct((T, D_MODEL), F32),
                   jax.ShapeDtypeStruct((T, PACKED), jnp.int32),
                   jax.ShapeDtypeStruct((T, LANES), F32),
                   jax.ShapeDtypeStruct((8, T), jnp.int32),
                   jax.ShapeDtypeStruct((N_EXPERTS, LANES), F32)],
        scratch_shapes=[pltpu.VMEM((N_EXPERTS, tm), F32)],
        compiler_params=pltpu.CompilerParams(dimension_semantics=("arbitrary",),
                                             vmem_limit_bytes=VMEM_LIMIT),
        name="outproj",
    )(x2, ya, ym, norm_mix, w_gab, wa, wm, wo, norm_w, w_rt, b_rt)


def _slots_kernel(cnt_ref, pstart_ref, blk_e_ref, nvalid_ref, nused_ref):
    R = SLOT_BLOCK
    nblk = blk_e_ref.shape[0]

    def per_expert(e, run):
        c = cnt_ref[e]
        end = run + ((c + R - 1) // R) * R
        pstart_ref[e] = run

        def set_blk(b, carry):
            blk_e_ref[b] = e
            nvalid_ref[b] = jnp.minimum(run + c - b * R, R)
            return carry
        lax.fori_loop(run // R, end // R, set_blk, 0)
        return end

    total = lax.fori_loop(0, N_EXPERTS, per_expert, 0)
    nused_ref[0] = total // R

    def tail_blk(b, carry):
        blk_e_ref[b] = N_EXPERTS - 1
        nvalid_ref[b] = 0
        return carry
    lax.fori_loop(total // R, nblk, tail_blk, 0)


def _slots(counts, nblk):
    smem = pl.BlockSpec(memory_space=pltpu.SMEM)
    return pl.pallas_call(
        _slots_kernel,
        in_specs=[smem],
        out_specs=[smem, smem, smem, smem],
        out_shape=[jax.ShapeDtypeStruct((N_EXPERTS,), jnp.int32),
                   jax.ShapeDtypeStruct((nblk,), jnp.int32),
                   jax.ShapeDtypeStruct((nblk,), jnp.int32),
                   jax.ShapeDtypeStruct((1,), jnp.int32)],
        name="slots",
    )(counts)


def _experts_kernel(blk_e_ref, nvalid_ref, nused_ref, xs_ref, wg_ref, wu_ref, wd_ref, ys_ref):
    j = pl.program_id(0)
    R = SLOT_BLOCK
    valid = j < nused_ref[0]

    @pl.when(valid)
    def _():
        rows = lax.broadcasted_iota(jnp.int32, (R, PACKED), 0)
        xw = jnp.where(rows < nvalid_ref[j], xs_ref[...], 0)
        xb = _unpack_rows(xw).astype(BF16)
        g = jnp.dot(xb, wg_ref[0], preferred_element_type=F32)
        u = jnp.dot(xb, wu_ref[0], preferred_element_type=F32)
        act = (g * _sigmoid(g) * u).astype(BF16)
        ys_ref[...] = _pack_rows(jnp.dot(act, wd_ref[0], preferred_element_type=F32))

    @pl.when(jnp.logical_not(valid))
    def _():
        ys_ref[...] = jnp.zeros_like(ys_ref)


def _experts(blk_e, nvalid, nused, xs, w_gate, w_up, w_down):
    nblk = blk_e.shape[0]
    R = SLOT_BLOCK
    wspec = lambda shape: pl.BlockSpec((1,) + shape, lambda j, be, nv, nu: (be[j], 0, 0))
    grid_spec = pltpu.PrefetchScalarGridSpec(
        num_scalar_prefetch=3,
        grid=(nblk,),
        in_specs=[pl.BlockSpec((R, PACKED), lambda j, be, nv, nu: (jnp.minimum(j, nu[0] - 1), 0)),
                  wspec((D_MODEL, D_EXPERT)),
                  wspec((D_MODEL, D_EXPERT)),
                  wspec((D_EXPERT, D_MODEL))],
        out_specs=pl.BlockSpec((R, PACKED), lambda j, be, nv, nu: (j, 0)),
    )
    return pl.pallas_call(
        _experts_kernel,
        grid_spec=grid_spec,
        out_shape=jax.ShapeDtypeStruct((nblk * R, PACKED), jnp.int32),
        compiler_params=pltpu.CompilerParams(dimension_semantics=("arbitrary",),
                                             vmem_limit_bytes=VMEM_LIMIT),
        name="experts",
    )(blk_e, nvalid, nused, xs, w_gate, w_up, w_down)


def _dest_kernel(pstart_ref, code_ref, dest_ref):
    c = code_ref[...]
    e = c >> RANK_BITS
    base = jnp.zeros_like(c)
    for k in range(N_EXPERTS):
        base = jnp.where(e == k, pstart_ref[k], base)
    dest_ref[...] = base + (c & (RANK_RADIX - 1))


def _dests(pstart, code_rows):
    T = code_rows.shape[1]
    tc = min(4096, T)
    return pl.pallas_call(
        _dest_kernel,
        grid=(T // tc,),
        in_specs=[pl.BlockSpec(memory_space=pltpu.SMEM),
                  pl.BlockSpec((8, tc), lambda i: (0, i))],
        out_specs=pl.BlockSpec((8, tc), lambda i: (0, i)),
        out_shape=jax.ShapeDtypeStruct((8, T), jnp.int32),
        name="dests",
    )(pstart, code_rows)


def _sc_move_rows(src, idx, n_out, scatter):
    M = idx.shape[0]
    n_src, width = src.shape
    info = plsc.get_sparse_core_info()
    nc, nw = info.num_cores, info.num_cores * info.num_subcores
    per_w = M // nw
    ch = SC_CHUNK
    nch = per_w // ch
    assert per_w * nw == M and nch * ch == per_w and nch % 2 == 0 and n_src % per_w == 0
    mesh = plsc.VectorSubcoreMesh(core_axis_name="c", subcore_axis_name="s")

    @functools.partial(
        pl.kernel, mesh=mesh, out_type=jax.ShapeDtypeStruct((n_out, width), src.dtype),
        scratch_types=[pltpu.VMEM((per_w,), jnp.int32),
                       pltpu.VMEM((ch, width), src.dtype), pltpu.VMEM((ch, width), src.dtype),
                       pltpu.SemaphoreType.DMA, pltpu.SemaphoreType.DMA,
                       pltpu.SemaphoreType.DMA, pltpu.SemaphoreType.DMA],
        name="sc_scatter" if scatter else "sc_gather")
    def move(src_hbm, idx_hbm, out_hbm, idx_v, buf0, buf1, in0, in1, out0, out1):
        wid = lax.axis_index("s") * nc + lax.axis_index("c")
        base = wid * per_w
        pltpu.sync_copy(idx_hbm.at[pl.ds(base, per_w)], idx_v)
        lin_base = lax.rem(base, n_src) if scatter else base

        def read(c, buf, sem):
            if scatter:
                return pltpu.make_async_copy(src_hbm.at[pl.ds(lin_base + c * ch, ch)], buf, sem)
            return pltpu.make_async_copy(src_hbm.at[idx_v.at[pl.ds(c * ch, ch)]], buf, sem)

        def write(c, buf, sem):
            if scatter:
                return pltpu.make_async_copy(buf, out_hbm.at[idx_v.at[pl.ds(c * ch, ch)]], sem)
            return pltpu.make_async_copy(buf, out_hbm.at[pl.ds(lin_base + c * ch, ch)], sem)

        read(0, buf0, in0).start()

        @pl.loop(0, nch, step=2)
        def _(c):
            @pl.when(c > 0)
            def _():
                write(c - 1, buf1, out1).wait()
            read(c + 1, buf1, in1).start()
            read(c, buf0, in0).wait()
            write(c, buf0, out0).start()
            read(c + 1, buf1, in1).wait()
            write(c + 1, buf1, out1).start()
            write(c, buf0, out0).wait()

            @pl.when(c + 2 < nch)
            def _():
                read(c + 2, buf0, in0).start()

        write(nch - 1, buf1, out1).wait()

    return move(src, idx)


def _final_kernel(x1_ref, y1_ref, y2_ref, meta_ref, nw_ref, o_ref):
    w1 = meta_ref[:, 4:5]
    w2 = meta_ref[:, 5:6]
    x2 = x1_ref[...] + (w1 * _unpack_rows(y1_ref[...]) + w2 * _unpack_rows(y2_ref[...]))
    o_ref[...] = _rms(x2, nw_ref[...])


def _final(x1, yg, meta, norm_w):
    T = x1.shape[0]
    tm = COMB_TILE
    nt = T // tm
    row = lambda i: (i, 0)
    return pl.pallas_call(
        _final_kernel,
        grid=(nt,),
        in_specs=[pl.BlockSpec((tm, D_MODEL), row),
                  pl.BlockSpec((tm, PACKED), row),
                  pl.BlockSpec((tm, PACKED), lambda i: (i + nt, 0)),
                  pl.BlockSpec((tm, LANES), row),
                  pl.BlockSpec((1, D_MODEL), lambda i: (0, 0))],
        out_specs=pl.BlockSpec((tm, D_MODEL), row),
        out_shape=jax.ShapeDtypeStruct((T, D_MODEL), F32),
        compiler_params=pltpu.CompilerParams(dimension_semantics=("arbitrary",),
                                             vmem_limit_bytes=VMEM_LIMIT),
        name="final",
    )(x1, yg, yg, meta, norm_w)


def _layer(x, norm_mix_w, w_in, conv_w, conv_b, b_igate, b_fgate, attn_sinks, mlstm_norm_w,
           w_attn_o, w_mlstm_o, w_out, norm_ffn_w, w_group, b_group, w_router, b_router,
           w_gate, w_up, w_down, out_norm_w):
    B, S, D = x.shape
    T = B * S
    x2 = x.reshape(T, D)

    w_tok = jnp.concatenate([w_in[:, _O_AK:_O_AV], w_in[:, _O_MK:_O_MV]], axis=1).astype(BF16)
    w_chan = jnp.concatenate([w_in[:, _O_AQ:_O_AK] * (HEAD_DIM ** -0.5), w_in[:, _O_AV:_O_MQ],
                              w_in[:, _O_MQ:_O_MK], w_in[:, _O_MV:_O_MI]], axis=1).T.astype(BF16)
    w_gates = jnp.pad(w_in[:, _O_MI:_O_GA], ((0, 0), (0, LANES - 2 * M_HEADS))).astype(BF16)
    expert_ws = [w.reshape(-1, w.shape[-1]) for w in (w_gate, w_up, w_down)]
    ak, mk, aqt, avt, mqt, mvt, mot, grow, wg16, wu16, wd16 = _inproj(
        x2, norm_mix_w.reshape(1, D), w_tok, w_chan, w_gates, expert_ws, B, S)

    ya = _attn(attn_sinks.astype(F32), aqt, ak.reshape(B, S, -1), avt)

    bias = jnp.concatenate([b_igate, b_fgate]).astype(F32)
    ym = _mlstm(mqt, mk.reshape(B, S, -1), mvt, mot, grow, conv_w, conv_b,
                bias.reshape(2 * M_HEADS, 1), mlstm_norm_w)

    gpad = jnp.zeros((8 - N_GROUPS, D), F32)
    w_rt = jnp.concatenate([w_group.T, gpad, w_router.T, jnp.zeros((LANES - ROUTER_ROWS, D), F32)], axis=0)
    w_rt_hi = w_rt.astype(BF16)
    w_rt = jnp.concatenate([w_rt_hi, (w_rt - w_rt_hi.astype(F32)).astype(BF16)], axis=0)
    b_rt = jnp.concatenate([b_group, jnp.zeros((8 - N_GROUPS,), F32), b_router]).astype(F32)
    b_rt = jnp.broadcast_to(b_rt[:, None], (ROUTER_ROWS, TOK_TILE))
    x1, h2p, meta, code_rows, cnt = _outproj(x2, ya.reshape(T, -1), ym.reshape(T, -1),
                                            norm_mix_w.reshape(1, D), w_in[:, _O_GA:].astype(BF16),
                                            w_attn_o.astype(BF16), w_mlstm_o.astype(BF16),
                                            w_out.astype(BF16), norm_ffn_w.reshape(1, D), w_rt, b_rt)

    counts = cnt[:, 0].astype(jnp.int32)
    nblk = (2 * T) // SLOT_BLOCK + N_EXPERTS
    pstart, blk_e, nvalid, nused = _slots(counts, nblk)
    dest = _dests(pstart, code_rows)[0:2].reshape(2 * T)

    xs = _sc_move_rows(h2p, dest, nblk * SLOT_BLOCK, scatter=True)
    ys = _experts(blk_e, nvalid, nused, xs, wg16.reshape(w_gate.shape), wu16.reshape(w_up.shape),
                  wd16.reshape(w_down.shape))
    yg = _sc_move_rows(ys, dest, 2 * T, scatter=False)
    out = _final(x1, yg, meta, out_norm_w.reshape(1, D))
    return out.reshape(B, S, D)


def kernel(x, norm_mix_w, w_in, conv_w, conv_b, b_igate, b_fgate, attn_sinks, mlstm_norm_w, w_attn_o,
           w_mlstm_o, w_out, norm_ffn_w, w_group, b_group, w_router, b_router, w_gate, w_up, w_down,
           norm_final_w):
    depth = w_in.shape[0]
    assert depth == 1, "final RMSNorm is fused into the last layer's combine kernel"
    return _layer(x, norm_mix_w[0], w_in[0], conv_w[0], conv_b[0], b_igate[0], b_fgate[0],
                  attn_sinks[0], mlstm_norm_w[0], w_attn_o[0], w_mlstm_o[0], w_out[0], norm_ffn_w[0],
                  w_group[0], b_group[0], w_router[0], b_router[0], w_gate[0], w_up[0], w_down[0],
                  norm_final_w)
```

```python
import functools

import jax
import jax.numpy as jnp
from jax import lax
from jax.experimental import pallas as pl
from jax.experimental.pallas import tpu as pltpu
from jax.experimental.pallas import tpu_sc as plsc

F32 = jnp.float32
BF16 = jnp.bfloat16

D_MODEL = 1024
N_Q_HEADS = 8
N_KV_HEADS = 2
HEAD_DIM = 64
WINDOW = 128
GQA_GROUP = N_Q_HEADS // N_KV_HEADS
M_HEADS = 4
M_QK_DIM = 64
M_V_DIM = 128
CONV_WIDTH = 4
N_GROUPS = 4
EXPERTS_PER_GROUP = 8
N_EXPERTS = N_GROUPS * EXPERTS_PER_GROUP
D_EXPERT = 512
EPS = 1e-6

ATT_Q_WIDTH = N_Q_HEADS * HEAD_DIM
ATT_KV_WIDTH = N_KV_HEADS * HEAD_DIM
M_QK_WIDTH = M_HEADS * M_QK_DIM
M_V_WIDTH = M_HEADS * M_V_DIM

LANES = 128
NEG = -1e30
VMEM_LIMIT = 56 * 1024 * 1024

TOK_TILE = 512
ATT_TILE = 512
M_CHUNK = 128
M_BATCH = 8
M_STATE_ROWS = M_V_DIM + 16
SLOT_BLOCK = 512
COMB_TILE = 512
RANK_RADIX = 65536
RANK_BITS = 16
ROUTER_ROWS = 8 + N_EXPERTS
PACKED = D_MODEL // 2
SC_CHUNK = 64

_O_AQ = 0
_O_AK = _O_AQ + ATT_Q_WIDTH
_O_AV = _O_AK + ATT_KV_WIDTH
_O_MQ = _O_AV + ATT_KV_WIDTH
_O_MK = _O_MQ + M_QK_WIDTH
_O_MV = _O_MK + M_QK_WIDTH
_O_MO = _O_MV + M_V_WIDTH
_O_MI = _O_MO + M_V_WIDTH
_O_MF = _O_MI + M_HEADS
_O_GA = _O_MF + M_HEADS
_O_GB = _O_GA + D_MODEL
_O_END = _O_GB + D_MODEL


def _rms(x, w):
    return x * lax.rsqrt(jnp.mean(x * x, axis=-1, keepdims=True) + EPS) * w


def _sigmoid(x):
    return 0.5 * jnp.tanh(0.5 * x) + 0.5


def _pack_rows(x):
    half = x.shape[1] // 2
    bits = lax.bitcast_convert_type(x.astype(BF16).astype(F32), jnp.uint32)
    packed = (bits[:, :half] >> 16) | (bits[:, half:] & jnp.uint32(0xFFFF0000))
    return lax.bitcast_convert_type(packed, jnp.int32)


def _unpack_rows(w):
    u = lax.bitcast_convert_type(w, jnp.uint32)
    lo = lax.bitcast_convert_type(u << 16, F32)
    hi = lax.bitcast_convert_type(u & jnp.uint32(0xFFFF0000), F32)
    return jnp.concatenate([lo, hi], axis=-1)


def _log_sigmoid(x):
    return jnp.minimum(x, 0.0) - jnp.log1p(jnp.exp(-jnp.abs(x)))


_TOK_SPLITS = (ATT_KV_WIDTH, M_QK_WIDTH)
_CHAN_SPLITS = (ATT_Q_WIDTH, ATT_KV_WIDTH, M_QK_WIDTH, M_V_WIDTH, M_V_WIDTH)


def _inproj_kernel(x_ref, nw_ref, w_ref, wt_ref, wg_ref, eg_ref, eu_ref, ed_ref, *out_refs):
    tok_refs = out_refs[:len(_TOK_SPLITS)]
    chan_refs = out_refs[len(_TOK_SPLITS):len(_TOK_SPLITS) + len(_CHAN_SPLITS)]
    grow_ref = out_refs[len(_TOK_SPLITS) + len(_CHAN_SPLITS)]
    for src, dst in zip((eg_ref, eu_ref, ed_ref), out_refs[-3:]):
        dst[...] = src[...].astype(BF16)
    h = _rms(x_ref[...], nw_ref[...]).astype(BF16)
    lo = 0
    for ref, width in zip(tok_refs, _TOK_SPLITS):
        ref[...] = jnp.dot(h, w_ref[:, lo:lo + width], preferred_element_type=F32).astype(BF16)
        lo += width
    lo = 0
    for ref, width in zip(chan_refs, _CHAN_SPLITS):
        ref[0] = lax.dot_general(wt_ref[lo:lo + width, :], h, (((1,), (1,)), ((), ())),
                                 preferred_element_type=F32).astype(BF16)
        lo += width
    g = jnp.dot(h, wg_ref[...], preferred_element_type=F32)
    grow_ref[...] = g.T[0:8, :]


def _inproj(x2, norm_w, w_tok, w_chan, w_gates, expert_ws, B, S):
    T = x2.shape[0]
    tm = TOK_TILE
    tpb = S // tm
    nsteps = T // tm
    row = lambda i: (i, 0)
    const = lambda i: (0, 0)
    chan = lambda i: (i // tpb, 0, i % tpb)
    cast_specs = [pl.BlockSpec((w.shape[0] // nsteps, w.shape[1]), row) for w in expert_ws]
    assert all(w.shape[0] % (8 * nsteps) == 0 for w in expert_ws)
    return pl.pallas_call(
        _inproj_kernel,
        grid=(nsteps,),
        in_specs=[pl.BlockSpec((tm, D_MODEL), row),
                  pl.BlockSpec((1, D_MODEL), const),
                  pl.BlockSpec(w_tok.shape, const),
                  pl.BlockSpec(w_chan.shape, const),
                  pl.BlockSpec((D_MODEL, LANES), const),
                  *cast_specs],
        out_specs=[*[pl.BlockSpec((tm, w), row) for w in _TOK_SPLITS],
                   *[pl.BlockSpec((1, w, tm), chan) for w in _CHAN_SPLITS],
                   pl.BlockSpec((8, tm), lambda i: (0, i)),
                   *cast_specs],
        out_shape=[*[jax.ShapeDtypeStruct((T, w), BF16) for w in _TOK_SPLITS],
                   *[jax.ShapeDtypeStruct((B, w, S), BF16) for w in _CHAN_SPLITS],
                   jax.ShapeDtypeStruct((8, T), F32),
                   *[jax.ShapeDtypeStruct(w.shape, BF16) for w in expert_ws]],
        compiler_params=pltpu.CompilerParams(dimension_semantics=("arbitrary",),
                                             vmem_limit_bytes=VMEM_LIMIT),
        name="inproj",
    )(x2, norm_w, w_tok, w_chan, w_gates, *expert_ws)


def _attn_kernel(sink_ref, qt_ref, k_ref, kp_ref, vt_ref, vtp_ref, o_ref):
    i = pl.program_id(1)
    W = WINDOW
    G = GQA_GROUP
    u = lax.broadcasted_iota(jnp.int32, (W, W), 0)
    t = lax.broadcasted_iota(jnp.int32, (W, W), 1)
    from_prev = u > t

    def keys_values(j):
        if j == 0:
            return kp_ref[0], vtp_ref[0], k_ref[0, 0:W, :], vt_ref[0, :, 0:W]
        return (k_ref[0, (j - 1) * W:j * W, :], vt_ref[0, :, (j - 1) * W:j * W],
                k_ref[0, j * W:(j + 1) * W, :], vt_ref[0, :, j * W:(j + 1) * W])

    def scores(j, g):
        k_prev, _, k_cur, _ = keys_values(j)
        dims = slice(g * HEAD_DIM, (g + 1) * HEAD_DIM)
        qg = jnp.concatenate([qt_ref[0, h * HEAD_DIM:(h + 1) * HEAD_DIM, j * W:(j + 1) * W]
                              for h in range(g * G, (g + 1) * G)], axis=1)
        return (jnp.dot(k_prev[:, dims], qg, preferred_element_type=F32),
                jnp.dot(k_cur[:, dims], qg, preferred_element_type=F32))

    tasks = [(j, g) for j in range(ATT_TILE // W) for g in range(N_KV_HEADS)]
    ahead = scores(*tasks[0])
    pair_rows = []
    for n_task, (j, g) in enumerate(tasks):
        s_prev, s_cur = ahead
        if n_task + 1 < len(tasks):
            ahead = scores(*tasks[n_task + 1])
        cols = slice(j * W, (j + 1) * W)
        _, vt_prev, _, vt_cur = keys_values(j)
        dims = slice(g * HEAD_DIM, (g + 1) * HEAD_DIM)
        for n in range(G):
            hc = slice(n * W, (n + 1) * W)
            sp = s_prev[:, hc]
            if j == 0:
                sp = jnp.where(i > 0, sp, NEG)
            s = jnp.where(from_prev, sp, s_cur[:, hc])
            sink = sink_ref[g * G + n]
            m = jnp.maximum(jnp.max(s, axis=0, keepdims=True), sink)
            p = jnp.exp(s - m)
            inv = 1.0 / (jnp.sum(p, axis=0, keepdims=True) + jnp.exp(sink - m))
            p_prev = jnp.where(from_prev, p, 0.0).astype(BF16)
            p_cur = jnp.where(from_prev, 0.0, p).astype(BF16)
            o = jnp.dot(vt_prev[dims, :], p_prev, preferred_element_type=F32) \
                + jnp.dot(vt_cur[dims, :], p_cur, preferred_element_type=F32)
            pair_rows.append(o * inv)
        if g == N_KV_HEADS - 1:
            out = [jnp.concatenate(pair_rows[n:n + 2], axis=0).T for n in range(0, N_Q_HEADS, 2)]
            o_ref[0, cols, :] = jnp.concatenate(out, axis=1).astype(BF16)
            pair_rows = []


def _attn(sinks, qt, k3, vt):
    B, S, _ = k3.shape
    tq = ATT_TILE
    per = tq // WINDOW
    chan = lambda b, i: (b, 0, i)
    tok = lambda b, i: (b, i, 0)
    return pl.pallas_call(
        _attn_kernel,
        grid=(B, S // tq),
        in_specs=[pl.BlockSpec(memory_space=pltpu.SMEM),
                  pl.BlockSpec((1, ATT_Q_WIDTH, tq), chan),
                  pl.BlockSpec((1, tq, ATT_KV_WIDTH), tok),
                  pl.BlockSpec((1, WINDOW, ATT_KV_WIDTH), lambda b, i: (b, jnp.maximum(i * per - 1, 0), 0)),
                  pl.BlockSpec((1, ATT_KV_WIDTH, tq), chan),
                  pl.BlockSpec((1, ATT_KV_WIDTH, WINDOW), lambda b, i: (b, 0, jnp.maximum(i * per - 1, 0)))],
        out_specs=pl.BlockSpec((1, tq, ATT_Q_WIDTH), tok),
        out_shape=jax.ShapeDtypeStruct((B, S, ATT_Q_WIDTH), BF16),
        compiler_params=pltpu.CompilerParams(dimension_semantics=("arbitrary", "arbitrary"),
                                             vmem_limit_bytes=VMEM_LIMIT),
        name="attn",
    )(sinks, qt, k3, k3, vt, vt)


def _mlstm_kernel(nb, *refs):
    L = M_CHUNK
    H = M_HEADS
    mqt_ref, mk_ref, mvt_ref, mot_ref = refs[:4]
    grow_refs = refs[4:4 + nb]
    (cwq_ref, cbq_ref, cwk_ref, cbk_ref, brow_ref, nw_ref, y_ref,
     state_ref, m_ref, prevq_ref, ubuf_ref) = refs[4 + nb:]
    cidx = pl.program_id(1)

    @pl.when(cidx == 0)
    def _():
        state_ref[...] = jnp.zeros_like(state_ref)
        m_ref[...] = jnp.zeros_like(m_ref)
        prevq_ref[...] = jnp.zeros_like(prevq_ref)
        ubuf_ref[:, 0:8, :] = jnp.zeros((nb, 8, M_QK_WIDTH), F32)

    ri = lax.broadcasted_iota(jnp.int32, (L, L), 0)
    ci = lax.broadcasted_iota(jnp.int32, (L, L), 1)
    causal_t = ri <= ci
    triu = jnp.where(causal_t, 1.0, 0.0).astype(F32)
    lane = lax.broadcasted_iota(jnp.int32, (8, L), 1)
    r2 = lax.broadcasted_iota(jnp.int32, (2 * L, L), 0)
    c2 = lax.broadcasted_iota(jnp.int32, (2 * L, L), 1)
    shifts = [jnp.where(r2 - c2 == L - k, 1.0, 0.0).astype(BF16) for k in range(1, CONV_WIDTH)]
    ones_rows = jnp.where(lax.broadcasted_iota(jnp.int32, (M_STATE_ROWS - M_V_DIM, L), 0) == 0,
                          1.0, 0.0).astype(BF16)

    pairs = [(bb, h) for bb in range(nb) for h in range(H)]
    states = [state_ref[bb * H + h] for bb, h in pairs]
    m_prevs = [m_ref[bb] for bb in range(nb)]

    def prepare(bb):
        cur = mqt_ref[bb]
        both = jnp.concatenate([prevq_ref[bb], cur], axis=1)
        acc = cbq_ref[...] + cwq_ref[CONV_WIDTH - 1] * cur.astype(F32)
        for k in range(1, CONV_WIDTH):
            acc = acc + cwq_ref[CONV_WIDTH - 1 - k] * jnp.dot(both, shifts[k - 1],
                                                              preferred_element_type=F32)
        qt = (acc * _sigmoid(acc)).astype(BF16)
        ubuf_ref[bb, 8:L + 8, :] = mk_ref[bb].astype(F32)
        acc = cbk_ref[...] + cwk_ref[CONV_WIDTH - 1:CONV_WIDTH, :] * ubuf_ref[bb, 8:L + 8, :]
        for j in range(CONV_WIDTH - 1):
            off = 8 - (CONV_WIDTH - 1) + j
            acc = acc + cwk_ref[j:j + 1, :] * ubuf_ref[bb, off:off + L, :]
        ubuf_ref[bb, 0:8, :] = ubuf_ref[bb, L:L + 8, :]
        kk = (acc * _sigmoid(acc) * (M_QK_DIM ** -0.5)).astype(BF16)

        gr = grow_refs[bb][...] + brow_ref[...]
        b = jnp.dot(_log_sigmoid(gr), triu, preferred_element_type=F32,
                    precision=lax.Precision.HIGHEST)
        gi = pltpu.roll(gr, 4, axis=0)
        u = gi - b
        cm = u
        for sh in (1, 2, 4, 8, 16, 32, 64):
            if sh < L:
                cm = jnp.maximum(cm, jnp.where(lane >= sh, pltpu.roll(cm, sh, axis=1), NEG))
        m_prev = m_prevs[bb]
        mt = b + jnp.maximum(m_prev, cm)
        wa = jnp.exp(b + m_prev - mt)
        emt = jnp.exp(-mt)
        m_new = jnp.broadcast_to(mt[:, L - 1:L], (8, L))
        b_last = jnp.broadcast_to(b[:, L - 1:L], (8, L))
        wc = jnp.exp(b_last + m_prev - m_new)
        ws = jnp.exp(b_last + u - m_new)
        return qt, kk, b - mt, u.T, wa, emt, wc, ws, m_new

    seqs = [prepare(bb) for bb in range(nb)]

    def products(bb, h, state):
        qt, kk = seqs[bb][0], seqs[bb][1]
        qh = qt[h * M_QK_DIM:(h + 1) * M_QK_DIM, :]
        kh = kk[:, h * M_QK_DIM:(h + 1) * M_QK_DIM]
        return (kh, jnp.dot(kh, qh, preferred_element_type=F32),
                jnp.dot(state.astype(BF16), qh, preferred_element_type=F32))

    new_states, outs = [], []
    ahead = products(*pairs[0], states[0])
    for n, ((bb, h), state) in enumerate(zip(pairs, states)):
        kh, st, inter = ahead
        if n + 1 < len(pairs):
            ahead = products(*pairs[n + 1], states[n + 1])
        _, _, bmt, ucol, wa, emt, wc, ws, _ = seqs[bb]
        row = slice(H + h, H + h + 1)
        wq = jnp.exp(jnp.where(causal_t, bmt[row, :] + ucol[:, H + h:H + h + 1], NEG))
        sw = (st * wq).astype(BF16)
        vext = jnp.concatenate([mvt_ref[bb, h * M_V_DIM:(h + 1) * M_V_DIM, :], ones_rows], axis=0)
        res = wa[row, :] * inter + jnp.dot(vext, sw, preferred_element_type=F32)
        num = res[:M_V_DIM, :]
        den = res[M_V_DIM:M_V_DIM + 1, :]
        hb = num / jnp.maximum(jnp.abs(den), emt[row, :])
        vw = (vext.astype(F32) * ws[row, :]).astype(BF16)
        new_states.append(wc[row, 0:1] * state + jnp.dot(vw, kh, preferred_element_type=F32))
        hn = hb * lax.rsqrt(jnp.mean(hb * hb, axis=0, keepdims=True) + EPS)
        hn = hn * nw_ref[h * M_V_DIM:(h + 1) * M_V_DIM, :]
        og = _sigmoid(mot_ref[bb, h * M_V_DIM:(h + 1) * M_V_DIM, :].astype(F32))
        outs.append((og * hn).T.astype(BF16))

    for i, (bb, h) in enumerate(pairs):
        state_ref[bb * H + h] = new_states[i]
    for bb in range(nb):
        m_ref[bb] = seqs[bb][8]
        prevq_ref[bb] = mqt_ref[bb]
        y_ref[bb] = jnp.concatenate(outs[bb * H:(bb + 1) * H], axis=1)


def _mlstm(mqt, mk3, mvt, mot, grow, conv_w, conv_b, bias_row, norm_w):
    B, S, _ = mk3.shape
    L = M_CHUNK
    nb = max(d for d in range(1, M_BATCH + 1) if B % d == 0)
    nc = S // L
    assert L == LANES, "per-head scalars are kept lane-replicated next to (8, L) gate rows"
    tok = lambda b, c: (b, c, 0)
    chan = lambda b, c: (b, 0, c)
    const2 = lambda b, c: (0, 0)
    const3 = lambda b, c: (0, 0, 0)
    grow_specs = [pl.BlockSpec((8, L), functools.partial(lambda b, c, n: (0, (b * nb + n) * nc + c), n=n))
                  for n in range(nb)]
    rep = lambda v: jnp.broadcast_to(v.astype(F32)[..., None], v.shape + (L,))
    cwq, cbq = rep(conv_w[:, :M_QK_WIDTH]), rep(conv_b[:M_QK_WIDTH])
    cwk, cbk = conv_w[:, M_QK_WIDTH:].astype(F32), conv_b[M_QK_WIDTH:].reshape(1, -1).astype(F32)
    return pl.pallas_call(
        functools.partial(_mlstm_kernel, nb),
        grid=(B // nb, nc),
        in_specs=[pl.BlockSpec((nb, M_QK_WIDTH, L), chan),
                  pl.BlockSpec((nb, L, M_QK_WIDTH), tok),
                  pl.BlockSpec((nb, M_V_WIDTH, L), chan),
                  pl.BlockSpec((nb, M_V_WIDTH, L), chan),
                  *grow_specs,
                  pl.BlockSpec((CONV_WIDTH, M_QK_WIDTH, L), const3),
                  pl.BlockSpec((M_QK_WIDTH, L), const2),
                  pl.BlockSpec((CONV_WIDTH, M_QK_WIDTH), const2),
                  pl.BlockSpec((1, M_QK_WIDTH), const2),
                  pl.BlockSpec((8, 1), const2),
                  pl.BlockSpec((M_V_WIDTH, L), const2)],
        out_specs=pl.BlockSpec((nb, L, M_V_WIDTH), tok),
        out_shape=jax.ShapeDtypeStruct((B, S, M_V_WIDTH), BF16),
        scratch_shapes=[pltpu.VMEM((nb * M_HEADS, M_STATE_ROWS, M_QK_DIM), F32),
                        pltpu.VMEM((nb, 8, LANES), F32),
                        pltpu.VMEM((nb, M_QK_WIDTH, L), BF16),
                        pltpu.VMEM((nb, L + 8, M_QK_WIDTH), F32)],
        compiler_params=pltpu.CompilerParams(dimension_semantics=("arbitrary", "arbitrary"),
                                             vmem_limit_bytes=VMEM_LIMIT),
        name="mlstm",
    )(mqt, mk3, mvt, mot, *([grow] * nb), cwq, cbq, cwk, cbk, bias_row, rep(norm_w))


def _outproj_kernel(x_ref, ya_ref, ym_ref, nmix_ref, wgab_ref, wa_ref, wm_ref, wo_ref, nw_ref, wr_ref, br_ref,
                    x1_ref, h2_ref, meta_ref, code_ref, cnt_ref, cnt_scr):
    tm = TOK_TILE
    i = pl.program_id(0)

    @pl.when(i == 0)
    def _():
        cnt_scr[...] = jnp.zeros_like(cnt_scr)

    pa = jnp.dot(ya_ref[...], wa_ref[...], preferred_element_type=F32)
    pm = jnp.dot(ym_ref[...], wm_ref[...], preferred_element_type=F32)
    x = x_ref[...]
    h1 = _rms(x, nmix_ref[...]).astype(BF16)
    ga = jnp.dot(h1, wgab_ref[:, :D_MODEL], preferred_element_type=F32)
    gb = jnp.dot(h1, wgab_ref[:, D_MODEL:], preferred_element_type=F32)
    mix = _sigmoid(ga) * pa + _sigmoid(gb) * pm
    x1 = x + jnp.dot(mix.astype(BF16), wo_ref[...], preferred_element_type=F32)
    x1_ref[...] = x1
    h2 = _rms(x1, nw_ref[...])
    h2_ref[...] = _pack_rows(h2)

    nt = (((1,), (1,)), ((), ()))
    h_hi = h2.astype(BF16)
    h_lo = (h2 - h_hi.astype(F32)).astype(BF16)
    hw = lax.dot_general(wr_ref[...], h_hi, nt, preferred_element_type=F32)
    logits = (hw[:LANES] + hw[LANES:]
              + lax.dot_general(wr_ref[:LANES, :], h_lo, nt, preferred_element_type=F32))[:ROUTER_ROWS]
    logits = logits + br_ref[...]
    row8 = lax.broadcasted_iota(jnp.int32, (8, tm), 0)

    def first_argmax(v):
        mx = jnp.max(v, axis=0, keepdims=True)
        idx = jnp.min(jnp.where(v == mx, row8, 8), axis=0, keepdims=True)
        return mx, idx

    gl = jnp.where(row8 < N_GROUPS, logits[0:8], NEG)
    gmax, gi = first_argmax(gl)
    gp = 1.0 / jnp.sum(jnp.exp(gl - gmax), axis=0, keepdims=True)
    el = jnp.zeros((8, tm), F32)
    for g in range(N_GROUPS):
        el = jnp.where(gi == g, logits[8 + 8 * g:16 + 8 * g], el)
    v1, j1 = first_argmax(el)
    v2, j2 = first_argmax(jnp.where(row8 == j1, NEG, el))
    t = jnp.exp(v2 - v1)
    w1 = gp / (1.0 + t)
    w2 = gp * t / (1.0 + t)
    e1 = gi * EXPERTS_PER_GROUP + j1
    e2 = gi * EXPERTS_PER_GROUP + j2

    erow = lax.broadcasted_iota(jnp.int32, (N_EXPERTS, tm), 0)
    hit1 = erow == e1
    hit2 = erow == e2
    onehot = jnp.where(hit1 | hit2, 1.0, 0.0)
    ri = lax.broadcasted_iota(jnp.int32, (tm, tm), 0)
    ci = lax.broadcasted_iota(jnp.int32, (tm, tm), 1)
    before = jnp.where(ri < ci, 1.0, 0.0).astype(BF16)
    rank = jnp.dot(onehot.astype(BF16), before, preferred_element_type=F32) + cnt_scr[...]
    r1 = jnp.sum(jnp.where(hit1, rank, 0.0), axis=0, keepdims=True)
    r2 = jnp.sum(jnp.where(hit2, rank, 0.0), axis=0, keepdims=True)
    cnt = jnp.broadcast_to((rank + onehot)[:, tm - 1:tm], (N_EXPERTS, tm))
    cnt_scr[...] = cnt
    cnt_ref[...] = cnt[:, :LANES]

    c1 = e1 * RANK_RADIX + r1.astype(jnp.int32)
    c2 = e2 * RANK_RADIX + r2.astype(jnp.int32)
    code_ref[...] = jnp.where(row8 == 0, c1, jnp.where(row8 == 1, c2, 0))
    wrow = lax.broadcasted_iota(jnp.int32, (LANES, tm), 0)
    meta_ref[...] = jnp.where(wrow == 4, w1, jnp.where(wrow == 5, w2, 0.0)).T


def _outproj(x2, ya, ym, norm_mix, w_gab, wa, wm, wo, norm_w, w_rt, b_rt):
    T = x2.shape[0]
    tm = TOK_TILE
    row = lambda i: (i, 0)
    const = lambda i: (0, 0)
    return pl.pallas_call(
        _outproj_kernel,
        grid=(T // tm,),
        in_specs=[pl.BlockSpec((tm, D_MODEL), row),
                  pl.BlockSpec((tm, ATT_Q_WIDTH), row),
                  pl.BlockSpec((tm, M_V_WIDTH), row),
                  pl.BlockSpec((1, D_MODEL), const),
                  pl.BlockSpec((D_MODEL, 2 * D_MODEL), const),
                  pl.BlockSpec((ATT_Q_WIDTH, D_MODEL), const),
                  pl.BlockSpec((M_V_WIDTH, D_MODEL), const),
                  pl.BlockSpec((D_MODEL, D_MODEL), const),
                  pl.BlockSpec((1, D_MODEL), const),
                  pl.BlockSpec((2 * LANES, D_MODEL), const),
                  pl.BlockSpec((ROUTER_ROWS, tm), const)],
        out_specs=[pl.BlockSpec((tm, D_MODEL), row),
                   pl.BlockSpec((tm, PACKED), row),
                   pl.BlockSpec((tm, LANES), row),
                   pl.BlockSpec((8, tm), lambda i: (0, i)),
                   pl.BlockSpec((N_EXPERTS, LANES), const)],
        out_shape=[jax.ShapeDtypeStruct((T, D_MODEL), F32),
                   jax.ShapeDtypeStruct((T, PACKED), jnp.int32),
                   jax.ShapeDtypeStruct((T, LANES), F32),
                   jax.ShapeDtypeStruct((8, T), jnp.int32),
                   jax.ShapeDtypeStruct((N_EXPERTS, LANES), F32)],
        scratch_shapes=[pltpu.VMEM((N_EXPERTS, tm), F32)],
        compiler_params=pltpu.CompilerParams(dimension_semantics=("arbitrary",),
                                             vmem_limit_bytes=VMEM_LIMIT),
        name="outproj",
    )(x2, ya, ym, norm_mix, w_gab, wa, wm, wo, norm_w, w_rt, b_rt)


def _slots_kernel(cnt_ref, pstart_ref, blk_e_ref, nvalid_ref, nused_ref):
    R = SLOT_BLOCK
    nblk = blk_e_ref.shape[0]

    def per_expert(e, run):
        c = cnt_ref[e]
        end = run + ((c + R - 1) // R) * R
        pstart_ref[e] = run

        def set_blk(b, carry):
            blk_e_ref[b] = e
            nvalid_ref[b] = jnp.minimum(run + c - b * R, R)
            return carry
        lax.fori_loop(run // R, end // R, set_blk, 0)
        return end

    total = lax.fori_loop(0, N_EXPERTS, per_expert, 0)
    nused_ref[0] = total // R

    def tail_blk(b, carry):
        blk_e_ref[b] = N_EXPERTS - 1
        nvalid_ref[b] = 0
        return carry
    lax.fori_loop(total // R, nblk, tail_blk, 0)


def _slots(counts, nblk):
    smem = pl.BlockSpec(memory_space=pltpu.SMEM)
    return pl.pallas_call(
        _slots_kernel,
        in_specs=[smem],
        out_specs=[smem, smem, smem, smem],
        out_shape=[jax.ShapeDtypeStruct((N_EXPERTS,), jnp.int32),
                   jax.ShapeDtypeStruct((nblk,), jnp.int32),
                   jax.ShapeDtypeStruct((nblk,), jnp.int32),
                   jax.ShapeDtypeStruct((1,), jnp.int32)],
        name="slots",
    )(counts)


def _experts_kernel(blk_e_ref, nvalid_ref, nused_ref, xs_ref, wg_ref, wu_ref, wd_ref, ys_ref):
    j = pl.program_id(0)
    R = SLOT_BLOCK
    valid = j < nused_ref[0]

    @pl.when(valid)
    def _():
        rows = lax.broadcasted_iota(jnp.int32, (R, PACKED), 0)
        xw = jnp.where(rows < nvalid_ref[j], xs_ref[...], 0)
        xb = _unpack_rows(xw).astype(BF16)
        g = jnp.dot(xb, wg_ref[0], preferred_element_type=F32)
        u = jnp.dot(xb, wu_ref[0], preferred_element_type=F32)
        act = (g * _sigmoid(g) * u).astype(BF16)
        ys_ref[...] = _pack_rows(jnp.dot(act, wd_ref[0], preferred_element_type=F32))

    @pl.when(jnp.logical_not(valid))
    def _():
        ys_ref[...] = jnp.zeros_like(ys_ref)


def _experts(blk_e, nvalid, nused, xs, w_gate, w_up, w_down):
    nblk = blk_e.shape[0]
    R = SLOT_BLOCK
    wspec = lambda shape: pl.BlockSpec((1,) + shape, lambda j, be, nv, nu: (be[j], 0, 0))
    grid_spec = pltpu.PrefetchScalarGridSpec(
        num_scalar_prefetch=3,
        grid=(nblk,),
        in_specs=[pl.BlockSpec((R, PACKED), lambda j, be, nv, nu: (jnp.minimum(j, nu[0] - 1), 0)),
                  wspec((D_MODEL, D_EXPERT)),
                  wspec((D_MODEL, D_EXPERT)),
                  wspec((D_EXPERT, D_MODEL))],
        out_specs=pl.BlockSpec((R, PACKED), lambda j, be, nv, nu: (j, 0)),
    )
    return pl.pallas_call(
        _experts_kernel,
        grid_spec=grid_spec,
        out_shape=jax.ShapeDtypeStruct((nblk * R, PACKED), jnp.int32),
        compiler_params=pltpu.CompilerParams(dimension_semantics=("arbitrary",),
                                             vmem_limit_bytes=VMEM_LIMIT),
        name="experts",
    )(blk_e, nvalid, nused, xs, w_gate, w_up, w_down)


def _dest_kernel(pstart_ref, code_ref, dest_ref):
    c = code_ref[...]
    e = c >> RANK_BITS
    base = jnp.zeros_like(c)
    for k in range(N_EXPERTS):
        base = jnp.where(e == k, pstart_ref[k], base)
    dest_ref[...] = base + (c & (RANK_RADIX - 1))


def _dests(pstart, code_rows):
    T = code_rows.shape[1]
    tc = min(4096, T)
    return pl.pallas_call(
        _dest_kernel,
        grid=(T // tc,),
        in_specs=[pl.BlockSpec(memory_space=pltpu.SMEM),
                  pl.BlockSpec((8, tc), lambda i: (0, i))],
        out_specs=pl.BlockSpec((8, tc), lambda i: (0, i)),
        out_shape=jax.ShapeDtypeStruct((8, T), jnp.int32),
        name="dests",
    )(pstart, code_rows)


def _sc_workers():
    info = plsc.get_sparse_core_info()
    return info.num_cores, info.num_cores * info.num_subcores


def _sc_scatter_rows(src, idx, n_out):
    n_src, width = src.shape
    nc, nw = _sc_workers()
    per_w = n_src // nw
    ch = SC_CHUNK
    nch = per_w // ch
    assert idx.shape[0] == 2 * n_src and per_w * nw == n_src and nch * ch == per_w and nch % 2 == 0
    mesh = plsc.VectorSubcoreMesh(core_axis_name="c", subcore_axis_name="s")
    dma = pltpu.SemaphoreType.DMA

    @functools.partial(
        pl.kernel, mesh=mesh, out_type=jax.ShapeDtypeStruct((n_out, width), src.dtype),
        scratch_types=[pltpu.VMEM((per_w,), jnp.int32), pltpu.VMEM((per_w,), jnp.int32),
                       pltpu.VMEM((ch, width), src.dtype), pltpu.VMEM((ch, width), src.dtype),
                       dma, dma, dma, dma, dma, dma],
        name="sc_scatter")
    def scatter(src_hbm, idx_hbm, out_hbm, idx_a, idx_b, buf0, buf1, in0, in1, out0a, out0b, out1a, out1b):
        base = (lax.axis_index("s") * nc + lax.axis_index("c")) * per_w
        pltpu.sync_copy(idx_hbm.at[pl.ds(base, per_w)], idx_a)
        pltpu.sync_copy(idx_hbm.at[pl.ds(n_src + base, per_w)], idx_b)

        def read(c, buf, sem):
            return pltpu.make_async_copy(src_hbm.at[pl.ds(base + c * ch, ch)], buf, sem)

        def writes(c, buf, sem_a, sem_b):
            rows = pl.ds(c * ch, ch)
            return (pltpu.make_async_copy(buf, out_hbm.at[idx_a.at[rows]], sem_a),
                    pltpu.make_async_copy(buf, out_hbm.at[idx_b.at[rows]], sem_b))

        def start(copies):
            for cp in copies:
                cp.start()

        def wait(copies):
            for cp in copies:
                cp.wait()

        read(0, buf0, in0).start()

        @pl.loop(0, nch, step=2)
        def _(c):
            @pl.when(c > 0)
            def _():
                wait(writes(c - 1, buf1, out1a, out1b))
            read(c + 1, buf1, in1).start()
            read(c, buf0, in0).wait()
            start(writes(c, buf0, out0a, out0b))
            read(c + 1, buf1, in1).wait()
            start(writes(c + 1, buf1, out1a, out1b))
            wait(writes(c, buf0, out0a, out0b))

            @pl.when(c + 2 < nch)
            def _():
                read(c + 2, buf0, in0).start()

        wait(writes(nch - 1, buf1, out1a, out1b))

    return scatter(src, idx)


def _sc_gather_rows(src, idx):
    M = idx.shape[0]
    width = src.shape[1]
    nc, nw = _sc_workers()
    per_w = M // nw
    ch = SC_CHUNK
    nch = per_w // ch
    assert per_w * nw == M and nch * ch == per_w and nch % 2 == 0
    mesh = plsc.VectorSubcoreMesh(core_axis_name="c", subcore_axis_name="s")
    dma = pltpu.SemaphoreType.DMA

    @functools.partial(
        pl.kernel, mesh=mesh, out_type=jax.ShapeDtypeStruct((M, width), src.dtype),
        scratch_types=[pltpu.VMEM((per_w,), jnp.int32),
                       pltpu.VMEM((ch, width), src.dtype), pltpu.VMEM((ch, width), src.dtype),
                       dma, dma, dma, dma],
        name="sc_gather")
    def gather(src_hbm, idx_hbm, out_hbm, idx_v, buf0, buf1, in0, in1, out0, out1):
        base = (lax.axis_index("s") * nc + lax.axis_index("c")) * per_w
        pltpu.sync_copy(idx_hbm.at[pl.ds(base, per_w)], idx_v)

        def read(c, buf, sem):
            return pltpu.make_async_copy(src_hbm.at[idx_v.at[pl.ds(c * ch, ch)]], buf, sem)

        def write(c, buf, sem):
            return pltpu.make_async_copy(buf, out_hbm.at[pl.ds(base + c * ch, ch)], sem)

        read(0, buf0, in0).start()

        @pl.loop(0, nch, step=2)
        def _(c):
            @pl.when(c > 0)
            def _():
                write(c - 1, buf1, out1).wait()
            read(c + 1, buf1, in1).start()
            read(c, buf0, in0).wait()
            write(c, buf0, out0).start()
            read(c + 1, buf1, in1).wait()
            write(c + 1, buf1, out1).start()
            write(c, buf0, out0).wait()

            @pl.when(c + 2 < nch)
            def _():
                read(c + 2, buf0, in0).start()

        write(nch - 1, buf1, out1).wait()

    return gather(src, idx)


def _final_kernel(x1_ref, y1_ref, y2_ref, meta_ref, nw_ref, o_ref):
    w1 = meta_ref[:, 4:5]
    w2 = meta_ref[:, 5:6]
    x2 = x1_ref[...] + (w1 * _unpack_rows(y1_ref[...]) + w2 * _unpack_rows(y2_ref[...]))
    o_ref[...] = _rms(x2, nw_ref[...])


def _final(x1, yg, meta, norm_w):
    T = x1.shape[0]
    tm = COMB_TILE
    nt = T // tm
    row = lambda i: (i, 0)
    return pl.pallas_call(
        _final_kernel,
        grid=(nt,),
        in_specs=[pl.BlockSpec((tm, D_MODEL), row),
                  pl.BlockSpec((tm, PACKED), row),
                  pl.BlockSpec((tm, PACKED), lambda i: (i + nt, 0)),
                  pl.BlockSpec((tm, LANES), row),
                  pl.BlockSpec((1, D_MODEL), lambda i: (0, 0))],
        out_specs=pl.BlockSpec((tm, D_MODEL), row),
        out_shape=jax.ShapeDtypeStruct((T, D_MODEL), F32),
        compiler_params=pltpu.CompilerParams(dimension_semantics=("arbitrary",),
                                             vmem_limit_bytes=VMEM_LIMIT),
        name="final",
    )(x1, yg, yg, meta, norm_w)


def _layer(x, norm_mix_w, w_in, conv_w, conv_b, b_igate, b_fgate, attn_sinks, mlstm_norm_w,
           w_attn_o, w_mlstm_o, w_out, norm_ffn_w, w_group, b_group, w_router, b_router,
           w_gate, w_up, w_down, out_norm_w):
    B, S, D = x.shape
    T = B * S
    x2 = x.reshape(T, D)

    w_tok = jnp.concatenate([w_in[:, _O_AK:_O_AV], w_in[:, _O_MK:_O_MV]], axis=1).astype(BF16)
    w_chan = jnp.concatenate([w_in[:, _O_AQ:_O_AK] * (HEAD_DIM ** -0.5), w_in[:, _O_AV:_O_MQ],
                              w_in[:, _O_MQ:_O_MK], w_in[:, _O_MV:_O_MI]], axis=1).T.astype(BF16)
    w_gates = jnp.pad(w_in[:, _O_MI:_O_GA], ((0, 0), (0, LANES - 2 * M_HEADS))).astype(BF16)
    expert_ws = [w.reshape(-1, w.shape[-1]) for w in (w_gate, w_up, w_down)]
    ak, mk, aqt, avt, mqt, mvt, mot, grow, wg16, wu16, wd16 = _inproj(
        x2, norm_mix_w.reshape(1, D), w_tok, w_chan, w_gates, expert_ws, B, S)

    ya = _attn(attn_sinks.astype(F32), aqt, ak.reshape(B, S, -1), avt)

    bias = jnp.concatenate([b_igate, b_fgate]).astype(F32)
    ym = _mlstm(mqt, mk.reshape(B, S, -1), mvt, mot, grow, conv_w, conv_b,
                bias.reshape(2 * M_HEADS, 1), mlstm_norm_w)

    gpad = jnp.zeros((8 - N_GROUPS, D), F32)
    w_rt = jnp.concatenate([w_group.T, gpad, w_router.T, jnp.zeros((LANES - ROUTER_ROWS, D), F32)], axis=0)
    w_rt_hi = w_rt.astype(BF16)
    w_rt = jnp.concatenate([w_rt_hi, (w_rt - w_rt_hi.astype(F32)).astype(BF16)], axis=0)
    b_rt = jnp.concatenate([b_group, jnp.zeros((8 - N_GROUPS,), F32), b_router]).astype(F32)
    b_rt = jnp.broadcast_to(b_rt[:, None], (ROUTER_ROWS, TOK_TILE))
    x1, h2p, meta, code_rows, cnt = _outproj(x2, ya.reshape(T, -1), ym.reshape(T, -1),
                                            norm_mix_w.reshape(1, D), w_in[:, _O_GA:].astype(BF16),
                                            w_attn_o.astype(BF16), w_mlstm_o.astype(BF16),
                                            w_out.astype(BF16), norm_ffn_w.reshape(1, D), w_rt, b_rt)

    counts = cnt[:, 0].astype(jnp.int32)
    nblk = (2 * T) // SLOT_BLOCK + N_EXPERTS
    pstart, blk_e, nvalid, nused = _slots(counts, nblk)
    dest = _dests(pstart, code_rows)[0:2].reshape(2 * T)

    xs = _sc_scatter_rows(h2p, dest, nblk * SLOT_BLOCK)
    ys = _experts(blk_e, nvalid, nused, xs, wg16.reshape(w_gate.shape), wu16.reshape(w_up.shape),
                  wd16.reshape(w_down.shape))
    yg = _sc_gather_rows(ys, dest)
    out = _final(x1, yg, meta, out_norm_w.reshape(1, D))
    return out.reshape(B, S, D)


def kernel(x, norm_mix_w, w_in, conv_w, conv_b, b_igate, b_fgate, attn_sinks, mlstm_norm_w, w_attn_o,
           w_mlstm_o, w_out, norm_ffn_w, w_group, b_group, w_router, b_router, w_gate, w_up, w_down,
           norm_final_w):
    depth = w_in.shape[0]
    assert depth == 1, "final RMSNorm is fused into the last layer's combine kernel"
    return _layer(x, norm_mix_w[0], w_in[0], conv_w[0], conv_b[0], b_igate[0], b_fgate[0],
                  attn_sinks[0], mlstm_norm_w[0], w_attn_o[0], w_mlstm_o[0], w_out[0], norm_ffn_w[0],
                  w_group[0], b_group[0], w_router[0], b_router[0], w_gate[0], w_up[0], w_down[0],
                  norm_final_w)
```

```python
import functools

import jax
import jax.numpy as jnp
from jax import lax
from jax.experimental import pallas as pl
from jax.experimental.pallas import tpu as pltpu
from jax.experimental.pallas import tpu_sc as plsc

F32 = jnp.float32
BF16 = jnp.bfloat16

D_MODEL = 1024
N_Q_HEADS = 8
N_KV_HEADS = 2
HEAD_DIM = 64
WINDOW = 128
GQA_GROUP = N_Q_HEADS // N_KV_HEADS
M_HEADS = 4
M_QK_DIM = 64
M_V_DIM = 128
CONV_WIDTH = 4
N_GROUPS = 4
EXPERTS_PER_GROUP = 8
N_EXPERTS = N_GROUPS * EXPERTS_PER_GROUP
D_EXPERT = 512
EPS = 1e-6

ATT_Q_WIDTH = N_Q_HEADS * HEAD_DIM
ATT_KV_WIDTH = N_KV_HEADS * HEAD_DIM
M_QK_WIDTH = M_HEADS * M_QK_DIM
M_V_WIDTH = M_HEADS * M_V_DIM

LANES = 128
NEG = -1e30
VMEM_LIMIT = 56 * 1024 * 1024

IN_TILE = 1024
TOK_TILE = 512
ATT_TILE = 512
M_CHUNK = 128
M_BATCH = 8
M_STATE_ROWS = M_V_DIM + 16
SLOT_BLOCK = 512
COMB_TILE = 512
RANK_RADIX = 65536
RANK_BITS = 16
ROUTER_ROWS = 8 + N_EXPERTS
PACKED = D_MODEL // 2
SC_CHUNK = 64

_O_AQ = 0
_O_AK = _O_AQ + ATT_Q_WIDTH
_O_AV = _O_AK + ATT_KV_WIDTH
_O_MQ = _O_AV + ATT_KV_WIDTH
_O_MK = _O_MQ + M_QK_WIDTH
_O_MV = _O_MK + M_QK_WIDTH
_O_MO = _O_MV + M_V_WIDTH
_O_MI = _O_MO + M_V_WIDTH
_O_MF = _O_MI + M_HEADS
_O_GA = _O_MF + M_HEADS
_O_GB = _O_GA + D_MODEL
_O_END = _O_GB + D_MODEL


def _rms(x, w):
    return x * lax.rsqrt(jnp.mean(x * x, axis=-1, keepdims=True) + EPS) * w


def _sigmoid(x):
    return 0.5 * jnp.tanh(0.5 * x) + 0.5


def _pack_rows(x):
    half = x.shape[1] // 2
    bits = lax.bitcast_convert_type(x.astype(BF16).astype(F32), jnp.uint32)
    packed = (bits[:, :half] >> 16) | (bits[:, half:] & jnp.uint32(0xFFFF0000))
    return lax.bitcast_convert_type(packed, jnp.int32)


def _unpack_rows(w):
    u = lax.bitcast_convert_type(w, jnp.uint32)
    lo = lax.bitcast_convert_type(u << 16, F32)
    hi = lax.bitcast_convert_type(u & jnp.uint32(0xFFFF0000), F32)
    return jnp.concatenate([lo, hi], axis=-1)


def _log_sigmoid(x):
    return jnp.minimum(x, 0.0) - jnp.log1p(jnp.exp(-jnp.abs(x)))


_TOK_SPLITS = (ATT_KV_WIDTH, M_QK_WIDTH)
_CHAN_SPLITS = (ATT_Q_WIDTH, ATT_KV_WIDTH, M_QK_WIDTH, M_V_WIDTH, M_V_WIDTH)


def _inproj_kernel(x_ref, nw_ref, w_ref, wt_ref, wg_ref, eg_ref, eu_ref, ed_ref, *out_refs):
    tok_refs = out_refs[:len(_TOK_SPLITS)]
    chan_refs = out_refs[len(_TOK_SPLITS):len(_TOK_SPLITS) + len(_CHAN_SPLITS)]
    grow_ref = out_refs[len(_TOK_SPLITS) + len(_CHAN_SPLITS)]
    for src, dst in zip((eg_ref, eu_ref, ed_ref), out_refs[-3:]):
        dst[...] = src[...].astype(BF16)
    h = _rms(x_ref[...], nw_ref[...]).astype(BF16)
    lo = 0
    for ref, width in zip(tok_refs, _TOK_SPLITS):
        ref[...] = jnp.dot(h, w_ref[:, lo:lo + width], preferred_element_type=F32).astype(BF16)
        lo += width
    lo = 0
    for ref, width in zip(chan_refs, _CHAN_SPLITS):
        ref[0] = lax.dot_general(wt_ref[lo:lo + width, :], h, (((1,), (1,)), ((), ())),
                                 preferred_element_type=F32).astype(BF16)
        lo += width
    g = jnp.dot(h, wg_ref[...], preferred_element_type=F32)
    grow_ref[...] = g.T[0:8, :]


def _inproj(x2, norm_w, w_tok, w_chan, w_gates, expert_ws, B, S):
    T = x2.shape[0]
    tm = min(IN_TILE, S)
    tpb = S // tm
    nsteps = T // tm
    row = lambda i: (i, 0)
    const = lambda i: (0, 0)
    chan = lambda i: (i // tpb, 0, i % tpb)
    cast_specs = [pl.BlockSpec((w.shape[0] // nsteps, w.shape[1]), row) for w in expert_ws]
    assert all(w.shape[0] % (8 * nsteps) == 0 for w in expert_ws)
    return pl.pallas_call(
        _inproj_kernel,
        grid=(nsteps,),
        in_specs=[pl.BlockSpec((tm, D_MODEL), row),
                  pl.BlockSpec((1, D_MODEL), const),
                  pl.BlockSpec(w_tok.shape, const),
                  pl.BlockSpec(w_chan.shape, const),
                  pl.BlockSpec((D_MODEL, LANES), const),
                  *cast_specs],
        out_specs=[*[pl.BlockSpec((tm, w), row) for w in _TOK_SPLITS],
                   *[pl.BlockSpec((1, w, tm), chan) for w in _CHAN_SPLITS],
                   pl.BlockSpec((8, tm), lambda i: (0, i)),
                   *cast_specs],
        out_shape=[*[jax.ShapeDtypeStruct((T, w), BF16) for w in _TOK_SPLITS],
                   *[jax.ShapeDtypeStruct((B, w, S), BF16) for w in _CHAN_SPLITS],
                   jax.ShapeDtypeStruct((8, T), F32),
                   *[jax.ShapeDtypeStruct(w.shape, BF16) for w in expert_ws]],
        compiler_params=pltpu.CompilerParams(dimension_semantics=("arbitrary",),
                                             vmem_limit_bytes=VMEM_LIMIT),
        name="inproj",
    )(x2, norm_w, w_tok, w_chan, w_gates, *expert_ws)


def _attn_kernel(sink_ref, qt_ref, k_ref, kp_ref, vt_ref, vtp_ref, o_ref):
    i = pl.program_id(1)
    W = WINDOW
    G = GQA_GROUP
    u = lax.broadcasted_iota(jnp.int32, (W, W), 0)
    t = lax.broadcasted_iota(jnp.int32, (W, W), 1)
    from_prev = u > t

    def keys_values(j):
        if j == 0:
            return kp_ref[0], vtp_ref[0], k_ref[0, 0:W, :], vt_ref[0, :, 0:W]
        return (k_ref[0, (j - 1) * W:j * W, :], vt_ref[0, :, (j - 1) * W:j * W],
                k_ref[0, j * W:(j + 1) * W, :], vt_ref[0, :, j * W:(j + 1) * W])

    def scores(j, g):
        k_prev, _, k_cur, _ = keys_values(j)
        dims = slice(g * HEAD_DIM, (g + 1) * HEAD_DIM)
        qg = jnp.concatenate([qt_ref[0, h * HEAD_DIM:(h + 1) * HEAD_DIM, j * W:(j + 1) * W]
                              for h in range(g * G, (g + 1) * G)], axis=1)
        return (jnp.dot(k_prev[:, dims], qg, preferred_element_type=F32),
                jnp.dot(k_cur[:, dims], qg, preferred_element_type=F32))

    tasks = [(j, g) for j in range(ATT_TILE // W) for g in range(N_KV_HEADS)]
    ahead = scores(*tasks[0])
    pair_rows = []
    for n_task, (j, g) in enumerate(tasks):
        s_prev, s_cur = ahead
        if n_task + 1 < len(tasks):
            ahead = scores(*tasks[n_task + 1])
        cols = slice(j * W, (j + 1) * W)
        _, vt_prev, _, vt_cur = keys_values(j)
        dims = slice(g * HEAD_DIM, (g + 1) * HEAD_DIM)
        for n in range(G):
            hc = slice(n * W, (n + 1) * W)
            sp = s_prev[:, hc]
            if j == 0:
                sp = jnp.where(i > 0, sp, NEG)
            s = jnp.where(from_prev, sp, s_cur[:, hc])
            sink = sink_ref[g * G + n]
            m = jnp.maximum(jnp.max(s, axis=0, keepdims=True), sink)
            p = jnp.exp(s - m)
            inv = 1.0 / (jnp.sum(p, axis=0, keepdims=True) + jnp.exp(sink - m))
            p_prev = jnp.where(from_prev, p, 0.0).astype(BF16)
            p_cur = jnp.where(from_prev, 0.0, p).astype(BF16)
            o = jnp.dot(vt_prev[dims, :], p_prev, preferred_element_type=F32) \
                + jnp.dot(vt_cur[dims, :], p_cur, preferred_element_type=F32)
            pair_rows.append(o * inv)
        if g == N_KV_HEADS - 1:
            out = [jnp.concatenate(pair_rows[n:n + 2], axis=0).T for n in range(0, N_Q_HEADS, 2)]
            o_ref[0, cols, :] = jnp.concatenate(out, axis=1).astype(BF16)
            pair_rows = []


def _attn(sinks, qt, k3, vt):
    B, S, _ = k3.shape
    tq = ATT_TILE
    per = tq // WINDOW
    chan = lambda b, i: (b, 0, i)
    tok = lambda b, i: (b, i, 0)
    return pl.pallas_call(
        _attn_kernel,
        grid=(B, S // tq),
        in_specs=[pl.BlockSpec(memory_space=pltpu.SMEM),
                  pl.BlockSpec((1, ATT_Q_WIDTH, tq), chan),
                  pl.BlockSpec((1, tq, ATT_KV_WIDTH), tok),
                  pl.BlockSpec((1, WINDOW, ATT_KV_WIDTH), lambda b, i: (b, jnp.maximum(i * per - 1, 0), 0)),
                  pl.BlockSpec((1, ATT_KV_WIDTH, tq), chan),
                  pl.BlockSpec((1, ATT_KV_WIDTH, WINDOW), lambda b, i: (b, 0, jnp.maximum(i * per - 1, 0)))],
        out_specs=pl.BlockSpec((1, tq, ATT_Q_WIDTH), tok),
        out_shape=jax.ShapeDtypeStruct((B, S, ATT_Q_WIDTH), BF16),
        compiler_params=pltpu.CompilerParams(dimension_semantics=("arbitrary", "arbitrary"),
                                             vmem_limit_bytes=VMEM_LIMIT),
        name="attn",
    )(sinks, qt, k3, k3, vt, vt)


def _mlstm_kernel(nb, *refs):
    L = M_CHUNK
    H = M_HEADS
    mqt_ref, mk_ref, mvt_ref, mot_ref = refs[:4]
    grow_refs = refs[4:4 + nb]
    (cwq_ref, cbq_ref, cwk_ref, cbk_ref, brow_ref, nw_ref, y_ref,
     state_ref, m_ref, prevq_ref, ubuf_ref) = refs[4 + nb:]
    cidx = pl.program_id(1)

    @pl.when(cidx == 0)
    def _():
        state_ref[...] = jnp.zeros_like(state_ref)
        m_ref[...] = jnp.zeros_like(m_ref)
        prevq_ref[...] = jnp.zeros_like(prevq_ref)
        ubuf_ref[:, 0:8, :] = jnp.zeros((nb, 8, M_QK_WIDTH), F32)

    ri = lax.broadcasted_iota(jnp.int32, (L, L), 0)
    ci = lax.broadcasted_iota(jnp.int32, (L, L), 1)
    causal_t = ri <= ci
    triu = jnp.where(causal_t, 1.0, 0.0).astype(F32)
    lane = lax.broadcasted_iota(jnp.int32, (8, L), 1)
    r2 = lax.broadcasted_iota(jnp.int32, (2 * L, L), 0)
    c2 = lax.broadcasted_iota(jnp.int32, (2 * L, L), 1)
    shifts = [jnp.where(r2 - c2 == L - k, 1.0, 0.0).astype(BF16) for k in range(1, CONV_WIDTH)]
    ones_rows = jnp.where(lax.broadcasted_iota(jnp.int32, (M_STATE_ROWS - M_V_DIM, L), 0) == 0,
                          1.0, 0.0).astype(BF16)

    pairs = [(bb, h) for bb in range(nb) for h in range(H)]
    states = [state_ref[bb * H + h] for bb, h in pairs]
    m_prevs = [m_ref[bb] for bb in range(nb)]

    def prepare(bb):
        cur = mqt_ref[bb]
        both = jnp.concatenate([prevq_ref[bb], cur], axis=1)
        acc = cbq_ref[...] + cwq_ref[CONV_WIDTH - 1] * cur.astype(F32)
        for k in range(1, CONV_WIDTH):
            acc = acc + cwq_ref[CONV_WIDTH - 1 - k] * jnp.dot(both, shifts[k - 1],
                                                              preferred_element_type=F32)
        qt = (acc * _sigmoid(acc)).astype(BF16)
        ubuf_ref[bb, 8:L + 8, :] = mk_ref[bb].astype(F32)
        acc = cbk_ref[...] + cwk_ref[CONV_WIDTH - 1:CONV_WIDTH, :] * ubuf_ref[bb, 8:L + 8, :]
        for j in range(CONV_WIDTH - 1):
            off = 8 - (CONV_WIDTH - 1) + j
            acc = acc + cwk_ref[j:j + 1, :] * ubuf_ref[bb, off:off + L, :]
        ubuf_ref[bb, 0:8, :] = ubuf_ref[bb, L:L + 8, :]
        kk = (acc * _sigmoid(acc) * (M_QK_DIM ** -0.5)).astype(BF16)

        gr = grow_refs[bb][...] + brow_ref[...]
        b = jnp.dot(_log_sigmoid(gr), triu, preferred_element_type=F32,
                    precision=lax.Precision.HIGHEST)
        gi = pltpu.roll(gr, 4, axis=0)
        u = gi - b
        cm = u
        for sh in (1, 2, 4, 8, 16, 32, 64):
            if sh < L:
                cm = jnp.maximum(cm, jnp.where(lane >= sh, pltpu.roll(cm, sh, axis=1), NEG))
        m_prev = m_prevs[bb]
        mt = b + jnp.maximum(m_prev, cm)
        wa = jnp.exp(b + m_prev - mt)
        emt = jnp.exp(-mt)
        m_new = jnp.broadcast_to(mt[:, L - 1:L], (8, L))
        b_last = jnp.broadcast_to(b[:, L - 1:L], (8, L))
        wc = jnp.exp(b_last + m_prev - m_new)
        ws = jnp.exp(b_last + u - m_new)
        return qt, kk, b - mt, u.T, wa, emt, wc, ws, m_new

    seqs = [prepare(bb) for bb in range(nb)]

    def products(bb, h, state):
        qt, kk = seqs[bb][0], seqs[bb][1]
        qh = qt[h * M_QK_DIM:(h + 1) * M_QK_DIM, :]
        kh = kk[:, h * M_QK_DIM:(h + 1) * M_QK_DIM]
        return (kh, jnp.dot(kh, qh, preferred_element_type=F32),
                jnp.dot(state.astype(BF16), qh, preferred_element_type=F32))

    new_states, outs = [], []
    ahead = products(*pairs[0], states[0])
    for n, ((bb, h), state) in enumerate(zip(pairs, states)):
        kh, st, inter = ahead
        if n + 1 < len(pairs):
            ahead = products(*pairs[n + 1], states[n + 1])
        _, _, bmt, ucol, wa, emt, wc, ws, _ = seqs[bb]
        row = slice(H + h, H + h + 1)
        wq = jnp.exp(jnp.where(causal_t, bmt[row, :] + ucol[:, H + h:H + h + 1], NEG))
        sw = (st * wq).astype(BF16)
        vext = jnp.concatenate([mvt_ref[bb, h * M_V_DIM:(h + 1) * M_V_DIM, :], ones_rows], axis=0)
        res = wa[row, :] * inter + jnp.dot(vext, sw, preferred_element_type=F32)
        num = res[:M_V_DIM, :]
        den = res[M_V_DIM:M_V_DIM + 1, :]
        hb = num / jnp.maximum(jnp.abs(den), emt[row, :])
        vw = (vext.astype(F32) * ws[row, :]).astype(BF16)
        new_states.append(wc[row, 0:1] * state + jnp.dot(vw, kh, preferred_element_type=F32))
        hn = hb * lax.rsqrt(jnp.mean(hb * hb, axis=0, keepdims=True) + EPS)
        hn = hn * nw_ref[h * M_V_DIM:(h + 1) * M_V_DIM, :]
        og = _sigmoid(mot_ref[bb, h * M_V_DIM:(h + 1) * M_V_DIM, :].astype(F32))
        outs.append((og * hn).T.astype(BF16))

    for i, (bb, h) in enumerate(pairs):
        state_ref[bb * H + h] = new_states[i]
    for bb in range(nb):
        m_ref[bb] = seqs[bb][8]
        prevq_ref[bb] = mqt_ref[bb]
        y_ref[bb] = jnp.concatenate(outs[bb * H:(bb + 1) * H], axis=1)


def _mlstm(mqt, mk3, mvt, mot, grow, conv_w, conv_b, bias_row, norm_w):
    B, S, _ = mk3.shape
    L = M_CHUNK
    nb = max(d for d in range(1, M_BATCH + 1) if B % d == 0)
    nc = S // L
    assert L == LANES, "per-head scalars are kept lane-replicated next to (8, L) gate rows"
    tok = lambda b, c: (b, c, 0)
    chan = lambda b, c: (b, 0, c)
    const2 = lambda b, c: (0, 0)
    const3 = lambda b, c: (0, 0, 0)
    grow_specs = [pl.BlockSpec((8, L), functools.partial(lambda b, c, n: (0, (b * nb + n) * nc + c), n=n))
                  for n in range(nb)]
    rep = lambda v: jnp.broadcast_to(v.astype(F32)[..., None], v.shape + (L,))
    cwq, cbq = rep(conv_w[:, :M_QK_WIDTH]), rep(conv_b[:M_QK_WIDTH])
    cwk, cbk = conv_w[:, M_QK_WIDTH:].astype(F32), conv_b[M_QK_WIDTH:].reshape(1, -1).astype(F32)
    return pl.pallas_call(
        functools.partial(_mlstm_kernel, nb),
        grid=(B // nb, nc),
        in_specs=[pl.BlockSpec((nb, M_QK_WIDTH, L), chan),
                  pl.BlockSpec((nb, L, M_QK_WIDTH), tok),
                  pl.BlockSpec((nb, M_V_WIDTH, L), chan),
                  pl.BlockSpec((nb, M_V_WIDTH, L), chan),
                  *grow_specs,
                  pl.BlockSpec((CONV_WIDTH, M_QK_WIDTH, L), const3),
                  pl.BlockSpec((M_QK_WIDTH, L), const2),
                  pl.BlockSpec((CONV_WIDTH, M_QK_WIDTH), const2),
                  pl.BlockSpec((1, M_QK_WIDTH), const2),
                  pl.BlockSpec((8, 1), const2),
                  pl.BlockSpec((M_V_WIDTH, L), const2)],
        out_specs=pl.BlockSpec((nb, L, M_V_WIDTH), tok),
        out_shape=jax.ShapeDtypeStruct((B, S, M_V_WIDTH), BF16),
        scratch_shapes=[pltpu.VMEM((nb * M_HEADS, M_STATE_ROWS, M_QK_DIM), F32),
                        pltpu.VMEM((nb, 8, LANES), F32),
                        pltpu.VMEM((nb, M_QK_WIDTH, L), BF16),
                        pltpu.VMEM((nb, L + 8, M_QK_WIDTH), F32)],
        compiler_params=pltpu.CompilerParams(dimension_semantics=("arbitrary", "arbitrary"),
                                             vmem_limit_bytes=VMEM_LIMIT),
        name="mlstm",
    )(mqt, mk3, mvt, mot, *([grow] * nb), cwq, cbq, cwk, cbk, bias_row, rep(norm_w))


def _outproj_kernel(x_ref, ya_ref, ym_ref, nmix_ref, wgab_ref, wa_ref, wm_ref, wo_ref, nw_ref, wr_ref, br_ref,
                    x1_ref, h2_ref, meta_ref, code_ref, cnt_ref, cnt_scr):
    tm = TOK_TILE
    i = pl.program_id(0)

    @pl.when(i == 0)
    def _():
        cnt_scr[...] = jnp.zeros_like(cnt_scr)

    pa = jnp.dot(ya_ref[...], wa_ref[...], preferred_element_type=F32)
    pm = jnp.dot(ym_ref[...], wm_ref[...], preferred_element_type=F32)
    x = x_ref[...]
    h1 = _rms(x, nmix_ref[...]).astype(BF16)
    ga = jnp.dot(h1, wgab_ref[:, :D_MODEL], preferred_element_type=F32)
    gb = jnp.dot(h1, wgab_ref[:, D_MODEL:], preferred_element_type=F32)
    mix = _sigmoid(ga) * pa + _sigmoid(gb) * pm
    x1 = x + jnp.dot(mix.astype(BF16), wo_ref[...], preferred_element_type=F32)
    x1_ref[...] = x1
    h2 = _rms(x1, nw_ref[...])
    h2_ref[...] = _pack_rows(h2)

    nt = (((1,), (1,)), ((), ()))
    h_hi = h2.astype(BF16)
    h_lo = (h2 - h_hi.astype(F32)).astype(BF16)
    hw = lax.dot_general(wr_ref[...], h_hi, nt, preferred_element_type=F32)
    logits = (hw[:LANES] + hw[LANES:]
              + lax.dot_general(wr_ref[:LANES, :], h_lo, nt, preferred_element_type=F32))[:ROUTER_ROWS]
    logits = logits + br_ref[...]
    row8 = lax.broadcasted_iota(jnp.int32, (8, tm), 0)

    def first_argmax(v):
        mx = jnp.max(v, axis=0, keepdims=True)
        idx = jnp.min(jnp.where(v == mx, row8, 8), axis=0, keepdims=True)
        return mx, idx

    gl = jnp.where(row8 < N_GROUPS, logits[0:8], NEG)
    gmax, gi = first_argmax(gl)
    gp = 1.0 / jnp.sum(jnp.exp(gl - gmax), axis=0, keepdims=True)
    el = jnp.zeros((8, tm), F32)
    for g in range(N_GROUPS):
        el = jnp.where(gi == g, logits[8 + 8 * g:16 + 8 * g], el)
    v1, j1 = first_argmax(el)
    v2, j2 = first_argmax(jnp.where(row8 == j1, NEG, el))
    t = jnp.exp(v2 - v1)
    w1 = gp / (1.0 + t)
    w2 = gp * t / (1.0 + t)
    e1 = gi * EXPERTS_PER_GROUP + j1
    e2 = gi * EXPERTS_PER_GROUP + j2

    erow = lax.broadcasted_iota(jnp.int32, (N_EXPERTS, tm), 0)
    hit1 = erow == e1
    hit2 = erow == e2
    onehot = jnp.where(hit1 | hit2, 1.0, 0.0)
    ri = lax.broadcasted_iota(jnp.int32, (tm, tm), 0)
    ci = lax.broadcasted_iota(jnp.int32, (tm, tm), 1)
    before = jnp.where(ri < ci, 1.0, 0.0).astype(BF16)
    rank = jnp.dot(onehot.astype(BF16), before, preferred_element_type=F32) + cnt_scr[...]
    r1 = jnp.sum(jnp.where(hit1, rank, 0.0), axis=0, keepdims=True)
    r2 = jnp.sum(jnp.where(hit2, rank, 0.0), axis=0, keepdims=True)
    cnt = jnp.broadcast_to((rank + onehot)[:, tm - 1:tm], (N_EXPERTS, tm))
    cnt_scr[...] = cnt
    cnt_ref[...] = cnt[:, :LANES]

    c1 = e1 * RANK_RADIX + r1.astype(jnp.int32)
    c2 = e2 * RANK_RADIX + r2.astype(jnp.int32)
    code_ref[...] = jnp.where(row8 == 0, c1, jnp.where(row8 == 1, c2, 0))
    wrow = lax.broadcasted_iota(jnp.int32, (LANES, tm), 0)
    meta_ref[...] = jnp.where(wrow == 4, w1, jnp.where(wrow == 5, w2, 0.0)).T


def _outproj(x2, ya, ym, norm_mix, w_gab, wa, wm, wo, norm_w, w_rt, b_rt):
    T = x2.shape[0]
    tm = TOK_TILE
    row = lambda i: (i, 0)
    const = lambda i: (0, 0)
    return pl.pallas_call(
        _outproj_kernel,
        grid=(T // tm,),
        in_specs=[pl.BlockSpec((tm, D_MODEL), row),
                  pl.BlockSpec((tm, ATT_Q_WIDTH), row),
                  pl.BlockSpec((tm, M_V_WIDTH), row),
                  pl.BlockSpec((1, D_MODEL), const),
                  pl.BlockSpec((D_MODEL, 2 * D_MODEL), const),
                  pl.BlockSpec((ATT_Q_WIDTH, D_MODEL), const),
                  pl.BlockSpec((M_V_WIDTH, D_MODEL), const),
                  pl.BlockSpec((D_MODEL, D_MODEL), const),
                  pl.BlockSpec((1, D_MODEL), const),
                  pl.BlockSpec((2 * LANES, D_MODEL), const),
                  pl.BlockSpec((ROUTER_ROWS, tm), const)],
        out_specs=[pl.BlockSpec((tm, D_MODEL), row),
                   pl.BlockSpec((tm, PACKED), row),
                   pl.BlockSpec((tm, LANES), row),
                   pl.BlockSpec((8, tm), lambda i: (0, i)),
                   pl.BlockSpec((N_EXPERTS, LANES), const)],
        out_shape=[jax.ShapeDtypeStruct((T, D_MODEL), F32),
                   jax.ShapeDtypeStruct((T, PACKED), jnp.int32),
                   jax.ShapeDtypeStruct((T, LANES), F32),
                   jax.ShapeDtypeStruct((8, T), jnp.int32),
                   jax.ShapeDtypeStruct((N_EXPERTS, LANES), F32)],
        scratch_shapes=[pltpu.VMEM((N_EXPERTS, tm), F32)],
        compiler_params=pltpu.CompilerParams(dimension_semantics=("arbitrary",),
                                             vmem_limit_bytes=VMEM_LIMIT),
        name="outproj",
    )(x2, ya, ym, norm_mix, w_gab, wa, wm, wo, norm_w, w_rt, b_rt)


def _slots_kernel(cnt_ref, pstart_ref):
    R = SLOT_BLOCK

    def per_expert(e, run):
        pstart_ref[e] = run
        return run + ((cnt_ref[e] + R - 1) // R) * R

    lax.fori_loop(0, N_EXPERTS, per_expert, 0)


def _slots(counts):
    smem = pl.BlockSpec(memory_space=pltpu.SMEM)
    return pl.pallas_call(
        _slots_kernel,
        in_specs=[smem],
        out_specs=smem,
        out_shape=jax.ShapeDtypeStruct((N_EXPERTS,), jnp.int32),
        name="slots",
    )(counts)


def _experts_kernel(pstart_ref, cnt_ref, xs_hbm, wg_ref, wu_ref, wd_ref, ys_hbm, xbuf, ybuf, lsem, ssem):
    e = pl.program_id(0)
    R = SLOT_BLOCK
    cnt = cnt_ref[e]
    nblk = (cnt + R - 1) // R
    first = pstart_ref[e]

    def rows_of(i):
        return pl.ds(pl.multiple_of(first + i * R, R), R)

    def load(i, slot):
        return pltpu.make_async_copy(xs_hbm.at[rows_of(i)], xbuf.at[slot], lsem.at[slot])

    def store(i, slot):
        return pltpu.make_async_copy(ybuf.at[slot], ys_hbm.at[rows_of(i)], ssem.at[slot])

    @pl.when(nblk > 0)
    def _():
        load(0, 0).start()

    def block(i, carry):
        slot = i % 2
        load(i, slot).wait()

        @pl.when(i + 1 < nblk)
        def _():
            load(i + 1, 1 - slot).start()

        @pl.when(i >= 2)
        def _():
            store(i - 2, slot).wait()

        rows = lax.broadcasted_iota(jnp.int32, (R, PACKED), 0)
        xw = jnp.where(rows < cnt - i * R, xbuf[slot], 0)
        xb = _unpack_rows(xw).astype(BF16)
        g = jnp.dot(xb, wg_ref[0], preferred_element_type=F32)
        u = jnp.dot(xb, wu_ref[0], preferred_element_type=F32)
        act = (g * _sigmoid(g) * u).astype(BF16)
        ybuf[slot] = _pack_rows(jnp.dot(act, wd_ref[0], preferred_element_type=F32))
        store(i, slot).start()
        return carry

    lax.fori_loop(0, nblk, block, 0)

    @pl.when(nblk >= 2)
    def _():
        store(nblk - 2, nblk % 2).wait()

    @pl.when(nblk >= 1)
    def _():
        store(nblk - 1, (nblk - 1) % 2).wait()


def _experts(pstart, counts, xs, w_gate, w_up, w_down):
    R = SLOT_BLOCK
    wspec = lambda shape: pl.BlockSpec((1,) + shape, lambda e, ps, ct: (e, 0, 0))
    grid_spec = pltpu.PrefetchScalarGridSpec(
        num_scalar_prefetch=2,
        grid=(N_EXPERTS,),
        in_specs=[pl.BlockSpec(memory_space=pl.ANY),
                  wspec((D_MODEL, D_EXPERT)),
                  wspec((D_MODEL, D_EXPERT)),
                  wspec((D_EXPERT, D_MODEL))],
        out_specs=pl.BlockSpec(memory_space=pl.ANY),
        scratch_shapes=[pltpu.VMEM((2, R, PACKED), jnp.int32),
                        pltpu.VMEM((2, R, PACKED), jnp.int32),
                        pltpu.SemaphoreType.DMA((2,)),
                        pltpu.SemaphoreType.DMA((2,))],
    )
    return pl.pallas_call(
        _experts_kernel,
        grid_spec=grid_spec,
        out_shape=jax.ShapeDtypeStruct(xs.shape, jnp.int32),
        compiler_params=pltpu.CompilerParams(dimension_semantics=("arbitrary",),
                                             vmem_limit_bytes=VMEM_LIMIT),
        name="experts",
    )(pstart, counts, xs, w_gate, w_up, w_down)


def _dest_kernel(pstart_ref, code_ref, dest_ref):
    c = code_ref[...]
    e = c >> RANK_BITS
    base = jnp.zeros_like(c)
    for k in range(N_EXPERTS):
        base = jnp.where(e == k, pstart_ref[k], base)
    dest_ref[...] = base + (c & (RANK_RADIX - 1))


def _dests(pstart, code_rows):
    T = code_rows.shape[1]
    tc = min(4096, T)
    return pl.pallas_call(
        _dest_kernel,
        grid=(T // tc,),
        in_specs=[pl.BlockSpec(memory_space=pltpu.SMEM),
                  pl.BlockSpec((8, tc), lambda i: (0, i))],
        out_specs=pl.BlockSpec((8, tc), lambda i: (0, i)),
        out_shape=jax.ShapeDtypeStruct((8, T), jnp.int32),
        name="dests",
    )(pstart, code_rows)


def _sc_workers():
    info = plsc.get_sparse_core_info()
    return info.num_cores, info.num_cores * info.num_subcores


def _sc_scatter_rows(src, idx, n_out):
    n_src, width = src.shape
    nc, nw = _sc_workers()
    per_w = n_src // nw
    ch = SC_CHUNK
    nch = per_w // ch
    assert idx.shape[0] == 2 * n_src and per_w * nw == n_src and nch * ch == per_w and nch % 2 == 0
    mesh = plsc.VectorSubcoreMesh(core_axis_name="c", subcore_axis_name="s")
    dma = pltpu.SemaphoreType.DMA

    @functools.partial(
        pl.kernel, mesh=mesh, out_type=jax.ShapeDtypeStruct((n_out, width), src.dtype),
        scratch_types=[pltpu.VMEM((per_w,), jnp.int32), pltpu.VMEM((per_w,), jnp.int32),
                       pltpu.VMEM((ch, width), src.dtype), pltpu.VMEM((ch, width), src.dtype),
                       dma, dma, dma, dma, dma, dma],
        name="sc_scatter")
    def scatter(src_hbm, idx_hbm, out_hbm, idx_a, idx_b, buf0, buf1, in0, in1, out0a, out0b, out1a, out1b):
        base = (lax.axis_index("s") * nc + lax.axis_index("c")) * per_w
        pltpu.sync_copy(idx_hbm.at[pl.ds(base, per_w)], idx_a)
        pltpu.sync_copy(idx_hbm.at[pl.ds(n_src + base, per_w)], idx_b)

        def read(c, buf, sem):
            return pltpu.make_async_copy(src_hbm.at[pl.ds(base + c * ch, ch)], buf, sem)

        def writes(c, buf, sem_a, sem_b):
            rows = pl.ds(c * ch, ch)
            return (pltpu.make_async_copy(buf, out_hbm.at[idx_a.at[rows]], sem_a),
                    pltpu.make_async_copy(buf, out_hbm.at[idx_b.at[rows]], sem_b))

        def start(copies):
            for cp in copies:
                cp.start()

        def wait(copies):
            for cp in copies:
                cp.wait()

        read(0, buf0, in0).start()

        @pl.loop(0, nch, step=2)
        def _(c):
            @pl.when(c > 0)
            def _():
                wait(writes(c - 1, buf1, out1a, out1b))
            read(c + 1, buf1, in1).start()
            read(c, buf0, in0).wait()
            start(writes(c, buf0, out0a, out0b))
            read(c + 1, buf1, in1).wait()
            start(writes(c + 1, buf1, out1a, out1b))
            wait(writes(c, buf0, out0a, out0b))

            @pl.when(c + 2 < nch)
            def _():
                read(c + 2, buf0, in0).start()

        wait(writes(nch - 1, buf1, out1a, out1b))

    return scatter(src, idx)


def _sc_gather_rows(src, idx):
    M = idx.shape[0]
    width = src.shape[1]
    nc, nw = _sc_workers()
    per_w = M // nw
    ch = SC_CHUNK
    nch = per_w // ch
    assert per_w * nw == M and nch * ch == per_w and nch % 2 == 0
    mesh = plsc.VectorSubcoreMesh(core_axis_name="c", subcore_axis_name="s")
    dma = pltpu.SemaphoreType.DMA

    @functools.partial(
        pl.kernel, mesh=mesh, out_type=jax.ShapeDtypeStruct((M, width), src.dtype),
        scratch_types=[pltpu.VMEM((per_w,), jnp.int32),
                       pltpu.VMEM((ch, width), src.dtype), pltpu.VMEM((ch, width), src.dtype),
                       dma, dma, dma, dma],
        name="sc_gather")
    def gather(src_hbm, idx_hbm, out_hbm, idx_v, buf0, buf1, in0, in1, out0, out1):
        base = (lax.axis_index("s") * nc + lax.axis_index("c")) * per_w
        pltpu.sync_copy(idx_hbm.at[pl.ds(base, per_w)], idx_v)

        def read(c, buf, sem):
            return pltpu.make_async_copy(src_hbm.at[idx_v.at[pl.ds(c * ch, ch)]], buf, sem)

        def write(c, buf, sem):
            return pltpu.make_async_copy(buf, out_hbm.at[pl.ds(base + c * ch, ch)], sem)

        read(0, buf0, in0).start()

        @pl.loop(0, nch, step=2)
        def _(c):
            @pl.when(c > 0)
            def _():
                write(c - 1, buf1, out1).wait()
            read(c + 1, buf1, in1).start()
            read(c, buf0, in0).wait()
            write(c, buf0, out0).start()
            read(c + 1, buf1, in1).wait()
            write(c + 1, buf1, out1).start()
            write(c, buf0, out0).wait()

            @pl.when(c + 2 < nch)
            def _():
                read(c + 2, buf0, in0).start()

        write(nch - 1, buf1, out1).wait()

    return gather(src, idx)


def _final_kernel(x1_ref, y1_ref, y2_ref, meta_ref, nw_ref, o_ref):
    w1 = meta_ref[:, 4:5]
    w2 = meta_ref[:, 5:6]
    x2 = x1_ref[...] + (w1 * _unpack_rows(y1_ref[...]) + w2 * _unpack_rows(y2_ref[...]))
    o_ref[...] = _rms(x2, nw_ref[...])


def _final(x1, yg, meta, norm_w):
    T = x1.shape[0]
    tm = COMB_TILE
    nt = T // tm
    row = lambda i: (i, 0)
    return pl.pallas_call(
        _final_kernel,
        grid=(nt,),
        in_specs=[pl.BlockSpec((tm, D_MODEL), row),
                  pl.BlockSpec((tm, PACKED), row),
                  pl.BlockSpec((tm, PACKED), lambda i: (i + nt, 0)),
                  pl.BlockSpec((tm, LANES), row),
                  pl.BlockSpec((1, D_MODEL), lambda i: (0, 0))],
        out_specs=pl.BlockSpec((tm, D_MODEL), row),
        out_shape=jax.ShapeDtypeStruct((T, D_MODEL), F32),
        compiler_params=pltpu.CompilerParams(dimension_semantics=("arbitrary",),
                                             vmem_limit_bytes=VMEM_LIMIT),
        name="final",
    )(x1, yg, yg, meta, norm_w)


def _layer(x, norm_mix_w, w_in, conv_w, conv_b, b_igate, b_fgate, attn_sinks, mlstm_norm_w,
           w_attn_o, w_mlstm_o, w_out, norm_ffn_w, w_group, b_group, w_router, b_router,
           w_gate, w_up, w_down, out_norm_w):
    B, S, D = x.shape
    T = B * S
    x2 = x.reshape(T, D)

    w_tok = jnp.concatenate([w_in[:, _O_AK:_O_AV], w_in[:, _O_MK:_O_MV]], axis=1).astype(BF16)
    w_chan = jnp.concatenate([w_in[:, _O_AQ:_O_AK] * (HEAD_DIM ** -0.5), w_in[:, _O_AV:_O_MQ],
                              w_in[:, _O_MQ:_O_MK], w_in[:, _O_MV:_O_MI]], axis=1).T.astype(BF16)
    w_gates = jnp.pad(w_in[:, _O_MI:_O_GA], ((0, 0), (0, LANES - 2 * M_HEADS))).astype(BF16)
    expert_ws = [w.reshape(-1, w.shape[-1]) for w in (w_gate, w_up, w_down)]
    ak, mk, aqt, avt, mqt, mvt, mot, grow, wg16, wu16, wd16 = _inproj(
        x2, norm_mix_w.reshape(1, D), w_tok, w_chan, w_gates, expert_ws, B, S)

    ya = _attn(attn_sinks.astype(F32), aqt, ak.reshape(B, S, -1), avt)

    bias = jnp.concatenate([b_igate, b_fgate]).astype(F32)
    ym = _mlstm(mqt, mk.reshape(B, S, -1), mvt, mot, grow, conv_w, conv_b,
                bias.reshape(2 * M_HEADS, 1), mlstm_norm_w)

    gpad = jnp.zeros((8 - N_GROUPS, D), F32)
    w_rt = jnp.concatenate([w_group.T, gpad, w_router.T, jnp.zeros((LANES - ROUTER_ROWS, D), F32)], axis=0)
    w_rt_hi = w_rt.astype(BF16)
    w_rt = jnp.concatenate([w_rt_hi, (w_rt - w_rt_hi.astype(F32)).astype(BF16)], axis=0)
    b_rt = jnp.concatenate([b_group, jnp.zeros((8 - N_GROUPS,), F32), b_router]).astype(F32)
    b_rt = jnp.broadcast_to(b_rt[:, None], (ROUTER_ROWS, TOK_TILE))
    x1, h2p, meta, code_rows, cnt = _outproj(x2, ya.reshape(T, -1), ym.reshape(T, -1),
                                            norm_mix_w.reshape(1, D), w_in[:, _O_GA:].astype(BF16),
                                            w_attn_o.astype(BF16), w_mlstm_o.astype(BF16),
                                            w_out.astype(BF16), norm_ffn_w.reshape(1, D), w_rt, b_rt)

    counts = cnt[:, 0].astype(jnp.int32)
    nblk = (2 * T) // SLOT_BLOCK + N_EXPERTS
    pstart = _slots(counts)
    dest = _dests(pstart, code_rows)[0:2].reshape(2 * T)

    xs = _sc_scatter_rows(h2p, dest, nblk * SLOT_BLOCK)
    ys = _experts(pstart, counts, xs, wg16.reshape(w_gate.shape), wu16.reshape(w_up.shape),
                  wd16.reshape(w_down.shape))
    yg = _sc_gather_rows(ys, dest)
    out = _final(x1, yg, meta, out_norm_w.reshape(1, D))
    return out.reshape(B, S, D)


def kernel(x, norm_mix_w, w_in, conv_w, conv_b, b_igate, b_fgate, attn_sinks, mlstm_norm_w, w_attn_o,
           w_mlstm_o, w_out, norm_ffn_w, w_group, b_group, w_router, b_router, w_gate, w_up, w_down,
           norm_final_w):
    depth = w_in.shape[0]
    assert depth == 1, "final RMSNorm is fused into the last layer's combine kernel"
    return _layer(x, norm_mix_w[0], w_in[0], conv_w[0], conv_b[0], b_igate[0], b_fgate[0],
                  attn_sinks[0], mlstm_norm_w[0], w_attn_o[0], w_mlstm_o[0], w_out[0], norm_ffn_w[0],
                  w_group[0], b_group[0], w_router[0], b_router[0], w_gate[0], w_up[0], w_down[0],
                  norm_final_w)
```

```python
import functools

import jax
import jax.numpy as jnp
from jax import lax
from jax.experimental import pallas as pl
from jax.experimental.pallas import tpu as pltpu
from jax.experimental.pallas import tpu_sc as plsc

F32 = jnp.float32
BF16 = jnp.bfloat16

D_MODEL = 1024
N_Q_HEADS = 8
N_KV_HEADS = 2
HEAD_DIM = 64
WINDOW = 128
GQA_GROUP = N_Q_HEADS // N_KV_HEADS
M_HEADS = 4
M_QK_DIM = 64
M_V_DIM = 128
CONV_WIDTH = 4
N_GROUPS = 4
EXPERTS_PER_GROUP = 8
N_EXPERTS = N_GROUPS * EXPERTS_PER_GROUP
D_EXPERT = 512
EPS = 1e-6

ATT_Q_WIDTH = N_Q_HEADS * HEAD_DIM
ATT_KV_WIDTH = N_KV_HEADS * HEAD_DIM
M_QK_WIDTH = M_HEADS * M_QK_DIM
M_V_WIDTH = M_HEADS * M_V_DIM

LANES = 128
NEG = -1e30
VMEM_LIMIT = 56 * 1024 * 1024

IN_TILE = 1024
TOK_TILE = 512
ATT_TILE = 1024
M_CHUNK = 128
M_BATCH = 8
M_STATE_ROWS = M_V_DIM + 16
SLOT_BLOCK = 512
COMB_TILE = 1024
RANK_RADIX = 65536
RANK_BITS = 16
ROUTER_ROWS = 8 + N_EXPERTS
PACKED = D_MODEL // 2
SC_CHUNK = 64

_O_AQ = 0
_O_AK = _O_AQ + ATT_Q_WIDTH
_O_AV = _O_AK + ATT_KV_WIDTH
_O_MQ = _O_AV + ATT_KV_WIDTH
_O_MK = _O_MQ + M_QK_WIDTH
_O_MV = _O_MK + M_QK_WIDTH
_O_MO = _O_MV + M_V_WIDTH
_O_MI = _O_MO + M_V_WIDTH
_O_MF = _O_MI + M_HEADS
_O_GA = _O_MF + M_HEADS
_O_GB = _O_GA + D_MODEL
_O_END = _O_GB + D_MODEL


def _rms(x, w):
    return x * lax.rsqrt(jnp.mean(x * x, axis=-1, keepdims=True) + EPS) * w


def _sigmoid(x):
    return 0.5 * jnp.tanh(0.5 * x) + 0.5


def _pack_rows(x):
    half = x.shape[1] // 2
    bits = lax.bitcast_convert_type(x.astype(BF16).astype(F32), jnp.uint32)
    packed = (bits[:, :half] >> 16) | (bits[:, half:] & jnp.uint32(0xFFFF0000))
    return lax.bitcast_convert_type(packed, jnp.int32)


def _unpack_rows(w):
    u = lax.bitcast_convert_type(w, jnp.uint32)
    lo = lax.bitcast_convert_type(u << 16, F32)
    hi = lax.bitcast_convert_type(u & jnp.uint32(0xFFFF0000), F32)
    return jnp.concatenate([lo, hi], axis=-1)


def _log_sigmoid(x):
    return jnp.minimum(x, 0.0) - jnp.log1p(jnp.exp(-jnp.abs(x)))


_TOK_SPLITS = (ATT_KV_WIDTH, M_QK_WIDTH)
_CHAN_SPLITS = (ATT_Q_WIDTH, ATT_KV_WIDTH, M_QK_WIDTH, M_V_WIDTH, M_V_WIDTH)


def _inproj_kernel(x_ref, nw_ref, w_ref, wt_ref, wg_ref, eg_ref, eu_ref, ed_ref, *out_refs):
    tok_refs = out_refs[:len(_TOK_SPLITS)]
    chan_refs = out_refs[len(_TOK_SPLITS):len(_TOK_SPLITS) + len(_CHAN_SPLITS)]
    grow_ref = out_refs[len(_TOK_SPLITS) + len(_CHAN_SPLITS)]
    for src, dst in zip((eg_ref, eu_ref, ed_ref), out_refs[-3:]):
        dst[...] = src[...].astype(BF16)
    h = _rms(x_ref[...], nw_ref[...]).astype(BF16)
    lo = 0
    for ref, width in zip(tok_refs, _TOK_SPLITS):
        ref[...] = jnp.dot(h, w_ref[:, lo:lo + width], preferred_element_type=F32).astype(BF16)
        lo += width
    lo = 0
    for ref, width in zip(chan_refs, _CHAN_SPLITS):
        ref[0] = lax.dot_general(wt_ref[lo:lo + width, :], h, (((1,), (1,)), ((), ())),
                                 preferred_element_type=F32).astype(BF16)
        lo += width
    g = jnp.dot(h, wg_ref[...], preferred_element_type=F32)
    grow_ref[...] = g.T[0:8, :]


def _inproj(x2, norm_w, w_tok, w_chan, w_gates, expert_ws, B, S):
    T = x2.shape[0]
    tm = min(IN_TILE, S)
    tpb = S // tm
    nsteps = T // tm
    row = lambda i: (i, 0)
    const = lambda i: (0, 0)
    chan = lambda i: (i // tpb, 0, i % tpb)
    cast_specs = [pl.BlockSpec((w.shape[0] // nsteps, w.shape[1]), row) for w in expert_ws]
    assert all(w.shape[0] % (8 * nsteps) == 0 for w in expert_ws)
    return pl.pallas_call(
        _inproj_kernel,
        grid=(nsteps,),
        in_specs=[pl.BlockSpec((tm, D_MODEL), row),
                  pl.BlockSpec((1, D_MODEL), const),
                  pl.BlockSpec(w_tok.shape, const),
                  pl.BlockSpec(w_chan.shape, const),
                  pl.BlockSpec((D_MODEL, LANES), const),
                  *cast_specs],
        out_specs=[*[pl.BlockSpec((tm, w), row) for w in _TOK_SPLITS],
                   *[pl.BlockSpec((1, w, tm), chan) for w in _CHAN_SPLITS],
                   pl.BlockSpec((8, tm), lambda i: (0, i)),
                   *cast_specs],
        out_shape=[*[jax.ShapeDtypeStruct((T, w), BF16) for w in _TOK_SPLITS],
                   *[jax.ShapeDtypeStruct((B, w, S), BF16) for w in _CHAN_SPLITS],
                   jax.ShapeDtypeStruct((8, T), F32),
                   *[jax.ShapeDtypeStruct(w.shape, BF16) for w in expert_ws]],
        compiler_params=pltpu.CompilerParams(dimension_semantics=("arbitrary",),
                                             vmem_limit_bytes=VMEM_LIMIT),
        name="inproj",
    )(x2, norm_w, w_tok, w_chan, w_gates, *expert_ws)


def _attn_kernel(tq, sink_ref, qt_ref, k_ref, kp_ref, vt_ref, vtp_ref, o_ref):
    i = pl.program_id(1)
    W = WINDOW
    G = GQA_GROUP
    u = lax.broadcasted_iota(jnp.int32, (W, W), 0)
    t = lax.broadcasted_iota(jnp.int32, (W, W), 1)
    from_prev = u > t

    def keys_values(j):
        if j == 0:
            return kp_ref[0], vtp_ref[0], k_ref[0, 0:W, :], vt_ref[0, :, 0:W]
        return (k_ref[0, (j - 1) * W:j * W, :], vt_ref[0, :, (j - 1) * W:j * W],
                k_ref[0, j * W:(j + 1) * W, :], vt_ref[0, :, j * W:(j + 1) * W])

    def scores(j, g):
        k_prev, _, k_cur, _ = keys_values(j)
        dims = slice(g * HEAD_DIM, (g + 1) * HEAD_DIM)
        qg = jnp.concatenate([qt_ref[0, h * HEAD_DIM:(h + 1) * HEAD_DIM, j * W:(j + 1) * W]
                              for h in range(g * G, (g + 1) * G)], axis=1)
        return (jnp.dot(k_prev[:, dims], qg, preferred_element_type=F32),
                jnp.dot(k_cur[:, dims], qg, preferred_element_type=F32))

    tasks = [(j, g) for j in range(tq // W) for g in range(N_KV_HEADS)]
    ahead = scores(*tasks[0])
    pair_rows = []
    for n_task, (j, g) in enumerate(tasks):
        s_prev, s_cur = ahead
        if n_task + 1 < len(tasks):
            ahead = scores(*tasks[n_task + 1])
        cols = slice(j * W, (j + 1) * W)
        _, vt_prev, _, vt_cur = keys_values(j)
        dims = slice(g * HEAD_DIM, (g + 1) * HEAD_DIM)
        for n in range(G):
            hc = slice(n * W, (n + 1) * W)
            sp = s_prev[:, hc]
            if j == 0:
                sp = jnp.where(i > 0, sp, NEG)
            s = jnp.where(from_prev, sp, s_cur[:, hc])
            sink = sink_ref[g * G + n]
            m = jnp.maximum(jnp.max(s, axis=0, keepdims=True), sink)
            p = jnp.exp(s - m)
            inv = 1.0 / (jnp.sum(p, axis=0, keepdims=True) + jnp.exp(sink - m))
            p_prev = jnp.where(from_prev, p, 0.0).astype(BF16)
            p_cur = jnp.where(from_prev, 0.0, p).astype(BF16)
            o = jnp.dot(vt_prev[dims, :], p_prev, preferred_element_type=F32) \
                + jnp.dot(vt_cur[dims, :], p_cur, preferred_element_type=F32)
            pair_rows.append(o * inv)
        if g == N_KV_HEADS - 1:
            out = [jnp.concatenate(pair_rows[n:n + 2], axis=0).T for n in range(0, N_Q_HEADS, 2)]
            o_ref[0, cols, :] = jnp.concatenate(out, axis=1).astype(BF16)
            pair_rows = []


def _attn(sinks, qt, k3, vt):
    B, S, _ = k3.shape
    tq = min(ATT_TILE, S)
    per = tq // WINDOW
    chan = lambda b, i: (b, 0, i)
    tok = lambda b, i: (b, i, 0)
    return pl.pallas_call(
        functools.partial(_attn_kernel, tq),
        grid=(B, S // tq),
        in_specs=[pl.BlockSpec(memory_space=pltpu.SMEM),
                  pl.BlockSpec((1, ATT_Q_WIDTH, tq), chan),
                  pl.BlockSpec((1, tq, ATT_KV_WIDTH), tok),
                  pl.BlockSpec((1, WINDOW, ATT_KV_WIDTH), lambda b, i: (b, jnp.maximum(i * per - 1, 0), 0)),
                  pl.BlockSpec((1, ATT_KV_WIDTH, tq), chan),
                  pl.BlockSpec((1, ATT_KV_WIDTH, WINDOW), lambda b, i: (b, 0, jnp.maximum(i * per - 1, 0)))],
        out_specs=pl.BlockSpec((1, tq, ATT_Q_WIDTH), tok),
        out_shape=jax.ShapeDtypeStruct((B, S, ATT_Q_WIDTH), BF16),
        compiler_params=pltpu.CompilerParams(dimension_semantics=("arbitrary", "arbitrary"),
                                             vmem_limit_bytes=VMEM_LIMIT),
        name="attn",
    )(sinks, qt, k3, k3, vt, vt)


def _mlstm_kernel(nb, *refs):
    L = M_CHUNK
    H = M_HEADS
    mqt_ref, mk_ref, mvt_ref, mot_ref = refs[:4]
    grow_refs = refs[4:4 + nb]
    (cwq_ref, cbq_ref, cwk_ref, cbk_ref, brow_ref, nw_ref, y_ref,
     state_ref, m_ref, prevq_ref, ubuf_ref) = refs[4 + nb:]
    cidx = pl.program_id(1)

    @pl.when(cidx == 0)
    def _():
        state_ref[...] = jnp.zeros_like(state_ref)
        m_ref[...] = jnp.zeros_like(m_ref)
        prevq_ref[...] = jnp.zeros_like(prevq_ref)
        ubuf_ref[:, 0:8, :] = jnp.zeros((nb, 8, M_QK_WIDTH), F32)

    ri = lax.broadcasted_iota(jnp.int32, (L, L), 0)
    ci = lax.broadcasted_iota(jnp.int32, (L, L), 1)
    causal_t = ri <= ci
    triu = jnp.where(causal_t, 1.0, 0.0).astype(F32)
    lane = lax.broadcasted_iota(jnp.int32, (8, L), 1)
    r2 = lax.broadcasted_iota(jnp.int32, (2 * L, L), 0)
    c2 = lax.broadcasted_iota(jnp.int32, (2 * L, L), 1)
    shifts = [jnp.where(r2 - c2 == L - k, 1.0, 0.0).astype(BF16) for k in range(1, CONV_WIDTH)]
    ones_rows = jnp.where(lax.broadcasted_iota(jnp.int32, (M_STATE_ROWS - M_V_DIM, L), 0) == 0,
                          1.0, 0.0).astype(BF16)

    pairs = [(bb, h) for bb in range(nb) for h in range(H)]
    states = [state_ref[bb * H + h] for bb, h in pairs]
    m_prevs = [m_ref[bb] for bb in range(nb)]

    def prepare(bb):
        cur = mqt_ref[bb]
        both = jnp.concatenate([prevq_ref[bb], cur], axis=1)
        acc = cbq_ref[...] + cwq_ref[CONV_WIDTH - 1] * cur.astype(F32)
        for k in range(1, CONV_WIDTH):
            acc = acc + cwq_ref[CONV_WIDTH - 1 - k] * jnp.dot(both, shifts[k - 1],
                                                              preferred_element_type=F32)
        qt = (acc * _sigmoid(acc)).astype(BF16)
        ubuf_ref[bb, 8:L + 8, :] = mk_ref[bb].astype(F32)
        acc = cbk_ref[...] + cwk_ref[CONV_WIDTH - 1:CONV_WIDTH, :] * ubuf_ref[bb, 8:L + 8, :]
        for j in range(CONV_WIDTH - 1):
            off = 8 - (CONV_WIDTH - 1) + j
            acc = acc + cwk_ref[j:j + 1, :] * ubuf_ref[bb, off:off + L, :]
        ubuf_ref[bb, 0:8, :] = ubuf_ref[bb, L:L + 8, :]
        kk = (acc * _sigmoid(acc) * (M_QK_DIM ** -0.5)).astype(BF16)

        gr = grow_refs[bb][...] + brow_ref[...]
        b = jnp.dot(_log_sigmoid(gr), triu, preferred_element_type=F32,
                    precision=lax.Precision.HIGHEST)
        gi = pltpu.roll(gr, 4, axis=0)
        u = gi - b
        cm = u
        for sh in (1, 2, 4, 8, 16, 32, 64):
            if sh < L:
                cm = jnp.maximum(cm, jnp.where(lane >= sh, pltpu.roll(cm, sh, axis=1), NEG))
        m_prev = m_prevs[bb]
        mt = b + jnp.maximum(m_prev, cm)
        wa = jnp.exp(b + m_prev - mt)
        emt = jnp.exp(-mt)
        m_new = jnp.broadcast_to(mt[:, L - 1:L], (8, L))
        b_last = jnp.broadcast_to(b[:, L - 1:L], (8, L))
        wc = jnp.exp(b_last + m_prev - m_new)
        ws = jnp.exp(b_last + u - m_new)
        return qt, kk, b - mt, u.T, wa, emt, wc, ws, m_new

    seqs = [prepare(bb) for bb in range(nb)]

    def products(bb, h, state):
        qt, kk = seqs[bb][0], seqs[bb][1]
        qh = qt[h * M_QK_DIM:(h + 1) * M_QK_DIM, :]
        kh = kk[:, h * M_QK_DIM:(h + 1) * M_QK_DIM]
        return (kh, jnp.dot(kh, qh, preferred_element_type=F32),
                jnp.dot(state.astype(BF16), qh, preferred_element_type=F32))

    new_states, outs = [], []
    ahead = products(*pairs[0], states[0])
    for n, ((bb, h), state) in enumerate(zip(pairs, states)):
        kh, st, inter = ahead
        if n + 1 < len(pairs):
            ahead = products(*pairs[n + 1], states[n + 1])
        _, _, bmt, ucol, wa, emt, wc, ws, _ = seqs[bb]
        row = slice(H + h, H + h + 1)
        wq = jnp.exp(jnp.where(causal_t, bmt[row, :] + ucol[:, H + h:H + h + 1], NEG))
        sw = (st * wq).astype(BF16)
        vext = jnp.concatenate([mvt_ref[bb, h * M_V_DIM:(h + 1) * M_V_DIM, :], ones_rows], axis=0)
        res = wa[row, :] * inter + jnp.dot(vext, sw, preferred_element_type=F32)
        num = res[:M_V_DIM, :]
        den = res[M_V_DIM:M_V_DIM + 1, :]
        hb = num / jnp.maximum(jnp.abs(den), emt[row, :])
        vw = (vext.astype(F32) * ws[row, :]).astype(BF16)
        new_states.append(wc[row, 0:1] * state + jnp.dot(vw, kh, preferred_element_type=F32))
        hn = hb * lax.rsqrt(jnp.mean(hb * hb, axis=0, keepdims=True) + EPS)
        hn = hn * nw_ref[h * M_V_DIM:(h + 1) * M_V_DIM, :]
        og = _sigmoid(mot_ref[bb, h * M_V_DIM:(h + 1) * M_V_DIM, :].astype(F32))
        outs.append((og * hn).T.astype(BF16))

    for i, (bb, h) in enumerate(pairs):
        state_ref[bb * H + h] = new_states[i]
    for bb in range(nb):
        m_ref[bb] = seqs[bb][8]
        prevq_ref[bb] = mqt_ref[bb]
        y_ref[bb] = jnp.concatenate(outs[bb * H:(bb + 1) * H], axis=1)


def _mlstm(mqt, mk3, mvt, mot, grow, conv_w, conv_b, bias_row, norm_w):
    B, S, _ = mk3.shape
    L = M_CHUNK
    nb = max(d for d in range(1, M_BATCH + 1) if B % d == 0)
    nc = S // L
    assert L == LANES, "per-head scalars are kept lane-replicated next to (8, L) gate rows"
    tok = lambda b, c: (b, c, 0)
    chan = lambda b, c: (b, 0, c)
    const2 = lambda b, c: (0, 0)
    const3 = lambda b, c: (0, 0, 0)
    grow_specs = [pl.BlockSpec((8, L), functools.partial(lambda b, c, n: (0, (b * nb + n) * nc + c), n=n))
                  for n in range(nb)]
    rep = lambda v: jnp.broadcast_to(v.astype(F32)[..., None], v.shape + (L,))
    cwq, cbq = rep(conv_w[:, :M_QK_WIDTH]), rep(conv_b[:M_QK_WIDTH])
    cwk, cbk = conv_w[:, M_QK_WIDTH:].astype(F32), conv_b[M_QK_WIDTH:].reshape(1, -1).astype(F32)
    return pl.pallas_call(
        functools.partial(_mlstm_kernel, nb),
        grid=(B // nb, nc),
        in_specs=[pl.BlockSpec((nb, M_QK_WIDTH, L), chan),
                  pl.BlockSpec((nb, L, M_QK_WIDTH), tok),
                  pl.BlockSpec((nb, M_V_WIDTH, L), chan),
                  pl.BlockSpec((nb, M_V_WIDTH, L), chan),
                  *grow_specs,
                  pl.BlockSpec((CONV_WIDTH, M_QK_WIDTH, L), const3),
                  pl.BlockSpec((M_QK_WIDTH, L), const2),
                  pl.BlockSpec((CONV_WIDTH, M_QK_WIDTH), const2),
                  pl.BlockSpec((1, M_QK_WIDTH), const2),
                  pl.BlockSpec((8, 1), const2),
                  pl.BlockSpec((M_V_WIDTH, L), const2)],
        out_specs=pl.BlockSpec((nb, L, M_V_WIDTH), tok),
        out_shape=jax.ShapeDtypeStruct((B, S, M_V_WIDTH), BF16),
        scratch_shapes=[pltpu.VMEM((nb * M_HEADS, M_STATE_ROWS, M_QK_DIM), F32),
                        pltpu.VMEM((nb, 8, LANES), F32),
                        pltpu.VMEM((nb, M_QK_WIDTH, L), BF16),
                        pltpu.VMEM((nb, L + 8, M_QK_WIDTH), F32)],
        compiler_params=pltpu.CompilerParams(dimension_semantics=("arbitrary", "arbitrary"),
                                             vmem_limit_bytes=VMEM_LIMIT),
        name="mlstm",
    )(mqt, mk3, mvt, mot, *([grow] * nb), cwq, cbq, cwk, cbk, bias_row, rep(norm_w))


def _outproj_kernel(x_ref, ya_ref, ym_ref, nmix_ref, wgab_ref, wa_ref, wm_ref, wo_ref, nw_ref, wr_ref, br_ref,
                    x1_ref, h2_ref, meta_ref, code_ref, cnt_ref, cnt_scr):
    tm = TOK_TILE
    i = pl.program_id(0)

    @pl.when(i == 0)
    def _():
        cnt_scr[...] = jnp.zeros_like(cnt_scr)

    pa = jnp.dot(ya_ref[...], wa_ref[...], preferred_element_type=F32)
    pm = jnp.dot(ym_ref[...], wm_ref[...], preferred_element_type=F32)
    x = x_ref[...]
    h1 = _rms(x, nmix_ref[...]).astype(BF16)
    ga = jnp.dot(h1, wgab_ref[:, :D_MODEL], preferred_element_type=F32)
    gb = jnp.dot(h1, wgab_ref[:, D_MODEL:], preferred_element_type=F32)
    mix = _sigmoid(ga) * pa + _sigmoid(gb) * pm
    x1 = x + jnp.dot(mix.astype(BF16), wo_ref[...], preferred_element_type=F32)
    x1_ref[...] = x1
    h2 = _rms(x1, nw_ref[...])
    h2_ref[...] = _pack_rows(h2)

    nt = (((1,), (1,)), ((), ()))
    h_hi = h2.astype(BF16)
    h_lo = (h2 - h_hi.astype(F32)).astype(BF16)
    hw = lax.dot_general(wr_ref[...], h_hi, nt, preferred_element_type=F32)
    logits = (hw[:LANES] + hw[LANES:]
              + lax.dot_general(wr_ref[:LANES, :], h_lo, nt, preferred_element_type=F32))[:ROUTER_ROWS]
    logits = logits + br_ref[...]
    row8 = lax.broadcasted_iota(jnp.int32, (8, tm), 0)

    def first_argmax(v):
        mx = jnp.max(v, axis=0, keepdims=True)
        idx = jnp.min(jnp.where(v == mx, row8, 8), axis=0, keepdims=True)
        return mx, idx

    gl = jnp.where(row8 < N_GROUPS, logits[0:8], NEG)
    gmax, gi = first_argmax(gl)
    gp = 1.0 / jnp.sum(jnp.exp(gl - gmax), axis=0, keepdims=True)
    el = jnp.zeros((8, tm), F32)
    for g in range(N_GROUPS):
        el = jnp.where(gi == g, logits[8 + 8 * g:16 + 8 * g], el)
    v1, j1 = first_argmax(el)
    v2, j2 = first_argmax(jnp.where(row8 == j1, NEG, el))
    t = jnp.exp(v2 - v1)
    w1 = gp / (1.0 + t)
    w2 = gp * t / (1.0 + t)
    e1 = gi * EXPERTS_PER_GROUP + j1
    e2 = gi * EXPERTS_PER_GROUP + j2

    erow = lax.broadcasted_iota(jnp.int32, (N_EXPERTS, tm), 0)
    hit1 = erow == e1
    hit2 = erow == e2
    onehot = jnp.where(hit1 | hit2, 1.0, 0.0)
    ri = lax.broadcasted_iota(jnp.int32, (tm, tm), 0)
    ci = lax.broadcasted_iota(jnp.int32, (tm, tm), 1)
    before = jnp.where(ri < ci, 1.0, 0.0).astype(BF16)
    rank = jnp.dot(onehot.astype(BF16), before, preferred_element_type=F32) + cnt_scr[...]
    r1 = jnp.sum(jnp.where(hit1, rank, 0.0), axis=0, keepdims=True)
    r2 = jnp.sum(jnp.where(hit2, rank, 0.0), axis=0, keepdims=True)
    cnt = jnp.broadcast_to((rank + onehot)[:, tm - 1:tm], (N_EXPERTS, tm))
    cnt_scr[...] = cnt
    cnt_ref[...] = cnt[:, :LANES]

    c1 = e1 * RANK_RADIX + r1.astype(jnp.int32)
    c2 = e2 * RANK_RADIX + r2.astype(jnp.int32)
    code_ref[...] = jnp.where(row8 == 0, c1, jnp.where(row8 == 1, c2, 0))
    wrow = lax.broadcasted_iota(jnp.int32, (LANES, tm), 0)
    meta_ref[...] = jnp.where(wrow == 4, w1, jnp.where(wrow == 5, w2, 0.0)).T


def _outproj(x2, ya, ym, norm_mix, w_gab, wa, wm, wo, norm_w, w_rt, b_rt):
    T = x2.shape[0]
    tm = TOK_TILE
    row = lambda i: (i, 0)
    const = lambda i: (0, 0)
    return pl.pallas_call(
        _outproj_kernel,
        grid=(T // tm,),
        in_specs=[pl.BlockSpec((tm, D_MODEL), row),
                  pl.BlockSpec((tm, ATT_Q_WIDTH), row),
                  pl.BlockSpec((tm, M_V_WIDTH), row),
                  pl.BlockSpec((1, D_MODEL), const),
                  pl.BlockSpec((D_MODEL, 2 * D_MODEL), const),
                  pl.BlockSpec((ATT_Q_WIDTH, D_MODEL), const),
                  pl.BlockSpec((M_V_WIDTH, D_MODEL), const),
                  pl.BlockSpec((D_MODEL, D_MODEL), const),
                  pl.BlockSpec((1, D_MODEL), const),
                  pl.BlockSpec((2 * LANES, D_MODEL), const),
                  pl.BlockSpec((ROUTER_ROWS, tm), const)],
        out_specs=[pl.BlockSpec((tm, D_MODEL), row),
                   pl.BlockSpec((tm, PACKED), row),
                   pl.BlockSpec((tm, LANES), row),
                   pl.BlockSpec((8, tm), lambda i: (0, i)),
                   pl.BlockSpec((N_EXPERTS, LANES), const)],
        out_shape=[jax.ShapeDtypeStruct((T, D_MODEL), F32),
                   jax.ShapeDtypeStruct((T, PACKED), jnp.int32),
                   jax.ShapeDtypeStruct((T, LANES), F32),
                   jax.ShapeDtypeStruct((8, T), jnp.int32),
                   jax.ShapeDtypeStruct((N_EXPERTS, LANES), F32)],
        scratch_shapes=[pltpu.VMEM((N_EXPERTS, tm), F32)],
        compiler_params=pltpu.CompilerParams(dimension_semantics=("arbitrary",),
                                             vmem_limit_bytes=VMEM_LIMIT),
        name="outproj",
    )(x2, ya, ym, norm_mix, w_gab, wa, wm, wo, norm_w, w_rt, b_rt)


def _slots_kernel(cnt_ref, pstart_ref, blk_e_ref, nvalid_ref, nused_ref):
    R = SLOT_BLOCK
    nblk = blk_e_ref.shape[0]

    def per_expert(e, run):
        c = cnt_ref[e]
        end = run + ((c + R - 1) // R) * R
        pstart_ref[e] = run

        def set_blk(b, carry):
            blk_e_ref[b] = e
            nvalid_ref[b] = jnp.minimum(run + c - b * R, R)
            return carry
        lax.fori_loop(run // R, end // R, set_blk, 0)
        return end

    total = lax.fori_loop(0, N_EXPERTS, per_expert, 0)
    nused_ref[0] = total // R

    def tail_blk(b, carry):
        blk_e_ref[b] = N_EXPERTS - 1
        nvalid_ref[b] = 0
        return carry
    lax.fori_loop(total // R, nblk, tail_blk, 0)


def _slots(counts, nblk):
    smem = pl.BlockSpec(memory_space=pltpu.SMEM)
    return pl.pallas_call(
        _slots_kernel,
        in_specs=[smem],
        out_specs=[smem, smem, smem, smem],
        out_shape=[jax.ShapeDtypeStruct((N_EXPERTS,), jnp.int32),
                   jax.ShapeDtypeStruct((nblk,), jnp.int32),
                   jax.ShapeDtypeStruct((nblk,), jnp.int32),
                   jax.ShapeDtypeStruct((1,), jnp.int32)],
        name="slots",
    )(counts)


def _experts_kernel(blk_e_ref, nvalid_ref, nused_ref, xs_ref, wg_ref, wu_ref, wd_ref, ys_ref):
    j = pl.program_id(0)
    R = SLOT_BLOCK
    valid = j < nused_ref[0]

    @pl.when(valid)
    def _():
        rows = lax.broadcasted_iota(jnp.int32, (R, PACKED), 0)
        xw = jnp.where(rows < nvalid_ref[j], xs_ref[...], 0)
        xb = _unpack_rows(xw).astype(BF16)
        g = jnp.dot(xb, wg_ref[0], preferred_element_type=F32)
        u = jnp.dot(xb, wu_ref[0], preferred_element_type=F32)
        act = (g * _sigmoid(g) * u).astype(BF16)
        ys_ref[...] = _pack_rows(jnp.dot(act, wd_ref[0], preferred_element_type=F32))

    @pl.when(jnp.logical_not(valid))
    def _():
        ys_ref[...] = jnp.zeros_like(ys_ref)


def _experts(blk_e, nvalid, nused, xs, w_gate, w_up, w_down):
    nblk = blk_e.shape[0]
    R = SLOT_BLOCK
    wspec = lambda shape: pl.BlockSpec((1,) + shape, lambda j, be, nv, nu: (be[j], 0, 0))
    grid_spec = pltpu.PrefetchScalarGridSpec(
        num_scalar_prefetch=3,
        grid=(nblk,),
        in_specs=[pl.BlockSpec((R, PACKED), lambda j, be, nv, nu: (jnp.minimum(j, nu[0] - 1), 0)),
                  wspec((D_MODEL, D_EXPERT)),
                  wspec((D_MODEL, D_EXPERT)),
                  wspec((D_EXPERT, D_MODEL))],
        out_specs=pl.BlockSpec((R, PACKED), lambda j, be, nv, nu: (j, 0)),
    )
    return pl.pallas_call(
        _experts_kernel,
        grid_spec=grid_spec,
        out_shape=jax.ShapeDtypeStruct((nblk * R, PACKED), jnp.int32),
        compiler_params=pltpu.CompilerParams(dimension_semantics=("arbitrary",),
                                             vmem_limit_bytes=VMEM_LIMIT),
        name="experts",
    )(blk_e, nvalid, nused, xs, w_gate, w_up, w_down)


def _dest_kernel(pstart_ref, code_ref, dest_ref):
    c = code_ref[...]
    e = c >> RANK_BITS
    base = jnp.zeros_like(c)
    for k in range(N_EXPERTS):
        base = jnp.where(e == k, pstart_ref[k], base)
    dest_ref[...] = base + (c & (RANK_RADIX - 1))


def _dests(pstart, code_rows):
    T = code_rows.shape[1]
    tc = min(4096, T)
    return pl.pallas_call(
        _dest_kernel,
        grid=(T // tc,),
        in_specs=[pl.BlockSpec(memory_space=pltpu.SMEM),
                  pl.BlockSpec((8, tc), lambda i: (0, i))],
        out_specs=pl.BlockSpec((8, tc), lambda i: (0, i)),
        out_shape=jax.ShapeDtypeStruct((8, T), jnp.int32),
        name="dests",
    )(pstart, code_rows)


def _sc_workers():
    info = plsc.get_sparse_core_info()
    return info.num_cores, info.num_cores * info.num_subcores


def _sc_scatter_rows(src, idx, n_out):
    n_src, width = src.shape
    nc, nw = _sc_workers()
    per_w = n_src // nw
    ch = SC_CHUNK
    nch = per_w // ch
    assert idx.shape[0] == 2 * n_src and per_w * nw == n_src and nch * ch == per_w and nch % 2 == 0
    mesh = plsc.VectorSubcoreMesh(core_axis_name="c", subcore_axis_name="s")
    dma = pltpu.SemaphoreType.DMA

    @functools.partial(
        pl.kernel, mesh=mesh, out_type=jax.ShapeDtypeStruct((n_out, width), src.dtype),
        scratch_types=[pltpu.VMEM((per_w,), jnp.int32), pltpu.VMEM((per_w,), jnp.int32),
                       pltpu.VMEM((ch, width), src.dtype), pltpu.VMEM((ch, width), src.dtype),
                       dma, dma, dma, dma, dma, dma],
        name="sc_scatter")
    def scatter(src_hbm, idx_hbm, out_hbm, idx_a, idx_b, buf0, buf1, in0, in1, out0a, out0b, out1a, out1b):
        base = (lax.axis_index("s") * nc + lax.axis_index("c")) * per_w
        pltpu.sync_copy(idx_hbm.at[pl.ds(base, per_w)], idx_a)
        pltpu.sync_copy(idx_hbm.at[pl.ds(n_src + base, per_w)], idx_b)

        def read(c, buf, sem):
            return pltpu.make_async_copy(src_hbm.at[pl.ds(base + c * ch, ch)], buf, sem)

        def writes(c, buf, sem_a, sem_b):
            rows = pl.ds(c * ch, ch)
            return (pltpu.make_async_copy(buf, out_hbm.at[idx_a.at[rows]], sem_a),
                    pltpu.make_async_copy(buf, out_hbm.at[idx_b.at[rows]], sem_b))

        def start(copies):
            for cp in copies:
                cp.start()

        def wait(copies):
            for cp in copies:
                cp.wait()

        read(0, buf0, in0).start()

        @pl.loop(0, nch, step=2)
        def _(c):
            @pl.when(c > 0)
            def _():
                wait(writes(c - 1, buf1, out1a, out1b))
            read(c + 1, buf1, in1).start()
            read(c, buf0, in0).wait()
            start(writes(c, buf0, out0a, out0b))
            read(c + 1, buf1, in1).wait()
            start(writes(c + 1, buf1, out1a, out1b))
            wait(writes(c, buf0, out0a, out0b))

            @pl.when(c + 2 < nch)
            def _():
                read(c + 2, buf0, in0).start()

        wait(writes(nch - 1, buf1, out1a, out1b))

    return scatter(src, idx)


def _sc_gather_rows(src, idx):
    M = idx.shape[0]
    width = src.shape[1]
    nc, nw = _sc_workers()
    per_w = M // nw
    ch = SC_CHUNK
    nch = per_w // ch
    assert per_w * nw == M and nch * ch == per_w and nch % 2 == 0
    mesh = plsc.VectorSubcoreMesh(core_axis_name="c", subcore_axis_name="s")
    dma = pltpu.SemaphoreType.DMA

    @functools.partial(
        pl.kernel, mesh=mesh, out_type=jax.ShapeDtypeStruct((M, width), src.dtype),
        scratch_types=[pltpu.VMEM((per_w,), jnp.int32),
                       pltpu.VMEM((ch, width), src.dtype), pltpu.VMEM((ch, width), src.dtype),
                       dma, dma, dma, dma],
        name="sc_gather")
    def gather(src_hbm, idx_hbm, out_hbm, idx_v, buf0, buf1, in0, in1, out0, out1):
        base = (lax.axis_index("s") * nc + lax.axis_index("c")) * per_w
        pltpu.sync_copy(idx_hbm.at[pl.ds(base, per_w)], idx_v)

        def read(c, buf, sem):
            return pltpu.make_async_copy(src_hbm.at[idx_v.at[pl.ds(c * ch, ch)]], buf, sem)

        def write(c, buf, sem):
            return pltpu.make_async_copy(buf, out_hbm.at[pl.ds(base + c * ch, ch)], sem)

        read(0, buf0, in0).start()

        @pl.loop(0, nch, step=2)
        def _(c):
            @pl.when(c > 0)
            def _():
                write(c - 1, buf1, out1).wait()
            read(c + 1, buf1, in1).start()
            read(c, buf0, in0).wait()
            write(c, buf0, out0).start()
            read(c + 1, buf1, in1).wait()
            write(c + 1, buf1, out1).start()
            write(c, buf0, out0).wait()

            @pl.when(c + 2 < nch)
            def _():
                read(c + 2, buf0, in0).start()

        write(nch - 1, buf1, out1).wait()

    return gather(src, idx)


def _final_kernel(x1_ref, y1_ref, y2_ref, meta_ref, nw_ref, o_ref):
    w1 = meta_ref[:, 4:5]
    w2 = meta_ref[:, 5:6]
    x2 = x1_ref[...] + (w1 * _unpack_rows(y1_ref[...]) + w2 * _unpack_rows(y2_ref[...]))
    o_ref[...] = _rms(x2, nw_ref[...])


def _final(x1, yg, meta, norm_w):
    T = x1.shape[0]
    tm = min(COMB_TILE, T)
    nt = T // tm
    row = lambda i: (i, 0)
    return pl.pallas_call(
        _final_kernel,
        grid=(nt,),
        in_specs=[pl.BlockSpec((tm, D_MODEL), row),
                  pl.BlockSpec((tm, PACKED), row),
                  pl.BlockSpec((tm, PACKED), lambda i: (i + nt, 0)),
                  pl.BlockSpec((tm, LANES), row),
                  pl.BlockSpec((1, D_MODEL), lambda i: (0, 0))],
        out_specs=pl.BlockSpec((tm, D_MODEL), row),
        out_shape=jax.ShapeDtypeStruct((T, D_MODEL), F32),
        compiler_params=pltpu.CompilerParams(dimension_semantics=("arbitrary",),
                                             vmem_limit_bytes=VMEM_LIMIT),
        name="final",
    )(x1, yg, yg, meta, norm_w)


def _layer(x, norm_mix_w, w_in, conv_w, conv_b, b_igate, b_fgate, attn_sinks, mlstm_norm_w,
           w_attn_o, w_mlstm_o, w_out, norm_ffn_w, w_group, b_group, w_router, b_router,
           w_gate, w_up, w_down, out_norm_w):
    B, S, D = x.shape
    T = B * S
    x2 = x.reshape(T, D)

    w_tok = jnp.concatenate([w_in[:, _O_AK:_O_AV], w_in[:, _O_MK:_O_MV]], axis=1).astype(BF16)
    w_chan = jnp.concatenate([w_in[:, _O_AQ:_O_AK] * (HEAD_DIM ** -0.5), w_in[:, _O_AV:_O_MQ],
                              w_in[:, _O_MQ:_O_MK], w_in[:, _O_MV:_O_MI]], axis=1).T.astype(BF16)
    w_gates = jnp.pad(w_in[:, _O_MI:_O_GA], ((0, 0), (0, LANES - 2 * M_HEADS))).astype(BF16)
    expert_ws = [w.reshape(-1, w.shape[-1]) for w in (w_gate, w_up, w_down)]
    ak, mk, aqt, avt, mqt, mvt, mot, grow, wg16, wu16, wd16 = _inproj(
        x2, norm_mix_w.reshape(1, D), w_tok, w_chan, w_gates, expert_ws, B, S)

    ya = _attn(attn_sinks.astype(F32), aqt, ak.reshape(B, S, -1), avt)

    bias = jnp.concatenate([b_igate, b_fgate]).astype(F32)
    ym = _mlstm(mqt, mk.reshape(B, S, -1), mvt, mot, grow, conv_w, conv_b,
                bias.reshape(2 * M_HEADS, 1), mlstm_norm_w)

    gpad = jnp.zeros((8 - N_GROUPS, D), F32)
    w_rt = jnp.concatenate([w_group.T, gpad, w_router.T, jnp.zeros((LANES - ROUTER_ROWS, D), F32)], axis=0)
    w_rt_hi = w_rt.astype(BF16)
    w_rt = jnp.concatenate([w_rt_hi, (w_rt - w_rt_hi.astype(F32)).astype(BF16)], axis=0)
    b_rt = jnp.concatenate([b_group, jnp.zeros((8 - N_GROUPS,), F32), b_router]).astype(F32)
    b_rt = jnp.broadcast_to(b_rt[:, None], (ROUTER_ROWS, TOK_TILE))
    x1, h2p, meta, code_rows, cnt = _outproj(x2, ya.reshape(T, -1), ym.reshape(T, -1),
                                            norm_mix_w.reshape(1, D), w_in[:, _O_GA:].astype(BF16),
                                            w_attn_o.astype(BF16), w_mlstm_o.astype(BF16),
                                            w_out.astype(BF16), norm_ffn_w.reshape(1, D), w_rt, b_rt)

    counts = cnt[:, 0].astype(jnp.int32)
    nblk = (2 * T) // SLOT_BLOCK + N_EXPERTS
    pstart, blk_e, nvalid, nused = _slots(counts, nblk)
    dest = _dests(pstart, code_rows)[0:2].reshape(2 * T)

    xs = _sc_scatter_rows(h2p, dest, nblk * SLOT_BLOCK)
    ys = _experts(blk_e, nvalid, nused, xs, wg16.reshape(w_gate.shape), wu16.reshape(w_up.shape),
                  wd16.reshape(w_down.shape))
    yg = _sc_gather_rows(ys, dest)
    out = _final(x1, yg, meta, out_norm_w.reshape(1, D))
    return out.reshape(B, S, D)


def kernel(x, norm_mix_w, w_in, conv_w, conv_b, b_igate, b_fgate, attn_sinks, mlstm_norm_w, w_attn_o,
           w_mlstm_o, w_out, norm_ffn_w, w_group, b_group, w_router, b_router, w_gate, w_up, w_down,
           norm_final_w):
    depth = w_in.shape[0]
    assert depth == 1, "final RMSNorm is fused into the last layer's combine kernel"
    return _layer(x, norm_mix_w[0], w_in[0], conv_w[0], conv_b[0], b_igate[0], b_fgate[0],
                  attn_sinks[0], mlstm_norm_w[0], w_attn_o[0], w_mlstm_o[0], w_out[0], norm_ffn_w[0],
                  w_group[0], b_group[0], w_router[0], b_router[0], w_gate[0], w_up[0], w_down[0],
                  norm_final_w)
```

```python
import functools

import jax
import jax.numpy as jnp
from jax import lax
from jax.experimental import pallas as pl
from jax.experimental.pallas import tpu as pltpu
from jax.experimental.pallas import tpu_sc as plsc

F32 = jnp.float32
BF16 = jnp.bfloat16

D_MODEL = 1024
N_Q_HEADS = 8
N_KV_HEADS = 2
HEAD_DIM = 64
WINDOW = 128
GQA_GROUP = N_Q_HEADS // N_KV_HEADS
M_HEADS = 4
M_QK_DIM = 64
M_V_DIM = 128
CONV_WIDTH = 4
N_GROUPS = 4
EXPERTS_PER_GROUP = 8
N_EXPERTS = N_GROUPS * EXPERTS_PER_GROUP
D_EXPERT = 512
EPS = 1e-6

ATT_Q_WIDTH = N_Q_HEADS * HEAD_DIM
ATT_KV_WIDTH = N_KV_HEADS * HEAD_DIM
M_QK_WIDTH = M_HEADS * M_QK_DIM
M_V_WIDTH = M_HEADS * M_V_DIM

LANES = 128
NEG = -1e30
VMEM_LIMIT = 56 * 1024 * 1024

IN_TILE = 1024
TOK_TILE = 1024
ATT_TILE = 1024
M_CHUNK = 128
M_BATCH = 8
M_STATE_ROWS = M_V_DIM + 16
SLOT_BLOCK = 512
COMB_TILE = 1024
RANK_RADIX = 65536
RANK_BITS = 16
ROUTER_ROWS = 8 + N_EXPERTS
PACKED = D_MODEL // 2
SC_CHUNK = 64

_O_AQ = 0
_O_AK = _O_AQ + ATT_Q_WIDTH
_O_AV = _O_AK + ATT_KV_WIDTH
_O_MQ = _O_AV + ATT_KV_WIDTH
_O_MK = _O_MQ + M_QK_WIDTH
_O_MV = _O_MK + M_QK_WIDTH
_O_MO = _O_MV + M_V_WIDTH
_O_MI = _O_MO + M_V_WIDTH
_O_MF = _O_MI + M_HEADS
_O_GA = _O_MF + M_HEADS
_O_GB = _O_GA + D_MODEL
_O_END = _O_GB + D_MODEL


def _rms(x, w):
    return x * lax.rsqrt(jnp.mean(x * x, axis=-1, keepdims=True) + EPS) * w


def _sigmoid(x):
    return 0.5 * jnp.tanh(0.5 * x) + 0.5


def _pack_rows(x):
    half = x.shape[1] // 2
    bits = lax.bitcast_convert_type(x.astype(BF16).astype(F32), jnp.uint32)
    packed = (bits[:, :half] >> 16) | (bits[:, half:] & jnp.uint32(0xFFFF0000))
    return lax.bitcast_convert_type(packed, jnp.int32)


def _unpack_rows(w):
    u = lax.bitcast_convert_type(w, jnp.uint32)
    lo = lax.bitcast_convert_type(u << 16, F32)
    hi = lax.bitcast_convert_type(u & jnp.uint32(0xFFFF0000), F32)
    return jnp.concatenate([lo, hi], axis=-1)


def _log_sigmoid(x):
    return jnp.minimum(x, 0.0) - jnp.log1p(jnp.exp(-jnp.abs(x)))


_TOK_SPLITS = (ATT_KV_WIDTH, M_QK_WIDTH)
_CHAN_SPLITS = (ATT_Q_WIDTH, ATT_KV_WIDTH, M_QK_WIDTH, M_V_WIDTH, M_V_WIDTH)


def _inproj_kernel(x_ref, nw_ref, w_ref, wt_ref, wg_ref, eg_ref, eu_ref, ed_ref, *out_refs):
    tok_refs = out_refs[:len(_TOK_SPLITS)]
    chan_refs = out_refs[len(_TOK_SPLITS):len(_TOK_SPLITS) + len(_CHAN_SPLITS)]
    grow_ref = out_refs[len(_TOK_SPLITS) + len(_CHAN_SPLITS)]
    for src, dst in zip((eg_ref, eu_ref, ed_ref), out_refs[-3:]):
        dst[...] = src[...].astype(BF16)
    h = _rms(x_ref[...], nw_ref[...]).astype(BF16)
    lo = 0
    for ref, width in zip(tok_refs, _TOK_SPLITS):
        ref[...] = jnp.dot(h, w_ref[:, lo:lo + width], preferred_element_type=F32).astype(BF16)
        lo += width
    lo = 0
    for ref, width in zip(chan_refs, _CHAN_SPLITS):
        ref[0] = lax.dot_general(wt_ref[lo:lo + width, :], h, (((1,), (1,)), ((), ())),
                                 preferred_element_type=F32).astype(BF16)
        lo += width
    g = jnp.dot(h, wg_ref[...], preferred_element_type=F32)
    grow_ref[...] = g.T[0:8, :]


def _inproj(x2, norm_w, w_tok, w_chan, w_gates, expert_ws, B, S):
    T = x2.shape[0]
    tm = min(IN_TILE, S)
    tpb = S // tm
    nsteps = T // tm
    row = lambda i: (i, 0)
    const = lambda i: (0, 0)
    chan = lambda i: (i // tpb, 0, i % tpb)
    cast_specs = [pl.BlockSpec((w.shape[0] // nsteps, w.shape[1]), row) for w in expert_ws]
    assert all(w.shape[0] % (8 * nsteps) == 0 for w in expert_ws)
    return pl.pallas_call(
        _inproj_kernel,
        grid=(nsteps,),
        in_specs=[pl.BlockSpec((tm, D_MODEL), row),
                  pl.BlockSpec((1, D_MODEL), const),
                  pl.BlockSpec(w_tok.shape, const),
                  pl.BlockSpec(w_chan.shape, const),
                  pl.BlockSpec((D_MODEL, LANES), const),
                  *cast_specs],
        out_specs=[*[pl.BlockSpec((tm, w), row) for w in _TOK_SPLITS],
                   *[pl.BlockSpec((1, w, tm), chan) for w in _CHAN_SPLITS],
                   pl.BlockSpec((8, tm), lambda i: (0, i)),
                   *cast_specs],
        out_shape=[*[jax.ShapeDtypeStruct((T, w), BF16) for w in _TOK_SPLITS],
                   *[jax.ShapeDtypeStruct((B, w, S), BF16) for w in _CHAN_SPLITS],
                   jax.ShapeDtypeStruct((8, T), F32),
                   *[jax.ShapeDtypeStruct(w.shape, BF16) for w in expert_ws]],
        compiler_params=pltpu.CompilerParams(dimension_semantics=("arbitrary",),
                                             vmem_limit_bytes=VMEM_LIMIT),
        name="inproj",
    )(x2, norm_w, w_tok, w_chan, w_gates, *expert_ws)


def _attn_kernel(tq, sink_ref, qt_ref, k_ref, kp_ref, vt_ref, vtp_ref, o_ref):
    i = pl.program_id(1)
    W = WINDOW
    G = GQA_GROUP
    u = lax.broadcasted_iota(jnp.int32, (W, W), 0)
    t = lax.broadcasted_iota(jnp.int32, (W, W), 1)
    from_prev = u > t

    def keys_values(j):
        if j == 0:
            return kp_ref[0], vtp_ref[0], k_ref[0, 0:W, :], vt_ref[0, :, 0:W]
        return (k_ref[0, (j - 1) * W:j * W, :], vt_ref[0, :, (j - 1) * W:j * W],
                k_ref[0, j * W:(j + 1) * W, :], vt_ref[0, :, j * W:(j + 1) * W])

    def scores(j, g):
        k_prev, _, k_cur, _ = keys_values(j)
        dims = slice(g * HEAD_DIM, (g + 1) * HEAD_DIM)
        qg = jnp.concatenate([qt_ref[0, h * HEAD_DIM:(h + 1) * HEAD_DIM, j * W:(j + 1) * W]
                              for h in range(g * G, (g + 1) * G)], axis=1)
        return (jnp.dot(k_prev[:, dims], qg, preferred_element_type=F32),
                jnp.dot(k_cur[:, dims], qg, preferred_element_type=F32))

    tasks = [(j, g) for j in range(tq // W) for g in range(N_KV_HEADS)]
    ahead = scores(*tasks[0])
    pair_rows = []
    for n_task, (j, g) in enumerate(tasks):
        s_prev, s_cur = ahead
        if n_task + 1 < len(tasks):
            ahead = scores(*tasks[n_task + 1])
        cols = slice(j * W, (j + 1) * W)
        _, vt_prev, _, vt_cur = keys_values(j)
        dims = slice(g * HEAD_DIM, (g + 1) * HEAD_DIM)
        for n in range(G):
            hc = slice(n * W, (n + 1) * W)
            sp = s_prev[:, hc]
            if j == 0:
                sp = jnp.where(i > 0, sp, NEG)
            s = jnp.where(from_prev, sp, s_cur[:, hc])
            sink = sink_ref[g * G + n]
            m = jnp.maximum(jnp.max(s, axis=0, keepdims=True), sink)
            p = jnp.exp(s - m)
            inv = 1.0 / (jnp.sum(p, axis=0, keepdims=True) + jnp.exp(sink - m))
            p_prev = jnp.where(from_prev, p, 0.0).astype(BF16)
            p_cur = jnp.where(from_prev, 0.0, p).astype(BF16)
            o = jnp.dot(vt_prev[dims, :], p_prev, preferred_element_type=F32) \
                + jnp.dot(vt_cur[dims, :], p_cur, preferred_element_type=F32)
            pair_rows.append(o * inv)
        if g == N_KV_HEADS - 1:
            out = [jnp.concatenate(pair_rows[n:n + 2], axis=0).T for n in range(0, N_Q_HEADS, 2)]
            o_ref[0, cols, :] = jnp.concatenate(out, axis=1).astype(BF16)
            pair_rows = []


def _attn(sinks, qt, k3, vt):
    B, S, _ = k3.shape
    tq = min(ATT_TILE, S)
    per = tq // WINDOW
    chan = lambda b, i: (b, 0, i)
    tok = lambda b, i: (b, i, 0)
    return pl.pallas_call(
        functools.partial(_attn_kernel, tq),
        grid=(B, S // tq),
        in_specs=[pl.BlockSpec(memory_space=pltpu.SMEM),
                  pl.BlockSpec((1, ATT_Q_WIDTH, tq), chan),
                  pl.BlockSpec((1, tq, ATT_KV_WIDTH), tok),
                  pl.BlockSpec((1, WINDOW, ATT_KV_WIDTH), lambda b, i: (b, jnp.maximum(i * per - 1, 0), 0)),
                  pl.BlockSpec((1, ATT_KV_WIDTH, tq), chan),
                  pl.BlockSpec((1, ATT_KV_WIDTH, WINDOW), lambda b, i: (b, 0, jnp.maximum(i * per - 1, 0)))],
        out_specs=pl.BlockSpec((1, tq, ATT_Q_WIDTH), tok),
        out_shape=jax.ShapeDtypeStruct((B, S, ATT_Q_WIDTH), BF16),
        compiler_params=pltpu.CompilerParams(dimension_semantics=("arbitrary", "arbitrary"),
                                             vmem_limit_bytes=VMEM_LIMIT),
        name="attn",
    )(sinks, qt, k3, k3, vt, vt)


def _mlstm_kernel(nb, *refs):
    L = M_CHUNK
    H = M_HEADS
    mqt_ref, mk_ref, mvt_ref, mot_ref = refs[:4]
    grow_refs = refs[4:4 + nb]
    (cwq_ref, cbq_ref, cwk_ref, cbk_ref, brow_ref, nw_ref, y_ref,
     state_ref, m_ref, prevq_ref, ubuf_ref) = refs[4 + nb:]
    cidx = pl.program_id(1)

    @pl.when(cidx == 0)
    def _():
        state_ref[...] = jnp.zeros_like(state_ref)
        m_ref[...] = jnp.zeros_like(m_ref)
        prevq_ref[...] = jnp.zeros_like(prevq_ref)
        ubuf_ref[:, 0:8, :] = jnp.zeros((nb, 8, M_QK_WIDTH), F32)

    ri = lax.broadcasted_iota(jnp.int32, (L, L), 0)
    ci = lax.broadcasted_iota(jnp.int32, (L, L), 1)
    causal_t = ri <= ci
    triu = jnp.where(causal_t, 1.0, 0.0).astype(F32)
    lane = lax.broadcasted_iota(jnp.int32, (8, L), 1)
    r2 = lax.broadcasted_iota(jnp.int32, (2 * L, L), 0)
    c2 = lax.broadcasted_iota(jnp.int32, (2 * L, L), 1)
    shifts = [jnp.where(r2 - c2 == L - k, 1.0, 0.0).astype(BF16) for k in range(1, CONV_WIDTH)]
    ones_rows = jnp.where(lax.broadcasted_iota(jnp.int32, (M_STATE_ROWS - M_V_DIM, L), 0) == 0,
                          1.0, 0.0).astype(BF16)

    pairs = [(bb, h) for bb in range(nb) for h in range(H)]
    states = [state_ref[bb * H + h] for bb, h in pairs]
    m_prevs = [m_ref[bb] for bb in range(nb)]

    def prepare(bb):
        cur = mqt_ref[bb]
        both = jnp.concatenate([prevq_ref[bb], cur], axis=1)
        acc = cbq_ref[...] + cwq_ref[CONV_WIDTH - 1] * cur.astype(F32)
        for k in range(1, CONV_WIDTH):
            acc = acc + cwq_ref[CONV_WIDTH - 1 - k] * jnp.dot(both, shifts[k - 1],
                                                              preferred_element_type=F32)
        qt = (acc * _sigmoid(acc)).astype(BF16)
        ubuf_ref[bb, 8:L + 8, :] = mk_ref[bb].astype(F32)
        acc = cbk_ref[...] + cwk_ref[CONV_WIDTH - 1:CONV_WIDTH, :] * ubuf_ref[bb, 8:L + 8, :]
        for j in range(CONV_WIDTH - 1):
            off = 8 - (CONV_WIDTH - 1) + j
            acc = acc + cwk_ref[j:j + 1, :] * ubuf_ref[bb, off:off + L, :]
        ubuf_ref[bb, 0:8, :] = ubuf_ref[bb, L:L + 8, :]
        kk = (acc * _sigmoid(acc) * (M_QK_DIM ** -0.5)).astype(BF16)

        gr = grow_refs[bb][...] + brow_ref[...]
        b = jnp.dot(_log_sigmoid(gr), triu, preferred_element_type=F32,
                    precision=lax.Precision.HIGHEST)
        gi = pltpu.roll(gr, 4, axis=0)
        u = gi - b
        cm = u
        for sh in (1, 2, 4, 8, 16, 32, 64):
            if sh < L:
                cm = jnp.maximum(cm, jnp.where(lane >= sh, pltpu.roll(cm, sh, axis=1), NEG))
        m_prev = m_prevs[bb]
        mt = b + jnp.maximum(m_prev, cm)
        wa = jnp.exp(b + m_prev - mt)
        emt = jnp.exp(-mt)
        m_new = jnp.broadcast_to(mt[:, L - 1:L], (8, L))
        b_last = jnp.broadcast_to(b[:, L - 1:L], (8, L))
        wc = jnp.exp(b_last + m_prev - m_new)
        ws = jnp.exp(b_last + u - m_new)
        return qt, kk, b - mt, u.T, wa, emt, wc, ws, m_new

    seqs = [prepare(bb) for bb in range(nb)]

    def products(bb, h, state):
        qt, kk = seqs[bb][0], seqs[bb][1]
        qh = qt[h * M_QK_DIM:(h + 1) * M_QK_DIM, :]
        kh = kk[:, h * M_QK_DIM:(h + 1) * M_QK_DIM]
        return (kh, jnp.dot(kh, qh, preferred_element_type=F32),
                jnp.dot(state.astype(BF16), qh, preferred_element_type=F32))

    new_states, outs = [], []
    ahead = products(*pairs[0], states[0])
    for n, ((bb, h), state) in enumerate(zip(pairs, states)):
        kh, st, inter = ahead
        if n + 1 < len(pairs):
            ahead = products(*pairs[n + 1], states[n + 1])
        _, _, bmt, ucol, wa, emt, wc, ws, _ = seqs[bb]
        row = slice(H + h, H + h + 1)
        wq = jnp.exp(jnp.where(causal_t, bmt[row, :] + ucol[:, H + h:H + h + 1], NEG))
        sw = (st * wq).astype(BF16)
        vext = jnp.concatenate([mvt_ref[bb, h * M_V_DIM:(h + 1) * M_V_DIM, :], ones_rows], axis=0)
        res = wa[row, :] * inter + jnp.dot(vext, sw, preferred_element_type=F32)
        num = res[:M_V_DIM, :]
        den = res[M_V_DIM:M_V_DIM + 1, :]
        hb = num / jnp.maximum(jnp.abs(den), emt[row, :])
        vw = (vext.astype(F32) * ws[row, :]).astype(BF16)
        new_states.append(wc[row, 0:1] * state + jnp.dot(vw, kh, preferred_element_type=F32))
        hn = hb * lax.rsqrt(jnp.mean(hb * hb, axis=0, keepdims=True) + EPS)
        hn = hn * nw_ref[h * M_V_DIM:(h + 1) * M_V_DIM, :]
        og = _sigmoid(mot_ref[bb, h * M_V_DIM:(h + 1) * M_V_DIM, :].astype(F32))
        outs.append((og * hn).T.astype(BF16))

    for i, (bb, h) in enumerate(pairs):
        state_ref[bb * H + h] = new_states[i]
    for bb in range(nb):
        m_ref[bb] = seqs[bb][8]
        prevq_ref[bb] = mqt_ref[bb]
        y_ref[bb] = jnp.concatenate(outs[bb * H:(bb + 1) * H], axis=1)


def _mlstm(mqt, mk3, mvt, mot, grow, conv_w, conv_b, bias_row, norm_w):
    B, S, _ = mk3.shape
    L = M_CHUNK
    nb = max(d for d in range(1, M_BATCH + 1) if B % d == 0)
    nc = S // L
    assert L == LANES, "per-head scalars are kept lane-replicated next to (8, L) gate rows"
    tok = lambda b, c: (b, c, 0)
    chan = lambda b, c: (b, 0, c)
    const2 = lambda b, c: (0, 0)
    const3 = lambda b, c: (0, 0, 0)
    grow_specs = [pl.BlockSpec((8, L), functools.partial(lambda b, c, n: (0, (b * nb + n) * nc + c), n=n))
                  for n in range(nb)]
    rep = lambda v: jnp.broadcast_to(v.astype(F32)[..., None], v.shape + (L,))
    cwq, cbq = rep(conv_w[:, :M_QK_WIDTH]), rep(conv_b[:M_QK_WIDTH])
    cwk, cbk = conv_w[:, M_QK_WIDTH:].astype(F32), conv_b[M_QK_WIDTH:].reshape(1, -1).astype(F32)
    return pl.pallas_call(
        functools.partial(_mlstm_kernel, nb),
        grid=(B // nb, nc),
        in_specs=[pl.BlockSpec((nb, M_QK_WIDTH, L), chan),
                  pl.BlockSpec((nb, L, M_QK_WIDTH), tok),
                  pl.BlockSpec((nb, M_V_WIDTH, L), chan),
                  pl.BlockSpec((nb, M_V_WIDTH, L), chan),
                  *grow_specs,
                  pl.BlockSpec((CONV_WIDTH, M_QK_WIDTH, L), const3),
                  pl.BlockSpec((M_QK_WIDTH, L), const2),
                  pl.BlockSpec((CONV_WIDTH, M_QK_WIDTH), const2),
                  pl.BlockSpec((1, M_QK_WIDTH), const2),
                  pl.BlockSpec((8, 1), const2),
                  pl.BlockSpec((M_V_WIDTH, L), const2)],
        out_specs=pl.BlockSpec((nb, L, M_V_WIDTH), tok),
        out_shape=jax.ShapeDtypeStruct((B, S, M_V_WIDTH), BF16),
        scratch_shapes=[pltpu.VMEM((nb * M_HEADS, M_STATE_ROWS, M_QK_DIM), F32),
                        pltpu.VMEM((nb, 8, LANES), F32),
                        pltpu.VMEM((nb, M_QK_WIDTH, L), BF16),
                        pltpu.VMEM((nb, L + 8, M_QK_WIDTH), F32)],
        compiler_params=pltpu.CompilerParams(dimension_semantics=("arbitrary", "arbitrary"),
                                             vmem_limit_bytes=VMEM_LIMIT),
        name="mlstm",
    )(mqt, mk3, mvt, mot, *([grow] * nb), cwq, cbq, cwk, cbk, bias_row, rep(norm_w))


def _outproj_kernel(x_ref, ya_ref, ym_ref, nmix_ref, wgab_ref, wa_ref, wm_ref, wo_ref, nw_ref, wr_ref, br_ref,
                    x1_ref, h2_ref, meta_ref, code_ref, cnt_ref, cnt_scr):
    tm = x_ref.shape[0]
    i = pl.program_id(0)

    @pl.when(i == 0)
    def _():
        cnt_scr[...] = jnp.zeros_like(cnt_scr)

    pa = jnp.dot(ya_ref[...], wa_ref[...], preferred_element_type=F32)
    pm = jnp.dot(ym_ref[...], wm_ref[...], preferred_element_type=F32)
    x = x_ref[...]
    h1 = _rms(x, nmix_ref[...]).astype(BF16)
    ga = jnp.dot(h1, wgab_ref[:, :D_MODEL], preferred_element_type=F32)
    gb = jnp.dot(h1, wgab_ref[:, D_MODEL:], preferred_element_type=F32)
    mix = _sigmoid(ga) * pa + _sigmoid(gb) * pm
    x1 = x + jnp.dot(mix.astype(BF16), wo_ref[...], preferred_element_type=F32)
    x1_ref[...] = x1
    h2 = _rms(x1, nw_ref[...])
    h2_ref[...] = _pack_rows(h2)

    nt = (((1,), (1,)), ((), ()))
    h_hi = h2.astype(BF16)
    h_lo = (h2 - h_hi.astype(F32)).astype(BF16)
    hw = lax.dot_general(wr_ref[...], h_hi, nt, preferred_element_type=F32)
    logits = (hw[:LANES] + hw[LANES:]
              + lax.dot_general(wr_ref[:LANES, :], h_lo, nt, preferred_element_type=F32))[:ROUTER_ROWS]
    logits = logits + br_ref[...]
    row8 = lax.broadcasted_iota(jnp.int32, (8, tm), 0)

    def first_argmax(v):
        mx = jnp.max(v, axis=0, keepdims=True)
        idx = jnp.min(jnp.where(v == mx, row8, 8), axis=0, keepdims=True)
        return mx, idx

    gl = jnp.where(row8 < N_GROUPS, logits[0:8], NEG)
    gmax, gi = first_argmax(gl)
    gp = 1.0 / jnp.sum(jnp.exp(gl - gmax), axis=0, keepdims=True)
    el = jnp.zeros((8, tm), F32)
    for g in range(N_GROUPS):
        el = jnp.where(gi == g, logits[8 + 8 * g:16 + 8 * g], el)
    v1, j1 = first_argmax(el)
    v2, j2 = first_argmax(jnp.where(row8 == j1, NEG, el))
    t = jnp.exp(v2 - v1)
    w1 = gp / (1.0 + t)
    w2 = gp * t / (1.0 + t)
    e1 = gi * EXPERTS_PER_GROUP + j1
    e2 = gi * EXPERTS_PER_GROUP + j2

    erow = lax.broadcasted_iota(jnp.int32, (N_EXPERTS, tm), 0)
    hit1 = erow == e1
    hit2 = erow == e2
    onehot = jnp.where(hit1 | hit2, 1.0, 0.0)
    ri = lax.broadcasted_iota(jnp.int32, (tm, tm), 0)
    ci = lax.broadcasted_iota(jnp.int32, (tm, tm), 1)
    before = jnp.where(ri < ci, 1.0, 0.0).astype(BF16)
    rank = jnp.dot(onehot.astype(BF16), before, preferred_element_type=F32) + cnt_scr[...]
    r1 = jnp.sum(jnp.where(hit1, rank, 0.0), axis=0, keepdims=True)
    r2 = jnp.sum(jnp.where(hit2, rank, 0.0), axis=0, keepdims=True)
    cnt = jnp.broadcast_to((rank + onehot)[:, tm - 1:tm], (N_EXPERTS, tm))
    cnt_scr[...] = cnt
    cnt_ref[...] = cnt[:, :LANES]

    c1 = e1 * RANK_RADIX + r1.astype(jnp.int32)
    c2 = e2 * RANK_RADIX + r2.astype(jnp.int32)
    code_ref[...] = jnp.where(row8 == 0, c1, jnp.where(row8 == 1, c2, 0))
    wrow = lax.broadcasted_iota(jnp.int32, (LANES, tm), 0)
    meta_ref[...] = jnp.where(wrow == 4, w1, jnp.where(wrow == 5, w2, 0.0)).T


def _outproj(x2, ya, ym, norm_mix, w_gab, wa, wm, wo, norm_w, w_rt, b_rt):
    T = x2.shape[0]
    tm = b_rt.shape[1]
    row = lambda i: (i, 0)
    const = lambda i: (0, 0)
    resident = lambda shape: pl.BlockSpec(shape, const, pipeline_mode=pl.Buffered(1))
    return pl.pallas_call(
        _outproj_kernel,
        grid=(T // tm,),
        in_specs=[pl.BlockSpec((tm, D_MODEL), row),
                  pl.BlockSpec((tm, ATT_Q_WIDTH), row),
                  pl.BlockSpec((tm, M_V_WIDTH), row),
                  resident((1, D_MODEL)),
                  resident((D_MODEL, 2 * D_MODEL)),
                  resident((ATT_Q_WIDTH, D_MODEL)),
                  resident((M_V_WIDTH, D_MODEL)),
                  resident((D_MODEL, D_MODEL)),
                  resident((1, D_MODEL)),
                  resident((2 * LANES, D_MODEL)),
                  resident((ROUTER_ROWS, tm))],
        out_specs=[pl.BlockSpec((tm, D_MODEL), row),
                   pl.BlockSpec((tm, PACKED), row),
                   pl.BlockSpec((tm, LANES), row),
                   pl.BlockSpec((8, tm), lambda i: (0, i)),
                   pl.BlockSpec((N_EXPERTS, LANES), const)],
        out_shape=[jax.ShapeDtypeStruct((T, D_MODEL), F32),
                   jax.ShapeDtypeStruct((T, PACKED), jnp.int32),
                   jax.ShapeDtypeStruct((T, LANES), F32),
                   jax.ShapeDtypeStruct((8, T), jnp.int32),
                   jax.ShapeDtypeStruct((N_EXPERTS, LANES), F32)],
        scratch_shapes=[pltpu.VMEM((N_EXPERTS, tm), F32)],
        compiler_params=pltpu.CompilerParams(dimension_semantics=("arbitrary",),
                                             vmem_limit_bytes=VMEM_LIMIT),
        name="outproj",
    )(x2, ya, ym, norm_mix, w_gab, wa, wm, wo, norm_w, w_rt, b_rt)


def _slots_kernel(cnt_ref, pstart_ref, blk_e_ref, nvalid_ref, nused_ref):
    R = SLOT_BLOCK
    nblk = blk_e_ref.shape[0]

    def per_expert(e, run):
        c = cnt_ref[e]
        end = run + ((c + R - 1) // R) * R
        pstart_ref[e] = run

        def set_blk(b, carry):
            blk_e_ref[b] = e
            nvalid_ref[b] = jnp.minimum(run + c - b * R, R)
            return carry
        lax.fori_loop(run // R, end // R, set_blk, 0)
        return end

    total = lax.fori_loop(0, N_EXPERTS, per_expert, 0)
    nused_ref[0] = total // R

    def tail_blk(b, carry):
        blk_e_ref[b] = N_EXPERTS - 1
        nvalid_ref[b] = 0
        return carry
    lax.fori_loop(total // R, nblk, tail_blk, 0)


def _slots(counts, nblk):
    smem = pl.BlockSpec(memory_space=pltpu.SMEM)
    return pl.pallas_call(
        _slots_kernel,
        in_specs=[smem],
        out_specs=[smem, smem, smem, smem],
        out_shape=[jax.ShapeDtypeStruct((N_EXPERTS,), jnp.int32),
                   jax.ShapeDtypeStruct((nblk,), jnp.int32),
                   jax.ShapeDtypeStruct((nblk,), jnp.int32),
                   jax.ShapeDtypeStruct((1,), jnp.int32)],
        name="slots",
    )(counts)


def _experts_kernel(blk_e_ref, nvalid_ref, nused_ref, xs_ref, wg_ref, wu_ref, wd_ref, ys_ref):
    j = pl.program_id(0)
    R = SLOT_BLOCK
    valid = j < nused_ref[0]

    @pl.when(valid)
    def _():
        rows = lax.broadcasted_iota(jnp.int32, (R, PACKED), 0)
        xw = jnp.where(rows < nvalid_ref[j], xs_ref[...], 0)
        xb = _unpack_rows(xw).astype(BF16)
        g = jnp.dot(xb, wg_ref[0], preferred_element_type=F32)
        u = jnp.dot(xb, wu_ref[0], preferred_element_type=F32)
        act = (g * _sigmoid(g) * u).astype(BF16)
        ys_ref[...] = _pack_rows(jnp.dot(act, wd_ref[0], preferred_element_type=F32))


def _experts(blk_e, nvalid, nused, xs, w_gate, w_up, w_down):
    nblk = blk_e.shape[0]
    R = SLOT_BLOCK
    wspec = lambda shape: pl.BlockSpec((1,) + shape, lambda j, be, nv, nu: (be[j], 0, 0))
    grid_spec = pltpu.PrefetchScalarGridSpec(
        num_scalar_prefetch=3,
        grid=(nblk,),
        in_specs=[pl.BlockSpec((R, PACKED), lambda j, be, nv, nu: (jnp.minimum(j, nu[0] - 1), 0)),
                  wspec((D_MODEL, D_EXPERT)),
                  wspec((D_MODEL, D_EXPERT)),
                  wspec((D_EXPERT, D_MODEL))],
        out_specs=pl.BlockSpec((R, PACKED), lambda j, be, nv, nu: (jnp.minimum(j, nu[0] - 1), 0)),
    )
    return pl.pallas_call(
        _experts_kernel,
        grid_spec=grid_spec,
        out_shape=jax.ShapeDtypeStruct((nblk * R, PACKED), jnp.int32),
        compiler_params=pltpu.CompilerParams(dimension_semantics=("arbitrary",),
                                             vmem_limit_bytes=VMEM_LIMIT),
        name="experts",
    )(blk_e, nvalid, nused, xs, w_gate, w_up, w_down)


def _dest_kernel(pstart_ref, code_ref, dest_ref):
    c = code_ref[...]
    e = c >> RANK_BITS
    base = jnp.zeros_like(c)
    for k in range(N_EXPERTS):
        base = jnp.where(e == k, pstart_ref[k], base)
    dest_ref[...] = base + (c & (RANK_RADIX - 1))


def _dests(pstart, code_rows):
    T = code_rows.shape[1]
    tc = min(4096, T)
    return pl.pallas_call(
        _dest_kernel,
        grid=(T // tc,),
        in_specs=[pl.BlockSpec(memory_space=pltpu.SMEM),
                  pl.BlockSpec((8, tc), lambda i: (0, i))],
        out_specs=pl.BlockSpec((8, tc), lambda i: (0, i)),
        out_shape=jax.ShapeDtypeStruct((8, T), jnp.int32),
        name="dests",
    )(pstart, code_rows)


def _sc_workers():
    info = plsc.get_sparse_core_info()
    return info.num_cores, info.num_cores * info.num_subcores


def _sc_scatter_rows(src, idx, n_out):
    n_src, width = src.shape
    nc, nw = _sc_workers()
    per_w = n_src // nw
    ch = SC_CHUNK
    nch = per_w // ch
    assert idx.shape[0] == 2 * n_src and per_w * nw == n_src and nch * ch == per_w and nch % 2 == 0
    mesh = plsc.VectorSubcoreMesh(core_axis_name="c", subcore_axis_name="s")
    dma = pltpu.SemaphoreType.DMA

    @functools.partial(
        pl.kernel, mesh=mesh, out_type=jax.ShapeDtypeStruct((n_out, width), src.dtype),
        scratch_types=[pltpu.VMEM((per_w,), jnp.int32), pltpu.VMEM((per_w,), jnp.int32),
                       pltpu.VMEM((ch, width), src.dtype), pltpu.VMEM((ch, width), src.dtype),
                       dma, dma, dma, dma, dma, dma],
        name="sc_scatter")
    def scatter(src_hbm, idx_hbm, out_hbm, idx_a, idx_b, buf0, buf1, in0, in1, out0a, out0b, out1a, out1b):
        base = (lax.axis_index("s") * nc + lax.axis_index("c")) * per_w
        pltpu.sync_copy(idx_hbm.at[pl.ds(base, per_w)], idx_a)
        pltpu.sync_copy(idx_hbm.at[pl.ds(n_src + base, per_w)], idx_b)

        def read(c, buf, sem):
            return pltpu.make_async_copy(src_hbm.at[pl.ds(base + c * ch, ch)], buf, sem)

        def writes(c, buf, sem_a, sem_b):
            rows = pl.ds(c * ch, ch)
            return (pltpu.make_async_copy(buf, out_hbm.at[idx_a.at[rows]], sem_a),
                    pltpu.make_async_copy(buf, out_hbm.at[idx_b.at[rows]], sem_b))

        def start(copies):
            for cp in copies:
                cp.start()

        def wait(copies):
            for cp in copies:
                cp.wait()

        read(0, buf0, in0).start()

        @pl.loop(0, nch, step=2)
        def _(c):
            @pl.when(c > 0)
            def _():
                wait(writes(c - 1, buf1, out1a, out1b))
            read(c + 1, buf1, in1).start()
            read(c, buf0, in0).wait()
            start(writes(c, buf0, out0a, out0b))
            read(c + 1, buf1, in1).wait()
            start(writes(c + 1, buf1, out1a, out1b))
            wait(writes(c, buf0, out0a, out0b))

            @pl.when(c + 2 < nch)
            def _():
                read(c + 2, buf0, in0).start()

        wait(writes(nch - 1, buf1, out1a, out1b))

    return scatter(src, idx)


def _sc_gather_rows(src, idx):
    M = idx.shape[0]
    width = src.shape[1]
    nc, nw = _sc_workers()
    per_w = M // nw
    ch = SC_CHUNK
    nch = per_w // ch
    assert per_w * nw == M and nch * ch == per_w and nch % 2 == 0
    mesh = plsc.VectorSubcoreMesh(core_axis_name="c", subcore_axis_name="s")
    dma = pltpu.SemaphoreType.DMA

    @functools.partial(
        pl.kernel, mesh=mesh, out_type=jax.ShapeDtypeStruct((M, width), src.dtype),
        scratch_types=[pltpu.VMEM((per_w,), jnp.int32),
                       pltpu.VMEM((ch, width), src.dtype), pltpu.VMEM((ch, width), src.dtype),
                       dma, dma, dma, dma],
        name="sc_gather")
    def gather(src_hbm, idx_hbm, out_hbm, idx_v, buf0, buf1, in0, in1, out0, out1):
        base = (lax.axis_index("s") * nc + lax.axis_index("c")) * per_w
        pltpu.sync_copy(idx_hbm.at[pl.ds(base, per_w)], idx_v)

        def read(c, buf, sem):
            return pltpu.make_async_copy(src_hbm.at[idx_v.at[pl.ds(c * ch, ch)]], buf, sem)

        def write(c, buf, sem):
            return pltpu.make_async_copy(buf, out_hbm.at[pl.ds(base + c * ch, ch)], sem)

        read(0, buf0, in0).start()

        @pl.loop(0, nch, step=2)
        def _(c):
            @pl.when(c > 0)
            def _():
                write(c - 1, buf1, out1).wait()
            read(c + 1, buf1, in1).start()
            read(c, buf0, in0).wait()
            write(c, buf0, out0).start()
            read(c + 1, buf1, in1).wait()
            write(c + 1, buf1, out1).start()
            write(c, buf0, out0).wait()

            @pl.when(c + 2 < nch)
            def _():
                read(c + 2, buf0, in0).start()

        write(nch - 1, buf1, out1).wait()

    return gather(src, idx)


def _final_kernel(x1_ref, y1_ref, y2_ref, meta_ref, nw_ref, o_ref):
    w1 = meta_ref[:, 4:5]
    w2 = meta_ref[:, 5:6]
    x2 = x1_ref[...] + (w1 * _unpack_rows(y1_ref[...]) + w2 * _unpack_rows(y2_ref[...]))
    o_ref[...] = _rms(x2, nw_ref[...])


def _final(x1, yg, meta, norm_w):
    T = x1.shape[0]
    tm = min(COMB_TILE, T)
    nt = T // tm
    row = lambda i: (i, 0)
    return pl.pallas_call(
        _final_kernel,
        grid=(nt,),
        in_specs=[pl.BlockSpec((tm, D_MODEL), row),
                  pl.BlockSpec((tm, PACKED), row),
                  pl.BlockSpec((tm, PACKED), lambda i: (i + nt, 0)),
                  pl.BlockSpec((tm, LANES), row),
                  pl.BlockSpec((1, D_MODEL), lambda i: (0, 0))],
        out_specs=pl.BlockSpec((tm, D_MODEL), row),
        out_shape=jax.ShapeDtypeStruct((T, D_MODEL), F32),
        compiler_params=pltpu.CompilerParams(dimension_semantics=("arbitrary",),
                                             vmem_limit_bytes=VMEM_LIMIT),
        name="final",
    )(x1, yg, yg, meta, norm_w)


def _layer(x, norm_mix_w, w_in, conv_w, conv_b, b_igate, b_fgate, attn_sinks, mlstm_norm_w,
           w_attn_o, w_mlstm_o, w_out, norm_ffn_w, w_group, b_group, w_router, b_router,
           w_gate, w_up, w_down, out_norm_w):
    B, S, D = x.shape
    T = B * S
    x2 = x.reshape(T, D)

    w_tok = jnp.concatenate([w_in[:, _O_AK:_O_AV], w_in[:, _O_MK:_O_MV]], axis=1).astype(BF16)
    w_chan = jnp.concatenate([w_in[:, _O_AQ:_O_AK] * (HEAD_DIM ** -0.5), w_in[:, _O_AV:_O_MQ],
                              w_in[:, _O_MQ:_O_MK], w_in[:, _O_MV:_O_MI]], axis=1).T.astype(BF16)
    w_gates = jnp.pad(w_in[:, _O_MI:_O_GA], ((0, 0), (0, LANES - 2 * M_HEADS))).astype(BF16)
    expert_ws = [w.reshape(-1, w.shape[-1]) for w in (w_gate, w_up, w_down)]
    ak, mk, aqt, avt, mqt, mvt, mot, grow, wg16, wu16, wd16 = _inproj(
        x2, norm_mix_w.reshape(1, D), w_tok, w_chan, w_gates, expert_ws, B, S)

    ya = _attn(attn_sinks.astype(F32), aqt, ak.reshape(B, S, -1), avt)

    bias = jnp.concatenate([b_igate, b_fgate]).astype(F32)
    ym = _mlstm(mqt, mk.reshape(B, S, -1), mvt, mot, grow, conv_w, conv_b,
                bias.reshape(2 * M_HEADS, 1), mlstm_norm_w)

    gpad = jnp.zeros((8 - N_GROUPS, D), F32)
    w_rt = jnp.concatenate([w_group.T, gpad, w_router.T, jnp.zeros((LANES - ROUTER_ROWS, D), F32)], axis=0)
    w_rt_hi = w_rt.astype(BF16)
    w_rt = jnp.concatenate([w_rt_hi, (w_rt - w_rt_hi.astype(F32)).astype(BF16)], axis=0)
    b_rt = jnp.concatenate([b_group, jnp.zeros((8 - N_GROUPS,), F32), b_router]).astype(F32)
    b_rt = jnp.broadcast_to(b_rt[:, None], (ROUTER_ROWS, min(TOK_TILE, T)))
    x1, h2p, meta, code_rows, cnt = _outproj(x2, ya.reshape(T, -1), ym.reshape(T, -1),
                                            norm_mix_w.reshape(1, D), w_in[:, _O_GA:].astype(BF16),
                                            w_attn_o.astype(BF16), w_mlstm_o.astype(BF16),
                                            w_out.astype(BF16), norm_ffn_w.reshape(1, D), w_rt, b_rt)

    counts = cnt[:, 0].astype(jnp.int32)
    nblk = (2 * T) // SLOT_BLOCK + N_EXPERTS
    pstart, blk_e, nvalid, nused = _slots(counts, nblk)
    dest = _dests(pstart, code_rows)[0:2].reshape(2 * T)

    xs = _sc_scatter_rows(h2p, dest, nblk * SLOT_BLOCK)
    ys = _experts(blk_e, nvalid, nused, xs, wg16.reshape(w_gate.shape), wu16.reshape(w_up.shape),
                  wd16.reshape(w_down.shape))
    yg = _sc_gather_rows(ys, dest)
    out = _final(x1, yg, meta, out_norm_w.reshape(1, D))
    return out.reshape(B, S, D)


def kernel(x, norm_mix_w, w_in, conv_w, conv_b, b_igate, b_fgate, attn_sinks, mlstm_norm_w, w_attn_o,
           w_mlstm_o, w_out, norm_ffn_w, w_group, b_group, w_router, b_router, w_gate, w_up, w_down,
           norm_final_w):
    depth = w_in.shape[0]
    assert depth == 1, "final RMSNorm is fused into the last layer's combine kernel"
    return _layer(x, norm_mix_w[0], w_in[0], conv_w[0], conv_b[0], b_igate[0], b_fgate[0],
                  attn_sinks[0], mlstm_norm_w[0], w_attn_o[0], w_mlstm_o[0], w_out[0], norm_ffn_w[0],
                  w_group[0], b_group[0], w_router[0], b_router[0], w_gate[0], w_up[0], w_down[0],
                  norm_final_w)
```

```python
import functools

import jax
import jax.numpy as jnp
from jax import lax
from jax.experimental import pallas as pl
from jax.experimental.pallas import tpu as pltpu
from jax.experimental.pallas import tpu_sc as plsc

F32 = jnp.float32
BF16 = jnp.bfloat16

D_MODEL = 1024
N_Q_HEADS = 8
N_KV_HEADS = 2
HEAD_DIM = 64
WINDOW = 128
GQA_GROUP = N_Q_HEADS // N_KV_HEADS
M_HEADS = 4
M_QK_DIM = 64
M_V_DIM = 128
CONV_WIDTH = 4
N_GROUPS = 4
EXPERTS_PER_GROUP = 8
N_EXPERTS = N_GROUPS * EXPERTS_PER_GROUP
D_EXPERT = 512
EPS = 1e-6

ATT_Q_WIDTH = N_Q_HEADS * HEAD_DIM
ATT_KV_WIDTH = N_KV_HEADS * HEAD_DIM
M_QK_WIDTH = M_HEADS * M_QK_DIM
M_V_WIDTH = M_HEADS * M_V_DIM

LANES = 128
NEG = -1e30
VMEM_LIMIT = 56 * 1024 * 1024

IN_TILE = 1024
TOK_TILE = 1024
ATT_TILE = 2048
LOOKAHEAD = 2
M_LOOKAHEAD = 4
M_CHUNK = 128
M_BATCH = 8
M_STATE_ROWS = M_V_DIM + 16
SLOT_BLOCK = 512
COMB_TILE = 1024
RANK_RADIX = 65536
RANK_BITS = 16
ROUTER_ROWS = 8 + N_EXPERTS
PACKED = D_MODEL // 2
SC_CHUNK = 64

_O_AQ = 0
_O_AK = _O_AQ + ATT_Q_WIDTH
_O_AV = _O_AK + ATT_KV_WIDTH
_O_MQ = _O_AV + ATT_KV_WIDTH
_O_MK = _O_MQ + M_QK_WIDTH
_O_MV = _O_MK + M_QK_WIDTH
_O_MO = _O_MV + M_V_WIDTH
_O_MI = _O_MO + M_V_WIDTH
_O_MF = _O_MI + M_HEADS
_O_GA = _O_MF + M_HEADS
_O_GB = _O_GA + D_MODEL
_O_END = _O_GB + D_MODEL


def _rms(x, w):
    return x * lax.rsqrt(jnp.mean(x * x, axis=-1, keepdims=True) + EPS) * w


def _sigmoid(x):
    return 0.5 * jnp.tanh(0.5 * x) + 0.5


def _pack_rows(x):
    half = x.shape[1] // 2
    bits = lax.bitcast_convert_type(x.astype(BF16).astype(F32), jnp.uint32)
    packed = (bits[:, :half] >> 16) | (bits[:, half:] & jnp.uint32(0xFFFF0000))
    return lax.bitcast_convert_type(packed, jnp.int32)


def _unpack_rows(w):
    u = lax.bitcast_convert_type(w, jnp.uint32)
    lo = lax.bitcast_convert_type(u << 16, F32)
    hi = lax.bitcast_convert_type(u & jnp.uint32(0xFFFF0000), F32)
    return jnp.concatenate([lo, hi], axis=-1)


def _log_sigmoid(x):
    return jnp.minimum(x, 0.0) - jnp.log1p(jnp.exp(-jnp.abs(x)))


_TOK_SPLITS = (ATT_KV_WIDTH, M_QK_WIDTH)
_CHAN_SPLITS = (ATT_Q_WIDTH, ATT_KV_WIDTH, M_QK_WIDTH, M_V_WIDTH, M_V_WIDTH)


def _inproj_kernel(x_ref, nw_ref, w_ref, wt_ref, wg_ref, eg_ref, eu_ref, ed_ref, *out_refs):
    tok_refs = out_refs[:len(_TOK_SPLITS)]
    chan_refs = out_refs[len(_TOK_SPLITS):len(_TOK_SPLITS) + len(_CHAN_SPLITS)]
    grow_ref = out_refs[len(_TOK_SPLITS) + len(_CHAN_SPLITS)]
    for src, dst in zip((eg_ref, eu_ref, ed_ref), out_refs[-3:]):
        dst[...] = src[...].astype(BF16)
    h = _rms(x_ref[...], nw_ref[...]).astype(BF16)
    lo = 0
    for ref, width in zip(tok_refs, _TOK_SPLITS):
        ref[...] = jnp.dot(h, w_ref[:, lo:lo + width], preferred_element_type=F32).astype(BF16)
        lo += width
    lo = 0
    for ref, width in zip(chan_refs, _CHAN_SPLITS):
        ref[0] = lax.dot_general(wt_ref[lo:lo + width, :], h, (((1,), (1,)), ((), ())),
                                 preferred_element_type=F32).astype(BF16)
        lo += width
    g = jnp.dot(h, wg_ref[...], preferred_element_type=F32)
    grow_ref[...] = g.T[0:8, :]


def _inproj(x2, norm_w, w_tok, w_chan, w_gates, expert_ws, B, S):
    T = x2.shape[0]
    tm = min(IN_TILE, S)
    tpb = S // tm
    nsteps = T // tm
    row = lambda i: (i, 0)
    const = lambda i: (0, 0)
    chan = lambda i: (i // tpb, 0, i % tpb)
    cast_specs = [pl.BlockSpec((w.shape[0] // nsteps, w.shape[1]), row) for w in expert_ws]
    assert all(w.shape[0] % (8 * nsteps) == 0 for w in expert_ws)
    return pl.pallas_call(
        _inproj_kernel,
        grid=(nsteps,),
        in_specs=[pl.BlockSpec((tm, D_MODEL), row),
                  pl.BlockSpec((1, D_MODEL), const),
                  pl.BlockSpec(w_tok.shape, const),
                  pl.BlockSpec(w_chan.shape, const),
                  pl.BlockSpec((D_MODEL, LANES), const),
                  *cast_specs],
        out_specs=[*[pl.BlockSpec((tm, w), row) for w in _TOK_SPLITS],
                   *[pl.BlockSpec((1, w, tm), chan) for w in _CHAN_SPLITS],
                   pl.BlockSpec((8, tm), lambda i: (0, i)),
                   *cast_specs],
        out_shape=[*[jax.ShapeDtypeStruct((T, w), BF16) for w in _TOK_SPLITS],
                   *[jax.ShapeDtypeStruct((B, w, S), BF16) for w in _CHAN_SPLITS],
                   jax.ShapeDtypeStruct((8, T), F32),
                   *[jax.ShapeDtypeStruct(w.shape, BF16) for w in expert_ws]],
        compiler_params=pltpu.CompilerParams(dimension_semantics=("arbitrary",),
                                             vmem_limit_bytes=VMEM_LIMIT),
        name="inproj",
    )(x2, norm_w, w_tok, w_chan, w_gates, *expert_ws)


def _attn_kernel(tq, sink_ref, qt_ref, k_ref, kp_ref, vt_ref, vtp_ref, o_ref):
    i = pl.program_id(1)
    W = WINDOW
    G = GQA_GROUP
    u = lax.broadcasted_iota(jnp.int32, (W, W), 0)
    t = lax.broadcasted_iota(jnp.int32, (W, W), 1)
    from_prev = u > t

    def keys_values(j):
        if j == 0:
            return kp_ref[0], vtp_ref[0], k_ref[0, 0:W, :], vt_ref[0, :, 0:W]
        return (k_ref[0, (j - 1) * W:j * W, :], vt_ref[0, :, (j - 1) * W:j * W],
                k_ref[0, j * W:(j + 1) * W, :], vt_ref[0, :, j * W:(j + 1) * W])

    def scores(j, g):
        k_prev, _, k_cur, _ = keys_values(j)
        dims = slice(g * HEAD_DIM, (g + 1) * HEAD_DIM)
        qg = jnp.concatenate([qt_ref[0, h * HEAD_DIM:(h + 1) * HEAD_DIM, j * W:(j + 1) * W]
                              for h in range(g * G, (g + 1) * G)], axis=1)
        return (jnp.dot(k_prev[:, dims], qg, preferred_element_type=F32),
                jnp.dot(k_cur[:, dims], qg, preferred_element_type=F32))

    tasks = [(j, g) for j in range(tq // W) for g in range(N_KV_HEADS)]
    queue = [scores(*t) for t in tasks[:LOOKAHEAD]]
    pair_rows = []
    for n_task, (j, g) in enumerate(tasks):
        s_prev, s_cur = queue.pop(0)
        if n_task + LOOKAHEAD < len(tasks):
            queue.append(scores(*tasks[n_task + LOOKAHEAD]))
        cols = slice(j * W, (j + 1) * W)
        _, vt_prev, _, vt_cur = keys_values(j)
        dims = slice(g * HEAD_DIM, (g + 1) * HEAD_DIM)
        for n in range(G):
            hc = slice(n * W, (n + 1) * W)
            sp = s_prev[:, hc]
            if j == 0:
                sp = jnp.where(i > 0, sp, NEG)
            s = jnp.where(from_prev, sp, s_cur[:, hc])
            sink = sink_ref[g * G + n]
            m = jnp.maximum(jnp.max(s, axis=0, keepdims=True), sink)
            p = jnp.exp(s - m)
            inv = 1.0 / (jnp.sum(p, axis=0, keepdims=True) + jnp.exp(sink - m))
            p_prev = jnp.where(from_prev, p, 0.0).astype(BF16)
            p_cur = jnp.where(from_prev, 0.0, p).astype(BF16)
            o = jnp.dot(vt_prev[dims, :], p_prev, preferred_element_type=F32) \
                + jnp.dot(vt_cur[dims, :], p_cur, preferred_element_type=F32)
            pair_rows.append(o * inv)
        if g == N_KV_HEADS - 1:
            out = [jnp.concatenate(pair_rows[n:n + 2], axis=0).T for n in range(0, N_Q_HEADS, 2)]
            o_ref[0, cols, :] = jnp.concatenate(out, axis=1).astype(BF16)
            pair_rows = []


def _attn(sinks, qt, k3, vt):
    B, S, _ = k3.shape
    tq = min(ATT_TILE, S)
    per = tq // WINDOW
    chan = lambda b, i: (b, 0, i)
    tok = lambda b, i: (b, i, 0)
    return pl.pallas_call(
        functools.partial(_attn_kernel, tq),
        grid=(B, S // tq),
        in_specs=[pl.BlockSpec(memory_space=pltpu.SMEM),
                  pl.BlockSpec((1, ATT_Q_WIDTH, tq), chan),
                  pl.BlockSpec((1, tq, ATT_KV_WIDTH), tok),
                  pl.BlockSpec((1, WINDOW, ATT_KV_WIDTH), lambda b, i: (b, jnp.maximum(i * per - 1, 0), 0)),
                  pl.BlockSpec((1, ATT_KV_WIDTH, tq), chan),
                  pl.BlockSpec((1, ATT_KV_WIDTH, WINDOW), lambda b, i: (b, 0, jnp.maximum(i * per - 1, 0)))],
        out_specs=pl.BlockSpec((1, tq, ATT_Q_WIDTH), tok),
        out_shape=jax.ShapeDtypeStruct((B, S, ATT_Q_WIDTH), BF16),
        compiler_params=pltpu.CompilerParams(dimension_semantics=("arbitrary", "arbitrary"),
                                             vmem_limit_bytes=VMEM_LIMIT),
        name="attn",
    )(sinks, qt, k3, k3, vt, vt)


def _mlstm_kernel(nb, *refs):
    L = M_CHUNK
    H = M_HEADS
    mqt_ref, mk_ref, mvt_ref, mot_ref = refs[:4]
    grow_refs = refs[4:4 + nb]
    (cwq_ref, cbq_ref, cwk_ref, cbk_ref, brow_ref, nw_ref, y_ref,
     state_ref, m_ref, prevq_ref, ubuf_ref) = refs[4 + nb:]
    cidx = pl.program_id(1)

    @pl.when(cidx == 0)
    def _():
        state_ref[...] = jnp.zeros_like(state_ref)
        m_ref[...] = jnp.zeros_like(m_ref)
        prevq_ref[...] = jnp.zeros_like(prevq_ref)
        ubuf_ref[:, 0:8, :] = jnp.zeros((nb, 8, M_QK_WIDTH), F32)

    ri = lax.broadcasted_iota(jnp.int32, (L, L), 0)
    ci = lax.broadcasted_iota(jnp.int32, (L, L), 1)
    causal_t = ri <= ci
    triu = jnp.where(causal_t, 1.0, 0.0).astype(F32)
    lane = lax.broadcasted_iota(jnp.int32, (8, L), 1)
    r2 = lax.broadcasted_iota(jnp.int32, (2 * L, L), 0)
    c2 = lax.broadcasted_iota(jnp.int32, (2 * L, L), 1)
    shifts = [jnp.where(r2 - c2 == L - k, 1.0, 0.0).astype(BF16) for k in range(1, CONV_WIDTH)]
    ones_rows = jnp.where(lax.broadcasted_iota(jnp.int32, (M_STATE_ROWS - M_V_DIM, L), 0) == 0,
                          1.0, 0.0).astype(BF16)

    pairs = [(bb, h) for bb in range(nb) for h in range(H)]
    states = [state_ref[bb * H + h] for bb, h in pairs]
    m_prevs = [m_ref[bb] for bb in range(nb)]

    def prepare(bb):
        cur = mqt_ref[bb]
        both = jnp.concatenate([prevq_ref[bb], cur], axis=1)
        acc = cbq_ref[...] + cwq_ref[CONV_WIDTH - 1] * cur.astype(F32)
        for k in range(1, CONV_WIDTH):
            acc = acc + cwq_ref[CONV_WIDTH - 1 - k] * jnp.dot(both, shifts[k - 1],
                                                              preferred_element_type=F32)
        qt = (acc * _sigmoid(acc)).astype(BF16)
        ubuf_ref[bb, 8:L + 8, :] = mk_ref[bb].astype(F32)
        acc = cbk_ref[...] + cwk_ref[CONV_WIDTH - 1:CONV_WIDTH, :] * ubuf_ref[bb, 8:L + 8, :]
        for j in range(CONV_WIDTH - 1):
            off = 8 - (CONV_WIDTH - 1) + j
            acc = acc + cwk_ref[j:j + 1, :] * ubuf_ref[bb, off:off + L, :]
        ubuf_ref[bb, 0:8, :] = ubuf_ref[bb, L:L + 8, :]
        kk = (acc * _sigmoid(acc) * (M_QK_DIM ** -0.5)).astype(BF16)

        gr = grow_refs[bb][...] + brow_ref[...]
        b = jnp.dot(_log_sigmoid(gr), triu, preferred_element_type=F32,
                    precision=lax.Precision.HIGHEST)
        gi = pltpu.roll(gr, 4, axis=0)
        u = gi - b
        cm = u
        for sh in (1, 2, 4, 8, 16, 32, 64):
            if sh < L:
                cm = jnp.maximum(cm, jnp.where(lane >= sh, pltpu.roll(cm, sh, axis=1), NEG))
        m_prev = m_prevs[bb]
        mt = b + jnp.maximum(m_prev, cm)
        wa = jnp.exp(b + m_prev - mt)
        emt = jnp.exp(-mt)
        m_new = jnp.broadcast_to(mt[:, L - 1:L], (8, L))
        b_last = jnp.broadcast_to(b[:, L - 1:L], (8, L))
        wc = jnp.exp(b_last + m_prev - m_new)
        ws = jnp.exp(b_last + u - m_new)
        return qt, kk, b - mt, u.T, wa, emt, wc, ws, m_new

    seqs = [prepare(bb) for bb in range(nb)]

    def products(bb, h, state):
        qt, kk = seqs[bb][0], seqs[bb][1]
        qh = qt[h * M_QK_DIM:(h + 1) * M_QK_DIM, :]
        kh = kk[:, h * M_QK_DIM:(h + 1) * M_QK_DIM]
        return (kh, jnp.dot(kh, qh, preferred_element_type=F32),
                jnp.dot(state.astype(BF16), qh, preferred_element_type=F32))

    new_states, outs = [], []
    depth = M_LOOKAHEAD
    queue = [products(*pairs[n], states[n]) for n in range(min(depth, len(pairs)))]
    for n, ((bb, h), state) in enumerate(zip(pairs, states)):
        kh, st, inter = queue.pop(0)
        if n + depth < len(pairs):
            queue.append(products(*pairs[n + depth], states[n + depth]))
        _, _, bmt, ucol, wa, emt, wc, ws, _ = seqs[bb]
        row = slice(H + h, H + h + 1)
        wq = jnp.exp(jnp.where(causal_t, bmt[row, :] + ucol[:, H + h:H + h + 1], NEG))
        sw = (st * wq).astype(BF16)
        vext = jnp.concatenate([mvt_ref[bb, h * M_V_DIM:(h + 1) * M_V_DIM, :], ones_rows], axis=0)
        res = wa[row, :] * inter + jnp.dot(vext, sw, preferred_element_type=F32)
        num = res[:M_V_DIM, :]
        den = res[M_V_DIM:M_V_DIM + 1, :]
        hb = num / jnp.maximum(jnp.abs(den), emt[row, :])
        vw = (vext.astype(F32) * ws[row, :]).astype(BF16)
        new_states.append(wc[row, 0:1] * state + jnp.dot(vw, kh, preferred_element_type=F32))
        hn = hb * lax.rsqrt(jnp.mean(hb * hb, axis=0, keepdims=True) + EPS)
        hn = hn * nw_ref[h * M_V_DIM:(h + 1) * M_V_DIM, :]
        og = _sigmoid(mot_ref[bb, h * M_V_DIM:(h + 1) * M_V_DIM, :].astype(F32))
        outs.append((og * hn).T.astype(BF16))

    for i, (bb, h) in enumerate(pairs):
        state_ref[bb * H + h] = new_states[i]
    for bb in range(nb):
        m_ref[bb] = seqs[bb][8]
        prevq_ref[bb] = mqt_ref[bb]
        y_ref[bb] = jnp.concatenate(outs[bb * H:(bb + 1) * H], axis=1)


def _mlstm(mqt, mk3, mvt, mot, grow, conv_w, conv_b, bias_row, norm_w):
    B, S, _ = mk3.shape
    L = M_CHUNK
    nb = max(d for d in range(1, M_BATCH + 1) if B % d == 0)
    nc = S // L
    assert L == LANES, "per-head scalars are kept lane-replicated next to (8, L) gate rows"
    tok = lambda b, c: (b, c, 0)
    chan = lambda b, c: (b, 0, c)
    const2 = lambda b, c: (0, 0)
    const3 = lambda b, c: (0, 0, 0)
    grow_specs = [pl.BlockSpec((8, L), functools.partial(lambda b, c, n: (0, (b * nb + n) * nc + c), n=n))
                  for n in range(nb)]
    rep = lambda v: jnp.broadcast_to(v.astype(F32)[..., None], v.shape + (L,))
    cwq, cbq = rep(conv_w[:, :M_QK_WIDTH]), rep(conv_b[:M_QK_WIDTH])
    cwk, cbk = conv_w[:, M_QK_WIDTH:].astype(F32), conv_b[M_QK_WIDTH:].reshape(1, -1).astype(F32)
    return pl.pallas_call(
        functools.partial(_mlstm_kernel, nb),
        grid=(B // nb, nc),
        in_specs=[pl.BlockSpec((nb, M_QK_WIDTH, L), chan),
                  pl.BlockSpec((nb, L, M_QK_WIDTH), tok),
                  pl.BlockSpec((nb, M_V_WIDTH, L), chan),
                  pl.BlockSpec((nb, M_V_WIDTH, L), chan),
                  *grow_specs,
                  pl.BlockSpec((CONV_WIDTH, M_QK_WIDTH, L), const3),
                  pl.BlockSpec((M_QK_WIDTH, L), const2),
                  pl.BlockSpec((CONV_WIDTH, M_QK_WIDTH), const2),
                  pl.BlockSpec((1, M_QK_WIDTH), const2),
                  pl.BlockSpec((8, 1), const2),
                  pl.BlockSpec((M_V_WIDTH, L), const2)],
        out_specs=pl.BlockSpec((nb, L, M_V_WIDTH), tok),
        out_shape=jax.ShapeDtypeStruct((B, S, M_V_WIDTH), BF16),
        scratch_shapes=[pltpu.VMEM((nb * M_HEADS, M_STATE_ROWS, M_QK_DIM), F32),
                        pltpu.VMEM((nb, 8, LANES), F32),
                        pltpu.VMEM((nb, M_QK_WIDTH, L), BF16),
                        pltpu.VMEM((nb, L + 8, M_QK_WIDTH), F32)],
        compiler_params=pltpu.CompilerParams(dimension_semantics=("arbitrary", "arbitrary"),
                                             vmem_limit_bytes=VMEM_LIMIT),
        name="mlstm",
    )(mqt, mk3, mvt, mot, *([grow] * nb), cwq, cbq, cwk, cbk, bias_row, rep(norm_w))


def _outproj_kernel(x_ref, ya_ref, ym_ref, nmix_ref, wgab_ref, wa_ref, wm_ref, wo_ref, nw_ref, wr_ref, br_ref,
                    x1_ref, h2_ref, meta_ref, code_ref, cnt_ref, cnt_scr):
    tm = x_ref.shape[0]
    i = pl.program_id(0)

    @pl.when(i == 0)
    def _():
        cnt_scr[...] = jnp.zeros_like(cnt_scr)

    pa = jnp.dot(ya_ref[...], wa_ref[...], preferred_element_type=F32)
    pm = jnp.dot(ym_ref[...], wm_ref[...], preferred_element_type=F32)
    x = x_ref[...]
    h1 = _rms(x, nmix_ref[...]).astype(BF16)
    ga = jnp.dot(h1, wgab_ref[:, :D_MODEL], preferred_element_type=F32)
    gb = jnp.dot(h1, wgab_ref[:, D_MODEL:], preferred_element_type=F32)
    mix = _sigmoid(ga) * pa + _sigmoid(gb) * pm
    x1 = x + jnp.dot(mix.astype(BF16), wo_ref[...], preferred_element_type=F32)
    x1_ref[...] = x1
    h2 = _rms(x1, nw_ref[...])
    h2_ref[...] = _pack_rows(h2)

    nt = (((1,), (1,)), ((), ()))
    h_hi = h2.astype(BF16)
    h_lo = (h2 - h_hi.astype(F32)).astype(BF16)
    hw = lax.dot_general(wr_ref[...], h_hi, nt, preferred_element_type=F32)
    logits = (hw[:LANES] + hw[LANES:]
              + lax.dot_general(wr_ref[:LANES, :], h_lo, nt, preferred_element_type=F32))[:ROUTER_ROWS]
    logits = logits + br_ref[...]
    row8 = lax.broadcasted_iota(jnp.int32, (8, tm), 0)

    def first_argmax(v):
        mx = jnp.max(v, axis=0, keepdims=True)
        idx = jnp.min(jnp.where(v == mx, row8, 8), axis=0, keepdims=True)
        return mx, idx

    gl = jnp.where(row8 < N_GROUPS, logits[0:8], NEG)
    gmax, gi = first_argmax(gl)
    gp = 1.0 / jnp.sum(jnp.exp(gl - gmax), axis=0, keepdims=True)
    el = jnp.zeros((8, tm), F32)
    for g in range(N_GROUPS):
        el = jnp.where(gi == g, logits[8 + 8 * g:16 + 8 * g], el)
    v1, j1 = first_argmax(el)
    v2, j2 = first_argmax(jnp.where(row8 == j1, NEG, el))
    t = jnp.exp(v2 - v1)
    w1 = gp / (1.0 + t)
    w2 = gp * t / (1.0 + t)
    e1 = gi * EXPERTS_PER_GROUP + j1
    e2 = gi * EXPERTS_PER_GROUP + j2

    erow = lax.broadcasted_iota(jnp.int32, (N_EXPERTS, tm), 0)
    hit1 = erow == e1
    hit2 = erow == e2
    onehot = jnp.where(hit1 | hit2, 1.0, 0.0)
    ri = lax.broadcasted_iota(jnp.int32, (tm, tm), 0)
    ci = lax.broadcasted_iota(jnp.int32, (tm, tm), 1)
    before = jnp.where(ri < ci, 1.0, 0.0).astype(BF16)
    rank = jnp.dot(onehot.astype(BF16), before, preferred_element_type=F32) + cnt_scr[...]
    r1 = jnp.sum(jnp.where(hit1, rank, 0.0), axis=0, keepdims=True)
    r2 = jnp.sum(jnp.where(hit2, rank, 0.0), axis=0, keepdims=True)
    cnt = jnp.broadcast_to((rank + onehot)[:, tm - 1:tm], (N_EXPERTS, tm))
    cnt_scr[...] = cnt
    cnt_ref[...] = cnt[:, :LANES]

    c1 = e1 * RANK_RADIX + r1.astype(jnp.int32)
    c2 = e2 * RANK_RADIX + r2.astype(jnp.int32)
    code_ref[...] = jnp.where(row8 == 0, c1, jnp.where(row8 == 1, c2, 0))
    wrow = lax.broadcasted_iota(jnp.int32, (LANES, tm), 0)
    meta_ref[...] = jnp.where(wrow == 4, w1, jnp.where(wrow == 5, w2, 0.0)).T


def _outproj(x2, ya, ym, norm_mix, w_gab, wa, wm, wo, norm_w, w_rt, b_rt):
    T = x2.shape[0]
    tm = b_rt.shape[1]
    row = lambda i: (i, 0)
    const = lambda i: (0, 0)
    resident = lambda shape: pl.BlockSpec(shape, const, pipeline_mode=pl.Buffered(1))
    return pl.pallas_call(
        _outproj_kernel,
        grid=(T // tm,),
        in_specs=[pl.BlockSpec((tm, D_MODEL), row),
                  pl.BlockSpec((tm, ATT_Q_WIDTH), row),
                  pl.BlockSpec((tm, M_V_WIDTH), row),
                  resident((1, D_MODEL)),
                  resident((D_MODEL, 2 * D_MODEL)),
                  resident((ATT_Q_WIDTH, D_MODEL)),
                  resident((M_V_WIDTH, D_MODEL)),
                  resident((D_MODEL, D_MODEL)),
                  resident((1, D_MODEL)),
                  resident((2 * LANES, D_MODEL)),
                  resident((ROUTER_ROWS, tm))],
        out_specs=[pl.BlockSpec((tm, D_MODEL), row),
                   pl.BlockSpec((tm, PACKED), row),
                   pl.BlockSpec((tm, LANES), row),
                   pl.BlockSpec((8, tm), lambda i: (0, i)),
                   pl.BlockSpec((N_EXPERTS, LANES), const)],
        out_shape=[jax.ShapeDtypeStruct((T, D_MODEL), F32),
                   jax.ShapeDtypeStruct((T, PACKED), jnp.int32),
                   jax.ShapeDtypeStruct((T, LANES), F32),
                   jax.ShapeDtypeStruct((8, T), jnp.int32),
                   jax.ShapeDtypeStruct((N_EXPERTS, LANES), F32)],
        scratch_shapes=[pltpu.VMEM((N_EXPERTS, tm), F32)],
        compiler_params=pltpu.CompilerParams(dimension_semantics=("arbitrary",),
                                             vmem_limit_bytes=VMEM_LIMIT),
        name="outproj",
    )(x2, ya, ym, norm_mix, w_gab, wa, wm, wo, norm_w, w_rt, b_rt)


def _slots_kernel(cnt_ref, pstart_ref, blk_e_ref, nvalid_ref, nused_ref):
    R = SLOT_BLOCK
    nblk = blk_e_ref.shape[0]

    def per_expert(e, run):
        c = cnt_ref[e]
        end = run + ((c + R - 1) // R) * R
        pstart_ref[e] = run

        def set_blk(b, carry):
            blk_e_ref[b] = e
            nvalid_ref[b] = jnp.minimum(run + c - b * R, R)
            return carry
        lax.fori_loop(run // R, end // R, set_blk, 0)
        return end

    total = lax.fori_loop(0, N_EXPERTS, per_expert, 0)
    nused_ref[0] = total // R

    def tail_blk(b, carry):
        blk_e_ref[b] = N_EXPERTS - 1
        nvalid_ref[b] = 0
        return carry
    lax.fori_loop(total // R, nblk, tail_blk, 0)


def _slots(counts, nblk):
    smem = pl.BlockSpec(memory_space=pltpu.SMEM)
    return pl.pallas_call(
        _slots_kernel,
        in_specs=[smem],
        out_specs=[smem, smem, smem, smem],
        out_shape=[jax.ShapeDtypeStruct((N_EXPERTS,), jnp.int32),
                   jax.ShapeDtypeStruct((nblk,), jnp.int32),
                   jax.ShapeDtypeStruct((nblk,), jnp.int32),
                   jax.ShapeDtypeStruct((1,), jnp.int32)],
        name="slots",
    )(counts)


def _experts_kernel(blk_e_ref, nvalid_ref, nused_ref, xs_ref, wg_ref, wu_ref, wd_ref, ys_ref):
    j = pl.program_id(0)
    R = SLOT_BLOCK
    valid = j < nused_ref[0]

    @pl.when(valid)
    def _():
        rows = lax.broadcasted_iota(jnp.int32, (R, PACKED), 0)
        xw = jnp.where(rows < nvalid_ref[j], xs_ref[...], 0)
        xb = _unpack_rows(xw).astype(BF16)
        g = jnp.dot(xb, wg_ref[0], preferred_element_type=F32)
        u = jnp.dot(xb, wu_ref[0], preferred_element_type=F32)
        act = (g * _sigmoid(g) * u).astype(BF16)
        ys_ref[...] = _pack_rows(jnp.dot(act, wd_ref[0], preferred_element_type=F32))


def _experts(blk_e, nvalid, nused, xs, w_gate, w_up, w_down):
    nblk = blk_e.shape[0]
    R = SLOT_BLOCK
    wspec = lambda shape: pl.BlockSpec((1,) + shape, lambda j, be, nv, nu: (be[j], 0, 0))
    grid_spec = pltpu.PrefetchScalarGridSpec(
        num_scalar_prefetch=3,
        grid=(nblk,),
        in_specs=[pl.BlockSpec((R, PACKED), lambda j, be, nv, nu: (jnp.minimum(j, nu[0] - 1), 0)),
                  wspec((D_MODEL, D_EXPERT)),
                  wspec((D_MODEL, D_EXPERT)),
                  wspec((D_EXPERT, D_MODEL))],
        out_specs=pl.BlockSpec((R, PACKED), lambda j, be, nv, nu: (jnp.minimum(j, nu[0] - 1), 0)),
    )
    return pl.pallas_call(
        _experts_kernel,
        grid_spec=grid_spec,
        out_shape=jax.ShapeDtypeStruct((nblk * R, PACKED), jnp.int32),
        compiler_params=pltpu.CompilerParams(dimension_semantics=("arbitrary",),
                                             vmem_limit_bytes=VMEM_LIMIT),
        name="experts",
    )(blk_e, nvalid, nused, xs, w_gate, w_up, w_down)


def _dest_kernel(pstart_ref, code_ref, dest_ref):
    c = code_ref[...]
    e = c >> RANK_BITS
    base = jnp.zeros_like(c)
    for k in range(N_EXPERTS):
        base = jnp.where(e == k, pstart_ref[k], base)
    dest_ref[...] = base + (c & (RANK_RADIX - 1))


def _dests(pstart, code_rows):
    T = code_rows.shape[1]
    tc = min(4096, T)
    return pl.pallas_call(
        _dest_kernel,
        grid=(T // tc,),
        in_specs=[pl.BlockSpec(memory_space=pltpu.SMEM),
                  pl.BlockSpec((8, tc), lambda i: (0, i))],
        out_specs=pl.BlockSpec((8, tc), lambda i: (0, i)),
        out_shape=jax.ShapeDtypeStruct((8, T), jnp.int32),
        name="dests",
    )(pstart, code_rows)


def _sc_workers():
    info = plsc.get_sparse_core_info()
    return info.num_cores, info.num_cores * info.num_subcores


def _sc_scatter_rows(src, idx, n_out):
    n_src, width = src.shape
    nc, nw = _sc_workers()
    per_w = n_src // nw
    ch = SC_CHUNK
    nch = per_w // ch
    assert idx.shape[0] == 2 * n_src and per_w * nw == n_src and nch * ch == per_w and nch % 2 == 0
    mesh = plsc.VectorSubcoreMesh(core_axis_name="c", subcore_axis_name="s")
    dma = pltpu.SemaphoreType.DMA

    @functools.partial(
        pl.kernel, mesh=mesh, out_type=jax.ShapeDtypeStruct((n_out, width), src.dtype),
        scratch_types=[pltpu.VMEM((per_w,), jnp.int32), pltpu.VMEM((per_w,), jnp.int32),
                       pltpu.VMEM((ch, width), src.dtype), pltpu.VMEM((ch, width), src.dtype),
                       dma, dma, dma, dma, dma, dma],
        name="sc_scatter")
    def scatter(src_hbm, idx_hbm, out_hbm, idx_a, idx_b, buf0, buf1, in0, in1, out0a, out0b, out1a, out1b):
        base = (lax.axis_index("s") * nc + lax.axis_index("c")) * per_w
        pltpu.sync_copy(idx_hbm.at[pl.ds(base, per_w)], idx_a)
        pltpu.sync_copy(idx_hbm.at[pl.ds(n_src + base, per_w)], idx_b)

        def read(c, buf, sem):
            return pltpu.make_async_copy(src_hbm.at[pl.ds(base + c * ch, ch)], buf, sem)

        def writes(c, buf, sem_a, sem_b):
            rows = pl.ds(c * ch, ch)
            return (pltpu.make_async_copy(buf, out_hbm.at[idx_a.at[rows]], sem_a),
                    pltpu.make_async_copy(buf, out_hbm.at[idx_b.at[rows]], sem_b))

        def start(copies):
            for cp in copies:
                cp.start()

        def wait(copies):
            for cp in copies:
                cp.wait()

        read(0, buf0, in0).start()

        @pl.loop(0, nch, step=2)
        def _(c):
            @pl.when(c > 0)
            def _():
                wait(writes(c - 1, buf1, out1a, out1b))
            read(c + 1, buf1, in1).start()
            read(c, buf0, in0).wait()
            start(writes(c, buf0, out0a, out0b))
            read(c + 1, buf1, in1).wait()
            start(writes(c + 1, buf1, out1a, out1b))
            wait(writes(c, buf0, out0a, out0b))

            @pl.when(c + 2 < nch)
            def _():
                read(c + 2, buf0, in0).start()

        wait(writes(nch - 1, buf1, out1a, out1b))

    return scatter(src, idx)


def _sc_gather_rows(src, idx):
    M = idx.shape[0]
    width = src.shape[1]
    nc, nw = _sc_workers()
    per_w = M // nw
    ch = SC_CHUNK
    nch = per_w // ch
    assert per_w * nw == M and nch * ch == per_w and nch % 2 == 0
    mesh = plsc.VectorSubcoreMesh(core_axis_name="c", subcore_axis_name="s")
    dma = pltpu.SemaphoreType.DMA

    @functools.partial(
        pl.kernel, mesh=mesh, out_type=jax.ShapeDtypeStruct((M, width), src.dtype),
        scratch_types=[pltpu.VMEM((per_w,), jnp.int32),
                       pltpu.VMEM((ch, width), src.dtype), pltpu.VMEM((ch, width), src.dtype),
                       dma, dma, dma, dma],
        name="sc_gather")
    def gather(src_hbm, idx_hbm, out_hbm, idx_v, buf0, buf1, in0, in1, out0, out1):
        base = (lax.axis_index("s") * nc + lax.axis_index("c")) * per_w
        pltpu.sync_copy(idx_hbm.at[pl.ds(base, per_w)], idx_v)

        def read(c, buf, sem):
            return pltpu.make_async_copy(src_hbm.at[idx_v.at[pl.ds(c * ch, ch)]], buf, sem)

        def write(c, buf, sem):
            return pltpu.make_async_copy(buf, out_hbm.at[pl.ds(base + c * ch, ch)], sem)

        read(0, buf0, in0).start()

        @pl.loop(0, nch, step=2)
        def _(c):
            @pl.when(c > 0)
            def _():
                write(c - 1, buf1, out1).wait()
            read(c + 1, buf1, in1).start()
            read(c, buf0, in0).wait()
            write(c, buf0, out0).start()
            read(c + 1, buf1, in1).wait()
            write(c + 1, buf1, out1).start()
            write(c, buf0, out0).wait()

            @pl.when(c + 2 < nch)
            def _():
                read(c + 2, buf0, in0).start()

        write(nch - 1, buf1, out1).wait()

    return gather(src, idx)


def _final_kernel(x1_ref, y1_ref, y2_ref, meta_ref, nw_ref, o_ref):
    w1 = meta_ref[:, 4:5]
    w2 = meta_ref[:, 5:6]
    x2 = x1_ref[...] + (w1 * _unpack_rows(y1_ref[...]) + w2 * _unpack_rows(y2_ref[...]))
    o_ref[...] = _rms(x2, nw_ref[...])


def _final(x1, yg, meta, norm_w):
    T = x1.shape[0]
    tm = min(COMB_TILE, T)
    nt = T // tm
    row = lambda i: (i, 0)
    return pl.pallas_call(
        _final_kernel,
        grid=(nt,),
        in_specs=[pl.BlockSpec((tm, D_MODEL), row),
                  pl.BlockSpec((tm, PACKED), row),
                  pl.BlockSpec((tm, PACKED), lambda i: (i + nt, 0)),
                  pl.BlockSpec((tm, LANES), row),
                  pl.BlockSpec((1, D_MODEL), lambda i: (0, 0))],
        out_specs=pl.BlockSpec((tm, D_MODEL), row),
        out_shape=jax.ShapeDtypeStruct((T, D_MODEL), F32),
        compiler_params=pltpu.CompilerParams(dimension_semantics=("arbitrary",),
                                             vmem_limit_bytes=VMEM_LIMIT),
        name="final",
    )(x1, yg, yg, meta, norm_w)


def _layer(x, norm_mix_w, w_in, conv_w, conv_b, b_igate, b_fgate, attn_sinks, mlstm_norm_w,
           w_attn_o, w_mlstm_o, w_out, norm_ffn_w, w_group, b_group, w_router, b_router,
           w_gate, w_up, w_down, out_norm_w):
    B, S, D = x.shape
    T = B * S
    x2 = x.reshape(T, D)

    w_tok = jnp.concatenate([w_in[:, _O_AK:_O_AV], w_in[:, _O_MK:_O_MV]], axis=1).astype(BF16)
    w_chan = jnp.concatenate([w_in[:, _O_AQ:_O_AK] * (HEAD_DIM ** -0.5), w_in[:, _O_AV:_O_MQ],
                              w_in[:, _O_MQ:_O_MK], w_in[:, _O_MV:_O_MI]], axis=1).T.astype(BF16)
    w_gates = jnp.pad(w_in[:, _O_MI:_O_GA], ((0, 0), (0, LANES - 2 * M_HEADS))).astype(BF16)
    expert_ws = [w.reshape(-1, w.shape[-1]) for w in (w_gate, w_up, w_down)]
    ak, mk, aqt, avt, mqt, mvt, mot, grow, wg16, wu16, wd16 = _inproj(
        x2, norm_mix_w.reshape(1, D), w_tok, w_chan, w_gates, expert_ws, B, S)

    ya = _attn(attn_sinks.astype(F32), aqt, ak.reshape(B, S, -1), avt)

    bias = jnp.concatenate([b_igate, b_fgate]).astype(F32)
    ym = _mlstm(mqt, mk.reshape(B, S, -1), mvt, mot, grow, conv_w, conv_b,
                bias.reshape(2 * M_HEADS, 1), mlstm_norm_w)

    gpad = jnp.zeros((8 - N_GROUPS, D), F32)
    w_rt = jnp.concatenate([w_group.T, gpad, w_router.T, jnp.zeros((LANES - ROUTER_ROWS, D), F32)], axis=0)
    w_rt_hi = w_rt.astype(BF16)
    w_rt = jnp.concatenate([w_rt_hi, (w_rt - w_rt_hi.astype(F32)).astype(BF16)], axis=0)
    b_rt = jnp.concatenate([b_group, jnp.zeros((8 - N_GROUPS,), F32), b_router]).astype(F32)
    b_rt = jnp.broadcast_to(b_rt[:, None], (ROUTER_ROWS, min(TOK_TILE, T)))
    x1, h2p, meta, code_rows, cnt = _outproj(x2, ya.reshape(T, -1), ym.reshape(T, -1),
                                            norm_mix_w.reshape(1, D), w_in[:, _O_GA:].astype(BF16),
                                            w_attn_o.astype(BF16), w_mlstm_o.astype(BF16),
                                            w_out.astype(BF16), norm_ffn_w.reshape(1, D), w_rt, b_rt)

    counts = cnt[:, 0].astype(jnp.int32)
    nblk = (2 * T) // SLOT_BLOCK + N_EXPERTS
    pstart, blk_e, nvalid, nused = _slots(counts, nblk)
    dest = _dests(pstart, code_rows)[0:2].reshape(2 * T)

    xs = _sc_scatter_rows(h2p, dest, nblk * SLOT_BLOCK)
    ys = _experts(blk_e, nvalid, nused, xs, wg16.reshape(w_gate.shape), wu16.reshape(w_up.shape),
                  wd16.reshape(w_down.shape))
    yg = _sc_gather_rows(ys, dest)
    out = _final(x1, yg, meta, out_norm_w.reshape(1, D))
    return out.reshape(B, S, D)


def kernel(x, norm_mix_w, w_in, conv_w, conv_b, b_igate, b_fgate, attn_sinks, mlstm_norm_w, w_attn_o,
           w_mlstm_o, w_out, norm_ffn_w, w_group, b_group, w_router, b_router, w_gate, w_up, w_down,
           norm_final_w):
    depth = w_in.shape[0]
    assert depth == 1, "final RMSNorm is fused into the last layer's combine kernel"
    return _layer(x, norm_mix_w[0], w_in[0], conv_w[0], conv_b[0], b_igate[0], b_fgate[0],
                  attn_sinks[0], mlstm_norm_w[0], w_attn_o[0], w_mlstm_o[0], w_out[0], norm_ffn_w[0],
                  w_group[0], b_group[0], w_router[0], b_router[0], w_gate[0], w_up[0], w_down[0],
                  norm_final_w)
```

```python
import functools

import jax
import jax.numpy as jnp
from jax import lax
from jax.experimental import pallas as pl
from jax.experimental.pallas import tpu as pltpu
from jax.experimental.pallas import tpu_sc as plsc

F32 = jnp.float32
BF16 = jnp.bfloat16

D_MODEL = 1024
N_Q_HEADS = 8
N_KV_HEADS = 2
HEAD_DIM = 64
WINDOW = 128
GQA_GROUP = N_Q_HEADS // N_KV_HEADS
M_HEADS = 4
M_QK_DIM = 64
M_V_DIM = 128
CONV_WIDTH = 4
N_GROUPS = 4
EXPERTS_PER_GROUP = 8
N_EXPERTS = N_GROUPS * EXPERTS_PER_GROUP
D_EXPERT = 512
EPS = 1e-6

ATT_Q_WIDTH = N_Q_HEADS * HEAD_DIM
ATT_KV_WIDTH = N_KV_HEADS * HEAD_DIM
M_QK_WIDTH = M_HEADS * M_QK_DIM
M_V_WIDTH = M_HEADS * M_V_DIM

LANES = 128
NEG = -1e30
V7X_VMEM_BYTES = 64 * 1024 * 1024
VMEM_LIMIT = V7X_VMEM_BYTES - 8 * 1024 * 1024

IN_TILE = 1024
TOK_TILE = 1024
ATT_TILE = 2048
LOOKAHEAD = 2
M_LOOKAHEAD = 4
M_CHUNK = 128
M_BATCH = 8
M_STATE_ROWS = M_V_DIM + 16
SLOT_BLOCK = 512
COMB_TILE = 1024
RANK_RADIX = 65536
RANK_BITS = 16
ROUTER_ROWS = 8 + N_EXPERTS
PACKED = D_MODEL // 2
SC_CHUNK = 64

_O_AQ = 0
_O_AK = _O_AQ + ATT_Q_WIDTH
_O_AV = _O_AK + ATT_KV_WIDTH
_O_MQ = _O_AV + ATT_KV_WIDTH
_O_MK = _O_MQ + M_QK_WIDTH
_O_MV = _O_MK + M_QK_WIDTH
_O_MO = _O_MV + M_V_WIDTH
_O_MI = _O_MO + M_V_WIDTH
_O_MF = _O_MI + M_HEADS
_O_GA = _O_MF + M_HEADS


def _rms(x, w):
    return x * lax.rsqrt(jnp.mean(x * x, axis=-1, keepdims=True) + EPS) * w


def _sigmoid(x):
    return 0.5 * jnp.tanh(0.5 * x) + 0.5


def _pack_rows(x):
    half = x.shape[1] // 2
    bits = lax.bitcast_convert_type(x.astype(BF16).astype(F32), jnp.uint32)
    packed = (bits[:, :half] >> 16) | (bits[:, half:] & jnp.uint32(0xFFFF0000))
    return lax.bitcast_convert_type(packed, jnp.int32)


def _unpack_rows(w):
    u = lax.bitcast_convert_type(w, jnp.uint32)
    lo = lax.bitcast_convert_type(u << 16, F32)
    hi = lax.bitcast_convert_type(u & jnp.uint32(0xFFFF0000), F32)
    return jnp.concatenate([lo, hi], axis=-1)


def _log_sigmoid(x):
    return jnp.minimum(x, 0.0) - jnp.log1p(jnp.exp(-jnp.abs(x)))


_TOK_SPLITS = (ATT_KV_WIDTH, M_QK_WIDTH)
_CHAN_SPLITS = (ATT_Q_WIDTH, ATT_KV_WIDTH, M_QK_WIDTH, M_V_WIDTH, M_V_WIDTH)


def _inproj_kernel(x_ref, nw_ref, w_ref, wt_ref, wg_ref, eg_ref, eu_ref, ed_ref, *out_refs):
    tok_refs = out_refs[:len(_TOK_SPLITS)]
    chan_refs = out_refs[len(_TOK_SPLITS):len(_TOK_SPLITS) + len(_CHAN_SPLITS)]
    grow_ref = out_refs[len(_TOK_SPLITS) + len(_CHAN_SPLITS)]
    for src, dst in zip((eg_ref, eu_ref, ed_ref), out_refs[-3:]):
        dst[...] = src[...].astype(BF16)
    h = _rms(x_ref[...], nw_ref[...]).astype(BF16)
    lo = 0
    for ref, width in zip(tok_refs, _TOK_SPLITS):
        ref[...] = jnp.dot(h, w_ref[:, lo:lo + width], preferred_element_type=F32).astype(BF16)
        lo += width
    lo = 0
    for ref, width in zip(chan_refs, _CHAN_SPLITS):
        ref[0] = lax.dot_general(wt_ref[lo:lo + width, :], h, (((1,), (1,)), ((), ())),
                                 preferred_element_type=F32).astype(BF16)
        lo += width
    g = jnp.dot(h, wg_ref[...], preferred_element_type=F32)
    grow_ref[...] = g.T[0:8, :]


def _inproj(x2, norm_w, w_tok, w_chan, w_gates, expert_ws, B, S):
    T = x2.shape[0]
    tm = min(IN_TILE, S)
    tpb = S // tm
    nsteps = T // tm
    row = lambda i: (i, 0)
    const = lambda i: (0, 0)
    chan = lambda i: (i // tpb, 0, i % tpb)
    cast_specs = [pl.BlockSpec((w.shape[0] // nsteps, w.shape[1]), row) for w in expert_ws]
    assert all(w.shape[0] % (8 * nsteps) == 0 for w in expert_ws)
    return pl.pallas_call(
        _inproj_kernel,
        grid=(nsteps,),
        in_specs=[pl.BlockSpec((tm, D_MODEL), row),
                  pl.BlockSpec((1, D_MODEL), const),
                  pl.BlockSpec(w_tok.shape, const),
                  pl.BlockSpec(w_chan.shape, const),
                  pl.BlockSpec((D_MODEL, LANES), const),
                  *cast_specs],
        out_specs=[*[pl.BlockSpec((tm, w), row) for w in _TOK_SPLITS],
                   *[pl.BlockSpec((1, w, tm), chan) for w in _CHAN_SPLITS],
                   pl.BlockSpec((8, tm), lambda i: (0, i)),
                   *cast_specs],
        out_shape=[*[jax.ShapeDtypeStruct((T, w), BF16) for w in _TOK_SPLITS],
                   *[jax.ShapeDtypeStruct((B, w, S), BF16) for w in _CHAN_SPLITS],
                   jax.ShapeDtypeStruct((8, T), F32),
                   *[jax.ShapeDtypeStruct(w.shape, BF16) for w in expert_ws]],
        compiler_params=pltpu.CompilerParams(dimension_semantics=("arbitrary",),
                                             vmem_limit_bytes=VMEM_LIMIT),
        name="inproj",
    )(x2, norm_w, w_tok, w_chan, w_gates, *expert_ws)


def _attn_kernel(tq, sink_ref, qt_ref, k_ref, kp_ref, vt_ref, vtp_ref, o_ref):
    i = pl.program_id(1)
    W = WINDOW
    G = GQA_GROUP
    u = lax.broadcasted_iota(jnp.int32, (W, W), 0)
    t = lax.broadcasted_iota(jnp.int32, (W, W), 1)
    from_prev = u > t

    def keys_values(j):
        if j == 0:
            return kp_ref[0], vtp_ref[0], k_ref[0, 0:W, :], vt_ref[0, :, 0:W]
        return (k_ref[0, (j - 1) * W:j * W, :], vt_ref[0, :, (j - 1) * W:j * W],
                k_ref[0, j * W:(j + 1) * W, :], vt_ref[0, :, j * W:(j + 1) * W])

    def scores(j, g):
        k_prev, _, k_cur, _ = keys_values(j)
        dims = slice(g * HEAD_DIM, (g + 1) * HEAD_DIM)
        qg = jnp.concatenate([qt_ref[0, h * HEAD_DIM:(h + 1) * HEAD_DIM, j * W:(j + 1) * W]
                              for h in range(g * G, (g + 1) * G)], axis=1)
        return (jnp.dot(k_prev[:, dims], qg, preferred_element_type=F32),
                jnp.dot(k_cur[:, dims], qg, preferred_element_type=F32))

    tasks = [(j, g) for j in range(tq // W) for g in range(N_KV_HEADS)]
    queue = [scores(*t) for t in tasks[:LOOKAHEAD]]
    pair_rows = []
    for n_task, (j, g) in enumerate(tasks):
        s_prev, s_cur = queue.pop(0)
        if n_task + LOOKAHEAD < len(tasks):
            queue.append(scores(*tasks[n_task + LOOKAHEAD]))
        cols = slice(j * W, (j + 1) * W)
        _, vt_prev, _, vt_cur = keys_values(j)
        dims = slice(g * HEAD_DIM, (g + 1) * HEAD_DIM)
        for n in range(G):
            hc = slice(n * W, (n + 1) * W)
            sp = s_prev[:, hc]
            if j == 0:
                sp = jnp.where(i > 0, sp, NEG)
            s = jnp.where(from_prev, sp, s_cur[:, hc])
            sink = sink_ref[g * G + n]
            m = jnp.maximum(jnp.max(s, axis=0, keepdims=True), sink)
            p = jnp.exp(s - m)
            inv = 1.0 / (jnp.sum(p, axis=0, keepdims=True) + jnp.exp(sink - m))
            p_prev = jnp.where(from_prev, p, 0.0).astype(BF16)
            p_cur = jnp.where(from_prev, 0.0, p).astype(BF16)
            o = jnp.dot(vt_prev[dims, :], p_prev, preferred_element_type=F32) \
                + jnp.dot(vt_cur[dims, :], p_cur, preferred_element_type=F32)
            pair_rows.append(o * inv)
        if g == N_KV_HEADS - 1:
            out = [jnp.concatenate(pair_rows[n:n + 2], axis=0).T for n in range(0, N_Q_HEADS, 2)]
            o_ref[0, cols, :] = jnp.concatenate(out, axis=1).astype(BF16)
            pair_rows = []


def _attn(sinks, qt, k3, vt):
    B, S, _ = k3.shape
    tq = min(ATT_TILE, S)
    per = tq // WINDOW
    chan = lambda b, i: (b, 0, i)
    tok = lambda b, i: (b, i, 0)
    return pl.pallas_call(
        functools.partial(_attn_kernel, tq),
        grid=(B, S // tq),
        in_specs=[pl.BlockSpec(memory_space=pltpu.SMEM),
                  pl.BlockSpec((1, ATT_Q_WIDTH, tq), chan),
                  pl.BlockSpec((1, tq, ATT_KV_WIDTH), tok),
                  pl.BlockSpec((1, WINDOW, ATT_KV_WIDTH), lambda b, i: (b, jnp.maximum(i * per - 1, 0), 0)),
                  pl.BlockSpec((1, ATT_KV_WIDTH, tq), chan),
                  pl.BlockSpec((1, ATT_KV_WIDTH, WINDOW), lambda b, i: (b, 0, jnp.maximum(i * per - 1, 0)))],
        out_specs=pl.BlockSpec((1, tq, ATT_Q_WIDTH), tok),
        out_shape=jax.ShapeDtypeStruct((B, S, ATT_Q_WIDTH), BF16),
        compiler_params=pltpu.CompilerParams(dimension_semantics=("arbitrary", "arbitrary"),
                                             vmem_limit_bytes=VMEM_LIMIT),
        name="attn",
    )(sinks, qt, k3, k3, vt, vt)


def _mlstm_kernel(nb, *refs):
    L = M_CHUNK
    H = M_HEADS
    mqt_ref, mk_ref, mvt_ref, mot_ref = refs[:4]
    grow_refs = refs[4:4 + nb]
    (cwq_ref, cbq_ref, cwk_ref, cbk_ref, brow_ref, nw_ref, y_ref,
     state_ref, m_ref, prevq_ref, ubuf_ref) = refs[4 + nb:]
    cidx = pl.program_id(1)

    @pl.when(cidx == 0)
    def _():
        state_ref[...] = jnp.zeros_like(state_ref)
        m_ref[...] = jnp.zeros_like(m_ref)
        prevq_ref[...] = jnp.zeros_like(prevq_ref)
        ubuf_ref[:, 0:8, :] = jnp.zeros((nb, 8, M_QK_WIDTH), F32)

    ri = lax.broadcasted_iota(jnp.int32, (L, L), 0)
    ci = lax.broadcasted_iota(jnp.int32, (L, L), 1)
    causal_t = ri <= ci
    triu = jnp.where(causal_t, 1.0, 0.0).astype(F32)
    lane = lax.broadcasted_iota(jnp.int32, (8, L), 1)
    r2 = lax.broadcasted_iota(jnp.int32, (2 * L, L), 0)
    c2 = lax.broadcasted_iota(jnp.int32, (2 * L, L), 1)
    shifts = [jnp.where(r2 - c2 == L - k, 1.0, 0.0).astype(BF16) for k in range(1, CONV_WIDTH)]
    ones_rows = jnp.where(lax.broadcasted_iota(jnp.int32, (M_STATE_ROWS - M_V_DIM, L), 0) == 0,
                          1.0, 0.0).astype(BF16)

    pairs = [(bb, h) for bb in range(nb) for h in range(H)]
    states = [state_ref[bb * H + h] for bb, h in pairs]
    m_prevs = [m_ref[bb] for bb in range(nb)]

    def prepare(bb):
        cur = mqt_ref[bb]
        both = jnp.concatenate([prevq_ref[bb], cur], axis=1)
        acc = cbq_ref[...] + cwq_ref[CONV_WIDTH - 1] * cur.astype(F32)
        for k in range(1, CONV_WIDTH):
            acc = acc + cwq_ref[CONV_WIDTH - 1 - k] * jnp.dot(both, shifts[k - 1],
                                                              preferred_element_type=F32)
        qt = (acc * _sigmoid(acc)).astype(BF16)
        ubuf_ref[bb, 8:L + 8, :] = mk_ref[bb].astype(F32)
        acc = cbk_ref[...] + cwk_ref[CONV_WIDTH - 1:CONV_WIDTH, :] * ubuf_ref[bb, 8:L + 8, :]
        for j in range(CONV_WIDTH - 1):
            off = 8 - (CONV_WIDTH - 1) + j
            acc = acc + cwk_ref[j:j + 1, :] * ubuf_ref[bb, off:off + L, :]
        ubuf_ref[bb, 0:8, :] = ubuf_ref[bb, L:L + 8, :]
        kk = (acc * _sigmoid(acc) * (M_QK_DIM ** -0.5)).astype(BF16)

        gr = grow_refs[bb][...] + brow_ref[...]
        b = jnp.dot(_log_sigmoid(gr), triu, preferred_element_type=F32,
                    precision=lax.Precision.HIGHEST)
        gi = pltpu.roll(gr, 4, axis=0)
        u = gi - b
        cm = u
        for sh in (1, 2, 4, 8, 16, 32, 64):
            if sh < L:
                cm = jnp.maximum(cm, jnp.where(lane >= sh, pltpu.roll(cm, sh, axis=1), NEG))
        m_prev = m_prevs[bb]
        mt = b + jnp.maximum(m_prev, cm)
        wa = jnp.exp(b + m_prev - mt)
        emt = jnp.exp(-mt)
        m_new = jnp.broadcast_to(mt[:, L - 1:L], (8, L))
        b_last = jnp.broadcast_to(b[:, L - 1:L], (8, L))
        wc = jnp.exp(b_last + m_prev - m_new)
        ws = jnp.exp(b_last + u - m_new)
        return qt, kk, b - mt, u.T, wa, emt, wc, ws, m_new

    seqs = [prepare(bb) for bb in range(nb)]

    def products(bb, h, state):
        qt, kk = seqs[bb][0], seqs[bb][1]
        qh = qt[h * M_QK_DIM:(h + 1) * M_QK_DIM, :]
        kh = kk[:, h * M_QK_DIM:(h + 1) * M_QK_DIM]
        return (kh, jnp.dot(kh, qh, preferred_element_type=F32),
                jnp.dot(state.astype(BF16), qh, preferred_element_type=F32))

    new_states, outs = [], []
    depth = M_LOOKAHEAD
    queue = [products(*pairs[n], states[n]) for n in range(min(depth, len(pairs)))]
    for n, ((bb, h), state) in enumerate(zip(pairs, states)):
        kh, st, inter = queue.pop(0)
        if n + depth < len(pairs):
            queue.append(products(*pairs[n + depth], states[n + depth]))
        _, _, bmt, ucol, wa, emt, wc, ws, _ = seqs[bb]
        row = slice(H + h, H + h + 1)
        wq = jnp.exp(jnp.where(causal_t, bmt[row, :] + ucol[:, H + h:H + h + 1], NEG))
        sw = (st * wq).astype(BF16)
        vext = jnp.concatenate([mvt_ref[bb, h * M_V_DIM:(h + 1) * M_V_DIM, :], ones_rows], axis=0)
        res = wa[row, :] * inter + jnp.dot(vext, sw, preferred_element_type=F32)
        num = res[:M_V_DIM, :]
        den = res[M_V_DIM:M_V_DIM + 1, :]
        hb = num / jnp.maximum(jnp.abs(den), emt[row, :])
        vw = (vext.astype(F32) * ws[row, :]).astype(BF16)
        new_states.append(wc[row, 0:1] * state + jnp.dot(vw, kh, preferred_element_type=F32))
        hn = hb * lax.rsqrt(jnp.mean(hb * hb, axis=0, keepdims=True) + EPS)
        hn = hn * nw_ref[h * M_V_DIM:(h + 1) * M_V_DIM, :]
        og = _sigmoid(mot_ref[bb, h * M_V_DIM:(h + 1) * M_V_DIM, :].astype(F32))
        outs.append((og * hn).T.astype(BF16))

    for i, (bb, h) in enumerate(pairs):
        state_ref[bb * H + h] = new_states[i]
    for bb in range(nb):
        m_ref[bb] = seqs[bb][8]
        prevq_ref[bb] = mqt_ref[bb]
        y_ref[bb] = jnp.concatenate(outs[bb * H:(bb + 1) * H], axis=1)


def _mlstm(mqt, mk3, mvt, mot, grow, conv_w, conv_b, bias_row, norm_w):
    B, S, _ = mk3.shape
    L = M_CHUNK
    nb = max(d for d in range(1, M_BATCH + 1) if B % d == 0)
    nc = S // L
    assert L == LANES, "per-head scalars are kept lane-replicated next to (8, L) gate rows"
    tok = lambda b, c: (b, c, 0)
    chan = lambda b, c: (b, 0, c)
    const2 = lambda b, c: (0, 0)
    const3 = lambda b, c: (0, 0, 0)
    grow_specs = [pl.BlockSpec((8, L), functools.partial(lambda b, c, n: (0, (b * nb + n) * nc + c), n=n))
                  for n in range(nb)]
    rep = lambda v: jnp.broadcast_to(v.astype(F32)[..., None], v.shape + (L,))
    cwq, cbq = rep(conv_w[:, :M_QK_WIDTH]), rep(conv_b[:M_QK_WIDTH])
    cwk, cbk = conv_w[:, M_QK_WIDTH:].astype(F32), conv_b[M_QK_WIDTH:].reshape(1, -1).astype(F32)
    return pl.pallas_call(
        functools.partial(_mlstm_kernel, nb),
        grid=(B // nb, nc),
        in_specs=[pl.BlockSpec((nb, M_QK_WIDTH, L), chan),
                  pl.BlockSpec((nb, L, M_QK_WIDTH), tok),
                  pl.BlockSpec((nb, M_V_WIDTH, L), chan),
                  pl.BlockSpec((nb, M_V_WIDTH, L), chan),
                  *grow_specs,
                  pl.BlockSpec((CONV_WIDTH, M_QK_WIDTH, L), const3),
                  pl.BlockSpec((M_QK_WIDTH, L), const2),
                  pl.BlockSpec((CONV_WIDTH, M_QK_WIDTH), const2),
                  pl.BlockSpec((1, M_QK_WIDTH), const2),
                  pl.BlockSpec((8, 1), const2),
                  pl.BlockSpec((M_V_WIDTH, L), const2)],
        out_specs=pl.BlockSpec((nb, L, M_V_WIDTH), tok),
        out_shape=jax.ShapeDtypeStruct((B, S, M_V_WIDTH), BF16),
        scratch_shapes=[pltpu.VMEM((nb * M_HEADS, M_STATE_ROWS, M_QK_DIM), F32),
                        pltpu.VMEM((nb, 8, LANES), F32),
                        pltpu.VMEM((nb, M_QK_WIDTH, L), BF16),
                        pltpu.VMEM((nb, L + 8, M_QK_WIDTH), F32)],
        compiler_params=pltpu.CompilerParams(dimension_semantics=("arbitrary", "arbitrary"),
                                             vmem_limit_bytes=VMEM_LIMIT),
        name="mlstm",
    )(mqt, mk3, mvt, mot, *([grow] * nb), cwq, cbq, cwk, cbk, bias_row, rep(norm_w))


def _outproj_kernel(x_ref, ya_ref, ym_ref, nmix_ref, wgab_ref, wa_ref, wm_ref, wo_ref, nw_ref, wr_ref, br_ref,
                    x1_ref, h2_ref, meta_ref, code_ref, cnt_ref, cnt_scr):
    tm = x_ref.shape[0]
    i = pl.program_id(0)

    @pl.when(i == 0)
    def _():
        cnt_scr[...] = jnp.zeros_like(cnt_scr)

    pa = jnp.dot(ya_ref[...], wa_ref[...], preferred_element_type=F32)
    pm = jnp.dot(ym_ref[...], wm_ref[...], preferred_element_type=F32)
    x = x_ref[...]
    h1 = _rms(x, nmix_ref[...]).astype(BF16)
    ga = jnp.dot(h1, wgab_ref[:, :D_MODEL], preferred_element_type=F32)
    gb = jnp.dot(h1, wgab_ref[:, D_MODEL:], preferred_element_type=F32)
    mix = _sigmoid(ga) * pa + _sigmoid(gb) * pm
    x1 = x + jnp.dot(mix.astype(BF16), wo_ref[...], preferred_element_type=F32)
    x1_ref[...] = x1
    h2 = _rms(x1, nw_ref[...])
    h2_ref[...] = _pack_rows(h2)

    nt = (((1,), (1,)), ((), ()))
    h_hi = h2.astype(BF16)
    h_lo = (h2 - h_hi.astype(F32)).astype(BF16)
    hw = lax.dot_general(wr_ref[...], h_hi, nt, preferred_element_type=F32)
    logits = (hw[:LANES] + hw[LANES:]
              + lax.dot_general(wr_ref[:LANES, :], h_lo, nt, preferred_element_type=F32))[:ROUTER_ROWS]
    logits = logits + br_ref[...]
    row8 = lax.broadcasted_iota(jnp.int32, (8, tm), 0)

    def first_argmax(v):
        mx = jnp.max(v, axis=0, keepdims=True)
        idx = jnp.min(jnp.where(v == mx, row8, 8), axis=0, keepdims=True)
        return mx, idx

    gl = jnp.where(row8 < N_GROUPS, logits[0:8], NEG)
    gmax, gi = first_argmax(gl)
    gp = 1.0 / jnp.sum(jnp.exp(gl - gmax), axis=0, keepdims=True)
    el = jnp.zeros((8, tm), F32)
    for g in range(N_GROUPS):
        el = jnp.where(gi == g, logits[8 + 8 * g:16 + 8 * g], el)
    v1, j1 = first_argmax(el)
    v2, j2 = first_argmax(jnp.where(row8 == j1, NEG, el))
    t = jnp.exp(v2 - v1)
    w1 = gp / (1.0 + t)
    w2 = gp * t / (1.0 + t)
    e1 = gi * EXPERTS_PER_GROUP + j1
    e2 = gi * EXPERTS_PER_GROUP + j2

    erow = lax.broadcasted_iota(jnp.int32, (N_EXPERTS, tm), 0)
    hit1 = erow == e1
    hit2 = erow == e2
    onehot = jnp.where(hit1 | hit2, 1.0, 0.0)
    ri = lax.broadcasted_iota(jnp.int32, (tm, tm), 0)
    ci = lax.broadcasted_iota(jnp.int32, (tm, tm), 1)
    before = jnp.where(ri < ci, 1.0, 0.0).astype(BF16)
    rank = jnp.dot(onehot.astype(BF16), before, preferred_element_type=F32) + cnt_scr[...]
    r1 = jnp.sum(jnp.where(hit1, rank, 0.0), axis=0, keepdims=True)
    r2 = jnp.sum(jnp.where(hit2, rank, 0.0), axis=0, keepdims=True)
    cnt = jnp.broadcast_to((rank + onehot)[:, tm - 1:tm], (N_EXPERTS, tm))
    cnt_scr[...] = cnt
    cnt_ref[...] = cnt[:, :LANES]

    c1 = e1 * RANK_RADIX + r1.astype(jnp.int32)
    c2 = e2 * RANK_RADIX + r2.astype(jnp.int32)
    code_ref[...] = jnp.where(row8 == 0, c1, jnp.where(row8 == 1, c2, 0))
    wrow = lax.broadcasted_iota(jnp.int32, (LANES, tm), 0)
    meta_ref[...] = jnp.where(wrow == 4, w1, jnp.where(wrow == 5, w2, 0.0)).T


def _outproj(x2, ya, ym, norm_mix, w_gab, wa, wm, wo, norm_w, w_rt, b_rt):
    T = x2.shape[0]
    tm = b_rt.shape[1]
    row = lambda i: (i, 0)
    const = lambda i: (0, 0)
    resident = lambda shape: pl.BlockSpec(shape, const, pipeline_mode=pl.Buffered(1))
    return pl.pallas_call(
        _outproj_kernel,
        grid=(T // tm,),
        in_specs=[pl.BlockSpec((tm, D_MODEL), row),
                  pl.BlockSpec((tm, ATT_Q_WIDTH), row),
                  pl.BlockSpec((tm, M_V_WIDTH), row),
                  resident((1, D_MODEL)),
                  resident((D_MODEL, 2 * D_MODEL)),
                  resident((ATT_Q_WIDTH, D_MODEL)),
                  resident((M_V_WIDTH, D_MODEL)),
                  resident((D_MODEL, D_MODEL)),
                  resident((1, D_MODEL)),
                  resident((2 * LANES, D_MODEL)),
                  resident((ROUTER_ROWS, tm))],
        out_specs=[pl.BlockSpec((tm, D_MODEL), row),
                   pl.BlockSpec((tm, PACKED), row),
                   pl.BlockSpec((tm, LANES), row),
                   pl.BlockSpec((8, tm), lambda i: (0, i)),
                   pl.BlockSpec((N_EXPERTS, LANES), const)],
        out_shape=[jax.ShapeDtypeStruct((T, D_MODEL), F32),
                   jax.ShapeDtypeStruct((T, PACKED), jnp.int32),
                   jax.ShapeDtypeStruct((T, LANES), F32),
                   jax.ShapeDtypeStruct((8, T), jnp.int32),
                   jax.ShapeDtypeStruct((N_EXPERTS, LANES), F32)],
        scratch_shapes=[pltpu.VMEM((N_EXPERTS, tm), F32)],
        compiler_params=pltpu.CompilerParams(dimension_semantics=("arbitrary",),
                                             vmem_limit_bytes=VMEM_LIMIT),
        name="outproj",
    )(x2, ya, ym, norm_mix, w_gab, wa, wm, wo, norm_w, w_rt, b_rt)


def _slots_kernel(cnt_ref, pstart_ref, blk_e_ref, nvalid_ref, nused_ref):
    R = SLOT_BLOCK
    nblk = blk_e_ref.shape[0]

    def per_expert(e, run):
        c = cnt_ref[e]
        end = run + ((c + R - 1) // R) * R
        pstart_ref[e] = run

        def set_blk(b, carry):
            blk_e_ref[b] = e
            nvalid_ref[b] = jnp.minimum(run + c - b * R, R)
            return carry
        lax.fori_loop(run // R, end // R, set_blk, 0)
        return end

    total = lax.fori_loop(0, N_EXPERTS, per_expert, 0)
    nused_ref[0] = total // R

    def tail_blk(b, carry):
        blk_e_ref[b] = N_EXPERTS - 1
        nvalid_ref[b] = 0
        return carry
    lax.fori_loop(total // R, nblk, tail_blk, 0)


def _slots(counts, nblk):
    smem = pl.BlockSpec(memory_space=pltpu.SMEM)
    return pl.pallas_call(
        _slots_kernel,
        in_specs=[smem],
        out_specs=[smem, smem, smem, smem],
        out_shape=[jax.ShapeDtypeStruct((N_EXPERTS,), jnp.int32),
                   jax.ShapeDtypeStruct((nblk,), jnp.int32),
                   jax.ShapeDtypeStruct((nblk,), jnp.int32),
                   jax.ShapeDtypeStruct((1,), jnp.int32)],
        name="slots",
    )(counts)


def _experts_kernel(blk_e_ref, nvalid_ref, nused_ref, xs_ref, wg_ref, wu_ref, wd_ref, ys_ref):
    j = pl.program_id(0)
    R = SLOT_BLOCK
    valid = j < nused_ref[0]

    @pl.when(valid)
    def _():
        rows = lax.broadcasted_iota(jnp.int32, (R, PACKED), 0)
        xw = jnp.where(rows < nvalid_ref[j], xs_ref[...], 0)
        xb = _unpack_rows(xw).astype(BF16)
        g = jnp.dot(xb, wg_ref[0], preferred_element_type=F32)
        u = jnp.dot(xb, wu_ref[0], preferred_element_type=F32)
        act = (g * _sigmoid(g) * u).astype(BF16)
        ys_ref[...] = _pack_rows(jnp.dot(act, wd_ref[0], preferred_element_type=F32))


def _experts(blk_e, nvalid, nused, xs, w_gate, w_up, w_down):
    nblk = blk_e.shape[0]
    R = SLOT_BLOCK
    wspec = lambda shape: pl.BlockSpec((1,) + shape, lambda j, be, nv, nu: (be[j], 0, 0))
    grid_spec = pltpu.PrefetchScalarGridSpec(
        num_scalar_prefetch=3,
        grid=(nblk,),
        in_specs=[pl.BlockSpec((R, PACKED), lambda j, be, nv, nu: (jnp.minimum(j, nu[0] - 1), 0)),
                  wspec((D_MODEL, D_EXPERT)),
                  wspec((D_MODEL, D_EXPERT)),
                  wspec((D_EXPERT, D_MODEL))],
        out_specs=pl.BlockSpec((R, PACKED), lambda j, be, nv, nu: (jnp.minimum(j, nu[0] - 1), 0)),
    )
    return pl.pallas_call(
        _experts_kernel,
        grid_spec=grid_spec,
        out_shape=jax.ShapeDtypeStruct((nblk * R, PACKED), jnp.int32),
        compiler_params=pltpu.CompilerParams(dimension_semantics=("arbitrary",),
                                             vmem_limit_bytes=VMEM_LIMIT),
        name="experts",
    )(blk_e, nvalid, nused, xs, w_gate, w_up, w_down)


def _dest_kernel(pstart_ref, code_ref, dest_ref):
    c = code_ref[...]
    e = c >> RANK_BITS
    base = jnp.zeros_like(c)
    for k in range(N_EXPERTS):
        base = jnp.where(e == k, pstart_ref[k], base)
    dest_ref[...] = base + (c & (RANK_RADIX - 1))


def _dests(pstart, code_rows):
    T = code_rows.shape[1]
    tc = min(4096, T)
    return pl.pallas_call(
        _dest_kernel,
        grid=(T // tc,),
        in_specs=[pl.BlockSpec(memory_space=pltpu.SMEM),
                  pl.BlockSpec((8, tc), lambda i: (0, i))],
        out_specs=pl.BlockSpec((8, tc), lambda i: (0, i)),
        out_shape=jax.ShapeDtypeStruct((8, T), jnp.int32),
        name="dests",
    )(pstart, code_rows)


def _sc_workers():
    info = plsc.get_sparse_core_info()
    return info.num_cores, info.num_cores * info.num_subcores


def _sc_scatter_rows(src, idx, n_out):
    n_src, width = src.shape
    nc, nw = _sc_workers()
    per_w = n_src // nw
    ch = SC_CHUNK
    nch = per_w // ch
    assert idx.shape[0] == 2 * n_src and per_w * nw == n_src and nch * ch == per_w and nch % 2 == 0
    mesh = plsc.VectorSubcoreMesh(core_axis_name="c", subcore_axis_name="s")
    dma = pltpu.SemaphoreType.DMA

    @functools.partial(
        pl.kernel, mesh=mesh, out_type=jax.ShapeDtypeStruct((n_out, width), src.dtype),
        scratch_types=[pltpu.VMEM((per_w,), jnp.int32), pltpu.VMEM((per_w,), jnp.int32),
                       pltpu.VMEM((ch, width), src.dtype), pltpu.VMEM((ch, width), src.dtype),
                       dma, dma, dma, dma, dma, dma],
        name="sc_scatter")
    def scatter(src_hbm, idx_hbm, out_hbm, idx_a, idx_b, buf0, buf1, in0, in1, out0a, out0b, out1a, out1b):
        base = (lax.axis_index("s") * nc + lax.axis_index("c")) * per_w
        pltpu.sync_copy(idx_hbm.at[pl.ds(base, per_w)], idx_a)
        pltpu.sync_copy(idx_hbm.at[pl.ds(n_src + base, per_w)], idx_b)

        def read(c, buf, sem):
            return pltpu.make_async_copy(src_hbm.at[pl.ds(base + c * ch, ch)], buf, sem)

        def writes(c, buf, sem_a, sem_b):
            rows = pl.ds(c * ch, ch)
            return (pltpu.make_async_copy(buf, out_hbm.at[idx_a.at[rows]], sem_a),
                    pltpu.make_async_copy(buf, out_hbm.at[idx_b.at[rows]], sem_b))

        def start(copies):
            for cp in copies:
                cp.start()

        def wait(copies):
            for cp in copies:
                cp.wait()

        read(0, buf0, in0).start()

        @pl.loop(0, nch, step=2)
        def _(c):
            @pl.when(c > 0)
            def _():
                wait(writes(c - 1, buf1, out1a, out1b))
            read(c + 1, buf1, in1).start()
            read(c, buf0, in0).wait()
            start(writes(c, buf0, out0a, out0b))
            read(c + 1, buf1, in1).wait()
            start(writes(c + 1, buf1, out1a, out1b))
            wait(writes(c, buf0, out0a, out0b))

            @pl.when(c + 2 < nch)
            def _():
                read(c + 2, buf0, in0).start()

        wait(writes(nch - 1, buf1, out1a, out1b))

    return scatter(src, idx)


def _sc_gather_rows(src, idx):
    M = idx.shape[0]
    width = src.shape[1]
    nc, nw = _sc_workers()
    per_w = M // nw
    ch = SC_CHUNK
    nch = per_w // ch
    assert per_w * nw == M and nch * ch == per_w and nch % 2 == 0
    mesh = plsc.VectorSubcoreMesh(core_axis_name="c", subcore_axis_name="s")
    dma = pltpu.SemaphoreType.DMA

    @functools.partial(
        pl.kernel, mesh=mesh, out_type=jax.ShapeDtypeStruct((M, width), src.dtype),
        scratch_types=[pltpu.VMEM((per_w,), jnp.int32),
                       pltpu.VMEM((ch, width), src.dtype), pltpu.VMEM((ch, width), src.dtype),
                       dma, dma, dma, dma],
        name="sc_gather")
    def gather(src_hbm, idx_hbm, out_hbm, idx_v, buf0, buf1, in0, in1, out0, out1):
        base = (lax.axis_index("s") * nc + lax.axis_index("c")) * per_w
        pltpu.sync_copy(idx_hbm.at[pl.ds(base, per_w)], idx_v)

        def read(c, buf, sem):
            return pltpu.make_async_copy(src_hbm.at[idx_v.at[pl.ds(c * ch, ch)]], buf, sem)

        def write(c, buf, sem):
            return pltpu.make_async_copy(buf, out_hbm.at[pl.ds(base + c * ch, ch)], sem)

        read(0, buf0, in0).start()

        @pl.loop(0, nch, step=2)
        def _(c):
            @pl.when(c > 0)
            def _():
                write(c - 1, buf1, out1).wait()
            read(c + 1, buf1, in1).start()
            read(c, buf0, in0).wait()
            write(c, buf0, out0).start()
            read(c + 1, buf1, in1).wait()
            write(c + 1, buf1, out1).start()
            write(c, buf0, out0).wait()

            @pl.when(c + 2 < nch)
            def _():
                read(c + 2, buf0, in0).start()

        write(nch - 1, buf1, out1).wait()

    return gather(src, idx)


def _final_kernel(x1_ref, y1_ref, y2_ref, meta_ref, nw_ref, o_ref):
    w1 = meta_ref[:, 4:5]
    w2 = meta_ref[:, 5:6]
    x2 = x1_ref[...] + (w1 * _unpack_rows(y1_ref[...]) + w2 * _unpack_rows(y2_ref[...]))
    o_ref[...] = _rms(x2, nw_ref[...])


def _final(x1, yg, meta, norm_w):
    T = x1.shape[0]
    tm = min(COMB_TILE, T)
    nt = T // tm
    row = lambda i: (i, 0)
    return pl.pallas_call(
        _final_kernel,
        grid=(nt,),
        in_specs=[pl.BlockSpec((tm, D_MODEL), row),
                  pl.BlockSpec((tm, PACKED), row),
                  pl.BlockSpec((tm, PACKED), lambda i: (i + nt, 0)),
                  pl.BlockSpec((tm, LANES), row),
                  pl.BlockSpec((1, D_MODEL), lambda i: (0, 0))],
        out_specs=pl.BlockSpec((tm, D_MODEL), row),
        out_shape=jax.ShapeDtypeStruct((T, D_MODEL), F32),
        compiler_params=pltpu.CompilerParams(dimension_semantics=("arbitrary",),
                                             vmem_limit_bytes=VMEM_LIMIT),
        name="final",
    )(x1, yg, yg, meta, norm_w)


def _layer(x, norm_mix_w, w_in, conv_w, conv_b, b_igate, b_fgate, attn_sinks, mlstm_norm_w,
           w_attn_o, w_mlstm_o, w_out, norm_ffn_w, w_group, b_group, w_router, b_router,
           w_gate, w_up, w_down, out_norm_w):
    B, S, D = x.shape
    T = B * S
    x2 = x.reshape(T, D)

    w_tok = jnp.concatenate([w_in[:, _O_AK:_O_AV], w_in[:, _O_MK:_O_MV]], axis=1).astype(BF16)
    w_chan = jnp.concatenate([w_in[:, _O_AQ:_O_AK] * (HEAD_DIM ** -0.5), w_in[:, _O_AV:_O_MQ],
                              w_in[:, _O_MQ:_O_MK], w_in[:, _O_MV:_O_MI]], axis=1).astype(BF16).T
    w_gates = jnp.pad(w_in[:, _O_MI:_O_GA], ((0, 0), (0, LANES - 2 * M_HEADS))).astype(BF16)
    expert_ws = [w.reshape(-1, w.shape[-1]) for w in (w_gate, w_up, w_down)]
    ak, mk, aqt, avt, mqt, mvt, mot, grow, wg16, wu16, wd16 = _inproj(
        x2, norm_mix_w.reshape(1, D), w_tok, w_chan, w_gates, expert_ws, B, S)

    ya = _attn(attn_sinks.astype(F32), aqt, ak.reshape(B, S, -1), avt)

    bias = jnp.concatenate([b_igate, b_fgate]).astype(F32)
    ym = _mlstm(mqt, mk.reshape(B, S, -1), mvt, mot, grow, conv_w, conv_b,
                bias.reshape(2 * M_HEADS, 1), mlstm_norm_w)

    gpad = jnp.zeros((8 - N_GROUPS, D), F32)
    w_rt = jnp.concatenate([w_group.T, gpad, w_router.T, jnp.zeros((LANES - ROUTER_ROWS, D), F32)], axis=0)
    w_rt_hi = w_rt.astype(BF16)
    w_rt = jnp.concatenate([w_rt_hi, (w_rt - w_rt_hi.astype(F32)).astype(BF16)], axis=0)
    b_rt = jnp.concatenate([b_group, jnp.zeros((8 - N_GROUPS,), F32), b_router]).astype(F32)
    b_rt = jnp.broadcast_to(b_rt[:, None], (ROUTER_ROWS, min(TOK_TILE, T)))
    x1, h2p, meta, code_rows, cnt = _outproj(x2, ya.reshape(T, -1), ym.reshape(T, -1),
                                            norm_mix_w.reshape(1, D), w_in[:, _O_GA:].astype(BF16),
                                            w_attn_o.astype(BF16), w_mlstm_o.astype(BF16),
                                            w_out.astype(BF16), norm_ffn_w.reshape(1, D), w_rt, b_rt)

    counts = cnt[:, 0].astype(jnp.int32)
    nblk = (2 * T) // SLOT_BLOCK + N_EXPERTS
    pstart, blk_e, nvalid, nused = _slots(counts, nblk)
    dest = _dests(pstart, code_rows)[0:2].reshape(2 * T)

    xs = _sc_scatter_rows(h2p, dest, nblk * SLOT_BLOCK)
    ys = _experts(blk_e, nvalid, nused, xs, wg16.reshape(w_gate.shape), wu16.reshape(w_up.shape),
                  wd16.reshape(w_down.shape))
    yg = _sc_gather_rows(ys, dest)
    out = _final(x1, yg, meta, out_norm_w.reshape(1, D))
    return out.reshape(B, S, D)


def kernel(x, norm_mix_w, w_in, conv_w, conv_b, b_igate, b_fgate, attn_sinks, mlstm_norm_w, w_attn_o,
           w_mlstm_o, w_out, norm_ffn_w, w_group, b_group, w_router, b_router, w_gate, w_up, w_down,
           norm_final_w):
    depth = w_in.shape[0]
    assert depth == 1, "final RMSNorm is fused into the last layer's combine kernel"
    return _layer(x, norm_mix_w[0], w_in[0], conv_w[0], conv_b[0], b_igate[0], b_fgate[0],
                  attn_sinks[0], mlstm_norm_w[0], w_attn_o[0], w_mlstm_o[0], w_out[0], norm_ffn_w[0],
                  w_group[0], b_group[0], w_router[0], b_router[0], w_gate[0], w_up[0], w_down[0],
                  norm_final_w)
```

```python
import functools

import jax
import jax.numpy as jnp
from jax import lax
from jax.experimental import pallas as pl
from jax.experimental.pallas import tpu as pltpu
from jax.experimental.pallas import tpu_sc as plsc

F32 = jnp.float32
BF16 = jnp.bfloat16

D_MODEL = 1024
N_Q_HEADS = 8
N_KV_HEADS = 2
HEAD_DIM = 64
WINDOW = 128
GQA_GROUP = N_Q_HEADS // N_KV_HEADS
M_HEADS = 4
M_QK_DIM = 64
M_V_DIM = 128
CONV_WIDTH = 4
N_GROUPS = 4
EXPERTS_PER_GROUP = 8
N_EXPERTS = N_GROUPS * EXPERTS_PER_GROUP
D_EXPERT = 512
EPS = 1e-6

ATT_Q_WIDTH = N_Q_HEADS * HEAD_DIM
ATT_KV_WIDTH = N_KV_HEADS * HEAD_DIM
M_QK_WIDTH = M_HEADS * M_QK_DIM
M_V_WIDTH = M_HEADS * M_V_DIM

LANES = 128
NEG = -1e30
V7X_VMEM_BYTES = 64 * 1024 * 1024
VMEM_LIMIT = V7X_VMEM_BYTES - 8 * 1024 * 1024

IN_TILE = 1024
TOK_TILE = 1024
ATT_TILE = 2048
LOOKAHEAD = 2
M_LOOKAHEAD = 4
M_CHUNK = 128
M_BATCH = 8
M_STATE_ROWS = M_V_DIM + 16
SLOT_BLOCK = 512
COMB_TILE = 1024
RANK_RADIX = 65536
RANK_BITS = 16
ROUTER_ROWS = 8 + N_EXPERTS
PACKED = D_MODEL // 2
SC_CHUNK = 64

_O_AQ = 0
_O_AK = _O_AQ + ATT_Q_WIDTH
_O_AV = _O_AK + ATT_KV_WIDTH
_O_MQ = _O_AV + ATT_KV_WIDTH
_O_MK = _O_MQ + M_QK_WIDTH
_O_MV = _O_MK + M_QK_WIDTH
_O_MO = _O_MV + M_V_WIDTH
_O_MI = _O_MO + M_V_WIDTH
_O_MF = _O_MI + M_HEADS
_O_GA = _O_MF + M_HEADS


def _rms(x, w):
    return x * lax.rsqrt(jnp.mean(x * x, axis=-1, keepdims=True) + EPS) * w


def _sigmoid(x):
    return 0.5 * jnp.tanh(0.5 * x) + 0.5


def _pack_rows(x):
    half = x.shape[1] // 2
    bits = lax.bitcast_convert_type(x.astype(BF16).astype(F32), jnp.uint32)
    packed = (bits[:, :half] >> 16) | (bits[:, half:] & jnp.uint32(0xFFFF0000))
    return lax.bitcast_convert_type(packed, jnp.int32)


def _unpack_rows(w):
    u = lax.bitcast_convert_type(w, jnp.uint32)
    lo = lax.bitcast_convert_type(u << 16, F32)
    hi = lax.bitcast_convert_type(u & jnp.uint32(0xFFFF0000), F32)
    return jnp.concatenate([lo, hi], axis=-1)


def _log_sigmoid(x):
    return jnp.minimum(x, 0.0) - jnp.log1p(jnp.exp(-jnp.abs(x)))


_TOK_SPLITS = (ATT_KV_WIDTH, M_QK_WIDTH)
_CHAN_SPLITS = (ATT_Q_WIDTH, ATT_KV_WIDTH, M_QK_WIDTH, M_V_WIDTH, M_V_WIDTH)
GATE_ROWS = 16


def _inproj_kernel(x_ref, nw_ref, w_ref, wt_ref, eg_ref, eu_ref, ed_ref, *out_refs):
    tok_refs = out_refs[:len(_TOK_SPLITS)]
    chan_refs = out_refs[len(_TOK_SPLITS):len(_TOK_SPLITS) + len(_CHAN_SPLITS)]
    grow_ref = out_refs[len(_TOK_SPLITS) + len(_CHAN_SPLITS)]
    for src, dst in zip((eg_ref, eu_ref, ed_ref), out_refs[-3:]):
        dst[...] = src[...].astype(BF16)
    h = _rms(x_ref[...], nw_ref[...]).astype(BF16)
    lo = 0
    for ref, width in zip(tok_refs, _TOK_SPLITS):
        ref[...] = jnp.dot(h, w_ref[:, lo:lo + width], preferred_element_type=F32).astype(BF16)
        lo += width
    lo = 0
    for ref, width in zip(chan_refs, _CHAN_SPLITS):
        ref[0] = lax.dot_general(wt_ref[lo:lo + width, :], h, (((1,), (1,)), ((), ())),
                                 preferred_element_type=F32).astype(BF16)
        lo += width
    grow_ref[...] = lax.dot_general(wt_ref[lo:lo + GATE_ROWS, :], h, (((1,), (1,)), ((), ())),
                                    preferred_element_type=F32)[0:2 * M_HEADS, :]


def _inproj(x2, norm_w, w_tok, w_chan, expert_ws, B, S):
    T = x2.shape[0]
    tm = min(IN_TILE, S)
    tpb = S // tm
    nsteps = T // tm
    row = lambda i: (i, 0)
    const = lambda i: (0, 0)
    chan = lambda i: (i // tpb, 0, i % tpb)
    cast_specs = [pl.BlockSpec((w.shape[0] // nsteps, w.shape[1]), row) for w in expert_ws]
    assert all(w.shape[0] % (8 * nsteps) == 0 for w in expert_ws)
    return pl.pallas_call(
        _inproj_kernel,
        grid=(nsteps,),
        in_specs=[pl.BlockSpec((tm, D_MODEL), row),
                  pl.BlockSpec((1, D_MODEL), const),
                  pl.BlockSpec(w_tok.shape, const),
                  pl.BlockSpec(w_chan.shape, const),
                  *cast_specs],
        out_specs=[*[pl.BlockSpec((tm, w), row) for w in _TOK_SPLITS],
                   *[pl.BlockSpec((1, w, tm), chan) for w in _CHAN_SPLITS],
                   pl.BlockSpec((8, tm), lambda i: (0, i)),
                   *cast_specs],
        out_shape=[*[jax.ShapeDtypeStruct((T, w), BF16) for w in _TOK_SPLITS],
                   *[jax.ShapeDtypeStruct((B, w, S), BF16) for w in _CHAN_SPLITS],
                   jax.ShapeDtypeStruct((8, T), F32),
                   *[jax.ShapeDtypeStruct(w.shape, BF16) for w in expert_ws]],
        compiler_params=pltpu.CompilerParams(dimension_semantics=("arbitrary",),
                                             vmem_limit_bytes=VMEM_LIMIT),
        name="inproj",
    )(x2, norm_w, w_tok, w_chan, *expert_ws)


def _attn_kernel(tq, sink_ref, qt_ref, k_ref, kp_ref, vt_ref, vtp_ref, o_ref):
    i = pl.program_id(1)
    W = WINDOW
    G = GQA_GROUP
    u = lax.broadcasted_iota(jnp.int32, (W, W), 0)
    t = lax.broadcasted_iota(jnp.int32, (W, W), 1)
    from_prev = u > t

    def keys_values(j):
        if j == 0:
            return kp_ref[0], vtp_ref[0], k_ref[0, 0:W, :], vt_ref[0, :, 0:W]
        return (k_ref[0, (j - 1) * W:j * W, :], vt_ref[0, :, (j - 1) * W:j * W],
                k_ref[0, j * W:(j + 1) * W, :], vt_ref[0, :, j * W:(j + 1) * W])

    def scores(j, g):
        k_prev, _, k_cur, _ = keys_values(j)
        dims = slice(g * HEAD_DIM, (g + 1) * HEAD_DIM)
        qg = jnp.concatenate([qt_ref[0, h * HEAD_DIM:(h + 1) * HEAD_DIM, j * W:(j + 1) * W]
                              for h in range(g * G, (g + 1) * G)], axis=1)
        return (jnp.dot(k_prev[:, dims], qg, preferred_element_type=F32),
                jnp.dot(k_cur[:, dims], qg, preferred_element_type=F32))

    tasks = [(j, g) for j in range(tq // W) for g in range(N_KV_HEADS)]
    queue = [scores(*t) for t in tasks[:LOOKAHEAD]]
    pair_rows = []
    for n_task, (j, g) in enumerate(tasks):
        s_prev, s_cur = queue.pop(0)
        if n_task + LOOKAHEAD < len(tasks):
            queue.append(scores(*tasks[n_task + LOOKAHEAD]))
        cols = slice(j * W, (j + 1) * W)
        _, vt_prev, _, vt_cur = keys_values(j)
        dims = slice(g * HEAD_DIM, (g + 1) * HEAD_DIM)
        for n in range(G):
            hc = slice(n * W, (n + 1) * W)
            sp = s_prev[:, hc]
            if j == 0:
                sp = jnp.where(i > 0, sp, NEG)
            s = jnp.where(from_prev, sp, s_cur[:, hc])
            sink = sink_ref[g * G + n]
            m = jnp.maximum(jnp.max(s, axis=0, keepdims=True), sink)
            p = jnp.exp(s - m)
            inv = 1.0 / (jnp.sum(p, axis=0, keepdims=True) + jnp.exp(sink - m))
            p_prev = jnp.where(from_prev, p, 0.0).astype(BF16)
            p_cur = jnp.where(from_prev, 0.0, p).astype(BF16)
            o = jnp.dot(vt_prev[dims, :], p_prev, preferred_element_type=F32) \
                + jnp.dot(vt_cur[dims, :], p_cur, preferred_element_type=F32)
            pair_rows.append(o * inv)
        if g == N_KV_HEADS - 1:
            out = [jnp.concatenate(pair_rows[n:n + 2], axis=0).T for n in range(0, N_Q_HEADS, 2)]
            o_ref[0, cols, :] = jnp.concatenate(out, axis=1).astype(BF16)
            pair_rows = []


def _attn(sinks, qt, k3, vt):
    B, S, _ = k3.shape
    tq = min(ATT_TILE, S)
    per = tq // WINDOW
    chan = lambda b, i: (b, 0, i)
    tok = lambda b, i: (b, i, 0)
    return pl.pallas_call(
        functools.partial(_attn_kernel, tq),
        grid=(B, S // tq),
        in_specs=[pl.BlockSpec(memory_space=pltpu.SMEM),
                  pl.BlockSpec((1, ATT_Q_WIDTH, tq), chan),
                  pl.BlockSpec((1, tq, ATT_KV_WIDTH), tok),
                  pl.BlockSpec((1, WINDOW, ATT_KV_WIDTH), lambda b, i: (b, jnp.maximum(i * per - 1, 0), 0)),
                  pl.BlockSpec((1, ATT_KV_WIDTH, tq), chan),
                  pl.BlockSpec((1, ATT_KV_WIDTH, WINDOW), lambda b, i: (b, 0, jnp.maximum(i * per - 1, 0)))],
        out_specs=pl.BlockSpec((1, tq, ATT_Q_WIDTH), tok),
        out_shape=jax.ShapeDtypeStruct((B, S, ATT_Q_WIDTH), BF16),
        compiler_params=pltpu.CompilerParams(dimension_semantics=("arbitrary", "arbitrary"),
                                             vmem_limit_bytes=VMEM_LIMIT),
        name="attn",
    )(sinks, qt, k3, k3, vt, vt)


def _mlstm_kernel(nb, *refs):
    L = M_CHUNK
    H = M_HEADS
    mqt_ref, mk_ref, mvt_ref, mot_ref = refs[:4]
    grow_refs = refs[4:4 + nb]
    (cwq_ref, cbq_ref, cwk_ref, cbk_ref, brow_ref, nw_ref, y_ref,
     state_ref, m_ref, prevq_ref, ubuf_ref) = refs[4 + nb:]
    cidx = pl.program_id(1)

    @pl.when(cidx == 0)
    def _():
        state_ref[...] = jnp.zeros_like(state_ref)
        m_ref[...] = jnp.zeros_like(m_ref)
        prevq_ref[...] = jnp.zeros_like(prevq_ref)
        ubuf_ref[:, 0:8, :] = jnp.zeros((nb, 8, M_QK_WIDTH), F32)

    ri = lax.broadcasted_iota(jnp.int32, (L, L), 0)
    ci = lax.broadcasted_iota(jnp.int32, (L, L), 1)
    causal_t = ri <= ci
    triu = jnp.where(causal_t, 1.0, 0.0).astype(F32)
    lane = lax.broadcasted_iota(jnp.int32, (8, L), 1)
    r2 = lax.broadcasted_iota(jnp.int32, (2 * L, L), 0)
    c2 = lax.broadcasted_iota(jnp.int32, (2 * L, L), 1)
    shifts = [jnp.where(r2 - c2 == L - k, 1.0, 0.0).astype(BF16) for k in range(1, CONV_WIDTH)]
    ones_rows = jnp.where(lax.broadcasted_iota(jnp.int32, (M_STATE_ROWS - M_V_DIM, L), 0) == 0,
                          1.0, 0.0).astype(BF16)

    pairs = [(bb, h) for bb in range(nb) for h in range(H)]
    states = [state_ref[bb * H + h] for bb, h in pairs]
    m_prevs = [m_ref[bb] for bb in range(nb)]

    def prepare(bb):
        cur = mqt_ref[bb]
        both = jnp.concatenate([prevq_ref[bb], cur], axis=1)
        acc = cbq_ref[...] + cwq_ref[CONV_WIDTH - 1] * cur.astype(F32)
        for k in range(1, CONV_WIDTH):
            acc = acc + cwq_ref[CONV_WIDTH - 1 - k] * jnp.dot(both, shifts[k - 1],
                                                              preferred_element_type=F32)
        qt = (acc * _sigmoid(acc)).astype(BF16)
        ubuf_ref[bb, 8:L + 8, :] = mk_ref[bb].astype(F32)
        acc = cbk_ref[...] + cwk_ref[CONV_WIDTH - 1:CONV_WIDTH, :] * ubuf_ref[bb, 8:L + 8, :]
        for j in range(CONV_WIDTH - 1):
            off = 8 - (CONV_WIDTH - 1) + j
            acc = acc + cwk_ref[j:j + 1, :] * ubuf_ref[bb, off:off + L, :]
        ubuf_ref[bb, 0:8, :] = ubuf_ref[bb, L:L + 8, :]
        kk = (acc * _sigmoid(acc) * (M_QK_DIM ** -0.5)).astype(BF16)

        gr = grow_refs[bb][...] + brow_ref[...]
        b = jnp.dot(_log_sigmoid(gr), triu, preferred_element_type=F32,
                    precision=lax.Precision.HIGHEST)
        gi = pltpu.roll(gr, 4, axis=0)
        u = gi - b
        cm = u
        for sh in (1, 2, 4, 8, 16, 32, 64):
            if sh < L:
                cm = jnp.maximum(cm, jnp.where(lane >= sh, pltpu.roll(cm, sh, axis=1), NEG))
        m_prev = m_prevs[bb]
        mt = b + jnp.maximum(m_prev, cm)
        wa = jnp.exp(b + m_prev - mt)
        emt = jnp.exp(-mt)
        m_new = jnp.broadcast_to(mt[:, L - 1:L], (8, L))
        b_last = jnp.broadcast_to(b[:, L - 1:L], (8, L))
        wc = jnp.exp(b_last + m_prev - m_new)
        ws = jnp.exp(b_last + u - m_new)
        return qt, kk, b - mt, u.T, wa, emt, wc, ws, m_new

    seqs = [prepare(bb) for bb in range(nb)]

    def products(bb, h, state):
        qt, kk = seqs[bb][0], seqs[bb][1]
        qh = qt[h * M_QK_DIM:(h + 1) * M_QK_DIM, :]
        kh = kk[:, h * M_QK_DIM:(h + 1) * M_QK_DIM]
        return (kh, jnp.dot(kh, qh, preferred_element_type=F32),
                jnp.dot(state.astype(BF16), qh, preferred_element_type=F32))

    new_states, outs = [], []
    depth = M_LOOKAHEAD
    queue = [products(*pairs[n], states[n]) for n in range(min(depth, len(pairs)))]
    for n, ((bb, h), state) in enumerate(zip(pairs, states)):
        kh, st, inter = queue.pop(0)
        if n + depth < len(pairs):
            queue.append(products(*pairs[n + depth], states[n + depth]))
        _, _, bmt, ucol, wa, emt, wc, ws, _ = seqs[bb]
        row = slice(H + h, H + h + 1)
        wq = jnp.exp(jnp.where(causal_t, bmt[row, :] + ucol[:, H + h:H + h + 1], NEG))
        sw = (st * wq).astype(BF16)
        vext = jnp.concatenate([mvt_ref[bb, h * M_V_DIM:(h + 1) * M_V_DIM, :], ones_rows], axis=0)
        res = wa[row, :] * inter + jnp.dot(vext, sw, preferred_element_type=F32)
        num = res[:M_V_DIM, :]
        den = res[M_V_DIM:M_V_DIM + 1, :]
        hb = num / jnp.maximum(jnp.abs(den), emt[row, :])
        vw = (vext.astype(F32) * ws[row, :]).astype(BF16)
        new_states.append(wc[row, 0:1] * state + jnp.dot(vw, kh, preferred_element_type=F32))
        hn = hb * lax.rsqrt(jnp.mean(hb * hb, axis=0, keepdims=True) + EPS)
        hn = hn * nw_ref[h * M_V_DIM:(h + 1) * M_V_DIM, :]
        og = _sigmoid(mot_ref[bb, h * M_V_DIM:(h + 1) * M_V_DIM, :].astype(F32))
        outs.append((og * hn).T.astype(BF16))

    for i, (bb, h) in enumerate(pairs):
        state_ref[bb * H + h] = new_states[i]
    for bb in range(nb):
        m_ref[bb] = seqs[bb][8]
        prevq_ref[bb] = mqt_ref[bb]
        y_ref[bb] = jnp.concatenate(outs[bb * H:(bb + 1) * H], axis=1)


def _mlstm(mqt, mk3, mvt, mot, grow, conv_w, conv_b, bias_row, norm_w):
    B, S, _ = mk3.shape
    L = M_CHUNK
    nb = max(d for d in range(1, M_BATCH + 1) if B % d == 0)
    nc = S // L
    assert L == LANES, "per-head scalars are kept lane-replicated next to (8, L) gate rows"
    tok = lambda b, c: (b, c, 0)
    chan = lambda b, c: (b, 0, c)
    const2 = lambda b, c: (0, 0)
    const3 = lambda b, c: (0, 0, 0)
    grow_specs = [pl.BlockSpec((8, L), functools.partial(lambda b, c, n: (0, (b * nb + n) * nc + c), n=n))
                  for n in range(nb)]
    rep = lambda v: jnp.broadcast_to(v.astype(F32)[..., None], v.shape + (L,))
    cwq, cbq = rep(conv_w[:, :M_QK_WIDTH]), rep(conv_b[:M_QK_WIDTH])
    cwk, cbk = conv_w[:, M_QK_WIDTH:].astype(F32), conv_b[M_QK_WIDTH:].reshape(1, -1).astype(F32)
    return pl.pallas_call(
        functools.partial(_mlstm_kernel, nb),
        grid=(B // nb, nc),
        in_specs=[pl.BlockSpec((nb, M_QK_WIDTH, L), chan),
                  pl.BlockSpec((nb, L, M_QK_WIDTH), tok),
                  pl.BlockSpec((nb, M_V_WIDTH, L), chan),
                  pl.BlockSpec((nb, M_V_WIDTH, L), chan),
                  *grow_specs,
                  pl.BlockSpec((CONV_WIDTH, M_QK_WIDTH, L), const3),
                  pl.BlockSpec((M_QK_WIDTH, L), const2),
                  pl.BlockSpec((CONV_WIDTH, M_QK_WIDTH), const2),
                  pl.BlockSpec((1, M_QK_WIDTH), const2),
                  pl.BlockSpec((8, 1), const2),
                  pl.BlockSpec((M_V_WIDTH, L), const2)],
        out_specs=pl.BlockSpec((nb, L, M_V_WIDTH), tok),
        out_shape=jax.ShapeDtypeStruct((B, S, M_V_WIDTH), BF16),
        scratch_shapes=[pltpu.VMEM((nb * M_HEADS, M_STATE_ROWS, M_QK_DIM), F32),
                        pltpu.VMEM((nb, 8, LANES), F32),
                        pltpu.VMEM((nb, M_QK_WIDTH, L), BF16),
                        pltpu.VMEM((nb, L + 8, M_QK_WIDTH), F32)],
        compiler_params=pltpu.CompilerParams(dimension_semantics=("arbitrary", "arbitrary"),
                                             vmem_limit_bytes=VMEM_LIMIT),
        name="mlstm",
    )(mqt, mk3, mvt, mot, *([grow] * nb), cwq, cbq, cwk, cbk, bias_row, rep(norm_w))


def _outproj_kernel(x_ref, ya_ref, ym_ref, nmix_ref, wgab_ref, wa_ref, wm_ref, wo_ref, nw_ref, wr_ref, br_ref,
                    x1_ref, h2_ref, meta_ref, code_ref, cnt_ref, cnt_scr):
    tm = x_ref.shape[0]
    i = pl.program_id(0)

    @pl.when(i == 0)
    def _():
        cnt_scr[...] = jnp.zeros_like(cnt_scr)

    pa = jnp.dot(ya_ref[...], wa_ref[...], preferred_element_type=F32)
    pm = jnp.dot(ym_ref[...], wm_ref[...], preferred_element_type=F32)
    x = x_ref[...]
    h1 = _rms(x, nmix_ref[...]).astype(BF16)
    ga = jnp.dot(h1, wgab_ref[:, :D_MODEL], preferred_element_type=F32)
    gb = jnp.dot(h1, wgab_ref[:, D_MODEL:], preferred_element_type=F32)
    mix = _sigmoid(ga) * pa + _sigmoid(gb) * pm
    x1 = x + jnp.dot(mix.astype(BF16), wo_ref[...], preferred_element_type=F32)
    x1_ref[...] = x1
    h2 = _rms(x1, nw_ref[...])
    h2_ref[...] = _pack_rows(h2)

    nt = (((1,), (1,)), ((), ()))
    h_hi = h2.astype(BF16)
    h_lo = (h2 - h_hi.astype(F32)).astype(BF16)
    hw = jnp.dot(h_hi, wr_ref[...], preferred_element_type=F32)
    logits = (hw[:, :LANES] + hw[:, LANES:]
              + jnp.dot(h_lo, wr_ref[:, :LANES], preferred_element_type=F32)).T[:ROUTER_ROWS]
    logits = logits + br_ref[...]
    row8 = lax.broadcasted_iota(jnp.int32, (8, tm), 0)

    def first_argmax(v):
        mx = jnp.max(v, axis=0, keepdims=True)
        idx = jnp.min(jnp.where(v == mx, row8, 8), axis=0, keepdims=True)
        return mx, idx

    gl = jnp.where(row8 < N_GROUPS, logits[0:8], NEG)
    gmax, gi = first_argmax(gl)
    gp = 1.0 / jnp.sum(jnp.exp(gl - gmax), axis=0, keepdims=True)
    el = jnp.zeros((8, tm), F32)
    for g in range(N_GROUPS):
        el = jnp.where(gi == g, logits[8 + 8 * g:16 + 8 * g], el)
    v1, j1 = first_argmax(el)
    v2, j2 = first_argmax(jnp.where(row8 == j1, NEG, el))
    t = jnp.exp(v2 - v1)
    w1 = gp / (1.0 + t)
    w2 = gp * t / (1.0 + t)
    e1 = gi * EXPERTS_PER_GROUP + j1
    e2 = gi * EXPERTS_PER_GROUP + j2

    erow = lax.broadcasted_iota(jnp.int32, (N_EXPERTS, tm), 0)
    hit1 = erow == e1
    hit2 = erow == e2
    onehot = jnp.where(hit1 | hit2, 1.0, 0.0)
    ri = lax.broadcasted_iota(jnp.int32, (tm, tm), 0)
    ci = lax.broadcasted_iota(jnp.int32, (tm, tm), 1)
    before = jnp.where(ri < ci, 1.0, 0.0).astype(BF16)
    rank = jnp.dot(onehot.astype(BF16), before, preferred_element_type=F32) + cnt_scr[...]
    r1 = jnp.sum(jnp.where(hit1, rank, 0.0), axis=0, keepdims=True)
    r2 = jnp.sum(jnp.where(hit2, rank, 0.0), axis=0, keepdims=True)
    cnt = jnp.broadcast_to((rank + onehot)[:, tm - 1:tm], (N_EXPERTS, tm))
    cnt_scr[...] = cnt
    cnt_ref[...] = cnt[:, :LANES]

    c1 = e1 * RANK_RADIX + r1.astype(jnp.int32)
    c2 = e2 * RANK_RADIX + r2.astype(jnp.int32)
    code_ref[...] = jnp.where(row8 == 0, c1, jnp.where(row8 == 1, c2, 0))
    wrow = lax.broadcasted_iota(jnp.int32, (LANES, tm), 0)
    meta_ref[...] = jnp.where(wrow == 4, w1, jnp.where(wrow == 5, w2, 0.0)).T


def _outproj(x2, ya, ym, norm_mix, w_gab, wa, wm, wo, norm_w, w_rt, b_rt):
    T = x2.shape[0]
    tm = b_rt.shape[1]
    row = lambda i: (i, 0)
    const = lambda i: (0, 0)
    resident = lambda shape: pl.BlockSpec(shape, const, pipeline_mode=pl.Buffered(1))
    return pl.pallas_call(
        _outproj_kernel,
        grid=(T // tm,),
        in_specs=[pl.BlockSpec((tm, D_MODEL), row),
                  pl.BlockSpec((tm, ATT_Q_WIDTH), row),
                  pl.BlockSpec((tm, M_V_WIDTH), row),
                  resident((1, D_MODEL)),
                  resident((D_MODEL, 2 * D_MODEL)),
                  resident((ATT_Q_WIDTH, D_MODEL)),
                  resident((M_V_WIDTH, D_MODEL)),
                  resident((D_MODEL, D_MODEL)),
                  resident((1, D_MODEL)),
                  resident((D_MODEL, 2 * LANES)),
                  resident((ROUTER_ROWS, tm))],
        out_specs=[pl.BlockSpec((tm, D_MODEL), row),
                   pl.BlockSpec((tm, PACKED), row),
                   pl.BlockSpec((tm, LANES), row),
                   pl.BlockSpec((8, tm), lambda i: (0, i)),
                   pl.BlockSpec((N_EXPERTS, LANES), const)],
        out_shape=[jax.ShapeDtypeStruct((T, D_MODEL), F32),
                   jax.ShapeDtypeStruct((T, PACKED), jnp.int32),
                   jax.ShapeDtypeStruct((T, LANES), F32),
                   jax.ShapeDtypeStruct((8, T), jnp.int32),
                   jax.ShapeDtypeStruct((N_EXPERTS, LANES), F32)],
        scratch_shapes=[pltpu.VMEM((N_EXPERTS, tm), F32)],
        compiler_params=pltpu.CompilerParams(dimension_semantics=("arbitrary",),
                                             vmem_limit_bytes=VMEM_LIMIT),
        name="outproj",
    )(x2, ya, ym, norm_mix, w_gab, wa, wm, wo, norm_w, w_rt, b_rt)


def _slots_kernel(cnt_ref, pstart_ref, blk_e_ref, nvalid_ref, nused_ref):
    R = SLOT_BLOCK
    nblk = blk_e_ref.shape[0]

    def per_expert(e, run):
        c = cnt_ref[e]
        end = run + ((c + R - 1) // R) * R
        pstart_ref[e] = run

        def set_blk(b, carry):
            blk_e_ref[b] = e
            nvalid_ref[b] = jnp.minimum(run + c - b * R, R)
            return carry
        lax.fori_loop(run // R, end // R, set_blk, 0)
        return end

    total = lax.fori_loop(0, N_EXPERTS, per_expert, 0)
    nused_ref[0] = total // R

    def tail_blk(b, carry):
        blk_e_ref[b] = N_EXPERTS - 1
        nvalid_ref[b] = 0
        return carry
    lax.fori_loop(total // R, nblk, tail_blk, 0)


def _slots(counts, nblk):
    smem = pl.BlockSpec(memory_space=pltpu.SMEM)
    return pl.pallas_call(
        _slots_kernel,
        in_specs=[smem],
        out_specs=[smem, smem, smem, smem],
        out_shape=[jax.ShapeDtypeStruct((N_EXPERTS,), jnp.int32),
                   jax.ShapeDtypeStruct((nblk,), jnp.int32),
                   jax.ShapeDtypeStruct((nblk,), jnp.int32),
                   jax.ShapeDtypeStruct((1,), jnp.int32)],
        name="slots",
    )(counts)


def _experts_kernel(blk_e_ref, nvalid_ref, nused_ref, xs_ref, wg_ref, wu_ref, wd_ref, ys_ref):
    j = pl.program_id(0)
    R = SLOT_BLOCK
    valid = j < nused_ref[0]

    @pl.when(valid)
    def _():
        rows = lax.broadcasted_iota(jnp.int32, (R, PACKED), 0)
        xw = jnp.where(rows < nvalid_ref[j], xs_ref[...], 0)
        xb = _unpack_rows(xw).astype(BF16)
        g = jnp.dot(xb, wg_ref[0], preferred_element_type=F32)
        u = jnp.dot(xb, wu_ref[0], preferred_element_type=F32)
        act = (g * _sigmoid(g) * u).astype(BF16)
        ys_ref[...] = _pack_rows(jnp.dot(act, wd_ref[0], preferred_element_type=F32))


def _experts(blk_e, nvalid, nused, xs, w_gate, w_up, w_down):
    nblk = blk_e.shape[0]
    R = SLOT_BLOCK
    wspec = lambda shape: pl.BlockSpec((1,) + shape, lambda j, be, nv, nu: (be[j], 0, 0))
    grid_spec = pltpu.PrefetchScalarGridSpec(
        num_scalar_prefetch=3,
        grid=(nblk,),
        in_specs=[pl.BlockSpec((R, PACKED), lambda j, be, nv, nu: (jnp.minimum(j, nu[0] - 1), 0)),
                  wspec((D_MODEL, D_EXPERT)),
                  wspec((D_MODEL, D_EXPERT)),
                  wspec((D_EXPERT, D_MODEL))],
        out_specs=pl.BlockSpec((R, PACKED), lambda j, be, nv, nu: (jnp.minimum(j, nu[0] - 1), 0)),
    )
    return pl.pallas_call(
        _experts_kernel,
        grid_spec=grid_spec,
        out_shape=jax.ShapeDtypeStruct((nblk * R, PACKED), jnp.int32),
        compiler_params=pltpu.CompilerParams(dimension_semantics=("arbitrary",),
                                             vmem_limit_bytes=VMEM_LIMIT),
        name="experts",
    )(blk_e, nvalid, nused, xs, w_gate, w_up, w_down)


def _dest_kernel(pstart_ref, code_ref, dest_ref):
    c = code_ref[...]
    e = c >> RANK_BITS
    base = jnp.zeros_like(c)
    for k in range(N_EXPERTS):
        base = jnp.where(e == k, pstart_ref[k], base)
    dest_ref[...] = base + (c & (RANK_RADIX - 1))


def _dests(pstart, code_rows):
    T = code_rows.shape[1]
    tc = min(4096, T)
    return pl.pallas_call(
        _dest_kernel,
        grid=(T // tc,),
        in_specs=[pl.BlockSpec(memory_space=pltpu.SMEM),
                  pl.BlockSpec((8, tc), lambda i: (0, i))],
        out_specs=pl.BlockSpec((8, tc), lambda i: (0, i)),
        out_shape=jax.ShapeDtypeStruct((8, T), jnp.int32),
        name="dests",
    )(pstart, code_rows)


def _sc_workers():
    info = plsc.get_sparse_core_info()
    return info.num_cores, info.num_cores * info.num_subcores


def _sc_scatter_rows(src, idx, n_out):
    n_src, width = src.shape
    nc, nw = _sc_workers()
    per_w = n_src // nw
    ch = SC_CHUNK
    nch = per_w // ch
    assert idx.shape[0] == 2 * n_src and per_w * nw == n_src and nch * ch == per_w and nch % 2 == 0
    mesh = plsc.VectorSubcoreMesh(core_axis_name="c", subcore_axis_name="s")
    dma = pltpu.SemaphoreType.DMA

    @functools.partial(
        pl.kernel, mesh=mesh, out_type=jax.ShapeDtypeStruct((n_out, width), src.dtype),
        scratch_types=[pltpu.VMEM((per_w,), jnp.int32), pltpu.VMEM((per_w,), jnp.int32),
                       pltpu.VMEM((ch, width), src.dtype), pltpu.VMEM((ch, width), src.dtype),
                       dma, dma, dma, dma, dma, dma],
        name="sc_scatter")
    def scatter(src_hbm, idx_hbm, out_hbm, idx_a, idx_b, buf0, buf1, in0, in1, out0a, out0b, out1a, out1b):
        base = (lax.axis_index("s") * nc + lax.axis_index("c")) * per_w
        pltpu.sync_copy(idx_hbm.at[pl.ds(base, per_w)], idx_a)
        pltpu.sync_copy(idx_hbm.at[pl.ds(n_src + base, per_w)], idx_b)

        def read(c, buf, sem):
            return pltpu.make_async_copy(src_hbm.at[pl.ds(base + c * ch, ch)], buf, sem)

        def writes(c, buf, sem_a, sem_b):
            rows = pl.ds(c * ch, ch)
            return (pltpu.make_async_copy(buf, out_hbm.at[idx_a.at[rows]], sem_a),
                    pltpu.make_async_copy(buf, out_hbm.at[idx_b.at[rows]], sem_b))

        def start(copies):
            for cp in copies:
                cp.start()

        def wait(copies):
            for cp in copies:
                cp.wait()

        read(0, buf0, in0).start()

        @pl.loop(0, nch, step=2)
        def _(c):
            @pl.when(c > 0)
            def _():
                wait(writes(c - 1, buf1, out1a, out1b))
            read(c + 1, buf1, in1).start()
            read(c, buf0, in0).wait()
            start(writes(c, buf0, out0a, out0b))
            read(c + 1, buf1, in1).wait()
            start(writes(c + 1, buf1, out1a, out1b))
            wait(writes(c, buf0, out0a, out0b))

            @pl.when(c + 2 < nch)
            def _():
                read(c + 2, buf0, in0).start()

        wait(writes(nch - 1, buf1, out1a, out1b))

    return scatter(src, idx)


def _sc_gather_rows(src, idx):
    M = idx.shape[0]
    width = src.shape[1]
    nc, nw = _sc_workers()
    per_w = M // nw
    ch = SC_CHUNK
    nch = per_w // ch
    assert per_w * nw == M and nch * ch == per_w and nch % 2 == 0
    mesh = plsc.VectorSubcoreMesh(core_axis_name="c", subcore_axis_name="s")
    dma = pltpu.SemaphoreType.DMA

    @functools.partial(
        pl.kernel, mesh=mesh, out_type=jax.ShapeDtypeStruct((M, width), src.dtype),
        scratch_types=[pltpu.VMEM((per_w,), jnp.int32),
                       pltpu.VMEM((ch, width), src.dtype), pltpu.VMEM((ch, width), src.dtype),
                       dma, dma, dma, dma],
        name="sc_gather")
    def gather(src_hbm, idx_hbm, out_hbm, idx_v, buf0, buf1, in0, in1, out0, out1):
        base = (lax.axis_index("s") * nc + lax.axis_index("c")) * per_w
        pltpu.sync_copy(idx_hbm.at[pl.ds(base, per_w)], idx_v)

        def read(c, buf, sem):
            return pltpu.make_async_copy(src_hbm.at[idx_v.at[pl.ds(c * ch, ch)]], buf, sem)

        def write(c, buf, sem):
            return pltpu.make_async_copy(buf, out_hbm.at[pl.ds(base + c * ch, ch)], sem)

        read(0, buf0, in0).start()

        @pl.loop(0, nch, step=2)
        def _(c):
            @pl.when(c > 0)
            def _():
                write(c - 1, buf1, out1).wait()
            read(c + 1, buf1, in1).start()
            read(c, buf0, in0).wait()
            write(c, buf0, out0).start()
            read(c + 1, buf1, in1).wait()
            write(c + 1, buf1, out1).start()
            write(c, buf0, out0).wait()

            @pl.when(c + 2 < nch)
            def _():
                read(c + 2, buf0, in0).start()

        write(nch - 1, buf1, out1).wait()

    return gather(src, idx)


def _final_kernel(x1_ref, y1_ref, y2_ref, meta_ref, nw_ref, o_ref):
    w1 = meta_ref[:, 4:5]
    w2 = meta_ref[:, 5:6]
    x2 = x1_ref[...] + (w1 * _unpack_rows(y1_ref[...]) + w2 * _unpack_rows(y2_ref[...]))
    o_ref[...] = _rms(x2, nw_ref[...])


def _final(x1, yg, meta, norm_w):
    T = x1.shape[0]
    tm = min(COMB_TILE, T)
    nt = T // tm
    row = lambda i: (i, 0)
    return pl.pallas_call(
        _final_kernel,
        grid=(nt,),
        in_specs=[pl.BlockSpec((tm, D_MODEL), row),
                  pl.BlockSpec((tm, PACKED), row),
                  pl.BlockSpec((tm, PACKED), lambda i: (i + nt, 0)),
                  pl.BlockSpec((tm, LANES), row),
                  pl.BlockSpec((1, D_MODEL), lambda i: (0, 0))],
        out_specs=pl.BlockSpec((tm, D_MODEL), row),
        out_shape=jax.ShapeDtypeStruct((T, D_MODEL), F32),
        compiler_params=pltpu.CompilerParams(dimension_semantics=("arbitrary",),
                                             vmem_limit_bytes=VMEM_LIMIT),
        name="final",
    )(x1, yg, yg, meta, norm_w)


def _layer(x, norm_mix_w, w_in, conv_w, conv_b, b_igate, b_fgate, attn_sinks, mlstm_norm_w,
           w_attn_o, w_mlstm_o, w_out, norm_ffn_w, w_group, b_group, w_router, b_router,
           w_gate, w_up, w_down, out_norm_w):
    B, S, D = x.shape
    T = B * S
    x2 = x.reshape(T, D)

    w_tok = jnp.concatenate([w_in[:, _O_AK:_O_AV], w_in[:, _O_MK:_O_MV]], axis=1).astype(BF16)
    w_chan = jnp.concatenate([w_in[:, _O_AQ:_O_AK] * (HEAD_DIM ** -0.5), w_in[:, _O_AV:_O_MQ],
                              w_in[:, _O_MQ:_O_MK], w_in[:, _O_MV:_O_MI],
                              jnp.pad(w_in[:, _O_MI:_O_GA], ((0, 0), (0, GATE_ROWS - 2 * M_HEADS)))],
                             axis=1).astype(BF16).T
    expert_ws = [w.reshape(-1, w.shape[-1]) for w in (w_gate, w_up, w_down)]
    ak, mk, aqt, avt, mqt, mvt, mot, grow, wg16, wu16, wd16 = _inproj(
        x2, norm_mix_w.reshape(1, D), w_tok, w_chan, expert_ws, B, S)

    ya = _attn(attn_sinks.astype(F32), aqt, ak.reshape(B, S, -1), avt)

    bias = jnp.concatenate([b_igate, b_fgate]).astype(F32)
    ym = _mlstm(mqt, mk.reshape(B, S, -1), mvt, mot, grow, conv_w, conv_b,
                bias.reshape(2 * M_HEADS, 1), mlstm_norm_w)

    gpad = jnp.zeros((8 - N_GROUPS, D), F32)
    w_rt = jnp.concatenate([w_group.T, gpad, w_router.T, jnp.zeros((LANES - ROUTER_ROWS, D), F32)], axis=0)
    w_rt_hi = w_rt.astype(BF16)
    w_rt = jnp.concatenate([w_rt_hi, (w_rt - w_rt_hi.astype(F32)).astype(BF16)], axis=0).T
    b_rt = jnp.concatenate([b_group, jnp.zeros((8 - N_GROUPS,), F32), b_router]).astype(F32)
    b_rt = jnp.broadcast_to(b_rt[:, None], (ROUTER_ROWS, min(TOK_TILE, T)))
    x1, h2p, meta, code_rows, cnt = _outproj(x2, ya.reshape(T, -1), ym.reshape(T, -1),
                                            norm_mix_w.reshape(1, D), w_in[:, _O_GA:].astype(BF16),
                                            w_attn_o.astype(BF16), w_mlstm_o.astype(BF16),
                                            w_out.astype(BF16), norm_ffn_w.reshape(1, D), w_rt, b_rt)

    counts = cnt[:, 0].astype(jnp.int32)
    nblk = (2 * T) // SLOT_BLOCK + N_EXPERTS
    pstart, blk_e, nvalid, nused = _slots(counts, nblk)
    dest = _dests(pstart, code_rows)[0:2].reshape(2 * T)

    xs = _sc_scatter_rows(h2p, dest, nblk * SLOT_BLOCK)
    ys = _experts(blk_e, nvalid, nused, xs, wg16.reshape(w_gate.shape), wu16.reshape(w_up.shape),
                  wd16.reshape(w_down.shape))
    yg = _sc_gather_rows(ys, dest)
    out = _final(x1, yg, meta, out_norm_w.reshape(1, D))
    return out.reshape(B, S, D)


def kernel(x, norm_mix_w, w_in, conv_w, conv_b, b_igate, b_fgate, attn_sinks, mlstm_norm_w, w_attn_o,
           w_mlstm_o, w_out, norm_ffn_w, w_group, b_group, w_router, b_router, w_gate, w_up, w_down,
           norm_final_w):
    depth = w_in.shape[0]
    assert depth == 1, "final RMSNorm is fused into the last layer's combine kernel"
    return _layer(x, norm_mix_w[0], w_in[0], conv_w[0], conv_b[0], b_igate[0], b_fgate[0],
                  attn_sinks[0], mlstm_norm_w[0], w_attn_o[0], w_mlstm_o[0], w_out[0], norm_ffn_w[0],
                  w_group[0], b_group[0], w_router[0], b_router[0], w_gate[0], w_up[0], w_down[0],
                  norm_final_w)
```

```python
import functools

import jax
import jax.numpy as jnp
from jax import lax
from jax.experimental import pallas as pl
from jax.experimental.pallas import tpu as pltpu
from jax.experimental.pallas import tpu_sc as plsc

F32 = jnp.float32
BF16 = jnp.bfloat16

D_MODEL = 1024
N_Q_HEADS = 8
N_KV_HEADS = 2
HEAD_DIM = 64
WINDOW = 128
GQA_GROUP = N_Q_HEADS // N_KV_HEADS
M_HEADS = 4
M_QK_DIM = 64
M_V_DIM = 128
CONV_WIDTH = 4
N_GROUPS = 4
EXPERTS_PER_GROUP = 8
N_EXPERTS = N_GROUPS * EXPERTS_PER_GROUP
D_EXPERT = 512
EPS = 1e-6

ATT_Q_WIDTH = N_Q_HEADS * HEAD_DIM
ATT_KV_WIDTH = N_KV_HEADS * HEAD_DIM
M_QK_WIDTH = M_HEADS * M_QK_DIM
M_V_WIDTH = M_HEADS * M_V_DIM

LANES = 128
NEG = -1e30
V7X_VMEM_BYTES = 64 * 1024 * 1024
VMEM_LIMIT = V7X_VMEM_BYTES - 8 * 1024 * 1024

IN_TILE = 1024
TOK_TILE = 1024
ATT_TILE = 2048
LOOKAHEAD = 2
M_LOOKAHEAD = 4
M_CHUNK = 128
M_BATCH = 8
M_STATE_ROWS = M_V_DIM + 16
SLOT_BLOCK = 512
COMB_TILE = 1024
RANK_RADIX = 65536
RANK_BITS = 16
ROUTER_ROWS = 8 + N_EXPERTS
PACKED = D_MODEL // 2
SC_CHUNK = 64

_O_AQ = 0
_O_AK = _O_AQ + ATT_Q_WIDTH
_O_AV = _O_AK + ATT_KV_WIDTH
_O_MQ = _O_AV + ATT_KV_WIDTH
_O_MK = _O_MQ + M_QK_WIDTH
_O_MV = _O_MK + M_QK_WIDTH
_O_MO = _O_MV + M_V_WIDTH
_O_MI = _O_MO + M_V_WIDTH
_O_MF = _O_MI + M_HEADS
_O_GA = _O_MF + M_HEADS


def _rms(x, w):
    return x * lax.rsqrt(jnp.mean(x * x, axis=-1, keepdims=True) + EPS) * w


def _sigmoid(x):
    return 0.5 * jnp.tanh(0.5 * x) + 0.5


def _pack_rows(x):
    half = x.shape[1] // 2
    bits = lax.bitcast_convert_type(x.astype(BF16).astype(F32), jnp.uint32)
    packed = (bits[:, :half] >> 16) | (bits[:, half:] & jnp.uint32(0xFFFF0000))
    return lax.bitcast_convert_type(packed, jnp.int32)


def _unpack_rows(w):
    u = lax.bitcast_convert_type(w, jnp.uint32)
    lo = lax.bitcast_convert_type(u << 16, F32)
    hi = lax.bitcast_convert_type(u & jnp.uint32(0xFFFF0000), F32)
    return jnp.concatenate([lo, hi], axis=-1)


def _log_sigmoid(x):
    return jnp.minimum(x, 0.0) - jnp.log1p(jnp.exp(-jnp.abs(x)))


_TOK_SPLITS = (ATT_KV_WIDTH, M_QK_WIDTH)
_CHAN_SPLITS = (ATT_Q_WIDTH, ATT_KV_WIDTH, M_QK_WIDTH, M_V_WIDTH, M_V_WIDTH)
GATE_ROWS = 16


def _inproj_kernel(x_ref, nw_ref, w_ref, wt_ref, *out_refs):
    tok_refs = out_refs[:len(_TOK_SPLITS)]
    chan_refs = out_refs[len(_TOK_SPLITS):len(_TOK_SPLITS) + len(_CHAN_SPLITS)]
    grow_ref = out_refs[-1]
    h = _rms(x_ref[...], nw_ref[...]).astype(BF16)
    lo = 0
    for ref, width in zip(tok_refs, _TOK_SPLITS):
        ref[...] = jnp.dot(h, w_ref[:, lo:lo + width], preferred_element_type=F32).astype(BF16)
        lo += width
    lo = 0
    for ref, width in zip(chan_refs, _CHAN_SPLITS):
        ref[0] = lax.dot_general(wt_ref[lo:lo + width, :], h, (((1,), (1,)), ((), ())),
                                 preferred_element_type=F32).astype(BF16)
        lo += width
    grow_ref[...] = lax.dot_general(wt_ref[lo:lo + GATE_ROWS, :], h, (((1,), (1,)), ((), ())),
                                    preferred_element_type=F32)[0:2 * M_HEADS, :]


def _inproj(x2, norm_w, w_tok, w_chan, B, S):
    T = x2.shape[0]
    tm = min(IN_TILE, S)
    tpb = S // tm
    nsteps = T // tm
    row = lambda i: (i, 0)
    const = lambda i: (0, 0)
    chan = lambda i: (i // tpb, 0, i % tpb)
    return pl.pallas_call(
        _inproj_kernel,
        grid=(nsteps,),
        in_specs=[pl.BlockSpec((tm, D_MODEL), row),
                  pl.BlockSpec((1, D_MODEL), const),
                  pl.BlockSpec(w_tok.shape, const),
                  pl.BlockSpec(w_chan.shape, const)],
        out_specs=[*[pl.BlockSpec((tm, w), row) for w in _TOK_SPLITS],
                   *[pl.BlockSpec((1, w, tm), chan) for w in _CHAN_SPLITS],
                   pl.BlockSpec((8, tm), lambda i: (0, i))],
        out_shape=[*[jax.ShapeDtypeStruct((T, w), BF16) for w in _TOK_SPLITS],
                   *[jax.ShapeDtypeStruct((B, w, S), BF16) for w in _CHAN_SPLITS],
                   jax.ShapeDtypeStruct((8, T), F32)],
        compiler_params=pltpu.CompilerParams(dimension_semantics=("arbitrary",),
                                             vmem_limit_bytes=VMEM_LIMIT),
        name="inproj",
    )(x2, norm_w, w_tok, w_chan)


def _attn_kernel(tq, sink_ref, qt_ref, k_ref, kp_ref, vt_ref, vtp_ref, o_ref):
    i = pl.program_id(1)
    W = WINDOW
    G = GQA_GROUP
    u = lax.broadcasted_iota(jnp.int32, (W, W), 0)
    t = lax.broadcasted_iota(jnp.int32, (W, W), 1)
    from_prev = u > t

    def keys_values(j):
        if j == 0:
            return kp_ref[0], vtp_ref[0], k_ref[0, 0:W, :], vt_ref[0, :, 0:W]
        return (k_ref[0, (j - 1) * W:j * W, :], vt_ref[0, :, (j - 1) * W:j * W],
                k_ref[0, j * W:(j + 1) * W, :], vt_ref[0, :, j * W:(j + 1) * W])

    def scores(j, g):
        k_prev, _, k_cur, _ = keys_values(j)
        dims = slice(g * HEAD_DIM, (g + 1) * HEAD_DIM)
        qg = jnp.concatenate([qt_ref[0, h * HEAD_DIM:(h + 1) * HEAD_DIM, j * W:(j + 1) * W]
                              for h in range(g * G, (g + 1) * G)], axis=1)
        return (jnp.dot(k_prev[:, dims], qg, preferred_element_type=F32),
                jnp.dot(k_cur[:, dims], qg, preferred_element_type=F32))

    tasks = [(j, g) for j in range(tq // W) for g in range(N_KV_HEADS)]
    queue = [scores(*t) for t in tasks[:LOOKAHEAD]]
    pair_rows = []
    for n_task, (j, g) in enumerate(tasks):
        s_prev, s_cur = queue.pop(0)
        if n_task + LOOKAHEAD < len(tasks):
            queue.append(scores(*tasks[n_task + LOOKAHEAD]))
        cols = slice(j * W, (j + 1) * W)
        _, vt_prev, _, vt_cur = keys_values(j)
        dims = slice(g * HEAD_DIM, (g + 1) * HEAD_DIM)
        for n in range(G):
            hc = slice(n * W, (n + 1) * W)
            sp = s_prev[:, hc]
            if j == 0:
                sp = jnp.where(i > 0, sp, NEG)
            s = jnp.where(from_prev, sp, s_cur[:, hc])
            sink = sink_ref[g * G + n]
            m = jnp.maximum(jnp.max(s, axis=0, keepdims=True), sink)
            p = jnp.exp(s - m)
            inv = 1.0 / (jnp.sum(p, axis=0, keepdims=True) + jnp.exp(sink - m))
            p_prev = jnp.where(from_prev, p, 0.0).astype(BF16)
            p_cur = jnp.where(from_prev, 0.0, p).astype(BF16)
            o = jnp.dot(vt_prev[dims, :], p_prev, preferred_element_type=F32) \
                + jnp.dot(vt_cur[dims, :], p_cur, preferred_element_type=F32)
            pair_rows.append(o * inv)
        if g == N_KV_HEADS - 1:
            out = [jnp.concatenate(pair_rows[n:n + 2], axis=0).T for n in range(0, N_Q_HEADS, 2)]
            o_ref[0, cols, :] = jnp.concatenate(out, axis=1).astype(BF16)
            pair_rows = []


def _attn(sinks, qt, k3, vt):
    B, S, _ = k3.shape
    tq = min(ATT_TILE, S)
    per = tq // WINDOW
    chan = lambda b, i: (b, 0, i)
    tok = lambda b, i: (b, i, 0)
    return pl.pallas_call(
        functools.partial(_attn_kernel, tq),
        grid=(B, S // tq),
        in_specs=[pl.BlockSpec(memory_space=pltpu.SMEM),
                  pl.BlockSpec((1, ATT_Q_WIDTH, tq), chan),
                  pl.BlockSpec((1, tq, ATT_KV_WIDTH), tok),
                  pl.BlockSpec((1, WINDOW, ATT_KV_WIDTH), lambda b, i: (b, jnp.maximum(i * per - 1, 0), 0)),
                  pl.BlockSpec((1, ATT_KV_WIDTH, tq), chan),
                  pl.BlockSpec((1, ATT_KV_WIDTH, WINDOW), lambda b, i: (b, 0, jnp.maximum(i * per - 1, 0)))],
        out_specs=pl.BlockSpec((1, tq, ATT_Q_WIDTH), tok),
        out_shape=jax.ShapeDtypeStruct((B, S, ATT_Q_WIDTH), BF16),
        compiler_params=pltpu.CompilerParams(dimension_semantics=("arbitrary", "arbitrary"),
                                             vmem_limit_bytes=VMEM_LIMIT),
        name="attn",
    )(sinks, qt, k3, k3, vt, vt)


def _mlstm_kernel(nb, *refs):
    L = M_CHUNK
    H = M_HEADS
    mqt_ref, mk_ref, mvt_ref, mot_ref = refs[:4]
    grow_refs = refs[4:4 + nb]
    (cwq_ref, cbq_ref, cwk_ref, cbk_ref, brow_ref, nw_ref, eg_ref, eu_ref, ed_ref,
     y_ref, eg16_ref, eu16_ref, ed16_ref, state_ref, m_ref, prevq_ref, ubuf_ref) = refs[4 + nb:]
    cidx = pl.program_id(1)
    for src, dst in ((eg_ref, eg16_ref), (eu_ref, eu16_ref), (ed_ref, ed16_ref)):
        dst[...] = src[...].astype(BF16)

    @pl.when(cidx == 0)
    def _():
        state_ref[...] = jnp.zeros_like(state_ref)
        m_ref[...] = jnp.zeros_like(m_ref)
        prevq_ref[...] = jnp.zeros_like(prevq_ref)
        ubuf_ref[:, 0:8, :] = jnp.zeros((nb, 8, M_QK_WIDTH), F32)

    ri = lax.broadcasted_iota(jnp.int32, (L, L), 0)
    ci = lax.broadcasted_iota(jnp.int32, (L, L), 1)
    causal_t = ri <= ci
    triu = jnp.where(causal_t, 1.0, 0.0).astype(F32)
    lane = lax.broadcasted_iota(jnp.int32, (8, L), 1)
    r2 = lax.broadcasted_iota(jnp.int32, (2 * L, L), 0)
    c2 = lax.broadcasted_iota(jnp.int32, (2 * L, L), 1)
    shifts = [jnp.where(r2 - c2 == L - k, 1.0, 0.0).astype(BF16) for k in range(1, CONV_WIDTH)]
    ones_rows = jnp.where(lax.broadcasted_iota(jnp.int32, (M_STATE_ROWS - M_V_DIM, L), 0) == 0,
                          1.0, 0.0).astype(BF16)

    pairs = [(bb, h) for bb in range(nb) for h in range(H)]
    states = [state_ref[bb * H + h] for bb, h in pairs]
    m_prevs = [m_ref[bb] for bb in range(nb)]

    def prepare(bb):
        cur = mqt_ref[bb]
        both = jnp.concatenate([prevq_ref[bb], cur], axis=1)
        acc = cbq_ref[...] + cwq_ref[CONV_WIDTH - 1] * cur.astype(F32)
        for k in range(1, CONV_WIDTH):
            acc = acc + cwq_ref[CONV_WIDTH - 1 - k] * jnp.dot(both, shifts[k - 1],
                                                              preferred_element_type=F32)
        qt = (acc * _sigmoid(acc)).astype(BF16)
        ubuf_ref[bb, 8:L + 8, :] = mk_ref[bb].astype(F32)
        acc = cbk_ref[...] + cwk_ref[CONV_WIDTH - 1:CONV_WIDTH, :] * ubuf_ref[bb, 8:L + 8, :]
        for j in range(CONV_WIDTH - 1):
            off = 8 - (CONV_WIDTH - 1) + j
            acc = acc + cwk_ref[j:j + 1, :] * ubuf_ref[bb, off:off + L, :]
        ubuf_ref[bb, 0:8, :] = ubuf_ref[bb, L:L + 8, :]
        kk = (acc * _sigmoid(acc) * (M_QK_DIM ** -0.5)).astype(BF16)

        gr = grow_refs[bb][...] + brow_ref[...]
        b = jnp.dot(_log_sigmoid(gr), triu, preferred_element_type=F32,
                    precision=lax.Precision.HIGHEST)
        gi = pltpu.roll(gr, 4, axis=0)
        u = gi - b
        cm = u
        for sh in (1, 2, 4, 8, 16, 32, 64):
            if sh < L:
                cm = jnp.maximum(cm, jnp.where(lane >= sh, pltpu.roll(cm, sh, axis=1), NEG))
        m_prev = m_prevs[bb]
        mt = b + jnp.maximum(m_prev, cm)
        wa = jnp.exp(b + m_prev - mt)
        emt = jnp.exp(-mt)
        m_new = jnp.broadcast_to(mt[:, L - 1:L], (8, L))
        b_last = jnp.broadcast_to(b[:, L - 1:L], (8, L))
        wc = jnp.exp(b_last + m_prev - m_new)
        ws = jnp.exp(b_last + u - m_new)
        return qt, kk, b - mt, u.T, wa, emt, wc, ws, m_new

    seqs = [prepare(bb) for bb in range(nb)]

    def products(bb, h, state):
        qt, kk = seqs[bb][0], seqs[bb][1]
        qh = qt[h * M_QK_DIM:(h + 1) * M_QK_DIM, :]
        kh = kk[:, h * M_QK_DIM:(h + 1) * M_QK_DIM]
        return (kh, jnp.dot(kh, qh, preferred_element_type=F32),
                jnp.dot(state.astype(BF16), qh, preferred_element_type=F32))

    new_states, outs = [], []
    depth = M_LOOKAHEAD
    queue = [products(*pairs[n], states[n]) for n in range(min(depth, len(pairs)))]
    for n, ((bb, h), state) in enumerate(zip(pairs, states)):
        kh, st, inter = queue.pop(0)
        if n + depth < len(pairs):
            queue.append(products(*pairs[n + depth], states[n + depth]))
        _, _, bmt, ucol, wa, emt, wc, ws, _ = seqs[bb]
        row = slice(H + h, H + h + 1)
        wq = jnp.exp(jnp.where(causal_t, bmt[row, :] + ucol[:, H + h:H + h + 1], NEG))
        sw = (st * wq).astype(BF16)
        vext = jnp.concatenate([mvt_ref[bb, h * M_V_DIM:(h + 1) * M_V_DIM, :], ones_rows], axis=0)
        res = wa[row, :] * inter + jnp.dot(vext, sw, preferred_element_type=F32)
        num = res[:M_V_DIM, :]
        den = res[M_V_DIM:M_V_DIM + 1, :]
        hb = num / jnp.maximum(jnp.abs(den), emt[row, :])
        vw = (vext.astype(F32) * ws[row, :]).astype(BF16)
        new_states.append(wc[row, 0:1] * state + jnp.dot(vw, kh, preferred_element_type=F32))
        hn = hb * lax.rsqrt(jnp.mean(hb * hb, axis=0, keepdims=True) + EPS)
        hn = hn * nw_ref[h * M_V_DIM:(h + 1) * M_V_DIM, :]
        og = _sigmoid(mot_ref[bb, h * M_V_DIM:(h + 1) * M_V_DIM, :].astype(F32))
        outs.append((og * hn).T.astype(BF16))

    for i, (bb, h) in enumerate(pairs):
        state_ref[bb * H + h] = new_states[i]
    for bb in range(nb):
        m_ref[bb] = seqs[bb][8]
        prevq_ref[bb] = mqt_ref[bb]
        y_ref[bb] = jnp.concatenate(outs[bb * H:(bb + 1) * H], axis=1)


def _mlstm(mqt, mk3, mvt, mot, grow, conv_w, conv_b, bias_row, norm_w, expert_ws):
    B, S, _ = mk3.shape
    L = M_CHUNK
    nb = max(d for d in range(1, M_BATCH + 1) if B % d == 0)
    nc = S // L
    assert L == LANES, "per-head scalars are kept lane-replicated next to (8, L) gate rows"
    tok = lambda b, c: (b, c, 0)
    chan = lambda b, c: (b, 0, c)
    const2 = lambda b, c: (0, 0)
    const3 = lambda b, c: (0, 0, 0)
    grow_specs = [pl.BlockSpec((8, L), functools.partial(lambda b, c, n: (0, (b * nb + n) * nc + c), n=n))
                  for n in range(nb)]
    rep = lambda v: jnp.broadcast_to(v.astype(F32)[..., None], v.shape + (L,))
    cwq, cbq = rep(conv_w[:, :M_QK_WIDTH]), rep(conv_b[:M_QK_WIDTH])
    cwk, cbk = conv_w[:, M_QK_WIDTH:].astype(F32), conv_b[M_QK_WIDTH:].reshape(1, -1).astype(F32)
    nsteps = (B // nb) * nc
    assert all(w.shape[0] % (8 * nsteps) == 0 for w in expert_ws)
    cast_specs = [pl.BlockSpec((w.shape[0] // nsteps, w.shape[1]), lambda b, c: (b * nc + c, 0))
                  for w in expert_ws]
    return pl.pallas_call(
        functools.partial(_mlstm_kernel, nb),
        grid=(B // nb, nc),
        in_specs=[pl.BlockSpec((nb, M_QK_WIDTH, L), chan),
                  pl.BlockSpec((nb, L, M_QK_WIDTH), tok),
                  pl.BlockSpec((nb, M_V_WIDTH, L), chan),
                  pl.BlockSpec((nb, M_V_WIDTH, L), chan),
                  *grow_specs,
                  pl.BlockSpec((CONV_WIDTH, M_QK_WIDTH, L), const3),
                  pl.BlockSpec((M_QK_WIDTH, L), const2),
                  pl.BlockSpec((CONV_WIDTH, M_QK_WIDTH), const2),
                  pl.BlockSpec((1, M_QK_WIDTH), const2),
                  pl.BlockSpec((8, 1), const2),
                  pl.BlockSpec((M_V_WIDTH, L), const2),
                  *cast_specs],
        out_specs=[pl.BlockSpec((nb, L, M_V_WIDTH), tok), *cast_specs],
        out_shape=[jax.ShapeDtypeStruct((B, S, M_V_WIDTH), BF16),
                   *[jax.ShapeDtypeStruct(w.shape, BF16) for w in expert_ws]],
        scratch_shapes=[pltpu.VMEM((nb * M_HEADS, M_STATE_ROWS, M_QK_DIM), F32),
                        pltpu.VMEM((nb, 8, LANES), F32),
                        pltpu.VMEM((nb, M_QK_WIDTH, L), BF16),
                        pltpu.VMEM((nb, L + 8, M_QK_WIDTH), F32)],
        compiler_params=pltpu.CompilerParams(dimension_semantics=("arbitrary", "arbitrary"),
                                             vmem_limit_bytes=VMEM_LIMIT),
        name="mlstm",
    )(mqt, mk3, mvt, mot, *([grow] * nb), cwq, cbq, cwk, cbk, bias_row, rep(norm_w), *expert_ws)


def _outproj_kernel(x_ref, ya_ref, ym_ref, nmix_ref, wgab_ref, wa_ref, wm_ref, wo_ref, nw_ref, wr_ref, br_ref,
                    x1_ref, h2_ref, meta_ref, code_ref, cnt_ref, cnt_scr):
    tm = x_ref.shape[0]
    i = pl.program_id(0)

    @pl.when(i == 0)
    def _():
        cnt_scr[...] = jnp.zeros_like(cnt_scr)

    pa = jnp.dot(ya_ref[...], wa_ref[...], preferred_element_type=F32)
    pm = jnp.dot(ym_ref[...], wm_ref[...], preferred_element_type=F32)
    x = x_ref[...]
    h1 = _rms(x, nmix_ref[...]).astype(BF16)
    ga = jnp.dot(h1, wgab_ref[:, :D_MODEL], preferred_element_type=F32)
    gb = jnp.dot(h1, wgab_ref[:, D_MODEL:], preferred_element_type=F32)
    mix = _sigmoid(ga) * pa + _sigmoid(gb) * pm
    x1 = x + jnp.dot(mix.astype(BF16), wo_ref[...], preferred_element_type=F32)
    x1_ref[...] = x1
    h2 = _rms(x1, nw_ref[...])
    h2_ref[...] = _pack_rows(h2)

    nt = (((1,), (1,)), ((), ()))
    h_hi = h2.astype(BF16)
    h_lo = (h2 - h_hi.astype(F32)).astype(BF16)
    hw = jnp.dot(h_hi, wr_ref[...], preferred_element_type=F32)
    logits = (hw[:, :LANES] + hw[:, LANES:]
              + jnp.dot(h_lo, wr_ref[:, :LANES], preferred_element_type=F32)).T[:ROUTER_ROWS]
    logits = logits + br_ref[...]
    row8 = lax.broadcasted_iota(jnp.int32, (8, tm), 0)

    def first_argmax(v):
        mx = jnp.max(v, axis=0, keepdims=True)
        idx = jnp.min(jnp.where(v == mx, row8, 8), axis=0, keepdims=True)
        return mx, idx

    gl = jnp.where(row8 < N_GROUPS, logits[0:8], NEG)
    gmax, gi = first_argmax(gl)
    gp = 1.0 / jnp.sum(jnp.exp(gl - gmax), axis=0, keepdims=True)
    el = jnp.zeros((8, tm), F32)
    for g in range(N_GROUPS):
        el = jnp.where(gi == g, logits[8 + 8 * g:16 + 8 * g], el)
    v1, j1 = first_argmax(el)
    v2, j2 = first_argmax(jnp.where(row8 == j1, NEG, el))
    t = jnp.exp(v2 - v1)
    w1 = gp / (1.0 + t)
    w2 = gp * t / (1.0 + t)
    e1 = gi * EXPERTS_PER_GROUP + j1
    e2 = gi * EXPERTS_PER_GROUP + j2

    erow = lax.broadcasted_iota(jnp.int32, (N_EXPERTS, tm), 0)
    hit1 = erow == e1
    hit2 = erow == e2
    onehot = jnp.where(hit1 | hit2, 1.0, 0.0)
    ri = lax.broadcasted_iota(jnp.int32, (tm, tm), 0)
    ci = lax.broadcasted_iota(jnp.int32, (tm, tm), 1)
    before = jnp.where(ri < ci, 1.0, 0.0).astype(BF16)
    rank = jnp.dot(onehot.astype(BF16), before, preferred_element_type=F32) + cnt_scr[...]
    r1 = jnp.sum(jnp.where(hit1, rank, 0.0), axis=0, keepdims=True)
    r2 = jnp.sum(jnp.where(hit2, rank, 0.0), axis=0, keepdims=True)
    cnt = jnp.broadcast_to((rank + onehot)[:, tm - 1:tm], (N_EXPERTS, tm))
    cnt_scr[...] = cnt
    cnt_ref[...] = cnt[:, :LANES]

    c1 = e1 * RANK_RADIX + r1.astype(jnp.int32)
    c2 = e2 * RANK_RADIX + r2.astype(jnp.int32)
    code_ref[...] = jnp.where(row8 == 0, c1, jnp.where(row8 == 1, c2, 0))
    wrow = lax.broadcasted_iota(jnp.int32, (LANES, tm), 0)
    meta_ref[...] = jnp.where(wrow == 4, w1, jnp.where(wrow == 5, w2, 0.0)).T


def _outproj(x2, ya, ym, norm_mix, w_gab, wa, wm, wo, norm_w, w_rt, b_rt):
    T = x2.shape[0]
    tm = b_rt.shape[1]
    row = lambda i: (i, 0)
    const = lambda i: (0, 0)
    resident = lambda shape: pl.BlockSpec(shape, const, pipeline_mode=pl.Buffered(1))
    return pl.pallas_call(
        _outproj_kernel,
        grid=(T // tm,),
        in_specs=[pl.BlockSpec((tm, D_MODEL), row),
                  pl.BlockSpec((tm, ATT_Q_WIDTH), row),
                  pl.BlockSpec((tm, M_V_WIDTH), row),
                  resident((1, D_MODEL)),
                  resident((D_MODEL, 2 * D_MODEL)),
                  resident((ATT_Q_WIDTH, D_MODEL)),
                  resident((M_V_WIDTH, D_MODEL)),
                  resident((D_MODEL, D_MODEL)),
                  resident((1, D_MODEL)),
                  resident((D_MODEL, 2 * LANES)),
                  resident((ROUTER_ROWS, tm))],
        out_specs=[pl.BlockSpec((tm, D_MODEL), row),
                   pl.BlockSpec((tm, PACKED), row),
                   pl.BlockSpec((tm, LANES), row),
                   pl.BlockSpec((8, tm), lambda i: (0, i)),
                   pl.BlockSpec((N_EXPERTS, LANES), const)],
        out_shape=[jax.ShapeDtypeStruct((T, D_MODEL), F32),
                   jax.ShapeDtypeStruct((T, PACKED), jnp.int32),
                   jax.ShapeDtypeStruct((T, LANES), F32),
                   jax.ShapeDtypeStruct((8, T), jnp.int32),
                   jax.ShapeDtypeStruct((N_EXPERTS, LANES), F32)],
        scratch_shapes=[pltpu.VMEM((N_EXPERTS, tm), F32)],
        compiler_params=pltpu.CompilerParams(dimension_semantics=("arbitrary",),
                                             vmem_limit_bytes=VMEM_LIMIT),
        name="outproj",
    )(x2, ya, ym, norm_mix, w_gab, wa, wm, wo, norm_w, w_rt, b_rt)


def _slots_kernel(cnt_ref, pstart_ref, blk_e_ref, nvalid_ref, nused_ref):
    R = SLOT_BLOCK
    nblk = blk_e_ref.shape[0]

    def per_expert(e, run):
        c = cnt_ref[e]
        end = run + ((c + R - 1) // R) * R
        pstart_ref[e] = run

        def set_blk(b, carry):
            blk_e_ref[b] = e
            nvalid_ref[b] = jnp.minimum(run + c - b * R, R)
            return carry
        lax.fori_loop(run // R, end // R, set_blk, 0)
        return end

    total = lax.fori_loop(0, N_EXPERTS, per_expert, 0)
    nused_ref[0] = total // R

    def tail_blk(b, carry):
        blk_e_ref[b] = N_EXPERTS - 1
        nvalid_ref[b] = 0
        return carry
    lax.fori_loop(total // R, nblk, tail_blk, 0)


def _slots(counts, nblk):
    smem = pl.BlockSpec(memory_space=pltpu.SMEM)
    return pl.pallas_call(
        _slots_kernel,
        in_specs=[smem],
        out_specs=[smem, smem, smem, smem],
        out_shape=[jax.ShapeDtypeStruct((N_EXPERTS,), jnp.int32),
                   jax.ShapeDtypeStruct((nblk,), jnp.int32),
                   jax.ShapeDtypeStruct((nblk,), jnp.int32),
                   jax.ShapeDtypeStruct((1,), jnp.int32)],
        name="slots",
    )(counts)


def _experts_kernel(blk_e_ref, nvalid_ref, nused_ref, xs_ref, wg_ref, wu_ref, wd_ref, ys_ref):
    j = pl.program_id(0)
    R = SLOT_BLOCK
    valid = j < nused_ref[0]

    @pl.when(valid)
    def _():
        rows = lax.broadcasted_iota(jnp.int32, (R, PACKED), 0)
        xw = jnp.where(rows < nvalid_ref[j], xs_ref[...], 0)
        xb = _unpack_rows(xw).astype(BF16)
        g = jnp.dot(xb, wg_ref[0], preferred_element_type=F32)
        u = jnp.dot(xb, wu_ref[0], preferred_element_type=F32)
        act = (g * _sigmoid(g) * u).astype(BF16)
        ys_ref[...] = _pack_rows(jnp.dot(act, wd_ref[0], preferred_element_type=F32))


def _experts(blk_e, nvalid, nused, xs, w_gate, w_up, w_down):
    nblk = blk_e.shape[0]
    R = SLOT_BLOCK
    wspec = lambda shape: pl.BlockSpec((1,) + shape, lambda j, be, nv, nu: (be[j], 0, 0))
    grid_spec = pltpu.PrefetchScalarGridSpec(
        num_scalar_prefetch=3,
        grid=(nblk,),
        in_specs=[pl.BlockSpec((R, PACKED), lambda j, be, nv, nu: (jnp.minimum(j, nu[0] - 1), 0)),
                  wspec((D_MODEL, D_EXPERT)),
                  wspec((D_MODEL, D_EXPERT)),
                  wspec((D_EXPERT, D_MODEL))],
        out_specs=pl.BlockSpec((R, PACKED), lambda j, be, nv, nu: (jnp.minimum(j, nu[0] - 1), 0)),
    )
    return pl.pallas_call(
        _experts_kernel,
        grid_spec=grid_spec,
        out_shape=jax.ShapeDtypeStruct((nblk * R, PACKED), jnp.int32),
        compiler_params=pltpu.CompilerParams(dimension_semantics=("arbitrary",),
                                             vmem_limit_bytes=VMEM_LIMIT),
        name="experts",
    )(blk_e, nvalid, nused, xs, w_gate, w_up, w_down)


def _dest_kernel(pstart_ref, code_ref, dest_ref):
    c = code_ref[...]
    e = c >> RANK_BITS
    base = jnp.zeros_like(c)
    for k in range(N_EXPERTS):
        base = jnp.where(e == k, pstart_ref[k], base)
    dest_ref[...] = base + (c & (RANK_RADIX - 1))


def _dests(pstart, code_rows):
    T = code_rows.shape[1]
    tc = min(4096, T)
    return pl.pallas_call(
        _dest_kernel,
        grid=(T // tc,),
        in_specs=[pl.BlockSpec(memory_space=pltpu.SMEM),
                  pl.BlockSpec((8, tc), lambda i: (0, i))],
        out_specs=pl.BlockSpec((8, tc), lambda i: (0, i)),
        out_shape=jax.ShapeDtypeStruct((8, T), jnp.int32),
        name="dests",
    )(pstart, code_rows)


def _sc_workers():
    info = plsc.get_sparse_core_info()
    return info.num_cores, info.num_cores * info.num_subcores


def _sc_scatter_rows(src, idx, n_out):
    n_src, width = src.shape
    nc, nw = _sc_workers()
    per_w = n_src // nw
    ch = SC_CHUNK
    nch = per_w // ch
    assert idx.shape[0] == 2 * n_src and per_w * nw == n_src and nch * ch == per_w and nch % 2 == 0
    mesh = plsc.VectorSubcoreMesh(core_axis_name="c", subcore_axis_name="s")
    dma = pltpu.SemaphoreType.DMA

    @functools.partial(
        pl.kernel, mesh=mesh, out_type=jax.ShapeDtypeStruct((n_out, width), src.dtype),
        scratch_types=[pltpu.VMEM((per_w,), jnp.int32), pltpu.VMEM((per_w,), jnp.int32),
                       pltpu.VMEM((ch, width), src.dtype), pltpu.VMEM((ch, width), src.dtype),
                       dma, dma, dma, dma, dma, dma],
        name="sc_scatter")
    def scatter(src_hbm, idx_hbm, out_hbm, idx_a, idx_b, buf0, buf1, in0, in1, out0a, out0b, out1a, out1b):
        base = (lax.axis_index("s") * nc + lax.axis_index("c")) * per_w
        pltpu.sync_copy(idx_hbm.at[pl.ds(base, per_w)], idx_a)
        pltpu.sync_copy(idx_hbm.at[pl.ds(n_src + base, per_w)], idx_b)

        def read(c, buf, sem):
            return pltpu.make_async_copy(src_hbm.at[pl.ds(base + c * ch, ch)], buf, sem)

        def writes(c, buf, sem_a, sem_b):
            rows = pl.ds(c * ch, ch)
            return (pltpu.make_async_copy(buf, out_hbm.at[idx_a.at[rows]], sem_a),
                    pltpu.make_async_copy(buf, out_hbm.at[idx_b.at[rows]], sem_b))

        def start(copies):
            for cp in copies:
                cp.start()

        def wait(copies):
            for cp in copies:
                cp.wait()

        read(0, buf0, in0).start()

        @pl.loop(0, nch, step=2)
        def _(c):
            @pl.when(c > 0)
            def _():
                wait(writes(c - 1, buf1, out1a, out1b))
            read(c + 1, buf1, in1).start()
            read(c, buf0, in0).wait()
            start(writes(c, buf0, out0a, out0b))
            read(c + 1, buf1, in1).wait()
            start(writes(c + 1, buf1, out1a, out1b))
            wait(writes(c, buf0, out0a, out0b))

            @pl.when(c + 2 < nch)
            def _():
                read(c + 2, buf0, in0).start()

        wait(writes(nch - 1, buf1, out1a, out1b))

    return scatter(src, idx)


def _sc_gather_rows(src, idx):
    M = idx.shape[0]
    width = src.shape[1]
    nc, nw = _sc_workers()
    per_w = M // nw
    ch = SC_CHUNK
    nch = per_w // ch
    assert per_w * nw == M and nch * ch == per_w and nch % 2 == 0
    mesh = plsc.VectorSubcoreMesh(core_axis_name="c", subcore_axis_name="s")
    dma = pltpu.SemaphoreType.DMA

    @functools.partial(
        pl.kernel, mesh=mesh, out_type=jax.ShapeDtypeStruct((M, width), src.dtype),
        scratch_types=[pltpu.VMEM((per_w,), jnp.int32),
                       pltpu.VMEM((ch, width), src.dtype), pltpu.VMEM((ch, width), src.dtype),
                       dma, dma, dma, dma],
        name="sc_gather")
    def gather(src_hbm, idx_hbm, out_hbm, idx_v, buf0, buf1, in0, in1, out0, out1):
        base = (lax.axis_index("s") * nc + lax.axis_index("c")) * per_w
        pltpu.sync_copy(idx_hbm.at[pl.ds(base, per_w)], idx_v)

        def read(c, buf, sem):
            return pltpu.make_async_copy(src_hbm.at[idx_v.at[pl.ds(c * ch, ch)]], buf, sem)

        def write(c, buf, sem):
            return pltpu.make_async_copy(buf, out_hbm.at[pl.ds(base + c * ch, ch)], sem)

        read(0, buf0, in0).start()

        @pl.loop(0, nch, step=2)
        def _(c):
            @pl.when(c > 0)
            def _():
                write(c - 1, buf1, out1).wait()
            read(c + 1, buf1, in1).start()
            read(c, buf0, in0).wait()
            write(c, buf0, out0).start()
            read(c + 1, buf1, in1).wait()
            write(c + 1, buf1, out1).start()
            write(c, buf0, out0).wait()

            @pl.when(c + 2 < nch)
            def _():
                read(c + 2, buf0, in0).start()

        write(nch - 1, buf1, out1).wait()

    return gather(src, idx)


def _final_kernel(x1_ref, y1_ref, y2_ref, meta_ref, nw_ref, o_ref):
    w1 = meta_ref[:, 4:5]
    w2 = meta_ref[:, 5:6]
    x2 = x1_ref[...] + (w1 * _unpack_rows(y1_ref[...]) + w2 * _unpack_rows(y2_ref[...]))
    o_ref[...] = _rms(x2, nw_ref[...])


def _final(x1, yg, meta, norm_w):
    T = x1.shape[0]
    tm = min(COMB_TILE, T)
    nt = T // tm
    row = lambda i: (i, 0)
    return pl.pallas_call(
        _final_kernel,
        grid=(nt,),
        in_specs=[pl.BlockSpec((tm, D_MODEL), row),
                  pl.BlockSpec((tm, PACKED), row),
                  pl.BlockSpec((tm, PACKED), lambda i: (i + nt, 0)),
                  pl.BlockSpec((tm, LANES), row),
                  pl.BlockSpec((1, D_MODEL), lambda i: (0, 0))],
        out_specs=pl.BlockSpec((tm, D_MODEL), row),
        out_shape=jax.ShapeDtypeStruct((T, D_MODEL), F32),
        compiler_params=pltpu.CompilerParams(dimension_semantics=("arbitrary",),
                                             vmem_limit_bytes=VMEM_LIMIT),
        name="final",
    )(x1, yg, yg, meta, norm_w)


def _layer(x, norm_mix_w, w_in, conv_w, conv_b, b_igate, b_fgate, attn_sinks, mlstm_norm_w,
           w_attn_o, w_mlstm_o, w_out, norm_ffn_w, w_group, b_group, w_router, b_router,
           w_gate, w_up, w_down, out_norm_w):
    B, S, D = x.shape
    T = B * S
    x2 = x.reshape(T, D)

    w_tok = jnp.concatenate([w_in[:, _O_AK:_O_AV], w_in[:, _O_MK:_O_MV]], axis=1).astype(BF16)
    w_chan = jnp.concatenate([w_in[:, _O_AQ:_O_AK] * (HEAD_DIM ** -0.5), w_in[:, _O_AV:_O_MQ],
                              w_in[:, _O_MQ:_O_MK], w_in[:, _O_MV:_O_MI],
                              jnp.pad(w_in[:, _O_MI:_O_GA], ((0, 0), (0, GATE_ROWS - 2 * M_HEADS)))],
                             axis=1).astype(BF16).T
    ak, mk, aqt, avt, mqt, mvt, mot, grow = _inproj(x2, norm_mix_w.reshape(1, D), w_tok, w_chan, B, S)

    ya = _attn(attn_sinks.astype(F32), aqt, ak.reshape(B, S, -1), avt)

    bias = jnp.concatenate([b_igate, b_fgate]).astype(F32)
    expert_ws = [w.reshape(-1, w.shape[-1]) for w in (w_gate, w_up, w_down)]
    ym, wg16, wu16, wd16 = _mlstm(mqt, mk.reshape(B, S, -1), mvt, mot, grow, conv_w, conv_b,
                                  bias.reshape(2 * M_HEADS, 1), mlstm_norm_w, expert_ws)

    gpad = jnp.zeros((8 - N_GROUPS, D), F32)
    w_rt = jnp.concatenate([w_group.T, gpad, w_router.T, jnp.zeros((LANES - ROUTER_ROWS, D), F32)], axis=0)
    w_rt_hi = w_rt.astype(BF16)
    w_rt = jnp.concatenate([w_rt_hi, (w_rt - w_rt_hi.astype(F32)).astype(BF16)], axis=0).T
    b_rt = jnp.concatenate([b_group, jnp.zeros((8 - N_GROUPS,), F32), b_router]).astype(F32)
    b_rt = jnp.broadcast_to(b_rt[:, None], (ROUTER_ROWS, min(TOK_TILE, T)))
    x1, h2p, meta, code_rows, cnt = _outproj(x2, ya.reshape(T, -1), ym.reshape(T, -1),
                                            norm_mix_w.reshape(1, D), w_in[:, _O_GA:].astype(BF16),
                                            w_attn_o.astype(BF16), w_mlstm_o.astype(BF16),
                                            w_out.astype(BF16), norm_ffn_w.reshape(1, D), w_rt, b_rt)

    counts = cnt[:, 0].astype(jnp.int32)
    nblk = (2 * T) // SLOT_BLOCK + N_EXPERTS
    pstart, blk_e, nvalid, nused = _slots(counts, nblk)
    dest = _dests(pstart, code_rows)[0:2].reshape(2 * T)

    xs = _sc_scatter_rows(h2p, dest, nblk * SLOT_BLOCK)
    ys = _experts(blk_e, nvalid, nused, xs, wg16.reshape(w_gate.shape), wu16.reshape(w_up.shape),
                  wd16.reshape(w_down.shape))
    yg = _sc_gather_rows(ys, dest)
    out = _final(x1, yg, meta, out_norm_w.reshape(1, D))
    return out.reshape(B, S, D)


def kernel(x, norm_mix_w, w_in, conv_w, conv_b, b_igate, b_fgate, attn_sinks, mlstm_norm_w, w_attn_o,
           w_mlstm_o, w_out, norm_ffn_w, w_group, b_group, w_router, b_router, w_gate, w_up, w_down,
           norm_final_w):
    depth = w_in.shape[0]
    assert depth == 1, "final RMSNorm is fused into the last layer's combine kernel"
    return _layer(x, norm_mix_w[0], w_in[0], conv_w[0], conv_b[0], b_igate[0], b_fgate[0],
                  attn_sinks[0], mlstm_norm_w[0], w_attn_o[0], w_mlstm_o[0], w_out[0], norm_ffn_w[0],
                  w_group[0], b_group[0], w_router[0], b_router[0], w_gate[0], w_up[0], w_down[0],
                  norm_final_w)
```

```python
import functools

import jax
import jax.numpy as jnp
from jax import lax
from jax.experimental import pallas as pl
from jax.experimental.pallas import tpu as pltpu
from jax.experimental.pallas import tpu_sc as plsc

F32 = jnp.float32
BF16 = jnp.bfloat16

D_MODEL = 1024
N_Q_HEADS = 8
N_KV_HEADS = 2
HEAD_DIM = 64
WINDOW = 128
GQA_GROUP = N_Q_HEADS // N_KV_HEADS
M_HEADS = 4
M_QK_DIM = 64
M_V_DIM = 128
CONV_WIDTH = 4
N_GROUPS = 4
EXPERTS_PER_GROUP = 8
N_EXPERTS = N_GROUPS * EXPERTS_PER_GROUP
D_EXPERT = 512
EPS = 1e-6

ATT_Q_WIDTH = N_Q_HEADS * HEAD_DIM
ATT_KV_WIDTH = N_KV_HEADS * HEAD_DIM
M_QK_WIDTH = M_HEADS * M_QK_DIM
M_V_WIDTH = M_HEADS * M_V_DIM

LANES = 128
NEG = -1e30
V7X_VMEM_BYTES = 64 * 1024 * 1024
VMEM_LIMIT = V7X_VMEM_BYTES - 8 * 1024 * 1024

IN_TILE = 1024
TOK_TILE = 1024
ATT_TILE = 2048
LOOKAHEAD = 2
M_LOOKAHEAD = 4
M_CHUNK = 128
M_BATCH = 8
M_STATE_ROWS = M_V_DIM + 16
SLOT_BLOCK = 512
COMB_TILE = 1024
RANK_RADIX = 65536
RANK_BITS = 16
ROUTER_ROWS = 8 + N_EXPERTS
PACKED = D_MODEL // 2
SC_CHUNK = 64

_O_AQ = 0
_O_AK = _O_AQ + ATT_Q_WIDTH
_O_AV = _O_AK + ATT_KV_WIDTH
_O_MQ = _O_AV + ATT_KV_WIDTH
_O_MK = _O_MQ + M_QK_WIDTH
_O_MV = _O_MK + M_QK_WIDTH
_O_MO = _O_MV + M_V_WIDTH
_O_MI = _O_MO + M_V_WIDTH
_O_MF = _O_MI + M_HEADS
_O_GA = _O_MF + M_HEADS


def _rms(x, w):
    return x * lax.rsqrt(jnp.mean(x * x, axis=-1, keepdims=True) + EPS) * w


def _sigmoid(x):
    return 0.5 * jnp.tanh(0.5 * x) + 0.5


def _pack_rows(x):
    half = x.shape[1] // 2
    bits = lax.bitcast_convert_type(x.astype(BF16).astype(F32), jnp.uint32)
    packed = (bits[:, :half] >> 16) | (bits[:, half:] & jnp.uint32(0xFFFF0000))
    return lax.bitcast_convert_type(packed, jnp.int32)


def _unpack_rows(w):
    u = lax.bitcast_convert_type(w, jnp.uint32)
    lo = lax.bitcast_convert_type(u << 16, F32)
    hi = lax.bitcast_convert_type(u & jnp.uint32(0xFFFF0000), F32)
    return jnp.concatenate([lo, hi], axis=-1)


def _log_sigmoid(x):
    return jnp.minimum(x, 0.0) - jnp.log1p(jnp.exp(-jnp.abs(x)))


_TOK_SPLITS = (ATT_KV_WIDTH, M_QK_WIDTH)
_CHAN_SPLITS = (ATT_Q_WIDTH, ATT_KV_WIDTH, M_QK_WIDTH, M_V_WIDTH, M_V_WIDTH)
GATE_ROWS = 16


def _inproj_kernel(x_ref, nw_ref, w_ref, wt_ref, *out_refs):
    tok_refs = out_refs[:len(_TOK_SPLITS)]
    chan_refs = out_refs[len(_TOK_SPLITS):len(_TOK_SPLITS) + len(_CHAN_SPLITS)]
    grow_ref = out_refs[-1]
    h = _rms(x_ref[...], nw_ref[...]).astype(BF16)
    lo = 0
    for ref, width in zip(tok_refs, _TOK_SPLITS):
        ref[...] = jnp.dot(h, w_ref[:, lo:lo + width], preferred_element_type=F32).astype(BF16)
        lo += width
    lo = 0
    for ref, width in zip(chan_refs, _CHAN_SPLITS):
        ref[0] = lax.dot_general(wt_ref[lo:lo + width, :], h, (((1,), (1,)), ((), ())),
                                 preferred_element_type=F32).astype(BF16)
        lo += width
    grow_ref[...] = lax.dot_general(wt_ref[lo:lo + GATE_ROWS, :], h, (((1,), (1,)), ((), ())),
                                    preferred_element_type=F32)[0:2 * M_HEADS, :]


def _inproj(x2, norm_w, w_tok, w_chan, B, S):
    T = x2.shape[0]
    tm = min(IN_TILE, S)
    tpb = S // tm
    nsteps = T // tm
    row = lambda i: (i, 0)
    const = lambda i: (0, 0)
    chan = lambda i: (i // tpb, 0, i % tpb)
    return pl.pallas_call(
        _inproj_kernel,
        grid=(nsteps,),
        in_specs=[pl.BlockSpec((tm, D_MODEL), row),
                  pl.BlockSpec((1, D_MODEL), const),
                  pl.BlockSpec(w_tok.shape, const),
                  pl.BlockSpec(w_chan.shape, const)],
        out_specs=[*[pl.BlockSpec((tm, w), row) for w in _TOK_SPLITS],
                   *[pl.BlockSpec((1, w, tm), chan) for w in _CHAN_SPLITS],
                   pl.BlockSpec((8, tm), lambda i: (0, i))],
        out_shape=[*[jax.ShapeDtypeStruct((T, w), BF16) for w in _TOK_SPLITS],
                   *[jax.ShapeDtypeStruct((B, w, S), BF16) for w in _CHAN_SPLITS],
                   jax.ShapeDtypeStruct((8, T), F32)],
        compiler_params=pltpu.CompilerParams(dimension_semantics=("arbitrary",),
                                             vmem_limit_bytes=VMEM_LIMIT),
        name="inproj",
    )(x2, norm_w, w_tok, w_chan)


def _attn_kernel(tq, sink_ref, qt_ref, k_ref, kp_ref, vt_ref, vtp_ref, o_ref):
    i = pl.program_id(1)
    W = WINDOW
    G = GQA_GROUP
    u = lax.broadcasted_iota(jnp.int32, (W, W), 0)
    t = lax.broadcasted_iota(jnp.int32, (W, W), 1)
    from_prev = u > t

    def keys_values(j):
        if j == 0:
            return kp_ref[0], vtp_ref[0], k_ref[0, 0:W, :], vt_ref[0, :, 0:W]
        return (k_ref[0, (j - 1) * W:j * W, :], vt_ref[0, :, (j - 1) * W:j * W],
                k_ref[0, j * W:(j + 1) * W, :], vt_ref[0, :, j * W:(j + 1) * W])

    def scores(j, g):
        k_prev, _, k_cur, _ = keys_values(j)
        dims = slice(g * HEAD_DIM, (g + 1) * HEAD_DIM)
        qg = jnp.concatenate([qt_ref[0, h * HEAD_DIM:(h + 1) * HEAD_DIM, j * W:(j + 1) * W]
                              for h in range(g * G, (g + 1) * G)], axis=1)
        return (jnp.dot(k_prev[:, dims], qg, preferred_element_type=F32),
                jnp.dot(k_cur[:, dims], qg, preferred_element_type=F32))

    tasks = [(j, g) for j in range(tq // W) for g in range(N_KV_HEADS)]
    queue = [scores(*t) for t in tasks[:LOOKAHEAD]]
    pair_rows = []
    for n_task, (j, g) in enumerate(tasks):
        s_prev, s_cur = queue.pop(0)
        if n_task + LOOKAHEAD < len(tasks):
            queue.append(scores(*tasks[n_task + LOOKAHEAD]))
        cols = slice(j * W, (j + 1) * W)
        _, vt_prev, _, vt_cur = keys_values(j)
        dims = slice(g * HEAD_DIM, (g + 1) * HEAD_DIM)
        for n in range(G):
            hc = slice(n * W, (n + 1) * W)
            sp = s_prev[:, hc]
            if j == 0:
                sp = jnp.where(i > 0, sp, NEG)
            s = jnp.where(from_prev, sp, s_cur[:, hc])
            sink = sink_ref[g * G + n]
            m = jnp.maximum(jnp.max(s, axis=0, keepdims=True), sink)
            p = jnp.exp(s - m)
            inv = 1.0 / (jnp.sum(p, axis=0, keepdims=True) + jnp.exp(sink - m))
            p_prev = jnp.where(from_prev, p, 0.0).astype(BF16)
            p_cur = jnp.where(from_prev, 0.0, p).astype(BF16)
            o = jnp.dot(vt_prev[dims, :], p_prev, preferred_element_type=F32) \
                + jnp.dot(vt_cur[dims, :], p_cur, preferred_element_type=F32)
            pair_rows.append(o * inv)
        if g == N_KV_HEADS - 1:
            out = [jnp.concatenate(pair_rows[n:n + 2], axis=0).T for n in range(0, N_Q_HEADS, 2)]
            o_ref[0, cols, :] = jnp.concatenate(out, axis=1).astype(BF16)
            pair_rows = []


def _attn(sinks, qt, k3, vt):
    B, S, _ = k3.shape
    tq = min(ATT_TILE, S)
    per = tq // WINDOW
    chan = lambda b, i: (b, 0, i)
    tok = lambda b, i: (b, i, 0)
    return pl.pallas_call(
        functools.partial(_attn_kernel, tq),
        grid=(B, S // tq),
        in_specs=[pl.BlockSpec(memory_space=pltpu.SMEM),
                  pl.BlockSpec((1, ATT_Q_WIDTH, tq), chan),
                  pl.BlockSpec((1, tq, ATT_KV_WIDTH), tok),
                  pl.BlockSpec((1, WINDOW, ATT_KV_WIDTH), lambda b, i: (b, jnp.maximum(i * per - 1, 0), 0)),
                  pl.BlockSpec((1, ATT_KV_WIDTH, tq), chan),
                  pl.BlockSpec((1, ATT_KV_WIDTH, WINDOW), lambda b, i: (b, 0, jnp.maximum(i * per - 1, 0)))],
        out_specs=pl.BlockSpec((1, tq, ATT_Q_WIDTH), tok),
        out_shape=jax.ShapeDtypeStruct((B, S, ATT_Q_WIDTH), BF16),
        compiler_params=pltpu.CompilerParams(dimension_semantics=("arbitrary", "arbitrary"),
                                             vmem_limit_bytes=VMEM_LIMIT),
        name="attn",
    )(sinks, qt, k3, k3, vt, vt)


def _mlstm_kernel(nb, *refs):
    L = M_CHUNK
    H = M_HEADS
    mqt_ref, mk_ref, mvt_ref, mot_ref = refs[:4]
    grow_refs = refs[4:4 + nb]
    (cwq_ref, cbq_ref, cwk_ref, cbk_ref, brow_ref, nw_ref, eg_ref, eu_ref, ed_ref,
     y_ref, eg16_ref, eu16_ref, ed16_ref, state_ref, m_ref, prevq_ref, ubuf_ref) = refs[4 + nb:]
    cidx = pl.program_id(1)
    for src, dst in ((eg_ref, eg16_ref), (eu_ref, eu16_ref), (ed_ref, ed16_ref)):
        dst[...] = src[...].astype(BF16)

    @pl.when(cidx == 0)
    def _():
        state_ref[...] = jnp.zeros_like(state_ref)
        m_ref[...] = jnp.zeros_like(m_ref)
        prevq_ref[...] = jnp.zeros_like(prevq_ref)
        ubuf_ref[:, 0:8, :] = jnp.zeros((nb, 8, M_QK_WIDTH), F32)

    ri = lax.broadcasted_iota(jnp.int32, (L, L), 0)
    ci = lax.broadcasted_iota(jnp.int32, (L, L), 1)
    causal_t = ri <= ci
    triu = jnp.where(causal_t, 1.0, 0.0).astype(F32)
    lane = lax.broadcasted_iota(jnp.int32, (8, L), 1)
    r2 = lax.broadcasted_iota(jnp.int32, (2 * L, L), 0)
    c2 = lax.broadcasted_iota(jnp.int32, (2 * L, L), 1)
    shifts = [jnp.where(r2 - c2 == L - k, 1.0, 0.0).astype(BF16) for k in range(1, CONV_WIDTH)]
    ones_rows = jnp.where(lax.broadcasted_iota(jnp.int32, (M_STATE_ROWS - M_V_DIM, L), 0) == 0,
                          1.0, 0.0).astype(BF16)

    pairs = [(bb, h) for bb in range(nb) for h in range(H)]
    states = [state_ref[bb * H + h] for bb, h in pairs]
    m_prevs = [m_ref[bb] for bb in range(nb)]

    def prepare(bb):
        cur = mqt_ref[bb]
        both = jnp.concatenate([prevq_ref[bb], cur], axis=1)
        acc = cbq_ref[...] + cwq_ref[CONV_WIDTH - 1] * cur.astype(F32)
        for k in range(1, CONV_WIDTH):
            acc = acc + cwq_ref[CONV_WIDTH - 1 - k] * jnp.dot(both, shifts[k - 1],
                                                              preferred_element_type=F32)
        qt = (acc * _sigmoid(acc)).astype(BF16)
        ubuf_ref[bb, 8:L + 8, :] = mk_ref[bb].astype(F32)
        acc = cbk_ref[...] + cwk_ref[CONV_WIDTH - 1:CONV_WIDTH, :] * ubuf_ref[bb, 8:L + 8, :]
        for j in range(CONV_WIDTH - 1):
            off = 8 - (CONV_WIDTH - 1) + j
            acc = acc + cwk_ref[j:j + 1, :] * ubuf_ref[bb, off:off + L, :]
        ubuf_ref[bb, 0:8, :] = ubuf_ref[bb, L:L + 8, :]
        kk = (acc * _sigmoid(acc) * (M_QK_DIM ** -0.5)).astype(BF16)

        gr = grow_refs[bb][...] + brow_ref[...]
        b = jnp.dot(_log_sigmoid(gr), triu, preferred_element_type=F32,
                    precision=lax.Precision.HIGHEST)
        gi = pltpu.roll(gr, 4, axis=0)
        u = gi - b
        cm = u
        for sh in (1, 2, 4, 8, 16, 32, 64):
            if sh < L:
                cm = jnp.maximum(cm, jnp.where(lane >= sh, pltpu.roll(cm, sh, axis=1), NEG))
        m_prev = m_prevs[bb]
        mt = b + jnp.maximum(m_prev, cm)
        wa = jnp.exp(b + m_prev - mt)
        emt = jnp.exp(-mt)
        m_new = jnp.broadcast_to(mt[:, L - 1:L], (8, L))
        b_last = jnp.broadcast_to(b[:, L - 1:L], (8, L))
        wc = jnp.exp(b_last + m_prev - m_new)
        ws = jnp.exp(b_last + u - m_new)
        return qt, kk, b - mt, u.T, wa, emt, wc, ws, m_new

    seqs = [prepare(bb) for bb in range(nb)]

    def products(bb, h, state):
        qt, kk = seqs[bb][0], seqs[bb][1]
        qh = qt[h * M_QK_DIM:(h + 1) * M_QK_DIM, :]
        kh = kk[:, h * M_QK_DIM:(h + 1) * M_QK_DIM]
        return (kh, jnp.dot(kh, qh, preferred_element_type=F32),
                jnp.dot(state.astype(BF16), qh, preferred_element_type=F32))

    new_states, outs = [], []
    depth = M_LOOKAHEAD
    queue = [products(*pairs[n], states[n]) for n in range(min(depth, len(pairs)))]
    for n, ((bb, h), state) in enumerate(zip(pairs, states)):
        kh, st, inter = queue.pop(0)
        if n + depth < len(pairs):
            queue.append(products(*pairs[n + depth], states[n + depth]))
        _, _, bmt, ucol, wa, emt, wc, ws, _ = seqs[bb]
        row = slice(H + h, H + h + 1)
        wq = jnp.exp(jnp.where(causal_t, bmt[row, :] + ucol[:, H + h:H + h + 1], NEG))
        sw = (st * wq).astype(BF16)
        vext = jnp.concatenate([mvt_ref[bb, h * M_V_DIM:(h + 1) * M_V_DIM, :], ones_rows], axis=0)
        res = wa[row, :] * inter + jnp.dot(vext, sw, preferred_element_type=F32)
        num = res[:M_V_DIM, :]
        den = res[M_V_DIM:M_V_DIM + 1, :]
        hb = num / jnp.maximum(jnp.abs(den), emt[row, :])
        vw = (vext.astype(F32) * ws[row, :]).astype(BF16)
        new_states.append(wc[row, 0:1] * state + jnp.dot(vw, kh, preferred_element_type=F32))
        hn = hb * lax.rsqrt(jnp.mean(hb * hb, axis=0, keepdims=True) + EPS)
        hn = hn * nw_ref[h * M_V_DIM:(h + 1) * M_V_DIM, :]
        og = _sigmoid(mot_ref[bb, h * M_V_DIM:(h + 1) * M_V_DIM, :].astype(F32))
        outs.append((og * hn).T.astype(BF16))

    for i, (bb, h) in enumerate(pairs):
        state_ref[bb * H + h] = new_states[i]
    for bb in range(nb):
        m_ref[bb] = seqs[bb][8]
        prevq_ref[bb] = mqt_ref[bb]
        y_ref[bb] = jnp.concatenate(outs[bb * H:(bb + 1) * H], axis=1)


def _mlstm(mqt, mk3, mvt, mot, grow, conv_w, conv_b, bias_row, norm_w, expert_ws):
    B, S, _ = mk3.shape
    L = M_CHUNK
    nb = max(d for d in range(1, M_BATCH + 1) if B % d == 0)
    nc = S // L
    assert L == LANES, "per-head scalars are kept lane-replicated next to (8, L) gate rows"
    tok = lambda b, c: (b, c, 0)
    chan = lambda b, c: (b, 0, c)
    const2 = lambda b, c: (0, 0)
    const3 = lambda b, c: (0, 0, 0)
    grow_specs = [pl.BlockSpec((8, L), functools.partial(lambda b, c, n: (0, (b * nb + n) * nc + c), n=n))
                  for n in range(nb)]
    rep = lambda v: jnp.broadcast_to(v.astype(F32)[..., None], v.shape + (L,))
    cwq, cbq = rep(conv_w[:, :M_QK_WIDTH]), rep(conv_b[:M_QK_WIDTH])
    cwk, cbk = conv_w[:, M_QK_WIDTH:].astype(F32), conv_b[M_QK_WIDTH:].reshape(1, -1).astype(F32)
    nsteps = (B // nb) * nc
    assert all(w.shape[0] % (8 * nsteps) == 0 for w in expert_ws)
    cast_specs = [pl.BlockSpec((w.shape[0] // nsteps, w.shape[1]), lambda b, c: (b * nc + c, 0))
                  for w in expert_ws]
    return pl.pallas_call(
        functools.partial(_mlstm_kernel, nb),
        grid=(B // nb, nc),
        in_specs=[pl.BlockSpec((nb, M_QK_WIDTH, L), chan),
                  pl.BlockSpec((nb, L, M_QK_WIDTH), tok),
                  pl.BlockSpec((nb, M_V_WIDTH, L), chan),
                  pl.BlockSpec((nb, M_V_WIDTH, L), chan),
                  *grow_specs,
                  pl.BlockSpec((CONV_WIDTH, M_QK_WIDTH, L), const3),
                  pl.BlockSpec((M_QK_WIDTH, L), const2),
                  pl.BlockSpec((CONV_WIDTH, M_QK_WIDTH), const2),
                  pl.BlockSpec((1, M_QK_WIDTH), const2),
                  pl.BlockSpec((8, 1), const2),
                  pl.BlockSpec((M_V_WIDTH, L), const2),
                  *cast_specs],
        out_specs=[pl.BlockSpec((nb, L, M_V_WIDTH), tok), *cast_specs],
        out_shape=[jax.ShapeDtypeStruct((B, S, M_V_WIDTH), BF16),
                   *[jax.ShapeDtypeStruct(w.shape, BF16) for w in expert_ws]],
        scratch_shapes=[pltpu.VMEM((nb * M_HEADS, M_STATE_ROWS, M_QK_DIM), F32),
                        pltpu.VMEM((nb, 8, LANES), F32),
                        pltpu.VMEM((nb, M_QK_WIDTH, L), BF16),
                        pltpu.VMEM((nb, L + 8, M_QK_WIDTH), F32)],
        compiler_params=pltpu.CompilerParams(dimension_semantics=("arbitrary", "arbitrary"),
                                             vmem_limit_bytes=VMEM_LIMIT),
        name="mlstm",
    )(mqt, mk3, mvt, mot, *([grow] * nb), cwq, cbq, cwk, cbk, bias_row, rep(norm_w), *expert_ws)


def _outproj_kernel(x_ref, ya_ref, ym_ref, nmix_ref, wgab_ref, wa_ref, wm_ref, wo_ref, nw_ref, wr_ref, br_ref,
                    x1_ref, h2_ref, meta_ref, code_ref, cnt_ref, cnt_scr):
    tm = x_ref.shape[0]
    i = pl.program_id(0)

    @pl.when(i == 0)
    def _():
        cnt_scr[...] = jnp.zeros_like(cnt_scr)

    pa = jnp.dot(ya_ref[...], wa_ref[...], preferred_element_type=F32)
    pm = jnp.dot(ym_ref[...], wm_ref[...], preferred_element_type=F32)
    x = x_ref[...]
    h1 = _rms(x, nmix_ref[...]).astype(BF16)
    ga = jnp.dot(h1, wgab_ref[:, :D_MODEL], preferred_element_type=F32)
    gb = jnp.dot(h1, wgab_ref[:, D_MODEL:], preferred_element_type=F32)
    mix = _sigmoid(ga) * pa + _sigmoid(gb) * pm
    x1 = x + jnp.dot(mix.astype(BF16), wo_ref[...], preferred_element_type=F32)
    x1_ref[...] = _pack_rows(x1)
    h2 = _rms(x1, nw_ref[...])
    h2_ref[...] = _pack_rows(h2)

    nt = (((1,), (1,)), ((), ()))
    h_hi = h2.astype(BF16)
    h_lo = (h2 - h_hi.astype(F32)).astype(BF16)
    hw = jnp.dot(h_hi, wr_ref[...], preferred_element_type=F32)
    logits = (hw[:, :LANES] + hw[:, LANES:]
              + jnp.dot(h_lo, wr_ref[:, :LANES], preferred_element_type=F32)).T[:ROUTER_ROWS]
    logits = logits + br_ref[...]
    row8 = lax.broadcasted_iota(jnp.int32, (8, tm), 0)

    def first_argmax(v):
        mx = jnp.max(v, axis=0, keepdims=True)
        idx = jnp.min(jnp.where(v == mx, row8, 8), axis=0, keepdims=True)
        return mx, idx

    gl = jnp.where(row8 < N_GROUPS, logits[0:8], NEG)
    gmax, gi = first_argmax(gl)
    gp = 1.0 / jnp.sum(jnp.exp(gl - gmax), axis=0, keepdims=True)
    el = jnp.zeros((8, tm), F32)
    for g in range(N_GROUPS):
        el = jnp.where(gi == g, logits[8 + 8 * g:16 + 8 * g], el)
    v1, j1 = first_argmax(el)
    v2, j2 = first_argmax(jnp.where(row8 == j1, NEG, el))
    t = jnp.exp(v2 - v1)
    w1 = gp / (1.0 + t)
    w2 = gp * t / (1.0 + t)
    e1 = gi * EXPERTS_PER_GROUP + j1
    e2 = gi * EXPERTS_PER_GROUP + j2

    erow = lax.broadcasted_iota(jnp.int32, (N_EXPERTS, tm), 0)
    hit1 = erow == e1
    hit2 = erow == e2
    onehot = jnp.where(hit1 | hit2, 1.0, 0.0)
    ri = lax.broadcasted_iota(jnp.int32, (tm, tm), 0)
    ci = lax.broadcasted_iota(jnp.int32, (tm, tm), 1)
    before = jnp.where(ri < ci, 1.0, 0.0).astype(BF16)
    rank = jnp.dot(onehot.astype(BF16), before, preferred_element_type=F32) + cnt_scr[...]
    r1 = jnp.sum(jnp.where(hit1, rank, 0.0), axis=0, keepdims=True)
    r2 = jnp.sum(jnp.where(hit2, rank, 0.0), axis=0, keepdims=True)
    cnt = jnp.broadcast_to((rank + onehot)[:, tm - 1:tm], (N_EXPERTS, tm))
    cnt_scr[...] = cnt
    cnt_ref[...] = cnt[:, :LANES]

    c1 = e1 * RANK_RADIX + r1.astype(jnp.int32)
    c2 = e2 * RANK_RADIX + r2.astype(jnp.int32)
    code_ref[...] = jnp.where(row8 == 0, c1, jnp.where(row8 == 1, c2, 0))
    wrow = lax.broadcasted_iota(jnp.int32, (LANES, tm), 0)
    meta_ref[...] = jnp.where(wrow == 4, w1, jnp.where(wrow == 5, w2, 0.0)).T


def _outproj(x2, ya, ym, norm_mix, w_gab, wa, wm, wo, norm_w, w_rt, b_rt):
    T = x2.shape[0]
    tm = b_rt.shape[1]
    row = lambda i: (i, 0)
    const = lambda i: (0, 0)
    resident = lambda shape: pl.BlockSpec(shape, const, pipeline_mode=pl.Buffered(1))
    return pl.pallas_call(
        _outproj_kernel,
        grid=(T // tm,),
        in_specs=[pl.BlockSpec((tm, D_MODEL), row),
                  pl.BlockSpec((tm, ATT_Q_WIDTH), row),
                  pl.BlockSpec((tm, M_V_WIDTH), row),
                  resident((1, D_MODEL)),
                  resident((D_MODEL, 2 * D_MODEL)),
                  resident((ATT_Q_WIDTH, D_MODEL)),
                  resident((M_V_WIDTH, D_MODEL)),
                  resident((D_MODEL, D_MODEL)),
                  resident((1, D_MODEL)),
                  resident((D_MODEL, 2 * LANES)),
                  resident((ROUTER_ROWS, tm))],
        out_specs=[pl.BlockSpec((tm, PACKED), row),
                   pl.BlockSpec((tm, PACKED), row),
                   pl.BlockSpec((tm, LANES), row),
                   pl.BlockSpec((8, tm), lambda i: (0, i)),
                   pl.BlockSpec((N_EXPERTS, LANES), const)],
        out_shape=[jax.ShapeDtypeStruct((T, PACKED), jnp.int32),
                   jax.ShapeDtypeStruct((T, PACKED), jnp.int32),
                   jax.ShapeDtypeStruct((T, LANES), F32),
                   jax.ShapeDtypeStruct((8, T), jnp.int32),
                   jax.ShapeDtypeStruct((N_EXPERTS, LANES), F32)],
        scratch_shapes=[pltpu.VMEM((N_EXPERTS, tm), F32)],
        compiler_params=pltpu.CompilerParams(dimension_semantics=("arbitrary",),
                                             vmem_limit_bytes=VMEM_LIMIT),
        name="outproj",
    )(x2, ya, ym, norm_mix, w_gab, wa, wm, wo, norm_w, w_rt, b_rt)


def _slots_kernel(cnt_ref, pstart_ref, blk_e_ref, nvalid_ref, nused_ref):
    R = SLOT_BLOCK
    nblk = blk_e_ref.shape[0]

    def per_expert(e, run):
        c = cnt_ref[e]
        end = run + ((c + R - 1) // R) * R
        pstart_ref[e] = run

        def set_blk(b, carry):
            blk_e_ref[b] = e
            nvalid_ref[b] = jnp.minimum(run + c - b * R, R)
            return carry
        lax.fori_loop(run // R, end // R, set_blk, 0)
        return end

    total = lax.fori_loop(0, N_EXPERTS, per_expert, 0)
    nused_ref[0] = total // R

    def tail_blk(b, carry):
        blk_e_ref[b] = N_EXPERTS - 1
        nvalid_ref[b] = 0
        return carry
    lax.fori_loop(total // R, nblk, tail_blk, 0)


def _slots(counts, nblk):
    smem = pl.BlockSpec(memory_space=pltpu.SMEM)
    return pl.pallas_call(
        _slots_kernel,
        in_specs=[smem],
        out_specs=[smem, smem, smem, smem],
        out_shape=[jax.ShapeDtypeStruct((N_EXPERTS,), jnp.int32),
                   jax.ShapeDtypeStruct((nblk,), jnp.int32),
                   jax.ShapeDtypeStruct((nblk,), jnp.int32),
                   jax.ShapeDtypeStruct((1,), jnp.int32)],
        name="slots",
    )(counts)


def _experts_kernel(blk_e_ref, nvalid_ref, nused_ref, xs_ref, wg_ref, wu_ref, wd_ref, ys_ref):
    j = pl.program_id(0)
    R = SLOT_BLOCK
    valid = j < nused_ref[0]

    @pl.when(valid)
    def _():
        rows = lax.broadcasted_iota(jnp.int32, (R, PACKED), 0)
        xw = jnp.where(rows < nvalid_ref[j], xs_ref[...], 0)
        xb = _unpack_rows(xw).astype(BF16)
        g = jnp.dot(xb, wg_ref[0], preferred_element_type=F32)
        u = jnp.dot(xb, wu_ref[0], preferred_element_type=F32)
        act = (g * _sigmoid(g) * u).astype(BF16)
        ys_ref[...] = _pack_rows(jnp.dot(act, wd_ref[0], preferred_element_type=F32))


def _experts(blk_e, nvalid, nused, xs, w_gate, w_up, w_down):
    nblk = blk_e.shape[0]
    R = SLOT_BLOCK
    wspec = lambda shape: pl.BlockSpec((1,) + shape, lambda j, be, nv, nu: (be[j], 0, 0))
    grid_spec = pltpu.PrefetchScalarGridSpec(
        num_scalar_prefetch=3,
        grid=(nblk,),
        in_specs=[pl.BlockSpec((R, PACKED), lambda j, be, nv, nu: (jnp.minimum(j, nu[0] - 1), 0)),
                  wspec((D_MODEL, D_EXPERT)),
                  wspec((D_MODEL, D_EXPERT)),
                  wspec((D_EXPERT, D_MODEL))],
        out_specs=pl.BlockSpec((R, PACKED), lambda j, be, nv, nu: (jnp.minimum(j, nu[0] - 1), 0)),
    )
    return pl.pallas_call(
        _experts_kernel,
        grid_spec=grid_spec,
        out_shape=jax.ShapeDtypeStruct((nblk * R, PACKED), jnp.int32),
        compiler_params=pltpu.CompilerParams(dimension_semantics=("arbitrary",),
                                             vmem_limit_bytes=VMEM_LIMIT),
        name="experts",
    )(blk_e, nvalid, nused, xs, w_gate, w_up, w_down)


def _dest_kernel(pstart_ref, code_ref, dest_ref):
    c = code_ref[...]
    e = c >> RANK_BITS
    base = jnp.zeros_like(c)
    for k in range(N_EXPERTS):
        base = jnp.where(e == k, pstart_ref[k], base)
    dest_ref[...] = base + (c & (RANK_RADIX - 1))


def _dests(pstart, code_rows):
    T = code_rows.shape[1]
    tc = min(4096, T)
    return pl.pallas_call(
        _dest_kernel,
        grid=(T // tc,),
        in_specs=[pl.BlockSpec(memory_space=pltpu.SMEM),
                  pl.BlockSpec((8, tc), lambda i: (0, i))],
        out_specs=pl.BlockSpec((8, tc), lambda i: (0, i)),
        out_shape=jax.ShapeDtypeStruct((8, T), jnp.int32),
        name="dests",
    )(pstart, code_rows)


def _sc_workers():
    info = plsc.get_sparse_core_info()
    return info.num_cores, info.num_cores * info.num_subcores


def _sc_scatter_rows(src, idx, n_out):
    n_src, width = src.shape
    nc, nw = _sc_workers()
    per_w = n_src // nw
    ch = SC_CHUNK
    nch = per_w // ch
    assert idx.shape[0] == 2 * n_src and per_w * nw == n_src and nch * ch == per_w and nch % 2 == 0
    mesh = plsc.VectorSubcoreMesh(core_axis_name="c", subcore_axis_name="s")
    dma = pltpu.SemaphoreType.DMA

    @functools.partial(
        pl.kernel, mesh=mesh, out_type=jax.ShapeDtypeStruct((n_out, width), src.dtype),
        scratch_types=[pltpu.VMEM((per_w,), jnp.int32), pltpu.VMEM((per_w,), jnp.int32),
                       pltpu.VMEM((ch, width), src.dtype), pltpu.VMEM((ch, width), src.dtype),
                       dma, dma, dma, dma, dma, dma],
        name="sc_scatter")
    def scatter(src_hbm, idx_hbm, out_hbm, idx_a, idx_b, buf0, buf1, in0, in1, out0a, out0b, out1a, out1b):
        base = (lax.axis_index("s") * nc + lax.axis_index("c")) * per_w
        pltpu.sync_copy(idx_hbm.at[pl.ds(base, per_w)], idx_a)
        pltpu.sync_copy(idx_hbm.at[pl.ds(n_src + base, per_w)], idx_b)

        def read(c, buf, sem):
            return pltpu.make_async_copy(src_hbm.at[pl.ds(base + c * ch, ch)], buf, sem)

        def writes(c, buf, sem_a, sem_b):
            rows = pl.ds(c * ch, ch)
            return (pltpu.make_async_copy(buf, out_hbm.at[idx_a.at[rows]], sem_a),
                    pltpu.make_async_copy(buf, out_hbm.at[idx_b.at[rows]], sem_b))

        def start(copies):
            for cp in copies:
                cp.start()

        def wait(copies):
            for cp in copies:
                cp.wait()

        read(0, buf0, in0).start()

        @pl.loop(0, nch, step=2)
        def _(c):
            @pl.when(c > 0)
            def _():
                wait(writes(c - 1, buf1, out1a, out1b))
            read(c + 1, buf1, in1).start()
            read(c, buf0, in0).wait()
            start(writes(c, buf0, out0a, out0b))
            read(c + 1, buf1, in1).wait()
            start(writes(c + 1, buf1, out1a, out1b))
            wait(writes(c, buf0, out0a, out0b))

            @pl.when(c + 2 < nch)
            def _():
                read(c + 2, buf0, in0).start()

        wait(writes(nch - 1, buf1, out1a, out1b))

    return scatter(src, idx)


def _sc_gather_rows(src, idx):
    M = idx.shape[0]
    width = src.shape[1]
    nc, nw = _sc_workers()
    per_w = M // nw
    ch = SC_CHUNK
    nch = per_w // ch
    assert per_w * nw == M and nch * ch == per_w and nch % 2 == 0
    mesh = plsc.VectorSubcoreMesh(core_axis_name="c", subcore_axis_name="s")
    dma = pltpu.SemaphoreType.DMA

    @functools.partial(
        pl.kernel, mesh=mesh, out_type=jax.ShapeDtypeStruct((M, width), src.dtype),
        scratch_types=[pltpu.VMEM((per_w,), jnp.int32),
                       pltpu.VMEM((ch, width), src.dtype), pltpu.VMEM((ch, width), src.dtype),
                       dma, dma, dma, dma],
        name="sc_gather")
    def gather(src_hbm, idx_hbm, out_hbm, idx_v, buf0, buf1, in0, in1, out0, out1):
        base = (lax.axis_index("s") * nc + lax.axis_index("c")) * per_w
        pltpu.sync_copy(idx_hbm.at[pl.ds(base, per_w)], idx_v)

        def read(c, buf, sem):
            return pltpu.make_async_copy(src_hbm.at[idx_v.at[pl.ds(c * ch, ch)]], buf, sem)

        def write(c, buf, sem):
            return pltpu.make_async_copy(buf, out_hbm.at[pl.ds(base + c * ch, ch)], sem)

        read(0, buf0, in0).start()

        @pl.loop(0, nch, step=2)
        def _(c):
            @pl.when(c > 0)
            def _():
                write(c - 1, buf1, out1).wait()
            read(c + 1, buf1, in1).start()
            read(c, buf0, in0).wait()
            write(c, buf0, out0).start()
            read(c + 1, buf1, in1).wait()
            write(c + 1, buf1, out1).start()
            write(c, buf0, out0).wait()

            @pl.when(c + 2 < nch)
            def _():
                read(c + 2, buf0, in0).start()

        write(nch - 1, buf1, out1).wait()

    return gather(src, idx)


def _final_kernel(x1_ref, y1_ref, y2_ref, meta_ref, nw_ref, o_ref):
    w1 = meta_ref[:, 4:5]
    w2 = meta_ref[:, 5:6]
    x2 = _unpack_rows(x1_ref[...]) + (w1 * _unpack_rows(y1_ref[...]) + w2 * _unpack_rows(y2_ref[...]))
    o_ref[...] = _rms(x2, nw_ref[...])


def _final(x1, yg, meta, norm_w):
    T = x1.shape[0]
    tm = min(COMB_TILE, T)
    nt = T // tm
    row = lambda i: (i, 0)
    return pl.pallas_call(
        _final_kernel,
        grid=(nt,),
        in_specs=[pl.BlockSpec((tm, PACKED), row),
                  pl.BlockSpec((tm, PACKED), row),
                  pl.BlockSpec((tm, PACKED), lambda i: (i + nt, 0)),
                  pl.BlockSpec((tm, LANES), row),
                  pl.BlockSpec((1, D_MODEL), lambda i: (0, 0))],
        out_specs=pl.BlockSpec((tm, D_MODEL), row),
        out_shape=jax.ShapeDtypeStruct((T, D_MODEL), F32),
        compiler_params=pltpu.CompilerParams(dimension_semantics=("arbitrary",),
                                             vmem_limit_bytes=VMEM_LIMIT),
        name="final",
    )(x1, yg, yg, meta, norm_w)


def _layer(x, norm_mix_w, w_in, conv_w, conv_b, b_igate, b_fgate, attn_sinks, mlstm_norm_w,
           w_attn_o, w_mlstm_o, w_out, norm_ffn_w, w_group, b_group, w_router, b_router,
           w_gate, w_up, w_down, out_norm_w):
    B, S, D = x.shape
    T = B * S
    x2 = x.reshape(T, D)

    w_tok = jnp.concatenate([w_in[:, _O_AK:_O_AV], w_in[:, _O_MK:_O_MV]], axis=1).astype(BF16)
    w_chan = jnp.concatenate([w_in[:, _O_AQ:_O_AK] * (HEAD_DIM ** -0.5), w_in[:, _O_AV:_O_MQ],
                              w_in[:, _O_MQ:_O_MK], w_in[:, _O_MV:_O_MI],
                              jnp.pad(w_in[:, _O_MI:_O_GA], ((0, 0), (0, GATE_ROWS - 2 * M_HEADS)))],
                             axis=1).astype(BF16).T
    ak, mk, aqt, avt, mqt, mvt, mot, grow = _inproj(x2, norm_mix_w.reshape(1, D), w_tok, w_chan, B, S)

    ya = _attn(attn_sinks.astype(F32), aqt, ak.reshape(B, S, -1), avt)

    bias = jnp.concatenate([b_igate, b_fgate]).astype(F32)
    expert_ws = [w.reshape(-1, w.shape[-1]) for w in (w_gate, w_up, w_down)]
    ym, wg16, wu16, wd16 = _mlstm(mqt, mk.reshape(B, S, -1), mvt, mot, grow, conv_w, conv_b,
                                  bias.reshape(2 * M_HEADS, 1), mlstm_norm_w, expert_ws)

    gpad = jnp.zeros((8 - N_GROUPS, D), F32)
    w_rt = jnp.concatenate([w_group.T, gpad, w_router.T, jnp.zeros((LANES - ROUTER_ROWS, D), F32)], axis=0)
    w_rt_hi = w_rt.astype(BF16)
    w_rt = jnp.concatenate([w_rt_hi, (w_rt - w_rt_hi.astype(F32)).astype(BF16)], axis=0).T
    b_rt = jnp.concatenate([b_group, jnp.zeros((8 - N_GROUPS,), F32), b_router]).astype(F32)
    b_rt = jnp.broadcast_to(b_rt[:, None], (ROUTER_ROWS, min(TOK_TILE, T)))
    x1, h2p, meta, code_rows, cnt = _outproj(x2, ya.reshape(T, -1), ym.reshape(T, -1),
                                            norm_mix_w.reshape(1, D), w_in[:, _O_GA:].astype(BF16),
                                            w_attn_o.astype(BF16), w_mlstm_o.astype(BF16),
                                            w_out.astype(BF16), norm_ffn_w.reshape(1, D), w_rt, b_rt)

    counts = cnt[:, 0].astype(jnp.int32)
    nblk = (2 * T) // SLOT_BLOCK + N_EXPERTS
    pstart, blk_e, nvalid, nused = _slots(counts, nblk)
    dest = _dests(pstart, code_rows)[0:2].reshape(2 * T)

    xs = _sc_scatter_rows(h2p, dest, nblk * SLOT_BLOCK)
    ys = _experts(blk_e, nvalid, nused, xs, wg16.reshape(w_gate.shape), wu16.reshape(w_up.shape),
                  wd16.reshape(w_down.shape))
    yg = _sc_gather_rows(ys, dest)
    out = _final(x1, yg, meta, out_norm_w.reshape(1, D))
    return out.reshape(B, S, D)


def kernel(x, norm_mix_w, w_in, conv_w, conv_b, b_igate, b_fgate, attn_sinks, mlstm_norm_w, w_attn_o,
           w_mlstm_o, w_out, norm_ffn_w, w_group, b_group, w_router, b_router, w_gate, w_up, w_down,
           norm_final_w):
    depth = w_in.shape[0]
    assert depth == 1, "final RMSNorm is fused into the last layer's combine kernel"
    return _layer(x, norm_mix_w[0], w_in[0], conv_w[0], conv_b[0], b_igate[0], b_fgate[0],
                  attn_sinks[0], mlstm_norm_w[0], w_attn_o[0], w_mlstm_o[0], w_out[0], norm_ffn_w[0],
                  w_group[0], b_group[0], w_router[0], b_router[0], w_gate[0], w_up[0], w_down[0],
                  norm_final_w)
```

```python
import functools

import jax
import jax.numpy as jnp
from jax import lax
from jax.experimental import pallas as pl
from jax.experimental.pallas import tpu as pltpu
from jax.experimental.pallas import tpu_sc as plsc

F32 = jnp.float32
BF16 = jnp.bfloat16

D_MODEL = 1024
N_Q_HEADS = 8
N_KV_HEADS = 2
HEAD_DIM = 64
WINDOW = 128
GQA_GROUP = N_Q_HEADS // N_KV_HEADS
M_HEADS = 4
M_QK_DIM = 64
M_V_DIM = 128
CONV_WIDTH = 4
N_GROUPS = 4
EXPERTS_PER_GROUP = 8
N_EXPERTS = N_GROUPS * EXPERTS_PER_GROUP
D_EXPERT = 512
EPS = 1e-6

ATT_Q_WIDTH = N_Q_HEADS * HEAD_DIM
ATT_KV_WIDTH = N_KV_HEADS * HEAD_DIM
M_QK_WIDTH = M_HEADS * M_QK_DIM
M_V_WIDTH = M_HEADS * M_V_DIM

LANES = 128
NEG = -1e30
V7X_VMEM_BYTES = 64 * 1024 * 1024
VMEM_LIMIT = V7X_VMEM_BYTES - 8 * 1024 * 1024

IN_TILE = 1024
TOK_TILE = 1024
ATT_TILE = 2048
LOOKAHEAD = 2
M_LOOKAHEAD = 4
M_CHUNK = 128
M_BATCH = 8
M_STATE_ROWS = M_V_DIM + 16
SLOT_BLOCK = 512
COMB_TILE = 1024
RANK_RADIX = 65536
RANK_BITS = 16
ROUTER_ROWS = 8 + N_EXPERTS
PACKED = D_MODEL // 2
SC_CHUNK = 64

_O_AQ = 0
_O_AK = _O_AQ + ATT_Q_WIDTH
_O_AV = _O_AK + ATT_KV_WIDTH
_O_MQ = _O_AV + ATT_KV_WIDTH
_O_MK = _O_MQ + M_QK_WIDTH
_O_MV = _O_MK + M_QK_WIDTH
_O_MO = _O_MV + M_V_WIDTH
_O_MI = _O_MO + M_V_WIDTH
_O_MF = _O_MI + M_HEADS
_O_GA = _O_MF + M_HEADS


def _rms(x, w):
    return x * lax.rsqrt(jnp.mean(x * x, axis=-1, keepdims=True) + EPS) * w


def _sigmoid(x):
    return 0.5 * jnp.tanh(0.5 * x) + 0.5


def _pack_rows(x):
    half = x.shape[1] // 2
    bits = lax.bitcast_convert_type(x.astype(BF16).astype(F32), jnp.uint32)
    packed = (bits[:, :half] >> 16) | (bits[:, half:] & jnp.uint32(0xFFFF0000))
    return lax.bitcast_convert_type(packed, jnp.int32)


def _unpack_rows(w):
    u = lax.bitcast_convert_type(w, jnp.uint32)
    lo = lax.bitcast_convert_type(u << 16, F32)
    hi = lax.bitcast_convert_type(u & jnp.uint32(0xFFFF0000), F32)
    return jnp.concatenate([lo, hi], axis=-1)


def _log_sigmoid(x):
    return jnp.minimum(x, 0.0) - jnp.log1p(jnp.exp(-jnp.abs(x)))


_TOK_SPLITS = (ATT_KV_WIDTH, M_QK_WIDTH)
_CHAN_SPLITS = (ATT_Q_WIDTH, ATT_KV_WIDTH, M_QK_WIDTH, M_V_WIDTH, M_V_WIDTH)
GATE_ROWS = 16


def _inproj_kernel(x_ref, nw_ref, w_ref, wt_ref, *out_refs):
    tok_refs = out_refs[:len(_TOK_SPLITS)]
    chan_refs = out_refs[len(_TOK_SPLITS):len(_TOK_SPLITS) + len(_CHAN_SPLITS)]
    grow_ref = out_refs[-1]
    h = _rms(x_ref[...], nw_ref[...]).astype(BF16)
    lo = 0
    for ref, width in zip(tok_refs, _TOK_SPLITS):
        ref[...] = jnp.dot(h, w_ref[:, lo:lo + width], preferred_element_type=F32).astype(BF16)
        lo += width
    lo = 0
    for ref, width in zip(chan_refs, _CHAN_SPLITS):
        ref[0] = lax.dot_general(wt_ref[lo:lo + width, :], h, (((1,), (1,)), ((), ())),
                                 preferred_element_type=F32).astype(BF16)
        lo += width
    grow_ref[...] = lax.dot_general(wt_ref[lo:lo + GATE_ROWS, :], h, (((1,), (1,)), ((), ())),
                                    preferred_element_type=F32)[0:2 * M_HEADS, :]


def _inproj(x2, norm_w, w_tok, w_chan, B, S):
    T = x2.shape[0]
    tm = min(IN_TILE, S)
    tpb = S // tm
    nsteps = T // tm
    row = lambda i: (i, 0)
    const = lambda i: (0, 0)
    chan = lambda i: (i // tpb, 0, i % tpb)
    return pl.pallas_call(
        _inproj_kernel,
        grid=(nsteps,),
        in_specs=[pl.BlockSpec((tm, D_MODEL), row),
                  pl.BlockSpec((1, D_MODEL), const),
                  pl.BlockSpec(w_tok.shape, const),
                  pl.BlockSpec(w_chan.shape, const)],
        out_specs=[*[pl.BlockSpec((tm, w), row) for w in _TOK_SPLITS],
                   *[pl.BlockSpec((1, w, tm), chan) for w in _CHAN_SPLITS],
                   pl.BlockSpec((8, tm), lambda i: (0, i))],
        out_shape=[*[jax.ShapeDtypeStruct((T, w), BF16) for w in _TOK_SPLITS],
                   *[jax.ShapeDtypeStruct((B, w, S), BF16) for w in _CHAN_SPLITS],
                   jax.ShapeDtypeStruct((8, T), F32)],
        compiler_params=pltpu.CompilerParams(dimension_semantics=("arbitrary",),
                                             vmem_limit_bytes=VMEM_LIMIT),
        name="inproj",
    )(x2, norm_w, w_tok, w_chan)


def _attn_kernel(tq, sink_ref, qt_ref, k_ref, kp_ref, vt_ref, vtp_ref, o_ref):
    i = pl.program_id(1)
    W = WINDOW
    G = GQA_GROUP
    u = lax.broadcasted_iota(jnp.int32, (W, W), 0)
    t = lax.broadcasted_iota(jnp.int32, (W, W), 1)
    from_prev = u > t

    def keys_values(j):
        if j == 0:
            return kp_ref[0], vtp_ref[0], k_ref[0, 0:W, :], vt_ref[0, :, 0:W]
        return (k_ref[0, (j - 1) * W:j * W, :], vt_ref[0, :, (j - 1) * W:j * W],
                k_ref[0, j * W:(j + 1) * W, :], vt_ref[0, :, j * W:(j + 1) * W])

    def scores(j, g):
        k_prev, _, k_cur, _ = keys_values(j)
        dims = slice(g * HEAD_DIM, (g + 1) * HEAD_DIM)
        qg = jnp.concatenate([qt_ref[0, h * HEAD_DIM:(h + 1) * HEAD_DIM, j * W:(j + 1) * W]
                              for h in range(g * G, (g + 1) * G)], axis=1)
        return (jnp.dot(k_prev[:, dims], qg, preferred_element_type=F32),
                jnp.dot(k_cur[:, dims], qg, preferred_element_type=F32))

    tasks = [(j, g) for j in range(tq // W) for g in range(N_KV_HEADS)]
    queue = [scores(*t) for t in tasks[:LOOKAHEAD]]
    pair_rows = []
    for n_task, (j, g) in enumerate(tasks):
        s_prev, s_cur = queue.pop(0)
        if n_task + LOOKAHEAD < len(tasks):
            queue.append(scores(*tasks[n_task + LOOKAHEAD]))
        cols = slice(j * W, (j + 1) * W)
        _, vt_prev, _, vt_cur = keys_values(j)
        dims = slice(g * HEAD_DIM, (g + 1) * HEAD_DIM)
        for n in range(G):
            hc = slice(n * W, (n + 1) * W)
            sp = s_prev[:, hc]
            if j == 0:
                sp = jnp.where(i > 0, sp, NEG)
            s = jnp.where(from_prev, sp, s_cur[:, hc])
            sink = sink_ref[g * G + n]
            m = jnp.maximum(jnp.max(s, axis=0, keepdims=True), sink)
            p = jnp.exp(s - m)
            inv = 1.0 / (jnp.sum(p, axis=0, keepdims=True) + jnp.exp(sink - m))
            p_prev = jnp.where(from_prev, p, 0.0).astype(BF16)
            p_cur = jnp.where(from_prev, 0.0, p).astype(BF16)
            o = jnp.dot(vt_prev[dims, :], p_prev, preferred_element_type=F32) \
                + jnp.dot(vt_cur[dims, :], p_cur, preferred_element_type=F32)
            pair_rows.append(o * inv)
        if g == N_KV_HEADS - 1:
            out = [jnp.concatenate(pair_rows[n:n + 2], axis=0).T for n in range(0, N_Q_HEADS, 2)]
            o_ref[0, cols, :] = jnp.concatenate(out, axis=1).astype(BF16)
            pair_rows = []


def _attn(sinks, qt, k3, vt):
    B, S, _ = k3.shape
    tq = min(ATT_TILE, S)
    per = tq // WINDOW
    chan = lambda b, i: (b, 0, i)
    tok = lambda b, i: (b, i, 0)
    return pl.pallas_call(
        functools.partial(_attn_kernel, tq),
        grid=(B, S // tq),
        in_specs=[pl.BlockSpec(memory_space=pltpu.SMEM),
                  pl.BlockSpec((1, ATT_Q_WIDTH, tq), chan),
                  pl.BlockSpec((1, tq, ATT_KV_WIDTH), tok),
                  pl.BlockSpec((1, WINDOW, ATT_KV_WIDTH), lambda b, i: (b, jnp.maximum(i * per - 1, 0), 0)),
                  pl.BlockSpec((1, ATT_KV_WIDTH, tq), chan),
                  pl.BlockSpec((1, ATT_KV_WIDTH, WINDOW), lambda b, i: (b, 0, jnp.maximum(i * per - 1, 0)))],
        out_specs=pl.BlockSpec((1, tq, ATT_Q_WIDTH), tok),
        out_shape=jax.ShapeDtypeStruct((B, S, ATT_Q_WIDTH), BF16),
        compiler_params=pltpu.CompilerParams(dimension_semantics=("arbitrary", "arbitrary"),
                                             vmem_limit_bytes=VMEM_LIMIT),
        name="attn",
    )(sinks, qt, k3, k3, vt, vt)


def _mlstm_kernel(nb, *refs):
    L = M_CHUNK
    H = M_HEADS
    mqt_ref, mk_ref, mvt_ref, mot_ref = refs[:4]
    grow_refs = refs[4:4 + nb]
    (cwq_ref, cbq_ref, cwk_ref, cbk_ref, brow_ref, nw_ref, eg_ref, eu_ref, ed_ref,
     y_ref, eg16_ref, eu16_ref, ed16_ref, state_ref, m_ref, prevq_ref, ubuf_ref) = refs[4 + nb:]
    cidx = pl.program_id(1)
    for src, dst in ((eg_ref, eg16_ref), (eu_ref, eu16_ref), (ed_ref, ed16_ref)):
        dst[...] = src[...].astype(BF16)

    @pl.when(cidx == 0)
    def _():
        state_ref[...] = jnp.zeros_like(state_ref)
        m_ref[...] = jnp.zeros_like(m_ref)
        prevq_ref[...] = jnp.zeros_like(prevq_ref)
        ubuf_ref[:, 0:8, :] = jnp.zeros((nb, 8, M_QK_WIDTH), F32)

    ri = lax.broadcasted_iota(jnp.int32, (L, L), 0)
    ci = lax.broadcasted_iota(jnp.int32, (L, L), 1)
    causal_t = ri <= ci
    triu = jnp.where(causal_t, 1.0, 0.0).astype(F32)
    lane = lax.broadcasted_iota(jnp.int32, (8, L), 1)
    r2 = lax.broadcasted_iota(jnp.int32, (2 * L, L), 0)
    c2 = lax.broadcasted_iota(jnp.int32, (2 * L, L), 1)
    shifts = [jnp.where(r2 - c2 == L - k, 1.0, 0.0).astype(BF16) for k in range(1, CONV_WIDTH)]
    ones_rows = jnp.where(lax.broadcasted_iota(jnp.int32, (M_STATE_ROWS - M_V_DIM, L), 0) == 0,
                          1.0, 0.0).astype(BF16)

    pairs = [(bb, h) for bb in range(nb) for h in range(H)]
    states = [state_ref[bb * H + h] for bb, h in pairs]
    m_prevs = [m_ref[bb] for bb in range(nb)]

    def prepare(bb):
        cur = mqt_ref[bb]
        both = jnp.concatenate([prevq_ref[bb], cur], axis=1)
        acc = cbq_ref[...] + cwq_ref[CONV_WIDTH - 1] * cur.astype(F32)
        for k in range(1, CONV_WIDTH):
            acc = acc + cwq_ref[CONV_WIDTH - 1 - k] * jnp.dot(both, shifts[k - 1],
                                                              preferred_element_type=F32)
        qt = (acc * _sigmoid(acc)).astype(BF16)
        ubuf_ref[bb, 8:L + 8, :] = mk_ref[bb].astype(F32)
        acc = cbk_ref[...] + cwk_ref[CONV_WIDTH - 1:CONV_WIDTH, :] * ubuf_ref[bb, 8:L + 8, :]
        for j in range(CONV_WIDTH - 1):
            off = 8 - (CONV_WIDTH - 1) + j
            acc = acc + cwk_ref[j:j + 1, :] * ubuf_ref[bb, off:off + L, :]
        ubuf_ref[bb, 0:8, :] = ubuf_ref[bb, L:L + 8, :]
        kk = (acc * _sigmoid(acc) * (M_QK_DIM ** -0.5)).astype(BF16)

        gr = grow_refs[bb][...] + brow_ref[...]
        b = jnp.dot(_log_sigmoid(gr), triu, preferred_element_type=F32,
                    precision=lax.Precision.HIGHEST)
        gi = pltpu.roll(gr, 4, axis=0)
        u = gi - b
        cm = u
        for sh in (1, 2, 4, 8, 16, 32, 64):
            if sh < L:
                cm = jnp.maximum(cm, jnp.where(lane >= sh, pltpu.roll(cm, sh, axis=1), NEG))
        m_prev = m_prevs[bb]
        mt = b + jnp.maximum(m_prev, cm)
        wa = jnp.exp(b + m_prev - mt)
        emt = jnp.exp(-mt)
        m_new = jnp.broadcast_to(mt[:, L - 1:L], (8, L))
        b_last = jnp.broadcast_to(b[:, L - 1:L], (8, L))
        wc = jnp.exp(b_last + m_prev - m_new)
        ws = jnp.exp(b_last + u - m_new)
        return qt, kk, b - mt, u.T, wa, emt, wc, ws, m_new

    seqs = [prepare(bb) for bb in range(nb)]

    def products(bb, h, state):
        qt, kk = seqs[bb][0], seqs[bb][1]
        qh = qt[h * M_QK_DIM:(h + 1) * M_QK_DIM, :]
        kh = kk[:, h * M_QK_DIM:(h + 1) * M_QK_DIM]
        return (kh, jnp.dot(kh, qh, preferred_element_type=F32),
                jnp.dot(state.astype(BF16), qh, preferred_element_type=F32))

    new_states, outs = [], []
    queue = [products(*pairs[n], states[n]) for n in range(min(M_LOOKAHEAD, len(pairs)))]
    for n, ((bb, h), state) in enumerate(zip(pairs, states)):
        kh, st, inter = queue.pop(0)
        if n + M_LOOKAHEAD < len(pairs):
            queue.append(products(*pairs[n + M_LOOKAHEAD], states[n + M_LOOKAHEAD]))
        _, _, bmt, ucol, wa, emt, wc, ws, _ = seqs[bb]
        row = slice(H + h, H + h + 1)
        wq = jnp.exp(jnp.where(causal_t, bmt[row, :] + ucol[:, H + h:H + h + 1], NEG))
        sw = (st * wq).astype(BF16)
        vext = jnp.concatenate([mvt_ref[bb, h * M_V_DIM:(h + 1) * M_V_DIM, :], ones_rows], axis=0)
        res = wa[row, :] * inter + jnp.dot(vext, sw, preferred_element_type=F32)
        num = res[:M_V_DIM, :]
        den = res[M_V_DIM:M_V_DIM + 1, :]
        hb = num / jnp.maximum(jnp.abs(den), emt[row, :])
        vw = (vext.astype(F32) * ws[row, :]).astype(BF16)
        new_states.append(wc[row, 0:1] * state + jnp.dot(vw, kh, preferred_element_type=F32))
        hn = hb * lax.rsqrt(jnp.mean(hb * hb, axis=0, keepdims=True) + EPS)
        hn = hn * nw_ref[h * M_V_DIM:(h + 1) * M_V_DIM, :]
        og = _sigmoid(mot_ref[bb, h * M_V_DIM:(h + 1) * M_V_DIM, :].astype(F32))
        outs.append((og * hn).T.astype(BF16))

    for i, (bb, h) in enumerate(pairs):
        state_ref[bb * H + h] = new_states[i]
    for bb in range(nb):
        m_ref[bb] = seqs[bb][8]
        prevq_ref[bb] = mqt_ref[bb]
        y_ref[bb] = jnp.concatenate(outs[bb * H:(bb + 1) * H], axis=1)


def _mlstm(mqt, mk3, mvt, mot, grow, conv_w, conv_b, bias_row, norm_w, expert_ws):
    B, S, _ = mk3.shape
    L = M_CHUNK
    nb = max(d for d in range(1, M_BATCH + 1) if B % d == 0)
    nc = S // L
    assert L == LANES, "per-head scalars are kept lane-replicated next to (8, L) gate rows"
    tok = lambda b, c: (b, c, 0)
    chan = lambda b, c: (b, 0, c)
    const2 = lambda b, c: (0, 0)
    const3 = lambda b, c: (0, 0, 0)
    grow_specs = [pl.BlockSpec((8, L), functools.partial(lambda b, c, n: (0, (b * nb + n) * nc + c), n=n))
                  for n in range(nb)]
    rep = lambda v: jnp.broadcast_to(v.astype(F32)[..., None], v.shape + (L,))
    cwq, cbq = rep(conv_w[:, :M_QK_WIDTH]), rep(conv_b[:M_QK_WIDTH])
    cwk, cbk = conv_w[:, M_QK_WIDTH:].astype(F32), conv_b[M_QK_WIDTH:].reshape(1, -1).astype(F32)
    nsteps = (B // nb) * nc
    assert all(w.shape[0] % (8 * nsteps) == 0 for w in expert_ws)
    cast_specs = [pl.BlockSpec((w.shape[0] // nsteps, w.shape[1]), lambda b, c: (b * nc + c, 0))
                  for w in expert_ws]
    return pl.pallas_call(
        functools.partial(_mlstm_kernel, nb),
        grid=(B // nb, nc),
        in_specs=[pl.BlockSpec((nb, M_QK_WIDTH, L), chan),
                  pl.BlockSpec((nb, L, M_QK_WIDTH), tok),
                  pl.BlockSpec((nb, M_V_WIDTH, L), chan),
                  pl.BlockSpec((nb, M_V_WIDTH, L), chan),
                  *grow_specs,
                  pl.BlockSpec((CONV_WIDTH, M_QK_WIDTH, L), const3),
                  pl.BlockSpec((M_QK_WIDTH, L), const2),
                  pl.BlockSpec((CONV_WIDTH, M_QK_WIDTH), const2),
                  pl.BlockSpec((1, M_QK_WIDTH), const2),
                  pl.BlockSpec((8, 1), const2),
                  pl.BlockSpec((M_V_WIDTH, L), const2),
                  *cast_specs],
        out_specs=[pl.BlockSpec((nb, L, M_V_WIDTH), tok), *cast_specs],
        out_shape=[jax.ShapeDtypeStruct((B, S, M_V_WIDTH), BF16),
                   *[jax.ShapeDtypeStruct(w.shape, BF16) for w in expert_ws]],
        scratch_shapes=[pltpu.VMEM((nb * M_HEADS, M_STATE_ROWS, M_QK_DIM), F32),
                        pltpu.VMEM((nb, 8, LANES), F32),
                        pltpu.VMEM((nb, M_QK_WIDTH, L), BF16),
                        pltpu.VMEM((nb, L + 8, M_QK_WIDTH), F32)],
        compiler_params=pltpu.CompilerParams(dimension_semantics=("arbitrary", "arbitrary"),
                                             vmem_limit_bytes=VMEM_LIMIT),
        name="mlstm",
    )(mqt, mk3, mvt, mot, *([grow] * nb), cwq, cbq, cwk, cbk, bias_row, rep(norm_w), *expert_ws)


def _outproj_kernel(x_ref, ya_ref, ym_ref, nmix_ref, wgab_ref, wa_ref, wm_ref, wo_ref, nw_ref, wr_ref, br_ref,
                    x1_ref, h2_ref, meta_ref, code_ref, cnt_ref, cnt_scr):
    tm = x_ref.shape[0]
    i = pl.program_id(0)

    @pl.when(i == 0)
    def _():
        cnt_scr[...] = jnp.zeros_like(cnt_scr)

    pa = jnp.dot(ya_ref[...], wa_ref[...], preferred_element_type=F32)
    pm = jnp.dot(ym_ref[...], wm_ref[...], preferred_element_type=F32)
    x = x_ref[...]
    h1 = _rms(x, nmix_ref[...]).astype(BF16)
    ga = jnp.dot(h1, wgab_ref[:, :D_MODEL], preferred_element_type=F32)
    gb = jnp.dot(h1, wgab_ref[:, D_MODEL:], preferred_element_type=F32)
    mix = _sigmoid(ga) * pa + _sigmoid(gb) * pm
    x1 = x + jnp.dot(mix.astype(BF16), wo_ref[...], preferred_element_type=F32)
    x1_ref[...] = x1
    h2 = _rms(x1, nw_ref[...])
    h2_ref[...] = _pack_rows(h2)

    nt = (((1,), (1,)), ((), ()))
    h_hi = h2.astype(BF16)
    h_lo = (h2 - h_hi.astype(F32)).astype(BF16)
    hw = jnp.dot(h_hi, wr_ref[...], preferred_element_type=F32)
    logits = (hw[:, :LANES] + hw[:, LANES:]
              + jnp.dot(h_lo, wr_ref[:, :LANES], preferred_element_type=F32)).T[:ROUTER_ROWS]
    logits = logits + br_ref[...]
    row8 = lax.broadcasted_iota(jnp.int32, (8, tm), 0)

    def first_argmax(v):
        mx = jnp.max(v, axis=0, keepdims=True)
        idx = jnp.min(jnp.where(v == mx, row8, 8), axis=0, keepdims=True)
        return mx, idx

    gl = jnp.where(row8 < N_GROUPS, logits[0:8], NEG)
    gmax, gi = first_argmax(gl)
    gp = 1.0 / jnp.sum(jnp.exp(gl - gmax), axis=0, keepdims=True)
    el = jnp.zeros((8, tm), F32)
    for g in range(N_GROUPS):
        el = jnp.where(gi == g, logits[8 + 8 * g:16 + 8 * g], el)
    v1, j1 = first_argmax(el)
    v2, j2 = first_argmax(jnp.where(row8 == j1, NEG, el))
    t = jnp.exp(v2 - v1)
    w1 = gp / (1.0 + t)
    w2 = gp * t / (1.0 + t)
    e1 = gi * EXPERTS_PER_GROUP + j1
    e2 = gi * EXPERTS_PER_GROUP + j2

    erow = lax.broadcasted_iota(jnp.int32, (N_EXPERTS, tm), 0)
    hit1 = erow == e1
    hit2 = erow == e2
    onehot = jnp.where(hit1 | hit2, 1.0, 0.0)
    ri = lax.broadcasted_iota(jnp.int32, (tm, tm), 0)
    ci = lax.broadcasted_iota(jnp.int32, (tm, tm), 1)
    before = jnp.where(ri < ci, 1.0, 0.0).astype(BF16)
    rank = jnp.dot(onehot.astype(BF16), before, preferred_element_type=F32) + cnt_scr[...]
    r1 = jnp.sum(jnp.where(hit1, rank, 0.0), axis=0, keepdims=True)
    r2 = jnp.sum(jnp.where(hit2, rank, 0.0), axis=0, keepdims=True)
    cnt = jnp.broadcast_to((rank + onehot)[:, tm - 1:tm], (N_EXPERTS, tm))
    cnt_scr[...] = cnt
    cnt_ref[...] = cnt[:, :LANES]

    c1 = e1 * RANK_RADIX + r1.astype(jnp.int32)
    c2 = e2 * RANK_RADIX + r2.astype(jnp.int32)
    code_ref[...] = jnp.where(row8 == 0, c1, jnp.where(row8 == 1, c2, 0))
    wrow = lax.broadcasted_iota(jnp.int32, (LANES, tm), 0)
    meta_ref[...] = jnp.where(wrow == 4, w1, jnp.where(wrow == 5, w2, 0.0)).T


def _outproj(x2, ya, ym, norm_mix, w_gab, wa, wm, wo, norm_w, w_rt, b_rt):
    T = x2.shape[0]
    tm = b_rt.shape[1]
    row = lambda i: (i, 0)
    const = lambda i: (0, 0)
    resident = lambda shape: pl.BlockSpec(shape, const, pipeline_mode=pl.Buffered(1))
    return pl.pallas_call(
        _outproj_kernel,
        grid=(T // tm,),
        in_specs=[pl.BlockSpec((tm, D_MODEL), row),
                  pl.BlockSpec((tm, ATT_Q_WIDTH), row),
                  pl.BlockSpec((tm, M_V_WIDTH), row),
                  resident((1, D_MODEL)),
                  resident((D_MODEL, 2 * D_MODEL)),
                  resident((ATT_Q_WIDTH, D_MODEL)),
                  resident((M_V_WIDTH, D_MODEL)),
                  resident((D_MODEL, D_MODEL)),
                  resident((1, D_MODEL)),
                  resident((D_MODEL, 2 * LANES)),
                  resident((ROUTER_ROWS, tm))],
        out_specs=[pl.BlockSpec((tm, D_MODEL), row),
                   pl.BlockSpec((tm, PACKED), row),
                   pl.BlockSpec((tm, LANES), row),
                   pl.BlockSpec((8, tm), lambda i: (0, i)),
                   pl.BlockSpec((N_EXPERTS, LANES), const)],
        out_shape=[jax.ShapeDtypeStruct((T, D_MODEL), F32),
                   jax.ShapeDtypeStruct((T, PACKED), jnp.int32),
                   jax.ShapeDtypeStruct((T, LANES), F32),
                   jax.ShapeDtypeStruct((8, T), jnp.int32),
                   jax.ShapeDtypeStruct((N_EXPERTS, LANES), F32)],
        scratch_shapes=[pltpu.VMEM((N_EXPERTS, tm), F32)],
        compiler_params=pltpu.CompilerParams(dimension_semantics=("arbitrary",),
                                             vmem_limit_bytes=VMEM_LIMIT),
        name="outproj",
    )(x2, ya, ym, norm_mix, w_gab, wa, wm, wo, norm_w, w_rt, b_rt)


def _slots_kernel(cnt_ref, pstart_ref, blk_e_ref, nvalid_ref, nused_ref):
    R = SLOT_BLOCK
    nblk = blk_e_ref.shape[0]

    def per_expert(e, run):
        c = cnt_ref[e]
        end = run + ((c + R - 1) // R) * R
        pstart_ref[e] = run

        def set_blk(b, carry):
            blk_e_ref[b] = e
            nvalid_ref[b] = jnp.minimum(run + c - b * R, R)
            return carry
        lax.fori_loop(run // R, end // R, set_blk, 0)
        return end

    total = lax.fori_loop(0, N_EXPERTS, per_expert, 0)
    nused_ref[0] = total // R

    def tail_blk(b, carry):
        blk_e_ref[b] = N_EXPERTS - 1
        nvalid_ref[b] = 0
        return carry
    lax.fori_loop(total // R, nblk, tail_blk, 0)


def _slots(counts, nblk):
    smem = pl.BlockSpec(memory_space=pltpu.SMEM)
    return pl.pallas_call(
        _slots_kernel,
        in_specs=[smem],
        out_specs=[smem, smem, smem, smem],
        out_shape=[jax.ShapeDtypeStruct((N_EXPERTS,), jnp.int32),
                   jax.ShapeDtypeStruct((nblk,), jnp.int32),
                   jax.ShapeDtypeStruct((nblk,), jnp.int32),
                   jax.ShapeDtypeStruct((1,), jnp.int32)],
        name="slots",
    )(counts)


def _experts_kernel(blk_e_ref, nvalid_ref, nused_ref, xs_ref, wg_ref, wu_ref, wd_ref, ys_ref):
    j = pl.program_id(0)
    R = SLOT_BLOCK
    valid = j < nused_ref[0]

    @pl.when(valid)
    def _():
        rows = lax.broadcasted_iota(jnp.int32, (R, PACKED), 0)
        xw = jnp.where(rows < nvalid_ref[j], xs_ref[...], 0)
        xb = _unpack_rows(xw).astype(BF16)
        g = jnp.dot(xb, wg_ref[0], preferred_element_type=F32)
        u = jnp.dot(xb, wu_ref[0], preferred_element_type=F32)
        act = (g * _sigmoid(g) * u).astype(BF16)
        ys_ref[...] = _pack_rows(jnp.dot(act, wd_ref[0], preferred_element_type=F32))


def _experts(blk_e, nvalid, nused, xs, w_gate, w_up, w_down):
    nblk = blk_e.shape[0]
    R = SLOT_BLOCK
    wspec = lambda shape: pl.BlockSpec((1,) + shape, lambda j, be, nv, nu: (be[j], 0, 0))
    grid_spec = pltpu.PrefetchScalarGridSpec(
        num_scalar_prefetch=3,
        grid=(nblk,),
        in_specs=[pl.BlockSpec((R, PACKED), lambda j, be, nv, nu: (jnp.minimum(j, nu[0] - 1), 0)),
                  wspec((D_MODEL, D_EXPERT)),
                  wspec((D_MODEL, D_EXPERT)),
                  wspec((D_EXPERT, D_MODEL))],
        out_specs=pl.BlockSpec((R, PACKED), lambda j, be, nv, nu: (jnp.minimum(j, nu[0] - 1), 0)),
    )
    return pl.pallas_call(
        _experts_kernel,
        grid_spec=grid_spec,
        out_shape=jax.ShapeDtypeStruct((nblk * R, PACKED), jnp.int32),
        compiler_params=pltpu.CompilerParams(dimension_semantics=("arbitrary",),
                                             vmem_limit_bytes=VMEM_LIMIT),
        name="experts",
    )(blk_e, nvalid, nused, xs, w_gate, w_up, w_down)


def _dest_kernel(pstart_ref, code_ref, dest_ref):
    c = code_ref[...]
    e = c >> RANK_BITS
    base = jnp.zeros_like(c)
    for k in range(N_EXPERTS):
        base = jnp.where(e == k, pstart_ref[k], base)
    dest_ref[...] = base + (c & (RANK_RADIX - 1))


def _dests(pstart, code_rows):
    T = code_rows.shape[1]
    tc = min(4096, T)
    return pl.pallas_call(
        _dest_kernel,
        grid=(T // tc,),
        in_specs=[pl.BlockSpec(memory_space=pltpu.SMEM),
                  pl.BlockSpec((8, tc), lambda i: (0, i))],
        out_specs=pl.BlockSpec((8, tc), lambda i: (0, i)),
        out_shape=jax.ShapeDtypeStruct((8, T), jnp.int32),
        name="dests",
    )(pstart, code_rows)


def _sc_workers():
    info = plsc.get_sparse_core_info()
    return info.num_cores, info.num_cores * info.num_subcores


def _sc_scatter_rows(src, idx, n_out):
    n_src, width = src.shape
    nc, nw = _sc_workers()
    per_w = n_src // nw
    ch = SC_CHUNK
    nch = per_w // ch
    assert idx.shape[0] == 2 * n_src and per_w * nw == n_src and nch * ch == per_w and nch % 2 == 0
    mesh = plsc.VectorSubcoreMesh(core_axis_name="c", subcore_axis_name="s")
    dma = pltpu.SemaphoreType.DMA

    @functools.partial(
        pl.kernel, mesh=mesh, out_type=jax.ShapeDtypeStruct((n_out, width), src.dtype),
        scratch_types=[pltpu.VMEM((per_w,), jnp.int32), pltpu.VMEM((per_w,), jnp.int32),
                       pltpu.VMEM((ch, width), src.dtype), pltpu.VMEM((ch, width), src.dtype),
                       dma, dma, dma, dma, dma, dma],
        name="sc_scatter")
    def scatter(src_hbm, idx_hbm, out_hbm, idx_a, idx_b, buf0, buf1, in0, in1, out0a, out0b, out1a, out1b):
        base = (lax.axis_index("s") * nc + lax.axis_index("c")) * per_w
        pltpu.sync_copy(idx_hbm.at[pl.ds(base, per_w)], idx_a)
        pltpu.sync_copy(idx_hbm.at[pl.ds(n_src + base, per_w)], idx_b)

        def read(c, buf, sem):
            return pltpu.make_async_copy(src_hbm.at[pl.ds(base + c * ch, ch)], buf, sem)

        def writes(c, buf, sem_a, sem_b):
            rows = pl.ds(c * ch, ch)
            return (pltpu.make_async_copy(buf, out_hbm.at[idx_a.at[rows]], sem_a),
                    pltpu.make_async_copy(buf, out_hbm.at[idx_b.at[rows]], sem_b))

        def start(copies):
            for cp in copies:
                cp.start()

        def wait(copies):
            for cp in copies:
                cp.wait()

        read(0, buf0, in0).start()

        @pl.loop(0, nch, step=2)
        def _(c):
            @pl.when(c > 0)
            def _():
                wait(writes(c - 1, buf1, out1a, out1b))
            read(c + 1, buf1, in1).start()
            read(c, buf0, in0).wait()
            start(writes(c, buf0, out0a, out0b))
            read(c + 1, buf1, in1).wait()
            start(writes(c + 1, buf1, out1a, out1b))
            wait(writes(c, buf0, out0a, out0b))

            @pl.when(c + 2 < nch)
            def _():
                read(c + 2, buf0, in0).start()

        wait(writes(nch - 1, buf1, out1a, out1b))

    return scatter(src, idx)


def _sc_gather_rows(src, idx):
    M = idx.shape[0]
    width = src.shape[1]
    nc, nw = _sc_workers()
    per_w = M // nw
    ch = SC_CHUNK
    nch = per_w // ch
    assert per_w * nw == M and nch * ch == per_w and nch % 2 == 0
    mesh = plsc.VectorSubcoreMesh(core_axis_name="c", subcore_axis_name="s")
    dma = pltpu.SemaphoreType.DMA

    @functools.partial(
        pl.kernel, mesh=mesh, out_type=jax.ShapeDtypeStruct((M, width), src.dtype),
        scratch_types=[pltpu.VMEM((per_w,), jnp.int32),
                       pltpu.VMEM((ch, width), src.dtype), pltpu.VMEM((ch, width), src.dtype),
                       dma, dma, dma, dma],
        name="sc_gather")
    def gather(src_hbm, idx_hbm, out_hbm, idx_v, buf0, buf1, in0, in1, out0, out1):
        base = (lax.axis_index("s") * nc + lax.axis_index("c")) * per_w
        pltpu.sync_copy(idx_hbm.at[pl.ds(base, per_w)], idx_v)

        def read(c, buf, sem):
            return pltpu.make_async_copy(src_hbm.at[idx_v.at[pl.ds(c * ch, ch)]], buf, sem)

        def write(c, buf, sem):
            return pltpu.make_async_copy(buf, out_hbm.at[pl.ds(base + c * ch, ch)], sem)

        read(0, buf0, in0).start()

        @pl.loop(0, nch, step=2)
        def _(c):
            @pl.when(c > 0)
            def _():
                write(c - 1, buf1, out1).wait()
            read(c + 1, buf1, in1).start()
            read(c, buf0, in0).wait()
            write(c, buf0, out0).start()
            read(c + 1, buf1, in1).wait()
            write(c + 1, buf1, out1).start()
            write(c, buf0, out0).wait()

            @pl.when(c + 2 < nch)
            def _():
                read(c + 2, buf0, in0).start()

        write(nch - 1, buf1, out1).wait()

    return gather(src, idx)


def _final_kernel(x1_ref, y1_ref, y2_ref, meta_ref, nw_ref, o_ref):
    w1 = meta_ref[:, 4:5]
    w2 = meta_ref[:, 5:6]
    x2 = x1_ref[...] + (w1 * _unpack_rows(y1_ref[...]) + w2 * _unpack_rows(y2_ref[...]))
    o_ref[...] = _rms(x2, nw_ref[...])


def _final(x1, yg, meta, norm_w):
    T = x1.shape[0]
    tm = min(COMB_TILE, T)
    nt = T // tm
    row = lambda i: (i, 0)
    return pl.pallas_call(
        _final_kernel,
        grid=(nt,),
        in_specs=[pl.BlockSpec((tm, D_MODEL), row),
                  pl.BlockSpec((tm, PACKED), row),
                  pl.BlockSpec((tm, PACKED), lambda i: (i + nt, 0)),
                  pl.BlockSpec((tm, LANES), row),
                  pl.BlockSpec((1, D_MODEL), lambda i: (0, 0))],
        out_specs=pl.BlockSpec((tm, D_MODEL), row),
        out_shape=jax.ShapeDtypeStruct((T, D_MODEL), F32),
        compiler_params=pltpu.CompilerParams(dimension_semantics=("arbitrary",),
                                             vmem_limit_bytes=VMEM_LIMIT),
        name="final",
    )(x1, yg, yg, meta, norm_w)


def _layer(x, norm_mix_w, w_in, conv_w, conv_b, b_igate, b_fgate, attn_sinks, mlstm_norm_w,
           w_attn_o, w_mlstm_o, w_out, norm_ffn_w, w_group, b_group, w_router, b_router,
           w_gate, w_up, w_down, out_norm_w):
    B, S, D = x.shape
    T = B * S
    x2 = x.reshape(T, D)

    w_tok = jnp.concatenate([w_in[:, _O_AK:_O_AV], w_in[:, _O_MK:_O_MV]], axis=1).astype(BF16)
    w_chan = jnp.concatenate([w_in[:, _O_AQ:_O_AK] * (HEAD_DIM ** -0.5), w_in[:, _O_AV:_O_MQ],
                              w_in[:, _O_MQ:_O_MK], w_in[:, _O_MV:_O_MI],
                              jnp.pad(w_in[:, _O_MI:_O_GA], ((0, 0), (0, GATE_ROWS - 2 * M_HEADS)))],
                             axis=1).astype(BF16).T
    ak, mk, aqt, avt, mqt, mvt, mot, grow = _inproj(x2, norm_mix_w.reshape(1, D), w_tok, w_chan, B, S)

    ya = _attn(attn_sinks.astype(F32), aqt, ak.reshape(B, S, -1), avt)

    bias = jnp.concatenate([b_igate, b_fgate]).astype(F32)
    expert_ws = [w.reshape(-1, w.shape[-1]) for w in (w_gate, w_up, w_down)]
    ym, wg16, wu16, wd16 = _mlstm(mqt, mk.reshape(B, S, -1), mvt, mot, grow, conv_w, conv_b,
                                  bias.reshape(2 * M_HEADS, 1), mlstm_norm_w, expert_ws)

    gpad = jnp.zeros((8 - N_GROUPS, D), F32)
    w_rt = jnp.concatenate([w_group.T, gpad, w_router.T, jnp.zeros((LANES - ROUTER_ROWS, D), F32)], axis=0)
    w_rt_hi = w_rt.astype(BF16)
    w_rt = jnp.concatenate([w_rt_hi, (w_rt - w_rt_hi.astype(F32)).astype(BF16)], axis=0).T
    b_rt = jnp.concatenate([b_group, jnp.zeros((8 - N_GROUPS,), F32), b_router]).astype(F32)
    b_rt = jnp.broadcast_to(b_rt[:, None], (ROUTER_ROWS, min(TOK_TILE, T)))
    x1, h2p, meta, code_rows, cnt = _outproj(x2, ya.reshape(T, -1), ym.reshape(T, -1),
                                            norm_mix_w.reshape(1, D), w_in[:, _O_GA:].astype(BF16),
                                            w_attn_o.astype(BF16), w_mlstm_o.astype(BF16),
                                            w_out.astype(BF16), norm_ffn_w.reshape(1, D), w_rt, b_rt)

    counts = cnt[:, 0].astype(jnp.int32)
    nblk = (2 * T) // SLOT_BLOCK + N_EXPERTS
    pstart, blk_e, nvalid, nused = _slots(counts, nblk)
    dest = _dests(pstart, code_rows)[0:2].reshape(2 * T)

    xs = _sc_scatter_rows(h2p, dest, nblk * SLOT_BLOCK)
    ys = _experts(blk_e, nvalid, nused, xs, wg16.reshape(w_gate.shape), wu16.reshape(w_up.shape),
                  wd16.reshape(w_down.shape))
    yg = _sc_gather_rows(ys, dest)
    out = _final(x1, yg, meta, out_norm_w.reshape(1, D))
    return out.reshape(B, S, D)


def kernel(x, norm_mix_w, w_in, conv_w, conv_b, b_igate, b_fgate, attn_sinks, mlstm_norm_w, w_attn_o,
           w_mlstm_o, w_out, norm_ffn_w, w_group, b_group, w_router, b_router, w_gate, w_up, w_down,
           norm_final_w):
    depth = w_in.shape[0]
    assert depth == 1, "final RMSNorm is fused into the last layer's combine kernel"
    return _layer(x, norm_mix_w[0], w_in[0], conv_w[0], conv_b[0], b_igate[0], b_fgate[0],
                  attn_sinks[0], mlstm_norm_w[0], w_attn_o[0], w_mlstm_o[0], w_out[0], norm_ffn_w[0],
                  w_group[0], b_group[0], w_router[0], b_router[0], w_gate[0], w_up[0], w_down[0],
                  norm_final_w)
```

```python
import functools

import jax
import jax.numpy as jnp
from jax import lax
from jax.experimental import pallas as pl
from jax.experimental.pallas import tpu as pltpu
from jax.experimental.pallas import tpu_sc as plsc

F32 = jnp.float32
BF16 = jnp.bfloat16

D_MODEL = 1024
N_Q_HEADS = 8
N_KV_HEADS = 2
HEAD_DIM = 64
WINDOW = 128
GQA_GROUP = N_Q_HEADS // N_KV_HEADS
M_HEADS = 4
M_QK_DIM = 64
M_V_DIM = 128
CONV_WIDTH = 4
N_GROUPS = 4
EXPERTS_PER_GROUP = 8
N_EXPERTS = N_GROUPS * EXPERTS_PER_GROUP
D_EXPERT = 512
EPS = 1e-6

ATT_Q_WIDTH = N_Q_HEADS * HEAD_DIM
ATT_KV_WIDTH = N_KV_HEADS * HEAD_DIM
M_QK_WIDTH = M_HEADS * M_QK_DIM
M_V_WIDTH = M_HEADS * M_V_DIM

LANES = 128
NEG = -1e30
V7X_VMEM_BYTES = 64 * 1024 * 1024
VMEM_LIMIT = V7X_VMEM_BYTES - 8 * 1024 * 1024

IN_TILE = 1024
TOK_TILE = 1024
ATT_TILE = 2048
LOOKAHEAD = 2
M_LOOKAHEAD = 4
M_CHUNK = 128
M_BATCH = 8
M_STATE_ROWS = M_V_DIM + 16
SLOT_BLOCK = 512
COMB_TILE = 1024
RANK_RADIX = 65536
RANK_BITS = 16
ROUTER_ROWS = 8 + N_EXPERTS
PACKED = D_MODEL // 2
SC_CHUNK = 64

_O_AQ = 0
_O_AK = _O_AQ + ATT_Q_WIDTH
_O_AV = _O_AK + ATT_KV_WIDTH
_O_MQ = _O_AV + ATT_KV_WIDTH
_O_MK = _O_MQ + M_QK_WIDTH
_O_MV = _O_MK + M_QK_WIDTH
_O_MO = _O_MV + M_V_WIDTH
_O_MI = _O_MO + M_V_WIDTH
_O_MF = _O_MI + M_HEADS
_O_GA = _O_MF + M_HEADS


def _rms(x, w):
    return x * lax.rsqrt(jnp.mean(x * x, axis=-1, keepdims=True) + EPS) * w


def _sigmoid(x):
    return 0.5 * jnp.tanh(0.5 * x) + 0.5


def _pack_rows(x):
    half = x.shape[1] // 2
    bits = lax.bitcast_convert_type(x.astype(BF16).astype(F32), jnp.uint32)
    packed = (bits[:, :half] >> 16) | (bits[:, half:] & jnp.uint32(0xFFFF0000))
    return lax.bitcast_convert_type(packed, jnp.int32)


def _unpack_rows(w):
    u = lax.bitcast_convert_type(w, jnp.uint32)
    lo = lax.bitcast_convert_type(u << 16, F32)
    hi = lax.bitcast_convert_type(u & jnp.uint32(0xFFFF0000), F32)
    return jnp.concatenate([lo, hi], axis=-1)


def _log_sigmoid(x):
    return jnp.minimum(x, 0.0) - jnp.log1p(jnp.exp(-jnp.abs(x)))


_TOK_SPLITS = (ATT_KV_WIDTH, M_QK_WIDTH)
_CHAN_SPLITS = (ATT_Q_WIDTH, ATT_KV_WIDTH, M_QK_WIDTH, M_V_WIDTH, M_V_WIDTH)
GATE_ROWS = 16
_NT = (((1,), (1,)), ((), ()))


def _inproj_kernel(x_ref, nw_ref, w_ref, wt_ref, *out_refs):
    tok_refs = out_refs[:len(_TOK_SPLITS)]
    chan_refs = out_refs[len(_TOK_SPLITS):len(_TOK_SPLITS) + len(_CHAN_SPLITS)]
    grow_ref = out_refs[-1]
    h = _rms(x_ref[...], nw_ref[...]).astype(BF16)
    lo = 0
    for ref, width in zip(tok_refs, _TOK_SPLITS):
        ref[...] = lax.dot_general(h, w_ref[lo:lo + width, :], _NT, preferred_element_type=F32).astype(BF16)
        lo += width
    lo = 0
    for ref, width in zip(chan_refs, _CHAN_SPLITS):
        ref[0] = lax.dot_general(wt_ref[lo:lo + width, :], h, (((1,), (1,)), ((), ())),
                                 preferred_element_type=F32).astype(BF16)
        lo += width
    grow_ref[...] = lax.dot_general(wt_ref[lo:lo + GATE_ROWS, :], h, (((1,), (1,)), ((), ())),
                                    preferred_element_type=F32)[0:2 * M_HEADS, :]


def _inproj(x2, norm_w, w_tok, w_chan, B, S):
    T = x2.shape[0]
    tm = min(IN_TILE, S)
    tpb = S // tm
    nsteps = T // tm
    row = lambda i: (i, 0)
    const = lambda i: (0, 0)
    chan = lambda i: (i // tpb, 0, i % tpb)
    return pl.pallas_call(
        _inproj_kernel,
        grid=(nsteps,),
        in_specs=[pl.BlockSpec((tm, D_MODEL), row),
                  pl.BlockSpec((1, D_MODEL), const),
                  pl.BlockSpec(w_tok.shape, const),
                  pl.BlockSpec(w_chan.shape, const)],
        out_specs=[*[pl.BlockSpec((tm, w), row) for w in _TOK_SPLITS],
                   *[pl.BlockSpec((1, w, tm), chan) for w in _CHAN_SPLITS],
                   pl.BlockSpec((8, tm), lambda i: (0, i))],
        out_shape=[*[jax.ShapeDtypeStruct((T, w), BF16) for w in _TOK_SPLITS],
                   *[jax.ShapeDtypeStruct((B, w, S), BF16) for w in _CHAN_SPLITS],
                   jax.ShapeDtypeStruct((8, T), F32)],
        compiler_params=pltpu.CompilerParams(dimension_semantics=("arbitrary",),
                                             vmem_limit_bytes=VMEM_LIMIT),
        name="inproj",
    )(x2, norm_w, w_tok, w_chan)


def _attn_kernel(tq, sink_ref, qt_ref, k_ref, kp_ref, vt_ref, vtp_ref, o_ref):
    i = pl.program_id(1)
    W = WINDOW
    G = GQA_GROUP
    u = lax.broadcasted_iota(jnp.int32, (W, W), 0)
    t = lax.broadcasted_iota(jnp.int32, (W, W), 1)
    from_prev = u > t

    def keys_values(j):
        if j == 0:
            return kp_ref[0], vtp_ref[0], k_ref[0, 0:W, :], vt_ref[0, :, 0:W]
        return (k_ref[0, (j - 1) * W:j * W, :], vt_ref[0, :, (j - 1) * W:j * W],
                k_ref[0, j * W:(j + 1) * W, :], vt_ref[0, :, j * W:(j + 1) * W])

    def scores(j, g):
        k_prev, _, k_cur, _ = keys_values(j)
        dims = slice(g * HEAD_DIM, (g + 1) * HEAD_DIM)
        qg = jnp.concatenate([qt_ref[0, h * HEAD_DIM:(h + 1) * HEAD_DIM, j * W:(j + 1) * W]
                              for h in range(g * G, (g + 1) * G)], axis=1)
        return (jnp.dot(k_prev[:, dims], qg, preferred_element_type=F32),
                jnp.dot(k_cur[:, dims], qg, preferred_element_type=F32))

    tasks = [(j, g) for j in range(tq // W) for g in range(N_KV_HEADS)]
    queue = [scores(*t) for t in tasks[:LOOKAHEAD]]
    pair_rows = []
    for n_task, (j, g) in enumerate(tasks):
        s_prev, s_cur = queue.pop(0)
        if n_task + LOOKAHEAD < len(tasks):
            queue.append(scores(*tasks[n_task + LOOKAHEAD]))
        cols = slice(j * W, (j + 1) * W)
        _, vt_prev, _, vt_cur = keys_values(j)
        dims = slice(g * HEAD_DIM, (g + 1) * HEAD_DIM)
        for n in range(G):
            hc = slice(n * W, (n + 1) * W)
            sp = s_prev[:, hc]
            if j == 0:
                sp = jnp.where(i > 0, sp, NEG)
            s = jnp.where(from_prev, sp, s_cur[:, hc])
            sink = sink_ref[g * G + n]
            m = jnp.maximum(jnp.max(s, axis=0, keepdims=True), sink)
            p = jnp.exp(s - m)
            inv = 1.0 / (jnp.sum(p, axis=0, keepdims=True) + jnp.exp(sink - m))
            p_prev = jnp.where(from_prev, p, 0.0).astype(BF16)
            p_cur = jnp.where(from_prev, 0.0, p).astype(BF16)
            o = jnp.dot(vt_prev[dims, :], p_prev, preferred_element_type=F32) \
                + jnp.dot(vt_cur[dims, :], p_cur, preferred_element_type=F32)
            pair_rows.append(o * inv)
        if g == N_KV_HEADS - 1:
            out = [jnp.concatenate(pair_rows[n:n + 2], axis=0).T for n in range(0, N_Q_HEADS, 2)]
            o_ref[0, cols, :] = jnp.concatenate(out, axis=1).astype(BF16)
            pair_rows = []


def _attn(sinks, qt, k3, vt):
    B, S, _ = k3.shape
    tq = min(ATT_TILE, S)
    per = tq // WINDOW
    chan = lambda b, i: (b, 0, i)
    tok = lambda b, i: (b, i, 0)
    return pl.pallas_call(
        functools.partial(_attn_kernel, tq),
        grid=(B, S // tq),
        in_specs=[pl.BlockSpec(memory_space=pltpu.SMEM),
                  pl.BlockSpec((1, ATT_Q_WIDTH, tq), chan),
                  pl.BlockSpec((1, tq, ATT_KV_WIDTH), tok),
                  pl.BlockSpec((1, WINDOW, ATT_KV_WIDTH), lambda b, i: (b, jnp.maximum(i * per - 1, 0), 0)),
                  pl.BlockSpec((1, ATT_KV_WIDTH, tq), chan),
                  pl.BlockSpec((1, ATT_KV_WIDTH, WINDOW), lambda b, i: (b, 0, jnp.maximum(i * per - 1, 0)))],
        out_specs=pl.BlockSpec((1, tq, ATT_Q_WIDTH), tok),
        out_shape=jax.ShapeDtypeStruct((B, S, ATT_Q_WIDTH), BF16),
        compiler_params=pltpu.CompilerParams(dimension_semantics=("arbitrary", "arbitrary"),
                                             vmem_limit_bytes=VMEM_LIMIT),
        name="attn",
    )(sinks, qt, k3, k3, vt, vt)


def _mlstm_kernel(nb, *refs):
    L = M_CHUNK
    H = M_HEADS
    mqt_ref, mk_ref, mvt_ref, mot_ref = refs[:4]
    grow_refs = refs[4:4 + nb]
    (cwq_ref, cbq_ref, cwk_ref, cbk_ref, brow_ref, nw_ref, eg_ref, eu_ref, ed_ref,
     y_ref, eg16_ref, eu16_ref, ed16_ref, state_ref, m_ref, prevq_ref, ubuf_ref) = refs[4 + nb:]
    cidx = pl.program_id(1)
    for src, dst in ((eg_ref, eg16_ref), (eu_ref, eu16_ref), (ed_ref, ed16_ref)):
        dst[...] = src[...].astype(BF16)

    @pl.when(cidx == 0)
    def _():
        state_ref[...] = jnp.zeros_like(state_ref)
        m_ref[...] = jnp.zeros_like(m_ref)
        prevq_ref[...] = jnp.zeros_like(prevq_ref)
        ubuf_ref[:, 0:8, :] = jnp.zeros((nb, 8, M_QK_WIDTH), F32)

    ri = lax.broadcasted_iota(jnp.int32, (L, L), 0)
    ci = lax.broadcasted_iota(jnp.int32, (L, L), 1)
    causal_t = ri <= ci
    triu = jnp.where(causal_t, 1.0, 0.0).astype(F32)
    lane = lax.broadcasted_iota(jnp.int32, (8, L), 1)
    r2 = lax.broadcasted_iota(jnp.int32, (2 * L, L), 0)
    c2 = lax.broadcasted_iota(jnp.int32, (2 * L, L), 1)
    shifts = [jnp.where(r2 - c2 == L - k, 1.0, 0.0).astype(BF16) for k in range(1, CONV_WIDTH)]
    ones_rows = jnp.where(lax.broadcasted_iota(jnp.int32, (M_STATE_ROWS - M_V_DIM, L), 0) == 0,
                          1.0, 0.0).astype(BF16)

    pairs = [(bb, h) for bb in range(nb) for h in range(H)]
    states = [state_ref[bb * H + h] for bb, h in pairs]
    m_prevs = [m_ref[bb] for bb in range(nb)]

    def prepare(bb):
        cur = mqt_ref[bb]
        both = jnp.concatenate([prevq_ref[bb], cur], axis=1)
        acc = cbq_ref[...] + cwq_ref[CONV_WIDTH - 1] * cur.astype(F32)
        for k in range(1, CONV_WIDTH):
            acc = acc + cwq_ref[CONV_WIDTH - 1 - k] * jnp.dot(both, shifts[k - 1],
                                                              preferred_element_type=F32)
        qt = (acc * _sigmoid(acc)).astype(BF16)
        ubuf_ref[bb, 8:L + 8, :] = mk_ref[bb].astype(F32)
        acc = cbk_ref[...] + cwk_ref[CONV_WIDTH - 1:CONV_WIDTH, :] * ubuf_ref[bb, 8:L + 8, :]
        for j in range(CONV_WIDTH - 1):
            off = 8 - (CONV_WIDTH - 1) + j
            acc = acc + cwk_ref[j:j + 1, :] * ubuf_ref[bb, off:off + L, :]
        ubuf_ref[bb, 0:8, :] = ubuf_ref[bb, L:L + 8, :]
        kk = (acc * _sigmoid(acc) * (M_QK_DIM ** -0.5)).astype(BF16)

        gr = grow_refs[bb][...] + brow_ref[...]
        b = jnp.dot(_log_sigmoid(gr), triu, preferred_element_type=F32,
                    precision=lax.Precision.HIGHEST)
        gi = pltpu.roll(gr, 4, axis=0)
        u = gi - b
        cm = u
        for sh in (1, 2, 4, 8, 16, 32, 64):
            if sh < L:
                cm = jnp.maximum(cm, jnp.where(lane >= sh, pltpu.roll(cm, sh, axis=1), NEG))
        m_prev = m_prevs[bb]
        mt = b + jnp.maximum(m_prev, cm)
        wa = jnp.exp(b + m_prev - mt)
        emt = jnp.exp(-mt)
        m_new = jnp.broadcast_to(mt[:, L - 1:L], (8, L))
        b_last = jnp.broadcast_to(b[:, L - 1:L], (8, L))
        wc = jnp.exp(b_last + m_prev - m_new)
        ws = jnp.exp(b_last + u - m_new)
        return qt, kk, b - mt, u.T, wa, emt, wc, ws, m_new

    seqs = [prepare(bb) for bb in range(nb)]

    def products(bb, h, state):
        qt, kk = seqs[bb][0], seqs[bb][1]
        qh = qt[h * M_QK_DIM:(h + 1) * M_QK_DIM, :]
        kh = kk[:, h * M_QK_DIM:(h + 1) * M_QK_DIM]
        return (kh, jnp.dot(kh, qh, preferred_element_type=F32),
                jnp.dot(state.astype(BF16), qh, preferred_element_type=F32))

    new_states, outs = [], []
    queue = [products(*pairs[n], states[n]) for n in range(min(M_LOOKAHEAD, len(pairs)))]
    for n, ((bb, h), state) in enumerate(zip(pairs, states)):
        kh, st, inter = queue.pop(0)
        if n + M_LOOKAHEAD < len(pairs):
            queue.append(products(*pairs[n + M_LOOKAHEAD], states[n + M_LOOKAHEAD]))
        _, _, bmt, ucol, wa, emt, wc, ws, _ = seqs[bb]
        row = slice(H + h, H + h + 1)
        wq = jnp.exp(jnp.where(causal_t, bmt[row, :] + ucol[:, H + h:H + h + 1], NEG))
        sw = (st * wq).astype(BF16)
        vext = jnp.concatenate([mvt_ref[bb, h * M_V_DIM:(h + 1) * M_V_DIM, :], ones_rows], axis=0)
        res = wa[row, :] * inter + jnp.dot(vext, sw, preferred_element_type=F32)
        num = res[:M_V_DIM, :]
        den = res[M_V_DIM:M_V_DIM + 1, :]
        hb = num / jnp.maximum(jnp.abs(den), emt[row, :])
        vw = (vext.astype(F32) * ws[row, :]).astype(BF16)
        new_states.append(wc[row, 0:1] * state + jnp.dot(vw, kh, preferred_element_type=F32))
        hn = hb * lax.rsqrt(jnp.mean(hb * hb, axis=0, keepdims=True) + EPS)
        hn = hn * nw_ref[h * M_V_DIM:(h + 1) * M_V_DIM, :]
        og = _sigmoid(mot_ref[bb, h * M_V_DIM:(h + 1) * M_V_DIM, :].astype(F32))
        outs.append((og * hn).T.astype(BF16))

    for i, (bb, h) in enumerate(pairs):
        state_ref[bb * H + h] = new_states[i]
    for bb in range(nb):
        m_ref[bb] = seqs[bb][8]
        prevq_ref[bb] = mqt_ref[bb]
        y_ref[bb] = jnp.concatenate(outs[bb * H:(bb + 1) * H], axis=1)


def _mlstm(mqt, mk3, mvt, mot, grow, conv_w, conv_b, bias_row, norm_w, expert_ws):
    B, S, _ = mk3.shape
    L = M_CHUNK
    nb = max(d for d in range(1, M_BATCH + 1) if B % d == 0)
    nc = S // L
    assert L == LANES, "per-head scalars are kept lane-replicated next to (8, L) gate rows"
    tok = lambda b, c: (b, c, 0)
    chan = lambda b, c: (b, 0, c)
    const2 = lambda b, c: (0, 0)
    const3 = lambda b, c: (0, 0, 0)
    grow_specs = [pl.BlockSpec((8, L), functools.partial(lambda b, c, n: (0, (b * nb + n) * nc + c), n=n))
                  for n in range(nb)]
    rep = lambda v: jnp.broadcast_to(v.astype(F32)[..., None], v.shape + (L,))
    cwq, cbq = rep(conv_w[:, :M_QK_WIDTH]), rep(conv_b[:M_QK_WIDTH])
    cwk, cbk = conv_w[:, M_QK_WIDTH:].astype(F32), conv_b[M_QK_WIDTH:].reshape(1, -1).astype(F32)
    nsteps = (B // nb) * nc
    assert all(w.shape[0] % (8 * nsteps) == 0 for w in expert_ws)
    cast_specs = [pl.BlockSpec((w.shape[0] // nsteps, w.shape[1]), lambda b, c: (b * nc + c, 0))
                  for w in expert_ws]
    return pl.pallas_call(
        functools.partial(_mlstm_kernel, nb),
        grid=(B // nb, nc),
        in_specs=[pl.BlockSpec((nb, M_QK_WIDTH, L), chan),
                  pl.BlockSpec((nb, L, M_QK_WIDTH), tok),
                  pl.BlockSpec((nb, M_V_WIDTH, L), chan),
                  pl.BlockSpec((nb, M_V_WIDTH, L), chan),
                  *grow_specs,
                  pl.BlockSpec((CONV_WIDTH, M_QK_WIDTH, L), const3),
                  pl.BlockSpec((M_QK_WIDTH, L), const2),
                  pl.BlockSpec((CONV_WIDTH, M_QK_WIDTH), const2),
                  pl.BlockSpec((1, M_QK_WIDTH), const2),
                  pl.BlockSpec((8, 1), const2),
                  pl.BlockSpec((M_V_WIDTH, L), const2),
                  *cast_specs],
        out_specs=[pl.BlockSpec((nb, L, M_V_WIDTH), tok), *cast_specs],
        out_shape=[jax.ShapeDtypeStruct((B, S, M_V_WIDTH), BF16),
                   *[jax.ShapeDtypeStruct(w.shape, BF16) for w in expert_ws]],
        scratch_shapes=[pltpu.VMEM((nb * M_HEADS, M_STATE_ROWS, M_QK_DIM), F32),
                        pltpu.VMEM((nb, 8, LANES), F32),
                        pltpu.VMEM((nb, M_QK_WIDTH, L), BF16),
                        pltpu.VMEM((nb, L + 8, M_QK_WIDTH), F32)],
        compiler_params=pltpu.CompilerParams(dimension_semantics=("arbitrary", "arbitrary"),
                                             vmem_limit_bytes=VMEM_LIMIT),
        name="mlstm",
    )(mqt, mk3, mvt, mot, *([grow] * nb), cwq, cbq, cwk, cbk, bias_row, rep(norm_w), *expert_ws)


def _outproj_kernel(x_ref, ya_ref, ym_ref, nmix_ref, wgab_ref, wa_ref, wm_ref, wo_ref, nw_ref, wr_ref, br_ref,
                    x1_ref, h2_ref, meta_ref, code_ref, cnt_ref, cnt_scr):
    tm = x_ref.shape[0]
    i = pl.program_id(0)

    @pl.when(i == 0)
    def _():
        cnt_scr[...] = jnp.zeros_like(cnt_scr)

    pa = jnp.dot(ya_ref[...], wa_ref[...], preferred_element_type=F32)
    pm = jnp.dot(ym_ref[...], wm_ref[...], preferred_element_type=F32)
    x = x_ref[...]
    h1 = _rms(x, nmix_ref[...]).astype(BF16)
    ga = lax.dot_general(h1, wgab_ref[:D_MODEL, :], _NT, preferred_element_type=F32)
    gb = lax.dot_general(h1, wgab_ref[D_MODEL:, :], _NT, preferred_element_type=F32)
    mix = _sigmoid(ga) * pa + _sigmoid(gb) * pm
    x1 = x + jnp.dot(mix.astype(BF16), wo_ref[...], preferred_element_type=F32)
    x1_ref[...] = x1
    h2 = _rms(x1, nw_ref[...])
    h2_ref[...] = _pack_rows(h2)

    nt = (((1,), (1,)), ((), ()))
    h_hi = h2.astype(BF16)
    h_lo = (h2 - h_hi.astype(F32)).astype(BF16)
    hw = jnp.dot(h_hi, wr_ref[...], preferred_element_type=F32)
    logits = (hw[:, :LANES] + hw[:, LANES:]
              + jnp.dot(h_lo, wr_ref[:, :LANES], preferred_element_type=F32)).T[:ROUTER_ROWS]
    logits = logits + br_ref[...]
    row8 = lax.broadcasted_iota(jnp.int32, (8, tm), 0)

    def first_argmax(v):
        mx = jnp.max(v, axis=0, keepdims=True)
        idx = jnp.min(jnp.where(v == mx, row8, 8), axis=0, keepdims=True)
        return mx, idx

    gl = jnp.where(row8 < N_GROUPS, logits[0:8], NEG)
    gmax, gi = first_argmax(gl)
    gp = 1.0 / jnp.sum(jnp.exp(gl - gmax), axis=0, keepdims=True)
    el = jnp.zeros((8, tm), F32)
    for g in range(N_GROUPS):
        el = jnp.where(gi == g, logits[8 + 8 * g:16 + 8 * g], el)
    v1, j1 = first_argmax(el)
    v2, j2 = first_argmax(jnp.where(row8 == j1, NEG, el))
    t = jnp.exp(v2 - v1)
    w1 = gp / (1.0 + t)
    w2 = gp * t / (1.0 + t)
    e1 = gi * EXPERTS_PER_GROUP + j1
    e2 = gi * EXPERTS_PER_GROUP + j2

    erow = lax.broadcasted_iota(jnp.int32, (N_EXPERTS, tm), 0)
    hit1 = erow == e1
    hit2 = erow == e2
    onehot = jnp.where(hit1 | hit2, 1.0, 0.0)
    ri = lax.broadcasted_iota(jnp.int32, (tm, tm), 0)
    ci = lax.broadcasted_iota(jnp.int32, (tm, tm), 1)
    before = jnp.where(ri < ci, 1.0, 0.0).astype(BF16)
    rank = jnp.dot(onehot.astype(BF16), before, preferred_element_type=F32) + cnt_scr[...]
    r1 = jnp.sum(jnp.where(hit1, rank, 0.0), axis=0, keepdims=True)
    r2 = jnp.sum(jnp.where(hit2, rank, 0.0), axis=0, keepdims=True)
    cnt = jnp.broadcast_to((rank + onehot)[:, tm - 1:tm], (N_EXPERTS, tm))
    cnt_scr[...] = cnt
    cnt_ref[...] = cnt[:, :LANES]

    c1 = e1 * RANK_RADIX + r1.astype(jnp.int32)
    c2 = e2 * RANK_RADIX + r2.astype(jnp.int32)
    code_ref[...] = jnp.where(row8 == 0, c1, jnp.where(row8 == 1, c2, 0))
    wrow = lax.broadcasted_iota(jnp.int32, (LANES, tm), 0)
    meta_ref[...] = jnp.where(wrow == 4, w1, jnp.where(wrow == 5, w2, 0.0)).T


def _outproj(x2, ya, ym, norm_mix, w_gab, wa, wm, wo, norm_w, w_rt, b_rt):
    T = x2.shape[0]
    tm = b_rt.shape[1]
    row = lambda i: (i, 0)
    const = lambda i: (0, 0)
    resident = lambda shape: pl.BlockSpec(shape, const, pipeline_mode=pl.Buffered(1))
    return pl.pallas_call(
        _outproj_kernel,
        grid=(T // tm,),
        in_specs=[pl.BlockSpec((tm, D_MODEL), row),
                  pl.BlockSpec((tm, ATT_Q_WIDTH), row),
                  pl.BlockSpec((tm, M_V_WIDTH), row),
                  resident((1, D_MODEL)),
                  resident((2 * D_MODEL, D_MODEL)),
                  resident((ATT_Q_WIDTH, D_MODEL)),
                  resident((M_V_WIDTH, D_MODEL)),
                  resident((D_MODEL, D_MODEL)),
                  resident((1, D_MODEL)),
                  resident((D_MODEL, 2 * LANES)),
                  resident((ROUTER_ROWS, tm))],
        out_specs=[pl.BlockSpec((tm, D_MODEL), row),
                   pl.BlockSpec((tm, PACKED), row),
                   pl.BlockSpec((tm, LANES), row),
                   pl.BlockSpec((8, tm), lambda i: (0, i)),
                   pl.BlockSpec((N_EXPERTS, LANES), const)],
        out_shape=[jax.ShapeDtypeStruct((T, D_MODEL), F32),
                   jax.ShapeDtypeStruct((T, PACKED), jnp.int32),
                   jax.ShapeDtypeStruct((T, LANES), F32),
                   jax.ShapeDtypeStruct((8, T), jnp.int32),
                   jax.ShapeDtypeStruct((N_EXPERTS, LANES), F32)],
        scratch_shapes=[pltpu.VMEM((N_EXPERTS, tm), F32)],
        compiler_params=pltpu.CompilerParams(dimension_semantics=("arbitrary",),
                                             vmem_limit_bytes=VMEM_LIMIT),
        name="outproj",
    )(x2, ya, ym, norm_mix, w_gab, wa, wm, wo, norm_w, w_rt, b_rt)


def _slots_kernel(cnt_ref, pstart_ref, blk_e_ref, nvalid_ref, nused_ref):
    R = SLOT_BLOCK
    nblk = blk_e_ref.shape[0]

    def per_expert(e, run):
        c = cnt_ref[e]
        end = run + ((c + R - 1) // R) * R
        pstart_ref[e] = run

        def set_blk(b, carry):
            blk_e_ref[b] = e
            nvalid_ref[b] = jnp.minimum(run + c - b * R, R)
            return carry
        lax.fori_loop(run // R, end // R, set_blk, 0)
        return end

    total = lax.fori_loop(0, N_EXPERTS, per_expert, 0)
    nused_ref[0] = total // R

    def tail_blk(b, carry):
        blk_e_ref[b] = N_EXPERTS - 1
        nvalid_ref[b] = 0
        return carry
    lax.fori_loop(total // R, nblk, tail_blk, 0)


def _slots(counts, nblk):
    smem = pl.BlockSpec(memory_space=pltpu.SMEM)
    return pl.pallas_call(
        _slots_kernel,
        in_specs=[smem],
        out_specs=[smem, smem, smem, smem],
        out_shape=[jax.ShapeDtypeStruct((N_EXPERTS,), jnp.int32),
                   jax.ShapeDtypeStruct((nblk,), jnp.int32),
                   jax.ShapeDtypeStruct((nblk,), jnp.int32),
                   jax.ShapeDtypeStruct((1,), jnp.int32)],
        name="slots",
    )(counts)


def _experts_kernel(blk_e_ref, nvalid_ref, nused_ref, xs_ref, wg_ref, wu_ref, wd_ref, ys_ref):
    j = pl.program_id(0)
    R = SLOT_BLOCK
    valid = j < nused_ref[0]

    @pl.when(valid)
    def _():
        rows = lax.broadcasted_iota(jnp.int32, (R, PACKED), 0)
        xw = jnp.where(rows < nvalid_ref[j], xs_ref[...], 0)
        xb = _unpack_rows(xw).astype(BF16)
        g = jnp.dot(xb, wg_ref[0], preferred_element_type=F32)
        u = jnp.dot(xb, wu_ref[0], preferred_element_type=F32)
        act = (g * _sigmoid(g) * u).astype(BF16)
        ys_ref[...] = _pack_rows(jnp.dot(act, wd_ref[0], preferred_element_type=F32))


def _experts(blk_e, nvalid, nused, xs, w_gate, w_up, w_down):
    nblk = blk_e.shape[0]
    R = SLOT_BLOCK
    wspec = lambda shape: pl.BlockSpec((1,) + shape, lambda j, be, nv, nu: (be[j], 0, 0))
    grid_spec = pltpu.PrefetchScalarGridSpec(
        num_scalar_prefetch=3,
        grid=(nblk,),
        in_specs=[pl.BlockSpec((R, PACKED), lambda j, be, nv, nu: (jnp.minimum(j, nu[0] - 1), 0)),
                  wspec((D_MODEL, D_EXPERT)),
                  wspec((D_MODEL, D_EXPERT)),
                  wspec((D_EXPERT, D_MODEL))],
        out_specs=pl.BlockSpec((R, PACKED), lambda j, be, nv, nu: (jnp.minimum(j, nu[0] - 1), 0)),
    )
    return pl.pallas_call(
        _experts_kernel,
        grid_spec=grid_spec,
        out_shape=jax.ShapeDtypeStruct((nblk * R, PACKED), jnp.int32),
        compiler_params=pltpu.CompilerParams(dimension_semantics=("arbitrary",),
                                             vmem_limit_bytes=VMEM_LIMIT),
        name="experts",
    )(blk_e, nvalid, nused, xs, w_gate, w_up, w_down)


def _dest_kernel(pstart_ref, code_ref, dest_ref):
    c = code_ref[...]
    e = c >> RANK_BITS
    base = jnp.zeros_like(c)
    for k in range(N_EXPERTS):
        base = jnp.where(e == k, pstart_ref[k], base)
    dest_ref[...] = base + (c & (RANK_RADIX - 1))


def _dests(pstart, code_rows):
    T = code_rows.shape[1]
    tc = min(4096, T)
    return pl.pallas_call(
        _dest_kernel,
        grid=(T // tc,),
        in_specs=[pl.BlockSpec(memory_space=pltpu.SMEM),
                  pl.BlockSpec((8, tc), lambda i: (0, i))],
        out_specs=pl.BlockSpec((8, tc), lambda i: (0, i)),
        out_shape=jax.ShapeDtypeStruct((8, T), jnp.int32),
        name="dests",
    )(pstart, code_rows)


def _sc_workers():
    info = plsc.get_sparse_core_info()
    return info.num_cores, info.num_cores * info.num_subcores


def _sc_scatter_rows(src, idx, n_out):
    n_src, width = src.shape
    nc, nw = _sc_workers()
    per_w = n_src // nw
    ch = SC_CHUNK
    nch = per_w // ch
    assert idx.shape[0] == 2 * n_src and per_w * nw == n_src and nch * ch == per_w and nch % 2 == 0
    mesh = plsc.VectorSubcoreMesh(core_axis_name="c", subcore_axis_name="s")
    dma = pltpu.SemaphoreType.DMA

    @functools.partial(
        pl.kernel, mesh=mesh, out_type=jax.ShapeDtypeStruct((n_out, width), src.dtype),
        scratch_types=[pltpu.VMEM((per_w,), jnp.int32), pltpu.VMEM((per_w,), jnp.int32),
                       pltpu.VMEM((ch, width), src.dtype), pltpu.VMEM((ch, width), src.dtype),
                       dma, dma, dma, dma, dma, dma],
        name="sc_scatter")
    def scatter(src_hbm, idx_hbm, out_hbm, idx_a, idx_b, buf0, buf1, in0, in1, out0a, out0b, out1a, out1b):
        base = (lax.axis_index("s") * nc + lax.axis_index("c")) * per_w
        pltpu.sync_copy(idx_hbm.at[pl.ds(base, per_w)], idx_a)
        pltpu.sync_copy(idx_hbm.at[pl.ds(n_src + base, per_w)], idx_b)

        def read(c, buf, sem):
            return pltpu.make_async_copy(src_hbm.at[pl.ds(base + c * ch, ch)], buf, sem)

        def writes(c, buf, sem_a, sem_b):
            rows = pl.ds(c * ch, ch)
            return (pltpu.make_async_copy(buf, out_hbm.at[idx_a.at[rows]], sem_a),
                    pltpu.make_async_copy(buf, out_hbm.at[idx_b.at[rows]], sem_b))

        def start(copies):
            for cp in copies:
                cp.start()

        def wait(copies):
            for cp in copies:
                cp.wait()

        read(0, buf0, in0).start()

        @pl.loop(0, nch, step=2)
        def _(c):
            @pl.when(c > 0)
            def _():
                wait(writes(c - 1, buf1, out1a, out1b))
            read(c + 1, buf1, in1).start()
            read(c, buf0, in0).wait()
            start(writes(c, buf0, out0a, out0b))
            read(c + 1, buf1, in1).wait()
            start(writes(c + 1, buf1, out1a, out1b))
            wait(writes(c, buf0, out0a, out0b))

            @pl.when(c + 2 < nch)
            def _():
                read(c + 2, buf0, in0).start()

        wait(writes(nch - 1, buf1, out1a, out1b))

    return scatter(src, idx)


def _sc_gather_rows(src, idx):
    M = idx.shape[0]
    width = src.shape[1]
    nc, nw = _sc_workers()
    per_w = M // nw
    ch = SC_CHUNK
    nch = per_w // ch
    assert per_w * nw == M and nch * ch == per_w and nch % 2 == 0
    mesh = plsc.VectorSubcoreMesh(core_axis_name="c", subcore_axis_name="s")
    dma = pltpu.SemaphoreType.DMA

    @functools.partial(
        pl.kernel, mesh=mesh, out_type=jax.ShapeDtypeStruct((M, width), src.dtype),
        scratch_types=[pltpu.VMEM((per_w,), jnp.int32),
                       pltpu.VMEM((ch, width), src.dtype), pltpu.VMEM((ch, width), src.dtype),
                       dma, dma, dma, dma],
        name="sc_gather")
    def gather(src_hbm, idx_hbm, out_hbm, idx_v, buf0, buf1, in0, in1, out0, out1):
        base = (lax.axis_index("s") * nc + lax.axis_index("c")) * per_w
        pltpu.sync_copy(idx_hbm.at[pl.ds(base, per_w)], idx_v)

        def read(c, buf, sem):
            return pltpu.make_async_copy(src_hbm.at[idx_v.at[pl.ds(c * ch, ch)]], buf, sem)

        def write(c, buf, sem):
            return pltpu.make_async_copy(buf, out_hbm.at[pl.ds(base + c * ch, ch)], sem)

        read(0, buf0, in0).start()

        @pl.loop(0, nch, step=2)
        def _(c):
            @pl.when(c > 0)
            def _():
                write(c - 1, buf1, out1).wait()
            read(c + 1, buf1, in1).start()
            read(c, buf0, in0).wait()
            write(c, buf0, out0).start()
            read(c + 1, buf1, in1).wait()
            write(c + 1, buf1, out1).start()
            write(c, buf0, out0).wait()

            @pl.when(c + 2 < nch)
            def _():
                read(c + 2, buf0, in0).start()

        write(nch - 1, buf1, out1).wait()

    return gather(src, idx)


def _final_kernel(x1_ref, y1_ref, y2_ref, meta_ref, nw_ref, o_ref):
    w1 = meta_ref[:, 4:5]
    w2 = meta_ref[:, 5:6]
    x2 = x1_ref[...] + (w1 * _unpack_rows(y1_ref[...]) + w2 * _unpack_rows(y2_ref[...]))
    o_ref[...] = _rms(x2, nw_ref[...])


def _final(x1, yg, meta, norm_w):
    T = x1.shape[0]
    tm = min(COMB_TILE, T)
    nt = T // tm
    row = lambda i: (i, 0)
    return pl.pallas_call(
        _final_kernel,
        grid=(nt,),
        in_specs=[pl.BlockSpec((tm, D_MODEL), row),
                  pl.BlockSpec((tm, PACKED), row),
                  pl.BlockSpec((tm, PACKED), lambda i: (i + nt, 0)),
                  pl.BlockSpec((tm, LANES), row),
                  pl.BlockSpec((1, D_MODEL), lambda i: (0, 0))],
        out_specs=pl.BlockSpec((tm, D_MODEL), row),
        out_shape=jax.ShapeDtypeStruct((T, D_MODEL), F32),
        compiler_params=pltpu.CompilerParams(dimension_semantics=("arbitrary",),
                                             vmem_limit_bytes=VMEM_LIMIT),
        name="final",
    )(x1, yg, yg, meta, norm_w)


def _layer(x, norm_mix_w, w_in, conv_w, conv_b, b_igate, b_fgate, attn_sinks, mlstm_norm_w,
           w_attn_o, w_mlstm_o, w_out, norm_ffn_w, w_group, b_group, w_router, b_router,
           w_gate, w_up, w_down, out_norm_w):
    B, S, D = x.shape
    T = B * S
    x2 = x.reshape(T, D)

    wt = w_in.T
    w_tok = jnp.concatenate([wt[_O_AK:_O_AV], wt[_O_MK:_O_MV]], axis=0).astype(BF16)
    w_chan = jnp.concatenate([wt[_O_AQ:_O_AK] * (HEAD_DIM ** -0.5), wt[_O_AV:_O_MQ], wt[_O_MQ:_O_MK],
                              wt[_O_MV:_O_MI],
                              jnp.pad(wt[_O_MI:_O_GA], ((0, GATE_ROWS - 2 * M_HEADS), (0, 0)))],
                             axis=0).astype(BF16)
    ak, mk, aqt, avt, mqt, mvt, mot, grow = _inproj(x2, norm_mix_w.reshape(1, D), w_tok, w_chan, B, S)

    ya = _attn(attn_sinks.astype(F32), aqt, ak.reshape(B, S, -1), avt)

    bias = jnp.concatenate([b_igate, b_fgate]).astype(F32)
    expert_ws = [w.reshape(-1, w.shape[-1]) for w in (w_gate, w_up, w_down)]
    ym, wg16, wu16, wd16 = _mlstm(mqt, mk.reshape(B, S, -1), mvt, mot, grow, conv_w, conv_b,
                                  bias.reshape(2 * M_HEADS, 1), mlstm_norm_w, expert_ws)

    gpad = jnp.zeros((8 - N_GROUPS, D), F32)
    w_rt = jnp.concatenate([w_group.T, gpad, w_router.T, jnp.zeros((LANES - ROUTER_ROWS, D), F32)], axis=0)
    w_rt_hi = w_rt.astype(BF16)
    w_rt = jnp.concatenate([w_rt_hi, (w_rt - w_rt_hi.astype(F32)).astype(BF16)], axis=0).T
    b_rt = jnp.concatenate([b_group, jnp.zeros((8 - N_GROUPS,), F32), b_router]).astype(F32)
    b_rt = jnp.broadcast_to(b_rt[:, None], (ROUTER_ROWS, min(TOK_TILE, T)))
    x1, h2p, meta, code_rows, cnt = _outproj(x2, ya.reshape(T, -1), ym.reshape(T, -1),
                                            norm_mix_w.reshape(1, D), wt[_O_GA:].astype(BF16),
                                            w_attn_o.astype(BF16), w_mlstm_o.astype(BF16),
                                            w_out.astype(BF16), norm_ffn_w.reshape(1, D), w_rt, b_rt)

    counts = cnt[:, 0].astype(jnp.int32)
    nblk = (2 * T) // SLOT_BLOCK + N_EXPERTS
    pstart, blk_e, nvalid, nused = _slots(counts, nblk)
    dest = _dests(pstart, code_rows)[0:2].reshape(2 * T)

    xs = _sc_scatter_rows(h2p, dest, nblk * SLOT_BLOCK)
    ys = _experts(blk_e, nvalid, nused, xs, wg16.reshape(w_gate.shape), wu16.reshape(w_up.shape),
                  wd16.reshape(w_down.shape))
    yg = _sc_gather_rows(ys, dest)
    out = _final(x1, yg, meta, out_norm_w.reshape(1, D))
    return out.reshape(B, S, D)


def kernel(x, norm_mix_w, w_in, conv_w, conv_b, b_igate, b_fgate, attn_sinks, mlstm_norm_w, w_attn_o,
           w_mlstm_o, w_out, norm_ffn_w, w_group, b_group, w_router, b_router, w_gate, w_up, w_down,
           norm_final_w):
    depth = w_in.shape[0]
    assert depth == 1, "final RMSNorm is fused into the last layer's combine kernel"
    return _layer(x, norm_mix_w[0], w_in[0], conv_w[0], conv_b[0], b_igate[0], b_fgate[0],
                  attn_sinks[0], mlstm_norm_w[0], w_attn_o[0], w_mlstm_o[0], w_out[0], norm_ffn_w[0],
                  w_group[0], b_group[0], w_router[0], b_router[0], w_gate[0], w_up[0], w_down[0],
                  norm_final_w)
```

```python
import functools

import jax
import jax.numpy as jnp
from jax import lax
from jax.experimental import pallas as pl
from jax.experimental.pallas import tpu as pltpu
from jax.experimental.pallas import tpu_sc as plsc

F32 = jnp.float32
BF16 = jnp.bfloat16

D_MODEL = 1024
N_Q_HEADS = 8
N_KV_HEADS = 2
HEAD_DIM = 64
WINDOW = 128
GQA_GROUP = N_Q_HEADS // N_KV_HEADS
M_HEADS = 4
M_QK_DIM = 64
M_V_DIM = 128
CONV_WIDTH = 4
N_GROUPS = 4
EXPERTS_PER_GROUP = 8
N_EXPERTS = N_GROUPS * EXPERTS_PER_GROUP
D_EXPERT = 512
EPS = 1e-6

ATT_Q_WIDTH = N_Q_HEADS * HEAD_DIM
ATT_KV_WIDTH = N_KV_HEADS * HEAD_DIM
M_QK_WIDTH = M_HEADS * M_QK_DIM
M_V_WIDTH = M_HEADS * M_V_DIM

LANES = 128
NEG = -1e30
V7X_VMEM_BYTES = 64 * 1024 * 1024
VMEM_LIMIT = V7X_VMEM_BYTES - 8 * 1024 * 1024

IN_TILE = 1024
TOK_TILE = 1024
ATT_TILE = 2048
LOOKAHEAD = 2
M_LOOKAHEAD = 4
M_CHUNK = 128
M_BATCH = 8
M_STATE_ROWS = M_V_DIM + 16
SLOT_BLOCK = 512
COMB_TILE = 1024
RANK_RADIX = 65536
RANK_BITS = 16
ROUTER_ROWS = 8 + N_EXPERTS
PACKED = D_MODEL // 2
SC_CHUNK = 64

_O_AQ = 0
_O_AK = _O_AQ + ATT_Q_WIDTH
_O_AV = _O_AK + ATT_KV_WIDTH
_O_MQ = _O_AV + ATT_KV_WIDTH
_O_MK = _O_MQ + M_QK_WIDTH
_O_MV = _O_MK + M_QK_WIDTH
_O_MO = _O_MV + M_V_WIDTH
_O_MI = _O_MO + M_V_WIDTH
_O_MF = _O_MI + M_HEADS
_O_GA = _O_MF + M_HEADS


def _rms(x, w):
    return x * lax.rsqrt(jnp.mean(x * x, axis=-1, keepdims=True) + EPS) * w


def _sigmoid(x):
    return 0.5 * jnp.tanh(0.5 * x) + 0.5


def _pack_rows(x):
    half = x.shape[1] // 2
    bits = lax.bitcast_convert_type(x.astype(BF16).astype(F32), jnp.uint32)
    packed = (bits[:, :half] >> 16) | (bits[:, half:] & jnp.uint32(0xFFFF0000))
    return lax.bitcast_convert_type(packed, jnp.int32)


def _unpack_rows(w):
    u = lax.bitcast_convert_type(w, jnp.uint32)
    lo = lax.bitcast_convert_type(u << 16, F32)
    hi = lax.bitcast_convert_type(u & jnp.uint32(0xFFFF0000), F32)
    return jnp.concatenate([lo, hi], axis=-1)


def _log_sigmoid(x):
    return jnp.minimum(x, 0.0) - jnp.log1p(jnp.exp(-jnp.abs(x)))


_TOK_SPLITS = (ATT_KV_WIDTH, M_QK_WIDTH)
_CHAN_SPLITS = (ATT_Q_WIDTH, ATT_KV_WIDTH, M_QK_WIDTH, M_V_WIDTH, M_V_WIDTH)
GATE_ROWS = 16
_NT = (((1,), (1,)), ((), ()))


def _inproj_kernel(x_ref, nw_ref, w_ref, wt_ref, *out_refs):
    tok_refs = out_refs[:len(_TOK_SPLITS)]
    chan_refs = out_refs[len(_TOK_SPLITS):len(_TOK_SPLITS) + len(_CHAN_SPLITS)]
    grow_ref = out_refs[-1]
    h = _rms(x_ref[...], nw_ref[...]).astype(BF16)
    lo = 0
    for ref, width in zip(tok_refs, _TOK_SPLITS):
        ref[...] = lax.dot_general(h, w_ref[lo:lo + width, :], _NT, preferred_element_type=F32).astype(BF16)
        lo += width
    lo = 0
    for ref, width in zip(chan_refs, _CHAN_SPLITS):
        ref[0] = lax.dot_general(wt_ref[lo:lo + width, :], h, (((1,), (1,)), ((), ())),
                                 preferred_element_type=F32).astype(BF16)
        lo += width
    grow_ref[...] = lax.dot_general(wt_ref[lo:lo + GATE_ROWS, :], h, (((1,), (1,)), ((), ())),
                                    preferred_element_type=F32)[0:2 * M_HEADS, :]


def _inproj(x2, norm_w, w_tok, w_chan, B, S):
    T = x2.shape[0]
    tm = min(IN_TILE, S)
    tpb = S // tm
    nsteps = T // tm
    row = lambda i: (i, 0)
    const = lambda i: (0, 0)
    chan = lambda i: (i // tpb, 0, i % tpb)
    return pl.pallas_call(
        _inproj_kernel,
        grid=(nsteps,),
        in_specs=[pl.BlockSpec((tm, D_MODEL), row),
                  pl.BlockSpec((1, D_MODEL), const),
                  pl.BlockSpec(w_tok.shape, const),
                  pl.BlockSpec(w_chan.shape, const)],
        out_specs=[*[pl.BlockSpec((tm, w), row) for w in _TOK_SPLITS],
                   *[pl.BlockSpec((1, w, tm), chan) for w in _CHAN_SPLITS],
                   pl.BlockSpec((8, tm), lambda i: (0, i))],
        out_shape=[*[jax.ShapeDtypeStruct((T, w), BF16) for w in _TOK_SPLITS],
                   *[jax.ShapeDtypeStruct((B, w, S), BF16) for w in _CHAN_SPLITS],
                   jax.ShapeDtypeStruct((8, T), F32)],
        compiler_params=pltpu.CompilerParams(dimension_semantics=("arbitrary",),
                                             vmem_limit_bytes=VMEM_LIMIT),
        name="inproj",
    )(x2, norm_w, w_tok, w_chan)


def _attn_kernel(tq, sink_ref, qt_ref, k_ref, kp_ref, vt_ref, vtp_ref, o_ref):
    i = pl.program_id(1)
    W = WINDOW
    G = GQA_GROUP
    u = lax.broadcasted_iota(jnp.int32, (W, W), 0)
    t = lax.broadcasted_iota(jnp.int32, (W, W), 1)
    from_prev = u > t

    def keys_values(j):
        if j == 0:
            return kp_ref[0], vtp_ref[0], k_ref[0, 0:W, :], vt_ref[0, :, 0:W]
        return (k_ref[0, (j - 1) * W:j * W, :], vt_ref[0, :, (j - 1) * W:j * W],
                k_ref[0, j * W:(j + 1) * W, :], vt_ref[0, :, j * W:(j + 1) * W])

    def scores(j, g):
        k_prev, _, k_cur, _ = keys_values(j)
        dims = slice(g * HEAD_DIM, (g + 1) * HEAD_DIM)
        qg = jnp.concatenate([qt_ref[0, h * HEAD_DIM:(h + 1) * HEAD_DIM, j * W:(j + 1) * W]
                              for h in range(g * G, (g + 1) * G)], axis=1)
        return (jnp.dot(k_prev[:, dims], qg, preferred_element_type=F32),
                jnp.dot(k_cur[:, dims], qg, preferred_element_type=F32))

    tasks = [(j, g) for j in range(tq // W) for g in range(N_KV_HEADS)]
    queue = [scores(*t) for t in tasks[:LOOKAHEAD]]
    pair_rows = []
    for n_task, (j, g) in enumerate(tasks):
        s_prev, s_cur = queue.pop(0)
        if n_task + LOOKAHEAD < len(tasks):
            queue.append(scores(*tasks[n_task + LOOKAHEAD]))
        cols = slice(j * W, (j + 1) * W)
        _, vt_prev, _, vt_cur = keys_values(j)
        dims = slice(g * HEAD_DIM, (g + 1) * HEAD_DIM)
        for n in range(G):
            hc = slice(n * W, (n + 1) * W)
            sp = s_prev[:, hc]
            if j == 0:
                sp = jnp.where(i > 0, sp, NEG)
            s = jnp.where(from_prev, sp, s_cur[:, hc])
            sink = sink_ref[g * G + n]
            m = jnp.maximum(jnp.max(s, axis=0, keepdims=True), sink)
            p = jnp.exp(s - m)
            inv = 1.0 / (jnp.sum(p, axis=0, keepdims=True) + jnp.exp(sink - m))
            p_prev = jnp.where(from_prev, p, 0.0).astype(BF16)
            p_cur = jnp.where(from_prev, 0.0, p).astype(BF16)
            o = jnp.dot(vt_prev[dims, :], p_prev, preferred_element_type=F32) \
                + jnp.dot(vt_cur[dims, :], p_cur, preferred_element_type=F32)
            pair_rows.append(o * inv)
        if g == N_KV_HEADS - 1:
            out = [jnp.concatenate(pair_rows[n:n + 2], axis=0).T for n in range(0, N_Q_HEADS, 2)]
            o_ref[0, cols, :] = jnp.concatenate(out, axis=1).astype(BF16)
            pair_rows = []


def _attn(sinks, qt, k3, vt):
    B, S, _ = k3.shape
    tq = min(ATT_TILE, S)
    per = tq // WINDOW
    chan = lambda b, i: (b, 0, i)
    tok = lambda b, i: (b, i, 0)
    return pl.pallas_call(
        functools.partial(_attn_kernel, tq),
        grid=(B, S // tq),
        in_specs=[pl.BlockSpec(memory_space=pltpu.SMEM),
                  pl.BlockSpec((1, ATT_Q_WIDTH, tq), chan),
                  pl.BlockSpec((1, tq, ATT_KV_WIDTH), tok),
                  pl.BlockSpec((1, WINDOW, ATT_KV_WIDTH), lambda b, i: (b, jnp.maximum(i * per - 1, 0), 0)),
                  pl.BlockSpec((1, ATT_KV_WIDTH, tq), chan),
                  pl.BlockSpec((1, ATT_KV_WIDTH, WINDOW), lambda b, i: (b, 0, jnp.maximum(i * per - 1, 0)))],
        out_specs=pl.BlockSpec((1, tq, ATT_Q_WIDTH), tok),
        out_shape=jax.ShapeDtypeStruct((B, S, ATT_Q_WIDTH), BF16),
        compiler_params=pltpu.CompilerParams(dimension_semantics=("arbitrary", "arbitrary"),
                                             vmem_limit_bytes=VMEM_LIMIT),
        name="attn",
    )(sinks, qt, k3, k3, vt, vt)


def _mlstm_kernel(nb, *refs):
    L = M_CHUNK
    H = M_HEADS
    mqt_ref, mk_ref, mvt_ref, mot_ref = refs[:4]
    grow_refs = refs[4:4 + nb]
    (cwq_ref, cbq_ref, cwk_ref, cbk_ref, brow_ref, nw_ref, eg_ref, eu_ref, ed_ref,
     y_ref, eg16_ref, eu16_ref, ed16_ref, state_ref, m_ref, prevq_ref, ubuf_ref) = refs[4 + nb:]
    cidx = pl.program_id(1)
    for src, dst in ((eg_ref, eg16_ref), (eu_ref, eu16_ref), (ed_ref, ed16_ref)):
        dst[...] = src[...].astype(BF16)

    @pl.when(cidx == 0)
    def _():
        state_ref[...] = jnp.zeros_like(state_ref)
        m_ref[...] = jnp.zeros_like(m_ref)
        prevq_ref[...] = jnp.zeros_like(prevq_ref)
        ubuf_ref[:, 0:8, :] = jnp.zeros((nb, 8, M_QK_WIDTH), F32)

    ri = lax.broadcasted_iota(jnp.int32, (L, L), 0)
    ci = lax.broadcasted_iota(jnp.int32, (L, L), 1)
    causal_t = ri <= ci
    triu = jnp.where(causal_t, 1.0, 0.0).astype(F32)
    lane = lax.broadcasted_iota(jnp.int32, (8, L), 1)
    r2 = lax.broadcasted_iota(jnp.int32, (2 * L, L), 0)
    c2 = lax.broadcasted_iota(jnp.int32, (2 * L, L), 1)
    shifts = [jnp.where(r2 - c2 == L - k, 1.0, 0.0).astype(BF16) for k in range(1, CONV_WIDTH)]
    ones_rows = jnp.where(lax.broadcasted_iota(jnp.int32, (M_STATE_ROWS - M_V_DIM, L), 0) == 0,
                          1.0, 0.0).astype(BF16)

    pairs = [(bb, h) for bb in range(nb) for h in range(H)]
    states = [state_ref[bb * H + h] for bb, h in pairs]
    m_prevs = [m_ref[bb] for bb in range(nb)]

    def prepare(bb):
        cur = mqt_ref[bb]
        both = jnp.concatenate([prevq_ref[bb], cur], axis=1)
        acc = cbq_ref[...] + cwq_ref[CONV_WIDTH - 1] * cur.astype(F32)
        for k in range(1, CONV_WIDTH):
            acc = acc + cwq_ref[CONV_WIDTH - 1 - k] * jnp.dot(both, shifts[k - 1],
                                                              preferred_element_type=F32)
        qt = (acc * _sigmoid(acc)).astype(BF16)
        ubuf_ref[bb, 8:L + 8, :] = mk_ref[bb].astype(F32)
        acc = cbk_ref[...] + cwk_ref[CONV_WIDTH - 1:CONV_WIDTH, :] * ubuf_ref[bb, 8:L + 8, :]
        for j in range(CONV_WIDTH - 1):
            off = 8 - (CONV_WIDTH - 1) + j
            acc = acc + cwk_ref[j:j + 1, :] * ubuf_ref[bb, off:off + L, :]
        ubuf_ref[bb, 0:8, :] = ubuf_ref[bb, L:L + 8, :]
        kk = (acc * _sigmoid(acc) * (M_QK_DIM ** -0.5)).astype(BF16)

        gr = grow_refs[bb][...] + brow_ref[...]
        b = jnp.dot(_log_sigmoid(gr), triu, preferred_element_type=F32,
                    precision=lax.Precision.HIGHEST)
        gi = pltpu.roll(gr, 4, axis=0)
        u = gi - b
        cm = u
        for sh in (1, 2, 4, 8, 16, 32, 64):
            if sh < L:
                cm = jnp.maximum(cm, jnp.where(lane >= sh, pltpu.roll(cm, sh, axis=1), NEG))
        m_prev = m_prevs[bb]
        mt = b + jnp.maximum(m_prev, cm)
        wa = jnp.exp(b + m_prev - mt)
        emt = jnp.exp(-mt)
        m_new = jnp.broadcast_to(mt[:, L - 1:L], (8, L))
        b_last = jnp.broadcast_to(b[:, L - 1:L], (8, L))
        wc = jnp.exp(b_last + m_prev - m_new)
        ws = jnp.exp(b_last + u - m_new)
        return qt, kk, b - mt, u.T, wa, emt, wc, ws, m_new

    seqs = [prepare(bb) for bb in range(nb)]

    def products(bb, h, state):
        qt, kk = seqs[bb][0], seqs[bb][1]
        qh = qt[h * M_QK_DIM:(h + 1) * M_QK_DIM, :]
        kh = kk[:, h * M_QK_DIM:(h + 1) * M_QK_DIM]
        return (kh, jnp.dot(kh, qh, preferred_element_type=F32),
                jnp.dot(state.astype(BF16), qh, preferred_element_type=F32))

    new_states, outs = [], []
    queue = [products(*pairs[n], states[n]) for n in range(min(M_LOOKAHEAD, len(pairs)))]
    for n, ((bb, h), state) in enumerate(zip(pairs, states)):
        kh, st, inter = queue.pop(0)
        if n + M_LOOKAHEAD < len(pairs):
            queue.append(products(*pairs[n + M_LOOKAHEAD], states[n + M_LOOKAHEAD]))
        _, _, bmt, ucol, wa, emt, wc, ws, _ = seqs[bb]
        row = slice(H + h, H + h + 1)
        wq = jnp.exp(jnp.where(causal_t, bmt[row, :] + ucol[:, H + h:H + h + 1], NEG))
        sw = (st * wq).astype(BF16)
        vext = jnp.concatenate([mvt_ref[bb, h * M_V_DIM:(h + 1) * M_V_DIM, :], ones_rows], axis=0)
        res = wa[row, :] * inter + jnp.dot(vext, sw, preferred_element_type=F32)
        num = res[:M_V_DIM, :]
        den = res[M_V_DIM:M_V_DIM + 1, :]
        hb = num / jnp.maximum(jnp.abs(den), emt[row, :])
        vw = (vext.astype(F32) * ws[row, :]).astype(BF16)
        new_states.append(wc[row, 0:1] * state + jnp.dot(vw, kh, preferred_element_type=F32))
        hn = hb * lax.rsqrt(jnp.mean(hb * hb, axis=0, keepdims=True) + EPS)
        hn = hn * nw_ref[h * M_V_DIM:(h + 1) * M_V_DIM, :]
        og = _sigmoid(mot_ref[bb, h * M_V_DIM:(h + 1) * M_V_DIM, :].astype(F32))
        outs.append((og * hn).T.astype(BF16))

    for i, (bb, h) in enumerate(pairs):
        state_ref[bb * H + h] = new_states[i]
    for bb in range(nb):
        m_ref[bb] = seqs[bb][8]
        prevq_ref[bb] = mqt_ref[bb]
        y_ref[bb] = jnp.concatenate(outs[bb * H:(bb + 1) * H], axis=1)


def _mlstm(mqt, mk3, mvt, mot, grow, conv_w, conv_b, bias_row, norm_w, expert_ws):
    B, S, _ = mk3.shape
    L = M_CHUNK
    nb = max(d for d in range(1, M_BATCH + 1) if B % d == 0)
    nc = S // L
    assert L == LANES, "per-head scalars are kept lane-replicated next to (8, L) gate rows"
    tok = lambda b, c: (b, c, 0)
    chan = lambda b, c: (b, 0, c)
    const2 = lambda b, c: (0, 0)
    const3 = lambda b, c: (0, 0, 0)
    grow_specs = [pl.BlockSpec((8, L), functools.partial(lambda b, c, n: (0, (b * nb + n) * nc + c), n=n))
                  for n in range(nb)]
    rep = lambda v: jnp.broadcast_to(v.astype(F32)[..., None], v.shape + (L,))
    cwq, cbq = rep(conv_w[:, :M_QK_WIDTH]), rep(conv_b[:M_QK_WIDTH])
    cwk, cbk = conv_w[:, M_QK_WIDTH:].astype(F32), conv_b[M_QK_WIDTH:].reshape(1, -1).astype(F32)
    nsteps = (B // nb) * nc
    assert all(w.shape[0] % (8 * nsteps) == 0 for w in expert_ws)
    cast_specs = [pl.BlockSpec((w.shape[0] // nsteps, w.shape[1]), lambda b, c: (b * nc + c, 0))
                  for w in expert_ws]
    return pl.pallas_call(
        functools.partial(_mlstm_kernel, nb),
        grid=(B // nb, nc),
        in_specs=[pl.BlockSpec((nb, M_QK_WIDTH, L), chan),
                  pl.BlockSpec((nb, L, M_QK_WIDTH), tok),
                  pl.BlockSpec((nb, M_V_WIDTH, L), chan),
                  pl.BlockSpec((nb, M_V_WIDTH, L), chan),
                  *grow_specs,
                  pl.BlockSpec((CONV_WIDTH, M_QK_WIDTH, L), const3),
                  pl.BlockSpec((M_QK_WIDTH, L), const2),
                  pl.BlockSpec((CONV_WIDTH, M_QK_WIDTH), const2),
                  pl.BlockSpec((1, M_QK_WIDTH), const2),
                  pl.BlockSpec((8, 1), const2),
                  pl.BlockSpec((M_V_WIDTH, L), const2),
                  *cast_specs],
        out_specs=[pl.BlockSpec((nb, L, M_V_WIDTH), tok), *cast_specs],
        out_shape=[jax.ShapeDtypeStruct((B, S, M_V_WIDTH), BF16),
                   *[jax.ShapeDtypeStruct(w.shape, BF16) for w in expert_ws]],
        scratch_shapes=[pltpu.VMEM((nb * M_HEADS, M_STATE_ROWS, M_QK_DIM), F32),
                        pltpu.VMEM((nb, 8, LANES), F32),
                        pltpu.VMEM((nb, M_QK_WIDTH, L), BF16),
                        pltpu.VMEM((nb, L + 8, M_QK_WIDTH), F32)],
        compiler_params=pltpu.CompilerParams(dimension_semantics=("arbitrary", "arbitrary"),
                                             vmem_limit_bytes=VMEM_LIMIT),
        name="mlstm",
    )(mqt, mk3, mvt, mot, *([grow] * nb), cwq, cbq, cwk, cbk, bias_row, rep(norm_w), *expert_ws)


def _outproj_kernel(x_ref, ya_ref, ym_ref, nmix_ref, wgab_ref, wa_ref, wm_ref, wo_ref, nw_ref, wr_ref, br_ref,
                    x1_ref, h2_ref, meta_ref, code_ref, cnt_ref, cnt_scr):
    tm = x_ref.shape[0]
    i = pl.program_id(0)

    @pl.when(i == 0)
    def _():
        cnt_scr[...] = jnp.zeros_like(cnt_scr)

    pa = jnp.dot(ya_ref[...], wa_ref[...], preferred_element_type=F32)
    pm = jnp.dot(ym_ref[...], wm_ref[...], preferred_element_type=F32)
    x = x_ref[...]
    h1 = _rms(x, nmix_ref[...]).astype(BF16)
    ga = lax.dot_general(h1, wgab_ref[:D_MODEL, :], _NT, preferred_element_type=F32)
    gb = lax.dot_general(h1, wgab_ref[D_MODEL:, :], _NT, preferred_element_type=F32)
    mix = _sigmoid(ga) * pa + _sigmoid(gb) * pm
    x1 = x + jnp.dot(mix.astype(BF16), wo_ref[...], preferred_element_type=F32)
    x1_ref[...] = x1
    h2 = _rms(x1, nw_ref[...])
    h2_ref[...] = _pack_rows(h2)

    nt = (((1,), (1,)), ((), ()))
    h_hi = h2.astype(BF16)
    h_lo = (h2 - h_hi.astype(F32)).astype(BF16)
    hw = jnp.dot(h_hi, wr_ref[...], preferred_element_type=F32)
    logits = (hw[:, :LANES] + hw[:, LANES:]
              + jnp.dot(h_lo, wr_ref[:, :LANES], preferred_element_type=F32)).T[:ROUTER_ROWS]
    logits = logits + br_ref[...]
    row8 = lax.broadcasted_iota(jnp.int32, (8, tm), 0)

    def first_argmax(v):
        mx = jnp.max(v, axis=0, keepdims=True)
        idx = jnp.min(jnp.where(v == mx, row8, 8), axis=0, keepdims=True)
        return mx, idx

    gl = jnp.where(row8 < N_GROUPS, logits[0:8], NEG)
    gmax, gi = first_argmax(gl)
    gp = 1.0 / jnp.sum(jnp.exp(gl - gmax), axis=0, keepdims=True)
    el = jnp.zeros((8, tm), F32)
    for g in range(N_GROUPS):
        el = jnp.where(gi == g, logits[8 + 8 * g:16 + 8 * g], el)
    v1, j1 = first_argmax(el)
    v2, j2 = first_argmax(jnp.where(row8 == j1, NEG, el))
    t = jnp.exp(v2 - v1)
    w1 = gp / (1.0 + t)
    w2 = gp * t / (1.0 + t)
    e1 = gi * EXPERTS_PER_GROUP + j1
    e2 = gi * EXPERTS_PER_GROUP + j2

    erow = lax.broadcasted_iota(jnp.int32, (N_EXPERTS, tm), 0)
    hit1 = erow == e1
    hit2 = erow == e2
    onehot = jnp.where(hit1 | hit2, 1.0, 0.0)
    ri = lax.broadcasted_iota(jnp.int32, (tm, tm), 0)
    ci = lax.broadcasted_iota(jnp.int32, (tm, tm), 1)
    before = jnp.where(ri < ci, 1.0, 0.0).astype(BF16)
    rank = jnp.dot(onehot.astype(BF16), before, preferred_element_type=F32) + cnt_scr[...]
    r1 = jnp.sum(jnp.where(hit1, rank, 0.0), axis=0, keepdims=True)
    r2 = jnp.sum(jnp.where(hit2, rank, 0.0), axis=0, keepdims=True)
    cnt = jnp.broadcast_to((rank + onehot)[:, tm - 1:tm], (N_EXPERTS, tm))
    cnt_scr[...] = cnt
    cnt_ref[...] = cnt[:, :LANES]

    c1 = e1 * RANK_RADIX + r1.astype(jnp.int32)
    c2 = e2 * RANK_RADIX + r2.astype(jnp.int32)
    code_ref[...] = jnp.where(row8 == 0, c1, jnp.where(row8 == 1, c2, 0))
    meta_ref[...] = jnp.where(row8 == 0, w1, jnp.where(row8 == 1, w2, 0.0))


def _outproj(x2, ya, ym, norm_mix, w_gab, wa, wm, wo, norm_w, w_rt, b_rt):
    T = x2.shape[0]
    tm = b_rt.shape[1]
    row = lambda i: (i, 0)
    const = lambda i: (0, 0)
    resident = lambda shape: pl.BlockSpec(shape, const, pipeline_mode=pl.Buffered(1))
    return pl.pallas_call(
        _outproj_kernel,
        grid=(T // tm,),
        in_specs=[pl.BlockSpec((tm, D_MODEL), row),
                  pl.BlockSpec((tm, ATT_Q_WIDTH), row),
                  pl.BlockSpec((tm, M_V_WIDTH), row),
                  resident((1, D_MODEL)),
                  resident((2 * D_MODEL, D_MODEL)),
                  resident((ATT_Q_WIDTH, D_MODEL)),
                  resident((M_V_WIDTH, D_MODEL)),
                  resident((D_MODEL, D_MODEL)),
                  resident((1, D_MODEL)),
                  resident((D_MODEL, 2 * LANES)),
                  resident((ROUTER_ROWS, tm))],
        out_specs=[pl.BlockSpec((tm, D_MODEL), row),
                   pl.BlockSpec((tm, PACKED), row),
                   pl.BlockSpec((8, tm), lambda i: (0, i)),
                   pl.BlockSpec((8, tm), lambda i: (0, i)),
                   pl.BlockSpec((N_EXPERTS, LANES), const)],
        out_shape=[jax.ShapeDtypeStruct((T, D_MODEL), F32),
                   jax.ShapeDtypeStruct((T, PACKED), jnp.int32),
                   jax.ShapeDtypeStruct((8, T), F32),
                   jax.ShapeDtypeStruct((8, T), jnp.int32),
                   jax.ShapeDtypeStruct((N_EXPERTS, LANES), F32)],
        scratch_shapes=[pltpu.VMEM((N_EXPERTS, tm), F32)],
        compiler_params=pltpu.CompilerParams(dimension_semantics=("arbitrary",),
                                             vmem_limit_bytes=VMEM_LIMIT),
        name="outproj",
    )(x2, ya, ym, norm_mix, w_gab, wa, wm, wo, norm_w, w_rt, b_rt)


def _slots_kernel(cnt_ref, pstart_ref, blk_e_ref, nvalid_ref, nused_ref):
    R = SLOT_BLOCK
    nblk = blk_e_ref.shape[0]

    def per_expert(e, run):
        c = cnt_ref[e]
        end = run + ((c + R - 1) // R) * R
        pstart_ref[e] = run

        def set_blk(b, carry):
            blk_e_ref[b] = e
            nvalid_ref[b] = jnp.minimum(run + c - b * R, R)
            return carry
        lax.fori_loop(run // R, end // R, set_blk, 0)
        return end

    total = lax.fori_loop(0, N_EXPERTS, per_expert, 0)
    nused_ref[0] = total // R

    def tail_blk(b, carry):
        blk_e_ref[b] = N_EXPERTS - 1
        nvalid_ref[b] = 0
        return carry
    lax.fori_loop(total // R, nblk, tail_blk, 0)


def _slots(counts, nblk):
    smem = pl.BlockSpec(memory_space=pltpu.SMEM)
    return pl.pallas_call(
        _slots_kernel,
        in_specs=[smem],
        out_specs=[smem, smem, smem, smem],
        out_shape=[jax.ShapeDtypeStruct((N_EXPERTS,), jnp.int32),
                   jax.ShapeDtypeStruct((nblk,), jnp.int32),
                   jax.ShapeDtypeStruct((nblk,), jnp.int32),
                   jax.ShapeDtypeStruct((1,), jnp.int32)],
        name="slots",
    )(counts)


def _experts_kernel(blk_e_ref, nvalid_ref, nused_ref, xs_ref, wg_ref, wu_ref, wd_ref, ys_ref):
    j = pl.program_id(0)
    R = SLOT_BLOCK
    valid = j < nused_ref[0]

    @pl.when(valid)
    def _():
        rows = lax.broadcasted_iota(jnp.int32, (R, PACKED), 0)
        xw = jnp.where(rows < nvalid_ref[j], xs_ref[...], 0)
        xb = _unpack_rows(xw).astype(BF16)
        g = jnp.dot(xb, wg_ref[0], preferred_element_type=F32)
        u = jnp.dot(xb, wu_ref[0], preferred_element_type=F32)
        act = (g * _sigmoid(g) * u).astype(BF16)
        ys_ref[...] = _pack_rows(jnp.dot(act, wd_ref[0], preferred_element_type=F32))


def _experts(blk_e, nvalid, nused, xs, w_gate, w_up, w_down):
    nblk = blk_e.shape[0]
    R = SLOT_BLOCK
    wspec = lambda shape: pl.BlockSpec((1,) + shape, lambda j, be, nv, nu: (be[j], 0, 0))
    grid_spec = pltpu.PrefetchScalarGridSpec(
        num_scalar_prefetch=3,
        grid=(nblk,),
        in_specs=[pl.BlockSpec((R, PACKED), lambda j, be, nv, nu: (jnp.minimum(j, nu[0] - 1), 0)),
                  wspec((D_MODEL, D_EXPERT)),
                  wspec((D_MODEL, D_EXPERT)),
                  wspec((D_EXPERT, D_MODEL))],
        out_specs=pl.BlockSpec((R, PACKED), lambda j, be, nv, nu: (jnp.minimum(j, nu[0] - 1), 0)),
    )
    return pl.pallas_call(
        _experts_kernel,
        grid_spec=grid_spec,
        out_shape=jax.ShapeDtypeStruct((nblk * R, PACKED), jnp.int32),
        compiler_params=pltpu.CompilerParams(dimension_semantics=("arbitrary",),
                                             vmem_limit_bytes=VMEM_LIMIT),
        name="experts",
    )(blk_e, nvalid, nused, xs, w_gate, w_up, w_down)


def _dest_kernel(pstart_ref, code_ref, dest_ref):
    c = code_ref[...]
    e = c >> RANK_BITS
    base = jnp.zeros_like(c)
    for k in range(N_EXPERTS):
        base = jnp.where(e == k, pstart_ref[k], base)
    dest_ref[...] = base + (c & (RANK_RADIX - 1))


def _dests(pstart, code_rows):
    T = code_rows.shape[1]
    tc = min(4096, T)
    return pl.pallas_call(
        _dest_kernel,
        grid=(T // tc,),
        in_specs=[pl.BlockSpec(memory_space=pltpu.SMEM),
                  pl.BlockSpec((8, tc), lambda i: (0, i))],
        out_specs=pl.BlockSpec((8, tc), lambda i: (0, i)),
        out_shape=jax.ShapeDtypeStruct((8, T), jnp.int32),
        name="dests",
    )(pstart, code_rows)


def _sc_workers():
    info = plsc.get_sparse_core_info()
    return info.num_cores, info.num_cores * info.num_subcores


def _sc_scatter_rows(src, idx, n_out):
    n_src, width = src.shape
    nc, nw = _sc_workers()
    per_w = n_src // nw
    ch = SC_CHUNK
    nch = per_w // ch
    assert idx.shape[0] == 2 * n_src and per_w * nw == n_src and nch * ch == per_w and nch % 2 == 0
    mesh = plsc.VectorSubcoreMesh(core_axis_name="c", subcore_axis_name="s")
    dma = pltpu.SemaphoreType.DMA

    @functools.partial(
        pl.kernel, mesh=mesh, out_type=jax.ShapeDtypeStruct((n_out, width), src.dtype),
        scratch_types=[pltpu.VMEM((per_w,), jnp.int32), pltpu.VMEM((per_w,), jnp.int32),
                       pltpu.VMEM((ch, width), src.dtype), pltpu.VMEM((ch, width), src.dtype),
                       dma, dma, dma, dma, dma, dma],
        name="sc_scatter")
    def scatter(src_hbm, idx_hbm, out_hbm, idx_a, idx_b, buf0, buf1, in0, in1, out0a, out0b, out1a, out1b):
        base = (lax.axis_index("s") * nc + lax.axis_index("c")) * per_w
        pltpu.sync_copy(idx_hbm.at[pl.ds(base, per_w)], idx_a)
        pltpu.sync_copy(idx_hbm.at[pl.ds(n_src + base, per_w)], idx_b)

        def read(c, buf, sem):
            return pltpu.make_async_copy(src_hbm.at[pl.ds(base + c * ch, ch)], buf, sem)

        def writes(c, buf, sem_a, sem_b):
            rows = pl.ds(c * ch, ch)
            return (pltpu.make_async_copy(buf, out_hbm.at[idx_a.at[rows]], sem_a),
                    pltpu.make_async_copy(buf, out_hbm.at[idx_b.at[rows]], sem_b))

        def start(copies):
            for cp in copies:
                cp.start()

        def wait(copies):
            for cp in copies:
                cp.wait()

        read(0, buf0, in0).start()

        @pl.loop(0, nch, step=2)
        def _(c):
            @pl.when(c > 0)
            def _():
                wait(writes(c - 1, buf1, out1a, out1b))
            read(c + 1, buf1, in1).start()
            read(c, buf0, in0).wait()
            start(writes(c, buf0, out0a, out0b))
            read(c + 1, buf1, in1).wait()
            start(writes(c + 1, buf1, out1a, out1b))
            wait(writes(c, buf0, out0a, out0b))

            @pl.when(c + 2 < nch)
            def _():
                read(c + 2, buf0, in0).start()

        wait(writes(nch - 1, buf1, out1a, out1b))

    return scatter(src, idx)


def _sc_gather_rows(src, idx):
    M = idx.shape[0]
    width = src.shape[1]
    nc, nw = _sc_workers()
    per_w = M // nw
    ch = SC_CHUNK
    nch = per_w // ch
    assert per_w * nw == M and nch * ch == per_w and nch % 2 == 0
    mesh = plsc.VectorSubcoreMesh(core_axis_name="c", subcore_axis_name="s")
    dma = pltpu.SemaphoreType.DMA

    @functools.partial(
        pl.kernel, mesh=mesh, out_type=jax.ShapeDtypeStruct((M, width), src.dtype),
        scratch_types=[pltpu.VMEM((per_w,), jnp.int32),
                       pltpu.VMEM((ch, width), src.dtype), pltpu.VMEM((ch, width), src.dtype),
                       dma, dma, dma, dma],
        name="sc_gather")
    def gather(src_hbm, idx_hbm, out_hbm, idx_v, buf0, buf1, in0, in1, out0, out1):
        base = (lax.axis_index("s") * nc + lax.axis_index("c")) * per_w
        pltpu.sync_copy(idx_hbm.at[pl.ds(base, per_w)], idx_v)

        def read(c, buf, sem):
            return pltpu.make_async_copy(src_hbm.at[idx_v.at[pl.ds(c * ch, ch)]], buf, sem)

        def write(c, buf, sem):
            return pltpu.make_async_copy(buf, out_hbm.at[pl.ds(base + c * ch, ch)], sem)

        read(0, buf0, in0).start()

        @pl.loop(0, nch, step=2)
        def _(c):
            @pl.when(c > 0)
            def _():
                write(c - 1, buf1, out1).wait()
            read(c + 1, buf1, in1).start()
            read(c, buf0, in0).wait()
            write(c, buf0, out0).start()
            read(c + 1, buf1, in1).wait()
            write(c + 1, buf1, out1).start()
            write(c, buf0, out0).wait()

            @pl.when(c + 2 < nch)
            def _():
                read(c + 2, buf0, in0).start()

        write(nch - 1, buf1, out1).wait()

    return gather(src, idx)


def _final_kernel(x1_ref, y1_ref, y2_ref, meta_ref, nw_ref, o_ref):
    wcol = meta_ref[...].T
    w1 = wcol[:, 0:1]
    w2 = wcol[:, 1:2]
    x2 = x1_ref[...] + (w1 * _unpack_rows(y1_ref[...]) + w2 * _unpack_rows(y2_ref[...]))
    o_ref[...] = _rms(x2, nw_ref[...])


def _final(x1, yg, meta, norm_w):
    T = x1.shape[0]
    tm = min(COMB_TILE, T)
    nt = T // tm
    row = lambda i: (i, 0)
    return pl.pallas_call(
        _final_kernel,
        grid=(nt,),
        in_specs=[pl.BlockSpec((tm, D_MODEL), row),
                  pl.BlockSpec((tm, PACKED), row),
                  pl.BlockSpec((tm, PACKED), lambda i: (i + nt, 0)),
                  pl.BlockSpec((8, tm), lambda i: (0, i)),
                  pl.BlockSpec((1, D_MODEL), lambda i: (0, 0))],
        out_specs=pl.BlockSpec((tm, D_MODEL), row),
        out_shape=jax.ShapeDtypeStruct((T, D_MODEL), F32),
        compiler_params=pltpu.CompilerParams(dimension_semantics=("arbitrary",),
                                             vmem_limit_bytes=VMEM_LIMIT),
        name="final",
    )(x1, yg, yg, meta, norm_w)


def _layer(x, norm_mix_w, w_in, conv_w, conv_b, b_igate, b_fgate, attn_sinks, mlstm_norm_w,
           w_attn_o, w_mlstm_o, w_out, norm_ffn_w, w_group, b_group, w_router, b_router,
           w_gate, w_up, w_down, out_norm_w):
    B, S, D = x.shape
    T = B * S
    x2 = x.reshape(T, D)

    wt = w_in.T
    w_tok = jnp.concatenate([wt[_O_AK:_O_AV], wt[_O_MK:_O_MV]], axis=0).astype(BF16)
    w_chan = jnp.concatenate([wt[_O_AQ:_O_AK] * (HEAD_DIM ** -0.5), wt[_O_AV:_O_MQ], wt[_O_MQ:_O_MK],
                              wt[_O_MV:_O_MI],
                              jnp.pad(wt[_O_MI:_O_GA], ((0, GATE_ROWS - 2 * M_HEADS), (0, 0)))],
                             axis=0).astype(BF16)
    ak, mk, aqt, avt, mqt, mvt, mot, grow = _inproj(x2, norm_mix_w.reshape(1, D), w_tok, w_chan, B, S)

    ya = _attn(attn_sinks.astype(F32), aqt, ak.reshape(B, S, -1), avt)

    bias = jnp.concatenate([b_igate, b_fgate]).astype(F32)
    expert_ws = [w.reshape(-1, w.shape[-1]) for w in (w_gate, w_up, w_down)]
    ym, wg16, wu16, wd16 = _mlstm(mqt, mk.reshape(B, S, -1), mvt, mot, grow, conv_w, conv_b,
                                  bias.reshape(2 * M_HEADS, 1), mlstm_norm_w, expert_ws)

    gpad = jnp.zeros((8 - N_GROUPS, D), F32)
    w_rt = jnp.concatenate([w_group.T, gpad, w_router.T, jnp.zeros((LANES - ROUTER_ROWS, D), F32)], axis=0)
    w_rt_hi = w_rt.astype(BF16)
    w_rt = jnp.concatenate([w_rt_hi, (w_rt - w_rt_hi.astype(F32)).astype(BF16)], axis=0).T
    b_rt = jnp.concatenate([b_group, jnp.zeros((8 - N_GROUPS,), F32), b_router]).astype(F32)
    b_rt = jnp.broadcast_to(b_rt[:, None], (ROUTER_ROWS, min(TOK_TILE, T)))
    x1, h2p, meta, code_rows, cnt = _outproj(x2, ya.reshape(T, -1), ym.reshape(T, -1),
                                            norm_mix_w.reshape(1, D), wt[_O_GA:].astype(BF16),
                                            w_attn_o.astype(BF16), w_mlstm_o.astype(BF16),
                                            w_out.astype(BF16), norm_ffn_w.reshape(1, D), w_rt, b_rt)

    counts = cnt[:, 0].astype(jnp.int32)
    nblk = (2 * T) // SLOT_BLOCK + N_EXPERTS
    pstart, blk_e, nvalid, nused = _slots(counts, nblk)
    dest = _dests(pstart, code_rows)[0:2].reshape(2 * T)

    xs = _sc_scatter_rows(h2p, dest, nblk * SLOT_BLOCK)
    ys = _experts(blk_e, nvalid, nused, xs, wg16.reshape(w_gate.shape), wu16.reshape(w_up.shape),
                  wd16.reshape(w_down.shape))
    yg = _sc_gather_rows(ys, dest)
    out = _final(x1, yg, meta, out_norm_w.reshape(1, D))
    return out.reshape(B, S, D)


def kernel(x, norm_mix_w, w_in, conv_w, conv_b, b_igate, b_fgate, attn_sinks, mlstm_norm_w, w_attn_o,
           w_mlstm_o, w_out, norm_ffn_w, w_group, b_group, w_router, b_router, w_gate, w_up, w_down,
           norm_final_w):
    depth = w_in.shape[0]
    assert depth == 1, "final RMSNorm is fused into the last layer's combine kernel"
    return _layer(x, norm_mix_w[0], w_in[0], conv_w[0], conv_b[0], b_igate[0], b_fgate[0],
                  attn_sinks[0], mlstm_norm_w[0], w_attn_o[0], w_mlstm_o[0], w_out[0], norm_ffn_w[0],
                  w_group[0], b_group[0], w_router[0], b_router[0], w_gate[0], w_up[0], w_down[0],
                  norm_final_w)
```

```python
import functools

import jax
import jax.numpy as jnp
from jax import lax
from jax.experimental import pallas as pl
from jax.experimental.pallas import tpu as pltpu
from jax.experimental.pallas import tpu_sc as plsc

F32 = jnp.float32
BF16 = jnp.bfloat16

D_MODEL = 1024
N_Q_HEADS = 8
N_KV_HEADS = 2
HEAD_DIM = 64
WINDOW = 128
GQA_GROUP = N_Q_HEADS // N_KV_HEADS
M_HEADS = 4
M_QK_DIM = 64
M_V_DIM = 128
CONV_WIDTH = 4
N_GROUPS = 4
EXPERTS_PER_GROUP = 8
N_EXPERTS = N_GROUPS * EXPERTS_PER_GROUP
D_EXPERT = 512
EPS = 1e-6

ATT_Q_WIDTH = N_Q_HEADS * HEAD_DIM
ATT_KV_WIDTH = N_KV_HEADS * HEAD_DIM
M_QK_WIDTH = M_HEADS * M_QK_DIM
M_V_WIDTH = M_HEADS * M_V_DIM

LANES = 128
NEG = -1e30
V7X_VMEM_BYTES = 64 * 1024 * 1024
VMEM_LIMIT = V7X_VMEM_BYTES - 8 * 1024 * 1024

IN_TILE = 1024
TOK_TILE = 1024
ATT_TILE = 2048
LOOKAHEAD = 2
M_LOOKAHEAD = 4
M_CHUNK = 128
M_BATCH = 8
M_STATE_ROWS = M_V_DIM + 16
SLOT_BLOCK = 512
COMB_TILE = 1024
RANK_RADIX = 65536
RANK_BITS = 16
ROUTER_ROWS = 8 + N_EXPERTS
PACKED = D_MODEL // 2
SC_CHUNK = 64

_O_AQ = 0
_O_AK = _O_AQ + ATT_Q_WIDTH
_O_AV = _O_AK + ATT_KV_WIDTH
_O_MQ = _O_AV + ATT_KV_WIDTH
_O_MK = _O_MQ + M_QK_WIDTH
_O_MV = _O_MK + M_QK_WIDTH
_O_MO = _O_MV + M_V_WIDTH
_O_MI = _O_MO + M_V_WIDTH
_O_MF = _O_MI + M_HEADS
_O_GA = _O_MF + M_HEADS


def _rms(x, w):
    return x * lax.rsqrt(jnp.mean(x * x, axis=-1, keepdims=True) + EPS) * w


def _sigmoid(x):
    return 0.5 * jnp.tanh(0.5 * x) + 0.5


def _pack_rows(x):
    half = x.shape[1] // 2
    bits = lax.bitcast_convert_type(x.astype(BF16).astype(F32), jnp.uint32)
    packed = (bits[:, :half] >> 16) | (bits[:, half:] & jnp.uint32(0xFFFF0000))
    return lax.bitcast_convert_type(packed, jnp.int32)


def _unpack_rows(w):
    u = lax.bitcast_convert_type(w, jnp.uint32)
    lo = lax.bitcast_convert_type(u << 16, F32)
    hi = lax.bitcast_convert_type(u & jnp.uint32(0xFFFF0000), F32)
    return jnp.concatenate([lo, hi], axis=-1)


def _log_sigmoid(x):
    return jnp.minimum(x, 0.0) - jnp.log1p(jnp.exp(-jnp.abs(x)))


_TOK_SPLITS = (ATT_KV_WIDTH, M_QK_WIDTH)
_CHAN_SPLITS = (ATT_Q_WIDTH, ATT_KV_WIDTH, M_QK_WIDTH, M_V_WIDTH, M_V_WIDTH)
GATE_ROWS = 16
_NT = (((1,), (1,)), ((), ()))


def _inproj_kernel(x_ref, nw_ref, w_ref, wt_ref, *out_refs):
    tok_refs = out_refs[:len(_TOK_SPLITS)]
    chan_refs = out_refs[len(_TOK_SPLITS):len(_TOK_SPLITS) + len(_CHAN_SPLITS)]
    grow_ref = out_refs[-1]
    h = _rms(x_ref[...], nw_ref[...]).astype(BF16)
    lo = 0
    for ref, width in zip(tok_refs, _TOK_SPLITS):
        ref[...] = lax.dot_general(h, w_ref[lo:lo + width, :], _NT, preferred_element_type=F32).astype(BF16)
        lo += width
    lo = 0
    for ref, width in zip(chan_refs, _CHAN_SPLITS):
        ref[0] = lax.dot_general(wt_ref[lo:lo + width, :], h, (((1,), (1,)), ((), ())),
                                 preferred_element_type=F32).astype(BF16)
        lo += width
    grow_ref[...] = lax.dot_general(wt_ref[lo:lo + GATE_ROWS, :], h, (((1,), (1,)), ((), ())),
                                    preferred_element_type=F32)[0:2 * M_HEADS, :]


def _inproj(x2, norm_w, w_tok, w_chan, B, S):
    T = x2.shape[0]
    tm = min(IN_TILE, S)
    tpb = S // tm
    nsteps = T // tm
    row = lambda i: (i, 0)
    const = lambda i: (0, 0)
    chan = lambda i: (i // tpb, 0, i % tpb)
    return pl.pallas_call(
        _inproj_kernel,
        grid=(nsteps,),
        in_specs=[pl.BlockSpec((tm, D_MODEL), row),
                  pl.BlockSpec((1, D_MODEL), const),
                  pl.BlockSpec(w_tok.shape, const),
                  pl.BlockSpec(w_chan.shape, const)],
        out_specs=[*[pl.BlockSpec((tm, w), row) for w in _TOK_SPLITS],
                   *[pl.BlockSpec((1, w, tm), chan) for w in _CHAN_SPLITS],
                   pl.BlockSpec((8, tm), lambda i: (0, i))],
        out_shape=[*[jax.ShapeDtypeStruct((T, w), BF16) for w in _TOK_SPLITS],
                   *[jax.ShapeDtypeStruct((B, w, S), BF16) for w in _CHAN_SPLITS],
                   jax.ShapeDtypeStruct((8, T), F32)],
        compiler_params=pltpu.CompilerParams(dimension_semantics=("arbitrary",),
                                             vmem_limit_bytes=VMEM_LIMIT),
        name="inproj",
    )(x2, norm_w, w_tok, w_chan)


def _attn_kernel(tq, sink_ref, qt_ref, k_ref, kp_ref, vt_ref, vtp_ref, o_ref):
    i = pl.program_id(1)
    W = WINDOW
    G = GQA_GROUP
    u = lax.broadcasted_iota(jnp.int32, (W, W), 0)
    t = lax.broadcasted_iota(jnp.int32, (W, W), 1)
    from_prev = u > t

    def keys_values(j):
        if j == 0:
            return kp_ref[0], vtp_ref[0], k_ref[0, 0:W, :], vt_ref[0, :, 0:W]
        return (k_ref[0, (j - 1) * W:j * W, :], vt_ref[0, :, (j - 1) * W:j * W],
                k_ref[0, j * W:(j + 1) * W, :], vt_ref[0, :, j * W:(j + 1) * W])

    def scores(j, g):
        k_prev, _, k_cur, _ = keys_values(j)
        dims = slice(g * HEAD_DIM, (g + 1) * HEAD_DIM)
        qg = jnp.concatenate([qt_ref[0, h * HEAD_DIM:(h + 1) * HEAD_DIM, j * W:(j + 1) * W]
                              for h in range(g * G, (g + 1) * G)], axis=1)
        return (jnp.dot(k_prev[:, dims], qg, preferred_element_type=F32),
                jnp.dot(k_cur[:, dims], qg, preferred_element_type=F32))

    tasks = [(j, g) for j in range(tq // W) for g in range(N_KV_HEADS)]
    queue = [scores(*t) for t in tasks[:LOOKAHEAD]]
    pair_rows = []
    for n_task, (j, g) in enumerate(tasks):
        s_prev, s_cur = queue.pop(0)
        if n_task + LOOKAHEAD < len(tasks):
            queue.append(scores(*tasks[n_task + LOOKAHEAD]))
        cols = slice(j * W, (j + 1) * W)
        _, vt_prev, _, vt_cur = keys_values(j)
        dims = slice(g * HEAD_DIM, (g + 1) * HEAD_DIM)
        for n in range(G):
            hc = slice(n * W, (n + 1) * W)
            sp = s_prev[:, hc]
            if j == 0:
                sp = jnp.where(i > 0, sp, NEG)
            s = jnp.where(from_prev, sp, s_cur[:, hc])
            sink = sink_ref[g * G + n]
            m = jnp.maximum(jnp.max(s, axis=0, keepdims=True), sink)
            p = jnp.exp(s - m)
            inv = 1.0 / (jnp.sum(p, axis=0, keepdims=True) + jnp.exp(sink - m))
            p_prev = jnp.where(from_prev, p, 0.0).astype(BF16)
            p_cur = jnp.where(from_prev, 0.0, p).astype(BF16)
            o = jnp.dot(vt_prev[dims, :], p_prev, preferred_element_type=F32) \
                + jnp.dot(vt_cur[dims, :], p_cur, preferred_element_type=F32)
            pair_rows.append(o * inv)
        if g == N_KV_HEADS - 1:
            out = [jnp.concatenate(pair_rows[n:n + 2], axis=0).T for n in range(0, N_Q_HEADS, 2)]
            o_ref[0, cols, :] = jnp.concatenate(out, axis=1).astype(BF16)
            pair_rows = []


def _attn(sinks, qt, k3, vt):
    B, S, _ = k3.shape
    tq = min(ATT_TILE, S)
    per = tq // WINDOW
    chan = lambda b, i: (b, 0, i)
    tok = lambda b, i: (b, i, 0)
    return pl.pallas_call(
        functools.partial(_attn_kernel, tq),
        grid=(B, S // tq),
        in_specs=[pl.BlockSpec(memory_space=pltpu.SMEM),
                  pl.BlockSpec((1, ATT_Q_WIDTH, tq), chan),
                  pl.BlockSpec((1, tq, ATT_KV_WIDTH), tok),
                  pl.BlockSpec((1, WINDOW, ATT_KV_WIDTH), lambda b, i: (b, jnp.maximum(i * per - 1, 0), 0)),
                  pl.BlockSpec((1, ATT_KV_WIDTH, tq), chan),
                  pl.BlockSpec((1, ATT_KV_WIDTH, WINDOW), lambda b, i: (b, 0, jnp.maximum(i * per - 1, 0)))],
        out_specs=pl.BlockSpec((1, tq, ATT_Q_WIDTH), tok),
        out_shape=jax.ShapeDtypeStruct((B, S, ATT_Q_WIDTH), BF16),
        compiler_params=pltpu.CompilerParams(dimension_semantics=("arbitrary", "arbitrary"),
                                             vmem_limit_bytes=VMEM_LIMIT),
        name="attn",
    )(sinks, qt, k3, k3, vt, vt)


def _mlstm_kernel(nb, *refs):
    L = M_CHUNK
    H = M_HEADS
    mqt_ref, mk_ref, mvt_ref, mot_ref = refs[:4]
    grow_refs = refs[4:4 + nb]
    (cwq_ref, cbq_ref, cwk_ref, cbk_ref, brow_ref, nw_ref, eg_ref, eu_ref, ed_ref,
     y_ref, eg16_ref, eu16_ref, ed16_ref, state_ref, m_ref, prevq_ref, ubuf_ref) = refs[4 + nb:]
    cidx = pl.program_id(1)
    for src, dst in ((eg_ref, eg16_ref), (eu_ref, eu16_ref), (ed_ref, ed16_ref)):
        dst[...] = src[...].astype(BF16)

    @pl.when(cidx == 0)
    def _():
        state_ref[...] = jnp.zeros_like(state_ref)
        m_ref[...] = jnp.zeros_like(m_ref)
        prevq_ref[...] = jnp.zeros_like(prevq_ref)
        ubuf_ref[:, 0:8, :] = jnp.zeros((nb, 8, M_QK_WIDTH), F32)

    ri = lax.broadcasted_iota(jnp.int32, (L, L), 0)
    ci = lax.broadcasted_iota(jnp.int32, (L, L), 1)
    causal_t = ri <= ci
    triu = jnp.where(causal_t, 1.0, 0.0).astype(F32)
    lane = lax.broadcasted_iota(jnp.int32, (8, L), 1)
    r2 = lax.broadcasted_iota(jnp.int32, (2 * L, L), 0)
    c2 = lax.broadcasted_iota(jnp.int32, (2 * L, L), 1)
    shifts = [jnp.where(r2 - c2 == L - k, 1.0, 0.0).astype(BF16) for k in range(1, CONV_WIDTH)]
    ones_rows = jnp.where(lax.broadcasted_iota(jnp.int32, (M_STATE_ROWS - M_V_DIM, L), 0) == 0,
                          1.0, 0.0).astype(BF16)

    pairs = [(bb, h) for bb in range(nb) for h in range(H)]
    states = [state_ref[bb * H + h] for bb, h in pairs]
    m_prevs = [m_ref[bb] for bb in range(nb)]

    def prepare(bb):
        cur = mqt_ref[bb]
        both = jnp.concatenate([prevq_ref[bb], cur], axis=1)
        acc = cbq_ref[...] + cwq_ref[CONV_WIDTH - 1] * cur.astype(F32)
        for k in range(1, CONV_WIDTH):
            acc = acc + cwq_ref[CONV_WIDTH - 1 - k] * jnp.dot(both, shifts[k - 1],
                                                              preferred_element_type=F32)
        qt = (acc * _sigmoid(acc)).astype(BF16)
        ubuf_ref[bb, 8:L + 8, :] = mk_ref[bb].astype(F32)
        acc = cbk_ref[...] + cwk_ref[CONV_WIDTH - 1:CONV_WIDTH, :] * ubuf_ref[bb, 8:L + 8, :]
        for j in range(CONV_WIDTH - 1):
            off = 8 - (CONV_WIDTH - 1) + j
            acc = acc + cwk_ref[j:j + 1, :] * ubuf_ref[bb, off:off + L, :]
        ubuf_ref[bb, 0:8, :] = ubuf_ref[bb, L:L + 8, :]
        kk = (acc * _sigmoid(acc) * (M_QK_DIM ** -0.5)).astype(BF16)

        gr = grow_refs[bb][...] + brow_ref[...]
        b = jnp.dot(_log_sigmoid(gr), triu, preferred_element_type=F32,
                    precision=lax.Precision.HIGHEST)
        gi = pltpu.roll(gr, 4, axis=0)
        u = gi - b
        cm = u
        for sh in (1, 2, 4, 8, 16, 32, 64):
            if sh < L:
                cm = jnp.maximum(cm, jnp.where(lane >= sh, pltpu.roll(cm, sh, axis=1), NEG))
        m_prev = m_prevs[bb]
        mt = b + jnp.maximum(m_prev, cm)
        wa = jnp.exp(b + m_prev - mt)
        emt = jnp.exp(-mt)
        m_new = jnp.broadcast_to(mt[:, L - 1:L], (8, L))
        b_last = jnp.broadcast_to(b[:, L - 1:L], (8, L))
        wc = jnp.exp(b_last + m_prev - m_new)
        ws = jnp.exp(b_last + u - m_new)
        return qt, kk, b - mt, u.T, wa, emt, wc, ws, m_new

    seqs = [prepare(bb) for bb in range(nb)]

    def products(bb, h, state):
        qt, kk = seqs[bb][0], seqs[bb][1]
        qh = qt[h * M_QK_DIM:(h + 1) * M_QK_DIM, :]
        kh = kk[:, h * M_QK_DIM:(h + 1) * M_QK_DIM]
        return (kh, jnp.dot(kh, qh, preferred_element_type=F32),
                jnp.dot(state.astype(BF16), qh, preferred_element_type=F32))

    new_states, outs = [], []
    queue = [products(*pairs[n], states[n]) for n in range(min(M_LOOKAHEAD, len(pairs)))]
    for n, ((bb, h), state) in enumerate(zip(pairs, states)):
        kh, st, inter = queue.pop(0)
        if n + M_LOOKAHEAD < len(pairs):
            queue.append(products(*pairs[n + M_LOOKAHEAD], states[n + M_LOOKAHEAD]))
        _, _, bmt, ucol, wa, emt, wc, ws, _ = seqs[bb]
        row = slice(H + h, H + h + 1)
        wq = jnp.exp(jnp.where(causal_t, bmt[row, :] + ucol[:, H + h:H + h + 1], NEG))
        sw = (st * wq).astype(BF16)
        vext = jnp.concatenate([mvt_ref[bb, h * M_V_DIM:(h + 1) * M_V_DIM, :], ones_rows], axis=0)
        res = wa[row, :] * inter + jnp.dot(vext, sw, preferred_element_type=F32)
        num = res[:M_V_DIM, :]
        den = res[M_V_DIM:M_V_DIM + 1, :]
        hb = num / jnp.maximum(jnp.abs(den), emt[row, :])
        vw = (vext.astype(F32) * ws[row, :]).astype(BF16)
        new_states.append(wc[row, 0:1] * state + jnp.dot(vw, kh, preferred_element_type=F32))
        hn = hb * lax.rsqrt(jnp.mean(hb * hb, axis=0, keepdims=True) + EPS)
        hn = hn * nw_ref[h * M_V_DIM:(h + 1) * M_V_DIM, :]
        og = _sigmoid(mot_ref[bb, h * M_V_DIM:(h + 1) * M_V_DIM, :].astype(F32))
        outs.append((og * hn).T.astype(BF16))

    for i, (bb, h) in enumerate(pairs):
        state_ref[bb * H + h] = new_states[i]
    for bb in range(nb):
        m_ref[bb] = seqs[bb][8]
        prevq_ref[bb] = mqt_ref[bb]
        y_ref[bb] = jnp.concatenate(outs[bb * H:(bb + 1) * H], axis=1)


def _mlstm(mqt, mk3, mvt, mot, grow, conv_w, conv_b, bias_row, norm_w, expert_ws):
    B, S, _ = mk3.shape
    L = M_CHUNK
    nb = max(d for d in range(1, M_BATCH + 1) if B % d == 0)
    nc = S // L
    assert L == LANES, "per-head scalars are kept lane-replicated next to (8, L) gate rows"
    tok = lambda b, c: (b, c, 0)
    chan = lambda b, c: (b, 0, c)
    const2 = lambda b, c: (0, 0)
    const3 = lambda b, c: (0, 0, 0)
    grow_specs = [pl.BlockSpec((8, L), functools.partial(lambda b, c, n: (0, (b * nb + n) * nc + c), n=n))
                  for n in range(nb)]
    rep = lambda v: jnp.broadcast_to(v.astype(F32)[..., None], v.shape + (L,))
    cwq, cbq = rep(conv_w[:, :M_QK_WIDTH]), rep(conv_b[:M_QK_WIDTH])
    cwk, cbk = conv_w[:, M_QK_WIDTH:].astype(F32), conv_b[M_QK_WIDTH:].reshape(1, -1).astype(F32)
    nsteps = (B // nb) * nc
    assert all(w.shape[0] % (8 * nsteps) == 0 for w in expert_ws)
    cast_specs = [pl.BlockSpec((w.shape[0] // nsteps, w.shape[1]), lambda b, c: (b * nc + c, 0))
                  for w in expert_ws]
    return pl.pallas_call(
        functools.partial(_mlstm_kernel, nb),
        grid=(B // nb, nc),
        in_specs=[pl.BlockSpec((nb, M_QK_WIDTH, L), chan),
                  pl.BlockSpec((nb, L, M_QK_WIDTH), tok),
                  pl.BlockSpec((nb, M_V_WIDTH, L), chan),
                  pl.BlockSpec((nb, M_V_WIDTH, L), chan),
                  *grow_specs,
                  pl.BlockSpec((CONV_WIDTH, M_QK_WIDTH, L), const3),
                  pl.BlockSpec((M_QK_WIDTH, L), const2),
                  pl.BlockSpec((CONV_WIDTH, M_QK_WIDTH), const2),
                  pl.BlockSpec((1, M_QK_WIDTH), const2),
                  pl.BlockSpec((8, 1), const2),
                  pl.BlockSpec((M_V_WIDTH, L), const2),
                  *cast_specs],
        out_specs=[pl.BlockSpec((nb, L, M_V_WIDTH), tok), *cast_specs],
        out_shape=[jax.ShapeDtypeStruct((B, S, M_V_WIDTH), BF16),
                   *[jax.ShapeDtypeStruct(w.shape, BF16) for w in expert_ws]],
        scratch_shapes=[pltpu.VMEM((nb * M_HEADS, M_STATE_ROWS, M_QK_DIM), F32),
                        pltpu.VMEM((nb, 8, LANES), F32),
                        pltpu.VMEM((nb, M_QK_WIDTH, L), BF16),
                        pltpu.VMEM((nb, L + 8, M_QK_WIDTH), F32)],
        compiler_params=pltpu.CompilerParams(dimension_semantics=("arbitrary", "arbitrary"),
                                             vmem_limit_bytes=VMEM_LIMIT),
        name="mlstm",
    )(mqt, mk3, mvt, mot, *([grow] * nb), cwq, cbq, cwk, cbk, bias_row, rep(norm_w), *expert_ws)


def _outproj_kernel(x_ref, ya_ref, ym_ref, nmix_ref, wgab_ref, wa_ref, wm_ref, wo_ref, nw_ref, wr_ref, br_ref,
                    x1_ref, h2_ref, meta_ref, code_ref, cnt_ref, cnt_scr):
    tm = x_ref.shape[0]
    i = pl.program_id(0)

    @pl.when(i == 0)
    def _():
        cnt_scr[...] = jnp.zeros_like(cnt_scr)

    pa = jnp.dot(ya_ref[...], wa_ref[...], preferred_element_type=F32)
    pm = jnp.dot(ym_ref[...], wm_ref[...], preferred_element_type=F32)
    x = x_ref[...]
    h1 = _rms(x, nmix_ref[...]).astype(BF16)
    ga = lax.dot_general(h1, wgab_ref[:D_MODEL, :], _NT, preferred_element_type=F32)
    gb = lax.dot_general(h1, wgab_ref[D_MODEL:, :], _NT, preferred_element_type=F32)
    mix = _sigmoid(ga) * pa + _sigmoid(gb) * pm
    x1 = x + jnp.dot(mix.astype(BF16), wo_ref[...], preferred_element_type=F32)
    x1_ref[...] = x1
    h2 = _rms(x1, nw_ref[...])
    h2_ref[...] = _pack_rows(h2)

    nt = (((1,), (1,)), ((), ()))
    h_hi = h2.astype(BF16)
    h_lo = (h2 - h_hi.astype(F32)).astype(BF16)
    hw = jnp.dot(h_hi, wr_ref[...], preferred_element_type=F32)
    logits = (hw[:, :LANES] + hw[:, LANES:]
              + jnp.dot(h_lo, wr_ref[:, :LANES], preferred_element_type=F32)).T[:ROUTER_ROWS]
    logits = logits + br_ref[...]
    row8 = lax.broadcasted_iota(jnp.int32, (8, tm), 0)

    def first_argmax(v):
        mx = jnp.max(v, axis=0, keepdims=True)
        idx = jnp.min(jnp.where(v == mx, row8, 8), axis=0, keepdims=True)
        return mx, idx

    gl = jnp.where(row8 < N_GROUPS, logits[0:8], NEG)
    gmax, gi = first_argmax(gl)
    gp = 1.0 / jnp.sum(jnp.exp(gl - gmax), axis=0, keepdims=True)
    el = jnp.zeros((8, tm), F32)
    for g in range(N_GROUPS):
        el = jnp.where(gi == g, logits[8 + 8 * g:16 + 8 * g], el)
    v1, j1 = first_argmax(el)
    v2, j2 = first_argmax(jnp.where(row8 == j1, NEG, el))
    t = jnp.exp(v2 - v1)
    w1 = gp / (1.0 + t)
    w2 = gp * t / (1.0 + t)
    e1 = gi * EXPERTS_PER_GROUP + j1
    e2 = gi * EXPERTS_PER_GROUP + j2

    erow = lax.broadcasted_iota(jnp.int32, (N_EXPERTS, tm), 0)
    hit1 = erow == e1
    hit2 = erow == e2
    onehot = jnp.where(hit1 | hit2, 1.0, 0.0)
    ri = lax.broadcasted_iota(jnp.int32, (LANES, LANES), 0)
    ci = lax.broadcasted_iota(jnp.int32, (LANES, LANES), 1)
    before = jnp.where(ri < ci, 1.0, 0.0).astype(BF16)
    run = cnt_scr[...]
    parts = []
    for c in range(tm // LANES):
        oh = onehot[:, c * LANES:(c + 1) * LANES]
        parts.append(jnp.dot(oh.astype(BF16), before, preferred_element_type=F32) + run)
        run = run + jnp.broadcast_to(jnp.sum(oh, axis=1, keepdims=True), (N_EXPERTS, LANES))
    rank = jnp.concatenate(parts, axis=1)
    r1 = jnp.sum(jnp.where(hit1, rank, 0.0), axis=0, keepdims=True)
    r2 = jnp.sum(jnp.where(hit2, rank, 0.0), axis=0, keepdims=True)
    cnt_scr[...] = run
    cnt_ref[...] = run

    c1 = e1 * RANK_RADIX + r1.astype(jnp.int32)
    c2 = e2 * RANK_RADIX + r2.astype(jnp.int32)
    code_ref[...] = jnp.where(row8 == 0, c1, jnp.where(row8 == 1, c2, 0))
    meta_ref[...] = jnp.where(row8 == 0, w1, jnp.where(row8 == 1, w2, 0.0))


def _outproj(x2, ya, ym, norm_mix, w_gab, wa, wm, wo, norm_w, w_rt, b_rt):
    T = x2.shape[0]
    tm = b_rt.shape[1]
    row = lambda i: (i, 0)
    const = lambda i: (0, 0)
    resident = lambda shape: pl.BlockSpec(shape, const, pipeline_mode=pl.Buffered(1))
    return pl.pallas_call(
        _outproj_kernel,
        grid=(T // tm,),
        in_specs=[pl.BlockSpec((tm, D_MODEL), row),
                  pl.BlockSpec((tm, ATT_Q_WIDTH), row),
                  pl.BlockSpec((tm, M_V_WIDTH), row),
                  resident((1, D_MODEL)),
                  resident((2 * D_MODEL, D_MODEL)),
                  resident((ATT_Q_WIDTH, D_MODEL)),
                  resident((M_V_WIDTH, D_MODEL)),
                  resident((D_MODEL, D_MODEL)),
                  resident((1, D_MODEL)),
                  resident((D_MODEL, 2 * LANES)),
                  resident((ROUTER_ROWS, tm))],
        out_specs=[pl.BlockSpec((tm, D_MODEL), row),
                   pl.BlockSpec((tm, PACKED), row),
                   pl.BlockSpec((8, tm), lambda i: (0, i)),
                   pl.BlockSpec((8, tm), lambda i: (0, i)),
                   pl.BlockSpec((N_EXPERTS, LANES), const)],
        out_shape=[jax.ShapeDtypeStruct((T, D_MODEL), F32),
                   jax.ShapeDtypeStruct((T, PACKED), jnp.int32),
                   jax.ShapeDtypeStruct((8, T), F32),
                   jax.ShapeDtypeStruct((8, T), jnp.int32),
                   jax.ShapeDtypeStruct((N_EXPERTS, LANES), F32)],
        scratch_shapes=[pltpu.VMEM((N_EXPERTS, LANES), F32)],
        compiler_params=pltpu.CompilerParams(dimension_semantics=("arbitrary",),
                                             vmem_limit_bytes=VMEM_LIMIT),
        name="outproj",
    )(x2, ya, ym, norm_mix, w_gab, wa, wm, wo, norm_w, w_rt, b_rt)


def _slots_kernel(cnt_ref, pstart_ref, blk_e_ref, nvalid_ref, nused_ref):
    R = SLOT_BLOCK
    nblk = blk_e_ref.shape[0]

    def per_expert(e, run):
        c = cnt_ref[e]
        end = run + ((c + R - 1) // R) * R
        pstart_ref[e] = run

        def set_blk(b, carry):
            blk_e_ref[b] = e
            nvalid_ref[b] = jnp.minimum(run + c - b * R, R)
            return carry
        lax.fori_loop(run // R, end // R, set_blk, 0)
        return end

    total = lax.fori_loop(0, N_EXPERTS, per_expert, 0)
    nused_ref[0] = total // R

    def tail_blk(b, carry):
        blk_e_ref[b] = N_EXPERTS - 1
        nvalid_ref[b] = 0
        return carry
    lax.fori_loop(total // R, nblk, tail_blk, 0)


def _slots(counts, nblk):
    smem = pl.BlockSpec(memory_space=pltpu.SMEM)
    return pl.pallas_call(
        _slots_kernel,
        in_specs=[smem],
        out_specs=[smem, smem, smem, smem],
        out_shape=[jax.ShapeDtypeStruct((N_EXPERTS,), jnp.int32),
                   jax.ShapeDtypeStruct((nblk,), jnp.int32),
                   jax.ShapeDtypeStruct((nblk,), jnp.int32),
                   jax.ShapeDtypeStruct((1,), jnp.int32)],
        name="slots",
    )(counts)


def _experts_kernel(blk_e_ref, nvalid_ref, nused_ref, xs_ref, wg_ref, wu_ref, wd_ref, ys_ref):
    j = pl.program_id(0)
    R = SLOT_BLOCK
    valid = j < nused_ref[0]

    @pl.when(valid)
    def _():
        rows = lax.broadcasted_iota(jnp.int32, (R, PACKED), 0)
        xw = jnp.where(rows < nvalid_ref[j], xs_ref[...], 0)
        xb = _unpack_rows(xw).astype(BF16)
        g = jnp.dot(xb, wg_ref[0], preferred_element_type=F32)
        u = jnp.dot(xb, wu_ref[0], preferred_element_type=F32)
        act = (g * _sigmoid(g) * u).astype(BF16)
        ys_ref[...] = _pack_rows(jnp.dot(act, wd_ref[0], preferred_element_type=F32))


def _experts(blk_e, nvalid, nused, xs, w_gate, w_up, w_down):
    nblk = blk_e.shape[0]
    R = SLOT_BLOCK
    wspec = lambda shape: pl.BlockSpec((1,) + shape, lambda j, be, nv, nu: (be[j], 0, 0))
    grid_spec = pltpu.PrefetchScalarGridSpec(
        num_scalar_prefetch=3,
        grid=(nblk,),
        in_specs=[pl.BlockSpec((R, PACKED), lambda j, be, nv, nu: (jnp.minimum(j, nu[0] - 1), 0)),
                  wspec((D_MODEL, D_EXPERT)),
                  wspec((D_MODEL, D_EXPERT)),
                  wspec((D_EXPERT, D_MODEL))],
        out_specs=pl.BlockSpec((R, PACKED), lambda j, be, nv, nu: (jnp.minimum(j, nu[0] - 1), 0)),
    )
    return pl.pallas_call(
        _experts_kernel,
        grid_spec=grid_spec,
        out_shape=jax.ShapeDtypeStruct((nblk * R, PACKED), jnp.int32),
        compiler_params=pltpu.CompilerParams(dimension_semantics=("arbitrary",),
                                             vmem_limit_bytes=VMEM_LIMIT),
        name="experts",
    )(blk_e, nvalid, nused, xs, w_gate, w_up, w_down)


def _dest_kernel(pstart_ref, code_ref, dest_ref):
    c = code_ref[...]
    e = c >> RANK_BITS
    base = jnp.zeros_like(c)
    for k in range(N_EXPERTS):
        base = jnp.where(e == k, pstart_ref[k], base)
    dest_ref[...] = base + (c & (RANK_RADIX - 1))


def _dests(pstart, code_rows):
    T = code_rows.shape[1]
    tc = min(4096, T)
    return pl.pallas_call(
        _dest_kernel,
        grid=(T // tc,),
        in_specs=[pl.BlockSpec(memory_space=pltpu.SMEM),
                  pl.BlockSpec((8, tc), lambda i: (0, i))],
        out_specs=pl.BlockSpec((8, tc), lambda i: (0, i)),
        out_shape=jax.ShapeDtypeStruct((8, T), jnp.int32),
        name="dests",
    )(pstart, code_rows)


def _sc_workers():
    info = plsc.get_sparse_core_info()
    return info.num_cores, info.num_cores * info.num_subcores


def _sc_scatter_rows(src, idx, n_out):
    n_src, width = src.shape
    nc, nw = _sc_workers()
    per_w = n_src // nw
    ch = SC_CHUNK
    nch = per_w // ch
    assert idx.shape[0] == 2 * n_src and per_w * nw == n_src and nch * ch == per_w and nch % 2 == 0
    mesh = plsc.VectorSubcoreMesh(core_axis_name="c", subcore_axis_name="s")
    dma = pltpu.SemaphoreType.DMA

    @functools.partial(
        pl.kernel, mesh=mesh, out_type=jax.ShapeDtypeStruct((n_out, width), src.dtype),
        scratch_types=[pltpu.VMEM((per_w,), jnp.int32), pltpu.VMEM((per_w,), jnp.int32),
                       pltpu.VMEM((ch, width), src.dtype), pltpu.VMEM((ch, width), src.dtype),
                       dma, dma, dma, dma, dma, dma],
        name="sc_scatter")
    def scatter(src_hbm, idx_hbm, out_hbm, idx_a, idx_b, buf0, buf1, in0, in1, out0a, out0b, out1a, out1b):
        base = (lax.axis_index("s") * nc + lax.axis_index("c")) * per_w
        pltpu.sync_copy(idx_hbm.at[pl.ds(base, per_w)], idx_a)
        pltpu.sync_copy(idx_hbm.at[pl.ds(n_src + base, per_w)], idx_b)

        def read(c, buf, sem):
            return pltpu.make_async_copy(src_hbm.at[pl.ds(base + c * ch, ch)], buf, sem)

        def writes(c, buf, sem_a, sem_b):
            rows = pl.ds(c * ch, ch)
            return (pltpu.make_async_copy(buf, out_hbm.at[idx_a.at[rows]], sem_a),
                    pltpu.make_async_copy(buf, out_hbm.at[idx_b.at[rows]], sem_b))

        def start(copies):
            for cp in copies:
                cp.start()

        def wait(copies):
            for cp in copies:
                cp.wait()

        read(0, buf0, in0).start()

        @pl.loop(0, nch, step=2)
        def _(c):
            @pl.when(c > 0)
            def _():
                wait(writes(c - 1, buf1, out1a, out1b))
            read(c + 1, buf1, in1).start()
            read(c, buf0, in0).wait()
            start(writes(c, buf0, out0a, out0b))
            read(c + 1, buf1, in1).wait()
            start(writes(c + 1, buf1, out1a, out1b))
            wait(writes(c, buf0, out0a, out0b))

            @pl.when(c + 2 < nch)
            def _():
                read(c + 2, buf0, in0).start()

        wait(writes(nch - 1, buf1, out1a, out1b))

    return scatter(src, idx)


def _sc_gather_rows(src, idx):
    M = idx.shape[0]
    width = src.shape[1]
    nc, nw = _sc_workers()
    per_w = M // nw
    ch = SC_CHUNK
    nch = per_w // ch
    assert per_w * nw == M and nch * ch == per_w and nch % 2 == 0
    mesh = plsc.VectorSubcoreMesh(core_axis_name="c", subcore_axis_name="s")
    dma = pltpu.SemaphoreType.DMA

    @functools.partial(
        pl.kernel, mesh=mesh, out_type=jax.ShapeDtypeStruct((M, width), src.dtype),
        scratch_types=[pltpu.VMEM((per_w,), jnp.int32),
                       pltpu.VMEM((ch, width), src.dtype), pltpu.VMEM((ch, width), src.dtype),
                       dma, dma, dma, dma],
        name="sc_gather")
    def gather(src_hbm, idx_hbm, out_hbm, idx_v, buf0, buf1, in0, in1, out0, out1):
        base = (lax.axis_index("s") * nc + lax.axis_index("c")) * per_w
        pltpu.sync_copy(idx_hbm.at[pl.ds(base, per_w)], idx_v)

        def read(c, buf, sem):
            return pltpu.make_async_copy(src_hbm.at[idx_v.at[pl.ds(c * ch, ch)]], buf, sem)

        def write(c, buf, sem):
            return pltpu.make_async_copy(buf, out_hbm.at[pl.ds(base + c * ch, ch)], sem)

        read(0, buf0, in0).start()

        @pl.loop(0, nch, step=2)
        def _(c):
            @pl.when(c > 0)
            def _():
                write(c - 1, buf1, out1).wait()
            read(c + 1, buf1, in1).start()
            read(c, buf0, in0).wait()
            write(c, buf0, out0).start()
            read(c + 1, buf1, in1).wait()
            write(c + 1, buf1, out1).start()
            write(c, buf0, out0).wait()

            @pl.when(c + 2 < nch)
            def _():
                read(c + 2, buf0, in0).start()

        write(nch - 1, buf1, out1).wait()

    return gather(src, idx)


def _final_kernel(x1_ref, y1_ref, y2_ref, meta_ref, nw_ref, o_ref):
    wcol = meta_ref[...].T
    w1 = wcol[:, 0:1]
    w2 = wcol[:, 1:2]
    x2 = x1_ref[...] + (w1 * _unpack_rows(y1_ref[...]) + w2 * _unpack_rows(y2_ref[...]))
    o_ref[...] = _rms(x2, nw_ref[...])


def _final(x1, yg, meta, norm_w):
    T = x1.shape[0]
    tm = min(COMB_TILE, T)
    nt = T // tm
    row = lambda i: (i, 0)
    return pl.pallas_call(
        _final_kernel,
        grid=(nt,),
        in_specs=[pl.BlockSpec((tm, D_MODEL), row),
                  pl.BlockSpec((tm, PACKED), row),
                  pl.BlockSpec((tm, PACKED), lambda i: (i + nt, 0)),
                  pl.BlockSpec((8, tm), lambda i: (0, i)),
                  pl.BlockSpec((1, D_MODEL), lambda i: (0, 0))],
        out_specs=pl.BlockSpec((tm, D_MODEL), row),
        out_shape=jax.ShapeDtypeStruct((T, D_MODEL), F32),
        compiler_params=pltpu.CompilerParams(dimension_semantics=("arbitrary",),
                                             vmem_limit_bytes=VMEM_LIMIT),
        name="final",
    )(x1, yg, yg, meta, norm_w)


def _layer(x, norm_mix_w, w_in, conv_w, conv_b, b_igate, b_fgate, attn_sinks, mlstm_norm_w,
           w_attn_o, w_mlstm_o, w_out, norm_ffn_w, w_group, b_group, w_router, b_router,
           w_gate, w_up, w_down, out_norm_w):
    B, S, D = x.shape
    T = B * S
    x2 = x.reshape(T, D)

    wt = w_in.T
    w_tok = jnp.concatenate([wt[_O_AK:_O_AV], wt[_O_MK:_O_MV]], axis=0).astype(BF16)
    w_chan = jnp.concatenate([wt[_O_AQ:_O_AK] * (HEAD_DIM ** -0.5), wt[_O_AV:_O_MQ], wt[_O_MQ:_O_MK],
                              wt[_O_MV:_O_MI],
                              jnp.pad(wt[_O_MI:_O_GA], ((0, GATE_ROWS - 2 * M_HEADS), (0, 0)))],
                             axis=0).astype(BF16)
    ak, mk, aqt, avt, mqt, mvt, mot, grow = _inproj(x2, norm_mix_w.reshape(1, D), w_tok, w_chan, B, S)

    ya = _attn(attn_sinks.astype(F32), aqt, ak.reshape(B, S, -1), avt)

    bias = jnp.concatenate([b_igate, b_fgate]).astype(F32)
    expert_ws = [w.reshape(-1, w.shape[-1]) for w in (w_gate, w_up, w_down)]
    ym, wg16, wu16, wd16 = _mlstm(mqt, mk.reshape(B, S, -1), mvt, mot, grow, conv_w, conv_b,
                                  bias.reshape(2 * M_HEADS, 1), mlstm_norm_w, expert_ws)

    gpad = jnp.zeros((8 - N_GROUPS, D), F32)
    w_rt = jnp.concatenate([w_group.T, gpad, w_router.T, jnp.zeros((LANES - ROUTER_ROWS, D), F32)], axis=0)
    w_rt_hi = w_rt.astype(BF16)
    w_rt = jnp.concatenate([w_rt_hi, (w_rt - w_rt_hi.astype(F32)).astype(BF16)], axis=0).T
    b_rt = jnp.concatenate([b_group, jnp.zeros((8 - N_GROUPS,), F32), b_router]).astype(F32)
    b_rt = jnp.broadcast_to(b_rt[:, None], (ROUTER_ROWS, min(TOK_TILE, T)))
    x1, h2p, meta, code_rows, cnt = _outproj(x2, ya.reshape(T, -1), ym.reshape(T, -1),
                                            norm_mix_w.reshape(1, D), wt[_O_GA:].astype(BF16),
                                            w_attn_o.astype(BF16), w_mlstm_o.astype(BF16),
                                            w_out.astype(BF16), norm_ffn_w.reshape(1, D), w_rt, b_rt)

    counts = cnt[:, 0].astype(jnp.int32)
    nblk = (2 * T) // SLOT_BLOCK + N_EXPERTS
    pstart, blk_e, nvalid, nused = _slots(counts, nblk)
    dest = _dests(pstart, code_rows)[0:2].reshape(2 * T)

    xs = _sc_scatter_rows(h2p, dest, nblk * SLOT_BLOCK)
    ys = _experts(blk_e, nvalid, nused, xs, wg16.reshape(w_gate.shape), wu16.reshape(w_up.shape),
                  wd16.reshape(w_down.shape))
    yg = _sc_gather_rows(ys, dest)
    out = _final(x1, yg, meta, out_norm_w.reshape(1, D))
    return out.reshape(B, S, D)


def kernel(x, norm_mix_w, w_in, conv_w, conv_b, b_igate, b_fgate, attn_sinks, mlstm_norm_w, w_attn_o,
           w_mlstm_o, w_out, norm_ffn_w, w_group, b_group, w_router, b_router, w_gate, w_up, w_down,
           norm_final_w):
    depth = w_in.shape[0]
    assert depth == 1, "final RMSNorm is fused into the last layer's combine kernel"
    return _layer(x, norm_mix_w[0], w_in[0], conv_w[0], conv_b[0], b_igate[0], b_fgate[0],
                  attn_sinks[0], mlstm_norm_w[0], w_attn_o[0], w_mlstm_o[0], w_out[0], norm_ffn_w[0],
                  w_group[0], b_group[0], w_router[0], b_router[0], w_gate[0], w_up[0], w_down[0],
                  norm_final_w)
```

```python
import functools

import jax
import jax.numpy as jnp
from jax import lax
from jax.experimental import pallas as pl
from jax.experimental.pallas import tpu as pltpu
from jax.experimental.pallas import tpu_sc as plsc

F32 = jnp.float32
BF16 = jnp.bfloat16

D_MODEL = 1024
N_Q_HEADS = 8
N_KV_HEADS = 2
HEAD_DIM = 64
WINDOW = 128
GQA_GROUP = N_Q_HEADS // N_KV_HEADS
M_HEADS = 4
M_QK_DIM = 64
M_V_DIM = 128
CONV_WIDTH = 4
N_GROUPS = 4
EXPERTS_PER_GROUP = 8
N_EXPERTS = N_GROUPS * EXPERTS_PER_GROUP
D_EXPERT = 512
EPS = 1e-6

ATT_Q_WIDTH = N_Q_HEADS * HEAD_DIM
ATT_KV_WIDTH = N_KV_HEADS * HEAD_DIM
M_QK_WIDTH = M_HEADS * M_QK_DIM
M_V_WIDTH = M_HEADS * M_V_DIM

LANES = 128
NEG = -1e30
V7X_VMEM_BYTES = 64 * 1024 * 1024
VMEM_LIMIT = V7X_VMEM_BYTES - 8 * 1024 * 1024

IN_TILE = 1024
TOK_TILE = 1024
ATT_TILE = 2048
LOOKAHEAD = 2
M_LOOKAHEAD = 4
M_CHUNK = 128
M_BATCH = 8
M_STATE_ROWS = M_V_DIM + 16
SLOT_BLOCK = 512
COMB_TILE = 1024
RANK_RADIX = 65536
RANK_BITS = 16
ROUTER_ROWS = 8 + N_EXPERTS
PACKED = D_MODEL // 2
SC_CHUNK = 64

_O_AQ = 0
_O_AK = _O_AQ + ATT_Q_WIDTH
_O_AV = _O_AK + ATT_KV_WIDTH
_O_MQ = _O_AV + ATT_KV_WIDTH
_O_MK = _O_MQ + M_QK_WIDTH
_O_MV = _O_MK + M_QK_WIDTH
_O_MO = _O_MV + M_V_WIDTH
_O_MI = _O_MO + M_V_WIDTH
_O_MF = _O_MI + M_HEADS
_O_GA = _O_MF + M_HEADS


def _rms(x, w):
    return x * lax.rsqrt(jnp.mean(x * x, axis=-1, keepdims=True) + EPS) * w


def _sigmoid(x):
    return 0.5 * jnp.tanh(0.5 * x) + 0.5


def _pack_rows(x):
    half = x.shape[1] // 2
    bits = lax.bitcast_convert_type(x.astype(BF16).astype(F32), jnp.uint32)
    packed = (bits[:, :half] >> 16) | (bits[:, half:] & jnp.uint32(0xFFFF0000))
    return lax.bitcast_convert_type(packed, jnp.int32)


def _unpack_rows(w):
    u = lax.bitcast_convert_type(w, jnp.uint32)
    lo = lax.bitcast_convert_type(u << 16, F32)
    hi = lax.bitcast_convert_type(u & jnp.uint32(0xFFFF0000), F32)
    return jnp.concatenate([lo, hi], axis=-1)


def _log_sigmoid(x):
    return jnp.minimum(x, 0.0) - jnp.log1p(jnp.exp(-jnp.abs(x)))


_TOK_SPLITS = (ATT_KV_WIDTH, M_QK_WIDTH)
_CHAN_SPLITS = (ATT_Q_WIDTH, ATT_KV_WIDTH, M_QK_WIDTH, M_V_WIDTH, M_V_WIDTH)
GATE_ROWS = 16
_NT = (((1,), (1,)), ((), ()))


def _inproj_kernel(x_ref, nw_ref, w_ref, wt_ref, *out_refs):
    tok_refs = out_refs[:len(_TOK_SPLITS)]
    chan_refs = out_refs[len(_TOK_SPLITS):len(_TOK_SPLITS) + len(_CHAN_SPLITS)]
    grow_ref = out_refs[-1]
    h = _rms(x_ref[...], nw_ref[...]).astype(BF16)
    lo = 0
    for ref, width in zip(tok_refs, _TOK_SPLITS):
        ref[...] = lax.dot_general(h, w_ref[lo:lo + width, :], _NT, preferred_element_type=F32).astype(BF16)
        lo += width
    lo = 0
    for ref, width in zip(chan_refs, _CHAN_SPLITS):
        ref[0] = lax.dot_general(wt_ref[lo:lo + width, :], h, (((1,), (1,)), ((), ())),
                                 preferred_element_type=F32).astype(BF16)
        lo += width
    grow_ref[...] = lax.dot_general(wt_ref[lo:lo + GATE_ROWS, :], h, (((1,), (1,)), ((), ())),
                                    preferred_element_type=F32)[0:2 * M_HEADS, :]


def _inproj(x2, norm_w, w_tok, w_chan, B, S):
    T = x2.shape[0]
    tm = min(IN_TILE, S)
    tpb = S // tm
    nsteps = T // tm
    row = lambda i: (i, 0)
    const = lambda i: (0, 0)
    chan = lambda i: (i // tpb, 0, i % tpb)
    return pl.pallas_call(
        _inproj_kernel,
        grid=(nsteps,),
        in_specs=[pl.BlockSpec((tm, D_MODEL), row),
                  pl.BlockSpec((1, D_MODEL), const),
                  pl.BlockSpec(w_tok.shape, const),
                  pl.BlockSpec(w_chan.shape, const)],
        out_specs=[*[pl.BlockSpec((tm, w), row) for w in _TOK_SPLITS],
                   *[pl.BlockSpec((1, w, tm), chan) for w in _CHAN_SPLITS],
                   pl.BlockSpec((8, tm), lambda i: (0, i))],
        out_shape=[*[jax.ShapeDtypeStruct((T, w), BF16) for w in _TOK_SPLITS],
                   *[jax.ShapeDtypeStruct((B, w, S), BF16) for w in _CHAN_SPLITS],
                   jax.ShapeDtypeStruct((8, T), F32)],
        compiler_params=pltpu.CompilerParams(dimension_semantics=("arbitrary",),
                                             vmem_limit_bytes=VMEM_LIMIT),
        name="inproj",
    )(x2, norm_w, w_tok, w_chan)


def _attn_kernel(tq, sink_ref, qt_ref, k_ref, kp_ref, vt_ref, vtp_ref, o_ref):
    i = pl.program_id(1)
    W = WINDOW
    G = GQA_GROUP
    u = lax.broadcasted_iota(jnp.int32, (W, W), 0)
    t = lax.broadcasted_iota(jnp.int32, (W, W), 1)
    from_prev = u > t

    def keys_values(j):
        if j == 0:
            return kp_ref[0], vtp_ref[0], k_ref[0, 0:W, :], vt_ref[0, :, 0:W]
        return (k_ref[0, (j - 1) * W:j * W, :], vt_ref[0, :, (j - 1) * W:j * W],
                k_ref[0, j * W:(j + 1) * W, :], vt_ref[0, :, j * W:(j + 1) * W])

    def scores(j, g):
        k_prev, _, k_cur, _ = keys_values(j)
        dims = slice(g * HEAD_DIM, (g + 1) * HEAD_DIM)
        qg = jnp.concatenate([qt_ref[0, h * HEAD_DIM:(h + 1) * HEAD_DIM, j * W:(j + 1) * W]
                              for h in range(g * G, (g + 1) * G)], axis=1)
        return (jnp.dot(k_prev[:, dims], qg, preferred_element_type=F32),
                jnp.dot(k_cur[:, dims], qg, preferred_element_type=F32))

    tasks = [(j, g) for j in range(tq // W) for g in range(N_KV_HEADS)]
    queue = [scores(*t) for t in tasks[:LOOKAHEAD]]
    pair_rows = []
    for n_task, (j, g) in enumerate(tasks):
        s_prev, s_cur = queue.pop(0)
        if n_task + LOOKAHEAD < len(tasks):
            queue.append(scores(*tasks[n_task + LOOKAHEAD]))
        cols = slice(j * W, (j + 1) * W)
        _, vt_prev, _, vt_cur = keys_values(j)
        dims = slice(g * HEAD_DIM, (g + 1) * HEAD_DIM)
        for n in range(G):
            hc = slice(n * W, (n + 1) * W)
            sp = s_prev[:, hc]
            if j == 0:
                sp = jnp.where(i > 0, sp, NEG)
            s = jnp.where(from_prev, sp, s_cur[:, hc])
            sink = sink_ref[g * G + n]
            m = jnp.maximum(jnp.max(s, axis=0, keepdims=True), sink)
            p = jnp.exp(s - m)
            inv = 1.0 / (jnp.sum(p, axis=0, keepdims=True) + jnp.exp(sink - m))
            p_prev = jnp.where(from_prev, p, 0.0).astype(BF16)
            p_cur = jnp.where(from_prev, 0.0, p).astype(BF16)
            o = jnp.dot(vt_prev[dims, :], p_prev, preferred_element_type=F32) \
                + jnp.dot(vt_cur[dims, :], p_cur, preferred_element_type=F32)
            pair_rows.append(o * inv)
        if g == N_KV_HEADS - 1:
            out = [jnp.concatenate(pair_rows[n:n + 2], axis=0).T for n in range(0, N_Q_HEADS, 2)]
            o_ref[0, cols, :] = jnp.concatenate(out, axis=1).astype(BF16)
            pair_rows = []


def _attn(sinks, qt, k3, vt):
    B, S, _ = k3.shape
    tq = min(ATT_TILE, S)
    per = tq // WINDOW
    chan = lambda b, i: (b, 0, i)
    tok = lambda b, i: (b, i, 0)
    return pl.pallas_call(
        functools.partial(_attn_kernel, tq),
        grid=(B, S // tq),
        in_specs=[pl.BlockSpec(memory_space=pltpu.SMEM),
                  pl.BlockSpec((1, ATT_Q_WIDTH, tq), chan),
                  pl.BlockSpec((1, tq, ATT_KV_WIDTH), tok),
                  pl.BlockSpec((1, WINDOW, ATT_KV_WIDTH), lambda b, i: (b, jnp.maximum(i * per - 1, 0), 0)),
                  pl.BlockSpec((1, ATT_KV_WIDTH, tq), chan),
                  pl.BlockSpec((1, ATT_KV_WIDTH, WINDOW), lambda b, i: (b, 0, jnp.maximum(i * per - 1, 0)))],
        out_specs=pl.BlockSpec((1, tq, ATT_Q_WIDTH), tok),
        out_shape=jax.ShapeDtypeStruct((B, S, ATT_Q_WIDTH), BF16),
        compiler_params=pltpu.CompilerParams(dimension_semantics=("arbitrary", "arbitrary"),
                                             vmem_limit_bytes=VMEM_LIMIT),
        name="attn",
    )(sinks, qt, k3, k3, vt, vt)


def _mlstm_kernel(nb, *refs):
    L = M_CHUNK
    H = M_HEADS
    mqt_ref, mk_ref, mvt_ref, mot_ref = refs[:4]
    grow_refs = refs[4:4 + nb]
    (cwq_ref, cbq_ref, cwk_ref, cbk_ref, brow_ref, nw_ref, y_ref,
     state_ref, m_ref, prevq_ref, ubuf_ref) = refs[4 + nb:]
    cidx = pl.program_id(1)

    @pl.when(cidx == 0)
    def _():
        state_ref[...] = jnp.zeros_like(state_ref)
        m_ref[...] = jnp.zeros_like(m_ref)
        prevq_ref[...] = jnp.zeros_like(prevq_ref)
        ubuf_ref[:, 0:8, :] = jnp.zeros((nb, 8, M_QK_WIDTH), F32)

    ri = lax.broadcasted_iota(jnp.int32, (L, L), 0)
    ci = lax.broadcasted_iota(jnp.int32, (L, L), 1)
    causal_t = ri <= ci
    triu = jnp.where(causal_t, 1.0, 0.0).astype(F32)
    lane = lax.broadcasted_iota(jnp.int32, (8, L), 1)
    r2 = lax.broadcasted_iota(jnp.int32, (2 * L, L), 0)
    c2 = lax.broadcasted_iota(jnp.int32, (2 * L, L), 1)
    shifts = [jnp.where(r2 - c2 == L - k, 1.0, 0.0).astype(BF16) for k in range(1, CONV_WIDTH)]
    ones_rows = jnp.where(lax.broadcasted_iota(jnp.int32, (M_STATE_ROWS - M_V_DIM, L), 0) == 0,
                          1.0, 0.0).astype(BF16)

    pairs = [(bb, h) for bb in range(nb) for h in range(H)]
    states = [state_ref[bb * H + h] for bb, h in pairs]
    m_prevs = [m_ref[bb] for bb in range(nb)]

    def prepare(bb):
        cur = mqt_ref[bb]
        both = jnp.concatenate([prevq_ref[bb], cur], axis=1)
        acc = cbq_ref[...] + cwq_ref[CONV_WIDTH - 1] * cur.astype(F32)
        for k in range(1, CONV_WIDTH):
            acc = acc + cwq_ref[CONV_WIDTH - 1 - k] * jnp.dot(both, shifts[k - 1],
                                                              preferred_element_type=F32)
        qt = (acc * _sigmoid(acc)).astype(BF16)
        ubuf_ref[bb, 8:L + 8, :] = mk_ref[bb].astype(F32)
        acc = cbk_ref[...] + cwk_ref[CONV_WIDTH - 1:CONV_WIDTH, :] * ubuf_ref[bb, 8:L + 8, :]
        for j in range(CONV_WIDTH - 1):
            off = 8 - (CONV_WIDTH - 1) + j
            acc = acc + cwk_ref[j:j + 1, :] * ubuf_ref[bb, off:off + L, :]
        ubuf_ref[bb, 0:8, :] = ubuf_ref[bb, L:L + 8, :]
        kk = (acc * _sigmoid(acc) * (M_QK_DIM ** -0.5)).astype(BF16)

        gr = grow_refs[bb][...] + brow_ref[...]
        b = jnp.dot(_log_sigmoid(gr), triu, preferred_element_type=F32,
                    precision=lax.Precision.HIGHEST)
        gi = pltpu.roll(gr, 4, axis=0)
        u = gi - b
        cm = u
        for sh in (1, 2, 4, 8, 16, 32, 64):
            if sh < L:
                cm = jnp.maximum(cm, jnp.where(lane >= sh, pltpu.roll(cm, sh, axis=1), NEG))
        m_prev = m_prevs[bb]
        mt = b + jnp.maximum(m_prev, cm)
        wa = jnp.exp(b + m_prev - mt)
        emt = jnp.exp(-mt)
        m_new = jnp.broadcast_to(mt[:, L - 1:L], (8, L))
        b_last = jnp.broadcast_to(b[:, L - 1:L], (8, L))
        wc = jnp.exp(b_last + m_prev - m_new)
        ws = jnp.exp(b_last + u - m_new)
        return qt, kk, b - mt, u.T, wa, emt, wc, ws, m_new

    seqs = [prepare(bb) for bb in range(nb)]

    def products(bb, h, state):
        qt, kk = seqs[bb][0], seqs[bb][1]
        qh = qt[h * M_QK_DIM:(h + 1) * M_QK_DIM, :]
        kh = kk[:, h * M_QK_DIM:(h + 1) * M_QK_DIM]
        return (kh, jnp.dot(kh, qh, preferred_element_type=F32),
                jnp.dot(state.astype(BF16), qh, preferred_element_type=F32))

    new_states, outs = [], []
    queue = [products(*pairs[n], states[n]) for n in range(min(M_LOOKAHEAD, len(pairs)))]
    for n, ((bb, h), state) in enumerate(zip(pairs, states)):
        kh, st, inter = queue.pop(0)
        if n + M_LOOKAHEAD < len(pairs):
            queue.append(products(*pairs[n + M_LOOKAHEAD], states[n + M_LOOKAHEAD]))
        _, _, bmt, ucol, wa, emt, wc, ws, _ = seqs[bb]
        row = slice(H + h, H + h + 1)
        wq = jnp.exp(jnp.where(causal_t, bmt[row, :] + ucol[:, H + h:H + h + 1], NEG))
        sw = (st * wq).astype(BF16)
        vext = jnp.concatenate([mvt_ref[bb, h * M_V_DIM:(h + 1) * M_V_DIM, :], ones_rows], axis=0)
        res = wa[row, :] * inter + jnp.dot(vext, sw, preferred_element_type=F32)
        num = res[:M_V_DIM, :]
        den = res[M_V_DIM:M_V_DIM + 1, :]
        hb = num / jnp.maximum(jnp.abs(den), emt[row, :])
        vw = (vext.astype(F32) * ws[row, :]).astype(BF16)
        new_states.append(wc[row, 0:1] * state + jnp.dot(vw, kh, preferred_element_type=F32))
        hn = hb * lax.rsqrt(jnp.mean(hb * hb, axis=0, keepdims=True) + EPS)
        hn = hn * nw_ref[h * M_V_DIM:(h + 1) * M_V_DIM, :]
        og = _sigmoid(mot_ref[bb, h * M_V_DIM:(h + 1) * M_V_DIM, :].astype(F32))
        outs.append((og * hn).T.astype(BF16))

    for i, (bb, h) in enumerate(pairs):
        state_ref[bb * H + h] = new_states[i]
    for bb in range(nb):
        m_ref[bb] = seqs[bb][8]
        prevq_ref[bb] = mqt_ref[bb]
        y_ref[bb] = jnp.concatenate(outs[bb * H:(bb + 1) * H], axis=1)


def _mlstm(mqt, mk3, mvt, mot, grow, conv_w, conv_b, bias_row, norm_w):
    B, S, _ = mk3.shape
    L = M_CHUNK
    nb = max(d for d in range(1, M_BATCH + 1) if B % d == 0)
    nc = S // L
    assert L == LANES, "per-head scalars are kept lane-replicated next to (8, L) gate rows"
    tok = lambda b, c: (b, c, 0)
    chan = lambda b, c: (b, 0, c)
    const2 = lambda b, c: (0, 0)
    const3 = lambda b, c: (0, 0, 0)
    grow_specs = [pl.BlockSpec((8, L), functools.partial(lambda b, c, n: (0, (b * nb + n) * nc + c), n=n))
                  for n in range(nb)]
    rep = lambda v: jnp.broadcast_to(v.astype(F32)[..., None], v.shape + (L,))
    cwq, cbq = rep(conv_w[:, :M_QK_WIDTH]), rep(conv_b[:M_QK_WIDTH])
    cwk, cbk = conv_w[:, M_QK_WIDTH:].astype(F32), conv_b[M_QK_WIDTH:].reshape(1, -1).astype(F32)
    return pl.pallas_call(
        functools.partial(_mlstm_kernel, nb),
        grid=(B // nb, nc),
        in_specs=[pl.BlockSpec((nb, M_QK_WIDTH, L), chan),
                  pl.BlockSpec((nb, L, M_QK_WIDTH), tok),
                  pl.BlockSpec((nb, M_V_WIDTH, L), chan),
                  pl.BlockSpec((nb, M_V_WIDTH, L), chan),
                  *grow_specs,
                  pl.BlockSpec((CONV_WIDTH, M_QK_WIDTH, L), const3),
                  pl.BlockSpec((M_QK_WIDTH, L), const2),
                  pl.BlockSpec((CONV_WIDTH, M_QK_WIDTH), const2),
                  pl.BlockSpec((1, M_QK_WIDTH), const2),
                  pl.BlockSpec((8, 1), const2),
                  pl.BlockSpec((M_V_WIDTH, L), const2)],
        out_specs=pl.BlockSpec((nb, L, M_V_WIDTH), tok),
        out_shape=jax.ShapeDtypeStruct((B, S, M_V_WIDTH), BF16),
        scratch_shapes=[pltpu.VMEM((nb * M_HEADS, M_STATE_ROWS, M_QK_DIM), F32),
                        pltpu.VMEM((nb, 8, LANES), F32),
                        pltpu.VMEM((nb, M_QK_WIDTH, L), BF16),
                        pltpu.VMEM((nb, L + 8, M_QK_WIDTH), F32)],
        compiler_params=pltpu.CompilerParams(dimension_semantics=("arbitrary", "arbitrary"),
                                             vmem_limit_bytes=VMEM_LIMIT),
        name="mlstm",
    )(mqt, mk3, mvt, mot, *([grow] * nb), cwq, cbq, cwk, cbk, bias_row, rep(norm_w))


def _outproj_kernel(x_ref, ya_ref, ym_ref, nmix_ref, wgab_ref, wa_ref, wm_ref, wo_ref, nw_ref, wr_ref, br_ref,
                    eg_ref, eu_ref, ed_ref,
                    x1_ref, h2_ref, meta_ref, code_ref, cnt_ref, eg16_ref, eu16_ref, ed16_ref, cnt_scr):
    tm = x_ref.shape[0]
    i = pl.program_id(0)
    for src, dst in ((eg_ref, eg16_ref), (eu_ref, eu16_ref), (ed_ref, ed16_ref)):
        dst[...] = src[...].astype(BF16)

    @pl.when(i == 0)
    def _():
        cnt_scr[...] = jnp.zeros_like(cnt_scr)

    pa = jnp.dot(ya_ref[...], wa_ref[...], preferred_element_type=F32)
    pm = jnp.dot(ym_ref[...], wm_ref[...], preferred_element_type=F32)
    x = x_ref[...]
    h1 = _rms(x, nmix_ref[...]).astype(BF16)
    ga = lax.dot_general(h1, wgab_ref[:D_MODEL, :], _NT, preferred_element_type=F32)
    gb = lax.dot_general(h1, wgab_ref[D_MODEL:, :], _NT, preferred_element_type=F32)
    mix = _sigmoid(ga) * pa + _sigmoid(gb) * pm
    x1 = x + jnp.dot(mix.astype(BF16), wo_ref[...], preferred_element_type=F32)
    x1_ref[...] = x1
    h2 = _rms(x1, nw_ref[...])
    h2_ref[...] = _pack_rows(h2)

    nt = (((1,), (1,)), ((), ()))
    h_hi = h2.astype(BF16)
    h_lo = (h2 - h_hi.astype(F32)).astype(BF16)
    hw = jnp.dot(h_hi, wr_ref[...], preferred_element_type=F32)
    logits = (hw[:, :LANES] + hw[:, LANES:]
              + jnp.dot(h_lo, wr_ref[:, :LANES], preferred_element_type=F32)).T[:ROUTER_ROWS]
    logits = logits + br_ref[...]
    row8 = lax.broadcasted_iota(jnp.int32, (8, tm), 0)

    def first_argmax(v):
        mx = jnp.max(v, axis=0, keepdims=True)
        idx = jnp.min(jnp.where(v == mx, row8, 8), axis=0, keepdims=True)
        return mx, idx

    gl = jnp.where(row8 < N_GROUPS, logits[0:8], NEG)
    gmax, gi = first_argmax(gl)
    gp = 1.0 / jnp.sum(jnp.exp(gl - gmax), axis=0, keepdims=True)
    el = jnp.zeros((8, tm), F32)
    for g in range(N_GROUPS):
        el = jnp.where(gi == g, logits[8 + 8 * g:16 + 8 * g], el)
    v1, j1 = first_argmax(el)
    v2, j2 = first_argmax(jnp.where(row8 == j1, NEG, el))
    t = jnp.exp(v2 - v1)
    w1 = gp / (1.0 + t)
    w2 = gp * t / (1.0 + t)
    e1 = gi * EXPERTS_PER_GROUP + j1
    e2 = gi * EXPERTS_PER_GROUP + j2

    erow = lax.broadcasted_iota(jnp.int32, (N_EXPERTS, tm), 0)
    hit1 = erow == e1
    hit2 = erow == e2
    onehot = jnp.where(hit1 | hit2, 1.0, 0.0)
    ri = lax.broadcasted_iota(jnp.int32, (LANES, LANES), 0)
    ci = lax.broadcasted_iota(jnp.int32, (LANES, LANES), 1)
    before = jnp.where(ri < ci, 1.0, 0.0).astype(BF16)
    run = cnt_scr[...]
    parts = []
    for c in range(tm // LANES):
        oh = onehot[:, c * LANES:(c + 1) * LANES]
        parts.append(jnp.dot(oh.astype(BF16), before, preferred_element_type=F32) + run)
        run = run + jnp.broadcast_to(jnp.sum(oh, axis=1, keepdims=True), (N_EXPERTS, LANES))
    rank = jnp.concatenate(parts, axis=1)
    r1 = jnp.sum(jnp.where(hit1, rank, 0.0), axis=0, keepdims=True)
    r2 = jnp.sum(jnp.where(hit2, rank, 0.0), axis=0, keepdims=True)
    cnt_scr[...] = run
    cnt_ref[...] = run

    c1 = e1 * RANK_RADIX + r1.astype(jnp.int32)
    c2 = e2 * RANK_RADIX + r2.astype(jnp.int32)
    code_ref[...] = jnp.where(row8 == 0, c1, jnp.where(row8 == 1, c2, 0))
    meta_ref[...] = jnp.where(row8 == 0, w1, jnp.where(row8 == 1, w2, 0.0))


def _outproj(x2, ya, ym, norm_mix, w_gab, wa, wm, wo, norm_w, w_rt, b_rt, expert_ws):
    T = x2.shape[0]
    tm = b_rt.shape[1]
    row = lambda i: (i, 0)
    nsteps = T // tm
    assert all(w.shape[0] % (8 * nsteps) == 0 for w in expert_ws)
    cast_specs = [pl.BlockSpec((w.shape[0] // nsteps, w.shape[1]), row) for w in expert_ws]
    const = lambda i: (0, 0)
    resident = lambda shape: pl.BlockSpec(shape, const, pipeline_mode=pl.Buffered(1))
    return pl.pallas_call(
        _outproj_kernel,
        grid=(T // tm,),
        in_specs=[pl.BlockSpec((tm, D_MODEL), row),
                  pl.BlockSpec((tm, ATT_Q_WIDTH), row),
                  pl.BlockSpec((tm, M_V_WIDTH), row),
                  resident((1, D_MODEL)),
                  resident((2 * D_MODEL, D_MODEL)),
                  resident((ATT_Q_WIDTH, D_MODEL)),
                  resident((M_V_WIDTH, D_MODEL)),
                  resident((D_MODEL, D_MODEL)),
                  resident((1, D_MODEL)),
                  resident((D_MODEL, 2 * LANES)),
                  resident((ROUTER_ROWS, tm)),
                  *cast_specs],
        out_specs=[pl.BlockSpec((tm, D_MODEL), row),
                   pl.BlockSpec((tm, PACKED), row),
                   pl.BlockSpec((8, tm), lambda i: (0, i)),
                   pl.BlockSpec((8, tm), lambda i: (0, i)),
                   pl.BlockSpec((N_EXPERTS, LANES), const),
                   *cast_specs],
        out_shape=[jax.ShapeDtypeStruct((T, D_MODEL), F32),
                   jax.ShapeDtypeStruct((T, PACKED), jnp.int32),
                   jax.ShapeDtypeStruct((8, T), F32),
                   jax.ShapeDtypeStruct((8, T), jnp.int32),
                   jax.ShapeDtypeStruct((N_EXPERTS, LANES), F32),
                   *[jax.ShapeDtypeStruct(w.shape, BF16) for w in expert_ws]],
        scratch_shapes=[pltpu.VMEM((N_EXPERTS, LANES), F32)],
        compiler_params=pltpu.CompilerParams(dimension_semantics=("arbitrary",),
                                             vmem_limit_bytes=V7X_VMEM_BYTES - 2 * 1024 * 1024),
        name="outproj",
    )(x2, ya, ym, norm_mix, w_gab, wa, wm, wo, norm_w, w_rt, b_rt, *expert_ws)


def _slots_kernel(cnt_ref, pstart_ref, blk_e_ref, nvalid_ref, nused_ref):
    R = SLOT_BLOCK
    nblk = blk_e_ref.shape[0]

    def per_expert(e, run):
        c = cnt_ref[e]
        end = run + ((c + R - 1) // R) * R
        pstart_ref[e] = run

        def set_blk(b, carry):
            blk_e_ref[b] = e
            nvalid_ref[b] = jnp.minimum(run + c - b * R, R)
            return carry
        lax.fori_loop(run // R, end // R, set_blk, 0)
        return end

    total = lax.fori_loop(0, N_EXPERTS, per_expert, 0)
    nused_ref[0] = total // R

    def tail_blk(b, carry):
        blk_e_ref[b] = N_EXPERTS - 1
        nvalid_ref[b] = 0
        return carry
    lax.fori_loop(total // R, nblk, tail_blk, 0)


def _slots(counts, nblk):
    smem = pl.BlockSpec(memory_space=pltpu.SMEM)
    return pl.pallas_call(
        _slots_kernel,
        in_specs=[smem],
        out_specs=[smem, smem, smem, smem],
        out_shape=[jax.ShapeDtypeStruct((N_EXPERTS,), jnp.int32),
                   jax.ShapeDtypeStruct((nblk,), jnp.int32),
                   jax.ShapeDtypeStruct((nblk,), jnp.int32),
                   jax.ShapeDtypeStruct((1,), jnp.int32)],
        name="slots",
    )(counts)


def _experts_kernel(blk_e_ref, nvalid_ref, nused_ref, xs_ref, wg_ref, wu_ref, wd_ref, ys_ref):
    j = pl.program_id(0)
    R = SLOT_BLOCK
    valid = j < nused_ref[0]

    @pl.when(valid)
    def _():
        rows = lax.broadcasted_iota(jnp.int32, (R, PACKED), 0)
        xw = jnp.where(rows < nvalid_ref[j], xs_ref[...], 0)
        xb = _unpack_rows(xw).astype(BF16)
        g = jnp.dot(xb, wg_ref[0], preferred_element_type=F32)
        u = jnp.dot(xb, wu_ref[0], preferred_element_type=F32)
        act = (g * _sigmoid(g) * u).astype(BF16)
        ys_ref[...] = _pack_rows(jnp.dot(act, wd_ref[0], preferred_element_type=F32))


def _experts(blk_e, nvalid, nused, xs, w_gate, w_up, w_down):
    nblk = blk_e.shape[0]
    R = SLOT_BLOCK
    wspec = lambda shape: pl.BlockSpec((1,) + shape, lambda j, be, nv, nu: (be[j], 0, 0))
    grid_spec = pltpu.PrefetchScalarGridSpec(
        num_scalar_prefetch=3,
        grid=(nblk,),
        in_specs=[pl.BlockSpec((R, PACKED), lambda j, be, nv, nu: (jnp.minimum(j, nu[0] - 1), 0)),
                  wspec((D_MODEL, D_EXPERT)),
                  wspec((D_MODEL, D_EXPERT)),
                  wspec((D_EXPERT, D_MODEL))],
        out_specs=pl.BlockSpec((R, PACKED), lambda j, be, nv, nu: (jnp.minimum(j, nu[0] - 1), 0)),
    )
    return pl.pallas_call(
        _experts_kernel,
        grid_spec=grid_spec,
        out_shape=jax.ShapeDtypeStruct((nblk * R, PACKED), jnp.int32),
        compiler_params=pltpu.CompilerParams(dimension_semantics=("arbitrary",),
                                             vmem_limit_bytes=VMEM_LIMIT),
        name="experts",
    )(blk_e, nvalid, nused, xs, w_gate, w_up, w_down)


def _dest_kernel(pstart_ref, code_ref, dest_ref):
    c = code_ref[...]
    e = c >> RANK_BITS
    base = jnp.zeros_like(c)
    for k in range(N_EXPERTS):
        base = jnp.where(e == k, pstart_ref[k], base)
    dest_ref[...] = base + (c & (RANK_RADIX - 1))


def _dests(pstart, code_rows):
    T = code_rows.shape[1]
    tc = min(4096, T)
    return pl.pallas_call(
        _dest_kernel,
        grid=(T // tc,),
        in_specs=[pl.BlockSpec(memory_space=pltpu.SMEM),
                  pl.BlockSpec((8, tc), lambda i: (0, i))],
        out_specs=pl.BlockSpec((8, tc), lambda i: (0, i)),
        out_shape=jax.ShapeDtypeStruct((8, T), jnp.int32),
        name="dests",
    )(pstart, code_rows)


def _sc_workers():
    info = plsc.get_sparse_core_info()
    return info.num_cores, info.num_cores * info.num_subcores


def _sc_scatter_rows(src, idx, n_out):
    n_src, width = src.shape
    nc, nw = _sc_workers()
    per_w = n_src // nw
    ch = SC_CHUNK
    nch = per_w // ch
    assert idx.shape[0] == 2 * n_src and per_w * nw == n_src and nch * ch == per_w and nch % 2 == 0
    mesh = plsc.VectorSubcoreMesh(core_axis_name="c", subcore_axis_name="s")
    dma = pltpu.SemaphoreType.DMA

    @functools.partial(
        pl.kernel, mesh=mesh, out_type=jax.ShapeDtypeStruct((n_out, width), src.dtype),
        scratch_types=[pltpu.VMEM((per_w,), jnp.int32), pltpu.VMEM((per_w,), jnp.int32),
                       pltpu.VMEM((ch, width), src.dtype), pltpu.VMEM((ch, width), src.dtype),
                       dma, dma, dma, dma, dma, dma],
        name="sc_scatter")
    def scatter(src_hbm, idx_hbm, out_hbm, idx_a, idx_b, buf0, buf1, in0, in1, out0a, out0b, out1a, out1b):
        base = (lax.axis_index("s") * nc + lax.axis_index("c")) * per_w
        pltpu.sync_copy(idx_hbm.at[pl.ds(base, per_w)], idx_a)
        pltpu.sync_copy(idx_hbm.at[pl.ds(n_src + base, per_w)], idx_b)

        def read(c, buf, sem):
            return pltpu.make_async_copy(src_hbm.at[pl.ds(base + c * ch, ch)], buf, sem)

        def writes(c, buf, sem_a, sem_b):
            rows = pl.ds(c * ch, ch)
            return (pltpu.make_async_copy(buf, out_hbm.at[idx_a.at[rows]], sem_a),
                    pltpu.make_async_copy(buf, out_hbm.at[idx_b.at[rows]], sem_b))

        def start(copies):
            for cp in copies:
                cp.start()

        def wait(copies):
            for cp in copies:
                cp.wait()

        read(0, buf0, in0).start()

        @pl.loop(0, nch, step=2)
        def _(c):
            @pl.when(c > 0)
            def _():
                wait(writes(c - 1, buf1, out1a, out1b))
            read(c + 1, buf1, in1).start()
            read(c, buf0, in0).wait()
            start(writes(c, buf0, out0a, out0b))
            read(c + 1, buf1, in1).wait()
            start(writes(c + 1, buf1, out1a, out1b))
            wait(writes(c, buf0, out0a, out0b))

            @pl.when(c + 2 < nch)
            def _():
                read(c + 2, buf0, in0).start()

        wait(writes(nch - 1, buf1, out1a, out1b))

    return scatter(src, idx)


def _sc_gather_rows(src, idx):
    M = idx.shape[0]
    width = src.shape[1]
    nc, nw = _sc_workers()
    per_w = M // nw
    ch = SC_CHUNK
    nch = per_w // ch
    assert per_w * nw == M and nch * ch == per_w and nch % 2 == 0
    mesh = plsc.VectorSubcoreMesh(core_axis_name="c", subcore_axis_name="s")
    dma = pltpu.SemaphoreType.DMA

    @functools.partial(
        pl.kernel, mesh=mesh, out_type=jax.ShapeDtypeStruct((M, width), src.dtype),
        scratch_types=[pltpu.VMEM((per_w,), jnp.int32),
                       pltpu.VMEM((ch, width), src.dtype), pltpu.VMEM((ch, width), src.dtype),
                       dma, dma, dma, dma],
        name="sc_gather")
    def gather(src_hbm, idx_hbm, out_hbm, idx_v, buf0, buf1, in0, in1, out0, out1):
        base = (lax.axis_index("s") * nc + lax.axis_index("c")) * per_w
        pltpu.sync_copy(idx_hbm.at[pl.ds(base, per_w)], idx_v)

        def read(c, buf, sem):
            return pltpu.make_async_copy(src_hbm.at[idx_v.at[pl.ds(c * ch, ch)]], buf, sem)

        def write(c, buf, sem):
            return pltpu.make_async_copy(buf, out_hbm.at[pl.ds(base + c * ch, ch)], sem)

        read(0, buf0, in0).start()

        @pl.loop(0, nch, step=2)
        def _(c):
            @pl.when(c > 0)
            def _():
                write(c - 1, buf1, out1).wait()
            read(c + 1, buf1, in1).start()
            read(c, buf0, in0).wait()
            write(c, buf0, out0).start()
            read(c + 1, buf1, in1).wait()
            write(c + 1, buf1, out1).start()
            write(c, buf0, out0).wait()

            @pl.when(c + 2 < nch)
            def _():
                read(c + 2, buf0, in0).start()

        write(nch - 1, buf1, out1).wait()

    return gather(src, idx)


def _final_kernel(x1_ref, y1_ref, y2_ref, meta_ref, nw_ref, o_ref):
    wcol = meta_ref[...].T
    w1 = wcol[:, 0:1]
    w2 = wcol[:, 1:2]
    x2 = x1_ref[...] + (w1 * _unpack_rows(y1_ref[...]) + w2 * _unpack_rows(y2_ref[...]))
    o_ref[...] = _rms(x2, nw_ref[...])


def _final(x1, yg, meta, norm_w):
    T = x1.shape[0]
    tm = min(COMB_TILE, T)
    nt = T // tm
    row = lambda i: (i, 0)
    return pl.pallas_call(
        _final_kernel,
        grid=(nt,),
        in_specs=[pl.BlockSpec((tm, D_MODEL), row),
                  pl.BlockSpec((tm, PACKED), row),
                  pl.BlockSpec((tm, PACKED), lambda i: (i + nt, 0)),
                  pl.BlockSpec((8, tm), lambda i: (0, i)),
                  pl.BlockSpec((1, D_MODEL), lambda i: (0, 0))],
        out_specs=pl.BlockSpec((tm, D_MODEL), row),
        out_shape=jax.ShapeDtypeStruct((T, D_MODEL), F32),
        compiler_params=pltpu.CompilerParams(dimension_semantics=("arbitrary",),
                                             vmem_limit_bytes=VMEM_LIMIT),
        name="final",
    )(x1, yg, yg, meta, norm_w)


def _layer(x, norm_mix_w, w_in, conv_w, conv_b, b_igate, b_fgate, attn_sinks, mlstm_norm_w,
           w_attn_o, w_mlstm_o, w_out, norm_ffn_w, w_group, b_group, w_router, b_router,
           w_gate, w_up, w_down, out_norm_w):
    B, S, D = x.shape
    T = B * S
    x2 = x.reshape(T, D)

    wt = w_in.T
    w_tok = jnp.concatenate([wt[_O_AK:_O_AV], wt[_O_MK:_O_MV]], axis=0).astype(BF16)
    w_chan = jnp.concatenate([wt[_O_AQ:_O_AK] * (HEAD_DIM ** -0.5), wt[_O_AV:_O_MQ], wt[_O_MQ:_O_MK],
                              wt[_O_MV:_O_MI],
                              jnp.pad(wt[_O_MI:_O_GA], ((0, GATE_ROWS - 2 * M_HEADS), (0, 0)))],
                             axis=0).astype(BF16)
    ak, mk, aqt, avt, mqt, mvt, mot, grow = _inproj(x2, norm_mix_w.reshape(1, D), w_tok, w_chan, B, S)

    ya = _attn(attn_sinks.astype(F32), aqt, ak.reshape(B, S, -1), avt)

    bias = jnp.concatenate([b_igate, b_fgate]).astype(F32)
    ym = _mlstm(mqt, mk.reshape(B, S, -1), mvt, mot, grow, conv_w, conv_b,
                bias.reshape(2 * M_HEADS, 1), mlstm_norm_w)

    gpad = jnp.zeros((8 - N_GROUPS, D), F32)
    w_rt = jnp.concatenate([w_group.T, gpad, w_router.T, jnp.zeros((LANES - ROUTER_ROWS, D), F32)], axis=0)
    w_rt_hi = w_rt.astype(BF16)
    w_rt = jnp.concatenate([w_rt_hi, (w_rt - w_rt_hi.astype(F32)).astype(BF16)], axis=0).T
    b_rt = jnp.concatenate([b_group, jnp.zeros((8 - N_GROUPS,), F32), b_router]).astype(F32)
    b_rt = jnp.broadcast_to(b_rt[:, None], (ROUTER_ROWS, min(TOK_TILE, T)))
    expert_ws = [w.reshape(-1, w.shape[-1]) for w in (w_gate, w_up, w_down)]
    x1, h2p, meta, code_rows, cnt, wg16, wu16, wd16 = _outproj(
        x2, ya.reshape(T, -1), ym.reshape(T, -1), norm_mix_w.reshape(1, D), wt[_O_GA:].astype(BF16),
        w_attn_o.astype(BF16), w_mlstm_o.astype(BF16), w_out.astype(BF16), norm_ffn_w.reshape(1, D),
        w_rt, b_rt, expert_ws)

    counts = cnt[:, 0].astype(jnp.int32)
    nblk = (2 * T) // SLOT_BLOCK + N_EXPERTS
    pstart, blk_e, nvalid, nused = _slots(counts, nblk)
    dest = _dests(pstart, code_rows)[0:2].reshape(2 * T)

    xs = _sc_scatter_rows(h2p, dest, nblk * SLOT_BLOCK)
    ys = _experts(blk_e, nvalid, nused, xs, wg16.reshape(w_gate.shape), wu16.reshape(w_up.shape),
                  wd16.reshape(w_down.shape))
    yg = _sc_gather_rows(ys, dest)
    out = _final(x1, yg, meta, out_norm_w.reshape(1, D))
    return out.reshape(B, S, D)


def kernel(x, norm_mix_w, w_in, conv_w, conv_b, b_igate, b_fgate, attn_sinks, mlstm_norm_w, w_attn_o,
           w_mlstm_o, w_out, norm_ffn_w, w_group, b_group, w_router, b_router, w_gate, w_up, w_down,
           norm_final_w):
    depth = w_in.shape[0]
    assert depth == 1, "final RMSNorm is fused into the last layer's combine kernel"
    return _layer(x, norm_mix_w[0], w_in[0], conv_w[0], conv_b[0], b_igate[0], b_fgate[0],
                  attn_sinks[0], mlstm_norm_w[0], w_attn_o[0], w_mlstm_o[0], w_out[0], norm_ffn_w[0],
                  w_group[0], b_group[0], w_router[0], b_router[0], w_gate[0], w_up[0], w_down[0],
                  norm_final_w)
```
